```python
import math
import jax, jax.numpy as jnp
from jax import lax
import numpy as np

D_MODEL = 1024
BATCH = 4
SEQ = 4096
DEPTH = 2

MEM_LEN = 256
N_BRANCH = 4
BRANCH_WIDTH = D_MODEL // 4
ROPE_THETA = 10000.0
NORM_EPS = 1e-6
QUERY_BLOCK = 128
DIFF_HEADS = 4
DIFF_HEAD_DIM = BRANCH_WIDTH // (2 * DIFF_HEADS)
HGRN_HEADS = 4
HGRN_KEY_DIM = BRANCH_WIDTH // HGRN_HEADS
HGRN_VAL_DIM = BRANCH_WIDTH // HGRN_HEADS
HGRN_CHUNK = 64
HGRN_MIN_FORGET = 1e-30
SGU_GROUPS = 4
SGU_GROUP_DIM = BRANCH_WIDTH // SGU_GROUPS
SGU_CHUNK = 128
DSA_HEADS = 4
DSA_HEAD_DIM = BRANCH_WIDTH // DSA_HEADS
DSA_IDX_HEADS = 4
DSA_IDX_DIM = 32
DSA_TOPK = 256
MEM_HEADS = 4
MEM_HEAD_DIM = 64
MOE_GROUPS = 4
MOE_EXPERTS_PER_GROUP = 8
MOE_N_EXPERTS = MOE_GROUPS * MOE_EXPERTS_PER_GROUP
MOE_HIDDEN = 256
MOE_TOPK = 2
MOE_BLOCK = 128

IN_SPLITS = (
    2 * DIFF_HEADS * DIFF_HEAD_DIM,
    2 * DIFF_HEADS * DIFF_HEAD_DIM,
    2 * DIFF_HEADS * DIFF_HEAD_DIM,
    HGRN_HEADS * HGRN_KEY_DIM,
    HGRN_HEADS * HGRN_KEY_DIM,
    HGRN_HEADS * HGRN_VAL_DIM,
    HGRN_HEADS * HGRN_VAL_DIM,
    2 * BRANCH_WIDTH,
    DSA_HEADS * DSA_HEAD_DIM,
    DSA_HEAD_DIM,
    DSA_HEAD_DIM,
    DSA_IDX_HEADS * DSA_IDX_DIM,
    DSA_IDX_DIM,
    DSA_IDX_HEADS,
    N_BRANCH * D_MODEL,
)
IN_WIDTH = sum(IN_SPLITS)

kernel_name = "hybrid_gated_four_mixer_hmoe"


def _rmsnorm(x, gain=None):
    xf = x.astype(jnp.float32)
    y = xf * lax.rsqrt(jnp.mean(xf * xf, axis=-1, keepdims=True) + NORM_EPS)
    if gain is not None:
        y = y * gain.astype(jnp.float32)
    return y.astype(x.dtype)


def _layernorm(x):
    xf = x.astype(jnp.float32)
    mu = jnp.mean(xf, axis=-1, keepdims=True)
    xc = xf - mu
    y = xc * lax.rsqrt(jnp.mean(xc * xc, axis=-1, keepdims=True) + NORM_EPS)
    return y.astype(x.dtype)


def _rope(x, positions):
    d = x.shape[-1]
    half = d // 2
    inv = ROPE_THETA ** (-jnp.arange(half, dtype=jnp.float32) * (2.0 / d))
    ang = positions.astype(jnp.float32)[..., None] * inv
    ang = ang.reshape(ang.shape[:2] + (1,) * (x.ndim - 3) + (half,))
    cos, sin = jnp.cos(ang), jnp.sin(ang)
    xf = x.astype(jnp.float32)
    x1, x2 = xf[..., :half], xf[..., half:]
    return jnp.concatenate([x1 * cos - x2 * sin, x2 * cos + x1 * sin], axis=-1).astype(x.dtype)


def _to_blocks(t, blk):
    b, s = t.shape[:2]
    return jnp.moveaxis(t.reshape((b, s // blk, blk) + t.shape[2:]), 1, 0)


def _from_blocks(t):
    t = jnp.moveaxis(t, 0, 1)
    return t.reshape((t.shape[0], t.shape[1] * t.shape[2]) + t.shape[3:])


def _diff_attention(q, k, v, lam, lam_init):
    s_len = q.shape[1]
    scale = DIFF_HEAD_DIM ** -0.5
    key_pos = jnp.arange(s_len)
    starts = jnp.arange(s_len // QUERY_BLOCK) * QUERY_BLOCK

    def block(args):
        qb, start = args
        s = jnp.einsum('bqhmd,bkhmd->bhmqk', qb, k).astype(jnp.float32) * scale
        qpos = start + jnp.arange(QUERY_BLOCK)
        causal = key_pos[None, :] <= qpos[:, None]
        p = jax.nn.softmax(jnp.where(causal, s, -jnp.inf), axis=-1)
        a = p[:, :, 0] - lam * p[:, :, 1]
        return jnp.einsum('bhqk,bkhe->bqhe', a.astype(v.dtype), v)

    o = _from_blocks(lax.map(block, (_to_blocks(q, QUERY_BLOCK), starts)))
    return _rmsnorm(o) * (1.0 - lam_init)


def _hgrn2(q, f_pre, inp, g, lb):
    b, s_len = q.shape[:2]
    H, DK, DV, C = HGRN_HEADS, HGRN_KEY_DIM, HGRN_VAL_DIM, HGRN_CHUNK
    qf = jax.nn.silu(q.astype(jnp.float32)).reshape(b, s_len, H, DK)
    fp = f_pre.astype(jnp.float32).reshape(b, s_len, H, DK)
    lbh = lb.astype(jnp.float32).reshape(H, DK)
    f = lbh + (1.0 - lbh) * jax.nn.sigmoid(fp)
    log_f = jnp.log(jnp.maximum(f, HGRN_MIN_FORGET))
    kf = (1.0 - lbh) * jax.nn.sigmoid(-fp)
    vf = inp.astype(jnp.float32).reshape(b, s_len, H, DV)

    def chunks(t):
        return t.reshape(b, s_len // C, C, H, t.shape[-1]).transpose(1, 0, 3, 2, 4)

    qc, kc, vc = chunks(qf), chunks(kf), chunks(vf)
    bc = jnp.cumsum(chunks(log_f), axis=3)
    tril = jnp.tril(jnp.ones((C, C), dtype=bool))

    def step(state, xs):
        qt, kt, vt, bt = xs
        o_inter = jnp.einsum('bhtk,bhkv->bhtv', qt * jnp.exp(bt), state)
        rel = jnp.where(tril[:, :, None], bt[:, :, :, None, :] - bt[:, :, None, :, :], -jnp.inf)
        att = jnp.einsum('bhtk,bhsk,bhtsk->bhts', qt, kt, jnp.exp(rel))
        o = o_inter + jnp.einsum('bhts,bhsv->bhtv', att, vt)
        b_end = bt[:, :, -1:, :]
        state = (jnp.exp(b_end[:, :, 0, :, None]) * state
                 + jnp.einsum('bhsk,bhsv->bhkv', kt * jnp.exp(b_end - bt), vt))
        return state, o

    state0 = jnp.zeros((b, H, DK, DV), jnp.float32)
    _, o = lax.scan(step, state0, (qc, kc, vc, bc))
    o = o.transpose(1, 0, 3, 2, 4).reshape(b, s_len, H, DV)
    o = _rmsnorm(o).reshape(b, s_len, H * DV)
    return (o * jax.nn.silu(g.astype(jnp.float32))).astype(g.dtype)


def _spatial_gating(uv, w_s, b_s):
    b, s_len = uv.shape[:2]
    u, v = jnp.split(jax.nn.gelu(uv), 2, axis=-1)
    v = _layernorm(v).reshape(b, s_len // SGU_CHUNK, SGU_CHUNK, SGU_GROUPS, SGU_GROUP_DIM)
    w = w_s * jnp.tril(jnp.ones((SGU_CHUNK, SGU_CHUNK), w_s.dtype))
    mixed = jnp.einsum('gts,bcsgd->bctgd', w, v) + b_s.T[:, :, None]
    return u * mixed.reshape(b, s_len, BRANCH_WIDTH)


def _dsa_attention(q, k, v, iq, ik, iw):
    b, s_len = q.shape[:2]
    n_sel = min(DSA_TOPK, s_len // 4)
    scale = DSA_HEAD_DIM ** -0.5
    iw = iw.astype(jnp.float32) * (DSA_IDX_HEADS ** -0.5 * DSA_IDX_DIM ** -0.5)
    key_pos = jnp.arange(s_len)
    starts = jnp.arange(s_len // QUERY_BLOCK) * QUERY_BLOCK

    def block(args):
        qb, iqb, iwb, start = args
        qpos = start + jnp.arange(QUERY_BLOCK)
        sc = jax.nn.relu(jnp.einsum('bqhd,bsd->bqhs', iqb, ik).astype(jnp.float32))
        sc = jnp.einsum('bqhs,bqh->bqs', sc, iwb)
        sc = jnp.where(key_pos[None, None, :] <= qpos[None, :, None], sc, -jnp.inf)
        _, idx = lax.top_k(sc, n_sel)
        valid = idx <= qpos[None, :, None]
        kg = jax.vmap(lambda kk, ii: kk[ii])(k, idx)
        vg = jax.vmap(lambda vv, ii: vv[ii])(v, idx)
        logits = jnp.einsum('bqhd,bqnd->bqhn', qb, kg).astype(jnp.float32) * scale
        p = jax.nn.softmax(jnp.where(valid[:, :, None, :], logits, -jnp.inf), axis=-1)
        return jnp.einsum('bqhn,bqnd->bqhd', p.astype(v.dtype), vg)

    o = _from_blocks(lax.map(block, (_to_blocks(q, QUERY_BLOCK), _to_blocks(iq, QUERY_BLOCK),
                                     _to_blocks(iw, QUERY_BLOCK), starts)))
    return o.reshape(b, s_len, DSA_HEADS * DSA_HEAD_DIM)


def _hybrid_mixer(h, positions, w_in, lam_vec, lam_init, lb, sgu_w, sgu_b, w_branch, w_out):
    b, s_len, _ = h.shape
    offsets = np.cumsum(IN_SPLITS)[:-1].tolist()
    (a_q, a_k, a_v, b_q, b_f, b_i, b_g, c_uv, d_q, d_k, d_v, d_iq, d_ik, d_iw,
     gate_pre) = jnp.split(h @ w_in, offsets, axis=-1)
    lv = lam_vec.astype(jnp.float32)
    lam = jnp.exp(jnp.sum(lv[0] * lv[1])) - jnp.exp(jnp.sum(lv[2] * lv[3])) + lam_init
    aq = _rope(a_q.reshape(b, s_len, DIFF_HEADS, 2, DIFF_HEAD_DIM), positions)
    ak = _rope(a_k.reshape(b, s_len, DIFF_HEADS, 2, DIFF_HEAD_DIM), positions)
    av = a_v.reshape(b, s_len, DIFF_HEADS, 2 * DIFF_HEAD_DIM)
    y_a = _diff_attention(aq, ak, av, lam, lam_init).reshape(b, s_len, BRANCH_WIDTH)
    y_b = _hgrn2(b_q, b_f, b_i, b_g, lb)
    y_c = _spatial_gating(c_uv, sgu_w, sgu_b)
    dq = _rope(d_q.reshape(b, s_len, DSA_HEADS, DSA_HEAD_DIM), positions)
    dk = _rope(d_k, positions)
    diq = _rope(d_iq.reshape(b, s_len, DSA_IDX_HEADS, DSA_IDX_DIM), positions)
    dik = _rope(d_ik, positions)
    y_d = _dsa_attention(dq, dk, d_v, diq, dik, d_iw)
    y = jnp.stack([y_a, y_b, y_c, y_d], axis=2)
    branch = jnp.einsum('bsnc,ncd->bsnd', y, w_branch)
    gates = jax.nn.sigmoid(gate_pre.reshape(b, s_len, N_BRANCH, D_MODEL))
    merged = jnp.sum(gates * branch, axis=2)
    return merged @ w_out


def _memory_attention(h, mem_n, w_q, w_kv, w_o):
    b, s_len, _ = h.shape
    m = mem_n.shape[1]
    q = (h @ w_q).reshape(b, s_len, MEM_HEADS, MEM_HEAD_DIM)
    kv = (mem_n @ w_kv).reshape(b, m, 2, MEM_HEADS, MEM_HEAD_DIM)
    s = jnp.einsum('bqhd,bmhd->bhqm', q, kv[:, :, 0]).astype(jnp.float32) * MEM_HEAD_DIM ** -0.5
    p = jax.nn.softmax(s, axis=-1)
    o = jnp.einsum('bhqm,bmhd->bqhd', p.astype(h.dtype), kv[:, :, 1])
    return o.reshape(b, s_len, MEM_HEADS * MEM_HEAD_DIM) @ w_o


def _hier_moe(h, w_rg, b_rg, w_re, b_re, w_gate, w_up, w_down):
    b, s_len, d = h.shape
    x = h.reshape(-1, d)
    n = x.shape[0]
    rows = jnp.arange(n)
    gl = (x @ w_rg).astype(jnp.float32) + b_rg.astype(jnp.float32)
    pg = jax.nn.softmax(gl, axis=-1)
    gsel = jnp.argmax(gl, axis=-1)
    pg_sel = pg[rows, gsel]
    el = ((x @ w_re).astype(jnp.float32) + b_re.astype(jnp.float32)).reshape(n, MOE_GROUPS, MOE_EXPERTS_PER_GROUP)
    pe = jax.nn.softmax(el[rows, gsel], axis=-1)
    top_p, top_i = lax.top_k(pe, MOE_TOPK)
    comb = pg_sel[:, None] * top_p / jnp.sum(top_p, axis=-1, keepdims=True)
    eid = gsel[:, None] * MOE_EXPERTS_PER_GROUP + top_i
    m = n * MOE_TOPK
    flat_e = eid.reshape(-1)
    flat_t = jnp.repeat(rows, MOE_TOPK)
    flat_w = comb.reshape(-1)
    order = jnp.argsort(flat_e)
    se = flat_e[order]
    counts = jnp.bincount(flat_e, length=MOE_N_EXPERTS)
    padded = (counts + MOE_BLOCK - 1) // MOE_BLOCK * MOE_BLOCK
    start = jnp.cumsum(counts) - counts
    pend = jnp.cumsum(padded)
    pstart = pend - padded
    dest = pstart[se] + jnp.arange(m) - start[se]
    n_blocks = -(-m // MOE_BLOCK) + MOE_N_EXPERTS
    p_rows = n_blocks * MOE_BLOCK
    buf_t = jnp.zeros((p_rows,), jnp.int32).at[dest].set(flat_t[order].astype(jnp.int32))
    buf_w = jnp.zeros((p_rows,), jnp.float32).at[dest].set(flat_w[order])
    block_e = jnp.clip(jnp.searchsorted(pend, jnp.arange(n_blocks) * MOE_BLOCK, side='right'),
                       0, MOE_N_EXPERTS - 1)
    xb = x[buf_t].reshape(n_blocks, MOE_BLOCK, d)

    def expert_block(args):
        xe, e = args
        hid = jax.nn.silu(xe @ w_gate[e]) * (xe @ w_up[e])
        return hid @ w_down[e]

    yb = lax.map(expert_block, (xb, block_e)).reshape(p_rows, d)
    y = jnp.zeros((n, d), x.dtype).at[buf_t].add(yb * buf_w[:, None].astype(x.dtype))
    return y.reshape(b, s_len, d)


def setup_inputs(seed: int = 0) -> dict:
    key = jax.random.key(seed)
    ks = jax.random.split(key, 26)
    f32 = jnp.float32
    nrm = lambda k, shape, scale: jax.random.normal(k, shape, f32) * scale
    gain = lambda k, shape: 1.0 + 0.01 * jax.random.normal(k, shape, f32)
    offs = jax.random.randint(ks[2], (BATCH, 1), 0, 1024)
    positions = (offs + jnp.arange(SEQ)[None, :]).astype(jnp.int32)
    return {
        "x": nrm(ks[0], (BATCH, SEQ, D_MODEL), 1.0),
        "mem": nrm(ks[1], (BATCH, MEM_LEN, D_MODEL), 1.0),
        "positions": positions,
        "norm_mix": gain(ks[3], (DEPTH, D_MODEL)),
        "w_in": nrm(ks[4], (DEPTH, D_MODEL, IN_WIDTH), D_MODEL ** -0.5),
        "diff_lambda": nrm(ks[5], (DEPTH, 4, DIFF_HEAD_DIM), 0.1),
        "hgrn_lb_logits": nrm(ks[6], (DEPTH, HGRN_HEADS * HGRN_KEY_DIM), 0.5),
        "spatial_w": nrm(ks[7], (DEPTH, SGU_GROUPS, SGU_CHUNK, SGU_CHUNK), 0.05),
        "spatial_b": gain(ks[8], (DEPTH, SGU_GROUPS, SGU_CHUNK)),
        "w_branch": nrm(ks[9], (DEPTH, N_BRANCH, BRANCH_WIDTH, D_MODEL), BRANCH_WIDTH ** -0.5),
        "w_out": nrm(ks[10], (DEPTH, D_MODEL, D_MODEL), D_MODEL ** -0.5),
        "norm_mem_q": gain(ks[11], (DEPTH, D_MODEL)),
        "norm_mem_kv": gain(ks[12], (DEPTH, D_MODEL)),
        "w_mem_q": nrm(ks[13], (DEPTH, D_MODEL, MEM_HEADS * MEM_HEAD_DIM), D_MODEL ** -0.5),
        "w_mem_kv": nrm(ks[14], (DEPTH, D_MODEL, 2 * MEM_HEADS * MEM_HEAD_DIM), D_MODEL ** -0.5),
        "w_mem_o": nrm(ks[15], (DEPTH, MEM_HEADS * MEM_HEAD_DIM, D_MODEL), (MEM_HEADS * MEM_HEAD_DIM) ** -0.5),
        "norm_ffn": gain(ks[16], (DEPTH, D_MODEL)),
        "w_router_group": nrm(ks[17], (DEPTH, D_MODEL, MOE_GROUPS), D_MODEL ** -0.5),
        "b_router_group": nrm(ks[18], (DEPTH, MOE_GROUPS), 0.01),
        "w_router_expert": nrm(ks[19], (DEPTH, D_MODEL, MOE_N_EXPERTS), D_MODEL ** -0.5),
        "b_router_expert": nrm(ks[20], (DEPTH, MOE_N_EXPERTS), 0.01),
        "w_exp_gate": nrm(ks[21], (DEPTH, MOE_N_EXPERTS, D_MODEL, MOE_HIDDEN), D_MODEL ** -0.5),
        "w_exp_up": nrm(ks[22], (DEPTH, MOE_N_EXPERTS, D_MODEL, MOE_HIDDEN), D_MODEL ** -0.5),
        "w_exp_down": nrm(ks[23], (DEPTH, MOE_N_EXPERTS, MOE_HIDDEN, D_MODEL), MOE_HIDDEN ** -0.5),
        "norm_final": gain(ks[24], (D_MODEL,)),
    }


def reference(x, mem, positions, norm_mix, w_in, diff_lambda, hgrn_lb_logits, spatial_w, spatial_b,
              w_branch, w_out, norm_mem_q, norm_mem_kv, w_mem_q, w_mem_kv, w_mem_o, norm_ffn,
              w_router_group, b_router_group, w_router_expert, b_router_expert,
              w_exp_gate, w_exp_up, w_exp_down, norm_final):
    lb_w = jax.nn.softmax(hgrn_lb_logits.astype(jnp.float32), axis=0)
    lower_bounds = jnp.cumsum(lb_w, axis=0) - lb_w[0]
    for l in range(DEPTH):
        lam_init = 0.8 - 0.6 * math.exp(-0.3 * l)
        x = x + _hybrid_mixer(_rmsnorm(x, norm_mix[l]), positions, w_in[l], diff_lambda[l], lam_init,
                              lower_bounds[l], spatial_w[l], spatial_b[l], w_branch[l], w_out[l])
        x = x + _memory_attention(_rmsnorm(x, norm_mem_q[l]), _rmsnorm(mem, norm_mem_kv[l]),
                                  w_mem_q[l], w_mem_kv[l], w_mem_o[l])
        x = x + _hier_moe(_rmsnorm(x, norm_ffn[l]), w_router_group[l], b_router_group[l],
                          w_router_expert[l], b_router_expert[l],
                          w_exp_gate[l], w_exp_up[l], w_exp_down[l])
    return _rmsnorm(x, norm_final)
```

```python
import functools
import math

import numpy as np
import jax
import jax.numpy as jnp
from jax import lax
from jax.experimental import pallas as pl
from jax.experimental.pallas import tpu as pltpu

F32 = jnp.float32
BF16 = jnp.bfloat16

NORM_EPS = 1e-6
ROPE_THETA = 10000.0
NEG_BIG = -1e30

N_BRANCH = 4
BRANCH_WIDTH = 256
DIFF_HEADS = 4
DIFF_HEAD_DIM = 32
HGRN_HEADS = 4
HGRN_DIM = 64
HGRN_CHUNK = 64
HGRN_MIN_FORGET = 1e-30
SGU_GROUPS = 4
SGU_GROUP_DIM = 64
SGU_CHUNK = 128
DSA_HEADS = 4
DSA_HEAD_DIM = 64
DSA_IDX_HEADS = 4
DSA_IDX_DIM = 32
DSA_TOPK = 256
MEM_HEADS = 4
MEM_HEAD_DIM = 64
MOE_GROUPS = 4
MOE_EXPERTS_PER_GROUP = 8
MOE_N_EXPERTS = 32

LANES = 128
VMEM_LIMIT = 56 * 1024 * 1024

PROJ_TM = 256
DIFF_TQ = 256
DIFF_TK = 512
HGRN_TC = 512
DSA_TQ = 128
DSA_TK = 512
MERGE_TM = 256
MOE_TM = 1024
ROPE_TM = 1024

C_AQ, C_AQS, C_AK, C_AKS, C_AV = 0, 256, 512, 768, 1024
C_HB = 1280
C_UV = 2304
C_DQ, C_DQS = 2816, 3072
C_IQ, C_IQS = 3328, 3456
C_DK, C_DKS = 3584, 3712
C_DVW = 3840
C_TOTAL = 3968


def _params(sem):
    return pltpu.CompilerParams(dimension_semantics=sem, vmem_limit_bytes=VMEM_LIMIT)


def _const_spec(shape):
    nd = len(shape)
    return pl.BlockSpec(shape, lambda *_: (0,) * nd, pipeline_mode=pl.Buffered(1))


def _rms(xf, gain=None):
    y = xf * lax.rsqrt(jnp.mean(xf * xf, axis=-1, keepdims=True) + NORM_EPS)
    return y if gain is None else y * gain


def _dot(a, b):
    return jnp.dot(a, b, preferred_element_type=F32)


def _dot_nt(a, b):
    return lax.dot_general(a, b, (((1,), (1,)), ((), ())), preferred_element_type=F32)


def _rope_table_kernel(pos_ref, frq_ref, sgn_ref, c32_ref, s32_ref, c64_ref, s64_ref):
    pos = pos_ref[...].astype(F32)
    a32 = pos * frq_ref[0:1, :]
    a64 = pos * frq_ref[1:2, :]
    c32_ref[...] = jnp.cos(a32)
    s32_ref[...] = jnp.sin(a32) * sgn_ref[0:1, :]
    c64_ref[...] = jnp.cos(a64)
    s64_ref[...] = jnp.sin(a64) * sgn_ref[1:2, :]


def _rope_tables(positions):
    n = positions.size
    pos = positions.reshape(n, 1).astype(jnp.int32)
    lane = np.arange(256)
    inv32 = ROPE_THETA ** (-jnp.arange(16, dtype=F32) * (2.0 / 32))
    inv64 = ROPE_THETA ** (-jnp.arange(32, dtype=F32) * (2.0 / 64))
    frq = jnp.stack([inv32[lane % 16], inv64[lane % 32]])
    sgn = jnp.asarray(np.stack([np.where(lane % 32 < 16, -1.0, 1.0),
                                np.where(lane % 64 < 32, -1.0, 1.0)]), F32)
    tm = ROPE_TM
    tab = jax.ShapeDtypeStruct((n, 256), F32)
    return pl.pallas_call(
        _rope_table_kernel,
        out_shape=(tab, tab, tab, tab),
        grid=(n // tm,),
        in_specs=[pl.BlockSpec((tm, 1), lambda i: (i, 0)), _const_spec((2, 256)), _const_spec((2, 256))],
        out_specs=tuple(pl.BlockSpec((tm, 256), lambda i: (i, 0)) for _ in range(4)),
        compiler_params=_params(("parallel",)),
        name="rope_tables",
    )(pos, frq, sgn)


def _gelu_tanh(x):
    return 0.5 * x * (1.0 + jnp.tanh(math.sqrt(2.0 / math.pi) * (x + 0.044715 * (x * x * x))))


def _proj_kernel(x_ref, g_ref, w_ref, c32_ref, s32_ref, c64_ref, s64_ref, sw_ref, sb_ref,
                 qa_ref, kat_ref, va_ref, hb_ref, yc_ref, qd_ref, iq_ref, dkt_ref, dvw_ref, *, tm):
    h = _rms(x_ref[...], g_ref[...]).astype(BF16)

    def proj(c0, width):
        return _dot(h, w_ref[:, c0:c0 + width])

    c32, s32, c64, s64 = c32_ref[...], s32_ref[...], c64_ref[...], s64_ref[...]
    qa = (proj(C_AQ, 256) * c32 + proj(C_AQS, 256) * s32) * (DIFF_HEAD_DIM ** -0.5)
    qa_ref[...] = qa.astype(BF16)
    ka = proj(C_AK, 256) * c32 + proj(C_AKS, 256) * s32
    kat_ref[0] = ka.T.astype(BF16)
    va_ref[...] = proj(C_AV, 256).astype(BF16)
    hb_ref[...] = proj(C_HB, 1024)
    qd = (proj(C_DQ, 256) * c64 + proj(C_DQS, 256) * s64) * (DSA_HEAD_DIM ** -0.5)
    qd_ref[...] = qd.astype(BF16)
    iq = proj(C_IQ, 128) * c32[:, :128] + proj(C_IQS, 128) * s32[:, :128]
    iq_ref[...] = iq.astype(BF16)
    lane = lax.broadcasted_iota(jnp.int32, (tm, 128), 1)
    cdk = jnp.where(lane < 64, c64[:, :128], c32[:, :128])
    sdk = jnp.where(lane < 64, s64[:, :128], s32[:, :128])
    dk = proj(C_DK, 128) * cdk + proj(C_DKS, 128) * sdk
    dkt_ref[0] = dk.T.astype(BF16)
    dvw = proj(C_DVW, 128)
    iw_scale = DSA_IDX_HEADS ** -0.5 * DSA_IDX_DIM ** -0.5
    dvw_ref[...] = jnp.where((lane >= 64) & (lane < 64 + DSA_IDX_HEADS), dvw * iw_scale, dvw)
    uv = _gelu_tanh(proj(C_UV, 512))
    u, v = uv[:, :256], uv[:, 256:]
    mu = jnp.mean(v, axis=-1, keepdims=True)
    vc = v - mu
    vn = (vc * lax.rsqrt(jnp.mean(vc * vc, axis=-1, keepdims=True) + NORM_EPS)).astype(BF16)
    r = lax.broadcasted_iota(jnp.int32, (SGU_GROUPS * SGU_CHUNK, SGU_CHUNK), 0)
    c = lax.broadcasted_iota(jnp.int32, (SGU_GROUPS * SGU_CHUNK, SGU_CHUNK), 1)
    wt = jnp.where((r % SGU_CHUNK) >= c, sw_ref[...], 0.0).astype(BF16)
    lane_grp = lax.broadcasted_iota(jnp.int32, (SGU_CHUNK, 256), 1) // SGU_GROUP_DIM
    for ch in range(tm // SGU_CHUNK):
        r0 = ch * SGU_CHUNK
        full = _dot(wt, vn[r0:r0 + SGU_CHUNK, :])
        mixed = sb_ref[...]
        for g in range(SGU_GROUPS):
            mixed = mixed + jnp.where(lane_grp == g, full[g * SGU_CHUNK:(g + 1) * SGU_CHUNK, :], 0.0)
        yc_ref[r0:r0 + SGU_CHUNK, :] = (u[r0:r0 + SGU_CHUNK, :] * mixed).astype(BF16)


def _projection(x, gain, w1, tabs, sw, sb, batch, seq):
    n, d = x.shape
    tm = PROJ_TM
    spt = seq // tm
    tok = lambda w: pl.BlockSpec((tm, w), lambda i: (i, 0))
    tr = lambda rows: pl.BlockSpec((1, rows, tm), lambda i: (i // spt, 0, i % spt))
    out_shape = (
        jax.ShapeDtypeStruct((n, 256), BF16),
        jax.ShapeDtypeStruct((batch, 256, seq), BF16),
        jax.ShapeDtypeStruct((n, 256), BF16),
        jax.ShapeDtypeStruct((n, 1024), F32),
        jax.ShapeDtypeStruct((n, 256), BF16),
        jax.ShapeDtypeStruct((n, 256), BF16),
        jax.ShapeDtypeStruct((n, 128), BF16),
        jax.ShapeDtypeStruct((batch, 128, seq), BF16),
        jax.ShapeDtypeStruct((n, 128), F32),
    )
    return pl.pallas_call(
        functools.partial(_proj_kernel, tm=tm),
        out_shape=out_shape,
        grid=(n // tm,),
        in_specs=[tok(d), _const_spec((1, d)), _const_spec((d, C_TOTAL)),
                  tok(256), tok(256), tok(256), tok(256),
                  _const_spec((SGU_GROUPS * SGU_CHUNK, SGU_CHUNK)), _const_spec((SGU_CHUNK, 256))],
        out_specs=(tok(256), tr(256), tok(256), tok(1024), tok(256), tok(256), tok(128), tr(128), tok(128)),
        compiler_params=_params(("parallel",)),
        name="projection",
    )(x, gain, w1, *tabs, sw, sb)


def _diff_attn_kernel(lam_ref, q_ref, kt_ref, v_ref, o_ref, *, lam_init, tq, tk):
    q0 = pl.program_id(1) * tq
    kb_diag = q0 // tk
    lv = lam_ref[...]
    lam = (jnp.exp(jnp.sum(lv[0:1] * lv[1:2], axis=-1, keepdims=True))
           - jnp.exp(jnp.sum(lv[2:3] * lv[3:4], axis=-1, keepdims=True)) + lam_init)
    q = q_ref[...]
    rows = q0 + lax.broadcasted_iota(jnp.int32, (tq, tk), 0)
    cols = lax.broadcasted_iota(jnp.int32, (tq, tk), 1)
    heads = []
    for hd in range(DIFF_HEADS):
        pair0 = (hd // 2) * 128
        maps = []
        for mp in range(2):
            c0 = hd * 2 * DIFF_HEAD_DIM + mp * DIFF_HEAD_DIM
            qhm = q[:, c0:c0 + DIFF_HEAD_DIM]

            def step(kb, carry, masked, c0=c0, qhm=qhm, pair0=pair0):
                m_i, l_i, acc = carry
                k0 = pl.multiple_of(kb * tk, tk)
                s = _dot(qhm, kt_ref[0, c0:c0 + DIFF_HEAD_DIM, pl.ds(k0, tk)])
                if masked:
                    s = jnp.where(k0 + cols <= rows, s, NEG_BIG)
                m_new = jnp.maximum(m_i, jnp.max(s, axis=-1, keepdims=True))
                p = jnp.exp(s - m_new)
                alpha = jnp.exp(m_i - m_new)
                l_new = alpha * l_i + jnp.sum(p, axis=-1, keepdims=True)
                pv = _dot(p.astype(BF16), v_ref[pl.ds(k0, tk), pair0:pair0 + 128])
                return m_new, l_new, alpha * acc + pv

            init = (jnp.full((tq, 1), NEG_BIG, F32), jnp.zeros((tq, 1), F32), jnp.zeros((tq, 128), F32))
            carry = lax.fori_loop(0, kb_diag, functools.partial(step, masked=False), init)
            _, l_f, acc = step(kb_diag, carry, True)
            maps.append(acc / l_f)
        o_pair = maps[0] - lam * maps[1]
        o_h = o_pair[:, (hd % 2) * 64:(hd % 2) * 64 + 64]
        heads.append(_rms(o_h) * (1.0 - lam_init))
    o_ref[...] = jnp.concatenate(heads, axis=-1).astype(BF16)


def _diff_attention(lam_vec, qa, kat, va, lam_init, batch, seq):
    tq, tk = DIFF_TQ, DIFF_TK
    nq = seq // tq
    return pl.pallas_call(
        functools.partial(_diff_attn_kernel, lam_init=lam_init, tq=tq, tk=tk),
        out_shape=jax.ShapeDtypeStruct((batch * seq, 256), BF16),
        grid=(batch, nq),
        in_specs=[_const_spec((4, DIFF_HEAD_DIM)),
                  pl.BlockSpec((tq, 256), lambda b, i: (b * nq + i, 0)),
                  pl.BlockSpec((1, 256, seq), lambda b, i: (b, 0, 0)),
                  pl.BlockSpec((seq, 256), lambda b, i: (b, 0))],
        out_specs=pl.BlockSpec((tq, 256), lambda b, i: (b * nq + i, 0)),
        compiler_params=_params(("parallel", "parallel")),
        name="diff_attention",
    )(lam_vec, qa, kat, va)


def _hgrn_kernel(lbl_ref, hb_ref, o_ref, st_ref, pstk_ref, *, layer, tc):
    cz = HGRN_CHUNK
    w = 256

    @pl.when(pl.program_id(1) == 0)
    def _():
        st_ref[...] = jnp.zeros_like(st_ref)

    lg = lbl_ref[...]
    e = jnp.exp(lg - jnp.max(lg, axis=0, keepdims=True))
    lw = e / jnp.sum(e, axis=0, keepdims=True)
    lb = jnp.sum(lw[0:layer + 1], axis=0, keepdims=True) - lw[0:1]

    ri = lax.broadcasted_iota(jnp.int32, (cz, cz), 0)
    ci = lax.broadcasted_iota(jnp.int32, (cz, cz), 1)
    tri = (ri >= ci).astype(F32)
    rb = lax.broadcasted_iota(jnp.int32, (w, w), 0) // HGRN_DIM
    cb = lax.broadcasted_iota(jnp.int32, (w, w), 1) // HGRN_DIM
    same_head = rb == cb
    head_ones = same_head.astype(BF16)
    trows = {r: r + lax.broadcasted_iota(jnp.int32, (cz - r, w), 0) for r in range(0, cz, 16)}

    def chunk(c, carry):
        r0 = pl.multiple_of(c * cz, cz)
        q = hb_ref[pl.ds(r0, cz), 0:256]
        fp = hb_ref[pl.ds(r0, cz), 256:512]
        v = hb_ref[pl.ds(r0, cz), 512:768]
        g = hb_ref[pl.ds(r0, cz), 768:1024]
        qf = q * jax.nn.sigmoid(q)
        f = lb + (1.0 - lb) * jax.nn.sigmoid(fp)
        log_f = jnp.log(jnp.maximum(f, HGRN_MIN_FORGET))
        kf = (1.0 - lb) * jax.nn.sigmoid(-fp)
        bc = jnp.dot(tri, log_f, preferred_element_type=F32, precision=lax.Precision.HIGHEST)
        st = st_ref[...]
        o = _dot_nt((qf * jnp.exp(bc)).astype(BF16), st.astype(BF16))
        for s in range(cz):
            r_lo = (s // 16) * 16
            arg = bc[r_lo:, :] - bc[s:s + 1, :]
            if s > r_lo:
                arg = jnp.where(trows[r_lo] >= s, arg, NEG_BIG)
            p = qf[r_lo:, :] * kf[s:s + 1, :] * jnp.exp(arg)
            if r_lo:
                pstk_ref[s * cz:s * cz + r_lo, :] = jnp.zeros((r_lo, w), BF16)
            pstk_ref[s * cz + r_lo:(s + 1) * cz, :] = p.astype(BF16)
        accs = [jnp.zeros((16, w), F32) for _ in range(cz // 16)]
        for sg in range(cz // 16):
            att = _dot(pstk_ref[sg * 16 * cz:(sg + 1) * 16 * cz, :], head_ones)
            for sl in range(16):
                s = sg * 16 + sl
                for j in range(sg, cz // 16):
                    accs[j] = accs[j] + att[sl * cz + 16 * j:sl * cz + 16 * j + 16, :] * v[s:s + 1, :]
        o = o + jnp.concatenate(accs, axis=0)
        b_end = bc[cz - 1:cz, :]
        kd = kf * jnp.exp(b_end - bc)
        upd = _dot(v.T.astype(BF16), kd.astype(BF16))
        st_ref[...] = st * jnp.exp(b_end) + jnp.where(same_head, upd, 0.0)
        ms = _dot(o * o, head_ones.astype(F32)) * (1.0 / HGRN_DIM)
        y = o * lax.rsqrt(ms + NORM_EPS)
        o_ref[pl.ds(r0, cz), :] = (y * (g * jax.nn.sigmoid(g))).astype(BF16)
        return carry

    lax.fori_loop(0, tc // cz, chunk, 0)


def _hgrn(lb_logits, hb, layer, batch, seq):
    tc = HGRN_TC
    nt = seq // tc
    cz = HGRN_CHUNK
    return pl.pallas_call(
        functools.partial(_hgrn_kernel, layer=layer, tc=tc),
        out_shape=jax.ShapeDtypeStruct((batch * seq, 256), BF16),
        grid=(batch, nt),
        in_specs=[_const_spec(lb_logits.shape),
                  pl.BlockSpec((tc, 1024), lambda b, i: (b * nt + i, 0))],
        out_specs=pl.BlockSpec((tc, 256), lambda b, i: (b * nt + i, 0)),
        scratch_shapes=[pltpu.VMEM((256, 256), F32), pltpu.VMEM((cz * cz, 256), BF16)],
        compiler_params=_params(("parallel", "arbitrary")),
        name="hgrn2",
    )(lb_logits, hb)


def _dsa_kernel(qd_ref, iq_ref, dvwq_ref, dkt_ref, dvwk_ref, o_ref, key_ref, bias_ref, *, tq, tk, n_sel):
    q0 = pl.program_id(1) * tq
    nkb = q0 // tk + 1
    rows = q0 + lax.broadcasted_iota(jnp.int32, (tq, tk), 0)
    cols = lax.broadcasted_iota(jnp.int32, (tq, tk), 1)
    iq = iq_ref[...]
    iw = dvwq_ref[:, 64:64 + DSA_IDX_HEADS]

    def score_block(kb, carry):
        k0 = pl.multiple_of(kb * tk, tk)
        ikt = dkt_ref[0, 64:64 + DSA_IDX_DIM, pl.ds(k0, tk)]
        sc = jnp.zeros((tq, tk), F32)
        for hd in range(DSA_IDX_HEADS):
            sh = _dot(iq[:, hd * DSA_IDX_DIM:(hd + 1) * DSA_IDX_DIM], ikt)
            sc = sc + jnp.maximum(sh, 0.0) * iw[:, hd:hd + 1]
        sc = jnp.where(k0 + cols <= rows, sc + 0.0, -jnp.inf)
        bits = pltpu.bitcast(sc, jnp.int32)
        key_ref[:, pl.ds(k0, tk)] = jnp.where(bits < 0, bits ^ jnp.int32(0x7FFFFFFF), bits)
        return carry

    lax.fori_loop(0, nkb, score_block, 0)

    def count(pred_fn):
        def body(kb, acc):
            k0 = pl.multiple_of(kb * tk, tk)
            blk = key_ref[:, pl.ds(k0, tk)]
            for j in range(tk // LANES):
                acc = acc + jnp.where(pred_fn(blk[:, j * LANES:(j + 1) * LANES]), 1, 0)
            return acc
        acc = lax.fori_loop(0, nkb, body, jnp.zeros((tq, LANES), jnp.int32))
        return jnp.sum(acc, axis=-1, keepdims=True)

    thr = jnp.full((tq, 1), np.int32(-2 ** 31), jnp.int32)
    for bit in range(31, -1, -1):
        trial = thr + np.int32(-2 ** 31 if bit == 31 else 2 ** bit)
        cnt = count(lambda blk, trial=trial: blk >= trial)
        thr = jnp.where(cnt >= n_sel, trial, thr)

    need = (n_sel - count(lambda blk: blk > thr)).astype(F32)
    ur = lax.broadcasted_iota(jnp.int32, (tk, tk), 0)
    uc = lax.broadcasted_iota(jnp.int32, (tk, tk), 1)
    before = (ur < uc).astype(BF16)

    def select_block(kb, carry):
        k0 = pl.multiple_of(kb * tk, tk)
        blk = key_ref[:, pl.ds(k0, tk)]
        eq = blk == thr
        eqf = jnp.where(eq, 1.0, 0.0)
        rank = carry + _dot(eqf.astype(BF16), before)
        sel = (blk > thr) | (eq & (rank < need))
        sel = sel & (k0 + cols <= rows)
        bias_ref[:, pl.ds(k0, tk)] = jnp.where(sel, 0.0, NEG_BIG)
        return carry + jnp.sum(eqf, axis=-1, keepdims=True)

    lax.fori_loop(0, nkb, select_block, jnp.zeros((tq, 1), F32))

    qd = qd_ref[...]
    outs = []
    for hd in range(DSA_HEADS):
        qh = qd[:, hd * DSA_HEAD_DIM:(hd + 1) * DSA_HEAD_DIM]

        def att_block(kb, carry, qh=qh):
            m_i, l_i, acc = carry
            k0 = pl.multiple_of(kb * tk, tk)
            s = _dot(qh, dkt_ref[0, 0:DSA_HEAD_DIM, pl.ds(k0, tk)]) + bias_ref[:, pl.ds(k0, tk)]
            m_new = jnp.maximum(m_i, jnp.max(s, axis=-1, keepdims=True))
            p = jnp.exp(s - m_new)
            alpha = jnp.exp(m_i - m_new)
            l_new = alpha * l_i + jnp.sum(p, axis=-1, keepdims=True)
            pv = _dot(p.astype(BF16), dvwk_ref[pl.ds(k0, tk), :].astype(BF16))
            return m_new, l_new, alpha * acc + pv

        init = (jnp.full((tq, 1), NEG_BIG, F32), jnp.zeros((tq, 1), F32), jnp.zeros((tq, 128), F32))
        _, l_f, acc = lax.fori_loop(0, nkb, att_block, init)
        outs.append((acc / l_f)[:, :DSA_HEAD_DIM])
    o_ref[...] = jnp.concatenate(outs, axis=-1).astype(BF16)


def _dsa(qd, iq, dkt, dvw, batch, seq):
    tq, tk = DSA_TQ, DSA_TK
    nq = seq // tq
    n_sel = min(DSA_TOPK, seq // 4)
    return pl.pallas_call(
        functools.partial(_dsa_kernel, tq=tq, tk=tk, n_sel=n_sel),
        out_shape=jax.ShapeDtypeStruct((batch * seq, 256), BF16),
        grid=(batch, nq),
        in_specs=[pl.BlockSpec((tq, 256), lambda b, i: (b * nq + i, 0)),
                  pl.BlockSpec((tq, 128), lambda b, i: (b * nq + i, 0)),
                  pl.BlockSpec((tq, 128), lambda b, i: (b * nq + i, 0)),
                  pl.BlockSpec((1, 128, seq), lambda b, i: (b, 0, 0)),
                  pl.BlockSpec((seq, 128), lambda b, i: (b, 0))],
        out_specs=pl.BlockSpec((tq, 256), lambda b, i: (b * nq + i, 0)),
        scratch_shapes=[pltpu.VMEM((tq, seq), jnp.int32), pltpu.VMEM((tq, seq), F32)],
        compiler_params=_params(("parallel", "parallel")),
        name="dsa",
    )(qd, iq, dvw, dkt, dvw)


def _mem_kv_kernel(mem_ref, g_ref, w_ref, kt_ref, v_ref):
    mn = _rms(mem_ref[0], g_ref[...]).astype(BF16)
    kv = _dot(mn, w_ref[...])
    kt_ref[0] = kv[:, :256].T.astype(BF16)
    v_ref[0] = kv[:, 256:].astype(BF16)


def _mem_kv(mem, gain, w_kv):
    b, m, d = mem.shape
    return pl.pallas_call(
        _mem_kv_kernel,
        out_shape=(jax.ShapeDtypeStruct((b, 256, m), BF16), jax.ShapeDtypeStruct((b, m, 256), BF16)),
        grid=(b,),
        in_specs=[pl.BlockSpec((1, m, d), lambda i: (i, 0, 0)), _const_spec((1, d)), _const_spec((d, 512))],
        out_specs=(pl.BlockSpec((1, 256, m), lambda i: (i, 0, 0)), pl.BlockSpec((1, m, 256), lambda i: (i, 0, 0))),
        compiler_params=_params(("parallel",)),
        name="mem_kv",
    )(mem, gain, w_kv)


def _merge_kernel(x_ref, ya_ref, yb_ref, yc_ref, yd_ref, gmix_ref, wg_ref, wbr_ref, wout_ref,
                  gq_ref, wq_ref, mkt_ref, mv_ref, wo_ref, gffn_ref, wr_ref, br_ref,
                  x2_ref, h3_ref, cw_ref, *, tm):
    x = x_ref[...]
    d = x.shape[-1]
    h = _rms(x, gmix_ref[...]).astype(BF16)
    merged = jnp.zeros((tm, d), F32)
    for n, y_ref in enumerate((ya_ref, yb_ref, yc_ref, yd_ref)):
        gate = jax.nn.sigmoid(_dot(h, wg_ref[:, n * d:(n + 1) * d]))
        merged = merged + gate * _dot(y_ref[...], wbr_ref[n])
    x1 = x + _dot(merged.astype(BF16), wout_ref[...])
    h2 = _rms(x1, gq_ref[...]).astype(BF16)
    q = (_dot(h2, wq_ref[...]) * (MEM_HEAD_DIM ** -0.5)).astype(BF16)
    lane_head = lax.broadcasted_iota(jnp.int32, (tm, 256), 1) // MEM_HEAD_DIM
    mv = mv_ref[0]
    o = jnp.zeros((tm, 256), F32)
    for hd in range(MEM_HEADS):
        s = _dot(q[:, hd * MEM_HEAD_DIM:(hd + 1) * MEM_HEAD_DIM], mkt_ref[0, hd * MEM_HEAD_DIM:(hd + 1) * MEM_HEAD_DIM, :])
        p = jnp.exp(s - jnp.max(s, axis=-1, keepdims=True))
        p = p / jnp.sum(p, axis=-1, keepdims=True)
        o = o + jnp.where(lane_head == hd, _dot(p.astype(BF16), mv), 0.0)
    x2 = x1 + _dot(o.astype(BF16), wo_ref[...])
    x2_ref[...] = x2
    h3 = _rms(x2, gffn_ref[...])
    h3_ref[...] = h3.astype(BF16)
    logits = jnp.dot(h3, wr_ref[...], preferred_element_type=F32, precision=lax.Precision.HIGHEST) + br_ref[...]
    lane = lax.broadcasted_iota(jnp.int32, (tm, LANES), 1)
    gl = jnp.where(lane < MOE_GROUPS, logits[:, :LANES], -jnp.inf)
    gmax = jnp.max(gl, axis=-1, keepdims=True)
    gsel = jnp.min(jnp.where(gl == gmax, lane, LANES), axis=-1, keepdims=True)
    pg_sel = 1.0 / jnp.sum(jnp.exp(gl - gmax), axis=-1, keepdims=True)
    el = jnp.where(lane // MOE_EXPERTS_PER_GROUP == gsel, logits[:, LANES:], -jnp.inf)
    m1 = jnp.max(el, axis=-1, keepdims=True)
    i1 = jnp.min(jnp.where(el == m1, lane, LANES), axis=-1, keepdims=True)
    el2 = jnp.where(lane == i1, -jnp.inf, el)
    m2 = jnp.max(el2, axis=-1, keepdims=True)
    i2 = jnp.min(jnp.where(el2 == m2, lane, LANES), axis=-1, keepdims=True)
    e21 = jnp.exp(m2 - m1)
    c1 = pg_sel / (1.0 + e21)
    cw_ref[...] = jnp.where(lane == i1, c1, jnp.where(lane == i2, c1 * e21, 0.0))


def _merge(x, ys, gmix, wg, wbr, wout, gq, wq, mkt, mv, wo, gffn, wr, br, batch, seq):
    n, d = x.shape
    tm = MERGE_TM
    spt = seq // tm
    m = mv.shape[1]
    tok = lambda w: pl.BlockSpec((tm, w), lambda i: (i, 0))
    return pl.pallas_call(
        functools.partial(_merge_kernel, tm=tm),
        out_shape=(jax.ShapeDtypeStruct((n, d), F32), jax.ShapeDtypeStruct((n, d), BF16),
                   jax.ShapeDtypeStruct((n, LANES), F32)),
        grid=(n // tm,),
        in_specs=[tok(d), tok(256), tok(256), tok(256), tok(256),
                  _const_spec((1, d)), _const_spec((d, N_BRANCH * d)), _const_spec((N_BRANCH, 256, d)),
                  _const_spec((d, d)), _const_spec((1, d)), _const_spec((d, 256)),
                  pl.BlockSpec((1, 256, m), lambda i: (i // spt, 0, 0)),
                  pl.BlockSpec((1, m, 256), lambda i: (i // spt, 0, 0)),
                  _const_spec((256, d)), _const_spec((1, d)), _const_spec((d, 2 * LANES)), _const_spec((1, 2 * LANES))],
        out_specs=(tok(d), tok(d), tok(LANES)),
        compiler_params=_params(("parallel",)),
        name="merge_mem_router",
    )(x, *ys, gmix, wg, wbr, wout, gq, wq, mkt, mv, wo, gffn, wr, br)


def _moe_kernel(h_ref, cw_ref, x_ref, wg_ref, wu_ref, wd_ref, gfin_ref, o_ref, acc_ref, *, tm, final_norm):
    e = pl.program_id(1)

    @pl.when(e == 0)
    def _():
        acc_ref[...] = jnp.zeros_like(acc_ref)

    h = h_ref[...]
    lane = lax.broadcasted_iota(jnp.int32, (tm, LANES), 1)
    c = jnp.sum(jnp.where(lane == e, cw_ref[...], 0.0), axis=-1, keepdims=True)
    gt = _dot(h, wg_ref[0])
    hid = gt * jax.nn.sigmoid(gt) * _dot(h, wu_ref[0]) * c
    acc_ref[...] += _dot(hid.astype(BF16), wd_ref[0])

    @pl.when(e == pl.num_programs(1) - 1)
    def _():
        y = x_ref[...] + acc_ref[...]
        o_ref[...] = _rms(y, gfin_ref[...]) if final_norm else y


def _moe(h3, cw, x2, wg, wu, wd, gfin, final_norm):
    n, d = x2.shape
    tm = MOE_TM
    ne, _, hid = wg.shape
    return pl.pallas_call(
        functools.partial(_moe_kernel, tm=tm, final_norm=final_norm),
        out_shape=jax.ShapeDtypeStruct((n, d), F32),
        grid=(n // tm, ne),
        in_specs=[pl.BlockSpec((tm, d), lambda i, e: (i, 0)),
                  pl.BlockSpec((tm, LANES), lambda i, e: (i, 0)),
                  pl.BlockSpec((tm, d), lambda i, e: (i, 0)),
                  pl.BlockSpec((1, d, hid), lambda i, e: (e, 0, 0)),
                  pl.BlockSpec((1, d, hid), lambda i, e: (e, 0, 0)),
                  pl.BlockSpec((1, hid, d), lambda i, e: (e, 0, 0)),
                  _const_spec((1, d))],
        out_specs=pl.BlockSpec((tm, d), lambda i, e: (i, 0)),
        scratch_shapes=[pltpu.VMEM((tm, d), F32)],
        compiler_params=_params(("parallel", "arbitrary")),
        name="moe_experts",
    )(h3, cw, x2, wg, wu, wd, gfin)


def _swap_half(w, group):
    j = np.arange(w.shape[-1])
    return w[:, (j // group) * group + (j % group + group // 2) % group]


def _projection_weight(w_in):
    d = w_in.shape[0]
    o = np.cumsum([0, 256, 256, 256, 256, 256, 256, 256, 512, 256, 64, 64, 128, 32, 4])
    seg = lambda i: w_in[:, o[i]:o[i + 1]]
    a_q, a_k, a_v = seg(0), seg(1), seg(2)
    hb = w_in[:, o[3]:o[7]]
    c_uv, d_q, d_k, d_v, d_iq, d_ik, d_iw = seg(7), seg(8), seg(9), seg(10), seg(11), seg(12), seg(13)
    z = lambda k: jnp.zeros((d, k), w_in.dtype)
    cols = [a_q, _swap_half(a_q, 32), a_k, _swap_half(a_k, 32), a_v, hb, c_uv,
            d_q, _swap_half(d_q, 64), d_iq, _swap_half(d_iq, 32),
            d_k, d_ik, z(32), _swap_half(d_k, 64), _swap_half(d_ik, 32), z(32),
            d_v, d_iw, z(60)]
    w1 = jnp.concatenate(cols, axis=1).astype(BF16)
    assert w1.shape[1] == C_TOTAL
    return w1, w_in[:, o[14]:].astype(BF16)


def kernel(x, mem, positions, norm_mix, w_in, diff_lambda, hgrn_lb_logits, spatial_w, spatial_b, w_branch, w_out,
           norm_mem_q, norm_mem_kv, w_mem_q, w_mem_kv, w_mem_o, norm_ffn, w_router_group, b_router_group,
           w_router_expert, b_router_expert, w_exp_gate, w_exp_up, w_exp_down, norm_final):
    batch, seq, d = x.shape
    depth = w_in.shape[0]
    n = batch * seq
    xf = x.reshape(n, d)
    tabs = _rope_tables(positions)
    row = lambda v: v.reshape(1, -1).astype(F32)
    for l in range(depth):
        lam_init = 0.8 - 0.6 * math.exp(-0.3 * l)
        w1, w_gate = _projection_weight(w_in[l])
        sw = spatial_w[l].reshape(SGU_GROUPS * SGU_CHUNK, SGU_CHUNK)
        sb = jnp.repeat(spatial_b[l].T, SGU_GROUP_DIM, axis=1)
        qa, kat, va, hb, y_c, qd, iq, dkt, dvw = _projection(xf, row(norm_mix[l]), w1, tabs, sw, sb, batch, seq)
        y_a = _diff_attention(diff_lambda[l], qa, kat, va, lam_init, batch, seq)
        y_b = _hgrn(hgrn_lb_logits, hb, l, batch, seq)
        y_d = _dsa(qd, iq, dkt, dvw, batch, seq)
        mkt, mv = _mem_kv(mem, row(norm_mem_kv[l]), w_mem_kv[l].astype(BF16))
        wr = jnp.zeros((d, 2 * LANES), F32)
        wr = wr.at[:, :MOE_GROUPS].set(w_router_group[l]).at[:, LANES:LANES + MOE_N_EXPERTS].set(w_router_expert[l])
        br = jnp.zeros((1, 2 * LANES), F32)
        br = br.at[0, :MOE_GROUPS].set(b_router_group[l]).at[0, LANES:LANES + MOE_N_EXPERTS].set(b_router_expert[l])
        x2, h3, cw = _merge(xf, (y_a, y_b, y_c, y_d), row(norm_mix[l]), w_gate, w_branch[l].astype(BF16),
                            w_out[l].astype(BF16), row(norm_mem_q[l]), w_mem_q[l].astype(BF16), mkt, mv,
                            w_mem_o[l].astype(BF16), row(norm_ffn[l]), wr, br, batch, seq)
        xf = _moe(h3, cw, x2, w_exp_gate[l].astype(BF16), w_exp_up[l].astype(BF16), w_exp_down[l].astype(BF16),
                  row(norm_final), final_norm=(l == depth - 1))
    return xf.reshape(batch, seq, d)
```

```python
import functools
import math

import numpy as np
import jax
import jax.numpy as jnp
from jax import lax
from jax.experimental import pallas as pl
from jax.experimental.pallas import tpu as pltpu

F32 = jnp.float32
BF16 = jnp.bfloat16

NORM_EPS = 1e-6
ROPE_THETA = 10000.0
NEG_BIG = -1e30

N_BRANCH = 4
BRANCH_WIDTH = 256
DIFF_HEADS = 4
DIFF_HEAD_DIM = 32
HGRN_HEADS = 4
HGRN_DIM = 64
HGRN_CHUNK = 64
HGRN_MIN_FORGET = 1e-30
SGU_GROUPS = 4
SGU_GROUP_DIM = 64
SGU_CHUNK = 128
DSA_HEADS = 4
DSA_HEAD_DIM = 64
DSA_IDX_HEADS = 4
DSA_IDX_DIM = 32
DSA_TOPK = 256
MEM_HEADS = 4
MEM_HEAD_DIM = 64
MOE_GROUPS = 4
MOE_EXPERTS_PER_GROUP = 8
MOE_N_EXPERTS = 32

LANES = 128
VMEM_LIMIT = 56 * 1024 * 1024

PROJ_TM = 256
DIFF_TQ = 256
DIFF_TK = 512
HGRN_TC = 512
DSA_TQ = 256
DSA_TK = 512
MERGE_TM = 256
MOE_TM = 1024
ROPE_TM = 1024

C_AQ, C_AK, C_AV = 0, 256, 512
C_HB = 768
C_UV = 1792
C_DQ = 2304
C_DKV = 2560
C_IQ = 2688
C_IKW = 2816
IW_LANE = 32
C_GATE = 2852
C_TOTAL = 2944
LOG2E = math.log2(math.e)


def _params(sem):
    return pltpu.CompilerParams(dimension_semantics=sem, vmem_limit_bytes=VMEM_LIMIT)


def _const_spec(shape):
    nd = len(shape)
    return pl.BlockSpec(shape, lambda *_: (0,) * nd, pipeline_mode=pl.Buffered(1))


def _rms(xf, gain=None):
    y = xf * lax.rsqrt(jnp.mean(xf * xf, axis=-1, keepdims=True) + NORM_EPS)
    return y if gain is None else y * gain


def _dot(a, b):
    return jnp.dot(a, b, preferred_element_type=F32)


def _dot_nt(a, b):
    return lax.dot_general(a, b, (((1,), (1,)), ((), ())), preferred_element_type=F32)


def _rope_table_kernel(pos_ref, frq_ref, sgn_ref, c32_ref, s32_ref, c64_ref, s64_ref):
    pos = pos_ref[...].astype(F32)
    a32 = pos * frq_ref[0:1, :]
    a64 = pos * frq_ref[1:2, :]
    c32_ref[...] = jnp.cos(a32)
    s32_ref[...] = jnp.sin(a32) * sgn_ref[0:1, :]
    c64_ref[...] = jnp.cos(a64)
    s64_ref[...] = jnp.sin(a64) * sgn_ref[1:2, :]


def _rope_tables(positions):
    n = positions.size
    pos = positions.reshape(n, 1).astype(jnp.int32)
    lane = np.arange(256)
    inv32 = ROPE_THETA ** (-jnp.arange(16, dtype=F32) * (2.0 / 32))
    inv64 = ROPE_THETA ** (-jnp.arange(32, dtype=F32) * (2.0 / 64))
    frq = jnp.stack([inv32[lane % 16], inv64[lane % 32]])
    sgn = jnp.asarray(np.stack([np.where(lane % 32 < 16, -1.0, 1.0),
                                np.where(lane % 64 < 32, -1.0, 1.0)]), F32)
    tm = ROPE_TM
    tab = jax.ShapeDtypeStruct((n, 256), F32)
    return pl.pallas_call(
        _rope_table_kernel,
        out_shape=(tab, tab, tab, tab),
        grid=(n // tm,),
        in_specs=[pl.BlockSpec((tm, 1), lambda i: (i, 0)), _const_spec((2, 256)), _const_spec((2, 256))],
        out_specs=tuple(pl.BlockSpec((tm, 256), lambda i: (i, 0)) for _ in range(4)),
        compiler_params=_params(("parallel",)),
        name="rope_tables",
    )(pos, frq, sgn)


def _gelu_tanh(x):
    return 0.5 * x * (1.0 + jnp.tanh(math.sqrt(2.0 / math.pi) * (x + 0.044715 * (x * x * x))))


def _rope(x, cos, sin_signed, half):
    w = x.shape[-1]
    lane = lax.broadcasted_iota(jnp.int32, x.shape, 1)
    partner = jnp.where(lane % (2 * half) < half, pltpu.roll(x, w - half, 1), pltpu.roll(x, half, 1))
    return x * cos + partner * sin_signed


def _proj_kernel(x_ref, g_ref, w_ref, c32_ref, s32_ref, c64_ref, s64_ref, sw_ref, sb_ref,
                 qa_ref, kat_ref, va_ref, hb_ref, yc_ref, qd_ref, iq_ref, dkt_ref, dkv_ref, iw_ref, *, tm):
    h = _rms(x_ref[...], g_ref[...]).astype(BF16)

    def proj(c0, width):
        return _dot(h, w_ref[:, c0:c0 + width])

    c32, s32, c64, s64 = c32_ref[...], s32_ref[...], c64_ref[...], s64_ref[...]
    qa_ref[...] = (_rope(proj(C_AQ, 256), c32, s32, 16) * (DIFF_HEAD_DIM ** -0.5 * LOG2E)).astype(BF16)
    kat_ref[0] = _rope(proj(C_AK, 256), c32, s32, 16).T.astype(BF16)
    va_ref[...] = proj(C_AV, 256).astype(BF16)
    hb_ref[...] = proj(C_HB, 1024)
    qd_ref[...] = (_rope(proj(C_DQ, 256), c64, s64, 32) * (DSA_HEAD_DIM ** -0.5 * LOG2E)).astype(BF16)
    iq_ref[...] = _rope(proj(C_IQ, 128), c32[:, :128], s32[:, :128], 16).astype(BF16)
    lane = lax.broadcasted_iota(jnp.int32, (tm, 128), 1)
    is_k = lane < DSA_HEAD_DIM
    dkv = _rope(proj(C_DKV, 128), jnp.where(is_k, c64[:, :128], 1.0), jnp.where(is_k, s64[:, :128], 0.0), 32)
    is_ik = lane < DSA_IDX_DIM
    ikw = _rope(proj(C_IKW, 128), jnp.where(is_ik, c32[:, :128], 1.0), jnp.where(is_ik, s32[:, :128], 0.0), 16)
    dkt_ref[0] = jnp.concatenate([dkv.T[:64, :], ikw.T[:64, :]], axis=0).astype(BF16)
    dkv_ref[...] = dkv.astype(BF16)
    iw_ref[...] = ikw * (DSA_IDX_HEADS ** -0.5 * DSA_IDX_DIM ** -0.5)
    uv = _gelu_tanh(proj(C_UV, 512))
    u, v = uv[:, :256], uv[:, 256:]
    mu = jnp.mean(v, axis=-1, keepdims=True)
    vc = v - mu
    vn = (vc * lax.rsqrt(jnp.mean(vc * vc, axis=-1, keepdims=True) + NORM_EPS)).astype(BF16)
    r = lax.broadcasted_iota(jnp.int32, (SGU_GROUPS * SGU_CHUNK, SGU_CHUNK), 0)
    c = lax.broadcasted_iota(jnp.int32, (SGU_GROUPS * SGU_CHUNK, SGU_CHUNK), 1)
    wt = jnp.where((r % SGU_CHUNK) >= c, sw_ref[...], 0.0).astype(BF16)
    lane_grp = lax.broadcasted_iota(jnp.int32, (SGU_CHUNK, 256), 1) // SGU_GROUP_DIM
    for ch in range(tm // SGU_CHUNK):
        r0 = ch * SGU_CHUNK
        full = _dot(wt, vn[r0:r0 + SGU_CHUNK, :])
        mixed = sb_ref[...]
        for g in range(SGU_GROUPS):
            mixed = mixed + jnp.where(lane_grp == g, full[g * SGU_CHUNK:(g + 1) * SGU_CHUNK, :], 0.0)
        yc_ref[r0:r0 + SGU_CHUNK, :] = (u[r0:r0 + SGU_CHUNK, :] * mixed).astype(BF16)


def _projection(x, gain, w1, tabs, sw, sb, batch, seq):
    n, d = x.shape
    tm = PROJ_TM
    spt = seq // tm
    tok = lambda w: pl.BlockSpec((tm, w), lambda i: (i, 0))
    tr = lambda rows: pl.BlockSpec((1, rows, tm), lambda i: (i // spt, 0, i % spt))
    out_shape = (
        jax.ShapeDtypeStruct((n, 256), BF16),
        jax.ShapeDtypeStruct((batch, 256, seq), BF16),
        jax.ShapeDtypeStruct((n, 256), BF16),
        jax.ShapeDtypeStruct((n, 1024), F32),
        jax.ShapeDtypeStruct((n, 256), BF16),
        jax.ShapeDtypeStruct((n, 256), BF16),
        jax.ShapeDtypeStruct((n, 128), BF16),
        jax.ShapeDtypeStruct((batch, 128, seq), BF16),
        jax.ShapeDtypeStruct((n, 128), BF16),
        jax.ShapeDtypeStruct((n, 128), F32),
    )
    return pl.pallas_call(
        functools.partial(_proj_kernel, tm=tm),
        out_shape=out_shape,
        grid=(n // tm,),
        in_specs=[tok(d), _const_spec((1, d)), _const_spec((d, C_TOTAL)),
                  tok(256), tok(256), tok(256), tok(256),
                  _const_spec((SGU_GROUPS * SGU_CHUNK, SGU_CHUNK)), _const_spec((SGU_CHUNK, 256))],
        out_specs=(tok(256), tr(256), tok(256), tok(1024), tok(256), tok(256), tok(128), tr(128), tok(128), tok(128)),
        compiler_params=_params(("parallel",)),
        name="projection",
    )(x, gain, w1, *tabs, sw, sb)


def _diff_attn_kernel(lam_ref, q_ref, kt_ref, v_ref, o_ref, *, lam_init, tq, tk):
    q0 = pl.program_id(1) * tq
    kb_diag = q0 // tk
    lv = lam_ref[...]
    lam = (jnp.exp(jnp.sum(lv[0:1] * lv[1:2], axis=-1, keepdims=True))
           - jnp.exp(jnp.sum(lv[2:3] * lv[3:4], axis=-1, keepdims=True)) + lam_init)
    q = q_ref[...]
    rows = q0 + lax.broadcasted_iota(jnp.int32, (tq, tk), 0)
    cols = lax.broadcasted_iota(jnp.int32, (tq, tk), 1)
    n_maps = 2 * DIFF_HEADS
    qs = [q[:, i * DIFF_HEAD_DIM:(i + 1) * DIFF_HEAD_DIM] for i in range(n_maps)]

    def step(kb, carry, masked):
        k0 = pl.multiple_of(kb * tk, tk)
        keep = (k0 + cols <= rows) if masked else None
        out = []
        for i in range(n_maps):
            m_i, l_i, acc = carry[i]
            pair0 = (i // 4) * 128
            s = _dot(qs[i], kt_ref[0, i * DIFF_HEAD_DIM:(i + 1) * DIFF_HEAD_DIM, pl.ds(k0, tk)])
            if masked:
                s = jnp.where(keep, s, NEG_BIG)
            m_new = jnp.maximum(m_i, jnp.max(s, axis=-1, keepdims=True))
            p = jnp.exp2(s - m_new)
            alpha = jnp.exp2(m_i - m_new)
            l_new = alpha * l_i + jnp.sum(p, axis=-1, keepdims=True)
            pv = _dot(p.astype(BF16), v_ref[pl.ds(k0, tk), pair0:pair0 + 128])
            out.append((m_new, l_new, alpha * acc + pv))
        return tuple(out)

    init = tuple((jnp.full((tq, 1), NEG_BIG, F32), jnp.zeros((tq, 1), F32), jnp.zeros((tq, 128), F32))
                 for _ in range(n_maps))
    carry = lax.fori_loop(0, kb_diag, functools.partial(step, masked=False), init)
    carry = step(kb_diag, carry, True)
    heads = []
    for hd in range(DIFF_HEADS):
        o0 = carry[2 * hd][2] / carry[2 * hd][1]
        o1 = carry[2 * hd + 1][2] / carry[2 * hd + 1][1]
        o_pair = o0 - lam * o1
        o_h = o_pair[:, (hd % 2) * 64:(hd % 2) * 64 + 64]
        heads.append(_rms(o_h) * (1.0 - lam_init))
    o_ref[...] = jnp.concatenate(heads, axis=-1).astype(BF16)


def _diff_attention(lam_vec, qa, kat, va, lam_init, batch, seq):
    tq, tk = DIFF_TQ, DIFF_TK
    nq = seq // tq
    return pl.pallas_call(
        functools.partial(_diff_attn_kernel, lam_init=lam_init, tq=tq, tk=tk),
        out_shape=jax.ShapeDtypeStruct((batch * seq, 256), BF16),
        grid=(batch, nq),
        in_specs=[_const_spec((4, DIFF_HEAD_DIM)),
                  pl.BlockSpec((tq, 256), lambda b, i: (b * nq + i, 0)),
                  pl.BlockSpec((1, 256, seq), lambda b, i: (b, 0, 0)),
                  pl.BlockSpec((seq, 256), lambda b, i: (b, 0))],
        out_specs=pl.BlockSpec((tq, 256), lambda b, i: (b * nq + i, 0)),
        compiler_params=_params(("parallel", "parallel")),
        name="diff_attention",
    )(lam_vec, qa, kat, va)


def _hgrn_kernel(lbl_ref, hb_ref, o_ref, st_ref, pstk_ref, *, layer, tc):
    cz = HGRN_CHUNK
    w = 256

    @pl.when(pl.program_id(1) == 0)
    def _():
        st_ref[...] = jnp.zeros_like(st_ref)

    lg = lbl_ref[...]
    e = jnp.exp(lg - jnp.max(lg, axis=0, keepdims=True))
    lw = e / jnp.sum(e, axis=0, keepdims=True)
    lb = jnp.sum(lw[0:layer + 1], axis=0, keepdims=True) - lw[0:1]

    ri = lax.broadcasted_iota(jnp.int32, (cz, cz), 0)
    ci = lax.broadcasted_iota(jnp.int32, (cz, cz), 1)
    tri = (ri >= ci).astype(F32)
    rb = lax.broadcasted_iota(jnp.int32, (w, w), 0) // HGRN_DIM
    cb = lax.broadcasted_iota(jnp.int32, (w, w), 1) // HGRN_DIM
    same_head = rb == cb
    head_ones = same_head.astype(BF16)
    trows = {r: r + lax.broadcasted_iota(jnp.int32, (cz - r, w), 0) for r in range(0, cz, 16)}

    def chunk(c, carry):
        r0 = pl.multiple_of(c * cz, cz)
        q = hb_ref[pl.ds(r0, cz), 0:256]
        fp = hb_ref[pl.ds(r0, cz), 256:512]
        v = hb_ref[pl.ds(r0, cz), 512:768]
        g = hb_ref[pl.ds(r0, cz), 768:1024]
        qf = q * jax.nn.sigmoid(q)
        f = lb + (1.0 - lb) * jax.nn.sigmoid(fp)
        log_f = jnp.log(jnp.maximum(f, HGRN_MIN_FORGET))
        kf = (1.0 - lb) * jax.nn.sigmoid(-fp)
        bc = jnp.dot(tri, log_f, preferred_element_type=F32, precision=lax.Precision.HIGHEST)
        st = st_ref[...]
        o = _dot_nt((qf * jnp.exp(bc)).astype(BF16), st.astype(BF16))
        for s in range(cz):
            r_lo = (s // 16) * 16
            arg = bc[r_lo:, :] - bc[s:s + 1, :]
            if s > r_lo:
                arg = jnp.where(trows[r_lo] >= s, arg, NEG_BIG)
            p = qf[r_lo:, :] * kf[s:s + 1, :] * jnp.exp(arg)
            if r_lo:
                pstk_ref[s * cz:s * cz + r_lo, :] = jnp.zeros((r_lo, w), BF16)
            pstk_ref[s * cz + r_lo:(s + 1) * cz, :] = p.astype(BF16)
        accs = [jnp.zeros((16, w), F32) for _ in range(cz // 16)]
        for sg in range(cz // 16):
            att = _dot(pstk_ref[sg * 16 * cz:(sg + 1) * 16 * cz, :], head_ones)
            for sl in range(16):
                s = sg * 16 + sl
                for j in range(sg, cz // 16):
                    accs[j] = accs[j] + att[sl * cz + 16 * j:sl * cz + 16 * j + 16, :] * v[s:s + 1, :]
        o = o + jnp.concatenate(accs, axis=0)
        b_end = bc[cz - 1:cz, :]
        kd = kf * jnp.exp(b_end - bc)
        upd = _dot(v.T.astype(BF16), kd.astype(BF16))
        st_ref[...] = st * jnp.exp(b_end) + jnp.where(same_head, upd, 0.0)
        ms = _dot(o * o, head_ones.astype(F32)) * (1.0 / HGRN_DIM)
        y = o * lax.rsqrt(ms + NORM_EPS)
        o_ref[pl.ds(r0, cz), :] = (y * (g * jax.nn.sigmoid(g))).astype(BF16)
        return carry

    lax.fori_loop(0, tc // cz, chunk, 0)


def _hgrn(lb_logits, hb, layer, batch, seq):
    tc = HGRN_TC
    nt = seq // tc
    cz = HGRN_CHUNK
    return pl.pallas_call(
        functools.partial(_hgrn_kernel, layer=layer, tc=tc),
        out_shape=jax.ShapeDtypeStruct((batch * seq, 256), BF16),
        grid=(batch, nt),
        in_specs=[_const_spec(lb_logits.shape),
                  pl.BlockSpec((tc, 1024), lambda b, i: (b * nt + i, 0))],
        out_specs=pl.BlockSpec((tc, 256), lambda b, i: (b * nt + i, 0)),
        scratch_shapes=[pltpu.VMEM((256, 256), F32), pltpu.VMEM((cz * cz, 256), BF16)],
        compiler_params=_params(("parallel", "arbitrary")),
        name="hgrn2",
    )(lb_logits, hb)


def _dsa_kernel(qd_ref, iq_ref, iw_ref, dkt_ref, dkv_ref, o_ref, key_ref, bias_ref, *, tq, tk, n_sel):
    q0 = pl.program_id(1) * tq
    nkb = q0 // tk + 1
    rows = q0 + lax.broadcasted_iota(jnp.int32, (tq, tk), 0)
    cols = lax.broadcasted_iota(jnp.int32, (tq, tk), 1)
    iq = iq_ref[...]
    iw = iw_ref[:, IW_LANE:IW_LANE + DSA_IDX_HEADS]
    lane_chunks = tk // LANES
    ones_lanes = jnp.ones((LANES, LANES), BF16)
    ones_block = jnp.ones((tk, LANES), BF16)
    tile = lambda x: jnp.concatenate([x] * lane_chunks, axis=1)

    def score_block(kb, carry):
        k0 = pl.multiple_of(kb * tk, tk)
        ikt = dkt_ref[0, 64:64 + DSA_IDX_DIM, pl.ds(k0, tk)]
        sc = jnp.zeros((tq, tk), F32)
        for hd in range(DSA_IDX_HEADS):
            sh = _dot(iq[:, hd * DSA_IDX_DIM:(hd + 1) * DSA_IDX_DIM], ikt)
            sc = sc + jnp.maximum(sh, 0.0) * iw[:, hd:hd + 1]
        sc = jnp.where(k0 + cols <= rows, sc + 0.0, -jnp.inf)
        bits = pltpu.bitcast(sc, jnp.int32)
        key_ref[:, pl.ds(k0, tk)] = jnp.where(bits < 0, bits ^ jnp.int32(0x7FFFFFFF), bits)
        return carry

    lax.fori_loop(0, nkb, score_block, 0)

    def count(pred_fn):
        def body(kb, acc):
            k0 = pl.multiple_of(kb * tk, tk)
            blk = key_ref[:, pl.ds(k0, tk)]
            for j in range(lane_chunks):
                acc = acc + jnp.where(pred_fn(blk[:, j * LANES:(j + 1) * LANES]), 1.0, 0.0)
            return acc
        acc = lax.fori_loop(0, nkb, body, jnp.zeros((tq, LANES), F32))
        return _dot(acc.astype(BF16), ones_lanes)

    thr = jnp.full((tq, LANES), np.int32(-2 ** 31), jnp.int32)
    for bit in range(31, -1, -1):
        trial = thr + np.int32(-2 ** 31 if bit == 31 else 2 ** bit)
        cnt = count(lambda blk, trial=trial: blk >= trial)
        thr = jnp.where(cnt >= n_sel, trial, thr)

    need = tile(n_sel - count(lambda blk: blk > thr))
    thr_t = tile(thr)
    ur = lax.broadcasted_iota(jnp.int32, (tk, tk), 0)
    uc = lax.broadcasted_iota(jnp.int32, (tk, tk), 1)
    before = (ur < uc).astype(BF16)

    def select_block(kb, seen):
        k0 = pl.multiple_of(kb * tk, tk)
        blk = key_ref[:, pl.ds(k0, tk)]
        eq = blk == thr_t
        eqb = jnp.where(eq, 1.0, 0.0).astype(BF16)
        rank = tile(seen) + _dot(eqb, before)
        sel = (blk > thr_t) | (eq & (rank < need))
        sel = sel & (k0 + cols <= rows)
        bias_ref[:, pl.ds(k0, tk)] = jnp.where(sel, 0.0, NEG_BIG)
        return seen + _dot(eqb, ones_block)

    lax.fori_loop(0, nkb, select_block, jnp.zeros((tq, LANES), F32))

    qd = qd_ref[...]
    qs = [qd[:, hd * DSA_HEAD_DIM:(hd + 1) * DSA_HEAD_DIM] for hd in range(DSA_HEADS)]

    def att_block(kb, carry):
        k0 = pl.multiple_of(kb * tk, tk)
        kt = dkt_ref[0, 0:DSA_HEAD_DIM, pl.ds(k0, tk)]
        kv = dkv_ref[pl.ds(k0, tk), :]
        bias = bias_ref[:, pl.ds(k0, tk)]
        out = []
        for hd in range(DSA_HEADS):
            m_i, l_i, acc = carry[hd]
            s = _dot(qs[hd], kt) + bias
            m_new = jnp.maximum(m_i, jnp.max(s, axis=-1, keepdims=True))
            p = jnp.exp2(s - m_new)
            alpha = jnp.exp2(m_i - m_new)
            l_new = alpha * l_i + jnp.sum(p, axis=-1, keepdims=True)
            out.append((m_new, l_new, alpha * acc + _dot(p.astype(BF16), kv)))
        return tuple(out)

    init = tuple((jnp.full((tq, 1), NEG_BIG, F32), jnp.zeros((tq, 1), F32), jnp.zeros((tq, 128), F32))
                 for _ in range(DSA_HEADS))
    carry = lax.fori_loop(0, nkb, att_block, init)
    outs = [(acc / l_f)[:, DSA_HEAD_DIM:] for _, l_f, acc in carry]
    o_ref[...] = jnp.concatenate(outs, axis=-1).astype(BF16)


def _dsa(qd, iq, iwq, dkt, dkv, batch, seq):
    tq, tk = DSA_TQ, DSA_TK
    nq = seq // tq
    n_sel = min(DSA_TOPK, seq // 4)
    return pl.pallas_call(
        functools.partial(_dsa_kernel, tq=tq, tk=tk, n_sel=n_sel),
        out_shape=jax.ShapeDtypeStruct((batch * seq, 256), BF16),
        grid=(batch, nq),
        in_specs=[pl.BlockSpec((tq, 256), lambda b, i: (b * nq + i, 0)),
                  pl.BlockSpec((tq, 128), lambda b, i: (b * nq + i, 0)),
                  pl.BlockSpec((tq, 128), lambda b, i: (b * nq + i, 0)),
                  pl.BlockSpec((1, 128, seq), lambda b, i: (b, 0, 0)),
                  pl.BlockSpec((seq, 128), lambda b, i: (b, 0))],
        out_specs=pl.BlockSpec((tq, 256), lambda b, i: (b * nq + i, 0)),
        scratch_shapes=[pltpu.VMEM((tq, seq), jnp.int32), pltpu.VMEM((tq, seq), F32)],
        compiler_params=_params(("parallel", "parallel")),
        name="dsa",
    )(qd, iq, iwq, dkt, dkv)


def _mem_kv_kernel(mem_ref, g_ref, w_ref, kt_ref, v_ref):
    mn = _rms(mem_ref[0], g_ref[...]).astype(BF16)
    kv = _dot(mn, w_ref[...])
    kt_ref[0] = kv[:, :256].T.astype(BF16)
    v_ref[0] = kv[:, 256:].astype(BF16)


def _mem_kv(mem, gain, w_kv):
    b, m, d = mem.shape
    return pl.pallas_call(
        _mem_kv_kernel,
        out_shape=(jax.ShapeDtypeStruct((b, 256, m), BF16), jax.ShapeDtypeStruct((b, m, 256), BF16)),
        grid=(b,),
        in_specs=[pl.BlockSpec((1, m, d), lambda i: (i, 0, 0)), _const_spec((1, d)), _const_spec((d, 512))],
        out_specs=(pl.BlockSpec((1, 256, m), lambda i: (i, 0, 0)), pl.BlockSpec((1, m, 256), lambda i: (i, 0, 0))),
        compiler_params=_params(("parallel",)),
        name="mem_kv",
    )(mem, gain, w_kv)


def _merge_kernel(x_ref, ya_ref, yb_ref, yc_ref, yd_ref, gmix_ref, wg_ref, wbr_ref, wout_ref,
                  gq_ref, wq_ref, mkt_ref, mv_ref, wo_ref, gffn_ref, wr_ref, br_ref,
                  x2_ref, h3_ref, cw_ref, *, tm):
    x = x_ref[...]
    d = x.shape[-1]
    h = _rms(x, gmix_ref[...]).astype(BF16)
    merged = jnp.zeros((tm, d), F32)
    for n, y_ref in enumerate((ya_ref, yb_ref, yc_ref, yd_ref)):
        gate = jax.nn.sigmoid(_dot(h, wg_ref[:, n * d:(n + 1) * d]))
        merged = merged + gate * _dot(y_ref[...], wbr_ref[n])
    x1 = x + _dot(merged.astype(BF16), wout_ref[...])
    h2 = _rms(x1, gq_ref[...]).astype(BF16)
    q = (_dot(h2, wq_ref[...]) * (MEM_HEAD_DIM ** -0.5)).astype(BF16)
    lane_head = lax.broadcasted_iota(jnp.int32, (tm, 256), 1) // MEM_HEAD_DIM
    mv = mv_ref[0]
    o = jnp.zeros((tm, 256), F32)
    for hd in range(MEM_HEADS):
        s = _dot(q[:, hd * MEM_HEAD_DIM:(hd + 1) * MEM_HEAD_DIM], mkt_ref[0, hd * MEM_HEAD_DIM:(hd + 1) * MEM_HEAD_DIM, :])
        p = jnp.exp(s - jnp.max(s, axis=-1, keepdims=True))
        p = p / jnp.sum(p, axis=-1, keepdims=True)
        o = o + jnp.where(lane_head == hd, _dot(p.astype(BF16), mv), 0.0)
    x2 = x1 + _dot(o.astype(BF16), wo_ref[...])
    x2_ref[...] = x2
    h3 = _rms(x2, gffn_ref[...])
    h3_ref[...] = h3.astype(BF16)
    logits = jnp.dot(h3, wr_ref[...], preferred_element_type=F32, precision=lax.Precision.HIGHEST) + br_ref[...]
    lane = lax.broadcasted_iota(jnp.int32, (tm, LANES), 1)
    gl = jnp.where(lane < MOE_GROUPS, logits[:, :LANES], -jnp.inf)
    gmax = jnp.max(gl, axis=-1, keepdims=True)
    gsel = jnp.min(jnp.where(gl == gmax, lane, LANES), axis=-1, keepdims=True)
    pg_sel = 1.0 / jnp.sum(jnp.exp(gl - gmax), axis=-1, keepdims=True)
    el = jnp.where(lane // MOE_EXPERTS_PER_GROUP == gsel, logits[:, LANES:], -jnp.inf)
    m1 = jnp.max(el, axis=-1, keepdims=True)
    i1 = jnp.min(jnp.where(el == m1, lane, LANES), axis=-1, keepdims=True)
    el2 = jnp.where(lane == i1, -jnp.inf, el)
    m2 = jnp.max(el2, axis=-1, keepdims=True)
    i2 = jnp.min(jnp.where(el2 == m2, lane, LANES), axis=-1, keepdims=True)
    e21 = jnp.exp(m2 - m1)
    c1 = pg_sel / (1.0 + e21)
    cw_ref[...] = jnp.where(lane == i1, c1, jnp.where(lane == i2, c1 * e21, 0.0))


def _merge(x, ys, gmix, wg, wbr, wout, gq, wq, mkt, mv, wo, gffn, wr, br, batch, seq):
    n, d = x.shape
    tm = MERGE_TM
    spt = seq // tm
    m = mv.shape[1]
    tok = lambda w: pl.BlockSpec((tm, w), lambda i: (i, 0))
    return pl.pallas_call(
        functools.partial(_merge_kernel, tm=tm),
        out_shape=(jax.ShapeDtypeStruct((n, d), F32), jax.ShapeDtypeStruct((n, d), BF16),
                   jax.ShapeDtypeStruct((n, LANES), F32)),
        grid=(n // tm,),
        in_specs=[tok(d), tok(256), tok(256), tok(256), tok(256),
                  _const_spec((1, d)), _const_spec((d, N_BRANCH * d)), _const_spec((N_BRANCH, 256, d)),
                  _const_spec((d, d)), _const_spec((1, d)), _const_spec((d, 256)),
                  pl.BlockSpec((1, 256, m), lambda i: (i // spt, 0, 0)),
                  pl.BlockSpec((1, m, 256), lambda i: (i // spt, 0, 0)),
                  _const_spec((256, d)), _const_spec((1, d)), _const_spec((d, 2 * LANES)), _const_spec((1, 2 * LANES))],
        out_specs=(tok(d), tok(d), tok(LANES)),
        compiler_params=_params(("parallel",)),
        name="merge_mem_router",
    )(x, *ys, gmix, wg, wbr, wout, gq, wq, mkt, mv, wo, gffn, wr, br)


def _moe_kernel(h_ref, cw_ref, x_ref, wg_ref, wu_ref, wd_ref, gfin_ref, o_ref, acc_ref, *, tm, final_norm):
    e = pl.program_id(1)

    @pl.when(e == 0)
    def _():
        acc_ref[...] = jnp.zeros_like(acc_ref)

    h = h_ref[...]
    lane = lax.broadcasted_iota(jnp.int32, (tm, LANES), 1)
    c = jnp.sum(jnp.where(lane == e, cw_ref[...], 0.0), axis=-1, keepdims=True)
    gt = _dot(h, wg_ref[0])
    hid = gt * jax.nn.sigmoid(gt) * _dot(h, wu_ref[0]) * c
    acc_ref[...] += _dot(hid.astype(BF16), wd_ref[0])

    @pl.when(e == pl.num_programs(1) - 1)
    def _():
        y = x_ref[...] + acc_ref[...]
        o_ref[...] = _rms(y, gfin_ref[...]) if final_norm else y


def _moe(h3, cw, x2, wg, wu, wd, gfin, final_norm):
    n, d = x2.shape
    tm = MOE_TM
    ne, _, hid = wg.shape
    return pl.pallas_call(
        functools.partial(_moe_kernel, tm=tm, final_norm=final_norm),
        out_shape=jax.ShapeDtypeStruct((n, d), F32),
        grid=(n // tm, ne),
        in_specs=[pl.BlockSpec((tm, d), lambda i, e: (i, 0)),
                  pl.BlockSpec((tm, LANES), lambda i, e: (i, 0)),
                  pl.BlockSpec((tm, d), lambda i, e: (i, 0)),
                  pl.BlockSpec((1, d, hid), lambda i, e: (e, 0, 0)),
                  pl.BlockSpec((1, d, hid), lambda i, e: (e, 0, 0)),
                  pl.BlockSpec((1, hid, d), lambda i, e: (e, 0, 0)),
                  _const_spec((1, d))],
        out_specs=pl.BlockSpec((tm, d), lambda i, e: (i, 0)),
        scratch_shapes=[pltpu.VMEM((tm, d), F32)],
        compiler_params=_params(("parallel", "arbitrary")),
        name="moe_experts",
    )(h3, cw, x2, wg, wu, wd, gfin)


def kernel(x, mem, positions, norm_mix, w_in, diff_lambda, hgrn_lb_logits, spatial_w, spatial_b, w_branch, w_out,
           norm_mem_q, norm_mem_kv, w_mem_q, w_mem_kv, w_mem_o, norm_ffn, w_router_group, b_router_group,
           w_router_expert, b_router_expert, w_exp_gate, w_exp_up, w_exp_down, norm_final):
    batch, seq, d = x.shape
    depth = w_in.shape[0]
    n = batch * seq
    xf = x.reshape(n, d)
    tabs = _rope_tables(positions)
    row = lambda v: v.reshape(1, -1).astype(F32)
    for l in range(depth):
        lam_init = 0.8 - 0.6 * math.exp(-0.3 * l)
        w1 = w_in[l][:, :C_TOTAL].astype(BF16)
        w_gate = w_in[l][:, C_GATE:].astype(BF16)
        sw = spatial_w[l].reshape(SGU_GROUPS * SGU_CHUNK, SGU_CHUNK)
        sb = jnp.repeat(spatial_b[l].T, SGU_GROUP_DIM, axis=1)
        qa, kat, va, hb, y_c, qd, iq, dkt, dkv, iwq = _projection(xf, row(norm_mix[l]), w1, tabs, sw, sb, batch, seq)
        y_a = _diff_attention(diff_lambda[l], qa, kat, va, lam_init, batch, seq)
        y_b = _hgrn(hgrn_lb_logits, hb, l, batch, seq)
        y_d = _dsa(qd, iq, iwq, dkt, dkv, batch, seq)
        mkt, mv = _mem_kv(mem, row(norm_mem_kv[l]), w_mem_kv[l].astype(BF16))
        wr = jnp.zeros((d, 2 * LANES), F32)
        wr = wr.at[:, :MOE_GROUPS].set(w_router_group[l]).at[:, LANES:LANES + MOE_N_EXPERTS].set(w_router_expert[l])
        br = jnp.zeros((1, 2 * LANES), F32)
        br = br.at[0, :MOE_GROUPS].set(b_router_group[l]).at[0, LANES:LANES + MOE_N_EXPERTS].set(b_router_expert[l])
        x2, h3, cw = _merge(xf, (y_a, y_b, y_c, y_d), row(norm_mix[l]), w_gate, w_branch[l].astype(BF16),
                            w_out[l].astype(BF16), row(norm_mem_q[l]), w_mem_q[l].astype(BF16), mkt, mv,
                            w_mem_o[l].astype(BF16), row(norm_ffn[l]), wr, br, batch, seq)
        xf = _moe(h3, cw, x2, w_exp_gate[l].astype(BF16), w_exp_up[l].astype(BF16), w_exp_down[l].astype(BF16),
                  row(norm_final), final_norm=(l == depth - 1))
    return xf.reshape(batch, seq, d)
```

```python
import functools
import math

import numpy as np
import jax
import jax.numpy as jnp
from jax import lax
from jax.experimental import pallas as pl
from jax.experimental.pallas import tpu as pltpu

F32 = jnp.float32
BF16 = jnp.bfloat16

NORM_EPS = 1e-6
ROPE_THETA = 10000.0
NEG_BIG = -1e30

N_BRANCH = 4
BRANCH_WIDTH = 256
DIFF_HEADS = 4
DIFF_HEAD_DIM = 32
HGRN_HEADS = 4
HGRN_DIM = 64
HGRN_CHUNK = 64
HGRN_MIN_FORGET = 1e-30
SGU_GROUPS = 4
SGU_GROUP_DIM = 64
SGU_CHUNK = 128
DSA_HEADS = 4
DSA_HEAD_DIM = 64
DSA_IDX_HEADS = 4
DSA_IDX_DIM = 32
DSA_TOPK = 256
MEM_HEADS = 4
MEM_HEAD_DIM = 64
MOE_GROUPS = 4
MOE_EXPERTS_PER_GROUP = 8
MOE_N_EXPERTS = 32

LANES = 128
VMEM_LIMIT = 56 * 1024 * 1024

PROJ_TM = 256
DIFF_TQ = 256
DIFF_TK = 512
HGRN_TC = 512
DSA_TQ = 256
DSA_TK = 512
MERGE_TM = 256
MOE_TM = 1024
ROPE_TM = 1024

C_AQ, C_AK, C_AV = 0, 256, 512
C_HB = 768
C_UV = 1792
C_DQ = 2304
C_DKV = 2560
C_IQ = 2688
C_IKW = 2816
IW_LANE = 32
C_GATE = 2852
C_TOTAL = 2944
LOG2E = math.log2(math.e)


def _params(sem):
    return pltpu.CompilerParams(dimension_semantics=sem, vmem_limit_bytes=VMEM_LIMIT)


def _const_spec(shape):
    nd = len(shape)
    return pl.BlockSpec(shape, lambda *_: (0,) * nd, pipeline_mode=pl.Buffered(1))


def _rms(xf, gain=None):
    y = xf * lax.rsqrt(jnp.mean(xf * xf, axis=-1, keepdims=True) + NORM_EPS)
    return y if gain is None else y * gain


def _dot(a, b):
    return jnp.dot(a, b, preferred_element_type=F32)


def _dot_nt(a, b):
    return lax.dot_general(a, b, (((1,), (1,)), ((), ())), preferred_element_type=F32)


def _rope_table_kernel(pos_ref, frq_ref, sgn_ref, c32_ref, s32_ref, c64_ref, s64_ref):
    pos = pos_ref[...].astype(F32)
    a32 = pos * frq_ref[0:1, :]
    a64 = pos * frq_ref[1:2, :]
    c32_ref[...] = jnp.cos(a32)
    s32_ref[...] = jnp.sin(a32) * sgn_ref[0:1, :]
    c64_ref[...] = jnp.cos(a64)
    s64_ref[...] = jnp.sin(a64) * sgn_ref[1:2, :]


def _rope_tables(positions):
    n = positions.size
    pos = positions.reshape(n, 1).astype(jnp.int32)
    lane = np.arange(256)
    inv32 = ROPE_THETA ** (-jnp.arange(16, dtype=F32) * (2.0 / 32))
    inv64 = ROPE_THETA ** (-jnp.arange(32, dtype=F32) * (2.0 / 64))
    frq = jnp.stack([inv32[lane % 16], inv64[lane % 32]])
    sgn = jnp.asarray(np.stack([np.where(lane % 32 < 16, -1.0, 1.0),
                                np.where(lane % 64 < 32, -1.0, 1.0)]), F32)
    tm = ROPE_TM
    tab = jax.ShapeDtypeStruct((n, 256), F32)
    return pl.pallas_call(
        _rope_table_kernel,
        out_shape=(tab, tab, tab, tab),
        grid=(n // tm,),
        in_specs=[pl.BlockSpec((tm, 1), lambda i: (i, 0)), _const_spec((2, 256)), _const_spec((2, 256))],
        out_specs=tuple(pl.BlockSpec((tm, 256), lambda i: (i, 0)) for _ in range(4)),
        compiler_params=_params(("parallel",)),
        name="rope_tables",
    )(pos, frq, sgn)


def _split_w_in_kernel(w_ref, w1_ref, wg_ref):
    w1_ref[...] = w_ref[:, :C_TOTAL].astype(BF16)
    a0 = (C_GATE // LANES) * LANES
    tail = w_ref[:, a0:]
    wg_ref[...] = tail[:, C_GATE - a0:].astype(BF16)


def _split_w_in(w_in, layer):
    _, d, width = w_in.shape
    tr = 128
    return pl.pallas_call(
        _split_w_in_kernel,
        out_shape=(jax.ShapeDtypeStruct((d, C_TOTAL), BF16), jax.ShapeDtypeStruct((d, width - C_GATE), BF16)),
        grid=(d // tr,),
        in_specs=[pl.BlockSpec((None, tr, width), lambda i: (layer, i, 0))],
        out_specs=(pl.BlockSpec((tr, C_TOTAL), lambda i: (i, 0)), pl.BlockSpec((tr, width - C_GATE), lambda i: (i, 0))),
        compiler_params=_params(("parallel",)),
        name="split_w_in",
    )(w_in)


def _gelu_tanh(x):
    return 0.5 * x * (1.0 + jnp.tanh(math.sqrt(2.0 / math.pi) * (x + 0.044715 * (x * x * x))))


def _rope(x, cos, sin_signed, half):
    w = x.shape[-1]
    lane = lax.broadcasted_iota(jnp.int32, x.shape, 1)
    partner = jnp.where(lane % (2 * half) < half, pltpu.roll(x, w - half, 1), pltpu.roll(x, half, 1))
    return x * cos + partner * sin_signed


def _proj_kernel(x_ref, g_ref, w_ref, c32_ref, s32_ref, c64_ref, s64_ref, sw_ref, sb_ref,
                 qat_ref, ka_ref, vat_ref, hb_ref, yc_ref, qdt_ref, iqt_ref, dkv_ref, dkvt_ref, ikw_ref, iwt_ref,
                 *, tm):
    h = _rms(x_ref[...], g_ref[...]).astype(BF16)

    def proj(c0, width):
        return _dot(h, w_ref[:, c0:c0 + width])

    c32, s32, c64, s64 = c32_ref[...], s32_ref[...], c64_ref[...], s64_ref[...]
    qat_ref[0] = (_rope(proj(C_AQ, 256), c32, s32, 16) * (DIFF_HEAD_DIM ** -0.5 * LOG2E)).T.astype(BF16)
    ka_ref[...] = _rope(proj(C_AK, 256), c32, s32, 16).astype(BF16)
    vat_ref[0] = proj(C_AV, 256).astype(BF16).T
    hb_ref[...] = proj(C_HB, 1024)
    qdt_ref[0] = (_rope(proj(C_DQ, 256), c64, s64, 32) * (DSA_HEAD_DIM ** -0.5 * LOG2E)).T.astype(BF16)
    iqt_ref[0] = _rope(proj(C_IQ, 128), c32[:, :128], s32[:, :128], 16).T.astype(BF16)
    lane = lax.broadcasted_iota(jnp.int32, (tm, 128), 1)
    is_k = lane < DSA_HEAD_DIM
    dkv = _rope(proj(C_DKV, 128), jnp.where(is_k, c64[:, :128], 1.0), jnp.where(is_k, s64[:, :128], 0.0), 32)
    is_ik = lane < DSA_IDX_DIM
    ikw = _rope(proj(C_IKW, 128), jnp.where(is_ik, c32[:, :128], 1.0), jnp.where(is_ik, s32[:, :128], 0.0), 16)
    dkv_ref[...] = dkv.astype(BF16)
    dkvt_ref[0] = dkv.T.astype(BF16)
    ikw_ref[...] = ikw.astype(BF16)
    iw_scale = DSA_IDX_HEADS ** -0.5 * DSA_IDX_DIM ** -0.5
    iwt_ref[0] = (ikw * iw_scale).T[IW_LANE:IW_LANE + 8, :]
    uv = _gelu_tanh(proj(C_UV, 512))
    u, v = uv[:, :256], uv[:, 256:]
    mu = jnp.mean(v, axis=-1, keepdims=True)
    vc = v - mu
    vn = (vc * lax.rsqrt(jnp.mean(vc * vc, axis=-1, keepdims=True) + NORM_EPS)).astype(BF16)
    r = lax.broadcasted_iota(jnp.int32, (SGU_GROUPS * SGU_CHUNK, SGU_CHUNK), 0)
    c = lax.broadcasted_iota(jnp.int32, (SGU_GROUPS * SGU_CHUNK, SGU_CHUNK), 1)
    wt = jnp.where((r % SGU_CHUNK) >= c, sw_ref[...], 0.0).astype(BF16)
    lane_grp = lax.broadcasted_iota(jnp.int32, (SGU_CHUNK, 256), 1) // SGU_GROUP_DIM
    for ch in range(tm // SGU_CHUNK):
        r0 = ch * SGU_CHUNK
        full = _dot(wt, vn[r0:r0 + SGU_CHUNK, :])
        mixed = sb_ref[...]
        for g in range(SGU_GROUPS):
            mixed = mixed + jnp.where(lane_grp == g, full[g * SGU_CHUNK:(g + 1) * SGU_CHUNK, :], 0.0)
        yc_ref[r0:r0 + SGU_CHUNK, :] = (u[r0:r0 + SGU_CHUNK, :] * mixed).astype(BF16)


def _projection(x, gain, w1, tabs, sw, sb, batch, seq):
    n, d = x.shape
    tm = PROJ_TM
    spt = seq // tm
    tok = lambda w: pl.BlockSpec((tm, w), lambda i: (i, 0))
    tr = lambda rows: pl.BlockSpec((1, rows, tm), lambda i: (i // spt, 0, i % spt))
    out_shape = (
        jax.ShapeDtypeStruct((batch, 256, seq), BF16),
        jax.ShapeDtypeStruct((n, 256), BF16),
        jax.ShapeDtypeStruct((batch, 256, seq), BF16),
        jax.ShapeDtypeStruct((n, 1024), F32),
        jax.ShapeDtypeStruct((n, 256), BF16),
        jax.ShapeDtypeStruct((batch, 256, seq), BF16),
        jax.ShapeDtypeStruct((batch, 128, seq), BF16),
        jax.ShapeDtypeStruct((n, 128), BF16),
        jax.ShapeDtypeStruct((batch, 128, seq), BF16),
        jax.ShapeDtypeStruct((n, 128), BF16),
        jax.ShapeDtypeStruct((batch, 8, seq), F32),
    )
    return pl.pallas_call(
        functools.partial(_proj_kernel, tm=tm),
        out_shape=out_shape,
        grid=(n // tm,),
        in_specs=[tok(d), _const_spec((1, d)), _const_spec((d, C_TOTAL)),
                  tok(256), tok(256), tok(256), tok(256),
                  _const_spec((SGU_GROUPS * SGU_CHUNK, SGU_CHUNK)), _const_spec((SGU_CHUNK, 256))],
        out_specs=(tr(256), tok(256), tr(256), tok(1024), tok(256), tr(256), tr(128), tok(128), tr(128), tok(128), tr(8)),
        compiler_params=_params(("parallel",)),
        name="projection",
    )(x, gain, w1, *tabs, sw, sb)


def _diff_attn_kernel(lam_ref, qt_ref, k_ref, vt_ref, o_ref, *, lam_init, tq, tk):
    q0 = pl.program_id(1) * tq
    kb_diag = q0 // tk
    lv = lam_ref[...]
    lam = (jnp.exp(jnp.sum(lv[0:1] * lv[1:2], axis=-1, keepdims=True))
           - jnp.exp(jnp.sum(lv[2:3] * lv[3:4], axis=-1, keepdims=True)) + lam_init)
    qt = qt_ref[0]
    feat = lax.broadcasted_iota(jnp.int32, (256, tq), 0) // DIFF_HEAD_DIM
    n_maps = 2 * DIFF_HEADS
    qz = [jnp.where(feat == i, qt, jnp.zeros_like(qt)) for i in range(n_maps)]
    key_i = lax.broadcasted_iota(jnp.int32, (tk, tq), 0)
    qry_i = q0 + lax.broadcasted_iota(jnp.int32, (tk, tq), 1)

    def step(kb, carry, masked):
        k0 = pl.multiple_of(kb * tk, tk)
        kblk = k_ref[pl.ds(k0, tk), :]
        keep = (k0 + key_i <= qry_i) if masked else None
        out = []
        for i in range(n_maps):
            m_i, l_i, acc = carry[i]
            s = _dot(kblk, qz[i])
            if masked:
                s = jnp.where(keep, s, NEG_BIG)
            m_new = jnp.maximum(m_i, jnp.max(s, axis=0, keepdims=True))
            p = jnp.exp2(s - m_new)
            alpha = jnp.exp2(m_i - m_new)
            l_new = alpha * l_i + jnp.sum(p, axis=0, keepdims=True)
            hd = i // 2
            pv = _dot(vt_ref[0, hd * 64:(hd + 1) * 64, pl.ds(k0, tk)], p.astype(BF16))
            out.append((m_new, l_new, alpha * acc + pv))
        return tuple(out)

    init = tuple((jnp.full((1, tq), NEG_BIG, F32), jnp.zeros((1, tq), F32), jnp.zeros((64, tq), F32))
                 for _ in range(n_maps))
    carry = lax.fori_loop(0, kb_diag, functools.partial(step, masked=False), init)
    carry = step(kb_diag, carry, True)
    heads = []
    for hd in range(DIFF_HEADS):
        o0 = carry[2 * hd][2] / carry[2 * hd][1]
        o1 = carry[2 * hd + 1][2] / carry[2 * hd + 1][1]
        o_h = o0 - lam * o1
        ms = jnp.mean(o_h * o_h, axis=0, keepdims=True)
        heads.append(o_h * lax.rsqrt(ms + NORM_EPS) * (1.0 - lam_init))
    o_ref[...] = jnp.concatenate(heads, axis=0).T.astype(BF16)


def _diff_attention(lam_vec, qat, ka, vat, lam_init, batch, seq):
    tq, tk = DIFF_TQ, DIFF_TK
    nq = seq // tq
    return pl.pallas_call(
        functools.partial(_diff_attn_kernel, lam_init=lam_init, tq=tq, tk=tk),
        out_shape=jax.ShapeDtypeStruct((batch * seq, 256), BF16),
        grid=(batch, nq),
        in_specs=[_const_spec((4, DIFF_HEAD_DIM)),
                  pl.BlockSpec((1, 256, tq), lambda b, i: (b, 0, i)),
                  pl.BlockSpec((seq, 256), lambda b, i: (b, 0)),
                  pl.BlockSpec((1, 256, seq), lambda b, i: (b, 0, 0))],
        out_specs=pl.BlockSpec((tq, 256), lambda b, i: (b * nq + i, 0)),
        compiler_params=_params(("parallel", "parallel")),
        name="diff_attention",
    )(lam_vec, qat, ka, vat)


def _hgrn_kernel(lbl_ref, hb_ref, o_ref, st_ref, pstk_ref, *, layer, tc):
    cz = HGRN_CHUNK
    w = 256

    @pl.when(pl.program_id(1) == 0)
    def _():
        st_ref[...] = jnp.zeros_like(st_ref)

    lg = lbl_ref[...]
    e = jnp.exp(lg - jnp.max(lg, axis=0, keepdims=True))
    lw = e / jnp.sum(e, axis=0, keepdims=True)
    lb = jnp.sum(lw[0:layer + 1], axis=0, keepdims=True) - lw[0:1]

    ri = lax.broadcasted_iota(jnp.int32, (cz, cz), 0)
    ci = lax.broadcasted_iota(jnp.int32, (cz, cz), 1)
    tri = (ri >= ci).astype(F32)
    rb = lax.broadcasted_iota(jnp.int32, (w, w), 0) // HGRN_DIM
    cb = lax.broadcasted_iota(jnp.int32, (w, w), 1) // HGRN_DIM
    same_head = rb == cb
    head_ones = same_head.astype(BF16)
    trows = {r: r + lax.broadcasted_iota(jnp.int32, (cz - r, w), 0) for r in range(0, cz, 16)}

    def chunk(c, carry):
        r0 = pl.multiple_of(c * cz, cz)
        q = hb_ref[pl.ds(r0, cz), 0:256]
        fp = hb_ref[pl.ds(r0, cz), 256:512]
        v = hb_ref[pl.ds(r0, cz), 512:768]
        g = hb_ref[pl.ds(r0, cz), 768:1024]
        qf = q * jax.nn.sigmoid(q)
        f = lb + (1.0 - lb) * jax.nn.sigmoid(fp)
        log_f = jnp.log(jnp.maximum(f, HGRN_MIN_FORGET))
        kf = (1.0 - lb) * jax.nn.sigmoid(-fp)
        bc = jnp.dot(tri, log_f, preferred_element_type=F32, precision=lax.Precision.HIGHEST)
        st = st_ref[...]
        o = _dot_nt((qf * jnp.exp(bc)).astype(BF16), st.astype(BF16))
        for s in range(cz):
            r_lo = (s // 16) * 16
            arg = bc[r_lo:, :] - bc[s:s + 1, :]
            if s > r_lo:
                arg = jnp.where(trows[r_lo] >= s, arg, NEG_BIG)
            p = qf[r_lo:, :] * kf[s:s + 1, :] * jnp.exp(arg)
            if r_lo:
                pstk_ref[s * cz:s * cz + r_lo, :] = jnp.zeros((r_lo, w), BF16)
            pstk_ref[s * cz + r_lo:(s + 1) * cz, :] = p.astype(BF16)
        accs = [jnp.zeros((16, w), F32) for _ in range(cz // 16)]
        for sg in range(cz // 16):
            att = _dot(pstk_ref[sg * 16 * cz:(sg + 1) * 16 * cz, :], head_ones)
            for sl in range(16):
                s = sg * 16 + sl
                for j in range(sg, cz // 16):
                    accs[j] = accs[j] + att[sl * cz + 16 * j:sl * cz + 16 * j + 16, :] * v[s:s + 1, :]
        o = o + jnp.concatenate(accs, axis=0)
        b_end = bc[cz - 1:cz, :]
        kd = kf * jnp.exp(b_end - bc)
        upd = _dot(v.T.astype(BF16), kd.astype(BF16))
        st_ref[...] = st * jnp.exp(b_end) + jnp.where(same_head, upd, 0.0)
        ms = _dot(o * o, head_ones.astype(F32)) * (1.0 / HGRN_DIM)
        y = o * lax.rsqrt(ms + NORM_EPS)
        o_ref[pl.ds(r0, cz), :] = (y * (g * jax.nn.sigmoid(g))).astype(BF16)
        return carry

    lax.fori_loop(0, tc // cz, chunk, 0)


def _hgrn(lb_logits, hb, layer, batch, seq):
    tc = HGRN_TC
    nt = seq // tc
    cz = HGRN_CHUNK
    return pl.pallas_call(
        functools.partial(_hgrn_kernel, layer=layer, tc=tc),
        out_shape=jax.ShapeDtypeStruct((batch * seq, 256), BF16),
        grid=(batch, nt),
        in_specs=[_const_spec(lb_logits.shape),
                  pl.BlockSpec((tc, 1024), lambda b, i: (b * nt + i, 0))],
        out_specs=pl.BlockSpec((tc, 256), lambda b, i: (b * nt + i, 0)),
        scratch_shapes=[pltpu.VMEM((256, 256), F32), pltpu.VMEM((cz * cz, 256), BF16)],
        compiler_params=_params(("parallel", "arbitrary")),
        name="hgrn2",
    )(lb_logits, hb)


def _dsa_kernel(qdt_ref, iqt_ref, iwt_ref, dkv_ref, dkvt_ref, ikw_ref, o_ref, key_ref, bias_ref, *, tq, tk, n_sel):
    q0 = pl.program_id(1) * tq
    nkb = q0 // tk + 1
    key_i = lax.broadcasted_iota(jnp.int32, (tk, tq), 0)
    qry_i = q0 + lax.broadcasted_iota(jnp.int32, (tk, tq), 1)
    grp = tk // 8
    rows8 = lambda x: x.reshape(grp, 8, tq)
    iqt = iqt_ref[0]
    zpad = jnp.zeros((LANES - DSA_IDX_DIM, tq), BF16)
    iqz = [jnp.concatenate([iqt[hd * DSA_IDX_DIM:(hd + 1) * DSA_IDX_DIM, :], zpad], axis=0)
           for hd in range(DSA_IDX_HEADS)]
    iw = iwt_ref[0]

    def score_block(kb, carry):
        k0 = pl.multiple_of(kb * tk, tk)
        ik = ikw_ref[pl.ds(k0, tk), :]
        sc = jnp.zeros((tk, tq), F32)
        for hd in range(DSA_IDX_HEADS):
            sc = sc + jnp.maximum(_dot(ik, iqz[hd]), 0.0) * iw[hd:hd + 1, :]
        sc = jnp.where(k0 + key_i <= qry_i, sc + 0.0, -jnp.inf)
        bits = pltpu.bitcast(sc, jnp.int32)
        key_ref[pl.ds(k0, tk), :] = jnp.where(bits < 0, bits ^ jnp.int32(0x7FFFFFFF), bits)
        return carry

    lax.fori_loop(0, nkb, score_block, 0)

    def count(pred_fn):
        def body(kb, acc):
            k0 = pl.multiple_of(kb * tk, tk)
            blk = rows8(key_ref[pl.ds(k0, tk), :])
            return acc + jnp.sum(jnp.where(pred_fn(blk), 1.0, 0.0), axis=0)
        acc = lax.fori_loop(0, nkb, body, jnp.zeros((8, tq), F32))
        return jnp.broadcast_to(jnp.sum(acc, axis=0, keepdims=True), (8, tq))

    thr = jnp.full((8, tq), np.int32(-2 ** 31), jnp.int32)
    for bit in range(31, -1, -1):
        trial = thr + np.int32(-2 ** 31 if bit == 31 else 2 ** bit)
        cnt = count(lambda blk, trial=trial: blk >= trial[None])
        thr = jnp.where(cnt >= n_sel, trial, thr)

    need = n_sel - count(lambda blk: blk > thr[None])
    ur = lax.broadcasted_iota(jnp.int32, (tk, tk), 0)
    uc = lax.broadcasted_iota(jnp.int32, (tk, tk), 1)
    earlier = (uc < ur).astype(BF16)
    ones8 = jnp.ones((8, tk), BF16)

    def select_block(kb, seen):
        k0 = pl.multiple_of(kb * tk, tk)
        blk = rows8(key_ref[pl.ds(k0, tk), :])
        eq = blk == thr[None]
        eqb = jnp.where(eq, 1.0, 0.0).reshape(tk, tq).astype(BF16)
        rank = rows8(_dot(earlier, eqb)) + seen[None]
        sel = (blk > thr[None]) | (eq & (rank < need[None]))
        bias = jnp.where(sel, 0.0, NEG_BIG).reshape(tk, tq)
        bias_ref[pl.ds(k0, tk), :] = jnp.where(k0 + key_i <= qry_i, bias, NEG_BIG)
        return seen + _dot(ones8, eqb)

    lax.fori_loop(0, nkb, select_block, jnp.zeros((8, tq), F32))

    qdt = qdt_ref[0]
    zq = jnp.zeros((LANES - DSA_HEAD_DIM, tq), BF16)
    qz = [jnp.concatenate([qdt[hd * DSA_HEAD_DIM:(hd + 1) * DSA_HEAD_DIM, :], zq], axis=0)
          for hd in range(DSA_HEADS)]

    def att_block(kb, carry):
        k0 = pl.multiple_of(kb * tk, tk)
        kblk = dkv_ref[pl.ds(k0, tk), :]
        vt = dkvt_ref[0, DSA_HEAD_DIM:, pl.ds(k0, tk)]
        bias = bias_ref[pl.ds(k0, tk), :]
        out = []
        for hd in range(DSA_HEADS):
            m_i, l_i, acc = carry[hd]
            s = _dot(kblk, qz[hd]) + bias
            m_new = jnp.maximum(m_i, jnp.max(s, axis=0, keepdims=True))
            p = jnp.exp2(s - m_new)
            alpha = jnp.exp2(m_i - m_new)
            l_new = alpha * l_i + jnp.sum(p, axis=0, keepdims=True)
            out.append((m_new, l_new, alpha * acc + _dot(vt, p.astype(BF16))))
        return tuple(out)

    init = tuple((jnp.full((1, tq), NEG_BIG, F32), jnp.zeros((1, tq), F32), jnp.zeros((DSA_HEAD_DIM, tq), F32))
                 for _ in range(DSA_HEADS))
    carry = lax.fori_loop(0, nkb, att_block, init)
    o_ref[...] = jnp.concatenate([acc / l_f for _, l_f, acc in carry], axis=0).T.astype(BF16)


def _dsa(qdt, iqt, iwt, dkv, dkvt, ikw, batch, seq):
    tq, tk = DSA_TQ, DSA_TK
    nq = seq // tq
    n_sel = min(DSA_TOPK, seq // 4)
    return pl.pallas_call(
        functools.partial(_dsa_kernel, tq=tq, tk=tk, n_sel=n_sel),
        out_shape=jax.ShapeDtypeStruct((batch * seq, 256), BF16),
        grid=(batch, nq),
        in_specs=[pl.BlockSpec((1, 256, tq), lambda b, i: (b, 0, i)),
                  pl.BlockSpec((1, 128, tq), lambda b, i: (b, 0, i)),
                  pl.BlockSpec((1, 8, tq), lambda b, i: (b, 0, i)),
                  pl.BlockSpec((seq, 128), lambda b, i: (b, 0)),
                  pl.BlockSpec((1, 128, seq), lambda b, i: (b, 0, 0)),
                  pl.BlockSpec((seq, 128), lambda b, i: (b, 0))],
        out_specs=pl.BlockSpec((tq, 256), lambda b, i: (b * nq + i, 0)),
        scratch_shapes=[pltpu.VMEM((seq, tq), jnp.int32), pltpu.VMEM((seq, tq), F32)],
        compiler_params=_params(("parallel", "parallel")),
        name="dsa",
    )(qdt, iqt, iwt, dkv, dkvt, ikw)


def _mem_kv_kernel(mem_ref, g_ref, w_ref, kt_ref, v_ref):
    mn = _rms(mem_ref[0], g_ref[...]).astype(BF16)
    kv = _dot(mn, w_ref[...])
    kt_ref[0] = kv[:, :256].T.astype(BF16)
    v_ref[0] = kv[:, 256:].astype(BF16)


def _mem_kv(mem, gain, w_kv):
    b, m, d = mem.shape
    return pl.pallas_call(
        _mem_kv_kernel,
        out_shape=(jax.ShapeDtypeStruct((b, 256, m), BF16), jax.ShapeDtypeStruct((b, m, 256), BF16)),
        grid=(b,),
        in_specs=[pl.BlockSpec((1, m, d), lambda i: (i, 0, 0)), _const_spec((1, d)), _const_spec((d, 512))],
        out_specs=(pl.BlockSpec((1, 256, m), lambda i: (i, 0, 0)), pl.BlockSpec((1, m, 256), lambda i: (i, 0, 0))),
        compiler_params=_params(("parallel",)),
        name="mem_kv",
    )(mem, gain, w_kv)


def _merge_kernel(x_ref, ya_ref, yb_ref, yc_ref, yd_ref, gmix_ref, wg_ref, wbr_ref, wout_ref,
                  gq_ref, wq_ref, mkt_ref, mv_ref, wo_ref, gffn_ref, wr_ref, br_ref,
                  x2_ref, h3_ref, cw_ref, *, tm):
    x = x_ref[...]
    d = x.shape[-1]
    h = _rms(x, gmix_ref[...]).astype(BF16)
    merged = jnp.zeros((tm, d), F32)
    for n, y_ref in enumerate((ya_ref, yb_ref, yc_ref, yd_ref)):
        gate = jax.nn.sigmoid(_dot(h, wg_ref[:, n * d:(n + 1) * d]))
        merged = merged + gate * _dot(y_ref[...], wbr_ref[n])
    x1 = x + _dot(merged.astype(BF16), wout_ref[...])
    h2 = _rms(x1, gq_ref[...]).astype(BF16)
    q = (_dot(h2, wq_ref[...]) * (MEM_HEAD_DIM ** -0.5)).astype(BF16)
    lane_head = lax.broadcasted_iota(jnp.int32, (tm, 256), 1) // MEM_HEAD_DIM
    mv = mv_ref[0]
    o = jnp.zeros((tm, 256), F32)
    for hd in range(MEM_HEADS):
        s = _dot(q[:, hd * MEM_HEAD_DIM:(hd + 1) * MEM_HEAD_DIM], mkt_ref[0, hd * MEM_HEAD_DIM:(hd + 1) * MEM_HEAD_DIM, :])
        p = jnp.exp(s - jnp.max(s, axis=-1, keepdims=True))
        p = p / jnp.sum(p, axis=-1, keepdims=True)
        o = o + jnp.where(lane_head == hd, _dot(p.astype(BF16), mv), 0.0)
    x2 = x1 + _dot(o.astype(BF16), wo_ref[...])
    x2_ref[...] = x2
    h3 = _rms(x2, gffn_ref[...])
    h3_ref[...] = h3.astype(BF16)
    logits = jnp.dot(h3, wr_ref[...], preferred_element_type=F32, precision=lax.Precision.HIGHEST) + br_ref[...]
    lane = lax.broadcasted_iota(jnp.int32, (tm, LANES), 1)
    gl = jnp.where(lane < MOE_GROUPS, logits[:, :LANES], -jnp.inf)
    gmax = jnp.max(gl, axis=-1, keepdims=True)
    gsel = jnp.min(jnp.where(gl == gmax, lane, LANES), axis=-1, keepdims=True)
    pg_sel = 1.0 / jnp.sum(jnp.exp(gl - gmax), axis=-1, keepdims=True)
    el = jnp.where(lane // MOE_EXPERTS_PER_GROUP == gsel, logits[:, LANES:], -jnp.inf)
    m1 = jnp.max(el, axis=-1, keepdims=True)
    i1 = jnp.min(jnp.where(el == m1, lane, LANES), axis=-1, keepdims=True)
    el2 = jnp.where(lane == i1, -jnp.inf, el)
    m2 = jnp.max(el2, axis=-1, keepdims=True)
    i2 = jnp.min(jnp.where(el2 == m2, lane, LANES), axis=-1, keepdims=True)
    e21 = jnp.exp(m2 - m1)
    c1 = pg_sel / (1.0 + e21)
    cw_ref[...] = jnp.where(lane == i1, c1, jnp.where(lane == i2, c1 * e21, 0.0))


def _merge(x, ys, gmix, wg, wbr, wout, gq, wq, mkt, mv, wo, gffn, wr, br, batch, seq):
    n, d = x.shape
    tm = MERGE_TM
    spt = seq // tm
    m = mv.shape[1]
    tok = lambda w: pl.BlockSpec((tm, w), lambda i: (i, 0))
    return pl.pallas_call(
        functools.partial(_merge_kernel, tm=tm),
        out_shape=(jax.ShapeDtypeStruct((n, d), F32), jax.ShapeDtypeStruct((n, d), BF16),
                   jax.ShapeDtypeStruct((n, LANES), F32)),
        grid=(n // tm,),
        in_specs=[tok(d), tok(256), tok(256), tok(256), tok(256),
                  _const_spec((1, d)), _const_spec((d, N_BRANCH * d)), _const_spec((N_BRANCH, 256, d)),
                  _const_spec((d, d)), _const_spec((1, d)), _const_spec((d, 256)),
                  pl.BlockSpec((1, 256, m), lambda i: (i // spt, 0, 0)),
                  pl.BlockSpec((1, m, 256), lambda i: (i // spt, 0, 0)),
                  _const_spec((256, d)), _const_spec((1, d)), _const_spec((d, 2 * LANES)), _const_spec((1, 2 * LANES))],
        out_specs=(tok(d), tok(d), tok(LANES)),
        compiler_params=_params(("parallel",)),
        name="merge_mem_router",
    )(x, *ys, gmix, wg, wbr, wout, gq, wq, mkt, mv, wo, gffn, wr, br)


def _moe_kernel(h_ref, cw_ref, x_ref, wg_ref, wu_ref, wd_ref, gfin_ref, o_ref, acc_ref, *, tm, final_norm):
    e = pl.program_id(1)

    @pl.when(e == 0)
    def _():
        acc_ref[...] = jnp.zeros_like(acc_ref)

    h = h_ref[...]
    lane = lax.broadcasted_iota(jnp.int32, (tm, LANES), 1)
    c = jnp.sum(jnp.where(lane == e, cw_ref[...], 0.0), axis=-1, keepdims=True)
    gt = _dot(h, wg_ref[0])
    hid = gt * jax.nn.sigmoid(gt) * _dot(h, wu_ref[0]) * c
    acc_ref[...] += _dot(hid.astype(BF16), wd_ref[0])

    @pl.when(e == pl.num_programs(1) - 1)
    def _():
        y = x_ref[...] + acc_ref[...]
        o_ref[...] = _rms(y, gfin_ref[...]) if final_norm else y


def _moe(h3, cw, x2, wg, wu, wd, gfin, final_norm):
    n, d = x2.shape
    tm = MOE_TM
    ne, _, hid = wg.shape
    return pl.pallas_call(
        functools.partial(_moe_kernel, tm=tm, final_norm=final_norm),
        out_shape=jax.ShapeDtypeStruct((n, d), F32),
        grid=(n // tm, ne),
        in_specs=[pl.BlockSpec((tm, d), lambda i, e: (i, 0)),
                  pl.BlockSpec((tm, LANES), lambda i, e: (i, 0)),
                  pl.BlockSpec((tm, d), lambda i, e: (i, 0)),
                  pl.BlockSpec((1, d, hid), lambda i, e: (e, 0, 0)),
                  pl.BlockSpec((1, d, hid), lambda i, e: (e, 0, 0)),
                  pl.BlockSpec((1, hid, d), lambda i, e: (e, 0, 0)),
                  _const_spec((1, d))],
        out_specs=pl.BlockSpec((tm, d), lambda i, e: (i, 0)),
        scratch_shapes=[pltpu.VMEM((tm, d), F32)],
        compiler_params=_params(("parallel", "arbitrary")),
        name="moe_experts",
    )(h3, cw, x2, wg, wu, wd, gfin)


def kernel(x, mem, positions, norm_mix, w_in, diff_lambda, hgrn_lb_logits, spatial_w, spatial_b, w_branch, w_out,
           norm_mem_q, norm_mem_kv, w_mem_q, w_mem_kv, w_mem_o, norm_ffn, w_router_group, b_router_group,
           w_router_expert, b_router_expert, w_exp_gate, w_exp_up, w_exp_down, norm_final):
    batch, seq, d = x.shape
    depth = w_in.shape[0]
    n = batch * seq
    xf = x.reshape(n, d)
    tabs = _rope_tables(positions)
    row = lambda v: v.reshape(1, -1).astype(F32)
    for l in range(depth):
        lam_init = 0.8 - 0.6 * math.exp(-0.3 * l)
        w1, w_gate = _split_w_in(w_in, l)
        sw = spatial_w[l].reshape(SGU_GROUPS * SGU_CHUNK, SGU_CHUNK)
        sb = jnp.repeat(spatial_b[l].T, SGU_GROUP_DIM, axis=1)
        qat, ka, vat, hb, y_c, qdt, iqt, dkv, dkvt, ikw, iwt = _projection(
            xf, row(norm_mix[l]), w1, tabs, sw, sb, batch, seq)
        y_a = _diff_attention(diff_lambda[l], qat, ka, vat, lam_init, batch, seq)
        y_b = _hgrn(hgrn_lb_logits, hb, l, batch, seq)
        y_d = _dsa(qdt, iqt, iwt, dkv, dkvt, ikw, batch, seq)
        mkt, mv = _mem_kv(mem, row(norm_mem_kv[l]), w_mem_kv[l].astype(BF16))
        wr = jnp.zeros((d, 2 * LANES), F32)
        wr = wr.at[:, :MOE_GROUPS].set(w_router_group[l]).at[:, LANES:LANES + MOE_N_EXPERTS].set(w_router_expert[l])
        br = jnp.zeros((1, 2 * LANES), F32)
        br = br.at[0, :MOE_GROUPS].set(b_router_group[l]).at[0, LANES:LANES + MOE_N_EXPERTS].set(b_router_expert[l])
        x2, h3, cw = _merge(xf, (y_a, y_b, y_c, y_d), row(norm_mix[l]), w_gate, w_branch[l].astype(BF16),
                            w_out[l].astype(BF16), row(norm_mem_q[l]), w_mem_q[l].astype(BF16), mkt, mv,
                            w_mem_o[l].astype(BF16), row(norm_ffn[l]), wr, br, batch, seq)
        xf = _moe(h3, cw, x2, w_exp_gate[l].astype(BF16), w_exp_up[l].astype(BF16), w_exp_down[l].astype(BF16),
                  row(norm_final), final_norm=(l == depth - 1))
    return xf.reshape(batch, seq, d)
```

```python
import functools
import math

import numpy as np
import jax
import jax.numpy as jnp
from jax import lax
from jax.experimental import pallas as pl
from jax.experimental.pallas import tpu as pltpu

F32 = jnp.float32
BF16 = jnp.bfloat16

NORM_EPS = 1e-6
ROPE_THETA = 10000.0
NEG_BIG = -1e30

N_BRANCH = 4
BRANCH_WIDTH = 256
DIFF_HEADS = 4
DIFF_HEAD_DIM = 32
HGRN_HEADS = 4
HGRN_DIM = 64
HGRN_CHUNK = 64
HGRN_MIN_FORGET = 1e-30
SGU_GROUPS = 4
SGU_GROUP_DIM = 64
SGU_CHUNK = 128
DSA_HEADS = 4
DSA_HEAD_DIM = 64
DSA_IDX_HEADS = 4
DSA_IDX_DIM = 32
DSA_TOPK = 256
MEM_HEADS = 4
MEM_HEAD_DIM = 64
MOE_GROUPS = 4
MOE_EXPERTS_PER_GROUP = 8
MOE_N_EXPERTS = 32
MOE_EXPERTS_PER_STEP = 2
ROUTER_EXPERT_LANE = 32

LANES = 128
VMEM_LIMIT = 56 * 1024 * 1024

PROJ_TM = 256
DIFF_TQ = 256
DIFF_TK = 512
HGRN_TC = 512
DSA_TQ = 256
DSA_TK = 512
MERGE_TM = 512
MOE_TM = 1024
ROPE_TM = 1024

C_AQ, C_AK, C_AV = 0, 256, 512
C_HB = 768
C_UV = 1792
C_DQ = 2304
C_DKV = 2560
C_IQ = 2688
C_IKW = 2816
IW_LANE = 32
C_GATE = 2852
C_TOTAL = 2944
LOG2E = math.log2(math.e)


def _params(sem):
    return pltpu.CompilerParams(dimension_semantics=sem, vmem_limit_bytes=VMEM_LIMIT)


def _const_spec(shape):
    nd = len(shape)
    return pl.BlockSpec(shape, lambda *_: (0,) * nd, pipeline_mode=pl.Buffered(1))


def _rms(xf, gain=None):
    y = xf * lax.rsqrt(jnp.mean(xf * xf, axis=-1, keepdims=True) + NORM_EPS)
    return y if gain is None else y * gain


def _sigmoid(x):
    return 0.5 * jnp.tanh(0.5 * x) + 0.5


def _dot(a, b):
    return jnp.dot(a, b, preferred_element_type=F32)


def _dot_nt(a, b):
    return lax.dot_general(a, b, (((1,), (1,)), ((), ())), preferred_element_type=F32)


def _rope_table_kernel(pos_ref, frq_ref, sgn_ref, c32_ref, s32_ref, c64_ref, s64_ref):
    pos = pos_ref[...].astype(F32)
    a32 = pos * frq_ref[0:1, :]
    a64 = pos * frq_ref[1:2, :]
    c32_ref[...] = jnp.cos(a32)
    s32_ref[...] = jnp.sin(a32) * sgn_ref[0:1, :]
    c64_ref[...] = jnp.cos(a64)
    s64_ref[...] = jnp.sin(a64) * sgn_ref[1:2, :]


def _rope_tables(positions):
    n = positions.size
    pos = positions.reshape(n, 1).astype(jnp.int32)
    lane = np.arange(256)
    inv32 = ROPE_THETA ** (-jnp.arange(16, dtype=F32) * (2.0 / 32))
    inv64 = ROPE_THETA ** (-jnp.arange(32, dtype=F32) * (2.0 / 64))
    frq = jnp.stack([inv32[lane % 16], inv64[lane % 32]])
    sgn = jnp.asarray(np.stack([np.where(lane % 32 < 16, -1.0, 1.0),
                                np.where(lane % 64 < 32, -1.0, 1.0)]), F32)
    tm = ROPE_TM
    tab = jax.ShapeDtypeStruct((n, 256), F32)
    return pl.pallas_call(
        _rope_table_kernel,
        out_shape=(tab, tab, tab, tab),
        grid=(n // tm,),
        in_specs=[pl.BlockSpec((tm, 1), lambda i: (i, 0)), _const_spec((2, 256)), _const_spec((2, 256))],
        out_specs=tuple(pl.BlockSpec((tm, 256), lambda i: (i, 0)) for _ in range(4)),
        compiler_params=_params(("parallel",)),
        name="rope_tables",
    )(pos, frq, sgn)


def _split_w_in_kernel(w_ref, w1_ref, wg_ref):
    w1_ref[...] = w_ref[:, :C_TOTAL].astype(BF16)
    a0 = (C_GATE // LANES) * LANES
    tail = w_ref[:, a0:]
    wg_ref[...] = tail[:, C_GATE - a0:].astype(BF16)


def _split_w_in(w_in, layer):
    _, d, width = w_in.shape
    tr = 128
    return pl.pallas_call(
        _split_w_in_kernel,
        out_shape=(jax.ShapeDtypeStruct((d, C_TOTAL), BF16), jax.ShapeDtypeStruct((d, width - C_GATE), BF16)),
        grid=(d // tr,),
        in_specs=[pl.BlockSpec((None, tr, width), lambda i: (layer, i, 0))],
        out_specs=(pl.BlockSpec((tr, C_TOTAL), lambda i: (i, 0)), pl.BlockSpec((tr, width - C_GATE), lambda i: (i, 0))),
        compiler_params=_params(("parallel",)),
        name="split_w_in",
    )(w_in)


def _gelu_tanh(x):
    return 0.5 * x * (1.0 + jnp.tanh(math.sqrt(2.0 / math.pi) * (x + 0.044715 * (x * x * x))))


def _rope(x, cos, sin_signed, half):
    w = x.shape[-1]
    lane = lax.broadcasted_iota(jnp.int32, x.shape, 1)
    partner = jnp.where(lane % (2 * half) < half, pltpu.roll(x, w - half, 1), pltpu.roll(x, half, 1))
    return x * cos + partner * sin_signed


def _proj_kernel(x_ref, g_ref, w_ref, c32_ref, s32_ref, c64_ref, s64_ref, sw_ref, sb_ref,
                 qat_ref, ka_ref, vat_ref, hb_ref, yc_ref, qdt_ref, iqt_ref, dkv_ref, dkvt_ref, ikw_ref, iwt_ref,
                 *, tm):
    h = _rms(x_ref[...], g_ref[...]).astype(BF16)

    def proj(c0, width):
        return _dot(h, w_ref[:, c0:c0 + width])

    c32, s32, c64, s64 = c32_ref[...], s32_ref[...], c64_ref[...], s64_ref[...]
    qat_ref[0] = (_rope(proj(C_AQ, 256), c32, s32, 16) * (DIFF_HEAD_DIM ** -0.5 * LOG2E)).T.astype(BF16)
    ka_ref[...] = _rope(proj(C_AK, 256), c32, s32, 16).astype(BF16)
    vat_ref[0] = proj(C_AV, 256).astype(BF16).T
    hb_ref[...] = proj(C_HB, 1024)
    qdt_ref[0] = (_rope(proj(C_DQ, 256), c64, s64, 32) * (DSA_HEAD_DIM ** -0.5 * LOG2E)).T.astype(BF16)
    iqt_ref[0] = _rope(proj(C_IQ, 128), c32[:, :128], s32[:, :128], 16).T.astype(BF16)
    lane = lax.broadcasted_iota(jnp.int32, (tm, 128), 1)
    is_k = lane < DSA_HEAD_DIM
    dkv = _rope(proj(C_DKV, 128), jnp.where(is_k, c64[:, :128], 1.0), jnp.where(is_k, s64[:, :128], 0.0), 32)
    is_ik = lane < DSA_IDX_DIM
    ikw = _rope(proj(C_IKW, 128), jnp.where(is_ik, c32[:, :128], 1.0), jnp.where(is_ik, s32[:, :128], 0.0), 16)
    dkv_ref[...] = dkv.astype(BF16)
    dkvt_ref[0] = dkv.T.astype(BF16)
    ikw_ref[...] = ikw.astype(BF16)
    iw_scale = DSA_IDX_HEADS ** -0.5 * DSA_IDX_DIM ** -0.5
    iwt_ref[0] = (ikw * iw_scale).T[IW_LANE:IW_LANE + 8, :]
    uv = _gelu_tanh(proj(C_UV, 512))
    u, v = uv[:, :256], uv[:, 256:]
    mu = jnp.mean(v, axis=-1, keepdims=True)
    vc = v - mu
    vn = (vc * lax.rsqrt(jnp.mean(vc * vc, axis=-1, keepdims=True) + NORM_EPS)).astype(BF16)
    r = lax.broadcasted_iota(jnp.int32, (SGU_GROUPS * SGU_CHUNK, SGU_CHUNK), 0)
    c = lax.broadcasted_iota(jnp.int32, (SGU_GROUPS * SGU_CHUNK, SGU_CHUNK), 1)
    wt = jnp.where((r % SGU_CHUNK) >= c, sw_ref[...], 0.0).astype(BF16)
    lane_grp = lax.broadcasted_iota(jnp.int32, (SGU_CHUNK, 256), 1) // SGU_GROUP_DIM
    for ch in range(tm // SGU_CHUNK):
        r0 = ch * SGU_CHUNK
        full = _dot(wt, vn[r0:r0 + SGU_CHUNK, :])
        mixed = sb_ref[...]
        for g in range(SGU_GROUPS):
            mixed = mixed + jnp.where(lane_grp == g, full[g * SGU_CHUNK:(g + 1) * SGU_CHUNK, :], 0.0)
        yc_ref[r0:r0 + SGU_CHUNK, :] = (u[r0:r0 + SGU_CHUNK, :] * mixed).astype(BF16)


def _projection(x, gain, w1, tabs, sw, sb, batch, seq):
    n, d = x.shape
    tm = PROJ_TM
    spt = seq // tm
    tok = lambda w: pl.BlockSpec((tm, w), lambda i: (i, 0))
    tr = lambda rows: pl.BlockSpec((1, rows, tm), lambda i: (i // spt, 0, i % spt))
    out_shape = (
        jax.ShapeDtypeStruct((batch, 256, seq), BF16),
        jax.ShapeDtypeStruct((n, 256), BF16),
        jax.ShapeDtypeStruct((batch, 256, seq), BF16),
        jax.ShapeDtypeStruct((n, 1024), F32),
        jax.ShapeDtypeStruct((n, 256), BF16),
        jax.ShapeDtypeStruct((batch, 256, seq), BF16),
        jax.ShapeDtypeStruct((batch, 128, seq), BF16),
        jax.ShapeDtypeStruct((n, 128), BF16),
        jax.ShapeDtypeStruct((batch, 128, seq), BF16),
        jax.ShapeDtypeStruct((n, 128), BF16),
        jax.ShapeDtypeStruct((batch, 8, seq), F32),
    )
    return pl.pallas_call(
        functools.partial(_proj_kernel, tm=tm),
        out_shape=out_shape,
        grid=(n // tm,),
        in_specs=[tok(d), _const_spec((1, d)), _const_spec((d, C_TOTAL)),
                  tok(256), tok(256), tok(256), tok(256),
                  _const_spec((SGU_GROUPS * SGU_CHUNK, SGU_CHUNK)), _const_spec((SGU_CHUNK, 256))],
        out_specs=(tr(256), tok(256), tr(256), tok(1024), tok(256), tr(256), tr(128), tok(128), tr(128), tok(128), tr(8)),
        compiler_params=_params(("parallel",)),
        name="projection",
    )(x, gain, w1, *tabs, sw, sb)


def _diff_attn_kernel(lam_ref, qt_ref, k_ref, vt_ref, o_ref, *, lam_init, tq, tk):
    q0 = pl.program_id(1) * tq
    kb_diag = q0 // tk
    lv = lam_ref[...]
    lam = (jnp.exp(jnp.sum(lv[0:1] * lv[1:2], axis=-1, keepdims=True))
           - jnp.exp(jnp.sum(lv[2:3] * lv[3:4], axis=-1, keepdims=True)) + lam_init)
    qt = qt_ref[0]
    feat = lax.broadcasted_iota(jnp.int32, (256, tq), 0) // DIFF_HEAD_DIM
    n_maps = 2 * DIFF_HEADS
    qz = jnp.concatenate([jnp.where(feat == i, qt, jnp.zeros_like(qt)) for i in range(n_maps)], axis=1)
    wide = n_maps * tq
    key_i = lax.broadcasted_iota(jnp.int32, (tk, wide), 0)
    qry_i = q0 + lax.broadcasted_iota(jnp.int32, (tk, wide), 1) % tq

    def step(kb, carry, masked):
        m_i, l_i, acc = carry
        k0 = pl.multiple_of(kb * tk, tk)
        s = _dot(k_ref[pl.ds(k0, tk), :], qz)
        if masked:
            s = jnp.where(k0 + key_i <= qry_i, s, NEG_BIG)
        m_new = jnp.maximum(m_i, jnp.max(s, axis=0, keepdims=True))
        p = jnp.exp2(s - m_new)
        alpha = jnp.exp2(m_i - m_new)
        l_new = alpha * l_i + jnp.sum(p, axis=0, keepdims=True)
        pb = p.astype(BF16)
        pv = jnp.concatenate(
            [_dot(vt_ref[0, hd * 64:(hd + 1) * 64, pl.ds(k0, tk)], pb[:, 2 * hd * tq:(2 * hd + 2) * tq])
             for hd in range(DIFF_HEADS)], axis=1)
        return m_new, l_new, alpha * acc + pv

    init = (jnp.full((1, wide), NEG_BIG, F32), jnp.zeros((1, wide), F32), jnp.zeros((64, wide), F32))
    carry = lax.fori_loop(0, kb_diag, functools.partial(step, masked=False), init)
    _, l_f, acc = step(kb_diag, carry, True)
    o_all = acc / l_f
    heads = []
    for hd in range(DIFF_HEADS):
        o0 = o_all[:, 2 * hd * tq:(2 * hd + 1) * tq]
        o1 = o_all[:, (2 * hd + 1) * tq:(2 * hd + 2) * tq]
        o_h = o0 - lam * o1
        ms = jnp.mean(o_h * o_h, axis=0, keepdims=True)
        heads.append(o_h * lax.rsqrt(ms + NORM_EPS) * (1.0 - lam_init))
    o_ref[...] = jnp.concatenate(heads, axis=0).T.astype(BF16)


def _diff_attention(lam_vec, qat, ka, vat, lam_init, batch, seq):
    tq, tk = DIFF_TQ, DIFF_TK
    nq = seq // tq
    return pl.pallas_call(
        functools.partial(_diff_attn_kernel, lam_init=lam_init, tq=tq, tk=tk),
        out_shape=jax.ShapeDtypeStruct((batch * seq, 256), BF16),
        grid=(batch, nq),
        in_specs=[_const_spec((4, DIFF_HEAD_DIM)),
                  pl.BlockSpec((1, 256, tq), lambda b, i: (b, 0, i)),
                  pl.BlockSpec((seq, 256), lambda b, i: (b, 0)),
                  pl.BlockSpec((1, 256, seq), lambda b, i: (b, 0, 0))],
        out_specs=pl.BlockSpec((tq, 256), lambda b, i: (b * nq + i, 0)),
        compiler_params=_params(("parallel", "parallel")),
        name="diff_attention",
    )(lam_vec, qat, ka, vat)


def _hgrn_kernel(lbl_ref, hb_ref, o_ref, st_ref, pstk_ref, *, layer, tc):
    cz = HGRN_CHUNK
    w = 256

    @pl.when(pl.program_id(1) == 0)
    def _():
        st_ref[...] = jnp.zeros_like(st_ref)

    lg = lbl_ref[...]
    e = jnp.exp(lg - jnp.max(lg, axis=0, keepdims=True))
    lw = e / jnp.sum(e, axis=0, keepdims=True)
    lb = jnp.sum(lw[0:layer + 1], axis=0, keepdims=True) - lw[0:1]

    ri = lax.broadcasted_iota(jnp.int32, (cz, cz), 0)
    ci = lax.broadcasted_iota(jnp.int32, (cz, cz), 1)
    tri = (ri >= ci).astype(F32)
    rb = lax.broadcasted_iota(jnp.int32, (w, w), 0) // HGRN_DIM
    cb = lax.broadcasted_iota(jnp.int32, (w, w), 1) // HGRN_DIM
    same_head = rb == cb
    head_ones = same_head.astype(BF16)
    trows = {r: r + lax.broadcasted_iota(jnp.int32, (cz - r, w), 0) for r in range(0, cz, 16)}

    def chunk(c, carry):
        r0 = pl.multiple_of(c * cz, cz)
        q = hb_ref[pl.ds(r0, cz), 0:256]
        fp = hb_ref[pl.ds(r0, cz), 256:512]
        v = hb_ref[pl.ds(r0, cz), 512:768]
        g = hb_ref[pl.ds(r0, cz), 768:1024]
        qf = q * _sigmoid(q)
        f = lb + (1.0 - lb) * jax.nn.sigmoid(fp)
        log_f = jnp.log(jnp.maximum(f, HGRN_MIN_FORGET))
        kf = (1.0 - lb) * jax.nn.sigmoid(-fp)
        bc = jnp.dot(tri, log_f, preferred_element_type=F32, precision=lax.Precision.HIGHEST)
        st = st_ref[...]
        o = _dot_nt((qf * jnp.exp(bc)).astype(BF16), st.astype(BF16))
        for s in range(cz):
            r_lo = (s // 16) * 16
            arg = bc[r_lo:, :] - bc[s:s + 1, :]
            if s > r_lo:
                arg = jnp.where(trows[r_lo] >= s, arg, NEG_BIG)
            p = qf[r_lo:, :] * kf[s:s + 1, :] * jnp.exp(arg)
            if r_lo:
                pstk_ref[s * cz:s * cz + r_lo, :] = jnp.zeros((r_lo, w), BF16)
            pstk_ref[s * cz + r_lo:(s + 1) * cz, :] = p.astype(BF16)
        accs = [jnp.zeros((16, w), F32) for _ in range(cz // 16)]
        for sg in range(cz // 16):
            att = _dot(pstk_ref[sg * 16 * cz:(sg + 1) * 16 * cz, :], head_ones)
            for sl in range(16):
                s = sg * 16 + sl
                for j in range(sg, cz // 16):
                    accs[j] = accs[j] + att[sl * cz + 16 * j:sl * cz + 16 * j + 16, :] * v[s:s + 1, :]
        o = o + jnp.concatenate(accs, axis=0)
        b_end = bc[cz - 1:cz, :]
        kd = kf * jnp.exp(b_end - bc)
        upd = _dot(v.T.astype(BF16), kd.astype(BF16))
        st_ref[...] = st * jnp.exp(b_end) + jnp.where(same_head, upd, 0.0)
        ms = _dot(o * o, head_ones.astype(F32)) * (1.0 / HGRN_DIM)
        y = o * lax.rsqrt(ms + NORM_EPS)
        o_ref[pl.ds(r0, cz), :] = (y * (g * _sigmoid(g))).astype(BF16)
        return carry

    lax.fori_loop(0, tc // cz, chunk, 0)


def _hgrn(lb_logits, hb, layer, batch, seq):
    tc = HGRN_TC
    nt = seq // tc
    cz = HGRN_CHUNK
    return pl.pallas_call(
        functools.partial(_hgrn_kernel, layer=layer, tc=tc),
        out_shape=jax.ShapeDtypeStruct((batch * seq, 256), BF16),
        grid=(batch, nt),
        in_specs=[_const_spec(lb_logits.shape),
                  pl.BlockSpec((tc, 1024), lambda b, i: (b * nt + i, 0))],
        out_specs=pl.BlockSpec((tc, 256), lambda b, i: (b * nt + i, 0)),
        scratch_shapes=[pltpu.VMEM((256, 256), F32), pltpu.VMEM((cz * cz, 256), BF16)],
        compiler_params=_params(("parallel", "arbitrary")),
        name="hgrn2",
    )(lb_logits, hb)


def _dsa_kernel(qdt_ref, iqt_ref, iwt_ref, dkv_ref, dkvt_ref, ikw_ref, o_ref, key_ref, bias_ref, *, tq, tk, n_sel):
    q0 = pl.program_id(1) * tq
    nkb = q0 // tk + 1
    key_i = lax.broadcasted_iota(jnp.int32, (tk, tq), 0)
    qry_i = q0 + lax.broadcasted_iota(jnp.int32, (tk, tq), 1)
    grp = tk // 8
    rows8 = lambda x: x.reshape(grp, 8, tq)
    iqt = iqt_ref[0]
    zpad = jnp.zeros((LANES - DSA_IDX_DIM, tq), BF16)
    iqz = jnp.concatenate([jnp.concatenate([iqt[hd * DSA_IDX_DIM:(hd + 1) * DSA_IDX_DIM, :], zpad], axis=0)
                           for hd in range(DSA_IDX_HEADS)], axis=1)
    iw = iwt_ref[0]

    def score_block(kb, carry):
        k0 = pl.multiple_of(kb * tk, tk)
        sh = jnp.maximum(_dot(ikw_ref[pl.ds(k0, tk), :], iqz), 0.0)
        sc = jnp.zeros((tk, tq), F32)
        for hd in range(DSA_IDX_HEADS):
            sc = sc + sh[:, hd * tq:(hd + 1) * tq] * iw[hd:hd + 1, :]
        sc = jnp.where(k0 + key_i <= qry_i, sc + 0.0, -jnp.inf)
        bits = pltpu.bitcast(sc, jnp.int32)
        key_ref[pl.ds(k0, tk), :] = jnp.where(bits < 0, bits ^ jnp.int32(0x7FFFFFFF), bits)
        return carry

    lax.fori_loop(0, nkb, score_block, 0)

    def count(pred_fn):
        def body(kb, acc):
            k0 = pl.multiple_of(kb * tk, tk)
            parts = None
            for c in range(tk // 64):
                blk = key_ref[pl.ds(k0 + 64 * c, 64), :].reshape(8, 8, tq)
                hit = jnp.where(pred_fn(blk), 1.0, 0.0)
                parts = [hit[j] if parts is None else parts[j] + hit[j] for j in range(8)]
            while len(parts) > 1:
                parts = [a + b for a, b in zip(parts[0::2], parts[1::2])]
            return acc + parts[0]
        acc = lax.fori_loop(0, nkb, body, jnp.zeros((8, tq), F32))
        return jnp.broadcast_to(jnp.sum(acc, axis=0, keepdims=True), (8, tq))

    thr = jnp.full((8, tq), np.int32(-2 ** 31), jnp.int32)
    for bit in range(31, -1, -1):
        trial = thr + np.int32(-2 ** 31 if bit == 31 else 2 ** bit)
        cnt = count(lambda blk, trial=trial: blk >= trial)
        thr = jnp.where(cnt >= n_sel, trial, thr)

    need = n_sel - count(lambda blk: blk > thr)
    ur = lax.broadcasted_iota(jnp.int32, (tk, tk), 0)
    uc = lax.broadcasted_iota(jnp.int32, (tk, tk), 1)
    earlier = (uc < ur).astype(BF16)
    ones8 = jnp.ones((8, tk), BF16)

    def select_block(kb, seen):
        k0 = pl.multiple_of(kb * tk, tk)
        blk = rows8(key_ref[pl.ds(k0, tk), :])
        eq = blk == thr[None]
        eqb = jnp.where(eq, 1.0, 0.0).reshape(tk, tq).astype(BF16)
        rank = rows8(_dot(earlier, eqb)) + seen[None]
        sel = (blk > thr[None]) | (eq & (rank < need[None]))
        bias = jnp.where(sel, 0.0, NEG_BIG).reshape(tk, tq)
        bias_ref[pl.ds(k0, tk), :] = jnp.where(k0 + key_i <= qry_i, bias, NEG_BIG)
        return seen + _dot(ones8, eqb)

    lax.fori_loop(0, nkb, select_block, jnp.zeros((8, tq), F32))

    qdt = qdt_ref[0]
    zq = jnp.zeros((LANES - DSA_HEAD_DIM, tq), BF16)
    qz = jnp.concatenate([jnp.concatenate([qdt[hd * DSA_HEAD_DIM:(hd + 1) * DSA_HEAD_DIM, :], zq], axis=0)
                          for hd in range(DSA_HEADS)], axis=1)
    wide = DSA_HEADS * tq

    def att_block(kb, carry):
        m_i, l_i, acc = carry
        k0 = pl.multiple_of(kb * tk, tk)
        bias = bias_ref[pl.ds(k0, tk), :]
        s = _dot(dkv_ref[pl.ds(k0, tk), :], qz) + jnp.concatenate([bias] * DSA_HEADS, axis=1)
        m_new = jnp.maximum(m_i, jnp.max(s, axis=0, keepdims=True))
        p = jnp.exp2(s - m_new)
        alpha = jnp.exp2(m_i - m_new)
        l_new = alpha * l_i + jnp.sum(p, axis=0, keepdims=True)
        pv = _dot(dkvt_ref[0, DSA_HEAD_DIM:, pl.ds(k0, tk)], p.astype(BF16))
        return m_new, l_new, alpha * acc + pv

    init = (jnp.full((1, wide), NEG_BIG, F32), jnp.zeros((1, wide), F32), jnp.zeros((DSA_HEAD_DIM, wide), F32))
    _, l_f, acc = lax.fori_loop(0, nkb, att_block, init)
    o_all = acc / l_f
    o_ref[...] = jnp.concatenate([o_all[:, hd * tq:(hd + 1) * tq] for hd in range(DSA_HEADS)],
                                 axis=0).T.astype(BF16)


def _dsa(qdt, iqt, iwt, dkv, dkvt, ikw, batch, seq):
    tq, tk = DSA_TQ, DSA_TK
    nq = seq // tq
    n_sel = min(DSA_TOPK, seq // 4)
    return pl.pallas_call(
        functools.partial(_dsa_kernel, tq=tq, tk=tk, n_sel=n_sel),
        out_shape=jax.ShapeDtypeStruct((batch * seq, 256), BF16),
        grid=(batch, nq),
        in_specs=[pl.BlockSpec((1, 256, tq), lambda b, i: (b, 0, i)),
                  pl.BlockSpec((1, 128, tq), lambda b, i: (b, 0, i)),
                  pl.BlockSpec((1, 8, tq), lambda b, i: (b, 0, i)),
                  pl.BlockSpec((seq, 128), lambda b, i: (b, 0)),
                  pl.BlockSpec((1, 128, seq), lambda b, i: (b, 0, 0)),
                  pl.BlockSpec((seq, 128), lambda b, i: (b, 0))],
        out_specs=pl.BlockSpec((tq, 256), lambda b, i: (b * nq + i, 0)),
        scratch_shapes=[pltpu.VMEM((seq, tq), jnp.int32), pltpu.VMEM((seq, tq), F32)],
        compiler_params=_params(("parallel", "parallel")),
        name="dsa",
    )(qdt, iqt, iwt, dkv, dkvt, ikw)


def _mem_kv_kernel(mem_ref, g_ref, w_ref, kt_ref, v_ref):
    mn = _rms(mem_ref[0], g_ref[...]).astype(BF16)
    kv = _dot(mn, w_ref[...])
    kt_ref[0] = kv[:, :256].T.astype(BF16)
    v_ref[0] = kv[:, 256:].astype(BF16)


def _mem_kv(mem, gain, w_kv):
    b, m, d = mem.shape
    return pl.pallas_call(
        _mem_kv_kernel,
        out_shape=(jax.ShapeDtypeStruct((b, 256, m), BF16), jax.ShapeDtypeStruct((b, m, 256), BF16)),
        grid=(b,),
        in_specs=[pl.BlockSpec((1, m, d), lambda i: (i, 0, 0)), _const_spec((1, d)), _const_spec((d, 512))],
        out_specs=(pl.BlockSpec((1, 256, m), lambda i: (i, 0, 0)), pl.BlockSpec((1, m, 256), lambda i: (i, 0, 0))),
        compiler_params=_params(("parallel",)),
        name="mem_kv",
    )(mem, gain, w_kv)


def _merge_kernel(x_ref, ya_ref, yb_ref, yc_ref, yd_ref, gmix_ref, wg_ref, wbr_ref, wout_ref,
                  gq_ref, wq_ref, mkt_ref, mv_ref, wo_ref, gffn_ref, wr_ref, br_ref,
                  x2_ref, h3_ref, cw_ref, *, tm):
    x = x_ref[...]
    d = x.shape[-1]
    h = _rms(x, gmix_ref[...]).astype(BF16)
    merged = jnp.zeros((tm, d), F32)
    for n, y_ref in enumerate((ya_ref, yb_ref, yc_ref, yd_ref)):
        gate = _sigmoid(_dot(h, wg_ref[:, n * d:(n + 1) * d]))
        merged = merged + gate * _dot(y_ref[...], wbr_ref[n])
    x1 = x + _dot(merged.astype(BF16), wout_ref[...])
    h2 = _rms(x1, gq_ref[...]).astype(BF16)
    q = (_dot(h2, wq_ref[...]) * (MEM_HEAD_DIM ** -0.5)).astype(BF16)
    lane_head = lax.broadcasted_iota(jnp.int32, (tm, 256), 1) // MEM_HEAD_DIM
    mv = mv_ref[0]
    o = jnp.zeros((tm, 256), F32)
    for hd in range(MEM_HEADS):
        s = _dot(q[:, hd * MEM_HEAD_DIM:(hd + 1) * MEM_HEAD_DIM], mkt_ref[0, hd * MEM_HEAD_DIM:(hd + 1) * MEM_HEAD_DIM, :])
        p = jnp.exp(s - jnp.max(s, axis=-1, keepdims=True))
        p = p / jnp.sum(p, axis=-1, keepdims=True)
        o = o + jnp.where(lane_head == hd, _dot(p.astype(BF16), mv), 0.0)
    x2 = x1 + _dot(o.astype(BF16), wo_ref[...])
    x2_ref[...] = x2
    h3 = _rms(x2, gffn_ref[...]).astype(BF16)
    h3_ref[...] = h3
    logits = _dot(h3, wr_ref[...]) + br_ref[...]
    lane = lax.broadcasted_iota(jnp.int32, (tm, LANES), 1)
    gl = jnp.where(lane < MOE_GROUPS, logits, -jnp.inf)
    gmax = jnp.max(gl, axis=-1, keepdims=True)
    gsel = jnp.min(jnp.where(gl == gmax, lane, LANES), axis=-1, keepdims=True)
    pg_sel = 1.0 / jnp.sum(jnp.exp(gl - gmax), axis=-1, keepdims=True)
    in_group = (lane - ROUTER_EXPERT_LANE) // MOE_EXPERTS_PER_GROUP == gsel
    el = jnp.where(in_group, logits, -jnp.inf)
    m1 = jnp.max(el, axis=-1, keepdims=True)
    i1 = jnp.min(jnp.where(el == m1, lane, LANES), axis=-1, keepdims=True)
    el2 = jnp.where(lane == i1, -jnp.inf, el)
    m2 = jnp.max(el2, axis=-1, keepdims=True)
    i2 = jnp.min(jnp.where(el2 == m2, lane, LANES), axis=-1, keepdims=True)
    e21 = jnp.exp(m2 - m1)
    c1 = pg_sel / (1.0 + e21)
    cw_ref[...] = jnp.where(lane == i1, c1, jnp.where(lane == i2, c1 * e21, 0.0))


def _merge(x, ys, gmix, wg, wbr, wout, gq, wq, mkt, mv, wo, gffn, wr, br, batch, seq):
    n, d = x.shape
    tm = MERGE_TM
    spt = seq // tm
    m = mv.shape[1]
    tok = lambda w: pl.BlockSpec((tm, w), lambda i: (i, 0))
    return pl.pallas_call(
        functools.partial(_merge_kernel, tm=tm),
        out_shape=(jax.ShapeDtypeStruct((n, d), F32), jax.ShapeDtypeStruct((n, d), BF16),
                   jax.ShapeDtypeStruct((n, LANES), F32)),
        grid=(n // tm,),
        in_specs=[tok(d), tok(256), tok(256), tok(256), tok(256),
                  _const_spec((1, d)), _const_spec((d, N_BRANCH * d)), _const_spec((N_BRANCH, 256, d)),
                  _const_spec((d, d)), _const_spec((1, d)), _const_spec((d, 256)),
                  pl.BlockSpec((1, 256, m), lambda i: (i // spt, 0, 0)),
                  pl.BlockSpec((1, m, 256), lambda i: (i // spt, 0, 0)),
                  _const_spec((256, d)), _const_spec((1, d)), _const_spec((d, LANES)), _const_spec((1, LANES))],
        out_specs=(tok(d), tok(d), tok(LANES)),
        compiler_params=_params(("parallel",)),
        name="merge_mem_router",
    )(x, *ys, gmix, wg, wbr, wout, gq, wq, mkt, mv, wo, gffn, wr, br)


def _moe_kernel(h_ref, cw_ref, x_ref, wg_ref, wu_ref, wd_ref, gfin_ref, o_ref, acc_ref, *, tm, final_norm):
    step = pl.program_id(1)

    @pl.when(step == 0)
    def _():
        acc_ref[...] = jnp.zeros_like(acc_ref)

    h = h_ref[...]
    hidden = wg_ref.shape[-1]
    cw = cw_ref[...]
    cw_hi = cw.astype(BF16)
    cw_lo = (cw - cw_hi.astype(F32)).astype(BF16)
    lane_row = lax.broadcasted_iota(jnp.int32, (LANES, hidden), 0)
    y = None
    for j in range(MOE_EXPERTS_PER_STEP):
        e = step * MOE_EXPERTS_PER_STEP + j
        pick = jnp.where(lane_row == ROUTER_EXPERT_LANE + e, 1.0, 0.0).astype(BF16)
        c = _dot(cw_hi, pick) + _dot(cw_lo, pick)
        gt = _dot(h, wg_ref[j])
        hid = gt * _sigmoid(gt) * _dot(h, wu_ref[j]) * c
        yj = _dot(hid.astype(BF16), wd_ref[j])
        y = yj if y is None else y + yj
    acc_ref[...] += y

    @pl.when(step == pl.num_programs(1) - 1)
    def _():
        out = x_ref[...] + acc_ref[...]
        o_ref[...] = _rms(out, gfin_ref[...]) if final_norm else out


def _moe(h3, cw, x2, wg, wu, wd, gfin, final_norm):
    n, d = x2.shape
    tm = MOE_TM
    ne, _, hid = wg.shape
    eps = MOE_EXPERTS_PER_STEP
    return pl.pallas_call(
        functools.partial(_moe_kernel, tm=tm, final_norm=final_norm),
        out_shape=jax.ShapeDtypeStruct((n, d), F32),
        grid=(n // tm, ne // eps),
        in_specs=[pl.BlockSpec((tm, d), lambda i, e: (i, 0)),
                  pl.BlockSpec((tm, LANES), lambda i, e: (i, 0)),
                  pl.BlockSpec((tm, d), lambda i, e: (i, 0)),
                  pl.BlockSpec((eps, d, hid), lambda i, e: (e, 0, 0)),
                  pl.BlockSpec((eps, d, hid), lambda i, e: (e, 0, 0)),
                  pl.BlockSpec((eps, hid, d), lambda i, e: (e, 0, 0)),
                  _const_spec((1, d))],
        out_specs=pl.BlockSpec((tm, d), lambda i, e: (i, 0)),
        scratch_shapes=[pltpu.VMEM((tm, d), F32)],
        compiler_params=_params(("parallel", "arbitrary")),
        name="moe_experts",
    )(h3, cw, x2, wg, wu, wd, gfin)


def kernel(x, mem, positions, norm_mix, w_in, diff_lambda, hgrn_lb_logits, spatial_w, spatial_b, w_branch, w_out,
           norm_mem_q, norm_mem_kv, w_mem_q, w_mem_kv, w_mem_o, norm_ffn, w_router_group, b_router_group,
           w_router_expert, b_router_expert, w_exp_gate, w_exp_up, w_exp_down, norm_final):
    batch, seq, d = x.shape
    depth = w_in.shape[0]
    n = batch * seq
    xf = x.reshape(n, d)
    tabs = _rope_tables(positions)
    row = lambda v: v.reshape(1, -1).astype(F32)
    for l in range(depth):
        lam_init = 0.8 - 0.6 * math.exp(-0.3 * l)
        w1, w_gate = _split_w_in(w_in, l)
        sw = spatial_w[l].reshape(SGU_GROUPS * SGU_CHUNK, SGU_CHUNK)
        sb = jnp.repeat(spatial_b[l].T, SGU_GROUP_DIM, axis=1)
        qat, ka, vat, hb, y_c, qdt, iqt, dkv, dkvt, ikw, iwt = _projection(
            xf, row(norm_mix[l]), w1, tabs, sw, sb, batch, seq)
        y_a = _diff_attention(diff_lambda[l], qat, ka, vat, lam_init, batch, seq)
        y_b = _hgrn(hgrn_lb_logits, hb, l, batch, seq)
        y_d = _dsa(qdt, iqt, iwt, dkv, dkvt, ikw, batch, seq)
        mkt, mv = _mem_kv(mem, row(norm_mem_kv[l]), w_mem_kv[l].astype(BF16))
        e0, e1 = ROUTER_EXPERT_LANE, ROUTER_EXPERT_LANE + MOE_N_EXPERTS
        wr = jnp.zeros((d, LANES), F32)
        wr = wr.at[:, :MOE_GROUPS].set(w_router_group[l]).at[:, e0:e1].set(w_router_expert[l]).astype(BF16)
        br = jnp.zeros((1, LANES), F32)
        br = br.at[0, :MOE_GROUPS].set(b_router_group[l]).at[0, e0:e1].set(b_router_expert[l])
        x2, h3, cw = _merge(xf, (y_a, y_b, y_c, y_d), row(norm_mix[l]), w_gate, w_branch[l].astype(BF16),
                            w_out[l].astype(BF16), row(norm_mem_q[l]), w_mem_q[l].astype(BF16), mkt, mv,
                            w_mem_o[l].astype(BF16), row(norm_ffn[l]), wr, br, batch, seq)
        xf = _moe(h3, cw, x2, w_exp_gate[l].astype(BF16), w_exp_up[l].astype(BF16), w_exp_down[l].astype(BF16),
                  row(norm_final), final_norm=(l == depth - 1))
    return xf.reshape(batch, seq, d)
```

```python
import functools
import math

import numpy as np
import jax
import jax.numpy as jnp
from jax import lax
from jax.experimental import pallas as pl
from jax.experimental.pallas import tpu as pltpu
from jax.experimental.pallas import tpu_sc as plsc

F32 = jnp.float32
BF16 = jnp.bfloat16

NORM_EPS = 1e-6
ROPE_THETA = 10000.0
NEG_BIG = -1e30

N_BRANCH = 4
BRANCH_WIDTH = 256
DIFF_HEADS = 4
DIFF_HEAD_DIM = 32
HGRN_HEADS = 4
HGRN_DIM = 64
HGRN_CHUNK = 64
HGRN_MIN_FORGET = 1e-30
SGU_GROUPS = 4
SGU_GROUP_DIM = 64
SGU_CHUNK = 128
DSA_HEADS = 4
DSA_HEAD_DIM = 64
DSA_IDX_HEADS = 4
DSA_IDX_DIM = 32
DSA_TOPK = 256
MEM_HEADS = 4
MEM_HEAD_DIM = 64
MOE_GROUPS = 4
MOE_EXPERTS_PER_GROUP = 8
MOE_N_EXPERTS = 32
MOE_BLOCK = 256
ROUTER_EXPERT_LANE = 32
SC_CORES = 2
SC_SUBCORES = 16
SC_GATHER_CHUNK = 16
SC_GATHER_BUFS = 4

LANES = 128
VMEM_LIMIT = 56 * 1024 * 1024

PROJ_TM = 256
DIFF_TQ = 256
DIFF_TK = 512
HGRN_TC = 512
DSA_TQ = 256
DSA_TK = 512
MERGE_TM = 512
COMBINE_TM = 512
ROPE_TM = 1024

C_AQ, C_AK, C_AV = 0, 256, 512
C_HB = 768
C_UV = 1792
C_DQ = 2304
C_DKV = 2560
C_IQ = 2688
C_IKW = 2816
IW_LANE = 32
C_GATE = 2852
C_TOTAL = 2944
LOG2E = math.log2(math.e)


def _params(sem):
    return pltpu.CompilerParams(dimension_semantics=sem, vmem_limit_bytes=VMEM_LIMIT)


def _const_spec(shape):
    nd = len(shape)
    return pl.BlockSpec(shape, lambda *_: (0,) * nd, pipeline_mode=pl.Buffered(1))


def _rms(xf, gain=None):
    y = xf * lax.rsqrt(jnp.mean(xf * xf, axis=-1, keepdims=True) + NORM_EPS)
    return y if gain is None else y * gain


def _sigmoid(x):
    return 0.5 * jnp.tanh(0.5 * x) + 0.5


def _dot(a, b):
    return jnp.dot(a, b, preferred_element_type=F32)


def _dot_nt(a, b):
    return lax.dot_general(a, b, (((1,), (1,)), ((), ())), preferred_element_type=F32)


def _rope_table_kernel(pos_ref, frq_ref, sgn_ref, c32_ref, s32_ref, c64_ref, s64_ref):
    pos = pos_ref[...].astype(F32)
    a32 = pos * frq_ref[0:1, :]
    a64 = pos * frq_ref[1:2, :]
    c32_ref[...] = jnp.cos(a32)
    s32_ref[...] = jnp.sin(a32) * sgn_ref[0:1, :]
    c64_ref[...] = jnp.cos(a64)
    s64_ref[...] = jnp.sin(a64) * sgn_ref[1:2, :]


def _rope_tables(positions):
    n = positions.size
    pos = positions.reshape(n, 1).astype(jnp.int32)
    lane = np.arange(256)
    inv32 = ROPE_THETA ** (-jnp.arange(16, dtype=F32) * (2.0 / 32))
    inv64 = ROPE_THETA ** (-jnp.arange(32, dtype=F32) * (2.0 / 64))
    frq = jnp.stack([inv32[lane % 16], inv64[lane % 32]])
    sgn = jnp.asarray(np.stack([np.where(lane % 32 < 16, -1.0, 1.0),
                                np.where(lane % 64 < 32, -1.0, 1.0)]), F32)
    tm = ROPE_TM
    tab = jax.ShapeDtypeStruct((n, 256), F32)
    return pl.pallas_call(
        _rope_table_kernel,
        out_shape=(tab, tab, tab, tab),
        grid=(n // tm,),
        in_specs=[pl.BlockSpec((tm, 1), lambda i: (i, 0)), _const_spec((2, 256)), _const_spec((2, 256))],
        out_specs=tuple(pl.BlockSpec((tm, 256), lambda i: (i, 0)) for _ in range(4)),
        compiler_params=_params(("parallel",)),
        name="rope_tables",
    )(pos, frq, sgn)


def _split_w_in_kernel(w_ref, w1_ref, wg_ref):
    w1_ref[...] = w_ref[:, :C_TOTAL].astype(BF16)
    a0 = (C_GATE // LANES) * LANES
    tail = w_ref[:, a0:]
    wg_ref[...] = tail[:, C_GATE - a0:].astype(BF16)


def _split_w_in(w_in, layer):
    _, d, width = w_in.shape
    tr = 128
    return pl.pallas_call(
        _split_w_in_kernel,
        out_shape=(jax.ShapeDtypeStruct((d, C_TOTAL), BF16), jax.ShapeDtypeStruct((d, width - C_GATE), BF16)),
        grid=(d // tr,),
        in_specs=[pl.BlockSpec((None, tr, width), lambda i: (layer, i, 0))],
        out_specs=(pl.BlockSpec((tr, C_TOTAL), lambda i: (i, 0)), pl.BlockSpec((tr, width - C_GATE), lambda i: (i, 0))),
        compiler_params=_params(("parallel",)),
        name="split_w_in",
    )(w_in)


def _gelu_tanh(x):
    return 0.5 * x * (1.0 + jnp.tanh(math.sqrt(2.0 / math.pi) * (x + 0.044715 * (x * x * x))))


def _rope(x, cos, sin_signed, half):
    w = x.shape[-1]
    lane = lax.broadcasted_iota(jnp.int32, x.shape, 1)
    partner = jnp.where(lane % (2 * half) < half, pltpu.roll(x, w - half, 1), pltpu.roll(x, half, 1))
    return x * cos + partner * sin_signed


def _proj_kernel(x_ref, g_ref, w_ref, c32_ref, s32_ref, c64_ref, s64_ref, sw_ref, sb_ref,
                 qat_ref, ka_ref, vat_ref, hb_ref, yc_ref, qdt_ref, iqt_ref, dkv_ref, dkvt_ref, ikw_ref, iwt_ref,
                 *, tm):
    h = _rms(x_ref[...], g_ref[...]).astype(BF16)

    def proj(c0, width):
        return _dot(h, w_ref[:, c0:c0 + width])

    c32, s32, c64, s64 = c32_ref[...], s32_ref[...], c64_ref[...], s64_ref[...]
    qat_ref[0] = (_rope(proj(C_AQ, 256), c32, s32, 16) * (DIFF_HEAD_DIM ** -0.5 * LOG2E)).T.astype(BF16)
    ka_ref[...] = _rope(proj(C_AK, 256), c32, s32, 16).astype(BF16)
    vat_ref[0] = proj(C_AV, 256).astype(BF16).T
    hb_ref[...] = proj(C_HB, 1024)
    qdt_ref[0] = (_rope(proj(C_DQ, 256), c64, s64, 32) * (DSA_HEAD_DIM ** -0.5 * LOG2E)).T.astype(BF16)
    iqt_ref[0] = _rope(proj(C_IQ, 128), c32[:, :128], s32[:, :128], 16).T.astype(BF16)
    lane = lax.broadcasted_iota(jnp.int32, (tm, 128), 1)
    is_k = lane < DSA_HEAD_DIM
    dkv = _rope(proj(C_DKV, 128), jnp.where(is_k, c64[:, :128], 1.0), jnp.where(is_k, s64[:, :128], 0.0), 32)
    is_ik = lane < DSA_IDX_DIM
    ikw = _rope(proj(C_IKW, 128), jnp.where(is_ik, c32[:, :128], 1.0), jnp.where(is_ik, s32[:, :128], 0.0), 16)
    dkv_ref[...] = dkv.astype(BF16)
    dkvt_ref[0] = dkv.T.astype(BF16)
    ikw_ref[...] = ikw.astype(BF16)
    iw_scale = DSA_IDX_HEADS ** -0.5 * DSA_IDX_DIM ** -0.5
    iwt_ref[0] = (ikw * iw_scale).T[IW_LANE:IW_LANE + 8, :]
    uv = _gelu_tanh(proj(C_UV, 512))
    u, v = uv[:, :256], uv[:, 256:]
    mu = jnp.mean(v, axis=-1, keepdims=True)
    vc = v - mu
    vn = (vc * lax.rsqrt(jnp.mean(vc * vc, axis=-1, keepdims=True) + NORM_EPS)).astype(BF16)
    r = lax.broadcasted_iota(jnp.int32, (SGU_GROUPS * SGU_CHUNK, SGU_CHUNK), 0)
    c = lax.broadcasted_iota(jnp.int32, (SGU_GROUPS * SGU_CHUNK, SGU_CHUNK), 1)
    wt = jnp.where((r % SGU_CHUNK) >= c, sw_ref[...], 0.0).astype(BF16)
    lane_grp = lax.broadcasted_iota(jnp.int32, (SGU_CHUNK, 256), 1) // SGU_GROUP_DIM
    for ch in range(tm // SGU_CHUNK):
        r0 = ch * SGU_CHUNK
        full = _dot(wt, vn[r0:r0 + SGU_CHUNK, :])
        mixed = sb_ref[...]
        for g in range(SGU_GROUPS):
            mixed = mixed + jnp.where(lane_grp == g, full[g * SGU_CHUNK:(g + 1) * SGU_CHUNK, :], 0.0)
        yc_ref[r0:r0 + SGU_CHUNK, :] = (u[r0:r0 + SGU_CHUNK, :] * mixed).astype(BF16)


def _projection(x, gain, w1, tabs, sw, sb, batch, seq):
    n, d = x.shape
    tm = PROJ_TM
    spt = seq // tm
    tok = lambda w: pl.BlockSpec((tm, w), lambda i: (i, 0))
    tr = lambda rows: pl.BlockSpec((1, rows, tm), lambda i: (i // spt, 0, i % spt))
    out_shape = (
        jax.ShapeDtypeStruct((batch, 256, seq), BF16),
        jax.ShapeDtypeStruct((n, 256), BF16),
        jax.ShapeDtypeStruct((batch, 256, seq), BF16),
        jax.ShapeDtypeStruct((n, 1024), F32),
        jax.ShapeDtypeStruct((n, 256), BF16),
        jax.ShapeDtypeStruct((batch, 256, seq), BF16),
        jax.ShapeDtypeStruct((batch, 128, seq), BF16),
        jax.ShapeDtypeStruct((n, 128), BF16),
        jax.ShapeDtypeStruct((batch, 128, seq), BF16),
        jax.ShapeDtypeStruct((n, 128), BF16),
        jax.ShapeDtypeStruct((batch, 8, seq), F32),
    )
    return pl.pallas_call(
        functools.partial(_proj_kernel, tm=tm),
        out_shape=out_shape,
        grid=(n // tm,),
        in_specs=[tok(d), _const_spec((1, d)), _const_spec((d, C_TOTAL)),
                  tok(256), tok(256), tok(256), tok(256),
                  _const_spec((SGU_GROUPS * SGU_CHUNK, SGU_CHUNK)), _const_spec((SGU_CHUNK, 256))],
        out_specs=(tr(256), tok(256), tr(256), tok(1024), tok(256), tr(256), tr(128), tok(128), tr(128), tok(128), tr(8)),
        compiler_params=_params(("parallel",)),
        name="projection",
    )(x, gain, w1, *tabs, sw, sb)


def _diff_attn_kernel(lam_ref, qt_ref, k_ref, vt_ref, o_ref, *, lam_init, tq, tk):
    q0 = pl.program_id(1) * tq
    kb_diag = q0 // tk
    lv = lam_ref[...]
    lam = (jnp.exp(jnp.sum(lv[0:1] * lv[1:2], axis=-1, keepdims=True))
           - jnp.exp(jnp.sum(lv[2:3] * lv[3:4], axis=-1, keepdims=True)) + lam_init)
    qt = qt_ref[0]
    feat = lax.broadcasted_iota(jnp.int32, (256, tq), 0) // DIFF_HEAD_DIM
    n_maps = 2 * DIFF_HEADS
    qz = jnp.concatenate([jnp.where(feat == i, qt, jnp.zeros_like(qt)) for i in range(n_maps)], axis=1)
    wide = n_maps * tq
    key_i = lax.broadcasted_iota(jnp.int32, (tk, wide), 0)
    qry_i = q0 + lax.broadcasted_iota(jnp.int32, (tk, wide), 1) % tq

    def step(kb, carry, masked):
        m_i, l_i, acc = carry
        k0 = pl.multiple_of(kb * tk, tk)
        s = _dot(k_ref[pl.ds(k0, tk), :], qz)
        if masked:
            s = jnp.where(k0 + key_i <= qry_i, s, NEG_BIG)
        m_new = jnp.maximum(m_i, jnp.max(s, axis=0, keepdims=True))
        p = jnp.exp2(s - m_new)
        alpha = jnp.exp2(m_i - m_new)
        l_new = alpha * l_i + jnp.sum(p, axis=0, keepdims=True)
        pb = p.astype(BF16)
        pv = jnp.concatenate(
            [_dot(vt_ref[0, hd * 64:(hd + 1) * 64, pl.ds(k0, tk)], pb[:, 2 * hd * tq:(2 * hd + 2) * tq])
             for hd in range(DIFF_HEADS)], axis=1)
        return m_new, l_new, alpha * acc + pv

    init = (jnp.full((1, wide), NEG_BIG, F32), jnp.zeros((1, wide), F32), jnp.zeros((64, wide), F32))
    carry = lax.fori_loop(0, kb_diag, functools.partial(step, masked=False), init)
    _, l_f, acc = step(kb_diag, carry, True)
    o_all = acc / l_f
    heads = []
    for hd in range(DIFF_HEADS):
        o0 = o_all[:, 2 * hd * tq:(2 * hd + 1) * tq]
        o1 = o_all[:, (2 * hd + 1) * tq:(2 * hd + 2) * tq]
        o_h = o0 - lam * o1
        ms = jnp.mean(o_h * o_h, axis=0, keepdims=True)
        heads.append(o_h * lax.rsqrt(ms + NORM_EPS) * (1.0 - lam_init))
    o_ref[...] = jnp.concatenate(heads, axis=0).T.astype(BF16)


def _diff_attention(lam_vec, qat, ka, vat, lam_init, batch, seq):
    tq, tk = DIFF_TQ, DIFF_TK
    nq = seq // tq
    return pl.pallas_call(
        functools.partial(_diff_attn_kernel, lam_init=lam_init, tq=tq, tk=tk),
        out_shape=jax.ShapeDtypeStruct((batch * seq, 256), BF16),
        grid=(batch, nq),
        in_specs=[_const_spec((4, DIFF_HEAD_DIM)),
                  pl.BlockSpec((1, 256, tq), lambda b, i: (b, 0, i)),
                  pl.BlockSpec((seq, 256), lambda b, i: (b, 0)),
                  pl.BlockSpec((1, 256, seq), lambda b, i: (b, 0, 0))],
        out_specs=pl.BlockSpec((tq, 256), lambda b, i: (b * nq + i, 0)),
        compiler_params=_params(("parallel", "parallel")),
        name="diff_attention",
    )(lam_vec, qat, ka, vat)


def _hgrn_kernel(lbl_ref, hb_ref, o_ref, st_ref, pstk_ref, *, layer, tc):
    cz = HGRN_CHUNK
    w = 256

    @pl.when(pl.program_id(1) == 0)
    def _():
        st_ref[...] = jnp.zeros_like(st_ref)

    lg = lbl_ref[...]
    e = jnp.exp(lg - jnp.max(lg, axis=0, keepdims=True))
    lw = e / jnp.sum(e, axis=0, keepdims=True)
    lb = jnp.sum(lw[0:layer + 1], axis=0, keepdims=True) - lw[0:1]

    ri = lax.broadcasted_iota(jnp.int32, (cz, cz), 0)
    ci = lax.broadcasted_iota(jnp.int32, (cz, cz), 1)
    tri = (ri >= ci).astype(F32)
    rb = lax.broadcasted_iota(jnp.int32, (w, w), 0) // HGRN_DIM
    cb = lax.broadcasted_iota(jnp.int32, (w, w), 1) // HGRN_DIM
    same_head = rb == cb
    head_ones = same_head.astype(BF16)
    trows = {r: r + lax.broadcasted_iota(jnp.int32, (cz - r, w), 0) for r in range(0, cz, 16)}

    def chunk(c, carry):
        r0 = pl.multiple_of(c * cz, cz)
        q = hb_ref[pl.ds(r0, cz), 0:256]
        fp = hb_ref[pl.ds(r0, cz), 256:512]
        v = hb_ref[pl.ds(r0, cz), 512:768]
        g = hb_ref[pl.ds(r0, cz), 768:1024]
        qf = q * _sigmoid(q)
        f = lb + (1.0 - lb) * jax.nn.sigmoid(fp)
        log_f = jnp.log(jnp.maximum(f, HGRN_MIN_FORGET))
        kf = (1.0 - lb) * jax.nn.sigmoid(-fp)
        bc = jnp.dot(tri, log_f, preferred_element_type=F32, precision=lax.Precision.HIGHEST)
        st = st_ref[...]
        o = _dot_nt((qf * jnp.exp(bc)).astype(BF16), st.astype(BF16))
        for s in range(cz):
            r_lo = (s // 16) * 16
            arg = bc[r_lo:, :] - bc[s:s + 1, :]
            if s > r_lo:
                arg = jnp.where(trows[r_lo] >= s, arg, NEG_BIG)
            p = qf[r_lo:, :] * kf[s:s + 1, :] * jnp.exp(arg)
            if r_lo:
                pstk_ref[s * cz:s * cz + r_lo, :] = jnp.zeros((r_lo, w), BF16)
            pstk_ref[s * cz + r_lo:(s + 1) * cz, :] = p.astype(BF16)
        accs = [jnp.zeros((16, w), F32) for _ in range(cz // 16)]
        for sg in range(cz // 16):
            att = _dot(pstk_ref[sg * 16 * cz:(sg + 1) * 16 * cz, :], head_ones)
            for sl in range(16):
                s = sg * 16 + sl
                for j in range(sg, cz // 16):
                    accs[j] = accs[j] + att[sl * cz + 16 * j:sl * cz + 16 * j + 16, :] * v[s:s + 1, :]
        o = o + jnp.concatenate(accs, axis=0)
        b_end = bc[cz - 1:cz, :]
        kd = kf * jnp.exp(b_end - bc)
        upd = _dot(v.T.astype(BF16), kd.astype(BF16))
        st_ref[...] = st * jnp.exp(b_end) + jnp.where(same_head, upd, 0.0)
        ms = _dot(o * o, head_ones.astype(F32)) * (1.0 / HGRN_DIM)
        y = o * lax.rsqrt(ms + NORM_EPS)
        o_ref[pl.ds(r0, cz), :] = (y * (g * _sigmoid(g))).astype(BF16)
        return carry

    lax.fori_loop(0, tc // cz, chunk, 0)


def _hgrn(lb_logits, hb, layer, batch, seq):
    tc = HGRN_TC
    nt = seq // tc
    cz = HGRN_CHUNK
    return pl.pallas_call(
        functools.partial(_hgrn_kernel, layer=layer, tc=tc),
        out_shape=jax.ShapeDtypeStruct((batch * seq, 256), BF16),
        grid=(batch, nt),
        in_specs=[_const_spec(lb_logits.shape),
                  pl.BlockSpec((tc, 1024), lambda b, i: (b * nt + i, 0))],
        out_specs=pl.BlockSpec((tc, 256), lambda b, i: (b * nt + i, 0)),
        scratch_shapes=[pltpu.VMEM((256, 256), F32), pltpu.VMEM((cz * cz, 256), BF16)],
        compiler_params=_params(("parallel", "arbitrary")),
        name="hgrn2",
    )(lb_logits, hb)


def _dsa_kernel(qdt_ref, iqt_ref, iwt_ref, dkv_ref, dkvt_ref, ikw_ref, o_ref, key_ref, bias_ref, *, tq, tk, n_sel):
    q0 = pl.program_id(1) * tq
    nkb = q0 // tk + 1
    key_i = lax.broadcasted_iota(jnp.int32, (tk, tq), 0)
    qry_i = q0 + lax.broadcasted_iota(jnp.int32, (tk, tq), 1)
    grp = tk // 8
    rows8 = lambda x: x.reshape(grp, 8, tq)
    iqt = iqt_ref[0]
    zpad = jnp.zeros((LANES - DSA_IDX_DIM, tq), BF16)
    iqz = jnp.concatenate([jnp.concatenate([iqt[hd * DSA_IDX_DIM:(hd + 1) * DSA_IDX_DIM, :], zpad], axis=0)
                           for hd in range(DSA_IDX_HEADS)], axis=1)
    iw = iwt_ref[0]

    def score_block(kb, carry):
        k0 = pl.multiple_of(kb * tk, tk)
        sh = jnp.maximum(_dot(ikw_ref[pl.ds(k0, tk), :], iqz), 0.0)
        sc = jnp.zeros((tk, tq), F32)
        for hd in range(DSA_IDX_HEADS):
            sc = sc + sh[:, hd * tq:(hd + 1) * tq] * iw[hd:hd + 1, :]
        sc = jnp.where(k0 + key_i <= qry_i, sc + 0.0, -jnp.inf)
        bits = pltpu.bitcast(sc, jnp.int32)
        key_ref[pl.ds(k0, tk), :] = jnp.where(bits < 0, bits ^ jnp.int32(0x7FFFFFFF), bits)
        return carry

    lax.fori_loop(0, nkb, score_block, 0)

    def count(pred_fn):
        def body(kb, acc):
            k0 = pl.multiple_of(kb * tk, tk)
            parts = None
            for c in range(tk // 64):
                blk = key_ref[pl.ds(k0 + 64 * c, 64), :].reshape(8, 8, tq)
                hit = jnp.where(pred_fn(blk), 1.0, 0.0)
                parts = [hit[j] if parts is None else parts[j] + hit[j] for j in range(8)]
            while len(parts) > 1:
                parts = [a + b for a, b in zip(parts[0::2], parts[1::2])]
            return acc + parts[0]
        acc = lax.fori_loop(0, nkb, body, jnp.zeros((8, tq), F32))
        return jnp.broadcast_to(jnp.sum(acc, axis=0, keepdims=True), (8, tq))

    thr = jnp.full((8, tq), np.int32(-2 ** 31), jnp.int32)
    for bit in range(31, -1, -1):
        trial = thr + np.int32(-2 ** 31 if bit == 31 else 2 ** bit)
        cnt = count(lambda blk, trial=trial: blk >= trial)
        thr = jnp.where(cnt >= n_sel, trial, thr)

    need = n_sel - count(lambda blk: blk > thr)
    ur = lax.broadcasted_iota(jnp.int32, (tk, tk), 0)
    uc = lax.broadcasted_iota(jnp.int32, (tk, tk), 1)
    earlier = (uc < ur).astype(BF16)
    ones8 = jnp.ones((8, tk), BF16)

    def select_block(kb, seen):
        k0 = pl.multiple_of(kb * tk, tk)
        blk = rows8(key_ref[pl.ds(k0, tk), :])
        eq = blk == thr[None]
        eqb = jnp.where(eq, 1.0, 0.0).reshape(tk, tq).astype(BF16)
        rank = rows8(_dot(earlier, eqb)) + seen[None]
        sel = (blk > thr[None]) | (eq & (rank < need[None]))
        bias = jnp.where(sel, 0.0, NEG_BIG).reshape(tk, tq)
        bias_ref[pl.ds(k0, tk), :] = jnp.where(k0 + key_i <= qry_i, bias, NEG_BIG)
        return seen + _dot(ones8, eqb)

    lax.fori_loop(0, nkb, select_block, jnp.zeros((8, tq), F32))

    qdt = qdt_ref[0]
    zq = jnp.zeros((LANES - DSA_HEAD_DIM, tq), BF16)
    qz = jnp.concatenate([jnp.concatenate([qdt[hd * DSA_HEAD_DIM:(hd + 1) * DSA_HEAD_DIM, :], zq], axis=0)
                          for hd in range(DSA_HEADS)], axis=1)
    wide = DSA_HEADS * tq

    def att_block(kb, carry):
        m_i, l_i, acc = carry
        k0 = pl.multiple_of(kb * tk, tk)
        bias = bias_ref[pl.ds(k0, tk), :]
        s = _dot(dkv_ref[pl.ds(k0, tk), :], qz) + jnp.concatenate([bias] * DSA_HEADS, axis=1)
        m_new = jnp.maximum(m_i, jnp.max(s, axis=0, keepdims=True))
        p = jnp.exp2(s - m_new)
        alpha = jnp.exp2(m_i - m_new)
        l_new = alpha * l_i + jnp.sum(p, axis=0, keepdims=True)
        pv = _dot(dkvt_ref[0, DSA_HEAD_DIM:, pl.ds(k0, tk)], p.astype(BF16))
        return m_new, l_new, alpha * acc + pv

    init = (jnp.full((1, wide), NEG_BIG, F32), jnp.zeros((1, wide), F32), jnp.zeros((DSA_HEAD_DIM, wide), F32))
    _, l_f, acc = lax.fori_loop(0, nkb, att_block, init)
    o_all = acc / l_f
    o_ref[...] = jnp.concatenate([o_all[:, hd * tq:(hd + 1) * tq] for hd in range(DSA_HEADS)],
                                 axis=0).T.astype(BF16)


def _dsa(qdt, iqt, iwt, dkv, dkvt, ikw, batch, seq):
    tq, tk = DSA_TQ, DSA_TK
    nq = seq // tq
    n_sel = min(DSA_TOPK, seq // 4)
    return pl.pallas_call(
        functools.partial(_dsa_kernel, tq=tq, tk=tk, n_sel=n_sel),
        out_shape=jax.ShapeDtypeStruct((batch * seq, 256), BF16),
        grid=(batch, nq),
        in_specs=[pl.BlockSpec((1, 256, tq), lambda b, i: (b, 0, i)),
                  pl.BlockSpec((1, 128, tq), lambda b, i: (b, 0, i)),
                  pl.BlockSpec((1, 8, tq), lambda b, i: (b, 0, i)),
                  pl.BlockSpec((seq, 128), lambda b, i: (b, 0)),
                  pl.BlockSpec((1, 128, seq), lambda b, i: (b, 0, 0)),
                  pl.BlockSpec((seq, 128), lambda b, i: (b, 0))],
        out_specs=pl.BlockSpec((tq, 256), lambda b, i: (b * nq + i, 0)),
        scratch_shapes=[pltpu.VMEM((seq, tq), jnp.int32), pltpu.VMEM((seq, tq), F32)],
        compiler_params=_params(("parallel", "parallel")),
        name="dsa",
    )(qdt, iqt, iwt, dkv, dkvt, ikw)


def _mem_kv_kernel(mem_ref, g_ref, w_ref, kt_ref, v_ref):
    mn = _rms(mem_ref[0], g_ref[...]).astype(BF16)
    kv = _dot(mn, w_ref[...])
    kt_ref[0] = kv[:, :256].T.astype(BF16)
    v_ref[0] = kv[:, 256:].astype(BF16)


def _mem_kv(mem, gain, w_kv):
    b, m, d = mem.shape
    return pl.pallas_call(
        _mem_kv_kernel,
        out_shape=(jax.ShapeDtypeStruct((b, 256, m), BF16), jax.ShapeDtypeStruct((b, m, 256), BF16)),
        grid=(b,),
        in_specs=[pl.BlockSpec((1, m, d), lambda i: (i, 0, 0)), _const_spec((1, d)), _const_spec((d, 512))],
        out_specs=(pl.BlockSpec((1, 256, m), lambda i: (i, 0, 0)), pl.BlockSpec((1, m, 256), lambda i: (i, 0, 0))),
        compiler_params=_params(("parallel",)),
        name="mem_kv",
    )(mem, gain, w_kv)


def _merge_kernel(x_ref, ya_ref, yb_ref, yc_ref, yd_ref, gmix_ref, wg_ref, wbr_ref, wout_ref,
                  gq_ref, wq_ref, mkt_ref, mv_ref, wo_ref, gffn_ref, wr_ref, br_ref,
                  x2_ref, rt_ref, *, tm):
    x = x_ref[...]
    d = x.shape[-1]
    h = _rms(x, gmix_ref[...]).astype(BF16)
    merged = jnp.zeros((tm, d), F32)
    for n, y_ref in enumerate((ya_ref, yb_ref, yc_ref, yd_ref)):
        gate = _sigmoid(_dot(h, wg_ref[:, n * d:(n + 1) * d]))
        merged = merged + gate * _dot(y_ref[...], wbr_ref[n])
    x1 = x + _dot(merged.astype(BF16), wout_ref[...])
    h2 = _rms(x1, gq_ref[...]).astype(BF16)
    q = (_dot(h2, wq_ref[...]) * (MEM_HEAD_DIM ** -0.5)).astype(BF16)
    lane_head = lax.broadcasted_iota(jnp.int32, (tm, 256), 1) // MEM_HEAD_DIM
    mv = mv_ref[0]
    o = jnp.zeros((tm, 256), F32)
    for hd in range(MEM_HEADS):
        s = _dot(q[:, hd * MEM_HEAD_DIM:(hd + 1) * MEM_HEAD_DIM], mkt_ref[0, hd * MEM_HEAD_DIM:(hd + 1) * MEM_HEAD_DIM, :])
        p = jnp.exp(s - jnp.max(s, axis=-1, keepdims=True))
        p = p / jnp.sum(p, axis=-1, keepdims=True)
        o = o + jnp.where(lane_head == hd, _dot(p.astype(BF16), mv), 0.0)
    x2 = x1 + _dot(o.astype(BF16), wo_ref[...])
    x2_ref[...] = x2
    h3 = _rms(x2, gffn_ref[...]).astype(BF16)
    logits = _dot(h3, wr_ref[...]) + br_ref[...]
    lane = lax.broadcasted_iota(jnp.int32, (tm, LANES), 1)
    gl = jnp.where(lane < MOE_GROUPS, logits, -jnp.inf)
    gmax = jnp.max(gl, axis=-1, keepdims=True)
    gsel = jnp.min(jnp.where(gl == gmax, lane, LANES), axis=-1, keepdims=True)
    pg_sel = 1.0 / jnp.sum(jnp.exp(gl - gmax), axis=-1, keepdims=True)
    in_group = (lane - ROUTER_EXPERT_LANE) // MOE_EXPERTS_PER_GROUP == gsel
    el = jnp.where(in_group, logits, -jnp.inf)
    m1 = jnp.max(el, axis=-1, keepdims=True)
    i1 = jnp.min(jnp.where(el == m1, lane, LANES), axis=-1, keepdims=True)
    el2 = jnp.where(lane == i1, -jnp.inf, el)
    m2 = jnp.max(el2, axis=-1, keepdims=True)
    i2 = jnp.min(jnp.where(el2 == m2, lane, LANES), axis=-1, keepdims=True)
    e21 = jnp.exp(m2 - m1)
    c1 = pg_sel / (1.0 + e21)
    ids = jnp.where(lane == 0, i1, i2) - ROUTER_EXPERT_LANE
    rt_ref[...] = jnp.where(lane < 2, ids.astype(F32), jnp.where(lane == 2, c1, jnp.where(lane == 3, c1 * e21, 0.0)))


def _merge(x, ys, gmix, wg, wbr, wout, gq, wq, mkt, mv, wo, gffn, wr, br, batch, seq):
    n, d = x.shape
    tm = MERGE_TM
    spt = seq // tm
    m = mv.shape[1]
    tok = lambda w: pl.BlockSpec((tm, w), lambda i: (i, 0))
    return pl.pallas_call(
        functools.partial(_merge_kernel, tm=tm),
        out_shape=(jax.ShapeDtypeStruct((n, d), F32), jax.ShapeDtypeStruct((n, LANES), F32)),
        grid=(n // tm,),
        in_specs=[tok(d), tok(256), tok(256), tok(256), tok(256),
                  _const_spec((1, d)), _const_spec((d, N_BRANCH * d)), _const_spec((N_BRANCH, 256, d)),
                  _const_spec((d, d)), _const_spec((1, d)), _const_spec((d, 256)),
                  pl.BlockSpec((1, 256, m), lambda i: (i // spt, 0, 0)),
                  pl.BlockSpec((1, m, 256), lambda i: (i // spt, 0, 0)),
                  _const_spec((256, d)), _const_spec((1, d)), _const_spec((d, LANES)), _const_spec((1, LANES))],
        out_specs=(tok(d), tok(LANES)),
        compiler_params=_params(("parallel",)),
        name="merge_mem_router",
    )(x, *ys, gmix, wg, wbr, wout, gq, wq, mkt, mv, wo, gffn, wr, br)


def _sc_gather_rows(table, idx):
    _, width = table.shape
    total = idx.shape[0]
    chunk, nbuf = SC_GATHER_CHUNK, SC_GATHER_BUFS
    workers = SC_CORES * SC_SUBCORES
    per_w = total // workers
    nch = per_w // chunk
    assert total % (workers * chunk * nbuf) == 0
    mesh = plsc.VectorSubcoreMesh(core_axis_name="c", subcore_axis_name="s")

    @functools.partial(
        pl.kernel, mesh=mesh, out_type=jax.ShapeDtypeStruct((total, width), table.dtype),
        scratch_types=[pltpu.VMEM((nch, chunk), jnp.int32), pltpu.VMEM((nbuf, chunk, width), table.dtype),
                       pltpu.SemaphoreType.DMA((nbuf,)), pltpu.SemaphoreType.DMA((nbuf,))])
    def gather_kernel(table_hbm, idx_hbm, out_hbm, idx_v, rows_v, gsem, wsem):
        wid = lax.axis_index("s") * SC_CORES + lax.axis_index("c")
        pltpu.sync_copy(idx_hbm.at[wid], idx_v)

        def gather(j, slot):
            return pltpu.make_async_copy(table_hbm.at[idx_v.at[j]], rows_v.at[slot], gsem.at[slot])

        def write(j, slot):
            off = pl.multiple_of(wid * per_w + j * chunk, chunk)
            return pltpu.make_async_copy(rows_v.at[slot], out_hbm.at[pl.ds(off, chunk)], wsem.at[slot])

        for slot in range(nbuf):
            gather(slot, slot).start()

        @pl.loop(0, nch // nbuf)
        def _(g):
            for slot in range(nbuf):
                j = g * nbuf + slot
                gather(j, slot).wait()
                write(j, slot).start()
                write(j, slot).wait()

                @pl.when(j + nbuf < nch)
                def _():
                    gather(j + nbuf, slot).start()

    return gather_kernel(table, idx.reshape(workers, nch, chunk))


def _dispatch_plan(rt, n):
    m = 2 * n
    ne, blk = MOE_N_EXPERTS, MOE_BLOCK
    p_rows = m + ne * blk
    flat_e = rt[:, :2].astype(jnp.int32).reshape(m)
    onehot = (flat_e[:, None] == jnp.arange(ne, dtype=jnp.int32)[None, :]).astype(jnp.int32)
    seen = jnp.cumsum(onehot, axis=0)
    counts = seen[-1]
    padded = (counts + blk - 1) // blk * blk
    pend = jnp.cumsum(padded)
    pstart = pend - padded
    dest = jnp.sum(onehot * (seen - 1 + pstart[None, :]), axis=1)
    src = (jnp.arange(p_rows, dtype=jnp.int32) % n).at[dest].set(jnp.arange(m, dtype=jnp.int32) // 2,
                                                                 unique_indices=True)
    block_e = jnp.clip(jnp.searchsorted(pend, jnp.arange(p_rows // blk, dtype=jnp.int32) * blk, side='right'),
                       0, ne - 1).astype(jnp.int32)
    n_used = (pend[-1] // blk).astype(jnp.int32).reshape(1)
    return src, dest.astype(jnp.int32), block_e, n_used


def _expert_block_kernel(be_ref, nu_ref, xs_ref, g_ref, wg_ref, wu_ref, wd_ref, o_ref):
    del be_ref
    used = pl.program_id(0) < nu_ref[0]

    @pl.when(used)
    def _():
        h = _rms(xs_ref[...], g_ref[...]).astype(BF16)
        gt = _dot(h, wg_ref[0])
        hid = gt * _sigmoid(gt) * _dot(h, wu_ref[0])
        o_ref[...] = _dot(hid.astype(BF16), wd_ref[0])

    @pl.when(jnp.logical_not(used))
    def _():
        o_ref[...] = jnp.zeros_like(o_ref)


def _expert_blocks(xs, gain, wg, wu, wd, block_e, n_used):
    p_rows, d = xs.shape
    hid = wg.shape[-1]
    blk = MOE_BLOCK
    grid_spec = pltpu.PrefetchScalarGridSpec(
        num_scalar_prefetch=2, grid=(p_rows // blk,),
        in_specs=[pl.BlockSpec((blk, d), lambda b, be, nu: (b, 0)),
                  pl.BlockSpec((1, d), lambda b, be, nu: (0, 0)),
                  pl.BlockSpec((1, d, hid), lambda b, be, nu: (be[b], 0, 0)),
                  pl.BlockSpec((1, d, hid), lambda b, be, nu: (be[b], 0, 0)),
                  pl.BlockSpec((1, hid, d), lambda b, be, nu: (be[b], 0, 0))],
        out_specs=pl.BlockSpec((blk, d), lambda b, be, nu: (b, 0)))
    return pl.pallas_call(
        _expert_block_kernel, out_shape=jax.ShapeDtypeStruct((p_rows, d), F32), grid_spec=grid_spec,
        compiler_params=_params(("parallel",)), name="moe_expert_blocks",
    )(block_e, n_used, xs, gain, wg, wu, wd)


def _combine_kernel(x_ref, y_ref, rt_ref, gfin_ref, o_ref, *, final_norm):
    d = x_ref.shape[-1]
    out = x_ref[...] + rt_ref[:, 2:3] * y_ref[:, :d] + rt_ref[:, 3:4] * y_ref[:, d:]
    o_ref[...] = _rms(out, gfin_ref[...]) if final_norm else out


def _combine(x2, y_pairs, rt, gfin, final_norm):
    n, d = x2.shape
    tm = COMBINE_TM
    return pl.pallas_call(
        functools.partial(_combine_kernel, final_norm=final_norm),
        out_shape=jax.ShapeDtypeStruct((n, d), F32),
        grid=(n // tm,),
        in_specs=[pl.BlockSpec((tm, d), lambda i: (i, 0)), pl.BlockSpec((tm, 2 * d), lambda i: (i, 0)),
                  pl.BlockSpec((tm, LANES), lambda i: (i, 0)), _const_spec((1, d))],
        out_specs=pl.BlockSpec((tm, d), lambda i: (i, 0)),
        compiler_params=_params(("parallel",)), name="moe_combine",
    )(x2, y_pairs, rt, gfin)


def _moe(x2, rt, gffn, wg, wu, wd, gfin, final_norm):
    n, d = x2.shape
    src, dest, block_e, n_used = _dispatch_plan(rt, n)
    xs = _sc_gather_rows(x2, src)
    yb = _expert_blocks(xs, gffn, wg, wu, wd, block_e, n_used)
    y_pairs = _sc_gather_rows(yb, dest).reshape(n, 2 * d)
    return _combine(x2, y_pairs, rt, gfin, final_norm)


def kernel(x, mem, positions, norm_mix, w_in, diff_lambda, hgrn_lb_logits, spatial_w, spatial_b, w_branch, w_out,
           norm_mem_q, norm_mem_kv, w_mem_q, w_mem_kv, w_mem_o, norm_ffn, w_router_group, b_router_group,
           w_router_expert, b_router_expert, w_exp_gate, w_exp_up, w_exp_down, norm_final):
    batch, seq, d = x.shape
    depth = w_in.shape[0]
    n = batch * seq
    xf = x.reshape(n, d)
    tabs = _rope_tables(positions)
    row = lambda v: v.reshape(1, -1).astype(F32)
    for l in range(depth):
        lam_init = 0.8 - 0.6 * math.exp(-0.3 * l)
        w1, w_gate = _split_w_in(w_in, l)
        sw = spatial_w[l].reshape(SGU_GROUPS * SGU_CHUNK, SGU_CHUNK)
        sb = jnp.repeat(spatial_b[l].T, SGU_GROUP_DIM, axis=1)
        qat, ka, vat, hb, y_c, qdt, iqt, dkv, dkvt, ikw, iwt = _projection(
            xf, row(norm_mix[l]), w1, tabs, sw, sb, batch, seq)
        y_a = _diff_attention(diff_lambda[l], qat, ka, vat, lam_init, batch, seq)
        y_b = _hgrn(hgrn_lb_logits, hb, l, batch, seq)
        y_d = _dsa(qdt, iqt, iwt, dkv, dkvt, ikw, batch, seq)
        mkt, mv = _mem_kv(mem, row(norm_mem_kv[l]), w_mem_kv[l].astype(BF16))
        e0, e1 = ROUTER_EXPERT_LANE, ROUTER_EXPERT_LANE + MOE_N_EXPERTS
        wr = jnp.zeros((d, LANES), F32)
        wr = wr.at[:, :MOE_GROUPS].set(w_router_group[l]).at[:, e0:e1].set(w_router_expert[l]).astype(BF16)
        br = jnp.zeros((1, LANES), F32)
        br = br.at[0, :MOE_GROUPS].set(b_router_group[l]).at[0, e0:e1].set(b_router_expert[l])
        x2, rt = _merge(xf, (y_a, y_b, y_c, y_d), row(norm_mix[l]), w_gate, w_branch[l].astype(BF16),
                        w_out[l].astype(BF16), row(norm_mem_q[l]), w_mem_q[l].astype(BF16), mkt, mv,
                        w_mem_o[l].astype(BF16), row(norm_ffn[l]), wr, br, batch, seq)
        xf = _moe(x2, rt, row(norm_ffn[l]), w_exp_gate[l].astype(BF16), w_exp_up[l].astype(BF16),
                  w_exp_down[l].astype(BF16), row(norm_final), final_norm=(l == depth - 1))
    return xf.reshape(batch, seq, d)
```

```python
import functools
import math

import numpy as np
import jax
import jax.numpy as jnp
from jax import lax
from jax.experimental import pallas as pl
from jax.experimental.pallas import tpu as pltpu
from jax.experimental.pallas import tpu_sc as plsc

F32 = jnp.float32
BF16 = jnp.bfloat16

NORM_EPS = 1e-6
ROPE_THETA = 10000.0
NEG_BIG = -1e30

N_BRANCH = 4
BRANCH_WIDTH = 256
DIFF_HEADS = 4
DIFF_HEAD_DIM = 32
HGRN_HEADS = 4
HGRN_DIM = 64
HGRN_CHUNK = 64
HGRN_MIN_FORGET = 1e-30
SGU_GROUPS = 4
SGU_GROUP_DIM = 64
SGU_CHUNK = 128
DSA_HEADS = 4
DSA_HEAD_DIM = 64
DSA_IDX_HEADS = 4
DSA_IDX_DIM = 32
DSA_TOPK = 256
MEM_HEADS = 4
MEM_HEAD_DIM = 64
MOE_GROUPS = 4
MOE_EXPERTS_PER_GROUP = 8
MOE_N_EXPERTS = 32
MOE_BLOCK = 256
ROUTER_EXPERT_LANE = 32
SC_CORES = 2
SC_SUBCORES = 16
SC_GATHER_CHUNK = 16
SC_GATHER_BUFS = 4

LANES = 128
VMEM_LIMIT = 56 * 1024 * 1024

PROJ_TM = 256
DIFF_TQ = 256
DIFF_TK = 512
HGRN_TC = 512
DSA_TQ = 256
DSA_TK = 512
MERGE_TM = 512
COMBINE_TM = 512
ROPE_TM = 1024

C_AQ, C_AK, C_AV = 0, 256, 512
C_HB = 768
C_UV = 1792
C_DQ = 2304
C_DKV = 2560
C_IQ = 2688
C_IKW = 2816
IW_LANE = 32
C_GATE = 2852
C_TOTAL = 2944
LOG2E = math.log2(math.e)


def _params(sem):
    return pltpu.CompilerParams(dimension_semantics=sem, vmem_limit_bytes=VMEM_LIMIT)


def _const_spec(shape):
    nd = len(shape)
    return pl.BlockSpec(shape, lambda *_: (0,) * nd, pipeline_mode=pl.Buffered(1))


def _rms(xf, gain=None):
    y = xf * lax.rsqrt(jnp.mean(xf * xf, axis=-1, keepdims=True) + NORM_EPS)
    return y if gain is None else y * gain


def _sigmoid(x):
    return 0.5 * jnp.tanh(0.5 * x) + 0.5


def _dot(a, b):
    return jnp.dot(a, b, preferred_element_type=F32)


def _dot_nt(a, b):
    return lax.dot_general(a, b, (((1,), (1,)), ((), ())), preferred_element_type=F32)


def _rope_table_kernel(pos_ref, frq_ref, sgn_ref, c32_ref, s32_ref, c64_ref, s64_ref):
    pos = pos_ref[...].astype(F32)
    a32 = pos * frq_ref[0:1, :]
    a64 = pos * frq_ref[1:2, :]
    c32_ref[...] = jnp.cos(a32)
    s32_ref[...] = jnp.sin(a32) * sgn_ref[0:1, :]
    c64_ref[...] = jnp.cos(a64)
    s64_ref[...] = jnp.sin(a64) * sgn_ref[1:2, :]


def _rope_tables(positions):
    n = positions.size
    pos = positions.reshape(n, 1).astype(jnp.int32)
    lane = np.arange(256)
    inv32 = ROPE_THETA ** (-jnp.arange(16, dtype=F32) * (2.0 / 32))
    inv64 = ROPE_THETA ** (-jnp.arange(32, dtype=F32) * (2.0 / 64))
    frq = jnp.stack([inv32[lane % 16], inv64[lane % 32]])
    sgn = jnp.asarray(np.stack([np.where(lane % 32 < 16, -1.0, 1.0),
                                np.where(lane % 64 < 32, -1.0, 1.0)]), F32)
    tm = ROPE_TM
    tab = jax.ShapeDtypeStruct((n, 256), F32)
    return pl.pallas_call(
        _rope_table_kernel,
        out_shape=(tab, tab, tab, tab),
        grid=(n // tm,),
        in_specs=[pl.BlockSpec((tm, 1), lambda i: (i, 0)), _const_spec((2, 256)), _const_spec((2, 256))],
        out_specs=tuple(pl.BlockSpec((tm, 256), lambda i: (i, 0)) for _ in range(4)),
        compiler_params=_params(("parallel",)),
        name="rope_tables",
    )(pos, frq, sgn)


def _split_w_in_kernel(w_ref, w1_ref, wg_ref):
    w1_ref[...] = w_ref[:, :C_TOTAL].astype(BF16)
    a0 = (C_GATE // LANES) * LANES
    tail = w_ref[:, a0:]
    wg_ref[...] = tail[:, C_GATE - a0:].astype(BF16)


def _split_w_in(w_in, layer):
    _, d, width = w_in.shape
    tr = 128
    return pl.pallas_call(
        _split_w_in_kernel,
        out_shape=(jax.ShapeDtypeStruct((d, C_TOTAL), BF16), jax.ShapeDtypeStruct((d, width - C_GATE), BF16)),
        grid=(d // tr,),
        in_specs=[pl.BlockSpec((None, tr, width), lambda i: (layer, i, 0))],
        out_specs=(pl.BlockSpec((tr, C_TOTAL), lambda i: (i, 0)), pl.BlockSpec((tr, width - C_GATE), lambda i: (i, 0))),
        compiler_params=_params(("parallel",)),
        name="split_w_in",
    )(w_in)


def _gelu_tanh(x):
    return 0.5 * x * (1.0 + jnp.tanh(math.sqrt(2.0 / math.pi) * (x + 0.044715 * (x * x * x))))


def _rope(x, cos, sin_signed, half):
    w = x.shape[-1]
    lane = lax.broadcasted_iota(jnp.int32, x.shape, 1)
    partner = jnp.where(lane % (2 * half) < half, pltpu.roll(x, w - half, 1), pltpu.roll(x, half, 1))
    return x * cos + partner * sin_signed


def _proj_kernel(x_ref, g_ref, w_ref, c32_ref, s32_ref, c64_ref, s64_ref, sw_ref, sb_ref,
                 qat_ref, ka_ref, vat_ref, hb_ref, yc_ref, qdt_ref, iqt_ref, dkv_ref, dkvt_ref, ikw_ref, iwt_ref,
                 *, tm):
    h = _rms(x_ref[...], g_ref[...]).astype(BF16)

    def proj(c0, width):
        return _dot(h, w_ref[:, c0:c0 + width])

    c32, s32, c64, s64 = c32_ref[...], s32_ref[...], c64_ref[...], s64_ref[...]
    qat_ref[0] = (_rope(proj(C_AQ, 256), c32, s32, 16) * (DIFF_HEAD_DIM ** -0.5 * LOG2E)).T.astype(BF16)
    ka_ref[...] = _rope(proj(C_AK, 256), c32, s32, 16).astype(BF16)
    vat_ref[0] = proj(C_AV, 256).astype(BF16).T
    hb_ref[...] = proj(C_HB, 1024)
    qdt_ref[0] = (_rope(proj(C_DQ, 256), c64, s64, 32) * (DSA_HEAD_DIM ** -0.5 * LOG2E)).T.astype(BF16)
    iqt_ref[0] = _rope(proj(C_IQ, 128), c32[:, :128], s32[:, :128], 16).T.astype(BF16)
    lane = lax.broadcasted_iota(jnp.int32, (tm, 128), 1)
    is_k = lane < DSA_HEAD_DIM
    dkv = _rope(proj(C_DKV, 128), jnp.where(is_k, c64[:, :128], 1.0), jnp.where(is_k, s64[:, :128], 0.0), 32)
    is_ik = lane < DSA_IDX_DIM
    ikw = _rope(proj(C_IKW, 128), jnp.where(is_ik, c32[:, :128], 1.0), jnp.where(is_ik, s32[:, :128], 0.0), 16)
    dkv_ref[...] = dkv.astype(BF16)
    dkvt_ref[0] = dkv.T.astype(BF16)
    ikw_ref[...] = ikw.astype(BF16)
    iw_scale = DSA_IDX_HEADS ** -0.5 * DSA_IDX_DIM ** -0.5
    iwt_ref[0] = (ikw * iw_scale).T[IW_LANE:IW_LANE + 8, :]
    uv = _gelu_tanh(proj(C_UV, 512))
    u, v = uv[:, :256], uv[:, 256:]
    mu = jnp.mean(v, axis=-1, keepdims=True)
    vc = v - mu
    vn = (vc * lax.rsqrt(jnp.mean(vc * vc, axis=-1, keepdims=True) + NORM_EPS)).astype(BF16)
    r = lax.broadcasted_iota(jnp.int32, (SGU_GROUPS * SGU_CHUNK, SGU_CHUNK), 0)
    c = lax.broadcasted_iota(jnp.int32, (SGU_GROUPS * SGU_CHUNK, SGU_CHUNK), 1)
    wt = jnp.where((r % SGU_CHUNK) >= c, sw_ref[...], 0.0).astype(BF16)
    lane_grp = lax.broadcasted_iota(jnp.int32, (SGU_CHUNK, 256), 1) // SGU_GROUP_DIM
    for ch in range(tm // SGU_CHUNK):
        r0 = ch * SGU_CHUNK
        full = _dot(wt, vn[r0:r0 + SGU_CHUNK, :])
        mixed = sb_ref[...]
        for g in range(SGU_GROUPS):
            mixed = mixed + jnp.where(lane_grp == g, full[g * SGU_CHUNK:(g + 1) * SGU_CHUNK, :], 0.0)
        yc_ref[r0:r0 + SGU_CHUNK, :] = (u[r0:r0 + SGU_CHUNK, :] * mixed).astype(BF16)


def _projection(x, gain, w1, tabs, sw, sb, batch, seq):
    n, d = x.shape
    tm = PROJ_TM
    spt = seq // tm
    tok = lambda w: pl.BlockSpec((tm, w), lambda i: (i, 0))
    tr = lambda rows: pl.BlockSpec((1, rows, tm), lambda i: (i // spt, 0, i % spt))
    out_shape = (
        jax.ShapeDtypeStruct((batch, 256, seq), BF16),
        jax.ShapeDtypeStruct((n, 256), BF16),
        jax.ShapeDtypeStruct((batch, 256, seq), BF16),
        jax.ShapeDtypeStruct((n, 1024), F32),
        jax.ShapeDtypeStruct((n, 256), BF16),
        jax.ShapeDtypeStruct((batch, 256, seq), BF16),
        jax.ShapeDtypeStruct((batch, 128, seq), BF16),
        jax.ShapeDtypeStruct((n, 128), BF16),
        jax.ShapeDtypeStruct((batch, 128, seq), BF16),
        jax.ShapeDtypeStruct((n, 128), BF16),
        jax.ShapeDtypeStruct((batch, 8, seq), F32),
    )
    return pl.pallas_call(
        functools.partial(_proj_kernel, tm=tm),
        out_shape=out_shape,
        grid=(n // tm,),
        in_specs=[tok(d), _const_spec((1, d)), _const_spec((d, C_TOTAL)),
                  tok(256), tok(256), tok(256), tok(256),
                  _const_spec((SGU_GROUPS * SGU_CHUNK, SGU_CHUNK)), _const_spec((SGU_CHUNK, 256))],
        out_specs=(tr(256), tok(256), tr(256), tok(1024), tok(256), tr(256), tr(128), tok(128), tr(128), tok(128), tr(8)),
        compiler_params=_params(("parallel",)),
        name="projection",
    )(x, gain, w1, *tabs, sw, sb)


def _diff_attn_kernel(lam_ref, qt_ref, k_ref, vt_ref, o_ref, *, lam_init, tq, tk):
    q0 = pl.program_id(1) * tq
    kb_diag = q0 // tk
    lv = lam_ref[...]
    lam = (jnp.exp(jnp.sum(lv[0:1] * lv[1:2], axis=-1, keepdims=True))
           - jnp.exp(jnp.sum(lv[2:3] * lv[3:4], axis=-1, keepdims=True)) + lam_init)
    qt = qt_ref[0]
    feat = lax.broadcasted_iota(jnp.int32, (256, tq), 0) // DIFF_HEAD_DIM
    n_maps = 2 * DIFF_HEADS
    qz = jnp.concatenate([jnp.where(feat == i, qt, jnp.zeros_like(qt)) for i in range(n_maps)], axis=1)
    wide = n_maps * tq
    key_i = lax.broadcasted_iota(jnp.int32, (tk, wide), 0)
    qry_i = q0 + lax.broadcasted_iota(jnp.int32, (tk, wide), 1) % tq

    def step(kb, carry, masked):
        m_i, l_i, acc = carry
        k0 = pl.multiple_of(kb * tk, tk)
        s = _dot(k_ref[pl.ds(k0, tk), :], qz)
        if masked:
            s = jnp.where(k0 + key_i <= qry_i, s, NEG_BIG)
        m_new = jnp.maximum(m_i, jnp.max(s, axis=0, keepdims=True))
        p = jnp.exp2(s - m_new)
        alpha = jnp.exp2(m_i - m_new)
        l_new = alpha * l_i + jnp.sum(p, axis=0, keepdims=True)
        pb = p.astype(BF16)
        pv = jnp.concatenate(
            [_dot(vt_ref[0, hd * 64:(hd + 1) * 64, pl.ds(k0, tk)], pb[:, 2 * hd * tq:(2 * hd + 2) * tq])
             for hd in range(DIFF_HEADS)], axis=1)
        return m_new, l_new, alpha * acc + pv

    init = (jnp.full((1, wide), NEG_BIG, F32), jnp.zeros((1, wide), F32), jnp.zeros((64, wide), F32))
    carry = lax.fori_loop(0, kb_diag, functools.partial(step, masked=False), init)
    _, l_f, acc = step(kb_diag, carry, True)
    o_all = acc / l_f
    heads = []
    for hd in range(DIFF_HEADS):
        o0 = o_all[:, 2 * hd * tq:(2 * hd + 1) * tq]
        o1 = o_all[:, (2 * hd + 1) * tq:(2 * hd + 2) * tq]
        o_h = o0 - lam * o1
        ms = jnp.mean(o_h * o_h, axis=0, keepdims=True)
        heads.append(o_h * lax.rsqrt(ms + NORM_EPS) * (1.0 - lam_init))
    o_ref[...] = jnp.concatenate(heads, axis=0).T.astype(BF16)


def _diff_attention(lam_vec, qat, ka, vat, lam_init, batch, seq):
    tq, tk = DIFF_TQ, DIFF_TK
    nq = seq // tq
    return pl.pallas_call(
        functools.partial(_diff_attn_kernel, lam_init=lam_init, tq=tq, tk=tk),
        out_shape=jax.ShapeDtypeStruct((batch * seq, 256), BF16),
        grid=(batch, nq),
        in_specs=[_const_spec((4, DIFF_HEAD_DIM)),
                  pl.BlockSpec((1, 256, tq), lambda b, i: (b, 0, i)),
                  pl.BlockSpec((seq, 256), lambda b, i: (b, 0)),
                  pl.BlockSpec((1, 256, seq), lambda b, i: (b, 0, 0))],
        out_specs=pl.BlockSpec((tq, 256), lambda b, i: (b * nq + i, 0)),
        compiler_params=_params(("parallel", "parallel")),
        name="diff_attention",
    )(lam_vec, qat, ka, vat)


def _hgrn_kernel(lbl_ref, hb_ref, o_ref, st_ref, pstk_ref, *, layer, tc):
    cz = HGRN_CHUNK
    w = 256

    @pl.when(pl.program_id(1) == 0)
    def _():
        st_ref[...] = jnp.zeros_like(st_ref)

    lg = lbl_ref[...]
    e = jnp.exp(lg - jnp.max(lg, axis=0, keepdims=True))
    lw = e / jnp.sum(e, axis=0, keepdims=True)
    lb = jnp.sum(lw[0:layer + 1], axis=0, keepdims=True) - lw[0:1]

    ri = lax.broadcasted_iota(jnp.int32, (cz, cz), 0)
    ci = lax.broadcasted_iota(jnp.int32, (cz, cz), 1)
    tri = (ri >= ci).astype(F32)
    rb = lax.broadcasted_iota(jnp.int32, (w, w), 0) // HGRN_DIM
    cb = lax.broadcasted_iota(jnp.int32, (w, w), 1) // HGRN_DIM
    same_head = rb == cb
    head_ones = same_head.astype(BF16)
    trows = {r: r + lax.broadcasted_iota(jnp.int32, (cz - r, w), 0) for r in range(0, cz, 16)}

    def chunk(c, carry):
        r0 = pl.multiple_of(c * cz, cz)
        q = hb_ref[pl.ds(r0, cz), 0:256]
        fp = hb_ref[pl.ds(r0, cz), 256:512]
        v = hb_ref[pl.ds(r0, cz), 512:768]
        g = hb_ref[pl.ds(r0, cz), 768:1024]
        qf = q * _sigmoid(q)
        f = lb + (1.0 - lb) * jax.nn.sigmoid(fp)
        log_f = jnp.log(jnp.maximum(f, HGRN_MIN_FORGET))
        kf = (1.0 - lb) * jax.nn.sigmoid(-fp)
        bc = jnp.dot(tri, log_f, preferred_element_type=F32, precision=lax.Precision.HIGHEST)
        st = st_ref[...]
        o = _dot_nt((qf * jnp.exp(bc)).astype(BF16), st.astype(BF16))
        for s in range(cz):
            r_lo = (s // 16) * 16
            arg = bc[r_lo:, :] - bc[s:s + 1, :]
            if s > r_lo:
                arg = jnp.where(trows[r_lo] >= s, arg, NEG_BIG)
            p = qf[r_lo:, :] * kf[s:s + 1, :] * jnp.exp(arg)
            if r_lo:
                pstk_ref[s * cz:s * cz + r_lo, :] = jnp.zeros((r_lo, w), BF16)
            pstk_ref[s * cz + r_lo:(s + 1) * cz, :] = p.astype(BF16)
        accs = [jnp.zeros((16, w), F32) for _ in range(cz // 16)]
        for sg in range(cz // 16):
            att = _dot(pstk_ref[sg * 16 * cz:(sg + 1) * 16 * cz, :], head_ones)
            for sl in range(16):
                s = sg * 16 + sl
                for j in range(sg, cz // 16):
                    accs[j] = accs[j] + att[sl * cz + 16 * j:sl * cz + 16 * j + 16, :] * v[s:s + 1, :]
        o = o + jnp.concatenate(accs, axis=0)
        b_end = bc[cz - 1:cz, :]
        kd = kf * jnp.exp(b_end - bc)
        upd = _dot(v.T.astype(BF16), kd.astype(BF16))
        st_ref[...] = st * jnp.exp(b_end) + jnp.where(same_head, upd, 0.0)
        ms = _dot(o * o, head_ones.astype(F32)) * (1.0 / HGRN_DIM)
        y = o * lax.rsqrt(ms + NORM_EPS)
        o_ref[pl.ds(r0, cz), :] = (y * (g * _sigmoid(g))).astype(BF16)
        return carry

    lax.fori_loop(0, tc // cz, chunk, 0)


def _hgrn(lb_logits, hb, layer, batch, seq):
    tc = HGRN_TC
    nt = seq // tc
    cz = HGRN_CHUNK
    return pl.pallas_call(
        functools.partial(_hgrn_kernel, layer=layer, tc=tc),
        out_shape=jax.ShapeDtypeStruct((batch * seq, 256), BF16),
        grid=(batch, nt),
        in_specs=[_const_spec(lb_logits.shape),
                  pl.BlockSpec((tc, 1024), lambda b, i: (b * nt + i, 0))],
        out_specs=pl.BlockSpec((tc, 256), lambda b, i: (b * nt + i, 0)),
        scratch_shapes=[pltpu.VMEM((256, 256), F32), pltpu.VMEM((cz * cz, 256), BF16)],
        compiler_params=_params(("parallel", "arbitrary")),
        name="hgrn2",
    )(lb_logits, hb)


def _dsa_kernel(qdt_ref, iqt_ref, iwt_ref, dkv_ref, dkvt_ref, ikw_ref, o_ref, key_ref, bias_ref, *, tq, tk, n_sel):
    q0 = pl.program_id(1) * tq
    nkb = q0 // tk + 1
    key_i = lax.broadcasted_iota(jnp.int32, (tk, tq), 0)
    qry_i = q0 + lax.broadcasted_iota(jnp.int32, (tk, tq), 1)
    grp = tk // 8
    rows8 = lambda x: x.reshape(grp, 8, tq)
    iqt = iqt_ref[0]
    zpad = jnp.zeros((LANES - DSA_IDX_DIM, tq), BF16)
    iqz = jnp.concatenate([jnp.concatenate([iqt[hd * DSA_IDX_DIM:(hd + 1) * DSA_IDX_DIM, :], zpad], axis=0)
                           for hd in range(DSA_IDX_HEADS)], axis=1)
    iw = iwt_ref[0]

    def score_block(kb, carry):
        k0 = pl.multiple_of(kb * tk, tk)
        sh = jnp.maximum(_dot(ikw_ref[pl.ds(k0, tk), :], iqz), 0.0)
        sc = jnp.zeros((tk, tq), F32)
        for hd in range(DSA_IDX_HEADS):
            sc = sc + sh[:, hd * tq:(hd + 1) * tq] * iw[hd:hd + 1, :]
        sc = jnp.where(k0 + key_i <= qry_i, sc + 0.0, -jnp.inf)
        bits = pltpu.bitcast(sc, jnp.int32)
        key_ref[pl.ds(k0, tk), :] = jnp.where(bits < 0, bits ^ jnp.int32(0x7FFFFFFF), bits)
        return carry

    lax.fori_loop(0, nkb, score_block, 0)

    def count(pred_fn):
        def body(kb, acc):
            k0 = pl.multiple_of(kb * tk, tk)
            parts = None
            for c in range(tk // 64):
                blk = key_ref[pl.ds(k0 + 64 * c, 64), :].reshape(8, 8, tq)
                hit = jnp.where(pred_fn(blk), 1.0, 0.0)
                parts = [hit[j] if parts is None else parts[j] + hit[j] for j in range(8)]
            while len(parts) > 1:
                parts = [a + b for a, b in zip(parts[0::2], parts[1::2])]
            return acc + parts[0]
        acc = lax.fori_loop(0, nkb, body, jnp.zeros((8, tq), F32))
        return jnp.broadcast_to(jnp.sum(acc, axis=0, keepdims=True), (8, tq))

    thr = jnp.full((8, tq), np.int32(-2 ** 31), jnp.int32)
    for bit in range(31, -1, -1):
        trial = thr + np.int32(-2 ** 31 if bit == 31 else 2 ** bit)
        cnt = count(lambda blk, trial=trial: blk >= trial)
        thr = jnp.where(cnt >= n_sel, trial, thr)

    need = n_sel - count(lambda blk: blk > thr)
    ur = lax.broadcasted_iota(jnp.int32, (tk, tk), 0)
    uc = lax.broadcasted_iota(jnp.int32, (tk, tk), 1)
    earlier = (uc < ur).astype(BF16)
    ones8 = jnp.ones((8, tk), BF16)

    def select_block(kb, seen):
        k0 = pl.multiple_of(kb * tk, tk)
        blk = rows8(key_ref[pl.ds(k0, tk), :])
        eq = blk == thr[None]
        eqb = jnp.where(eq, 1.0, 0.0).reshape(tk, tq).astype(BF16)
        rank = rows8(_dot(earlier, eqb)) + seen[None]
        sel = (blk > thr[None]) | (eq & (rank < need[None]))
        bias = jnp.where(sel, 0.0, NEG_BIG).reshape(tk, tq)
        bias_ref[pl.ds(k0, tk), :] = jnp.where(k0 + key_i <= qry_i, bias, NEG_BIG)
        return seen + _dot(ones8, eqb)

    lax.fori_loop(0, nkb, select_block, jnp.zeros((8, tq), F32))

    qdt = qdt_ref[0]
    zq = jnp.zeros((LANES - DSA_HEAD_DIM, tq), BF16)
    qz = jnp.concatenate([jnp.concatenate([qdt[hd * DSA_HEAD_DIM:(hd + 1) * DSA_HEAD_DIM, :], zq], axis=0)
                          for hd in range(DSA_HEADS)], axis=1)
    wide = DSA_HEADS * tq

    def att_block(kb, carry):
        m_i, l_i, acc = carry
        k0 = pl.multiple_of(kb * tk, tk)
        bias = bias_ref[pl.ds(k0, tk), :]
        s = _dot(dkv_ref[pl.ds(k0, tk), :], qz) + jnp.concatenate([bias] * DSA_HEADS, axis=1)
        m_new = jnp.maximum(m_i, jnp.max(s, axis=0, keepdims=True))
        p = jnp.exp2(s - m_new)
        alpha = jnp.exp2(m_i - m_new)
        l_new = alpha * l_i + jnp.sum(p, axis=0, keepdims=True)
        pv = _dot(dkvt_ref[0, DSA_HEAD_DIM:, pl.ds(k0, tk)], p.astype(BF16))
        return m_new, l_new, alpha * acc + pv

    init = (jnp.full((1, wide), NEG_BIG, F32), jnp.zeros((1, wide), F32), jnp.zeros((DSA_HEAD_DIM, wide), F32))
    _, l_f, acc = lax.fori_loop(0, nkb, att_block, init)
    o_all = acc / l_f
    o_ref[...] = jnp.concatenate([o_all[:, hd * tq:(hd + 1) * tq] for hd in range(DSA_HEADS)],
                                 axis=0).T.astype(BF16)


def _dsa(qdt, iqt, iwt, dkv, dkvt, ikw, batch, seq):
    tq, tk = DSA_TQ, DSA_TK
    nq = seq // tq
    n_sel = min(DSA_TOPK, seq // 4)
    return pl.pallas_call(
        functools.partial(_dsa_kernel, tq=tq, tk=tk, n_sel=n_sel),
        out_shape=jax.ShapeDtypeStruct((batch * seq, 256), BF16),
        grid=(batch, nq),
        in_specs=[pl.BlockSpec((1, 256, tq), lambda b, i: (b, 0, i)),
                  pl.BlockSpec((1, 128, tq), lambda b, i: (b, 0, i)),
                  pl.BlockSpec((1, 8, tq), lambda b, i: (b, 0, i)),
                  pl.BlockSpec((seq, 128), lambda b, i: (b, 0)),
                  pl.BlockSpec((1, 128, seq), lambda b, i: (b, 0, 0)),
                  pl.BlockSpec((seq, 128), lambda b, i: (b, 0))],
        out_specs=pl.BlockSpec((tq, 256), lambda b, i: (b * nq + i, 0)),
        scratch_shapes=[pltpu.VMEM((seq, tq), jnp.int32), pltpu.VMEM((seq, tq), F32)],
        compiler_params=_params(("parallel", "parallel")),
        name="dsa",
    )(qdt, iqt, iwt, dkv, dkvt, ikw)


def _mem_kv_kernel(mem_ref, g_ref, w_ref, kt_ref, v_ref):
    mn = _rms(mem_ref[0], g_ref[...]).astype(BF16)
    kv = _dot(mn, w_ref[...])
    kt_ref[0] = kv[:, :256].T.astype(BF16)
    v_ref[0] = kv[:, 256:].astype(BF16)


def _mem_kv(mem, gain, w_kv):
    b, m, d = mem.shape
    return pl.pallas_call(
        _mem_kv_kernel,
        out_shape=(jax.ShapeDtypeStruct((b, 256, m), BF16), jax.ShapeDtypeStruct((b, m, 256), BF16)),
        grid=(b,),
        in_specs=[pl.BlockSpec((1, m, d), lambda i: (i, 0, 0)), _const_spec((1, d)), _const_spec((d, 512))],
        out_specs=(pl.BlockSpec((1, 256, m), lambda i: (i, 0, 0)), pl.BlockSpec((1, m, 256), lambda i: (i, 0, 0))),
        compiler_params=_params(("parallel",)),
        name="mem_kv",
    )(mem, gain, w_kv)


def _merge_kernel(x_ref, ya_ref, yb_ref, yc_ref, yd_ref, gmix_ref, wg_ref, wbr_ref, wout_ref,
                  gq_ref, wq_ref, mkt_ref, mv_ref, wo_ref, gffn_ref, wr_ref, br_ref,
                  x2_ref, rt_ref, cnt_ref, run_ref, *, tm):
    x = x_ref[...]
    d = x.shape[-1]
    h = _rms(x, gmix_ref[...]).astype(BF16)
    merged = jnp.zeros((tm, d), F32)
    for n, y_ref in enumerate((ya_ref, yb_ref, yc_ref, yd_ref)):
        gate = _sigmoid(_dot(h, wg_ref[:, n * d:(n + 1) * d]))
        merged = merged + gate * _dot(y_ref[...], wbr_ref[n])
    x1 = x + _dot(merged.astype(BF16), wout_ref[...])
    h2 = _rms(x1, gq_ref[...]).astype(BF16)
    q = (_dot(h2, wq_ref[...]) * (MEM_HEAD_DIM ** -0.5)).astype(BF16)
    lane_head = lax.broadcasted_iota(jnp.int32, (tm, 256), 1) // MEM_HEAD_DIM
    mv = mv_ref[0]
    o = jnp.zeros((tm, 256), F32)
    for hd in range(MEM_HEADS):
        s = _dot(q[:, hd * MEM_HEAD_DIM:(hd + 1) * MEM_HEAD_DIM], mkt_ref[0, hd * MEM_HEAD_DIM:(hd + 1) * MEM_HEAD_DIM, :])
        p = jnp.exp(s - jnp.max(s, axis=-1, keepdims=True))
        p = p / jnp.sum(p, axis=-1, keepdims=True)
        o = o + jnp.where(lane_head == hd, _dot(p.astype(BF16), mv), 0.0)
    x2 = x1 + _dot(o.astype(BF16), wo_ref[...])
    x2_ref[...] = x2
    h3 = _rms(x2, gffn_ref[...]).astype(BF16)
    logits = _dot(h3, wr_ref[...]) + br_ref[...]
    lane = lax.broadcasted_iota(jnp.int32, (tm, LANES), 1)
    gl = jnp.where(lane < MOE_GROUPS, logits, -jnp.inf)
    gmax = jnp.max(gl, axis=-1, keepdims=True)
    gsel = jnp.min(jnp.where(gl == gmax, lane, LANES), axis=-1, keepdims=True)
    pg_sel = 1.0 / jnp.sum(jnp.exp(gl - gmax), axis=-1, keepdims=True)
    in_group = (lane - ROUTER_EXPERT_LANE) // MOE_EXPERTS_PER_GROUP == gsel
    el = jnp.where(in_group, logits, -jnp.inf)
    m1 = jnp.max(el, axis=-1, keepdims=True)
    i1 = jnp.min(jnp.where(el == m1, lane, LANES), axis=-1, keepdims=True)
    el2 = jnp.where(lane == i1, -jnp.inf, el)
    m2 = jnp.max(el2, axis=-1, keepdims=True)
    i2 = jnp.min(jnp.where(el2 == m2, lane, LANES), axis=-1, keepdims=True)
    e21 = jnp.exp(m2 - m1)
    c1 = pg_sel / (1.0 + e21)
    @pl.when(pl.program_id(0) == 0)
    def _():
        run_ref[...] = jnp.zeros_like(run_ref)

    oh1 = jnp.where(lane == i1, 1.0, 0.0)
    oh2 = jnp.where(lane == i2, 1.0, 0.0)
    both = oh1 + oh2
    tr = lax.broadcasted_iota(jnp.int32, (tm, tm), 0)
    tc = lax.broadcasted_iota(jnp.int32, (tm, tm), 1)
    before = _dot(jnp.where(tc < tr, 1.0, 0.0).astype(BF16), both.astype(BF16)) + run_ref[0:1, :]
    r1 = jnp.sum(oh1 * before, axis=-1, keepdims=True)
    r2 = jnp.sum(oh2 * before, axis=-1, keepdims=True)
    total = run_ref[...] + jnp.sum(both, axis=0, keepdims=True)
    run_ref[...] = total
    cnt_ref[...] = total
    ids = (jnp.where(lane == 0, i1, i2) - ROUTER_EXPERT_LANE).astype(F32)
    rt_ref[...] = jnp.where(lane < 2, ids, jnp.where(lane == 2, c1, jnp.where(lane == 3, c1 * e21,
                            jnp.where(lane == 4, r1, jnp.where(lane == 5, r2, 0.0)))))


def _merge(x, ys, gmix, wg, wbr, wout, gq, wq, mkt, mv, wo, gffn, wr, br, batch, seq):
    n, d = x.shape
    tm = MERGE_TM
    spt = seq // tm
    m = mv.shape[1]
    tok = lambda w: pl.BlockSpec((tm, w), lambda i: (i, 0))
    return pl.pallas_call(
        functools.partial(_merge_kernel, tm=tm),
        out_shape=(jax.ShapeDtypeStruct((n, d), F32), jax.ShapeDtypeStruct((n, LANES), F32),
                   jax.ShapeDtypeStruct((8, LANES), F32)),
        grid=(n // tm,),
        in_specs=[tok(d), tok(256), tok(256), tok(256), tok(256),
                  _const_spec((1, d)), _const_spec((d, N_BRANCH * d)), _const_spec((N_BRANCH, 256, d)),
                  _const_spec((d, d)), _const_spec((1, d)), _const_spec((d, 256)),
                  pl.BlockSpec((1, 256, m), lambda i: (i // spt, 0, 0)),
                  pl.BlockSpec((1, m, 256), lambda i: (i // spt, 0, 0)),
                  _const_spec((256, d)), _const_spec((1, d)), _const_spec((d, LANES)), _const_spec((1, LANES))],
        out_specs=(tok(d), tok(LANES), pl.BlockSpec((8, LANES), lambda i: (0, 0))),
        scratch_shapes=[pltpu.VMEM((8, LANES), F32)],
        compiler_params=_params(("arbitrary",)),
        name="merge_mem_router",
    )(x, *ys, gmix, wg, wbr, wout, gq, wq, mkt, mv, wo, gffn, wr, br)


def _sc_gather_rows(table, idx):
    _, width = table.shape
    total = idx.shape[0]
    chunk, nbuf = SC_GATHER_CHUNK, SC_GATHER_BUFS
    workers = SC_CORES * SC_SUBCORES
    per_w = total // workers
    nch = per_w // chunk
    assert total % (workers * chunk * nbuf) == 0
    mesh = plsc.VectorSubcoreMesh(core_axis_name="c", subcore_axis_name="s")

    @functools.partial(
        pl.kernel, mesh=mesh, out_type=jax.ShapeDtypeStruct((total, width), table.dtype),
        scratch_types=[pltpu.VMEM((nch, chunk), jnp.int32), pltpu.VMEM((nbuf, chunk, width), table.dtype),
                       pltpu.SemaphoreType.DMA((nbuf,)), pltpu.SemaphoreType.DMA((nbuf,))])
    def gather_kernel(table_hbm, idx_hbm, out_hbm, idx_v, rows_v, gsem, wsem):
        wid = lax.axis_index("s") * SC_CORES + lax.axis_index("c")
        pltpu.sync_copy(idx_hbm.at[wid], idx_v)

        def gather(j, slot):
            return pltpu.make_async_copy(table_hbm.at[idx_v.at[j]], rows_v.at[slot], gsem.at[slot])

        def write(j, slot):
            off = pl.multiple_of(wid * per_w + j * chunk, chunk)
            return pltpu.make_async_copy(rows_v.at[slot], out_hbm.at[pl.ds(off, chunk)], wsem.at[slot])

        for slot in range(nbuf):
            gather(slot, slot).start()

        @pl.loop(0, nch // nbuf)
        def _(g):
            for slot in range(nbuf):
                j = g * nbuf + slot
                gather(j, slot).wait()
                write(j, slot).start()
                write(j, slot).wait()

                @pl.when(j + nbuf < nch)
                def _():
                    gather(j + nbuf, slot).start()

    return gather_kernel(table, idx.reshape(workers, nch, chunk))


def _sc_scatter_rows(table, dest2, total):
    n, width = table.shape
    chunk, nbuf = SC_GATHER_CHUNK, SC_GATHER_BUFS
    workers = SC_CORES * SC_SUBCORES
    per_w = n // workers
    nch = per_w // chunk
    assert n % (workers * chunk * nbuf) == 0
    mesh = plsc.VectorSubcoreMesh(core_axis_name="c", subcore_axis_name="s")

    @functools.partial(
        pl.kernel, mesh=mesh, out_type=jax.ShapeDtypeStruct((total, width), table.dtype),
        scratch_types=[pltpu.VMEM((2, nch, chunk), jnp.int32), pltpu.VMEM((nbuf, chunk, width), table.dtype),
                       pltpu.SemaphoreType.DMA((nbuf,)), pltpu.SemaphoreType.DMA((nbuf,))])
    def scatter_kernel(table_hbm, idx_hbm, out_hbm, idx_v, rows_v, rsem, wsem):
        wid = lax.axis_index("s") * SC_CORES + lax.axis_index("c")
        pltpu.sync_copy(idx_hbm.at[wid], idx_v)

        def read(j, slot):
            off = pl.multiple_of(wid * per_w + j * chunk, chunk)
            return pltpu.make_async_copy(table_hbm.at[pl.ds(off, chunk)], rows_v.at[slot], rsem.at[slot])

        def write(j, slot, k):
            return pltpu.make_async_copy(rows_v.at[slot], out_hbm.at[idx_v.at[k, j]], wsem.at[slot])

        for slot in range(nbuf):
            read(slot, slot).start()

        @pl.loop(0, nch // nbuf)
        def _(g):
            for slot in range(nbuf):
                j = g * nbuf + slot
                read(j, slot).wait()
                write(j, slot, 0).start()
                write(j, slot, 1).start()
                write(j, slot, 0).wait()
                write(j, slot, 1).wait()

                @pl.when(j + nbuf < nch)
                def _():
                    read(j + nbuf, slot).start()

    idx = dest2.reshape(2, workers, nch, chunk).transpose(1, 0, 2, 3)
    return scatter_kernel(table, idx)


def _dispatch_plan(rt, cnt, n):
    ne, blk = MOE_N_EXPERTS, MOE_BLOCK
    n_blocks = (2 * n) // blk + ne
    experts = jnp.arange(ne, dtype=jnp.int32)
    counts = cnt[0, ROUTER_EXPERT_LANE:ROUTER_EXPERT_LANE + ne].astype(jnp.int32)
    padded = (counts + blk - 1) // blk * blk
    pend = jnp.cumsum(padded)
    pstart = pend - padded
    ids = rt[:, 0:2].astype(jnp.int32)
    pos = rt[:, 4:6].astype(jnp.int32)
    first_row = jnp.sum(jnp.where(ids[:, :, None] == experts[None, None, :], pstart[None, None, :], 0), axis=-1)
    dest2 = (first_row + pos).T
    b0 = jnp.arange(n_blocks, dtype=jnp.int32) * blk
    block_e = jnp.minimum(jnp.sum((pend[None, :] <= b0[:, None]).astype(jnp.int32), axis=1), ne - 1)
    n_valid = jnp.clip(counts[block_e] - (b0 - pstart[block_e]), 0, blk).astype(jnp.int32)
    return dest2, block_e, n_valid


def _expert_block_kernel(be_ref, nv_ref, xs_ref, g_ref, wg_ref, wu_ref, wd_ref, o_ref):
    del be_ref
    valid = nv_ref[pl.program_id(0)]

    @pl.when(valid > 0)
    def _():
        row = lax.broadcasted_iota(jnp.int32, xs_ref.shape, 0)
        xs = jnp.where(row < valid, xs_ref[...], 0.0)
        h = _rms(xs, g_ref[...]).astype(BF16)
        gt = _dot(h, wg_ref[0])
        hid = gt * _sigmoid(gt) * _dot(h, wu_ref[0])
        o_ref[...] = _dot(hid.astype(BF16), wd_ref[0])

    @pl.when(valid == 0)
    def _():
        o_ref[...] = jnp.zeros_like(o_ref)


def _expert_blocks(xs, gain, wg, wu, wd, block_e, n_used):
    p_rows, d = xs.shape
    hid = wg.shape[-1]
    blk = MOE_BLOCK
    grid_spec = pltpu.PrefetchScalarGridSpec(
        num_scalar_prefetch=2, grid=(p_rows // blk,),
        in_specs=[pl.BlockSpec((blk, d), lambda b, be, nu: (b, 0)),
                  pl.BlockSpec((1, d), lambda b, be, nu: (0, 0)),
                  pl.BlockSpec((1, d, hid), lambda b, be, nu: (be[b], 0, 0)),
                  pl.BlockSpec((1, d, hid), lambda b, be, nu: (be[b], 0, 0)),
                  pl.BlockSpec((1, hid, d), lambda b, be, nu: (be[b], 0, 0))],
        out_specs=pl.BlockSpec((blk, d), lambda b, be, nu: (b, 0)))
    return pl.pallas_call(
        _expert_block_kernel, out_shape=jax.ShapeDtypeStruct((p_rows, d), F32), grid_spec=grid_spec,
        compiler_params=_params(("parallel",)), name="moe_expert_blocks",
    )(block_e, n_used, xs, gain, wg, wu, wd)


def _combine_kernel(x_ref, y1_ref, y2_ref, rt_ref, gfin_ref, o_ref, *, final_norm):
    out = x_ref[...] + rt_ref[:, 2:3] * y1_ref[...] + rt_ref[:, 3:4] * y2_ref[...]
    o_ref[...] = _rms(out, gfin_ref[...]) if final_norm else out


def _combine(x2, y_halves, rt, gfin, final_norm):
    n, d = x2.shape
    tm = COMBINE_TM
    nt = n // tm
    return pl.pallas_call(
        functools.partial(_combine_kernel, final_norm=final_norm),
        out_shape=jax.ShapeDtypeStruct((n, d), F32),
        grid=(nt,),
        in_specs=[pl.BlockSpec((tm, d), lambda i: (i, 0)),
                  pl.BlockSpec((tm, d), lambda i: (i, 0)),
                  pl.BlockSpec((tm, d), lambda i: (i + nt, 0)),
                  pl.BlockSpec((tm, LANES), lambda i: (i, 0)), _const_spec((1, d))],
        out_specs=pl.BlockSpec((tm, d), lambda i: (i, 0)),
        compiler_params=_params(("parallel",)), name="moe_combine",
    )(x2, y_halves, y_halves, rt, gfin)


def _moe(x2, rt, cnt, gffn, wg, wu, wd, gfin, final_norm):
    n, d = x2.shape
    dest2, block_e, n_valid = _dispatch_plan(rt, cnt, n)
    xs = _sc_scatter_rows(x2, dest2, block_e.shape[0] * MOE_BLOCK)
    yb = _expert_blocks(xs, gffn, wg, wu, wd, block_e, n_valid)
    y_halves = _sc_gather_rows(yb, dest2.reshape(2 * n))
    return _combine(x2, y_halves, rt, gfin, final_norm)


def kernel(x, mem, positions, norm_mix, w_in, diff_lambda, hgrn_lb_logits, spatial_w, spatial_b, w_branch, w_out,
           norm_mem_q, norm_mem_kv, w_mem_q, w_mem_kv, w_mem_o, norm_ffn, w_router_group, b_router_group,
           w_router_expert, b_router_expert, w_exp_gate, w_exp_up, w_exp_down, norm_final):
    batch, seq, d = x.shape
    depth = w_in.shape[0]
    n = batch * seq
    xf = x.reshape(n, d)
    tabs = _rope_tables(positions)
    row = lambda v: v.reshape(1, -1).astype(F32)
    for l in range(depth):
        lam_init = 0.8 - 0.6 * math.exp(-0.3 * l)
        w1, w_gate = _split_w_in(w_in, l)
        sw = spatial_w[l].reshape(SGU_GROUPS * SGU_CHUNK, SGU_CHUNK)
        sb = jnp.repeat(spatial_b[l].T, SGU_GROUP_DIM, axis=1)
        qat, ka, vat, hb, y_c, qdt, iqt, dkv, dkvt, ikw, iwt = _projection(
            xf, row(norm_mix[l]), w1, tabs, sw, sb, batch, seq)
        y_a = _diff_attention(diff_lambda[l], qat, ka, vat, lam_init, batch, seq)
        y_b = _hgrn(hgrn_lb_logits, hb, l, batch, seq)
        y_d = _dsa(qdt, iqt, iwt, dkv, dkvt, ikw, batch, seq)
        mkt, mv = _mem_kv(mem, row(norm_mem_kv[l]), w_mem_kv[l].astype(BF16))
        e0, e1 = ROUTER_EXPERT_LANE, ROUTER_EXPERT_LANE + MOE_N_EXPERTS
        wr = jnp.zeros((d, LANES), F32)
        wr = wr.at[:, :MOE_GROUPS].set(w_router_group[l]).at[:, e0:e1].set(w_router_expert[l]).astype(BF16)
        br = jnp.zeros((1, LANES), F32)
        br = br.at[0, :MOE_GROUPS].set(b_router_group[l]).at[0, e0:e1].set(b_router_expert[l])
        x2, rt, cnt = _merge(xf, (y_a, y_b, y_c, y_d), row(norm_mix[l]), w_gate, w_branch[l].astype(BF16),
                             w_out[l].astype(BF16), row(norm_mem_q[l]), w_mem_q[l].astype(BF16), mkt, mv,
                             w_mem_o[l].astype(BF16), row(norm_ffn[l]), wr, br, batch, seq)
        xf = _moe(x2, rt, cnt, row(norm_ffn[l]), w_exp_gate[l].astype(BF16), w_exp_up[l].astype(BF16),
                  w_exp_down[l].astype(BF16), row(norm_final), final_norm=(l == depth - 1))
    return xf.reshape(batch, seq, d)
```

```python
import functools
import math

import numpy as np
import jax
import jax.numpy as jnp
from jax import lax
from jax.experimental import pallas as pl
from jax.experimental.pallas import tpu as pltpu
from jax.experimental.pallas import tpu_sc as plsc

F32 = jnp.float32
BF16 = jnp.bfloat16

NORM_EPS = 1e-6
ROPE_THETA = 10000.0
NEG_BIG = -1e30

N_BRANCH = 4
BRANCH_WIDTH = 256
DIFF_HEADS = 4
DIFF_HEAD_DIM = 32
HGRN_HEADS = 4
HGRN_DIM = 64
HGRN_CHUNK = 64
HGRN_MIN_FORGET = 1e-30
SGU_GROUPS = 4
SGU_GROUP_DIM = 64
SGU_CHUNK = 128
DSA_HEADS = 4
DSA_HEAD_DIM = 64
DSA_IDX_HEADS = 4
DSA_IDX_DIM = 32
DSA_TOPK = 256
MEM_HEADS = 4
MEM_HEAD_DIM = 64
MOE_GROUPS = 4
MOE_EXPERTS_PER_GROUP = 8
MOE_N_EXPERTS = 32
MOE_BLOCK = 256
ROUTER_EXPERT_LANE = 32
SC_CORES = 2
SC_SUBCORES = 16
SC_GATHER_CHUNK = 16
SC_GATHER_BUFS = 4

LANES = 128
VMEM_LIMIT = 56 * 1024 * 1024

PROJ_TM = 256
DIFF_TQ = 256
DIFF_TK = 512
HGRN_TC = 512
DSA_TQ = 256
DSA_TK = 512
MERGE_TM = 512
COMBINE_TM = 512
ROPE_TM = 1024

C_AQ, C_AK, C_AV = 0, 256, 512
C_HB = 768
C_UV = 1792
C_DQ = 2304
C_DKV = 2560
C_IQ = 2688
C_IKW = 2816
IW_LANE = 32
C_GATE = 2852
C_TOTAL = 2944
LOG2E = math.log2(math.e)


def _params(sem):
    return pltpu.CompilerParams(dimension_semantics=sem, vmem_limit_bytes=VMEM_LIMIT)


def _const_spec(shape):
    nd = len(shape)
    return pl.BlockSpec(shape, lambda *_: (0,) * nd, pipeline_mode=pl.Buffered(1))


def _rms(xf, gain=None):
    y = xf * lax.rsqrt(jnp.mean(xf * xf, axis=-1, keepdims=True) + NORM_EPS)
    return y if gain is None else y * gain


def _sigmoid(x):
    return 0.5 * jnp.tanh(0.5 * x) + 0.5


def _dot(a, b):
    return jnp.dot(a, b, preferred_element_type=F32)


def _dot_nt(a, b):
    return lax.dot_general(a, b, (((1,), (1,)), ((), ())), preferred_element_type=F32)


def _rope_table_kernel(pos_ref, frq_ref, sgn_ref, c32_ref, s32_ref, c64_ref, s64_ref):
    pos = pos_ref[...].astype(F32)
    a32 = pos * frq_ref[0:1, :]
    a64 = pos * frq_ref[1:2, :]
    c32_ref[...] = jnp.cos(a32)
    s32_ref[...] = jnp.sin(a32) * sgn_ref[0:1, :]
    c64_ref[...] = jnp.cos(a64)
    s64_ref[...] = jnp.sin(a64) * sgn_ref[1:2, :]


def _rope_tables(positions):
    n = positions.size
    pos = positions.reshape(n, 1).astype(jnp.int32)
    lane = np.arange(256)
    inv32 = ROPE_THETA ** (-jnp.arange(16, dtype=F32) * (2.0 / 32))
    inv64 = ROPE_THETA ** (-jnp.arange(32, dtype=F32) * (2.0 / 64))
    frq = jnp.stack([inv32[lane % 16], inv64[lane % 32]])
    sgn = jnp.asarray(np.stack([np.where(lane % 32 < 16, -1.0, 1.0),
                                np.where(lane % 64 < 32, -1.0, 1.0)]), F32)
    tm = ROPE_TM
    tab = jax.ShapeDtypeStruct((n, 256), F32)
    return pl.pallas_call(
        _rope_table_kernel,
        out_shape=(tab, tab, tab, tab),
        grid=(n // tm,),
        in_specs=[pl.BlockSpec((tm, 1), lambda i: (i, 0)), _const_spec((2, 256)), _const_spec((2, 256))],
        out_specs=tuple(pl.BlockSpec((tm, 256), lambda i: (i, 0)) for _ in range(4)),
        compiler_params=_params(("parallel",)),
        name="rope_tables",
    )(pos, frq, sgn)


def _split_w_in_kernel(w_ref, w1_ref, wg_ref):
    w1_ref[...] = w_ref[:, :C_TOTAL].astype(BF16)
    a0 = (C_GATE // LANES) * LANES
    tail = w_ref[:, a0:]
    wg_ref[...] = tail[:, C_GATE - a0:].astype(BF16)


def _split_w_in(w_in, layer):
    _, d, width = w_in.shape
    tr = 128
    return pl.pallas_call(
        _split_w_in_kernel,
        out_shape=(jax.ShapeDtypeStruct((d, C_TOTAL), BF16), jax.ShapeDtypeStruct((d, width - C_GATE), BF16)),
        grid=(d // tr,),
        in_specs=[pl.BlockSpec((None, tr, width), lambda i: (layer, i, 0))],
        out_specs=(pl.BlockSpec((tr, C_TOTAL), lambda i: (i, 0)), pl.BlockSpec((tr, width - C_GATE), lambda i: (i, 0))),
        compiler_params=_params(("parallel",)),
        name="split_w_in",
    )(w_in)


def _gelu_tanh(x):
    return 0.5 * x * (1.0 + jnp.tanh(math.sqrt(2.0 / math.pi) * (x + 0.044715 * (x * x * x))))


def _rope(x, cos, sin_signed, half):
    w = x.shape[-1]
    lane = lax.broadcasted_iota(jnp.int32, x.shape, 1)
    partner = jnp.where(lane % (2 * half) < half, pltpu.roll(x, w - half, 1), pltpu.roll(x, half, 1))
    return x * cos + partner * sin_signed


def _proj_kernel(x_ref, g_ref, w_ref, c32_ref, s32_ref, c64_ref, s64_ref, sw_ref, sb_ref,
                 qat_ref, ka_ref, vat_ref, hb_ref, yc_ref, qdt_ref, iqt_ref, dkv_ref, dkvt_ref, ikw_ref, iwt_ref,
                 *, tm):
    h = _rms(x_ref[...], g_ref[...]).astype(BF16)

    def proj(c0, width):
        return _dot(h, w_ref[:, c0:c0 + width])

    c32, s32, c64, s64 = c32_ref[...], s32_ref[...], c64_ref[...], s64_ref[...]
    qat_ref[0] = (_rope(proj(C_AQ, 256), c32, s32, 16) * (DIFF_HEAD_DIM ** -0.5 * LOG2E)).T.astype(BF16)
    ka_ref[...] = _rope(proj(C_AK, 256), c32, s32, 16).astype(BF16)
    vat_ref[0] = proj(C_AV, 256).astype(BF16).T
    hb_ref[...] = proj(C_HB, 1024)
    qdt_ref[0] = (_rope(proj(C_DQ, 256), c64, s64, 32) * (DSA_HEAD_DIM ** -0.5 * LOG2E)).T.astype(BF16)
    iqt_ref[0] = _rope(proj(C_IQ, 128), c32[:, :128], s32[:, :128], 16).T.astype(BF16)
    lane = lax.broadcasted_iota(jnp.int32, (tm, 128), 1)
    is_k = lane < DSA_HEAD_DIM
    dkv = _rope(proj(C_DKV, 128), jnp.where(is_k, c64[:, :128], 1.0), jnp.where(is_k, s64[:, :128], 0.0), 32)
    is_ik = lane < DSA_IDX_DIM
    ikw = _rope(proj(C_IKW, 128), jnp.where(is_ik, c32[:, :128], 1.0), jnp.where(is_ik, s32[:, :128], 0.0), 16)
    dkv_ref[...] = dkv.astype(BF16)
    dkvt_ref[0] = dkv.T.astype(BF16)
    ikw_ref[...] = ikw.astype(BF16)
    iw_scale = DSA_IDX_HEADS ** -0.5 * DSA_IDX_DIM ** -0.5
    iwt_ref[0] = (ikw * iw_scale).T[IW_LANE:IW_LANE + 8, :]
    uv = _gelu_tanh(proj(C_UV, 512))
    u, v = uv[:, :256], uv[:, 256:]
    mu = jnp.mean(v, axis=-1, keepdims=True)
    vc = v - mu
    vn = (vc * lax.rsqrt(jnp.mean(vc * vc, axis=-1, keepdims=True) + NORM_EPS)).astype(BF16)
    r = lax.broadcasted_iota(jnp.int32, (SGU_GROUPS * SGU_CHUNK, SGU_CHUNK), 0)
    c = lax.broadcasted_iota(jnp.int32, (SGU_GROUPS * SGU_CHUNK, SGU_CHUNK), 1)
    wt = jnp.where((r % SGU_CHUNK) >= c, sw_ref[...], 0.0).astype(BF16)
    lane_grp = lax.broadcasted_iota(jnp.int32, (SGU_CHUNK, 256), 1) // SGU_GROUP_DIM
    for ch in range(tm // SGU_CHUNK):
        r0 = ch * SGU_CHUNK
        full = _dot(wt, vn[r0:r0 + SGU_CHUNK, :])
        mixed = sb_ref[...]
        for g in range(SGU_GROUPS):
            mixed = mixed + jnp.where(lane_grp == g, full[g * SGU_CHUNK:(g + 1) * SGU_CHUNK, :], 0.0)
        yc_ref[r0:r0 + SGU_CHUNK, :] = (u[r0:r0 + SGU_CHUNK, :] * mixed).astype(BF16)


def _projection(x, gain, w1, tabs, sw, sb, batch, seq):
    n, d = x.shape
    tm = PROJ_TM
    spt = seq // tm
    tok = lambda w: pl.BlockSpec((tm, w), lambda i: (i, 0))
    tr = lambda rows: pl.BlockSpec((1, rows, tm), lambda i: (i // spt, 0, i % spt))
    out_shape = (
        jax.ShapeDtypeStruct((batch, 256, seq), BF16),
        jax.ShapeDtypeStruct((n, 256), BF16),
        jax.ShapeDtypeStruct((batch, 256, seq), BF16),
        jax.ShapeDtypeStruct((n, 1024), F32),
        jax.ShapeDtypeStruct((n, 256), BF16),
        jax.ShapeDtypeStruct((batch, 256, seq), BF16),
        jax.ShapeDtypeStruct((batch, 128, seq), BF16),
        jax.ShapeDtypeStruct((n, 128), BF16),
        jax.ShapeDtypeStruct((batch, 128, seq), BF16),
        jax.ShapeDtypeStruct((n, 128), BF16),
        jax.ShapeDtypeStruct((batch, 8, seq), F32),
    )
    return pl.pallas_call(
        functools.partial(_proj_kernel, tm=tm),
        out_shape=out_shape,
        grid=(n // tm,),
        in_specs=[tok(d), _const_spec((1, d)), _const_spec((d, C_TOTAL)),
                  tok(256), tok(256), tok(256), tok(256),
                  _const_spec((SGU_GROUPS * SGU_CHUNK, SGU_CHUNK)), _const_spec((SGU_CHUNK, 256))],
        out_specs=(tr(256), tok(256), tr(256), tok(1024), tok(256), tr(256), tr(128), tok(128), tr(128), tok(128), tr(8)),
        compiler_params=_params(("parallel",)),
        name="projection",
    )(x, gain, w1, *tabs, sw, sb)


def _diff_attn_kernel(lam_ref, qt_ref, k_ref, vt_ref, o_ref, *, lam_init, tq, tk):
    q0 = pl.program_id(1) * tq
    kb_diag = q0 // tk
    lv = lam_ref[...]
    lam = (jnp.exp(jnp.sum(lv[0:1] * lv[1:2], axis=-1, keepdims=True))
           - jnp.exp(jnp.sum(lv[2:3] * lv[3:4], axis=-1, keepdims=True)) + lam_init)
    qt = qt_ref[0]
    feat = lax.broadcasted_iota(jnp.int32, (256, tq), 0) // DIFF_HEAD_DIM
    n_maps = 2 * DIFF_HEADS
    qz = jnp.concatenate([jnp.where(feat == i, qt, jnp.zeros_like(qt)) for i in range(n_maps)], axis=1)
    wide = n_maps * tq
    key_i = lax.broadcasted_iota(jnp.int32, (tk, wide), 0)
    qry_i = q0 + lax.broadcasted_iota(jnp.int32, (tk, wide), 1) % tq

    def step(kb, carry, masked):
        m_i, l_i, acc = carry
        k0 = pl.multiple_of(kb * tk, tk)
        s = _dot(k_ref[pl.ds(k0, tk), :], qz)
        if masked:
            s = jnp.where(k0 + key_i <= qry_i, s, NEG_BIG)
        m_new = jnp.maximum(m_i, jnp.max(s, axis=0, keepdims=True))
        p = jnp.exp2(s - m_new)
        alpha = jnp.exp2(m_i - m_new)
        l_new = alpha * l_i + jnp.sum(p, axis=0, keepdims=True)
        pb = p.astype(BF16)
        pv = jnp.concatenate(
            [_dot(vt_ref[0, hd * 64:(hd + 1) * 64, pl.ds(k0, tk)], pb[:, 2 * hd * tq:(2 * hd + 2) * tq])
             for hd in range(DIFF_HEADS)], axis=1)
        return m_new, l_new, alpha * acc + pv

    init = (jnp.full((1, wide), NEG_BIG, F32), jnp.zeros((1, wide), F32), jnp.zeros((64, wide), F32))
    carry = lax.fori_loop(0, kb_diag, functools.partial(step, masked=False), init)
    _, l_f, acc = step(kb_diag, carry, True)
    o_all = acc / l_f
    heads = []
    for hd in range(DIFF_HEADS):
        o0 = o_all[:, 2 * hd * tq:(2 * hd + 1) * tq]
        o1 = o_all[:, (2 * hd + 1) * tq:(2 * hd + 2) * tq]
        o_h = o0 - lam * o1
        ms = jnp.mean(o_h * o_h, axis=0, keepdims=True)
        heads.append(o_h * lax.rsqrt(ms + NORM_EPS) * (1.0 - lam_init))
    o_ref[...] = jnp.concatenate(heads, axis=0).T.astype(BF16)


def _diff_attention(lam_vec, qat, ka, vat, lam_init, batch, seq):
    tq, tk = DIFF_TQ, DIFF_TK
    nq = seq // tq
    return pl.pallas_call(
        functools.partial(_diff_attn_kernel, lam_init=lam_init, tq=tq, tk=tk),
        out_shape=jax.ShapeDtypeStruct((batch * seq, 256), BF16),
        grid=(batch, nq),
        in_specs=[_const_spec((4, DIFF_HEAD_DIM)),
                  pl.BlockSpec((1, 256, tq), lambda b, i: (b, 0, i)),
                  pl.BlockSpec((seq, 256), lambda b, i: (b, 0)),
                  pl.BlockSpec((1, 256, seq), lambda b, i: (b, 0, 0))],
        out_specs=pl.BlockSpec((tq, 256), lambda b, i: (b * nq + i, 0)),
        compiler_params=_params(("parallel", "parallel")),
        name="diff_attention",
    )(lam_vec, qat, ka, vat)


def _hgrn_kernel(lbl_ref, hb_ref, o_ref, st_ref, pstk_ref, *, layer, tc):
    cz = HGRN_CHUNK
    w = 256

    @pl.when(pl.program_id(1) == 0)
    def _():
        st_ref[...] = jnp.zeros_like(st_ref)

    lg = lbl_ref[...]
    e = jnp.exp(lg - jnp.max(lg, axis=0, keepdims=True))
    lw = e / jnp.sum(e, axis=0, keepdims=True)
    lb = jnp.sum(lw[0:layer + 1], axis=0, keepdims=True) - lw[0:1]

    ri = lax.broadcasted_iota(jnp.int32, (cz, cz), 0)
    ci = lax.broadcasted_iota(jnp.int32, (cz, cz), 1)
    tri = (ri >= ci).astype(F32)
    rb = lax.broadcasted_iota(jnp.int32, (w, w), 0) // HGRN_DIM
    cb = lax.broadcasted_iota(jnp.int32, (w, w), 1) // HGRN_DIM
    same_head = rb == cb
    head_ones = same_head.astype(BF16)
    trows = {r: r + lax.broadcasted_iota(jnp.int32, (cz - r, w), 0) for r in range(0, cz, 16)}

    def chunk(c, carry):
        r0 = pl.multiple_of(c * cz, cz)
        q = hb_ref[pl.ds(r0, cz), 0:256]
        fp = hb_ref[pl.ds(r0, cz), 256:512]
        v = hb_ref[pl.ds(r0, cz), 512:768]
        g = hb_ref[pl.ds(r0, cz), 768:1024]
        qf = q * _sigmoid(q)
        f = lb + (1.0 - lb) * jax.nn.sigmoid(fp)
        log_f = jnp.log(jnp.maximum(f, HGRN_MIN_FORGET))
        kf = (1.0 - lb) * jax.nn.sigmoid(-fp)
        bc = jnp.dot(tri, log_f, preferred_element_type=F32, precision=lax.Precision.HIGHEST)
        st = st_ref[...]
        o = _dot_nt((qf * jnp.exp(bc)).astype(BF16), st.astype(BF16))
        for s in range(cz):
            r_lo = (s // 16) * 16
            arg = bc[r_lo:, :] - bc[s:s + 1, :]
            if s > r_lo:
                arg = jnp.where(trows[r_lo] >= s, arg, NEG_BIG)
            p = qf[r_lo:, :] * kf[s:s + 1, :] * jnp.exp(arg)
            if r_lo:
                pstk_ref[s * cz:s * cz + r_lo, :] = jnp.zeros((r_lo, w), BF16)
            pstk_ref[s * cz + r_lo:(s + 1) * cz, :] = p.astype(BF16)
        accs = [jnp.zeros((16, w), F32) for _ in range(cz // 16)]
        for sg in range(cz // 16):
            att = _dot(pstk_ref[sg * 16 * cz:(sg + 1) * 16 * cz, :], head_ones)
            for sl in range(16):
                s = sg * 16 + sl
                for j in range(sg, cz // 16):
                    accs[j] = accs[j] + att[sl * cz + 16 * j:sl * cz + 16 * j + 16, :] * v[s:s + 1, :]
        o = o + jnp.concatenate(accs, axis=0)
        b_end = bc[cz - 1:cz, :]
        kd = kf * jnp.exp(b_end - bc)
        upd = _dot(v.T.astype(BF16), kd.astype(BF16))
        st_ref[...] = st * jnp.exp(b_end) + jnp.where(same_head, upd, 0.0)
        ms = _dot(o * o, head_ones.astype(F32)) * (1.0 / HGRN_DIM)
        y = o * lax.rsqrt(ms + NORM_EPS)
        o_ref[pl.ds(r0, cz), :] = (y * (g * _sigmoid(g))).astype(BF16)
        return carry

    lax.fori_loop(0, tc // cz, chunk, 0)


def _hgrn(lb_logits, hb, layer, batch, seq):
    tc = HGRN_TC
    nt = seq // tc
    cz = HGRN_CHUNK
    return pl.pallas_call(
        functools.partial(_hgrn_kernel, layer=layer, tc=tc),
        out_shape=jax.ShapeDtypeStruct((batch * seq, 256), BF16),
        grid=(batch, nt),
        in_specs=[_const_spec(lb_logits.shape),
                  pl.BlockSpec((tc, 1024), lambda b, i: (b * nt + i, 0))],
        out_specs=pl.BlockSpec((tc, 256), lambda b, i: (b * nt + i, 0)),
        scratch_shapes=[pltpu.VMEM((256, 256), F32), pltpu.VMEM((cz * cz, 256), BF16)],
        compiler_params=_params(("parallel", "arbitrary")),
        name="hgrn2",
    )(lb_logits, hb)


def _dsa_kernel(qdt_ref, iqt_ref, iwt_ref, dkv_ref, dkvt_ref, ikw_ref, o_ref, key_ref, bias_ref, half_ref,
                *, tq, tk, n_sel):
    q0 = pl.program_id(1) * tq
    nkb = q0 // tk + 1
    key_i = lax.broadcasted_iota(jnp.int32, (tk, tq), 0)
    qry_i = q0 + lax.broadcasted_iota(jnp.int32, (tk, tq), 1)
    grp = tk // 8
    rows8 = lambda x: x.reshape(grp, 8, tq)
    iqt = iqt_ref[0]
    zpad = jnp.zeros((LANES - DSA_IDX_DIM, tq), BF16)
    iqz = jnp.concatenate([jnp.concatenate([iqt[hd * DSA_IDX_DIM:(hd + 1) * DSA_IDX_DIM, :], zpad], axis=0)
                           for hd in range(DSA_IDX_HEADS)], axis=1)
    iw = iwt_ref[0]

    def score_block(kb, carry):
        k0 = pl.multiple_of(kb * tk, tk)
        sh = jnp.maximum(_dot(ikw_ref[pl.ds(k0, tk), :], iqz), 0.0)
        sc = jnp.zeros((tk, tq), F32)
        for hd in range(DSA_IDX_HEADS):
            sc = sc + sh[:, hd * tq:(hd + 1) * tq] * iw[hd:hd + 1, :]
        sc = jnp.where(k0 + key_i <= qry_i, sc + 0.0, -jnp.inf)
        bits = pltpu.bitcast(sc, jnp.int32)
        key = jnp.where(bits < 0, bits ^ jnp.int32(0x7FFFFFFF), bits)
        key_ref[pl.ds(k0, tk), :] = key
        half_ref[pl.ds(k0, tk), :] = (key >> 16).astype(jnp.int16)
        return carry

    lax.fori_loop(0, nkb, score_block, 0)

    one16, zero16 = jnp.ones((), jnp.int16), jnp.zeros((), jnp.int16)
    low16 = np.int16(-2 ** 15)

    def count16(limit, strict):
        def body(kb, acc):
            k0 = pl.multiple_of(kb * tk, tk)
            for c in range(tk // 128):
                blk = half_ref[pl.ds(k0 + 128 * c, 128), :].reshape(8, 16, tq)
                hit = jnp.where(blk > limit if strict else blk >= limit, one16, zero16)
                parts = [hit[j] for j in range(8)]
                while len(parts) > 1:
                    parts = [a + b for a, b in zip(parts[0::2], parts[1::2])]
                acc = acc + parts[0]
            return acc
        acc = lax.fori_loop(0, nkb, body, jnp.zeros((16, tq), jnp.int16))
        return jnp.broadcast_to(jnp.sum(acc.astype(jnp.int32), axis=0, keepdims=True), (16, tq))

    def search16(need):
        t = jnp.full((16, tq), -2 ** 15, jnp.int32)
        for bit in range(15, -1, -1):
            trial = t + 2 ** bit
            t = jnp.where(count16(trial.astype(jnp.int16), False) >= need, trial, t)
        return t

    t_hi = search16(n_sel)
    t_hi16 = t_hi.astype(jnp.int16)
    need_lo = n_sel - count16(t_hi16, True)

    def low_block(kb, carry):
        k0 = pl.multiple_of(kb * tk, tk)
        lo = ((key_ref[pl.ds(k0, tk), :] & 0xFFFF) - 2 ** 15).astype(jnp.int16).reshape(tk // 16, 16, tq)
        hi = half_ref[pl.ds(k0, tk), :].reshape(tk // 16, 16, tq)
        half_ref[pl.ds(k0, tk), :] = jnp.where(hi == t_hi16, lo, low16).reshape(tk, tq)
        return carry

    lax.fori_loop(0, nkb, low_block, 0)
    t_lo = search16(need_lo)
    thr = ((t_hi << 16) | (t_lo + 2 ** 15))[0:8, :]

    def count(pred_fn):
        def body(kb, acc):
            k0 = pl.multiple_of(kb * tk, tk)
            parts = None
            for c in range(tk // 64):
                blk = key_ref[pl.ds(k0 + 64 * c, 64), :].reshape(8, 8, tq)
                hit = jnp.where(pred_fn(blk), 1.0, 0.0)
                parts = [hit[j] if parts is None else parts[j] + hit[j] for j in range(8)]
            while len(parts) > 1:
                parts = [a + b for a, b in zip(parts[0::2], parts[1::2])]
            return acc + parts[0]
        acc = lax.fori_loop(0, nkb, body, jnp.zeros((8, tq), F32))
        return jnp.broadcast_to(jnp.sum(acc, axis=0, keepdims=True), (8, tq))

    need = n_sel - count(lambda blk: blk > thr)
    ur = lax.broadcasted_iota(jnp.int32, (tk, tk), 0)
    uc = lax.broadcasted_iota(jnp.int32, (tk, tk), 1)
    earlier = (uc < ur).astype(BF16)
    ones8 = jnp.ones((8, tk), BF16)

    def select_block(kb, seen):
        k0 = pl.multiple_of(kb * tk, tk)
        blk = rows8(key_ref[pl.ds(k0, tk), :])
        eq = blk == thr[None]
        eqb = jnp.where(eq, 1.0, 0.0).reshape(tk, tq).astype(BF16)
        rank = rows8(_dot(earlier, eqb)) + seen[None]
        sel = (blk > thr[None]) | (eq & (rank < need[None]))
        bias = jnp.where(sel, 0.0, NEG_BIG).reshape(tk, tq)
        bias_ref[pl.ds(k0, tk), :] = jnp.where(k0 + key_i <= qry_i, bias, NEG_BIG)
        return seen + _dot(ones8, eqb)

    lax.fori_loop(0, nkb, select_block, jnp.zeros((8, tq), F32))

    qdt = qdt_ref[0]
    zq = jnp.zeros((LANES - DSA_HEAD_DIM, tq), BF16)
    qz = jnp.concatenate([jnp.concatenate([qdt[hd * DSA_HEAD_DIM:(hd + 1) * DSA_HEAD_DIM, :], zq], axis=0)
                          for hd in range(DSA_HEADS)], axis=1)
    wide = DSA_HEADS * tq

    def att_block(kb, carry):
        m_i, l_i, acc = carry
        k0 = pl.multiple_of(kb * tk, tk)
        bias = bias_ref[pl.ds(k0, tk), :]
        s = _dot(dkv_ref[pl.ds(k0, tk), :], qz) + jnp.concatenate([bias] * DSA_HEADS, axis=1)
        m_new = jnp.maximum(m_i, jnp.max(s, axis=0, keepdims=True))
        p = jnp.exp2(s - m_new)
        alpha = jnp.exp2(m_i - m_new)
        l_new = alpha * l_i + jnp.sum(p, axis=0, keepdims=True)
        pv = _dot(dkvt_ref[0, DSA_HEAD_DIM:, pl.ds(k0, tk)], p.astype(BF16))
        return m_new, l_new, alpha * acc + pv

    init = (jnp.full((1, wide), NEG_BIG, F32), jnp.zeros((1, wide), F32), jnp.zeros((DSA_HEAD_DIM, wide), F32))
    _, l_f, acc = lax.fori_loop(0, nkb, att_block, init)
    o_all = acc / l_f
    o_ref[...] = jnp.concatenate([o_all[:, hd * tq:(hd + 1) * tq] for hd in range(DSA_HEADS)],
                                 axis=0).T.astype(BF16)


def _dsa(qdt, iqt, iwt, dkv, dkvt, ikw, batch, seq):
    tq, tk = DSA_TQ, DSA_TK
    nq = seq // tq
    n_sel = min(DSA_TOPK, seq // 4)
    return pl.pallas_call(
        functools.partial(_dsa_kernel, tq=tq, tk=tk, n_sel=n_sel),
        out_shape=jax.ShapeDtypeStruct((batch * seq, 256), BF16),
        grid=(batch, nq),
        in_specs=[pl.BlockSpec((1, 256, tq), lambda b, i: (b, 0, i)),
                  pl.BlockSpec((1, 128, tq), lambda b, i: (b, 0, i)),
                  pl.BlockSpec((1, 8, tq), lambda b, i: (b, 0, i)),
                  pl.BlockSpec((seq, 128), lambda b, i: (b, 0)),
                  pl.BlockSpec((1, 128, seq), lambda b, i: (b, 0, 0)),
                  pl.BlockSpec((seq, 128), lambda b, i: (b, 0))],
        out_specs=pl.BlockSpec((tq, 256), lambda b, i: (b * nq + i, 0)),
        scratch_shapes=[pltpu.VMEM((seq, tq), jnp.int32), pltpu.VMEM((seq, tq), F32),
                        pltpu.VMEM((seq, tq), jnp.int16)],
        compiler_params=_params(("parallel", "parallel")),
        name="dsa",
    )(qdt, iqt, iwt, dkv, dkvt, ikw)


def _mem_kv_kernel(mem_ref, g_ref, w_ref, kt_ref, v_ref):
    mn = _rms(mem_ref[0], g_ref[...]).astype(BF16)
    kv = _dot(mn, w_ref[...])
    kt_ref[0] = kv[:, :256].T.astype(BF16)
    v_ref[0] = kv[:, 256:].astype(BF16)


def _mem_kv(mem, gain, w_kv):
    b, m, d = mem.shape
    return pl.pallas_call(
        _mem_kv_kernel,
        out_shape=(jax.ShapeDtypeStruct((b, 256, m), BF16), jax.ShapeDtypeStruct((b, m, 256), BF16)),
        grid=(b,),
        in_specs=[pl.BlockSpec((1, m, d), lambda i: (i, 0, 0)), _const_spec((1, d)), _const_spec((d, 512))],
        out_specs=(pl.BlockSpec((1, 256, m), lambda i: (i, 0, 0)), pl.BlockSpec((1, m, 256), lambda i: (i, 0, 0))),
        compiler_params=_params(("parallel",)),
        name="mem_kv",
    )(mem, gain, w_kv)


def _merge_kernel(x_ref, ya_ref, yb_ref, yc_ref, yd_ref, gmix_ref, wg_ref, wbr_ref, wout_ref,
                  gq_ref, wq_ref, mkt_ref, mv_ref, wo_ref, gffn_ref, wr_ref, br_ref,
                  x2_ref, rt_ref, cnt_ref, run_ref, *, tm):
    x = x_ref[...]
    d = x.shape[-1]
    h = _rms(x, gmix_ref[...]).astype(BF16)
    merged = jnp.zeros((tm, d), F32)
    for n, y_ref in enumerate((ya_ref, yb_ref, yc_ref, yd_ref)):
        gate = _sigmoid(_dot(h, wg_ref[:, n * d:(n + 1) * d]))
        merged = merged + gate * _dot(y_ref[...], wbr_ref[n])
    x1 = x + _dot(merged.astype(BF16), wout_ref[...])
    h2 = _rms(x1, gq_ref[...]).astype(BF16)
    q = (_dot(h2, wq_ref[...]) * (MEM_HEAD_DIM ** -0.5)).astype(BF16)
    lane_head = lax.broadcasted_iota(jnp.int32, (tm, 256), 1) // MEM_HEAD_DIM
    mv = mv_ref[0]
    o = jnp.zeros((tm, 256), F32)
    for hd in range(MEM_HEADS):
        s = _dot(q[:, hd * MEM_HEAD_DIM:(hd + 1) * MEM_HEAD_DIM], mkt_ref[0, hd * MEM_HEAD_DIM:(hd + 1) * MEM_HEAD_DIM, :])
        p = jnp.exp(s - jnp.max(s, axis=-1, keepdims=True))
        p = p / jnp.sum(p, axis=-1, keepdims=True)
        o = o + jnp.where(lane_head == hd, _dot(p.astype(BF16), mv), 0.0)
    x2 = x1 + _dot(o.astype(BF16), wo_ref[...])
    x2_ref[...] = x2
    h3 = _rms(x2, gffn_ref[...]).astype(BF16)
    logits = _dot(h3, wr_ref[...]) + br_ref[...]
    lane = lax.broadcasted_iota(jnp.int32, (tm, LANES), 1)
    gl = jnp.where(lane < MOE_GROUPS, logits, -jnp.inf)
    gmax = jnp.max(gl, axis=-1, keepdims=True)
    gsel = jnp.min(jnp.where(gl == gmax, lane, LANES), axis=-1, keepdims=True)
    pg_sel = 1.0 / jnp.sum(jnp.exp(gl - gmax), axis=-1, keepdims=True)
    in_group = (lane - ROUTER_EXPERT_LANE) // MOE_EXPERTS_PER_GROUP == gsel
    el = jnp.where(in_group, logits, -jnp.inf)
    m1 = jnp.max(el, axis=-1, keepdims=True)
    i1 = jnp.min(jnp.where(el == m1, lane, LANES), axis=-1, keepdims=True)
    el2 = jnp.where(lane == i1, -jnp.inf, el)
    m2 = jnp.max(el2, axis=-1, keepdims=True)
    i2 = jnp.min(jnp.where(el2 == m2, lane, LANES), axis=-1, keepdims=True)
    e21 = jnp.exp(m2 - m1)
    c1 = pg_sel / (1.0 + e21)
    @pl.when(pl.program_id(0) == 0)
    def _():
        run_ref[...] = jnp.zeros_like(run_ref)

    oh1 = jnp.where(lane == i1, 1.0, 0.0)
    oh2 = jnp.where(lane == i2, 1.0, 0.0)
    both = oh1 + oh2
    tr = lax.broadcasted_iota(jnp.int32, (tm, tm), 0)
    tc = lax.broadcasted_iota(jnp.int32, (tm, tm), 1)
    before = _dot(jnp.where(tc < tr, 1.0, 0.0).astype(BF16), both.astype(BF16)) + run_ref[0:1, :]
    r1 = jnp.sum(oh1 * before, axis=-1, keepdims=True)
    r2 = jnp.sum(oh2 * before, axis=-1, keepdims=True)
    total = run_ref[...] + jnp.sum(both, axis=0, keepdims=True)
    run_ref[...] = total
    cnt_ref[...] = total
    ids = (jnp.where(lane == 0, i1, i2) - ROUTER_EXPERT_LANE).astype(F32)
    rt_ref[...] = jnp.where(lane < 2, ids, jnp.where(lane == 2, c1, jnp.where(lane == 3, c1 * e21,
                            jnp.where(lane == 4, r1, jnp.where(lane == 5, r2, 0.0)))))


def _merge(x, ys, gmix, wg, wbr, wout, gq, wq, mkt, mv, wo, gffn, wr, br, batch, seq):
    n, d = x.shape
    tm = MERGE_TM
    spt = seq // tm
    m = mv.shape[1]
    tok = lambda w: pl.BlockSpec((tm, w), lambda i: (i, 0))
    return pl.pallas_call(
        functools.partial(_merge_kernel, tm=tm),
        out_shape=(jax.ShapeDtypeStruct((n, d), F32), jax.ShapeDtypeStruct((n, LANES), F32),
                   jax.ShapeDtypeStruct((8, LANES), F32)),
        grid=(n // tm,),
        in_specs=[tok(d), tok(256), tok(256), tok(256), tok(256),
                  _const_spec((1, d)), _const_spec((d, N_BRANCH * d)), _const_spec((N_BRANCH, 256, d)),
                  _const_spec((d, d)), _const_spec((1, d)), _const_spec((d, 256)),
                  pl.BlockSpec((1, 256, m), lambda i: (i // spt, 0, 0)),
                  pl.BlockSpec((1, m, 256), lambda i: (i // spt, 0, 0)),
                  _const_spec((256, d)), _const_spec((1, d)), _const_spec((d, LANES)), _const_spec((1, LANES))],
        out_specs=(tok(d), tok(LANES), pl.BlockSpec((8, LANES), lambda i: (0, 0))),
        scratch_shapes=[pltpu.VMEM((8, LANES), F32)],
        compiler_params=_params(("arbitrary",)),
        name="merge_mem_router",
    )(x, *ys, gmix, wg, wbr, wout, gq, wq, mkt, mv, wo, gffn, wr, br)


def _sc_gather_rows(table, idx):
    _, width = table.shape
    total = idx.shape[0]
    chunk, nbuf = SC_GATHER_CHUNK, SC_GATHER_BUFS
    workers = SC_CORES * SC_SUBCORES
    per_w = total // workers
    nch = per_w // chunk
    assert total % (workers * chunk * nbuf) == 0
    mesh = plsc.VectorSubcoreMesh(core_axis_name="c", subcore_axis_name="s")

    @functools.partial(
        pl.kernel, mesh=mesh, out_type=jax.ShapeDtypeStruct((total, width), table.dtype),
        scratch_types=[pltpu.VMEM((nch, chunk), jnp.int32), pltpu.VMEM((nbuf, chunk, width), table.dtype),
                       pltpu.SemaphoreType.DMA((nbuf,)), pltpu.SemaphoreType.DMA((nbuf,))])
    def gather_kernel(table_hbm, idx_hbm, out_hbm, idx_v, rows_v, gsem, wsem):
        wid = lax.axis_index("s") * SC_CORES + lax.axis_index("c")
        pltpu.sync_copy(idx_hbm.at[wid], idx_v)

        def gather(j, slot):
            return pltpu.make_async_copy(table_hbm.at[idx_v.at[j]], rows_v.at[slot], gsem.at[slot])

        def write(j, slot):
            off = pl.multiple_of(wid * per_w + j * chunk, chunk)
            return pltpu.make_async_copy(rows_v.at[slot], out_hbm.at[pl.ds(off, chunk)], wsem.at[slot])

        for slot in range(nbuf):
            gather(slot, slot).start()

        @pl.loop(0, nch // nbuf)
        def _(g):
            for slot in range(nbuf):
                j = g * nbuf + slot
                gather(j, slot).wait()
                write(j, slot).start()
                write(j, slot).wait()

                @pl.when(j + nbuf < nch)
                def _():
                    gather(j + nbuf, slot).start()

    return gather_kernel(table, idx.reshape(workers, nch, chunk))


def _sc_scatter_rows(table, dest2, total):
    n, width = table.shape
    chunk, nbuf = SC_GATHER_CHUNK, SC_GATHER_BUFS
    workers = SC_CORES * SC_SUBCORES
    per_w = n // workers
    nch = per_w // chunk
    assert n % (workers * chunk * nbuf) == 0
    mesh = plsc.VectorSubcoreMesh(core_axis_name="c", subcore_axis_name="s")

    @functools.partial(
        pl.kernel, mesh=mesh, out_type=jax.ShapeDtypeStruct((total, width), table.dtype),
        scratch_types=[pltpu.VMEM((2, nch, chunk), jnp.int32), pltpu.VMEM((nbuf, chunk, width), table.dtype),
                       pltpu.SemaphoreType.DMA((nbuf,)), pltpu.SemaphoreType.DMA((nbuf,))])
    def scatter_kernel(table_hbm, idx_hbm, out_hbm, idx_v, rows_v, rsem, wsem):
        wid = lax.axis_index("s") * SC_CORES + lax.axis_index("c")
        pltpu.sync_copy(idx_hbm.at[wid], idx_v)

        def read(j, slot):
            off = pl.multiple_of(wid * per_w + j * chunk, chunk)
            return pltpu.make_async_copy(table_hbm.at[pl.ds(off, chunk)], rows_v.at[slot], rsem.at[slot])

        def write(j, slot, k):
            return pltpu.make_async_copy(rows_v.at[slot], out_hbm.at[idx_v.at[k, j]], wsem.at[slot])

        for slot in range(nbuf):
            read(slot, slot).start()

        @pl.loop(0, nch // nbuf)
        def _(g):
            for slot in range(nbuf):
                j = g * nbuf + slot
                read(j, slot).wait()
                write(j, slot, 0).start()
                write(j, slot, 1).start()
                write(j, slot, 0).wait()
                write(j, slot, 1).wait()

                @pl.when(j + nbuf < nch)
                def _():
                    read(j + nbuf, slot).start()

    idx = dest2.reshape(2, workers, nch, chunk).transpose(1, 0, 2, 3)
    return scatter_kernel(table, idx)


def _dispatch_plan(rt, cnt, n):
    ne, blk = MOE_N_EXPERTS, MOE_BLOCK
    n_blocks = (2 * n) // blk + ne
    experts = jnp.arange(ne, dtype=jnp.int32)
    counts = cnt[0, ROUTER_EXPERT_LANE:ROUTER_EXPERT_LANE + ne].astype(jnp.int32)
    padded = (counts + blk - 1) // blk * blk
    pend = jnp.cumsum(padded)
    pstart = pend - padded
    ids = rt[:, 0:2].astype(jnp.int32)
    pos = rt[:, 4:6].astype(jnp.int32)
    first_row = jnp.sum(jnp.where(ids[:, :, None] == experts[None, None, :], pstart[None, None, :], 0), axis=-1)
    dest2 = (first_row + pos).T
    b0 = jnp.arange(n_blocks, dtype=jnp.int32) * blk
    block_e = jnp.minimum(jnp.sum((pend[None, :] <= b0[:, None]).astype(jnp.int32), axis=1), ne - 1)
    n_valid = jnp.clip(counts[block_e] - (b0 - pstart[block_e]), 0, blk).astype(jnp.int32)
    return dest2, block_e, n_valid


def _expert_block_kernel(be_ref, nv_ref, xs_ref, g_ref, wg_ref, wu_ref, wd_ref, o_ref):
    del be_ref
    valid = nv_ref[pl.program_id(0)]

    @pl.when(valid > 0)
    def _():
        row = lax.broadcasted_iota(jnp.int32, xs_ref.shape, 0)
        xs = jnp.where(row < valid, xs_ref[...], 0.0)
        h = _rms(xs, g_ref[...]).astype(BF16)
        gt = _dot(h, wg_ref[0])
        hid = gt * _sigmoid(gt) * _dot(h, wu_ref[0])
        o_ref[...] = _dot(hid.astype(BF16), wd_ref[0])

    @pl.when(valid == 0)
    def _():
        o_ref[...] = jnp.zeros_like(o_ref)


def _expert_blocks(xs, gain, wg, wu, wd, block_e, n_used):
    p_rows, d = xs.shape
    hid = wg.shape[-1]
    blk = MOE_BLOCK
    grid_spec = pltpu.PrefetchScalarGridSpec(
        num_scalar_prefetch=2, grid=(p_rows // blk,),
        in_specs=[pl.BlockSpec((blk, d), lambda b, be, nu: (b, 0)),
                  pl.BlockSpec((1, d), lambda b, be, nu: (0, 0)),
                  pl.BlockSpec((1, d, hid), lambda b, be, nu: (be[b], 0, 0)),
                  pl.BlockSpec((1, d, hid), lambda b, be, nu: (be[b], 0, 0)),
                  pl.BlockSpec((1, hid, d), lambda b, be, nu: (be[b], 0, 0))],
        out_specs=pl.BlockSpec((blk, d), lambda b, be, nu: (b, 0)))
    return pl.pallas_call(
        _expert_block_kernel, out_shape=jax.ShapeDtypeStruct((p_rows, d), F32), grid_spec=grid_spec,
        compiler_params=_params(("parallel",)), name="moe_expert_blocks",
    )(block_e, n_used, xs, gain, wg, wu, wd)


def _combine_kernel(x_ref, y1_ref, y2_ref, rt_ref, gfin_ref, o_ref, *, final_norm):
    out = x_ref[...] + rt_ref[:, 2:3] * y1_ref[...] + rt_ref[:, 3:4] * y2_ref[...]
    o_ref[...] = _rms(out, gfin_ref[...]) if final_norm else out


def _combine(x2, y_halves, rt, gfin, final_norm):
    n, d = x2.shape
    tm = COMBINE_TM
    nt = n // tm
    return pl.pallas_call(
        functools.partial(_combine_kernel, final_norm=final_norm),
        out_shape=jax.ShapeDtypeStruct((n, d), F32),
        grid=(nt,),
        in_specs=[pl.BlockSpec((tm, d), lambda i: (i, 0)),
                  pl.BlockSpec((tm, d), lambda i: (i, 0)),
                  pl.BlockSpec((tm, d), lambda i: (i + nt, 0)),
                  pl.BlockSpec((tm, LANES), lambda i: (i, 0)), _const_spec((1, d))],
        out_specs=pl.BlockSpec((tm, d), lambda i: (i, 0)),
        compiler_params=_params(("parallel",)), name="moe_combine",
    )(x2, y_halves, y_halves, rt, gfin)


def _moe(x2, rt, cnt, gffn, wg, wu, wd, gfin, final_norm):
    n, d = x2.shape
    dest2, block_e, n_valid = _dispatch_plan(rt, cnt, n)
    xs = _sc_scatter_rows(x2, dest2, block_e.shape[0] * MOE_BLOCK)
    yb = _expert_blocks(xs, gffn, wg, wu, wd, block_e, n_valid)
    y_halves = _sc_gather_rows(yb, dest2.reshape(2 * n))
    return _combine(x2, y_halves, rt, gfin, final_norm)


def kernel(x, mem, positions, norm_mix, w_in, diff_lambda, hgrn_lb_logits, spatial_w, spatial_b, w_branch, w_out,
           norm_mem_q, norm_mem_kv, w_mem_q, w_mem_kv, w_mem_o, norm_ffn, w_router_group, b_router_group,
           w_router_expert, b_router_expert, w_exp_gate, w_exp_up, w_exp_down, norm_final):
    batch, seq, d = x.shape
    depth = w_in.shape[0]
    n = batch * seq
    xf = x.reshape(n, d)
    tabs = _rope_tables(positions)
    row = lambda v: v.reshape(1, -1).astype(F32)
    for l in range(depth):
        lam_init = 0.8 - 0.6 * math.exp(-0.3 * l)
        w1, w_gate = _split_w_in(w_in, l)
        sw = spatial_w[l].reshape(SGU_GROUPS * SGU_CHUNK, SGU_CHUNK)
        sb = jnp.repeat(spatial_b[l].T, SGU_GROUP_DIM, axis=1)
        qat, ka, vat, hb, y_c, qdt, iqt, dkv, dkvt, ikw, iwt = _projection(
            xf, row(norm_mix[l]), w1, tabs, sw, sb, batch, seq)
        y_a = _diff_attention(diff_lambda[l], qat, ka, vat, lam_init, batch, seq)
        y_b = _hgrn(hgrn_lb_logits, hb, l, batch, seq)
        y_d = _dsa(qdt, iqt, iwt, dkv, dkvt, ikw, batch, seq)
        mkt, mv = _mem_kv(mem, row(norm_mem_kv[l]), w_mem_kv[l].astype(BF16))
        e0, e1 = ROUTER_EXPERT_LANE, ROUTER_EXPERT_LANE + MOE_N_EXPERTS
        wr = jnp.zeros((d, LANES), F32)
        wr = wr.at[:, :MOE_GROUPS].set(w_router_group[l]).at[:, e0:e1].set(w_router_expert[l]).astype(BF16)
        br = jnp.zeros((1, LANES), F32)
        br = br.at[0, :MOE_GROUPS].set(b_router_group[l]).at[0, e0:e1].set(b_router_expert[l])
        x2, rt, cnt = _merge(xf, (y_a, y_b, y_c, y_d), row(norm_mix[l]), w_gate, w_branch[l].astype(BF16),
                             w_out[l].astype(BF16), row(norm_mem_q[l]), w_mem_q[l].astype(BF16), mkt, mv,
                             w_mem_o[l].astype(BF16), row(norm_ffn[l]), wr, br, batch, seq)
        xf = _moe(x2, rt, cnt, row(norm_ffn[l]), w_exp_gate[l].astype(BF16), w_exp_up[l].astype(BF16),
                  w_exp_down[l].astype(BF16), row(norm_final), final_norm=(l == depth - 1))
    return xf.reshape(batch, seq, d)
```

```python
import functools
import math

import numpy as np
import jax
import jax.numpy as jnp
from jax import lax
from jax.experimental import pallas as pl
from jax.experimental.pallas import tpu as pltpu
from jax.experimental.pallas import tpu_sc as plsc

F32 = jnp.float32
BF16 = jnp.bfloat16

NORM_EPS = 1e-6
ROPE_THETA = 10000.0
NEG_BIG = -1e30

N_BRANCH = 4
BRANCH_WIDTH = 256
DIFF_HEADS = 4
DIFF_HEAD_DIM = 32
HGRN_HEADS = 4
HGRN_DIM = 64
HGRN_CHUNK = 64
HGRN_MIN_FORGET = 1e-30
SGU_GROUPS = 4
SGU_GROUP_DIM = 64
SGU_CHUNK = 128
DSA_HEADS = 4
DSA_HEAD_DIM = 64
DSA_IDX_HEADS = 4
DSA_IDX_DIM = 32
DSA_TOPK = 256
MEM_HEADS = 4
MEM_HEAD_DIM = 64
MOE_GROUPS = 4
MOE_EXPERTS_PER_GROUP = 8
MOE_N_EXPERTS = 32
MOE_BLOCK = 512
ROUTER_EXPERT_LANE = 32
SC_CORES = 2
SC_SUBCORES = 16
SC_GATHER_CHUNK = 16
SC_GATHER_BUFS = 4

LANES = 128
VMEM_LIMIT = 56 * 1024 * 1024

PROJ_TM = 256
DIFF_TQ = 256
DIFF_TK = 512
HGRN_TC = 512
DSA_TQ = 256
DSA_TK = 512
MERGE_TM = 512
COMBINE_TM = 512
ROPE_TM = 1024

C_AQ, C_AK, C_AV = 0, 256, 512
C_HB = 768
C_UV = 1792
C_DQ = 2304
C_DKV = 2560
C_IQ = 2688
C_IKW = 2816
IW_LANE = 32
C_GATE = 2852
C_TOTAL = 2944
LOG2E = math.log2(math.e)


def _params(sem):
    return pltpu.CompilerParams(dimension_semantics=sem, vmem_limit_bytes=VMEM_LIMIT)


def _const_spec(shape):
    nd = len(shape)
    return pl.BlockSpec(shape, lambda *_: (0,) * nd, pipeline_mode=pl.Buffered(1))


def _rms(xf, gain=None):
    y = xf * lax.rsqrt(jnp.mean(xf * xf, axis=-1, keepdims=True) + NORM_EPS)
    return y if gain is None else y * gain


def _sigmoid(x):
    return 0.5 * jnp.tanh(0.5 * x) + 0.5


def _dot(a, b):
    return jnp.dot(a, b, preferred_element_type=F32)


def _dot_nt(a, b):
    return lax.dot_general(a, b, (((1,), (1,)), ((), ())), preferred_element_type=F32)


def _rope_table_kernel(pos_ref, frq_ref, sgn_ref, c32_ref, s32_ref, c64_ref, s64_ref):
    pos = pos_ref[...].astype(F32)
    a32 = pos * frq_ref[0:1, :]
    a64 = pos * frq_ref[1:2, :]
    c32_ref[...] = jnp.cos(a32)
    s32_ref[...] = jnp.sin(a32) * sgn_ref[0:1, :]
    c64_ref[...] = jnp.cos(a64)
    s64_ref[...] = jnp.sin(a64) * sgn_ref[1:2, :]


def _rope_tables(positions):
    n = positions.size
    pos = positions.reshape(n, 1).astype(jnp.int32)
    lane = np.arange(256)
    inv32 = ROPE_THETA ** (-jnp.arange(16, dtype=F32) * (2.0 / 32))
    inv64 = ROPE_THETA ** (-jnp.arange(32, dtype=F32) * (2.0 / 64))
    frq = jnp.stack([inv32[lane % 16], inv64[lane % 32]])
    sgn = jnp.asarray(np.stack([np.where(lane % 32 < 16, -1.0, 1.0),
                                np.where(lane % 64 < 32, -1.0, 1.0)]), F32)
    tm = ROPE_TM
    tab = jax.ShapeDtypeStruct((n, 256), F32)
    return pl.pallas_call(
        _rope_table_kernel,
        out_shape=(tab, tab, tab, tab),
        grid=(n // tm,),
        in_specs=[pl.BlockSpec((tm, 1), lambda i: (i, 0)), _const_spec((2, 256)), _const_spec((2, 256))],
        out_specs=tuple(pl.BlockSpec((tm, 256), lambda i: (i, 0)) for _ in range(4)),
        compiler_params=_params(("parallel",)),
        name="rope_tables",
    )(pos, frq, sgn)


def _split_w_in_kernel(w_ref, w1_ref, wg_ref):
    w1_ref[...] = w_ref[:, :C_TOTAL].astype(BF16)
    a0 = (C_GATE // LANES) * LANES
    tail = w_ref[:, a0:]
    wg_ref[...] = tail[:, C_GATE - a0:].astype(BF16)


def _split_w_in(w_in, layer):
    _, d, width = w_in.shape
    tr = 128
    return pl.pallas_call(
        _split_w_in_kernel,
        out_shape=(jax.ShapeDtypeStruct((d, C_TOTAL), BF16), jax.ShapeDtypeStruct((d, width - C_GATE), BF16)),
        grid=(d // tr,),
        in_specs=[pl.BlockSpec((None, tr, width), lambda i: (layer, i, 0))],
        out_specs=(pl.BlockSpec((tr, C_TOTAL), lambda i: (i, 0)), pl.BlockSpec((tr, width - C_GATE), lambda i: (i, 0))),
        compiler_params=_params(("parallel",)),
        name="split_w_in",
    )(w_in)


def _gelu_tanh(x):
    return 0.5 * x * (1.0 + jnp.tanh(math.sqrt(2.0 / math.pi) * (x + 0.044715 * (x * x * x))))


def _rope(x, cos, sin_signed, half):
    w = x.shape[-1]
    lane = lax.broadcasted_iota(jnp.int32, x.shape, 1)
    partner = jnp.where(lane % (2 * half) < half, pltpu.roll(x, w - half, 1), pltpu.roll(x, half, 1))
    return x * cos + partner * sin_signed


def _proj_kernel(x_ref, g_ref, w_ref, c32_ref, s32_ref, c64_ref, s64_ref, sw_ref, sb_ref,
                 qat_ref, ka_ref, vat_ref, hb_ref, yc_ref, qdt_ref, iqt_ref, dkv_ref, dkvt_ref, ikw_ref, iwt_ref,
                 *, tm):
    h = _rms(x_ref[...], g_ref[...]).astype(BF16)

    def proj(c0, width):
        return _dot(h, w_ref[:, c0:c0 + width])

    c32, s32, c64, s64 = c32_ref[...], s32_ref[...], c64_ref[...], s64_ref[...]
    qat_ref[0] = (_rope(proj(C_AQ, 256), c32, s32, 16) * (DIFF_HEAD_DIM ** -0.5 * LOG2E)).T.astype(BF16)
    ka_ref[...] = _rope(proj(C_AK, 256), c32, s32, 16).astype(BF16)
    vat_ref[0] = proj(C_AV, 256).astype(BF16).T
    hb_ref[...] = proj(C_HB, 1024)
    qdt_ref[0] = (_rope(proj(C_DQ, 256), c64, s64, 32) * (DSA_HEAD_DIM ** -0.5 * LOG2E)).T.astype(BF16)
    iqt_ref[0] = _rope(proj(C_IQ, 128), c32[:, :128], s32[:, :128], 16).T.astype(BF16)
    lane = lax.broadcasted_iota(jnp.int32, (tm, 128), 1)
    is_k = lane < DSA_HEAD_DIM
    dkv = _rope(proj(C_DKV, 128), jnp.where(is_k, c64[:, :128], 1.0), jnp.where(is_k, s64[:, :128], 0.0), 32)
    is_ik = lane < DSA_IDX_DIM
    ikw = _rope(proj(C_IKW, 128), jnp.where(is_ik, c32[:, :128], 1.0), jnp.where(is_ik, s32[:, :128], 0.0), 16)
    dkv_ref[...] = dkv.astype(BF16)
    dkvt_ref[0] = dkv.T.astype(BF16)
    ikw_ref[...] = ikw.astype(BF16)
    iw_scale = DSA_IDX_HEADS ** -0.5 * DSA_IDX_DIM ** -0.5
    iwt_ref[0] = (ikw * iw_scale).T[IW_LANE:IW_LANE + 8, :]
    uv = _gelu_tanh(proj(C_UV, 512))
    u, v = uv[:, :256], uv[:, 256:]
    mu = jnp.mean(v, axis=-1, keepdims=True)
    vc = v - mu
    vn = (vc * lax.rsqrt(jnp.mean(vc * vc, axis=-1, keepdims=True) + NORM_EPS)).astype(BF16)
    r = lax.broadcasted_iota(jnp.int32, (SGU_GROUPS * SGU_CHUNK, SGU_CHUNK), 0)
    c = lax.broadcasted_iota(jnp.int32, (SGU_GROUPS * SGU_CHUNK, SGU_CHUNK), 1)
    wt = jnp.where((r % SGU_CHUNK) >= c, sw_ref[...], 0.0).astype(BF16)
    lane_grp = lax.broadcasted_iota(jnp.int32, (SGU_CHUNK, 256), 1) // SGU_GROUP_DIM
    for ch in range(tm // SGU_CHUNK):
        r0 = ch * SGU_CHUNK
        full = _dot(wt, vn[r0:r0 + SGU_CHUNK, :])
        mixed = sb_ref[...]
        for g in range(SGU_GROUPS):
            mixed = mixed + jnp.where(lane_grp == g, full[g * SGU_CHUNK:(g + 1) * SGU_CHUNK, :], 0.0)
        yc_ref[r0:r0 + SGU_CHUNK, :] = (u[r0:r0 + SGU_CHUNK, :] * mixed).astype(BF16)


def _projection(x, gain, w1, tabs, sw, sb, batch, seq):
    n, d = x.shape
    tm = PROJ_TM
    spt = seq // tm
    tok = lambda w: pl.BlockSpec((tm, w), lambda i: (i, 0))
    tr = lambda rows: pl.BlockSpec((1, rows, tm), lambda i: (i // spt, 0, i % spt))
    out_shape = (
        jax.ShapeDtypeStruct((batch, 256, seq), BF16),
        jax.ShapeDtypeStruct((n, 256), BF16),
        jax.ShapeDtypeStruct((batch, 256, seq), BF16),
        jax.ShapeDtypeStruct((n, 1024), F32),
        jax.ShapeDtypeStruct((n, 256), BF16),
        jax.ShapeDtypeStruct((batch, 256, seq), BF16),
        jax.ShapeDtypeStruct((batch, 128, seq), BF16),
        jax.ShapeDtypeStruct((n, 128), BF16),
        jax.ShapeDtypeStruct((batch, 128, seq), BF16),
        jax.ShapeDtypeStruct((n, 128), BF16),
        jax.ShapeDtypeStruct((batch, 8, seq), F32),
    )
    return pl.pallas_call(
        functools.partial(_proj_kernel, tm=tm),
        out_shape=out_shape,
        grid=(n // tm,),
        in_specs=[tok(d), _const_spec((1, d)), _const_spec((d, C_TOTAL)),
                  tok(256), tok(256), tok(256), tok(256),
                  _const_spec((SGU_GROUPS * SGU_CHUNK, SGU_CHUNK)), _const_spec((SGU_CHUNK, 256))],
        out_specs=(tr(256), tok(256), tr(256), tok(1024), tok(256), tr(256), tr(128), tok(128), tr(128), tok(128), tr(8)),
        compiler_params=_params(("parallel",)),
        name="projection",
    )(x, gain, w1, *tabs, sw, sb)


def _diff_attn_kernel(lam_ref, qt_ref, k_ref, vt_ref, o_ref, *, lam_init, tq, tk):
    q0 = pl.program_id(1) * tq
    kb_diag = q0 // tk
    lv = lam_ref[...]
    lam = (jnp.exp(jnp.sum(lv[0:1] * lv[1:2], axis=-1, keepdims=True))
           - jnp.exp(jnp.sum(lv[2:3] * lv[3:4], axis=-1, keepdims=True)) + lam_init)
    qt = qt_ref[0]
    feat = lax.broadcasted_iota(jnp.int32, (256, tq), 0) // DIFF_HEAD_DIM
    n_maps = 2 * DIFF_HEADS
    qz = jnp.concatenate([jnp.where(feat == i, qt, jnp.zeros_like(qt)) for i in range(n_maps)], axis=1)
    wide = n_maps * tq
    key_i = lax.broadcasted_iota(jnp.int32, (tk, wide), 0)
    qry_i = q0 + lax.broadcasted_iota(jnp.int32, (tk, wide), 1) % tq

    def step(kb, carry, masked):
        m_i, l_i, acc = carry
        k0 = pl.multiple_of(kb * tk, tk)
        s = _dot(k_ref[pl.ds(k0, tk), :], qz)
        if masked:
            s = jnp.where(k0 + key_i <= qry_i, s, NEG_BIG)
        m_new = jnp.maximum(m_i, jnp.max(s, axis=0, keepdims=True))
        p = jnp.exp2(s - m_new)
        alpha = jnp.exp2(m_i - m_new)
        l_new = alpha * l_i + jnp.sum(p, axis=0, keepdims=True)
        pb = p.astype(BF16)
        pv = jnp.concatenate(
            [_dot(vt_ref[0, hd * 64:(hd + 1) * 64, pl.ds(k0, tk)], pb[:, 2 * hd * tq:(2 * hd + 2) * tq])
             for hd in range(DIFF_HEADS)], axis=1)
        return m_new, l_new, alpha * acc + pv

    init = (jnp.full((1, wide), NEG_BIG, F32), jnp.zeros((1, wide), F32), jnp.zeros((64, wide), F32))
    carry = lax.fori_loop(0, kb_diag, functools.partial(step, masked=False), init)
    _, l_f, acc = step(kb_diag, carry, True)
    o_all = acc / l_f
    heads = []
    for hd in range(DIFF_HEADS):
        o0 = o_all[:, 2 * hd * tq:(2 * hd + 1) * tq]
        o1 = o_all[:, (2 * hd + 1) * tq:(2 * hd + 2) * tq]
        o_h = o0 - lam * o1
        ms = jnp.mean(o_h * o_h, axis=0, keepdims=True)
        heads.append(o_h * lax.rsqrt(ms + NORM_EPS) * (1.0 - lam_init))
    o_ref[...] = jnp.concatenate(heads, axis=0).T.astype(BF16)


def _diff_attention(lam_vec, qat, ka, vat, lam_init, batch, seq):
    tq, tk = DIFF_TQ, DIFF_TK
    nq = seq // tq
    return pl.pallas_call(
        functools.partial(_diff_attn_kernel, lam_init=lam_init, tq=tq, tk=tk),
        out_shape=jax.ShapeDtypeStruct((batch * seq, 256), BF16),
        grid=(batch, nq),
        in_specs=[_const_spec((4, DIFF_HEAD_DIM)),
                  pl.BlockSpec((1, 256, tq), lambda b, i: (b, 0, i)),
                  pl.BlockSpec((seq, 256), lambda b, i: (b, 0)),
                  pl.BlockSpec((1, 256, seq), lambda b, i: (b, 0, 0))],
        out_specs=pl.BlockSpec((tq, 256), lambda b, i: (b * nq + i, 0)),
        compiler_params=_params(("parallel", "parallel")),
        name="diff_attention",
    )(lam_vec, qat, ka, vat)


def _hgrn_kernel(lbl_ref, hb_ref, o_ref, st_ref, pstk_ref, *, layer, tc):
    cz = HGRN_CHUNK
    w = 256

    @pl.when(pl.program_id(1) == 0)
    def _():
        st_ref[...] = jnp.zeros_like(st_ref)

    lg = lbl_ref[...]
    e = jnp.exp(lg - jnp.max(lg, axis=0, keepdims=True))
    lw = e / jnp.sum(e, axis=0, keepdims=True)
    lb = jnp.sum(lw[0:layer + 1], axis=0, keepdims=True) - lw[0:1]

    ri = lax.broadcasted_iota(jnp.int32, (cz, cz), 0)
    ci = lax.broadcasted_iota(jnp.int32, (cz, cz), 1)
    tri = (ri >= ci).astype(F32)
    rb = lax.broadcasted_iota(jnp.int32, (w, w), 0) // HGRN_DIM
    cb = lax.broadcasted_iota(jnp.int32, (w, w), 1) // HGRN_DIM
    same_head = rb == cb
    head_ones = same_head.astype(BF16)
    trows = {r: r + lax.broadcasted_iota(jnp.int32, (cz - r, w), 0) for r in range(0, cz, 16)}

    def chunk(c, carry):
        r0 = pl.multiple_of(c * cz, cz)
        q = hb_ref[pl.ds(r0, cz), 0:256]
        fp = hb_ref[pl.ds(r0, cz), 256:512]
        v = hb_ref[pl.ds(r0, cz), 512:768]
        g = hb_ref[pl.ds(r0, cz), 768:1024]
        qf = q * _sigmoid(q)
        f = lb + (1.0 - lb) * jax.nn.sigmoid(fp)
        log_f = jnp.log(jnp.maximum(f, HGRN_MIN_FORGET))
        kf = (1.0 - lb) * jax.nn.sigmoid(-fp)
        bc = jnp.dot(tri, log_f, preferred_element_type=F32, precision=lax.Precision.HIGHEST)
        st = st_ref[...]
        o = _dot_nt((qf * jnp.exp(bc)).astype(BF16), st.astype(BF16))
        for s in range(cz):
            r_lo = (s // 16) * 16
            arg = bc[r_lo:, :] - bc[s:s + 1, :]
            if s > r_lo:
                arg = jnp.where(trows[r_lo] >= s, arg, NEG_BIG)
            p = qf[r_lo:, :] * kf[s:s + 1, :] * jnp.exp(arg)
            if r_lo:
                pstk_ref[s * cz:s * cz + r_lo, :] = jnp.zeros((r_lo, w), BF16)
            pstk_ref[s * cz + r_lo:(s + 1) * cz, :] = p.astype(BF16)
        accs = [jnp.zeros((16, w), F32) for _ in range(cz // 16)]
        for sg in range(cz // 16):
            att = _dot(pstk_ref[sg * 16 * cz:(sg + 1) * 16 * cz, :], head_ones)
            for sl in range(16):
                s = sg * 16 + sl
                for j in range(sg, cz // 16):
                    accs[j] = accs[j] + att[sl * cz + 16 * j:sl * cz + 16 * j + 16, :] * v[s:s + 1, :]
        o = o + jnp.concatenate(accs, axis=0)
        b_end = bc[cz - 1:cz, :]
        kd = kf * jnp.exp(b_end - bc)
        upd = _dot(v.T.astype(BF16), kd.astype(BF16))
        st_ref[...] = st * jnp.exp(b_end) + jnp.where(same_head, upd, 0.0)
        ms = _dot(o * o, head_ones.astype(F32)) * (1.0 / HGRN_DIM)
        y = o * lax.rsqrt(ms + NORM_EPS)
        o_ref[pl.ds(r0, cz), :] = (y * (g * _sigmoid(g))).astype(BF16)
        return carry

    lax.fori_loop(0, tc // cz, chunk, 0)


def _hgrn(lb_logits, hb, layer, batch, seq):
    tc = HGRN_TC
    nt = seq // tc
    cz = HGRN_CHUNK
    return pl.pallas_call(
        functools.partial(_hgrn_kernel, layer=layer, tc=tc),
        out_shape=jax.ShapeDtypeStruct((batch * seq, 256), BF16),
        grid=(batch, nt),
        in_specs=[_const_spec(lb_logits.shape),
                  pl.BlockSpec((tc, 1024), lambda b, i: (b * nt + i, 0))],
        out_specs=pl.BlockSpec((tc, 256), lambda b, i: (b * nt + i, 0)),
        scratch_shapes=[pltpu.VMEM((256, 256), F32), pltpu.VMEM((cz * cz, 256), BF16)],
        compiler_params=_params(("parallel", "arbitrary")),
        name="hgrn2",
    )(lb_logits, hb)


def _dsa_kernel(qdt_ref, iqt_ref, iwt_ref, dkv_ref, dkvt_ref, ikw_ref, o_ref, key_ref, bias_ref, half_ref,
                *, tq, tk, n_sel):
    q0 = pl.program_id(1) * tq
    nkb = q0 // tk + 1
    key_i = lax.broadcasted_iota(jnp.int32, (tk, tq), 0)
    qry_i = q0 + lax.broadcasted_iota(jnp.int32, (tk, tq), 1)
    grp = tk // 8
    rows8 = lambda x: x.reshape(grp, 8, tq)
    iqt = iqt_ref[0]
    zpad = jnp.zeros((LANES - DSA_IDX_DIM, tq), BF16)
    iqz = jnp.concatenate([jnp.concatenate([iqt[hd * DSA_IDX_DIM:(hd + 1) * DSA_IDX_DIM, :], zpad], axis=0)
                           for hd in range(DSA_IDX_HEADS)], axis=1)
    iw = iwt_ref[0]

    def score_block(kb, carry):
        k0 = pl.multiple_of(kb * tk, tk)
        sh = jnp.maximum(_dot(ikw_ref[pl.ds(k0, tk), :], iqz), 0.0)
        sc = jnp.zeros((tk, tq), F32)
        for hd in range(DSA_IDX_HEADS):
            sc = sc + sh[:, hd * tq:(hd + 1) * tq] * iw[hd:hd + 1, :]
        sc = jnp.where(k0 + key_i <= qry_i, sc + 0.0, -jnp.inf)
        bits = pltpu.bitcast(sc, jnp.int32)
        key = jnp.where(bits < 0, bits ^ jnp.int32(0x7FFFFFFF), bits)
        key_ref[pl.ds(k0, tk), :] = key
        half_ref[pl.ds(k0, tk), :] = (key >> 16).astype(jnp.int16)
        return carry

    lax.fori_loop(0, nkb, score_block, 0)

    one16, zero16 = jnp.ones((), jnp.int16), jnp.zeros((), jnp.int16)
    low16 = np.int16(-2 ** 15)

    def count16(limit, strict):
        def body(kb, acc):
            k0 = pl.multiple_of(kb * tk, tk)
            for c in range(tk // 128):
                blk = half_ref[pl.ds(k0 + 128 * c, 128), :].reshape(8, 16, tq)
                hit = jnp.where(blk > limit if strict else blk >= limit, one16, zero16)
                parts = [hit[j] for j in range(8)]
                while len(parts) > 1:
                    parts = [a + b for a, b in zip(parts[0::2], parts[1::2])]
                acc = acc + parts[0]
            return acc
        acc = lax.fori_loop(0, nkb, body, jnp.zeros((16, tq), jnp.int16))
        return jnp.broadcast_to(jnp.sum(acc.astype(jnp.int32), axis=0, keepdims=True), (16, tq))

    def search16(need):
        t = jnp.full((16, tq), -2 ** 15, jnp.int32)
        for bit in range(15, -1, -1):
            trial = t + 2 ** bit
            t = jnp.where(count16(trial.astype(jnp.int16), False) >= need, trial, t)
        return t

    t_hi = search16(n_sel)
    t_hi16 = t_hi.astype(jnp.int16)
    need_lo = n_sel - count16(t_hi16, True)

    def low_block(kb, carry):
        k0 = pl.multiple_of(kb * tk, tk)
        lo = ((key_ref[pl.ds(k0, tk), :] & 0xFFFF) - 2 ** 15).astype(jnp.int16).reshape(tk // 16, 16, tq)
        hi = half_ref[pl.ds(k0, tk), :].reshape(tk // 16, 16, tq)
        half_ref[pl.ds(k0, tk), :] = jnp.where(hi == t_hi16, lo, low16).reshape(tk, tq)
        return carry

    lax.fori_loop(0, nkb, low_block, 0)
    t_lo = search16(need_lo)
    thr = ((t_hi << 16) | (t_lo + 2 ** 15))[0:8, :]

    def count(pred_fn):
        def body(kb, acc):
            k0 = pl.multiple_of(kb * tk, tk)
            parts = None
            for c in range(tk // 64):
                blk = key_ref[pl.ds(k0 + 64 * c, 64), :].reshape(8, 8, tq)
                hit = jnp.where(pred_fn(blk), 1.0, 0.0)
                parts = [hit[j] if parts is None else parts[j] + hit[j] for j in range(8)]
            while len(parts) > 1:
                parts = [a + b for a, b in zip(parts[0::2], parts[1::2])]
            return acc + parts[0]
        acc = lax.fori_loop(0, nkb, body, jnp.zeros((8, tq), F32))
        return jnp.broadcast_to(jnp.sum(acc, axis=0, keepdims=True), (8, tq))

    ties_over = count16(t_lo.astype(jnp.int16), False) > need_lo
    any_over = jnp.max(jnp.where(ties_over, 1, 0)) > 0

    def select_plain():
        def block(kb, carry):
            k0 = pl.multiple_of(kb * tk, tk)
            keep = (rows8(key_ref[pl.ds(k0, tk), :]) >= thr[None])
            bias = jnp.where(keep, 0.0, NEG_BIG).reshape(tk, tq)
            bias_ref[pl.ds(k0, tk), :] = jnp.where(k0 + key_i <= qry_i, bias, NEG_BIG)
            return carry
        lax.fori_loop(0, nkb, block, 0)

    def select_ranked():
        need = n_sel - count(lambda blk: blk > thr)
        ur = lax.broadcasted_iota(jnp.int32, (tk, tk), 0)
        uc = lax.broadcasted_iota(jnp.int32, (tk, tk), 1)
        earlier = (uc < ur).astype(BF16)
        ones8 = jnp.ones((8, tk), BF16)

        def block(kb, seen):
            k0 = pl.multiple_of(kb * tk, tk)
            blk = rows8(key_ref[pl.ds(k0, tk), :])
            eq = blk == thr[None]
            eqb = jnp.where(eq, 1.0, 0.0).reshape(tk, tq).astype(BF16)
            rank = rows8(_dot(earlier, eqb)) + seen[None]
            sel = (blk > thr[None]) | (eq & (rank < need[None]))
            bias = jnp.where(sel, 0.0, NEG_BIG).reshape(tk, tq)
            bias_ref[pl.ds(k0, tk), :] = jnp.where(k0 + key_i <= qry_i, bias, NEG_BIG)
            return seen + _dot(ones8, eqb)

        lax.fori_loop(0, nkb, block, jnp.zeros((8, tq), F32))

    pl.when(any_over)(select_ranked)
    pl.when(jnp.logical_not(any_over))(select_plain)

    qdt = qdt_ref[0]
    zq = jnp.zeros((LANES - DSA_HEAD_DIM, tq), BF16)
    qz = jnp.concatenate([jnp.concatenate([qdt[hd * DSA_HEAD_DIM:(hd + 1) * DSA_HEAD_DIM, :], zq], axis=0)
                          for hd in range(DSA_HEADS)], axis=1)
    wide = DSA_HEADS * tq

    def att_block(kb, carry):
        m_i, l_i, acc = carry
        k0 = pl.multiple_of(kb * tk, tk)
        bias = bias_ref[pl.ds(k0, tk), :]
        s = _dot(dkv_ref[pl.ds(k0, tk), :], qz) + jnp.concatenate([bias] * DSA_HEADS, axis=1)
        m_new = jnp.maximum(m_i, jnp.max(s, axis=0, keepdims=True))
        p = jnp.exp2(s - m_new)
        alpha = jnp.exp2(m_i - m_new)
        l_new = alpha * l_i + jnp.sum(p, axis=0, keepdims=True)
        pv = _dot(dkvt_ref[0, DSA_HEAD_DIM:, pl.ds(k0, tk)], p.astype(BF16))
        return m_new, l_new, alpha * acc + pv

    init = (jnp.full((1, wide), NEG_BIG, F32), jnp.zeros((1, wide), F32), jnp.zeros((DSA_HEAD_DIM, wide), F32))
    _, l_f, acc = lax.fori_loop(0, nkb, att_block, init)
    o_all = acc / l_f
    o_ref[...] = jnp.concatenate([o_all[:, hd * tq:(hd + 1) * tq] for hd in range(DSA_HEADS)],
                                 axis=0).T.astype(BF16)


def _dsa(qdt, iqt, iwt, dkv, dkvt, ikw, batch, seq):
    tq, tk = DSA_TQ, DSA_TK
    nq = seq // tq
    n_sel = min(DSA_TOPK, seq // 4)
    return pl.pallas_call(
        functools.partial(_dsa_kernel, tq=tq, tk=tk, n_sel=n_sel),
        out_shape=jax.ShapeDtypeStruct((batch * seq, 256), BF16),
        grid=(batch, nq),
        in_specs=[pl.BlockSpec((1, 256, tq), lambda b, i: (b, 0, i)),
                  pl.BlockSpec((1, 128, tq), lambda b, i: (b, 0, i)),
                  pl.BlockSpec((1, 8, tq), lambda b, i: (b, 0, i)),
                  pl.BlockSpec((seq, 128), lambda b, i: (b, 0)),
                  pl.BlockSpec((1, 128, seq), lambda b, i: (b, 0, 0)),
                  pl.BlockSpec((seq, 128), lambda b, i: (b, 0))],
        out_specs=pl.BlockSpec((tq, 256), lambda b, i: (b * nq + i, 0)),
        scratch_shapes=[pltpu.VMEM((seq, tq), jnp.int32), pltpu.VMEM((seq, tq), F32),
                        pltpu.VMEM((seq, tq), jnp.int16)],
        compiler_params=_params(("parallel", "parallel")),
        name="dsa",
    )(qdt, iqt, iwt, dkv, dkvt, ikw)


def _mem_kv_kernel(mem_ref, g_ref, w_ref, kt_ref, v_ref):
    mn = _rms(mem_ref[0], g_ref[...]).astype(BF16)
    kv = _dot(mn, w_ref[...])
    kt_ref[0] = kv[:, :256].T.astype(BF16)
    v_ref[0] = kv[:, 256:].astype(BF16)


def _mem_kv(mem, gain, w_kv):
    b, m, d = mem.shape
    return pl.pallas_call(
        _mem_kv_kernel,
        out_shape=(jax.ShapeDtypeStruct((b, 256, m), BF16), jax.ShapeDtypeStruct((b, m, 256), BF16)),
        grid=(b,),
        in_specs=[pl.BlockSpec((1, m, d), lambda i: (i, 0, 0)), _const_spec((1, d)), _const_spec((d, 512))],
        out_specs=(pl.BlockSpec((1, 256, m), lambda i: (i, 0, 0)), pl.BlockSpec((1, m, 256), lambda i: (i, 0, 0))),
        compiler_params=_params(("parallel",)),
        name="mem_kv",
    )(mem, gain, w_kv)


def _merge_kernel(x_ref, ya_ref, yb_ref, yc_ref, yd_ref, gmix_ref, wg_ref, wbr_ref, wout_ref,
                  gq_ref, wq_ref, mkt_ref, mv_ref, wo_ref, gffn_ref, wr_ref, br_ref,
                  x2_ref, rt_ref, cnt_ref, run_ref, *, tm):
    x = x_ref[...]
    d = x.shape[-1]
    h = _rms(x, gmix_ref[...]).astype(BF16)
    merged = jnp.zeros((tm, d), F32)
    for n, y_ref in enumerate((ya_ref, yb_ref, yc_ref, yd_ref)):
        gate = _sigmoid(_dot(h, wg_ref[:, n * d:(n + 1) * d]))
        merged = merged + gate * _dot(y_ref[...], wbr_ref[n])
    x1 = x + _dot(merged.astype(BF16), wout_ref[...])
    h2 = _rms(x1, gq_ref[...]).astype(BF16)
    q = (_dot(h2, wq_ref[...]) * (MEM_HEAD_DIM ** -0.5)).astype(BF16)
    lane_head = lax.broadcasted_iota(jnp.int32, (tm, 256), 1) // MEM_HEAD_DIM
    mv = mv_ref[0]
    o = jnp.zeros((tm, 256), F32)
    for hd in range(MEM_HEADS):
        s = _dot(q[:, hd * MEM_HEAD_DIM:(hd + 1) * MEM_HEAD_DIM], mkt_ref[0, hd * MEM_HEAD_DIM:(hd + 1) * MEM_HEAD_DIM, :])
        p = jnp.exp(s - jnp.max(s, axis=-1, keepdims=True))
        p = p / jnp.sum(p, axis=-1, keepdims=True)
        o = o + jnp.where(lane_head == hd, _dot(p.astype(BF16), mv), 0.0)
    x2 = x1 + _dot(o.astype(BF16), wo_ref[...])
    x2_ref[...] = x2
    h3 = _rms(x2, gffn_ref[...]).astype(BF16)
    logits = _dot(h3, wr_ref[...]) + br_ref[...]
    lane = lax.broadcasted_iota(jnp.int32, (tm, LANES), 1)
    gl = jnp.where(lane < MOE_GROUPS, logits, -jnp.inf)
    gmax = jnp.max(gl, axis=-1, keepdims=True)
    gsel = jnp.min(jnp.where(gl == gmax, lane, LANES), axis=-1, keepdims=True)
    pg_sel = 1.0 / jnp.sum(jnp.exp(gl - gmax), axis=-1, keepdims=True)
    in_group = (lane - ROUTER_EXPERT_LANE) // MOE_EXPERTS_PER_GROUP == gsel
    el = jnp.where(in_group, logits, -jnp.inf)
    m1 = jnp.max(el, axis=-1, keepdims=True)
    i1 = jnp.min(jnp.where(el == m1, lane, LANES), axis=-1, keepdims=True)
    el2 = jnp.where(lane == i1, -jnp.inf, el)
    m2 = jnp.max(el2, axis=-1, keepdims=True)
    i2 = jnp.min(jnp.where(el2 == m2, lane, LANES), axis=-1, keepdims=True)
    e21 = jnp.exp(m2 - m1)
    c1 = pg_sel / (1.0 + e21)
    @pl.when(pl.program_id(0) == 0)
    def _():
        run_ref[...] = jnp.zeros_like(run_ref)

    oh1 = jnp.where(lane == i1, 1.0, 0.0)
    oh2 = jnp.where(lane == i2, 1.0, 0.0)
    both = oh1 + oh2
    tr = lax.broadcasted_iota(jnp.int32, (tm, tm), 0)
    tc = lax.broadcasted_iota(jnp.int32, (tm, tm), 1)
    before = _dot(jnp.where(tc < tr, 1.0, 0.0).astype(BF16), both.astype(BF16)) + run_ref[0:1, :]
    r1 = jnp.sum(oh1 * before, axis=-1, keepdims=True)
    r2 = jnp.sum(oh2 * before, axis=-1, keepdims=True)
    total = run_ref[...] + jnp.sum(both, axis=0, keepdims=True)
    run_ref[...] = total
    cnt_ref[...] = total
    ids = (jnp.where(lane == 0, i1, i2) - ROUTER_EXPERT_LANE).astype(F32)
    rt_ref[...] = jnp.where(lane < 2, ids, jnp.where(lane == 2, c1, jnp.where(lane == 3, c1 * e21,
                            jnp.where(lane == 4, r1, jnp.where(lane == 5, r2, 0.0)))))


def _merge(x, ys, gmix, wg, wbr, wout, gq, wq, mkt, mv, wo, gffn, wr, br, batch, seq):
    n, d = x.shape
    tm = MERGE_TM
    spt = seq // tm
    m = mv.shape[1]
    tok = lambda w: pl.BlockSpec((tm, w), lambda i: (i, 0))
    return pl.pallas_call(
        functools.partial(_merge_kernel, tm=tm),
        out_shape=(jax.ShapeDtypeStruct((n, d), F32), jax.ShapeDtypeStruct((n, LANES), F32),
                   jax.ShapeDtypeStruct((8, LANES), F32)),
        grid=(n // tm,),
        in_specs=[tok(d), tok(256), tok(256), tok(256), tok(256),
                  _const_spec((1, d)), _const_spec((d, N_BRANCH * d)), _const_spec((N_BRANCH, 256, d)),
                  _const_spec((d, d)), _const_spec((1, d)), _const_spec((d, 256)),
                  pl.BlockSpec((1, 256, m), lambda i: (i // spt, 0, 0)),
                  pl.BlockSpec((1, m, 256), lambda i: (i // spt, 0, 0)),
                  _const_spec((256, d)), _const_spec((1, d)), _const_spec((d, LANES)), _const_spec((1, LANES))],
        out_specs=(tok(d), tok(LANES), pl.BlockSpec((8, LANES), lambda i: (0, 0))),
        scratch_shapes=[pltpu.VMEM((8, LANES), F32)],
        compiler_params=_params(("arbitrary",)),
        name="merge_mem_router",
    )(x, *ys, gmix, wg, wbr, wout, gq, wq, mkt, mv, wo, gffn, wr, br)


def _sc_gather_rows(table, idx):
    _, width = table.shape
    total = idx.shape[0]
    chunk, nbuf = SC_GATHER_CHUNK, SC_GATHER_BUFS
    workers = SC_CORES * SC_SUBCORES
    per_w = total // workers
    nch = per_w // chunk
    assert total % (workers * chunk * nbuf) == 0
    mesh = plsc.VectorSubcoreMesh(core_axis_name="c", subcore_axis_name="s")

    @functools.partial(
        pl.kernel, mesh=mesh, out_type=jax.ShapeDtypeStruct((total, width), table.dtype),
        scratch_types=[pltpu.VMEM((nch, chunk), jnp.int32), pltpu.VMEM((nbuf, chunk, width), table.dtype),
                       pltpu.SemaphoreType.DMA((nbuf,)), pltpu.SemaphoreType.DMA((nbuf,))])
    def gather_kernel(table_hbm, idx_hbm, out_hbm, idx_v, rows_v, gsem, wsem):
        wid = lax.axis_index("s") * SC_CORES + lax.axis_index("c")
        pltpu.sync_copy(idx_hbm.at[wid], idx_v)

        def gather(j, slot):
            return pltpu.make_async_copy(table_hbm.at[idx_v.at[j]], rows_v.at[slot], gsem.at[slot])

        def write(j, slot):
            off = pl.multiple_of(wid * per_w + j * chunk, chunk)
            return pltpu.make_async_copy(rows_v.at[slot], out_hbm.at[pl.ds(off, chunk)], wsem.at[slot])

        for slot in range(nbuf):
            gather(slot, slot).start()

        @pl.loop(0, nch // nbuf)
        def _(g):
            for slot in range(nbuf):
                j = g * nbuf + slot
                gather(j, slot).wait()
                write(j, slot).start()
                write(j, slot).wait()

                @pl.when(j + nbuf < nch)
                def _():
                    gather(j + nbuf, slot).start()

    return gather_kernel(table, idx.reshape(workers, nch, chunk))


def _sc_scatter_rows(table, dest2, total):
    n, width = table.shape
    chunk, nbuf = SC_GATHER_CHUNK, SC_GATHER_BUFS
    workers = SC_CORES * SC_SUBCORES
    per_w = n // workers
    nch = per_w // chunk
    assert n % (workers * chunk * nbuf) == 0
    mesh = plsc.VectorSubcoreMesh(core_axis_name="c", subcore_axis_name="s")

    @functools.partial(
        pl.kernel, mesh=mesh, out_type=jax.ShapeDtypeStruct((total, width), table.dtype),
        scratch_types=[pltpu.VMEM((2, nch, chunk), jnp.int32), pltpu.VMEM((nbuf, chunk, width), table.dtype),
                       pltpu.SemaphoreType.DMA((nbuf,)), pltpu.SemaphoreType.DMA((nbuf,))])
    def scatter_kernel(table_hbm, idx_hbm, out_hbm, idx_v, rows_v, rsem, wsem):
        wid = lax.axis_index("s") * SC_CORES + lax.axis_index("c")
        pltpu.sync_copy(idx_hbm.at[wid], idx_v)

        def read(j, slot):
            off = pl.multiple_of(wid * per_w + j * chunk, chunk)
            return pltpu.make_async_copy(table_hbm.at[pl.ds(off, chunk)], rows_v.at[slot], rsem.at[slot])

        def write(j, slot, k):
            return pltpu.make_async_copy(rows_v.at[slot], out_hbm.at[idx_v.at[k, j]], wsem.at[slot])

        for slot in range(nbuf):
            read(slot, slot).start()

        @pl.loop(0, nch // nbuf)
        def _(g):
            for slot in range(nbuf):
                j = g * nbuf + slot
                read(j, slot).wait()
                write(j, slot, 0).start()
                write(j, slot, 1).start()
                write(j, slot, 0).wait()
                write(j, slot, 1).wait()

                @pl.when(j + nbuf < nch)
                def _():
                    read(j + nbuf, slot).start()

    idx = dest2.reshape(2, workers, nch, chunk).transpose(1, 0, 2, 3)
    return scatter_kernel(table, idx)


def _dispatch_plan(rt, cnt, n):
    ne, blk = MOE_N_EXPERTS, MOE_BLOCK
    n_blocks = (2 * n) // blk + ne
    experts = jnp.arange(ne, dtype=jnp.int32)
    counts = cnt[0, ROUTER_EXPERT_LANE:ROUTER_EXPERT_LANE + ne].astype(jnp.int32)
    padded = (counts + blk - 1) // blk * blk
    pend = jnp.cumsum(padded)
    pstart = pend - padded
    ids = rt[:, 0:2].astype(jnp.int32)
    pos = rt[:, 4:6].astype(jnp.int32)
    first_row = jnp.sum(jnp.where(ids[:, :, None] == experts[None, None, :], pstart[None, None, :], 0), axis=-1)
    dest2 = (first_row + pos).T
    b0 = jnp.arange(n_blocks, dtype=jnp.int32) * blk
    block_e = jnp.minimum(jnp.sum((pend[None, :] <= b0[:, None]).astype(jnp.int32), axis=1), ne - 1)
    n_valid = jnp.clip(counts[block_e] - (b0 - pstart[block_e]), 0, blk).astype(jnp.int32)
    return dest2, block_e, n_valid


def _expert_block_kernel(be_ref, nv_ref, xs_ref, g_ref, wg_ref, wu_ref, wd_ref, o_ref):
    del be_ref
    valid = nv_ref[pl.program_id(0)]

    @pl.when(valid > 0)
    def _():
        row = lax.broadcasted_iota(jnp.int32, xs_ref.shape, 0)
        xs = jnp.where(row < valid, xs_ref[...], 0.0)
        h = _rms(xs, g_ref[...]).astype(BF16)
        gt = _dot(h, wg_ref[0])
        hid = gt * _sigmoid(gt) * _dot(h, wu_ref[0])
        o_ref[...] = _dot(hid.astype(BF16), wd_ref[0])

    @pl.when(valid == 0)
    def _():
        o_ref[...] = jnp.zeros_like(o_ref)


def _expert_blocks(xs, gain, wg, wu, wd, block_e, n_used):
    p_rows, d = xs.shape
    hid = wg.shape[-1]
    blk = MOE_BLOCK
    grid_spec = pltpu.PrefetchScalarGridSpec(
        num_scalar_prefetch=2, grid=(p_rows // blk,),
        in_specs=[pl.BlockSpec((blk, d), lambda b, be, nu: (b, 0)),
                  pl.BlockSpec((1, d), lambda b, be, nu: (0, 0)),
                  pl.BlockSpec((1, d, hid), lambda b, be, nu: (be[b], 0, 0)),
                  pl.BlockSpec((1, d, hid), lambda b, be, nu: (be[b], 0, 0)),
                  pl.BlockSpec((1, hid, d), lambda b, be, nu: (be[b], 0, 0))],
        out_specs=pl.BlockSpec((blk, d), lambda b, be, nu: (b, 0)))
    return pl.pallas_call(
        _expert_block_kernel, out_shape=jax.ShapeDtypeStruct((p_rows, d), F32), grid_spec=grid_spec,
        compiler_params=_params(("parallel",)), name="moe_expert_blocks",
    )(block_e, n_used, xs, gain, wg, wu, wd)


def _combine_kernel(x_ref, y1_ref, y2_ref, rt_ref, gfin_ref, o_ref, *, final_norm):
    out = x_ref[...] + rt_ref[:, 2:3] * y1_ref[...] + rt_ref[:, 3:4] * y2_ref[...]
    o_ref[...] = _rms(out, gfin_ref[...]) if final_norm else out


def _combine(x2, y_halves, rt, gfin, final_norm):
    n, d = x2.shape
    tm = COMBINE_TM
    nt = n // tm
    return pl.pallas_call(
        functools.partial(_combine_kernel, final_norm=final_norm),
        out_shape=jax.ShapeDtypeStruct((n, d), F32),
        grid=(nt,),
        in_specs=[pl.BlockSpec((tm, d), lambda i: (i, 0)),
                  pl.BlockSpec((tm, d), lambda i: (i, 0)),
                  pl.BlockSpec((tm, d), lambda i: (i + nt, 0)),
                  pl.BlockSpec((tm, LANES), lambda i: (i, 0)), _const_spec((1, d))],
        out_specs=pl.BlockSpec((tm, d), lambda i: (i, 0)),
        compiler_params=_params(("parallel",)), name="moe_combine",
    )(x2, y_halves, y_halves, rt, gfin)


def _moe(x2, rt, cnt, gffn, wg, wu, wd, gfin, final_norm):
    n, d = x2.shape
    dest2, block_e, n_valid = _dispatch_plan(rt, cnt, n)
    xs = _sc_scatter_rows(x2, dest2, block_e.shape[0] * MOE_BLOCK)
    yb = _expert_blocks(xs, gffn, wg, wu, wd, block_e, n_valid)
    y_halves = _sc_gather_rows(yb, dest2.reshape(2 * n))
    return _combine(x2, y_halves, rt, gfin, final_norm)


def kernel(x, mem, positions, norm_mix, w_in, diff_lambda, hgrn_lb_logits, spatial_w, spatial_b, w_branch, w_out,
           norm_mem_q, norm_mem_kv, w_mem_q, w_mem_kv, w_mem_o, norm_ffn, w_router_group, b_router_group,
           w_router_expert, b_router_expert, w_exp_gate, w_exp_up, w_exp_down, norm_final):
    batch, seq, d = x.shape
    depth = w_in.shape[0]
    n = batch * seq
    xf = x.reshape(n, d)
    tabs = _rope_tables(positions)
    row = lambda v: v.reshape(1, -1).astype(F32)
    for l in range(depth):
        lam_init = 0.8 - 0.6 * math.exp(-0.3 * l)
        w1, w_gate = _split_w_in(w_in, l)
        sw = spatial_w[l].reshape(SGU_GROUPS * SGU_CHUNK, SGU_CHUNK)
        sb = jnp.repeat(spatial_b[l].T, SGU_GROUP_DIM, axis=1)
        qat, ka, vat, hb, y_c, qdt, iqt, dkv, dkvt, ikw, iwt = _projection(
            xf, row(norm_mix[l]), w1, tabs, sw, sb, batch, seq)
        y_a = _diff_attention(diff_lambda[l], qat, ka, vat, lam_init, batch, seq)
        y_b = _hgrn(hgrn_lb_logits, hb, l, batch, seq)
        y_d = _dsa(qdt, iqt, iwt, dkv, dkvt, ikw, batch, seq)
        mkt, mv = _mem_kv(mem, row(norm_mem_kv[l]), w_mem_kv[l].astype(BF16))
        e0, e1 = ROUTER_EXPERT_LANE, ROUTER_EXPERT_LANE + MOE_N_EXPERTS
        wr = jnp.zeros((d, LANES), F32)
        wr = wr.at[:, :MOE_GROUPS].set(w_router_group[l]).at[:, e0:e1].set(w_router_expert[l]).astype(BF16)
        br = jnp.zeros((1, LANES), F32)
        br = br.at[0, :MOE_GROUPS].set(b_router_group[l]).at[0, e0:e1].set(b_router_expert[l])
        x2, rt, cnt = _merge(xf, (y_a, y_b, y_c, y_d), row(norm_mix[l]), w_gate, w_branch[l].astype(BF16),
                             w_out[l].astype(BF16), row(norm_mem_q[l]), w_mem_q[l].astype(BF16), mkt, mv,
                             w_mem_o[l].astype(BF16), row(norm_ffn[l]), wr, br, batch, seq)
        xf = _moe(x2, rt, cnt, row(norm_ffn[l]), w_exp_gate[l].astype(BF16), w_exp_up[l].astype(BF16),
                  w_exp_down[l].astype(BF16), row(norm_final), final_norm=(l == depth - 1))
    return xf.reshape(batch, seq, d)
```

```python
import functools
import math

import numpy as np
import jax
import jax.numpy as jnp
from jax import lax
from jax.experimental import pallas as pl
from jax.experimental.pallas import tpu as pltpu
from jax.experimental.pallas import tpu_sc as plsc

F32 = jnp.float32
BF16 = jnp.bfloat16

NORM_EPS = 1e-6
ROPE_THETA = 10000.0
NEG_BIG = -1e30

N_BRANCH = 4
BRANCH_WIDTH = 256
DIFF_HEADS = 4
DIFF_HEAD_DIM = 32
HGRN_HEADS = 4
HGRN_DIM = 64
HGRN_CHUNK = 64
HGRN_MIN_FORGET = 1e-30
SGU_GROUPS = 4
SGU_GROUP_DIM = 64
SGU_CHUNK = 128
DSA_HEADS = 4
DSA_HEAD_DIM = 64
DSA_IDX_HEADS = 4
DSA_IDX_DIM = 32
DSA_TOPK = 256
MEM_HEADS = 4
MEM_HEAD_DIM = 64
MOE_GROUPS = 4
MOE_EXPERTS_PER_GROUP = 8
MOE_N_EXPERTS = 32
MOE_BLOCK = 512
ROUTER_EXPERT_LANE = 32
SC_CORES = 2
SC_SUBCORES = 16
SC_GATHER_CHUNK = 16
SC_GATHER_BUFS = 4

LANES = 128
VMEM_LIMIT = 56 * 1024 * 1024

PROJ_TM = 512
DIFF_TQ = 256
DIFF_TK = 512
HGRN_TC = 512
DSA_TQ = 256
DSA_TK = 512
MERGE_TM = 512
COMBINE_TM = 512
ROPE_TM = 1024

C_AQ, C_AK, C_AV = 0, 256, 512
C_HB = 768
C_UV = 1792
C_DQ = 2304
C_DKV = 2560
C_IQ = 2688
C_IKW = 2816
IW_LANE = 32
C_GATE = 2852
C_TOTAL = 2944
LOG2E = math.log2(math.e)


def _params(sem):
    return pltpu.CompilerParams(dimension_semantics=sem, vmem_limit_bytes=VMEM_LIMIT)


def _const_spec(shape):
    nd = len(shape)
    return pl.BlockSpec(shape, lambda *_: (0,) * nd, pipeline_mode=pl.Buffered(1))


def _rms(xf, gain=None):
    y = xf * lax.rsqrt(jnp.mean(xf * xf, axis=-1, keepdims=True) + NORM_EPS)
    return y if gain is None else y * gain


def _sigmoid(x):
    return 0.5 * jnp.tanh(0.5 * x) + 0.5


def _dot(a, b):
    return jnp.dot(a, b, preferred_element_type=F32)


def _dot_nt(a, b):
    return lax.dot_general(a, b, (((1,), (1,)), ((), ())), preferred_element_type=F32)


def _rope_table_kernel(pos_ref, frq_ref, sgn_ref, c32_ref, s32_ref, c64_ref, s64_ref):
    pos = pos_ref[...].astype(F32)
    twice = lambda t: jnp.concatenate([t, t], axis=1)
    a32 = pos * frq_ref[0:1, :LANES]
    a64 = pos * frq_ref[1:2, :LANES]
    c32_ref[...] = twice(jnp.cos(a32))
    s32_ref[...] = twice(jnp.sin(a32) * sgn_ref[0:1, :LANES])
    c64_ref[...] = twice(jnp.cos(a64))
    s64_ref[...] = twice(jnp.sin(a64) * sgn_ref[1:2, :LANES])


def _rope_tables(positions):
    n = positions.size
    pos = positions.reshape(n, 1).astype(jnp.int32)
    lane = np.arange(256)
    inv32 = ROPE_THETA ** (-jnp.arange(16, dtype=F32) * (2.0 / 32))
    inv64 = ROPE_THETA ** (-jnp.arange(32, dtype=F32) * (2.0 / 64))
    frq = jnp.stack([inv32[lane % 16], inv64[lane % 32]])
    sgn = jnp.asarray(np.stack([np.where(lane % 32 < 16, -1.0, 1.0),
                                np.where(lane % 64 < 32, -1.0, 1.0)]), F32)
    tm = ROPE_TM
    tab = jax.ShapeDtypeStruct((n, 256), F32)
    return pl.pallas_call(
        _rope_table_kernel,
        out_shape=(tab, tab, tab, tab),
        grid=(n // tm,),
        in_specs=[pl.BlockSpec((tm, 1), lambda i: (i, 0)), _const_spec((2, 256)), _const_spec((2, 256))],
        out_specs=tuple(pl.BlockSpec((tm, 256), lambda i: (i, 0)) for _ in range(4)),
        compiler_params=_params(("parallel",)),
        name="rope_tables",
    )(pos, frq, sgn)


def _split_w_in_kernel(w_ref, w1_ref, wg_ref):
    w1_ref[...] = w_ref[:, :C_TOTAL].astype(BF16)
    a0 = (C_GATE // LANES) * LANES
    tail = w_ref[:, a0:]
    wg_ref[...] = tail[:, C_GATE - a0:].astype(BF16)


def _split_w_in(w_in, layer):
    _, d, width = w_in.shape
    tr = 128
    return pl.pallas_call(
        _split_w_in_kernel,
        out_shape=(jax.ShapeDtypeStruct((d, C_TOTAL), BF16), jax.ShapeDtypeStruct((d, width - C_GATE), BF16)),
        grid=(d // tr,),
        in_specs=[pl.BlockSpec((None, tr, width), lambda i: (layer, i, 0))],
        out_specs=(pl.BlockSpec((tr, C_TOTAL), lambda i: (i, 0)), pl.BlockSpec((tr, width - C_GATE), lambda i: (i, 0))),
        compiler_params=_params(("parallel",)),
        name="split_w_in",
    )(w_in)


def _gelu_tanh(x):
    return 0.5 * x * (1.0 + jnp.tanh(math.sqrt(2.0 / math.pi) * (x + 0.044715 * (x * x * x))))


def _rope(x, cos, sin_signed, half):
    w = x.shape[-1]
    lane = lax.broadcasted_iota(jnp.int32, x.shape, 1)
    partner = jnp.where(lane % (2 * half) < half, pltpu.roll(x, w - half, 1), pltpu.roll(x, half, 1))
    return x * cos + partner * sin_signed


def _proj_kernel(x_ref, g_ref, w_ref, c32_ref, s32_ref, c64_ref, s64_ref, sw_ref, sb_ref,
                 qat_ref, ka_ref, vat_ref, hb_ref, yc_ref, qdt_ref, iqt_ref, dkv_ref, dkvt_ref, ikw_ref, iwt_ref,
                 *, tm):
    h = _rms(x_ref[...], g_ref[...]).astype(BF16)

    def proj(c0, width):
        return _dot(h, w_ref[:, c0:c0 + width])

    c32, s32, c64, s64 = c32_ref[...], s32_ref[...], c64_ref[...], s64_ref[...]
    qat_ref[0] = (_rope(proj(C_AQ, 256), c32, s32, 16) * (DIFF_HEAD_DIM ** -0.5 * LOG2E)).T.astype(BF16)
    ka_ref[...] = _rope(proj(C_AK, 256), c32, s32, 16).astype(BF16)
    vat_ref[0] = proj(C_AV, 256).astype(BF16).T
    hb_ref[...] = proj(C_HB, 1024)
    qdt_ref[0] = (_rope(proj(C_DQ, 256), c64, s64, 32) * (DSA_HEAD_DIM ** -0.5 * LOG2E)).T.astype(BF16)
    iqt_ref[0] = _rope(proj(C_IQ, 128), c32[:, :128], s32[:, :128], 16).T.astype(BF16)
    lane = lax.broadcasted_iota(jnp.int32, (tm, 128), 1)
    is_k = lane < DSA_HEAD_DIM
    dkv = _rope(proj(C_DKV, 128), jnp.where(is_k, c64[:, :128], 1.0), jnp.where(is_k, s64[:, :128], 0.0), 32)
    is_ik = lane < DSA_IDX_DIM
    ikw = _rope(proj(C_IKW, 128), jnp.where(is_ik, c32[:, :128], 1.0), jnp.where(is_ik, s32[:, :128], 0.0), 16)
    dkv_ref[...] = dkv.astype(BF16)
    dkvt_ref[0] = dkv.T.astype(BF16)
    ikw_ref[...] = ikw.astype(BF16)
    iw_scale = DSA_IDX_HEADS ** -0.5 * DSA_IDX_DIM ** -0.5
    iwt_ref[0] = (ikw * iw_scale).T[IW_LANE:IW_LANE + 8, :]
    uv = _gelu_tanh(proj(C_UV, 512))
    u, v = uv[:, :256], uv[:, 256:]
    mu = jnp.mean(v, axis=-1, keepdims=True)
    vc = v - mu
    vn = (vc * lax.rsqrt(jnp.mean(vc * vc, axis=-1, keepdims=True) + NORM_EPS)).astype(BF16)
    r = lax.broadcasted_iota(jnp.int32, (SGU_GROUPS * SGU_CHUNK, SGU_CHUNK), 0)
    c = lax.broadcasted_iota(jnp.int32, (SGU_GROUPS * SGU_CHUNK, SGU_CHUNK), 1)
    wt = jnp.where((r % SGU_CHUNK) >= c, sw_ref[...], 0.0).astype(BF16)
    lane_grp = lax.broadcasted_iota(jnp.int32, (SGU_CHUNK, 256), 1) // SGU_GROUP_DIM
    for ch in range(tm // SGU_CHUNK):
        r0 = ch * SGU_CHUNK
        full = _dot(wt, vn[r0:r0 + SGU_CHUNK, :])
        mixed = sb_ref[...]
        for g in range(SGU_GROUPS):
            mixed = mixed + jnp.where(lane_grp == g, full[g * SGU_CHUNK:(g + 1) * SGU_CHUNK, :], 0.0)
        yc_ref[r0:r0 + SGU_CHUNK, :] = (u[r0:r0 + SGU_CHUNK, :] * mixed).astype(BF16)


def _projection(x, gain, w1, tabs, sw, sb, batch, seq):
    n, d = x.shape
    tm = PROJ_TM
    spt = seq // tm
    tok = lambda w: pl.BlockSpec((tm, w), lambda i: (i, 0))
    tr = lambda rows: pl.BlockSpec((1, rows, tm), lambda i: (i // spt, 0, i % spt))
    out_shape = (
        jax.ShapeDtypeStruct((batch, 256, seq), BF16),
        jax.ShapeDtypeStruct((n, 256), BF16),
        jax.ShapeDtypeStruct((batch, 256, seq), BF16),
        jax.ShapeDtypeStruct((n, 1024), F32),
        jax.ShapeDtypeStruct((n, 256), BF16),
        jax.ShapeDtypeStruct((batch, 256, seq), BF16),
        jax.ShapeDtypeStruct((batch, 128, seq), BF16),
        jax.ShapeDtypeStruct((n, 128), BF16),
        jax.ShapeDtypeStruct((batch, 128, seq), BF16),
        jax.ShapeDtypeStruct((n, 128), BF16),
        jax.ShapeDtypeStruct((batch, 8, seq), F32),
    )
    return pl.pallas_call(
        functools.partial(_proj_kernel, tm=tm),
        out_shape=out_shape,
        grid=(n // tm,),
        in_specs=[tok(d), _const_spec((1, d)), _const_spec((d, C_TOTAL)),
                  tok(256), tok(256), tok(256), tok(256),
                  _const_spec((SGU_GROUPS * SGU_CHUNK, SGU_CHUNK)), _const_spec((SGU_CHUNK, 256))],
        out_specs=(tr(256), tok(256), tr(256), tok(1024), tok(256), tr(256), tr(128), tok(128), tr(128), tok(128), tr(8)),
        compiler_params=_params(("parallel",)),
        name="projection",
    )(x, gain, w1, *tabs, sw, sb)


def _diff_attn_kernel(lam_ref, qt_ref, k_ref, vt_ref, o_ref, sa_ref, sb_ref, *, lam_init, tq, tk):
    q0 = pl.program_id(1) * tq
    kb_diag = q0 // tk
    lv = lam_ref[...]
    lam = (jnp.exp(jnp.sum(lv[0:1] * lv[1:2], axis=-1, keepdims=True))
           - jnp.exp(jnp.sum(lv[2:3] * lv[3:4], axis=-1, keepdims=True)) + lam_init)
    qt = qt_ref[0]
    feat = lax.broadcasted_iota(jnp.int32, (256, tq), 0) // DIFF_HEAD_DIM
    n_maps = 2 * DIFF_HEADS
    qz = jnp.concatenate([jnp.where(feat == i, qt, jnp.zeros_like(qt)) for i in range(n_maps)], axis=1)
    wide = n_maps * tq
    key_i = lax.broadcasted_iota(jnp.int32, (tk, wide), 0)
    qry_i = q0 + lax.broadcasted_iota(jnp.int32, (tk, wide), 1) % tq

    def scores(kb, dst_ref):
        k0 = pl.multiple_of(kb * tk, tk)
        dst_ref[...] = _dot(k_ref[pl.ds(k0, tk), :], qz)

    def absorb(src_ref, kb, carry, masked):
        m_i, l_i, acc = carry
        k0 = pl.multiple_of(kb * tk, tk)
        s = src_ref[...]
        if masked:
            s = jnp.where(k0 + key_i <= qry_i, s, NEG_BIG)
        m_new = jnp.maximum(m_i, jnp.max(s, axis=0, keepdims=True))
        p = jnp.exp2(s - m_new)
        alpha = jnp.exp2(m_i - m_new)
        l_new = alpha * l_i + jnp.sum(p, axis=0, keepdims=True)
        pb = p.astype(BF16)
        pv = jnp.concatenate(
            [_dot(vt_ref[0, hd * 64:(hd + 1) * 64, pl.ds(k0, tk)], pb[:, 2 * hd * tq:(2 * hd + 2) * tq])
             for hd in range(DIFF_HEADS)], axis=1)
        return m_new, l_new, alpha * acc + pv

    def pair(j, carry):
        kb = 2 * j
        scores(kb + 1, sb_ref)
        carry = absorb(sa_ref, kb, carry, False)
        scores(kb + 2, sa_ref)
        return absorb(sb_ref, kb + 1, carry, False)

    init = (jnp.full((1, wide), NEG_BIG, F32), jnp.zeros((1, wide), F32), jnp.zeros((64, wide), F32))
    scores(0, sa_ref)
    carry = lax.fori_loop(0, kb_diag // 2, pair, init)

    def tail_odd(carry):
        scores(kb_diag, sb_ref)
        carry = absorb(sa_ref, kb_diag - 1, carry, False)
        return absorb(sb_ref, kb_diag, carry, True)

    def tail_even(carry):
        return absorb(sa_ref, kb_diag, carry, True)

    _, l_f, acc = lax.cond(kb_diag % 2 == 1, tail_odd, tail_even, carry)
    o_all = acc / l_f
    heads = []
    for hd in range(DIFF_HEADS):
        o0 = o_all[:, 2 * hd * tq:(2 * hd + 1) * tq]
        o1 = o_all[:, (2 * hd + 1) * tq:(2 * hd + 2) * tq]
        o_h = o0 - lam * o1
        ms = jnp.mean(o_h * o_h, axis=0, keepdims=True)
        heads.append(o_h * lax.rsqrt(ms + NORM_EPS) * (1.0 - lam_init))
    o_ref[...] = jnp.concatenate(heads, axis=0).T.astype(BF16)


def _diff_attention(lam_vec, qat, ka, vat, lam_init, batch, seq):
    tq, tk = DIFF_TQ, DIFF_TK
    nq = seq // tq
    return pl.pallas_call(
        functools.partial(_diff_attn_kernel, lam_init=lam_init, tq=tq, tk=tk),
        out_shape=jax.ShapeDtypeStruct((batch * seq, 256), BF16),
        grid=(batch, nq),
        in_specs=[_const_spec((4, DIFF_HEAD_DIM)),
                  pl.BlockSpec((1, 256, tq), lambda b, i: (b, 0, i)),
                  pl.BlockSpec((seq, 256), lambda b, i: (b, 0)),
                  pl.BlockSpec((1, 256, seq), lambda b, i: (b, 0, 0))],
        out_specs=pl.BlockSpec((tq, 256), lambda b, i: (b * nq + i, 0)),
        scratch_shapes=[pltpu.VMEM((tk, 2 * DIFF_HEADS * tq), F32), pltpu.VMEM((tk, 2 * DIFF_HEADS * tq), F32)],
        compiler_params=_params(("parallel", "parallel")),
        name="diff_attention",
    )(lam_vec, qat, ka, vat)


def _hgrn_kernel(lbl_ref, hb_ref, o_ref, st_ref, pstk_ref, *, layer, tc):
    cz = HGRN_CHUNK
    w = 256

    @pl.when(pl.program_id(1) == 0)
    def _():
        st_ref[...] = jnp.zeros_like(st_ref)

    lg = lbl_ref[...]
    e = jnp.exp(lg - jnp.max(lg, axis=0, keepdims=True))
    lw = e / jnp.sum(e, axis=0, keepdims=True)
    lb = jnp.sum(lw[0:layer + 1], axis=0, keepdims=True) - lw[0:1]

    ri = lax.broadcasted_iota(jnp.int32, (cz, cz), 0)
    ci = lax.broadcasted_iota(jnp.int32, (cz, cz), 1)
    tri = (ri >= ci).astype(F32)
    rb = lax.broadcasted_iota(jnp.int32, (w, w), 0) // HGRN_DIM
    cb = lax.broadcasted_iota(jnp.int32, (w, w), 1) // HGRN_DIM
    same_head = rb == cb
    head_ones = same_head.astype(BF16)
    trows = {r: r + lax.broadcasted_iota(jnp.int32, (cz - r, w), 0) for r in range(0, cz, 16)}

    def chunk(c, carry):
        r0 = pl.multiple_of(c * cz, cz)
        q = hb_ref[pl.ds(r0, cz), 0:256]
        fp = hb_ref[pl.ds(r0, cz), 256:512]
        v = hb_ref[pl.ds(r0, cz), 512:768]
        g = hb_ref[pl.ds(r0, cz), 768:1024]
        qf = q * _sigmoid(q)
        f = lb + (1.0 - lb) * jax.nn.sigmoid(fp)
        log_f = jnp.log(jnp.maximum(f, HGRN_MIN_FORGET))
        kf = (1.0 - lb) * jax.nn.sigmoid(-fp)
        bc = jnp.dot(tri, log_f, preferred_element_type=F32, precision=lax.Precision.HIGHEST)
        st = st_ref[...]
        o = _dot_nt((qf * jnp.exp(bc)).astype(BF16), st.astype(BF16))
        for s in range(cz):
            r_lo = (s // 16) * 16
            arg = bc[r_lo:, :] - bc[s:s + 1, :]
            if s > r_lo:
                arg = jnp.where(trows[r_lo] >= s, arg, NEG_BIG)
            p = qf[r_lo:, :] * kf[s:s + 1, :] * jnp.exp(arg)
            if r_lo:
                pstk_ref[s * cz:s * cz + r_lo, :] = jnp.zeros((r_lo, w), BF16)
            pstk_ref[s * cz + r_lo:(s + 1) * cz, :] = p.astype(BF16)
        accs = [jnp.zeros((16, w), F32) for _ in range(cz // 16)]
        for sg in range(cz // 16):
            att = _dot(pstk_ref[sg * 16 * cz:(sg + 1) * 16 * cz, :], head_ones)
            for sl in range(16):
                s = sg * 16 + sl
                for j in range(sg, cz // 16):
                    accs[j] = accs[j] + att[sl * cz + 16 * j:sl * cz + 16 * j + 16, :] * v[s:s + 1, :]
        o = o + jnp.concatenate(accs, axis=0)
        b_end = bc[cz - 1:cz, :]
        kd = kf * jnp.exp(b_end - bc)
        upd = _dot(v.T.astype(BF16), kd.astype(BF16))
        st_ref[...] = st * jnp.exp(b_end) + jnp.where(same_head, upd, 0.0)
        ms = _dot(o * o, head_ones.astype(F32)) * (1.0 / HGRN_DIM)
        y = o * lax.rsqrt(ms + NORM_EPS)
        o_ref[pl.ds(r0, cz), :] = (y * (g * _sigmoid(g))).astype(BF16)
        return carry

    lax.fori_loop(0, tc // cz, chunk, 0)


def _hgrn(lb_logits, hb, layer, batch, seq):
    tc = HGRN_TC
    nt = seq // tc
    cz = HGRN_CHUNK
    return pl.pallas_call(
        functools.partial(_hgrn_kernel, layer=layer, tc=tc),
        out_shape=jax.ShapeDtypeStruct((batch * seq, 256), BF16),
        grid=(batch, nt),
        in_specs=[_const_spec(lb_logits.shape),
                  pl.BlockSpec((tc, 1024), lambda b, i: (b * nt + i, 0))],
        out_specs=pl.BlockSpec((tc, 256), lambda b, i: (b * nt + i, 0)),
        scratch_shapes=[pltpu.VMEM((256, 256), F32), pltpu.VMEM((cz * cz, 256), BF16)],
        compiler_params=_params(("parallel", "arbitrary")),
        name="hgrn2",
    )(lb_logits, hb)


def _dsa_kernel(qdt_ref, iqt_ref, iwt_ref, dkv_ref, dkvt_ref, ikw_ref, o_ref, key_ref, bias_ref, half_ref,
                *, tq, tk, n_sel):
    q0 = pl.program_id(1) * tq
    nkb = q0 // tk + 1
    key_i = lax.broadcasted_iota(jnp.int32, (tk, tq), 0)
    qry_i = q0 + lax.broadcasted_iota(jnp.int32, (tk, tq), 1)
    grp = tk // 8
    rows8 = lambda x: x.reshape(grp, 8, tq)
    iqt = iqt_ref[0]
    zpad = jnp.zeros((LANES - DSA_IDX_DIM, tq), BF16)
    iqz = jnp.concatenate([jnp.concatenate([iqt[hd * DSA_IDX_DIM:(hd + 1) * DSA_IDX_DIM, :], zpad], axis=0)
                           for hd in range(DSA_IDX_HEADS)], axis=1)
    iw = iwt_ref[0]

    def score_block(kb, carry):
        k0 = pl.multiple_of(kb * tk, tk)
        sh = jnp.maximum(_dot(ikw_ref[pl.ds(k0, tk), :], iqz), 0.0)
        sc = jnp.zeros((tk, tq), F32)
        for hd in range(DSA_IDX_HEADS):
            sc = sc + sh[:, hd * tq:(hd + 1) * tq] * iw[hd:hd + 1, :]
        sc = jnp.where(k0 + key_i <= qry_i, sc + 0.0, -jnp.inf)
        bits = pltpu.bitcast(sc, jnp.int32)
        key = jnp.where(bits < 0, bits ^ jnp.int32(0x7FFFFFFF), bits)
        key_ref[pl.ds(k0, tk), :] = key
        half_ref[pl.ds(k0, tk), :] = (key >> 16).astype(jnp.int16)
        return carry

    lax.fori_loop(0, nkb, score_block, 0)

    one16, zero16 = jnp.ones((), jnp.int16), jnp.zeros((), jnp.int16)
    low16 = np.int16(-2 ** 15)

    def count16(limit, strict):
        def body(kb, acc):
            k0 = pl.multiple_of(kb * tk, tk)
            for c in range(tk // 128):
                blk = half_ref[pl.ds(k0 + 128 * c, 128), :].reshape(8, 16, tq)
                hit = jnp.where(blk > limit if strict else blk >= limit, one16, zero16)
                parts = [hit[j] for j in range(8)]
                while len(parts) > 1:
                    parts = [a + b for a, b in zip(parts[0::2], parts[1::2])]
                acc = acc + parts[0]
            return acc
        acc = lax.fori_loop(0, nkb, body, jnp.zeros((16, tq), jnp.int16))
        return jnp.broadcast_to(jnp.sum(acc.astype(jnp.int32), axis=0, keepdims=True), (16, tq))

    def search16(need):
        t = jnp.full((16, tq), -2 ** 15, jnp.int32)
        for bit in range(15, -1, -1):
            trial = t + 2 ** bit
            t = jnp.where(count16(trial.astype(jnp.int16), False) >= need, trial, t)
        return t

    t_hi = search16(n_sel)
    t_hi16 = t_hi.astype(jnp.int16)
    need_lo = n_sel - count16(t_hi16, True)

    def low_block(kb, carry):
        k0 = pl.multiple_of(kb * tk, tk)
        lo = ((key_ref[pl.ds(k0, tk), :] & 0xFFFF) - 2 ** 15).astype(jnp.int16).reshape(tk // 16, 16, tq)
        hi = half_ref[pl.ds(k0, tk), :].reshape(tk // 16, 16, tq)
        half_ref[pl.ds(k0, tk), :] = jnp.where(hi == t_hi16, lo, low16).reshape(tk, tq)
        return carry

    lax.fori_loop(0, nkb, low_block, 0)
    t_lo = search16(need_lo)
    thr = ((t_hi << 16) | (t_lo + 2 ** 15))[0:8, :]

    need = (need_lo - count16(t_lo.astype(jnp.int16), True))[0:8, :].astype(F32)
    ur = lax.broadcasted_iota(jnp.int32, (tk, tk), 0)
    uc = lax.broadcasted_iota(jnp.int32, (tk, tk), 1)
    earlier = (uc < ur).astype(BF16)
    ones8 = jnp.ones((8, tk), BF16)

    def select_block(kb, seen):
        k0 = pl.multiple_of(kb * tk, tk)
        blk = rows8(key_ref[pl.ds(k0, tk), :])
        eq = blk == thr[None]
        eqb = jnp.where(eq, 1.0, 0.0).reshape(tk, tq).astype(BF16)
        rank = rows8(_dot(earlier, eqb)) + seen[None]
        sel = (blk > thr[None]) | (eq & (rank < need[None]))
        bias = jnp.where(sel, 0.0, NEG_BIG).reshape(tk, tq)
        bias_ref[pl.ds(k0, tk), :] = jnp.where(k0 + key_i <= qry_i, bias, NEG_BIG)
        return seen + _dot(ones8, eqb)

    lax.fori_loop(0, nkb, select_block, jnp.zeros((8, tq), F32))

    qdt = qdt_ref[0]
    zq = jnp.zeros((LANES - DSA_HEAD_DIM, tq), BF16)
    qz = jnp.concatenate([jnp.concatenate([qdt[hd * DSA_HEAD_DIM:(hd + 1) * DSA_HEAD_DIM, :], zq], axis=0)
                          for hd in range(DSA_HEADS)], axis=1)
    wide = DSA_HEADS * tq

    def att_block(kb, carry):
        m_i, l_i, acc = carry
        k0 = pl.multiple_of(kb * tk, tk)
        bias = bias_ref[pl.ds(k0, tk), :]
        s = _dot(dkv_ref[pl.ds(k0, tk), :], qz) + jnp.concatenate([bias] * DSA_HEADS, axis=1)
        m_new = jnp.maximum(m_i, jnp.max(s, axis=0, keepdims=True))
        p = jnp.exp2(s - m_new)
        alpha = jnp.exp2(m_i - m_new)
        l_new = alpha * l_i + jnp.sum(p, axis=0, keepdims=True)
        pv = _dot(dkvt_ref[0, DSA_HEAD_DIM:, pl.ds(k0, tk)], p.astype(BF16))
        return m_new, l_new, alpha * acc + pv

    init = (jnp.full((1, wide), NEG_BIG, F32), jnp.zeros((1, wide), F32), jnp.zeros((DSA_HEAD_DIM, wide), F32))
    _, l_f, acc = lax.fori_loop(0, nkb, att_block, init)
    o_all = acc / l_f
    o_ref[...] = jnp.concatenate([o_all[:, hd * tq:(hd + 1) * tq] for hd in range(DSA_HEADS)],
                                 axis=0).T.astype(BF16)


def _dsa(qdt, iqt, iwt, dkv, dkvt, ikw, batch, seq):
    tq, tk = DSA_TQ, DSA_TK
    nq = seq // tq
    n_sel = min(DSA_TOPK, seq // 4)
    return pl.pallas_call(
        functools.partial(_dsa_kernel, tq=tq, tk=tk, n_sel=n_sel),
        out_shape=jax.ShapeDtypeStruct((batch * seq, 256), BF16),
        grid=(batch, nq),
        in_specs=[pl.BlockSpec((1, 256, tq), lambda b, i: (b, 0, i)),
                  pl.BlockSpec((1, 128, tq), lambda b, i: (b, 0, i)),
                  pl.BlockSpec((1, 8, tq), lambda b, i: (b, 0, i)),
                  pl.BlockSpec((seq, 128), lambda b, i: (b, 0)),
                  pl.BlockSpec((1, 128, seq), lambda b, i: (b, 0, 0)),
                  pl.BlockSpec((seq, 128), lambda b, i: (b, 0))],
        out_specs=pl.BlockSpec((tq, 256), lambda b, i: (b * nq + i, 0)),
        scratch_shapes=[pltpu.VMEM((seq, tq), jnp.int32), pltpu.VMEM((seq, tq), F32),
                        pltpu.VMEM((seq, tq), jnp.int16)],
        compiler_params=_params(("parallel", "parallel")),
        name="dsa",
    )(qdt, iqt, iwt, dkv, dkvt, ikw)


def _mem_kv_kernel(mem_ref, g_ref, w_ref, kt_ref, v_ref):
    mn = _rms(mem_ref[0], g_ref[...]).astype(BF16)
    kv = _dot(mn, w_ref[...])
    kt_ref[0] = kv[:, :256].T.astype(BF16)
    v_ref[0] = kv[:, 256:].astype(BF16)


def _mem_kv(mem, gain, w_kv):
    b, m, d = mem.shape
    return pl.pallas_call(
        _mem_kv_kernel,
        out_shape=(jax.ShapeDtypeStruct((b, 256, m), BF16), jax.ShapeDtypeStruct((b, m, 256), BF16)),
        grid=(b,),
        in_specs=[pl.BlockSpec((1, m, d), lambda i: (i, 0, 0)), _const_spec((1, d)), _const_spec((d, 512))],
        out_specs=(pl.BlockSpec((1, 256, m), lambda i: (i, 0, 0)), pl.BlockSpec((1, m, 256), lambda i: (i, 0, 0))),
        compiler_params=_params(("parallel",)),
        name="mem_kv",
    )(mem, gain, w_kv)


def _merge_kernel(x_ref, ya_ref, yb_ref, yc_ref, yd_ref, gmix_ref, wg_ref, wbr_ref, wout_ref,
                  gq_ref, wq_ref, mkt_ref, mv_ref, wo_ref, gffn_ref, wr_ref, br_ref,
                  x2_ref, rt_ref, cnt_ref, run_ref, *, tm):
    x = x_ref[...]
    d = x.shape[-1]
    h = _rms(x, gmix_ref[...]).astype(BF16)
    merged = jnp.zeros((tm, d), F32)
    for n, y_ref in enumerate((ya_ref, yb_ref, yc_ref, yd_ref)):
        gate = _sigmoid(_dot(h, wg_ref[:, n * d:(n + 1) * d]))
        merged = merged + gate * _dot(y_ref[...], wbr_ref[n])
    x1 = x + _dot(merged.astype(BF16), wout_ref[...])
    h2 = _rms(x1, gq_ref[...]).astype(BF16)
    q = (_dot(h2, wq_ref[...]) * (MEM_HEAD_DIM ** -0.5)).astype(BF16)
    lane_head = lax.broadcasted_iota(jnp.int32, (tm, 256), 1) // MEM_HEAD_DIM
    mv = mv_ref[0]
    o = jnp.zeros((tm, 256), F32)
    for hd in range(MEM_HEADS):
        s = _dot(q[:, hd * MEM_HEAD_DIM:(hd + 1) * MEM_HEAD_DIM], mkt_ref[0, hd * MEM_HEAD_DIM:(hd + 1) * MEM_HEAD_DIM, :])
        p = jnp.exp(s - jnp.max(s, axis=-1, keepdims=True))
        p = p / jnp.sum(p, axis=-1, keepdims=True)
        o = o + jnp.where(lane_head == hd, _dot(p.astype(BF16), mv), 0.0)
    x2 = x1 + _dot(o.astype(BF16), wo_ref[...])
    x2_ref[...] = x2
    h3 = _rms(x2, gffn_ref[...]).astype(BF16)
    logits = _dot(h3, wr_ref[...]) + br_ref[...]
    lane = lax.broadcasted_iota(jnp.int32, (tm, LANES), 1)
    gl = jnp.where(lane < MOE_GROUPS, logits, -jnp.inf)
    gmax = jnp.max(gl, axis=-1, keepdims=True)
    gsel = jnp.min(jnp.where(gl == gmax, lane, LANES), axis=-1, keepdims=True)
    pg_sel = 1.0 / jnp.sum(jnp.exp(gl - gmax), axis=-1, keepdims=True)
    in_group = (lane - ROUTER_EXPERT_LANE) // MOE_EXPERTS_PER_GROUP == gsel
    el = jnp.where(in_group, logits, -jnp.inf)
    m1 = jnp.max(el, axis=-1, keepdims=True)
    i1 = jnp.min(jnp.where(el == m1, lane, LANES), axis=-1, keepdims=True)
    el2 = jnp.where(lane == i1, -jnp.inf, el)
    m2 = jnp.max(el2, axis=-1, keepdims=True)
    i2 = jnp.min(jnp.where(el2 == m2, lane, LANES), axis=-1, keepdims=True)
    e21 = jnp.exp(m2 - m1)
    c1 = pg_sel / (1.0 + e21)
    @pl.when(pl.program_id(0) == 0)
    def _():
        run_ref[...] = jnp.zeros_like(run_ref)

    oh1 = jnp.where(lane == i1, 1.0, 0.0)
    oh2 = jnp.where(lane == i2, 1.0, 0.0)
    both = oh1 + oh2
    tr = lax.broadcasted_iota(jnp.int32, (tm, tm), 0)
    tc = lax.broadcasted_iota(jnp.int32, (tm, tm), 1)
    before = _dot(jnp.where(tc < tr, 1.0, 0.0).astype(BF16), both.astype(BF16)) + run_ref[0:1, :]
    r1 = jnp.sum(oh1 * before, axis=-1, keepdims=True)
    r2 = jnp.sum(oh2 * before, axis=-1, keepdims=True)
    total = run_ref[...] + jnp.sum(both, axis=0, keepdims=True)
    run_ref[...] = total
    cnt_ref[...] = total
    ids = (jnp.where(lane == 0, i1, i2) - ROUTER_EXPERT_LANE).astype(F32)
    rt_ref[...] = jnp.where(lane < 2, ids, jnp.where(lane == 2, c1, jnp.where(lane == 3, c1 * e21,
                            jnp.where(lane == 4, r1, jnp.where(lane == 5, r2, 0.0)))))


def _merge(x, ys, gmix, wg, wbr, wout, gq, wq, mkt, mv, wo, gffn, wr, br, batch, seq):
    n, d = x.shape
    tm = MERGE_TM
    spt = seq // tm
    m = mv.shape[1]
    tok = lambda w: pl.BlockSpec((tm, w), lambda i: (i, 0))
    return pl.pallas_call(
        functools.partial(_merge_kernel, tm=tm),
        out_shape=(jax.ShapeDtypeStruct((n, d), F32), jax.ShapeDtypeStruct((n, LANES), F32),
                   jax.ShapeDtypeStruct((8, LANES), F32)),
        grid=(n // tm,),
        in_specs=[tok(d), tok(256), tok(256), tok(256), tok(256),
                  _const_spec((1, d)), _const_spec((d, N_BRANCH * d)), _const_spec((N_BRANCH, 256, d)),
                  _const_spec((d, d)), _const_spec((1, d)), _const_spec((d, 256)),
                  pl.BlockSpec((1, 256, m), lambda i: (i // spt, 0, 0)),
                  pl.BlockSpec((1, m, 256), lambda i: (i // spt, 0, 0)),
                  _const_spec((256, d)), _const_spec((1, d)), _const_spec((d, LANES)), _const_spec((1, LANES))],
        out_specs=(tok(d), tok(LANES), pl.BlockSpec((8, LANES), lambda i: (0, 0))),
        scratch_shapes=[pltpu.VMEM((8, LANES), F32)],
        compiler_params=_params(("arbitrary",)),
        name="merge_mem_router",
    )(x, *ys, gmix, wg, wbr, wout, gq, wq, mkt, mv, wo, gffn, wr, br)


def _sc_gather_rows(table, idx):
    _, width = table.shape
    total = idx.shape[0]
    chunk, nbuf = SC_GATHER_CHUNK, SC_GATHER_BUFS
    workers = SC_CORES * SC_SUBCORES
    per_w = total // workers
    nch = per_w // chunk
    assert total % (workers * chunk * nbuf) == 0
    mesh = plsc.VectorSubcoreMesh(core_axis_name="c", subcore_axis_name="s")

    @functools.partial(
        pl.kernel, mesh=mesh, out_type=jax.ShapeDtypeStruct((total, width), table.dtype),
        scratch_types=[pltpu.VMEM((nch, chunk), jnp.int32), pltpu.VMEM((nbuf, chunk, width), table.dtype),
                       pltpu.SemaphoreType.DMA((nbuf,)), pltpu.SemaphoreType.DMA((nbuf,))])
    def gather_kernel(table_hbm, idx_hbm, out_hbm, idx_v, rows_v, gsem, wsem):
        wid = lax.axis_index("s") * SC_CORES + lax.axis_index("c")
        pltpu.sync_copy(idx_hbm.at[wid], idx_v)

        def gather(j, slot):
            return pltpu.make_async_copy(table_hbm.at[idx_v.at[j]], rows_v.at[slot], gsem.at[slot])

        def write(j, slot):
            off = pl.multiple_of(wid * per_w + j * chunk, chunk)
            return pltpu.make_async_copy(rows_v.at[slot], out_hbm.at[pl.ds(off, chunk)], wsem.at[slot])

        for slot in range(nbuf):
            gather(slot, slot).start()

        @pl.loop(0, nch // nbuf)
        def _(g):
            for slot in range(nbuf):
                j = g * nbuf + slot
                gather(j, slot).wait()
                write(j, slot).start()
                write(j, slot).wait()

                @pl.when(j + nbuf < nch)
                def _():
                    gather(j + nbuf, slot).start()

    return gather_kernel(table, idx.reshape(workers, nch, chunk))


def _sc_scatter_rows(table, dest2, total):
    n, width = table.shape
    chunk, nbuf = SC_GATHER_CHUNK, SC_GATHER_BUFS
    workers = SC_CORES * SC_SUBCORES
    per_w = n // workers
    nch = per_w // chunk
    assert n % (workers * chunk * nbuf) == 0
    mesh = plsc.VectorSubcoreMesh(core_axis_name="c", subcore_axis_name="s")

    @functools.partial(
        pl.kernel, mesh=mesh, out_type=jax.ShapeDtypeStruct((total, width), table.dtype),
        scratch_types=[pltpu.VMEM((2, nch, chunk), jnp.int32), pltpu.VMEM((nbuf, chunk, width), table.dtype),
                       pltpu.SemaphoreType.DMA((nbuf,)), pltpu.SemaphoreType.DMA((nbuf,))])
    def scatter_kernel(table_hbm, idx_hbm, out_hbm, idx_v, rows_v, rsem, wsem):
        wid = lax.axis_index("s") * SC_CORES + lax.axis_index("c")
        pltpu.sync_copy(idx_hbm.at[wid], idx_v)

        def read(j, slot):
            off = pl.multiple_of(wid * per_w + j * chunk, chunk)
            return pltpu.make_async_copy(table_hbm.at[pl.ds(off, chunk)], rows_v.at[slot], rsem.at[slot])

        def write(j, slot, k):
            return pltpu.make_async_copy(rows_v.at[slot], out_hbm.at[idx_v.at[k, j]], wsem.at[slot])

        for slot in range(nbuf):
            read(slot, slot).start()

        @pl.loop(0, nch // nbuf)
        def _(g):
            for slot in range(nbuf):
                j = g * nbuf + slot
                read(j, slot).wait()
                write(j, slot, 0).start()
                write(j, slot, 1).start()
                write(j, slot, 0).wait()
                write(j, slot, 1).wait()

                @pl.when(j + nbuf < nch)
                def _():
                    read(j + nbuf, slot).start()

    idx = dest2.reshape(2, workers, nch, chunk).transpose(1, 0, 2, 3)
    return scatter_kernel(table, idx)


def _dispatch_plan(rt, cnt, n):
    ne, blk = MOE_N_EXPERTS, MOE_BLOCK
    n_blocks = (2 * n) // blk + ne
    experts = jnp.arange(ne, dtype=jnp.int32)
    counts = cnt[0, ROUTER_EXPERT_LANE:ROUTER_EXPERT_LANE + ne].astype(jnp.int32)
    padded = (counts + blk - 1) // blk * blk
    pend = jnp.cumsum(padded)
    pstart = pend - padded
    ids = rt[:, 0:2].astype(jnp.int32)
    pos = rt[:, 4:6].astype(jnp.int32)
    first_row = jnp.sum(jnp.where(ids[:, :, None] == experts[None, None, :], pstart[None, None, :], 0), axis=-1)
    dest2 = (first_row + pos).T
    b0 = jnp.arange(n_blocks, dtype=jnp.int32) * blk
    block_e = jnp.minimum(jnp.sum((pend[None, :] <= b0[:, None]).astype(jnp.int32), axis=1), ne - 1)
    n_valid = jnp.clip(counts[block_e] - (b0 - pstart[block_e]), 0, blk).astype(jnp.int32)
    return dest2, block_e, n_valid


def _expert_block_kernel(be_ref, nv_ref, xs_ref, g_ref, wg_ref, wu_ref, wd_ref, o_ref, wgb_ref, wub_ref, wdb_ref):
    b = pl.program_id(0)
    valid = nv_ref[b]

    @pl.when((b == 0) | (be_ref[b] != be_ref[jnp.maximum(b - 1, 0)]))
    def _():
        wgb_ref[...] = wg_ref[0].astype(BF16)
        wub_ref[...] = wu_ref[0].astype(BF16)
        wdb_ref[...] = wd_ref[0].astype(BF16)

    @pl.when(valid > 0)
    def _():
        row = lax.broadcasted_iota(jnp.int32, xs_ref.shape, 0)
        xs = jnp.where(row < valid, xs_ref[...], 0.0)
        h = _rms(xs, g_ref[...]).astype(BF16)
        gt = _dot(h, wgb_ref[...])
        hid = gt * _sigmoid(gt) * _dot(h, wub_ref[...])
        o_ref[...] = _dot(hid.astype(BF16), wdb_ref[...])

    @pl.when(valid == 0)
    def _():
        o_ref[...] = jnp.zeros_like(o_ref)


def _expert_blocks(xs, gain, wg, wu, wd, layer, block_e, n_used):
    p_rows, d = xs.shape
    hid = wg.shape[-1]
    blk = MOE_BLOCK
    grid_spec = pltpu.PrefetchScalarGridSpec(
        num_scalar_prefetch=2, grid=(p_rows // blk,),
        in_specs=[pl.BlockSpec((blk, d), lambda b, be, nu: (b, 0)),
                  pl.BlockSpec((1, d), lambda b, be, nu: (0, 0)),
                  pl.BlockSpec((None, 1, d, hid), lambda b, be, nu: (layer, be[b], 0, 0)),
                  pl.BlockSpec((None, 1, d, hid), lambda b, be, nu: (layer, be[b], 0, 0)),
                  pl.BlockSpec((None, 1, hid, d), lambda b, be, nu: (layer, be[b], 0, 0))],
        out_specs=pl.BlockSpec((blk, d), lambda b, be, nu: (b, 0)),
        scratch_shapes=[pltpu.VMEM((d, hid), BF16), pltpu.VMEM((d, hid), BF16), pltpu.VMEM((hid, d), BF16)])
    return pl.pallas_call(
        _expert_block_kernel, out_shape=jax.ShapeDtypeStruct((p_rows, d), F32), grid_spec=grid_spec,
        compiler_params=_params(("arbitrary",)),
        name="moe_expert_blocks",
    )(block_e, n_used, xs, gain, wg, wu, wd)


def _combine_kernel(x_ref, y1_ref, y2_ref, rt_ref, gfin_ref, o_ref, *, final_norm):
    out = x_ref[...] + rt_ref[:, 2:3] * y1_ref[...] + rt_ref[:, 3:4] * y2_ref[...]
    o_ref[...] = _rms(out, gfin_ref[...]) if final_norm else out


def _combine(x2, y_halves, rt, gfin, final_norm):
    n, d = x2.shape
    tm = COMBINE_TM
    nt = n // tm
    return pl.pallas_call(
        functools.partial(_combine_kernel, final_norm=final_norm),
        out_shape=jax.ShapeDtypeStruct((n, d), F32),
        grid=(nt,),
        in_specs=[pl.BlockSpec((tm, d), lambda i: (i, 0)),
                  pl.BlockSpec((tm, d), lambda i: (i, 0)),
                  pl.BlockSpec((tm, d), lambda i: (i + nt, 0)),
                  pl.BlockSpec((tm, LANES), lambda i: (i, 0)), _const_spec((1, d))],
        out_specs=pl.BlockSpec((tm, d), lambda i: (i, 0)),
        compiler_params=_params(("parallel",)), name="moe_combine",
    )(x2, y_halves, y_halves, rt, gfin)


def _moe(x2, rt, cnt, gffn, wg, wu, wd, layer, gfin, final_norm):
    n, d = x2.shape
    dest2, block_e, n_valid = _dispatch_plan(rt, cnt, n)
    xs = _sc_scatter_rows(x2, dest2, block_e.shape[0] * MOE_BLOCK)
    yb = _expert_blocks(xs, gffn, wg, wu, wd, layer, block_e, n_valid)
    y_halves = _sc_gather_rows(yb, dest2.reshape(2 * n))
    return _combine(x2, y_halves, rt, gfin, final_norm)


def kernel(x, mem, positions, norm_mix, w_in, diff_lambda, hgrn_lb_logits, spatial_w, spatial_b, w_branch, w_out,
           norm_mem_q, norm_mem_kv, w_mem_q, w_mem_kv, w_mem_o, norm_ffn, w_router_group, b_router_group,
           w_router_expert, b_router_expert, w_exp_gate, w_exp_up, w_exp_down, norm_final):
    batch, seq, d = x.shape
    depth = w_in.shape[0]
    n = batch * seq
    xf = x.reshape(n, d)
    tabs = _rope_tables(positions)
    row = lambda v: v.reshape(1, -1).astype(F32)
    for l in range(depth):
        lam_init = 0.8 - 0.6 * math.exp(-0.3 * l)
        w1, w_gate = _split_w_in(w_in, l)
        sw = spatial_w[l].reshape(SGU_GROUPS * SGU_CHUNK, SGU_CHUNK)
        sb = jnp.repeat(spatial_b[l].T, SGU_GROUP_DIM, axis=1)
        qat, ka, vat, hb, y_c, qdt, iqt, dkv, dkvt, ikw, iwt = _projection(
            xf, row(norm_mix[l]), w1, tabs, sw, sb, batch, seq)
        y_a = _diff_attention(diff_lambda[l], qat, ka, vat, lam_init, batch, seq)
        y_b = _hgrn(hgrn_lb_logits, hb, l, batch, seq)
        y_d = _dsa(qdt, iqt, iwt, dkv, dkvt, ikw, batch, seq)
        mkt, mv = _mem_kv(mem, row(norm_mem_kv[l]), w_mem_kv[l].astype(BF16))
        e0, e1 = ROUTER_EXPERT_LANE, ROUTER_EXPERT_LANE + MOE_N_EXPERTS
        wr = jnp.zeros((d, LANES), F32)
        wr = wr.at[:, :MOE_GROUPS].set(w_router_group[l]).at[:, e0:e1].set(w_router_expert[l]).astype(BF16)
        br = jnp.zeros((1, LANES), F32)
        br = br.at[0, :MOE_GROUPS].set(b_router_group[l]).at[0, e0:e1].set(b_router_expert[l])
        x2, rt, cnt = _merge(xf, (y_a, y_b, y_c, y_d), row(norm_mix[l]), w_gate, w_branch[l].astype(BF16),
                             w_out[l].astype(BF16), row(norm_mem_q[l]), w_mem_q[l].astype(BF16), mkt, mv,
                             w_mem_o[l].astype(BF16), row(norm_ffn[l]), wr, br, batch, seq)
        xf = _moe(x2, rt, cnt, row(norm_ffn[l]), w_exp_gate, w_exp_up, w_exp_down, l,
                  row(norm_final), final_norm=(l == depth - 1))
    return xf.reshape(batch, seq, d)
```

```python
import functools
import math

import numpy as np
import jax
import jax.numpy as jnp
from jax import lax
from jax.experimental import pallas as pl
from jax.experimental.pallas import tpu as pltpu
from jax.experimental.pallas import tpu_sc as plsc

F32 = jnp.float32
BF16 = jnp.bfloat16

NORM_EPS = 1e-6
ROPE_THETA = 10000.0
NEG_BIG = -1e30

N_BRANCH = 4
BRANCH_WIDTH = 256
DIFF_HEADS = 4
DIFF_HEAD_DIM = 32
HGRN_HEADS = 4
HGRN_DIM = 64
HGRN_CHUNK = 64
HGRN_MIN_FORGET = 1e-30
SGU_GROUPS = 4
SGU_GROUP_DIM = 64
SGU_CHUNK = 128
DSA_HEADS = 4
DSA_HEAD_DIM = 64
DSA_IDX_HEADS = 4
DSA_IDX_DIM = 32
DSA_TOPK = 256
MEM_HEADS = 4
MEM_HEAD_DIM = 64
MOE_GROUPS = 4
MOE_EXPERTS_PER_GROUP = 8
MOE_N_EXPERTS = 32
MOE_BLOCK = 512
ROUTER_EXPERT_LANE = 32
SC_CORES = 2
SC_SUBCORES = 16
SC_GATHER_CHUNK = 16
SC_GATHER_BUFS = 4

LANES = 128
VMEM_LIMIT = 56 * 1024 * 1024

PROJ_TM = 512
DIFF_TQ = 256
DIFF_TK = 512
HGRN_TC = 512
DSA_TQ = 256
DSA_TK = 512
MERGE_TM = 512
COMBINE_TM = 512
ROPE_TM = 1024

C_AQ, C_AK, C_AV = 0, 256, 512
C_HB = 768
C_UV = 1792
C_DQ = 2304
C_DKV = 2560
C_IQ = 2688
C_IKW = 2816
IW_LANE = 32
C_GATE = 2852
C_TOTAL = 2944
LOG2E = math.log2(math.e)


def _params(sem):
    return pltpu.CompilerParams(dimension_semantics=sem, vmem_limit_bytes=VMEM_LIMIT)


def _const_spec(shape):
    nd = len(shape)
    return pl.BlockSpec(shape, lambda *_: (0,) * nd, pipeline_mode=pl.Buffered(1))


def _rms(xf, gain=None):
    y = xf * lax.rsqrt(jnp.mean(xf * xf, axis=-1, keepdims=True) + NORM_EPS)
    return y if gain is None else y * gain


def _sigmoid(x):
    return 0.5 * jnp.tanh(0.5 * x) + 0.5


def _dot(a, b):
    return jnp.dot(a, b, preferred_element_type=F32)


def _dot_nt(a, b):
    return lax.dot_general(a, b, (((1,), (1,)), ((), ())), preferred_element_type=F32)


def _rope_table_kernel(pos_ref, frq_ref, sgn_ref, c32_ref, s32_ref, c64_ref, s64_ref):
    pos = pos_ref[...].astype(F32)
    twice = lambda t: jnp.concatenate([t, t], axis=1)
    a32 = pos * frq_ref[0:1, :LANES]
    a64 = pos * frq_ref[1:2, :LANES]
    c32_ref[...] = twice(jnp.cos(a32))
    s32_ref[...] = twice(jnp.sin(a32) * sgn_ref[0:1, :LANES])
    c64_ref[...] = twice(jnp.cos(a64))
    s64_ref[...] = twice(jnp.sin(a64) * sgn_ref[1:2, :LANES])


def _rope_tables(positions):
    n = positions.size
    pos = positions.reshape(n, 1).astype(jnp.int32)
    lane = np.arange(256)
    inv32 = ROPE_THETA ** (-jnp.arange(16, dtype=F32) * (2.0 / 32))
    inv64 = ROPE_THETA ** (-jnp.arange(32, dtype=F32) * (2.0 / 64))
    frq = jnp.stack([inv32[lane % 16], inv64[lane % 32]])
    sgn = jnp.asarray(np.stack([np.where(lane % 32 < 16, -1.0, 1.0),
                                np.where(lane % 64 < 32, -1.0, 1.0)]), F32)
    tm = ROPE_TM
    tab = jax.ShapeDtypeStruct((n, 256), F32)
    return pl.pallas_call(
        _rope_table_kernel,
        out_shape=(tab, tab, tab, tab),
        grid=(n // tm,),
        in_specs=[pl.BlockSpec((tm, 1), lambda i: (i, 0)), _const_spec((2, 256)), _const_spec((2, 256))],
        out_specs=tuple(pl.BlockSpec((tm, 256), lambda i: (i, 0)) for _ in range(4)),
        compiler_params=_params(("parallel",)),
        name="rope_tables",
    )(pos, frq, sgn)


def _w1_kernel(wt_ref, o_ref, *, layer):
    o_ref[...] = wt_ref[:, layer, :].T.astype(BF16)


def _wgate_kernel(a_ref, b_ref, o_ref, *, layer):
    off = C_GATE % LANES
    o_ref[...] = jnp.concatenate([a_ref[off:, layer, :], b_ref[:off, layer, :]], axis=0).T.astype(BF16)


def _split_w_in(w_in, layer):
    depth, d, width = w_in.shape
    wt = jnp.transpose(w_in, (2, 0, 1))
    rows = lambda f: pl.BlockSpec((LANES, depth, d), lambda i: (f(i), 0, 0))
    cols = pl.BlockSpec((d, LANES), lambda i: (0, i))
    w1 = pl.pallas_call(
        functools.partial(_w1_kernel, layer=layer), out_shape=jax.ShapeDtypeStruct((d, C_TOTAL), BF16),
        grid=(C_TOTAL // LANES,), in_specs=[rows(lambda i: i)], out_specs=cols,
        compiler_params=_params(("parallel",)), name="split_w_in",
    )(wt)
    a0 = C_GATE // LANES
    gate_w = width - C_GATE
    wg = pl.pallas_call(
        functools.partial(_wgate_kernel, layer=layer), out_shape=jax.ShapeDtypeStruct((d, gate_w), BF16),
        grid=(gate_w // LANES,), in_specs=[rows(lambda i: a0 + i), rows(lambda i: a0 + i + 1)], out_specs=cols,
        compiler_params=_params(("parallel",)), name="split_w_gate",
    )(wt, wt)
    return w1, wg


def _gelu_tanh(x):
    return 0.5 * x * (1.0 + jnp.tanh(math.sqrt(2.0 / math.pi) * (x + 0.044715 * (x * x * x))))


def _rope(x, cos, sin_signed, half):
    w = x.shape[-1]
    lane = lax.broadcasted_iota(jnp.int32, x.shape, 1)
    partner = jnp.where(lane % (2 * half) < half, pltpu.roll(x, w - half, 1), pltpu.roll(x, half, 1))
    return x * cos + partner * sin_signed


def _proj_kernel(x_ref, g_ref, w_ref, c32_ref, s32_ref, c64_ref, s64_ref, sw_ref, sb_ref,
                 qat_ref, ka_ref, vat_ref, hb_ref, yc_ref, qdt_ref, iqt_ref, dkv_ref, dkvt_ref, ikw_ref, iwt_ref,
                 *, tm):
    h = _rms(x_ref[...], g_ref[...]).astype(BF16)

    def proj(c0, width):
        return _dot(h, w_ref[:, c0:c0 + width])

    c32, s32, c64, s64 = c32_ref[...], s32_ref[...], c64_ref[...], s64_ref[...]
    qat_ref[0] = (_rope(proj(C_AQ, 256), c32, s32, 16) * (DIFF_HEAD_DIM ** -0.5 * LOG2E)).T.astype(BF16)
    ka_ref[...] = _rope(proj(C_AK, 256), c32, s32, 16).astype(BF16)
    vat_ref[0] = proj(C_AV, 256).astype(BF16).T
    hb_ref[...] = proj(C_HB, 1024)
    qdt_ref[0] = (_rope(proj(C_DQ, 256), c64, s64, 32) * (DSA_HEAD_DIM ** -0.5 * LOG2E)).T.astype(BF16)
    iqt_ref[0] = _rope(proj(C_IQ, 128), c32[:, :128], s32[:, :128], 16).T.astype(BF16)
    lane = lax.broadcasted_iota(jnp.int32, (tm, 128), 1)
    is_k = lane < DSA_HEAD_DIM
    dkv = _rope(proj(C_DKV, 128), jnp.where(is_k, c64[:, :128], 1.0), jnp.where(is_k, s64[:, :128], 0.0), 32)
    is_ik = lane < DSA_IDX_DIM
    ikw = _rope(proj(C_IKW, 128), jnp.where(is_ik, c32[:, :128], 1.0), jnp.where(is_ik, s32[:, :128], 0.0), 16)
    dkv_ref[...] = dkv.astype(BF16)
    dkvt_ref[0] = dkv.T.astype(BF16)
    ikw_ref[...] = ikw.astype(BF16)
    iw_scale = DSA_IDX_HEADS ** -0.5 * DSA_IDX_DIM ** -0.5
    iwt_ref[0] = (ikw * iw_scale).T[IW_LANE:IW_LANE + 8, :]
    uv = _gelu_tanh(proj(C_UV, 512))
    u, v = uv[:, :256], uv[:, 256:]
    mu = jnp.mean(v, axis=-1, keepdims=True)
    vc = v - mu
    vn = (vc * lax.rsqrt(jnp.mean(vc * vc, axis=-1, keepdims=True) + NORM_EPS)).astype(BF16)
    r = lax.broadcasted_iota(jnp.int32, (SGU_GROUPS * SGU_CHUNK, SGU_CHUNK), 0)
    c = lax.broadcasted_iota(jnp.int32, (SGU_GROUPS * SGU_CHUNK, SGU_CHUNK), 1)
    wt = jnp.where((r % SGU_CHUNK) >= c, sw_ref[...], 0.0).astype(BF16)
    lane_grp = lax.broadcasted_iota(jnp.int32, (SGU_CHUNK, 256), 1) // SGU_GROUP_DIM
    for ch in range(tm // SGU_CHUNK):
        r0 = ch * SGU_CHUNK
        full = _dot(wt, vn[r0:r0 + SGU_CHUNK, :])
        mixed = sb_ref[...]
        for g in range(SGU_GROUPS):
            mixed = mixed + jnp.where(lane_grp == g, full[g * SGU_CHUNK:(g + 1) * SGU_CHUNK, :], 0.0)
        yc_ref[r0:r0 + SGU_CHUNK, :] = (u[r0:r0 + SGU_CHUNK, :] * mixed).astype(BF16)


def _projection(x, gain, w1, tabs, sw, sb, batch, seq):
    n, d = x.shape
    tm = PROJ_TM
    spt = seq // tm
    tok = lambda w: pl.BlockSpec((tm, w), lambda i: (i, 0))
    tr = lambda rows: pl.BlockSpec((1, rows, tm), lambda i: (i // spt, 0, i % spt))
    out_shape = (
        jax.ShapeDtypeStruct((batch, 256, seq), BF16),
        jax.ShapeDtypeStruct((n, 256), BF16),
        jax.ShapeDtypeStruct((batch, 256, seq), BF16),
        jax.ShapeDtypeStruct((n, 1024), F32),
        jax.ShapeDtypeStruct((n, 256), BF16),
        jax.ShapeDtypeStruct((batch, 256, seq), BF16),
        jax.ShapeDtypeStruct((batch, 128, seq), BF16),
        jax.ShapeDtypeStruct((n, 128), BF16),
        jax.ShapeDtypeStruct((batch, 128, seq), BF16),
        jax.ShapeDtypeStruct((n, 128), BF16),
        jax.ShapeDtypeStruct((batch, 8, seq), F32),
    )
    return pl.pallas_call(
        functools.partial(_proj_kernel, tm=tm),
        out_shape=out_shape,
        grid=(n // tm,),
        in_specs=[tok(d), _const_spec((1, d)), _const_spec((d, C_TOTAL)),
                  tok(256), tok(256), tok(256), tok(256),
                  _const_spec((SGU_GROUPS * SGU_CHUNK, SGU_CHUNK)), _const_spec((SGU_CHUNK, 256))],
        out_specs=(tr(256), tok(256), tr(256), tok(1024), tok(256), tr(256), tr(128), tok(128), tr(128), tok(128), tr(8)),
        compiler_params=_params(("parallel",)),
        name="projection",
    )(x, gain, w1, *tabs, sw, sb)


def _diff_attn_kernel(lam_ref, qt_ref, k_ref, vt_ref, o_ref, sa_ref, sb_ref, *, lam_init, tq, tk):
    q0 = pl.program_id(1) * tq
    kb_diag = q0 // tk
    lv = lam_ref[...]
    lam = (jnp.exp(jnp.sum(lv[0:1] * lv[1:2], axis=-1, keepdims=True))
           - jnp.exp(jnp.sum(lv[2:3] * lv[3:4], axis=-1, keepdims=True)) + lam_init)
    qt = qt_ref[0]
    feat = lax.broadcasted_iota(jnp.int32, (256, tq), 0) // DIFF_HEAD_DIM
    n_maps = 2 * DIFF_HEADS
    qz = jnp.concatenate([jnp.where(feat == i, qt, jnp.zeros_like(qt)) for i in range(n_maps)], axis=1)
    wide = n_maps * tq
    key_i = lax.broadcasted_iota(jnp.int32, (tk, wide), 0)
    qry_i = q0 + lax.broadcasted_iota(jnp.int32, (tk, wide), 1) % tq

    def scores(kb, dst_ref):
        k0 = pl.multiple_of(kb * tk, tk)
        dst_ref[...] = _dot(k_ref[pl.ds(k0, tk), :], qz)

    def absorb(src_ref, kb, carry, masked):
        m_i, l_i, acc = carry
        k0 = pl.multiple_of(kb * tk, tk)
        s = src_ref[...]
        if masked:
            s = jnp.where(k0 + key_i <= qry_i, s, NEG_BIG)
        m_new = jnp.maximum(m_i, jnp.max(s, axis=0, keepdims=True))
        p = jnp.exp2(s - m_new)
        alpha = jnp.exp2(m_i - m_new)
        l_new = alpha * l_i + jnp.sum(p, axis=0, keepdims=True)
        pb = p.astype(BF16)
        pv = jnp.concatenate(
            [_dot(vt_ref[0, hd * 64:(hd + 1) * 64, pl.ds(k0, tk)], pb[:, 2 * hd * tq:(2 * hd + 2) * tq])
             for hd in range(DIFF_HEADS)], axis=1)
        return m_new, l_new, alpha * acc + pv

    def pair(j, carry):
        kb = 2 * j
        scores(kb + 1, sb_ref)
        carry = absorb(sa_ref, kb, carry, False)
        scores(kb + 2, sa_ref)
        return absorb(sb_ref, kb + 1, carry, False)

    init = (jnp.full((1, wide), NEG_BIG, F32), jnp.zeros((1, wide), F32), jnp.zeros((64, wide), F32))
    scores(0, sa_ref)
    carry = lax.fori_loop(0, kb_diag // 2, pair, init)

    def tail_odd(carry):
        scores(kb_diag, sb_ref)
        carry = absorb(sa_ref, kb_diag - 1, carry, False)
        return absorb(sb_ref, kb_diag, carry, True)

    def tail_even(carry):
        return absorb(sa_ref, kb_diag, carry, True)

    _, l_f, acc = lax.cond(kb_diag % 2 == 1, tail_odd, tail_even, carry)
    o_all = acc / l_f
    heads = []
    for hd in range(DIFF_HEADS):
        o0 = o_all[:, 2 * hd * tq:(2 * hd + 1) * tq]
        o1 = o_all[:, (2 * hd + 1) * tq:(2 * hd + 2) * tq]
        o_h = o0 - lam * o1
        ms = jnp.mean(o_h * o_h, axis=0, keepdims=True)
        heads.append(o_h * lax.rsqrt(ms + NORM_EPS) * (1.0 - lam_init))
    o_ref[...] = jnp.concatenate(heads, axis=0).T.astype(BF16)


def _diff_attention(lam_vec, qat, ka, vat, lam_init, batch, seq):
    tq, tk = DIFF_TQ, DIFF_TK
    nq = seq // tq
    return pl.pallas_call(
        functools.partial(_diff_attn_kernel, lam_init=lam_init, tq=tq, tk=tk),
        out_shape=jax.ShapeDtypeStruct((batch * seq, 256), BF16),
        grid=(batch, nq),
        in_specs=[_const_spec((4, DIFF_HEAD_DIM)),
                  pl.BlockSpec((1, 256, tq), lambda b, i: (b, 0, i)),
                  pl.BlockSpec((seq, 256), lambda b, i: (b, 0)),
                  pl.BlockSpec((1, 256, seq), lambda b, i: (b, 0, 0))],
        out_specs=pl.BlockSpec((tq, 256), lambda b, i: (b * nq + i, 0)),
        scratch_shapes=[pltpu.VMEM((tk, 2 * DIFF_HEADS * tq), F32), pltpu.VMEM((tk, 2 * DIFF_HEADS * tq), F32)],
        compiler_params=_params(("parallel", "parallel")),
        name="diff_attention",
    )(lam_vec, qat, ka, vat)


def _hgrn_kernel(lbl_ref, hb_ref, o_ref, st_ref, pstk_ref, *, layer, tc):
    cz = HGRN_CHUNK
    w = 256

    @pl.when(pl.program_id(1) == 0)
    def _():
        st_ref[...] = jnp.zeros_like(st_ref)

    lg = lbl_ref[...]
    e = jnp.exp(lg - jnp.max(lg, axis=0, keepdims=True))
    lw = e / jnp.sum(e, axis=0, keepdims=True)
    lb = jnp.sum(lw[0:layer + 1], axis=0, keepdims=True) - lw[0:1]

    ri = lax.broadcasted_iota(jnp.int32, (cz, cz), 0)
    ci = lax.broadcasted_iota(jnp.int32, (cz, cz), 1)
    tri = (ri >= ci).astype(F32)
    rb = lax.broadcasted_iota(jnp.int32, (w, w), 0) // HGRN_DIM
    cb = lax.broadcasted_iota(jnp.int32, (w, w), 1) // HGRN_DIM
    same_head = rb == cb
    head_ones = same_head.astype(BF16)
    trows = {r: r + lax.broadcasted_iota(jnp.int32, (cz - r, w), 0) for r in range(0, cz, 16)}

    def chunk(c, carry):
        r0 = pl.multiple_of(c * cz, cz)
        q = hb_ref[pl.ds(r0, cz), 0:256]
        fp = hb_ref[pl.ds(r0, cz), 256:512]
        v = hb_ref[pl.ds(r0, cz), 512:768]
        g = hb_ref[pl.ds(r0, cz), 768:1024]
        qf = q * _sigmoid(q)
        f = lb + (1.0 - lb) * jax.nn.sigmoid(fp)
        log_f = jnp.log(jnp.maximum(f, HGRN_MIN_FORGET))
        kf = (1.0 - lb) * jax.nn.sigmoid(-fp)
        bc = jnp.dot(tri, log_f, preferred_element_type=F32, precision=lax.Precision.HIGHEST)
        st = st_ref[...]
        o = _dot_nt((qf * jnp.exp(bc)).astype(BF16), st.astype(BF16))
        for s in range(cz):
            r_lo = (s // 16) * 16
            arg = bc[r_lo:, :] - bc[s:s + 1, :]
            if s > r_lo:
                arg = jnp.where(trows[r_lo] >= s, arg, NEG_BIG)
            p = qf[r_lo:, :] * kf[s:s + 1, :] * jnp.exp(arg)
            if r_lo:
                pstk_ref[s * cz:s * cz + r_lo, :] = jnp.zeros((r_lo, w), BF16)
            pstk_ref[s * cz + r_lo:(s + 1) * cz, :] = p.astype(BF16)
        accs = [jnp.zeros((16, w), F32) for _ in range(cz // 16)]
        for sg in range(cz // 16):
            att = _dot(pstk_ref[sg * 16 * cz:(sg + 1) * 16 * cz, :], head_ones)
            for sl in range(16):
                s = sg * 16 + sl
                for j in range(sg, cz // 16):
                    accs[j] = accs[j] + att[sl * cz + 16 * j:sl * cz + 16 * j + 16, :] * v[s:s + 1, :]
        o = o + jnp.concatenate(accs, axis=0)
        b_end = bc[cz - 1:cz, :]
        kd = kf * jnp.exp(b_end - bc)
        upd = _dot(v.T.astype(BF16), kd.astype(BF16))
        st_ref[...] = st * jnp.exp(b_end) + jnp.where(same_head, upd, 0.0)
        ms = _dot(o * o, head_ones.astype(F32)) * (1.0 / HGRN_DIM)
        y = o * lax.rsqrt(ms + NORM_EPS)
        o_ref[pl.ds(r0, cz), :] = (y * (g * _sigmoid(g))).astype(BF16)
        return carry

    lax.fori_loop(0, tc // cz, chunk, 0)


def _hgrn(lb_logits, hb, layer, batch, seq):
    tc = HGRN_TC
    nt = seq // tc
    cz = HGRN_CHUNK
    return pl.pallas_call(
        functools.partial(_hgrn_kernel, layer=layer, tc=tc),
        out_shape=jax.ShapeDtypeStruct((batch * seq, 256), BF16),
        grid=(batch, nt),
        in_specs=[_const_spec(lb_logits.shape),
                  pl.BlockSpec((tc, 1024), lambda b, i: (b * nt + i, 0))],
        out_specs=pl.BlockSpec((tc, 256), lambda b, i: (b * nt + i, 0)),
        scratch_shapes=[pltpu.VMEM((256, 256), F32), pltpu.VMEM((cz * cz, 256), BF16)],
        compiler_params=_params(("parallel", "arbitrary")),
        name="hgrn2",
    )(lb_logits, hb)


def _dsa_kernel(qdt_ref, iqt_ref, iwt_ref, dkv_ref, dkvt_ref, ikw_ref, o_ref, key_ref, bias_ref, half_ref,
                sa_ref, sb_ref, *, tq, tk, n_sel):
    q0 = pl.program_id(1) * tq
    nkb = q0 // tk + 1
    key_i = lax.broadcasted_iota(jnp.int32, (tk, tq), 0)
    qry_i = q0 + lax.broadcasted_iota(jnp.int32, (tk, tq), 1)
    grp = tk // 8
    rows8 = lambda x: x.reshape(grp, 8, tq)
    iqt = iqt_ref[0]
    zpad = jnp.zeros((LANES - DSA_IDX_DIM, tq), BF16)
    iqz = jnp.concatenate([jnp.concatenate([iqt[hd * DSA_IDX_DIM:(hd + 1) * DSA_IDX_DIM, :], zpad], axis=0)
                           for hd in range(DSA_IDX_HEADS)], axis=1)
    iw = iwt_ref[0]

    def score_block(kb, carry):
        k0 = pl.multiple_of(kb * tk, tk)
        sh = jnp.maximum(_dot(ikw_ref[pl.ds(k0, tk), :], iqz), 0.0)
        sc = jnp.zeros((tk, tq), F32)
        for hd in range(DSA_IDX_HEADS):
            sc = sc + sh[:, hd * tq:(hd + 1) * tq] * iw[hd:hd + 1, :]
        sc = jnp.where(k0 + key_i <= qry_i, sc + 0.0, -jnp.inf)
        bits = pltpu.bitcast(sc, jnp.int32)
        key = jnp.where(bits < 0, bits ^ jnp.int32(0x7FFFFFFF), bits)
        key_ref[pl.ds(k0, tk), :] = key
        half_ref[pl.ds(k0, tk), :] = (key >> 16).astype(jnp.int16)
        return carry

    lax.fori_loop(0, nkb, score_block, 0)

    one16, zero16 = jnp.ones((), jnp.int16), jnp.zeros((), jnp.int16)
    low16 = np.int16(-2 ** 15)

    def count16(limit, strict):
        def body(kb, acc):
            k0 = pl.multiple_of(kb * tk, tk)
            for c in range(tk // 128):
                blk = half_ref[pl.ds(k0 + 128 * c, 128), :].reshape(8, 16, tq)
                hit = jnp.where(blk > limit if strict else blk >= limit, one16, zero16)
                parts = [hit[j] for j in range(8)]
                while len(parts) > 1:
                    parts = [a + b for a, b in zip(parts[0::2], parts[1::2])]
                acc = acc + parts[0]
            return acc
        acc = lax.fori_loop(0, nkb, body, jnp.zeros((16, tq), jnp.int16))
        return jnp.broadcast_to(jnp.sum(acc.astype(jnp.int32), axis=0, keepdims=True), (16, tq))

    def search16(need):
        t = jnp.full((16, tq), -2 ** 15, jnp.int32)
        for bit in range(15, -1, -1):
            trial = t + 2 ** bit
            t = jnp.where(count16(trial.astype(jnp.int16), False) >= need, trial, t)
        return t

    t_hi = search16(n_sel)
    t_hi16 = t_hi.astype(jnp.int16)
    need_lo = n_sel - count16(t_hi16, True)

    def low_block(kb, carry):
        k0 = pl.multiple_of(kb * tk, tk)
        lo = ((key_ref[pl.ds(k0, tk), :] & 0xFFFF) - 2 ** 15).astype(jnp.int16).reshape(tk // 16, 16, tq)
        hi = half_ref[pl.ds(k0, tk), :].reshape(tk // 16, 16, tq)
        half_ref[pl.ds(k0, tk), :] = jnp.where(hi == t_hi16, lo, low16).reshape(tk, tq)
        return carry

    lax.fori_loop(0, nkb, low_block, 0)
    t_lo = search16(need_lo)
    thr = ((t_hi << 16) | (t_lo + 2 ** 15))[0:8, :]

    need = (need_lo - count16(t_lo.astype(jnp.int16), True))[0:8, :].astype(F32)
    ur = lax.broadcasted_iota(jnp.int32, (tk, tk), 0)
    uc = lax.broadcasted_iota(jnp.int32, (tk, tk), 1)
    earlier = (uc < ur).astype(BF16)
    ones8 = jnp.ones((8, tk), BF16)

    def select_block(kb, seen):
        k0 = pl.multiple_of(kb * tk, tk)
        blk = rows8(key_ref[pl.ds(k0, tk), :])
        eq = blk == thr[None]
        eqb = jnp.where(eq, 1.0, 0.0).reshape(tk, tq).astype(BF16)
        rank = rows8(_dot(earlier, eqb)) + seen[None]
        sel = (blk > thr[None]) | (eq & (rank < need[None]))
        bias = jnp.where(sel, 0.0, NEG_BIG).reshape(tk, tq)
        bias_ref[pl.ds(k0, tk), :] = jnp.where(k0 + key_i <= qry_i, bias, NEG_BIG)
        return seen + _dot(ones8, eqb)

    lax.fori_loop(0, nkb, select_block, jnp.zeros((8, tq), F32))

    qdt = qdt_ref[0]
    zq = jnp.zeros((LANES - DSA_HEAD_DIM, tq), BF16)
    qz = jnp.concatenate([jnp.concatenate([qdt[hd * DSA_HEAD_DIM:(hd + 1) * DSA_HEAD_DIM, :], zq], axis=0)
                          for hd in range(DSA_HEADS)], axis=1)
    wide = DSA_HEADS * tq

    def scores(kb, dst_ref):
        k0 = pl.multiple_of(kb * tk, tk)
        dst_ref[...] = _dot(dkv_ref[pl.ds(k0, tk), :], qz)

    def absorb(src_ref, kb, carry):
        m_i, l_i, acc = carry
        k0 = pl.multiple_of(kb * tk, tk)
        bias = bias_ref[pl.ds(k0, tk), :]
        s = src_ref[...] + jnp.concatenate([bias] * DSA_HEADS, axis=1)
        m_new = jnp.maximum(m_i, jnp.max(s, axis=0, keepdims=True))
        p = jnp.exp2(s - m_new)
        alpha = jnp.exp2(m_i - m_new)
        l_new = alpha * l_i + jnp.sum(p, axis=0, keepdims=True)
        pv = _dot(dkvt_ref[0, DSA_HEAD_DIM:, pl.ds(k0, tk)], p.astype(BF16))
        return m_new, l_new, alpha * acc + pv

    def pair(j, carry):
        kb = 2 * j
        scores(kb + 1, sb_ref)
        carry = absorb(sa_ref, kb, carry)
        scores(kb + 2, sa_ref)
        return absorb(sb_ref, kb + 1, carry)

    init = (jnp.full((1, wide), NEG_BIG, F32), jnp.zeros((1, wide), F32), jnp.zeros((DSA_HEAD_DIM, wide), F32))
    last = nkb - 1
    scores(0, sa_ref)
    carry = lax.fori_loop(0, last // 2, pair, init)

    def tail_odd(carry):
        scores(last, sb_ref)
        return absorb(sb_ref, last, absorb(sa_ref, last - 1, carry))

    def tail_even(carry):
        return absorb(sa_ref, last, carry)

    _, l_f, acc = lax.cond(last % 2 == 1, tail_odd, tail_even, carry)
    o_all = acc / l_f
    o_ref[...] = jnp.concatenate([o_all[:, hd * tq:(hd + 1) * tq] for hd in range(DSA_HEADS)],
                                 axis=0).T.astype(BF16)


def _dsa(qdt, iqt, iwt, dkv, dkvt, ikw, batch, seq):
    tq, tk = DSA_TQ, DSA_TK
    nq = seq // tq
    n_sel = min(DSA_TOPK, seq // 4)
    return pl.pallas_call(
        functools.partial(_dsa_kernel, tq=tq, tk=tk, n_sel=n_sel),
        out_shape=jax.ShapeDtypeStruct((batch * seq, 256), BF16),
        grid=(batch, nq),
        in_specs=[pl.BlockSpec((1, 256, tq), lambda b, i: (b, 0, i)),
                  pl.BlockSpec((1, 128, tq), lambda b, i: (b, 0, i)),
                  pl.BlockSpec((1, 8, tq), lambda b, i: (b, 0, i)),
                  pl.BlockSpec((seq, 128), lambda b, i: (b, 0)),
                  pl.BlockSpec((1, 128, seq), lambda b, i: (b, 0, 0)),
                  pl.BlockSpec((seq, 128), lambda b, i: (b, 0))],
        out_specs=pl.BlockSpec((tq, 256), lambda b, i: (b * nq + i, 0)),
        scratch_shapes=[pltpu.VMEM((seq, tq), jnp.int32), pltpu.VMEM((seq, tq), F32),
                        pltpu.VMEM((seq, tq), jnp.int16),
                        pltpu.VMEM((tk, DSA_HEADS * tq), F32), pltpu.VMEM((tk, DSA_HEADS * tq), F32)],
        compiler_params=_params(("parallel", "parallel")),
        name="dsa",
    )(qdt, iqt, iwt, dkv, dkvt, ikw)


def _mem_kv_kernel(mem_ref, g_ref, w_ref, kt_ref, v_ref):
    mn = _rms(mem_ref[0], g_ref[...]).astype(BF16)
    kv = _dot(mn, w_ref[...])
    kt_ref[0] = kv[:, :256].T.astype(BF16)
    v_ref[0] = kv[:, 256:].astype(BF16)


def _mem_kv(mem, gain, w_kv):
    b, m, d = mem.shape
    return pl.pallas_call(
        _mem_kv_kernel,
        out_shape=(jax.ShapeDtypeStruct((b, 256, m), BF16), jax.ShapeDtypeStruct((b, m, 256), BF16)),
        grid=(b,),
        in_specs=[pl.BlockSpec((1, m, d), lambda i: (i, 0, 0)), _const_spec((1, d)), _const_spec((d, 512))],
        out_specs=(pl.BlockSpec((1, 256, m), lambda i: (i, 0, 0)), pl.BlockSpec((1, m, 256), lambda i: (i, 0, 0))),
        compiler_params=_params(("parallel",)),
        name="mem_kv",
    )(mem, gain, w_kv)


def _merge_kernel(x_ref, ya_ref, yb_ref, yc_ref, yd_ref, gmix_ref, wg_ref, wbr_ref, wout_ref,
                  gq_ref, wq_ref, mkt_ref, mv_ref, wo_ref, gffn_ref, wr_ref, br_ref,
                  x2_ref, rt_ref, cnt_ref, run_ref, *, tm):
    x = x_ref[...]
    d = x.shape[-1]
    h = _rms(x, gmix_ref[...]).astype(BF16)
    merged = jnp.zeros((tm, d), F32)
    for n, y_ref in enumerate((ya_ref, yb_ref, yc_ref, yd_ref)):
        gate = _sigmoid(_dot(h, wg_ref[:, n * d:(n + 1) * d]))
        merged = merged + gate * _dot(y_ref[...], wbr_ref[n])
    x1 = x + _dot(merged.astype(BF16), wout_ref[...])
    h2 = _rms(x1, gq_ref[...]).astype(BF16)
    q = (_dot(h2, wq_ref[...]) * (MEM_HEAD_DIM ** -0.5)).astype(BF16)
    lane_head = lax.broadcasted_iota(jnp.int32, (tm, 256), 1) // MEM_HEAD_DIM
    mv = mv_ref[0]
    o = jnp.zeros((tm, 256), F32)
    for hd in range(MEM_HEADS):
        s = _dot(q[:, hd * MEM_HEAD_DIM:(hd + 1) * MEM_HEAD_DIM], mkt_ref[0, hd * MEM_HEAD_DIM:(hd + 1) * MEM_HEAD_DIM, :])
        p = jnp.exp(s - jnp.max(s, axis=-1, keepdims=True))
        p = p / jnp.sum(p, axis=-1, keepdims=True)
        o = o + jnp.where(lane_head == hd, _dot(p.astype(BF16), mv), 0.0)
    x2 = x1 + _dot(o.astype(BF16), wo_ref[...])
    x2_ref[...] = x2
    h3 = _rms(x2, gffn_ref[...]).astype(BF16)
    logits = _dot(h3, wr_ref[...]) + br_ref[...]
    lane = lax.broadcasted_iota(jnp.int32, (tm, LANES), 1)
    gl = jnp.where(lane < MOE_GROUPS, logits, -jnp.inf)
    gmax = jnp.max(gl, axis=-1, keepdims=True)
    gsel = jnp.min(jnp.where(gl == gmax, lane, LANES), axis=-1, keepdims=True)
    pg_sel = 1.0 / jnp.sum(jnp.exp(gl - gmax), axis=-1, keepdims=True)
    in_group = (lane - ROUTER_EXPERT_LANE) // MOE_EXPERTS_PER_GROUP == gsel
    el = jnp.where(in_group, logits, -jnp.inf)
    m1 = jnp.max(el, axis=-1, keepdims=True)
    i1 = jnp.min(jnp.where(el == m1, lane, LANES), axis=-1, keepdims=True)
    el2 = jnp.where(lane == i1, -jnp.inf, el)
    m2 = jnp.max(el2, axis=-1, keepdims=True)
    i2 = jnp.min(jnp.where(el2 == m2, lane, LANES), axis=-1, keepdims=True)
    e21 = jnp.exp(m2 - m1)
    c1 = pg_sel / (1.0 + e21)
    @pl.when(pl.program_id(0) == 0)
    def _():
        run_ref[...] = jnp.zeros_like(run_ref)

    oh1 = jnp.where(lane == i1, 1.0, 0.0)
    oh2 = jnp.where(lane == i2, 1.0, 0.0)
    both = oh1 + oh2
    tr = lax.broadcasted_iota(jnp.int32, (tm, tm), 0)
    tc = lax.broadcasted_iota(jnp.int32, (tm, tm), 1)
    before = _dot(jnp.where(tc < tr, 1.0, 0.0).astype(BF16), both.astype(BF16)) + run_ref[0:1, :]
    r1 = jnp.sum(oh1 * before, axis=-1, keepdims=True)
    r2 = jnp.sum(oh2 * before, axis=-1, keepdims=True)
    total = run_ref[...] + jnp.sum(both, axis=0, keepdims=True)
    run_ref[...] = total
    cnt_ref[...] = total
    ids = (jnp.where(lane == 0, i1, i2) - ROUTER_EXPERT_LANE).astype(F32)
    rt_ref[...] = jnp.where(lane < 2, ids, jnp.where(lane == 2, c1, jnp.where(lane == 3, c1 * e21,
                            jnp.where(lane == 4, r1, jnp.where(lane == 5, r2, 0.0)))))


def _merge(x, ys, gmix, wg, wbr, wout, gq, wq, mkt, mv, wo, gffn, wr, br, batch, seq):
    n, d = x.shape
    tm = MERGE_TM
    spt = seq // tm
    m = mv.shape[1]
    tok = lambda w: pl.BlockSpec((tm, w), lambda i: (i, 0))
    return pl.pallas_call(
        functools.partial(_merge_kernel, tm=tm),
        out_shape=(jax.ShapeDtypeStruct((n, d), F32), jax.ShapeDtypeStruct((n, LANES), F32),
                   jax.ShapeDtypeStruct((8, LANES), F32)),
        grid=(n // tm,),
        in_specs=[tok(d), tok(256), tok(256), tok(256), tok(256),
                  _const_spec((1, d)), _const_spec((d, N_BRANCH * d)), _const_spec((N_BRANCH, 256, d)),
                  _const_spec((d, d)), _const_spec((1, d)), _const_spec((d, 256)),
                  pl.BlockSpec((1, 256, m), lambda i: (i // spt, 0, 0)),
                  pl.BlockSpec((1, m, 256), lambda i: (i // spt, 0, 0)),
                  _const_spec((256, d)), _const_spec((1, d)), _const_spec((d, LANES)), _const_spec((1, LANES))],
        out_specs=(tok(d), tok(LANES), pl.BlockSpec((8, LANES), lambda i: (0, 0))),
        scratch_shapes=[pltpu.VMEM((8, LANES), F32)],
        compiler_params=_params(("arbitrary",)),
        name="merge_mem_router",
    )(x, *ys, gmix, wg, wbr, wout, gq, wq, mkt, mv, wo, gffn, wr, br)


def _sc_gather_rows(table, idx):
    _, width = table.shape
    total = idx.shape[0]
    chunk, nbuf = SC_GATHER_CHUNK, SC_GATHER_BUFS
    workers = SC_CORES * SC_SUBCORES
    per_w = total // workers
    nch = per_w // chunk
    assert total % (workers * chunk * nbuf) == 0
    mesh = plsc.VectorSubcoreMesh(core_axis_name="c", subcore_axis_name="s")

    @functools.partial(
        pl.kernel, mesh=mesh, out_type=jax.ShapeDtypeStruct((total, width), table.dtype),
        scratch_types=[pltpu.VMEM((nch, chunk), jnp.int32), pltpu.VMEM((nbuf, chunk, width), table.dtype),
                       pltpu.SemaphoreType.DMA((nbuf,)), pltpu.SemaphoreType.DMA((nbuf,))])
    def gather_kernel(table_hbm, idx_hbm, out_hbm, idx_v, rows_v, gsem, wsem):
        wid = lax.axis_index("s") * SC_CORES + lax.axis_index("c")
        pltpu.sync_copy(idx_hbm.at[wid], idx_v)

        def gather(j, slot):
            return pltpu.make_async_copy(table_hbm.at[idx_v.at[j]], rows_v.at[slot], gsem.at[slot])

        def write(j, slot):
            off = pl.multiple_of(wid * per_w + j * chunk, chunk)
            return pltpu.make_async_copy(rows_v.at[slot], out_hbm.at[pl.ds(off, chunk)], wsem.at[slot])

        for slot in range(nbuf):
            gather(slot, slot).start()

        @pl.loop(0, nch // nbuf)
        def _(g):
            for slot in range(nbuf):
                j = g * nbuf + slot
                gather(j, slot).wait()
                write(j, slot).start()
                write(j, slot).wait()

                @pl.when(j + nbuf < nch)
                def _():
                    gather(j + nbuf, slot).start()

    return gather_kernel(table, idx.reshape(workers, nch, chunk))


def _sc_scatter_rows(table, dest2, total):
    n, width = table.shape
    chunk, nbuf = SC_GATHER_CHUNK, SC_GATHER_BUFS
    workers = SC_CORES * SC_SUBCORES
    per_w = n // workers
    nch = per_w // chunk
    assert n % (workers * chunk * nbuf) == 0
    mesh = plsc.VectorSubcoreMesh(core_axis_name="c", subcore_axis_name="s")

    @functools.partial(
        pl.kernel, mesh=mesh, out_type=jax.ShapeDtypeStruct((total, width), table.dtype),
        scratch_types=[pltpu.VMEM((2, nch, chunk), jnp.int32), pltpu.VMEM((nbuf, chunk, width), table.dtype),
                       pltpu.SemaphoreType.DMA((nbuf,)), pltpu.SemaphoreType.DMA((nbuf,))])
    def scatter_kernel(table_hbm, idx_hbm, out_hbm, idx_v, rows_v, rsem, wsem):
        wid = lax.axis_index("s") * SC_CORES + lax.axis_index("c")
        pltpu.sync_copy(idx_hbm.at[wid], idx_v)

        def read(j, slot):
            off = pl.multiple_of(wid * per_w + j * chunk, chunk)
            return pltpu.make_async_copy(table_hbm.at[pl.ds(off, chunk)], rows_v.at[slot], rsem.at[slot])

        def write(j, slot, k):
            return pltpu.make_async_copy(rows_v.at[slot], out_hbm.at[idx_v.at[k, j]], wsem.at[slot])

        for slot in range(nbuf):
            read(slot, slot).start()

        @pl.loop(0, nch // nbuf)
        def _(g):
            for slot in range(nbuf):
                j = g * nbuf + slot
                read(j, slot).wait()
                write(j, slot, 0).start()
                write(j, slot, 1).start()
                write(j, slot, 0).wait()
                write(j, slot, 1).wait()

                @pl.when(j + nbuf < nch)
                def _():
                    read(j + nbuf, slot).start()

    idx = dest2.reshape(2, workers, nch, chunk).transpose(1, 0, 2, 3)
    return scatter_kernel(table, idx)


def _dispatch_plan(rt, cnt, n):
    ne, blk = MOE_N_EXPERTS, MOE_BLOCK
    n_blocks = (2 * n) // blk + ne
    experts = jnp.arange(ne, dtype=jnp.int32)
    counts = cnt[0, ROUTER_EXPERT_LANE:ROUTER_EXPERT_LANE + ne].astype(jnp.int32)
    padded = (counts + blk - 1) // blk * blk
    pend = jnp.cumsum(padded)
    pstart = pend - padded
    ids = rt[:, 0:2].astype(jnp.int32)
    pos = rt[:, 4:6].astype(jnp.int32)
    first_row = jnp.sum(jnp.where(ids[:, :, None] == experts[None, None, :], pstart[None, None, :], 0), axis=-1)
    dest2 = (first_row + pos).T
    b0 = jnp.arange(n_blocks, dtype=jnp.int32) * blk
    block_e = jnp.minimum(jnp.sum((pend[None, :] <= b0[:, None]).astype(jnp.int32), axis=1), ne - 1)
    n_valid = jnp.clip(counts[block_e] - (b0 - pstart[block_e]), 0, blk).astype(jnp.int32)
    return dest2, block_e, n_valid


def _expert_block_kernel(be_ref, nv_ref, xs_ref, g_ref, wg_ref, wu_ref, wd_ref, o_ref, wgb_ref, wub_ref, wdb_ref):
    b = pl.program_id(0)
    valid = nv_ref[b]

    @pl.when((b == 0) | (be_ref[b] != be_ref[jnp.maximum(b - 1, 0)]))
    def _():
        wgb_ref[...] = wg_ref[0].astype(BF16)
        wub_ref[...] = wu_ref[0].astype(BF16)
        wdb_ref[...] = wd_ref[0].astype(BF16)

    @pl.when(valid > 0)
    def _():
        row = lax.broadcasted_iota(jnp.int32, xs_ref.shape, 0)
        xs = jnp.where(row < valid, xs_ref[...], 0.0)
        h = _rms(xs, g_ref[...]).astype(BF16)
        gt = _dot(h, wgb_ref[...])
        hid = gt * _sigmoid(gt) * _dot(h, wub_ref[...])
        o_ref[...] = _dot(hid.astype(BF16), wdb_ref[...])

    @pl.when(valid == 0)
    def _():
        o_ref[...] = jnp.zeros_like(o_ref)


def _expert_blocks(xs, gain, wg, wu, wd, layer, block_e, n_used):
    p_rows, d = xs.shape
    hid = wg.shape[-1]
    blk = MOE_BLOCK
    grid_spec = pltpu.PrefetchScalarGridSpec(
        num_scalar_prefetch=2, grid=(p_rows // blk,),
        in_specs=[pl.BlockSpec((blk, d), lambda b, be, nu: (b, 0)),
                  pl.BlockSpec((1, d), lambda b, be, nu: (0, 0)),
                  pl.BlockSpec((None, 1, d, hid), lambda b, be, nu: (layer, be[b], 0, 0)),
                  pl.BlockSpec((None, 1, d, hid), lambda b, be, nu: (layer, be[b], 0, 0)),
                  pl.BlockSpec((None, 1, hid, d), lambda b, be, nu: (layer, be[b], 0, 0))],
        out_specs=pl.BlockSpec((blk, d), lambda b, be, nu: (b, 0)),
        scratch_shapes=[pltpu.VMEM((d, hid), BF16), pltpu.VMEM((d, hid), BF16), pltpu.VMEM((hid, d), BF16)])
    return pl.pallas_call(
        _expert_block_kernel, out_shape=jax.ShapeDtypeStruct((p_rows, d), F32), grid_spec=grid_spec,
        compiler_params=_params(("arbitrary",)),
        name="moe_expert_blocks",
    )(block_e, n_used, xs, gain, wg, wu, wd)


def _combine_kernel(x_ref, y1_ref, y2_ref, rt_ref, gfin_ref, o_ref, *, final_norm):
    out = x_ref[...] + rt_ref[:, 2:3] * y1_ref[...] + rt_ref[:, 3:4] * y2_ref[...]
    o_ref[...] = _rms(out, gfin_ref[...]) if final_norm else out


def _combine(x2, y_halves, rt, gfin, final_norm):
    n, d = x2.shape
    tm = COMBINE_TM
    nt = n // tm
    return pl.pallas_call(
        functools.partial(_combine_kernel, final_norm=final_norm),
        out_shape=jax.ShapeDtypeStruct((n, d), F32),
        grid=(nt,),
        in_specs=[pl.BlockSpec((tm, d), lambda i: (i, 0)),
                  pl.BlockSpec((tm, d), lambda i: (i, 0)),
                  pl.BlockSpec((tm, d), lambda i: (i + nt, 0)),
                  pl.BlockSpec((tm, LANES), lambda i: (i, 0)), _const_spec((1, d))],
        out_specs=pl.BlockSpec((tm, d), lambda i: (i, 0)),
        compiler_params=_params(("parallel",)), name="moe_combine",
    )(x2, y_halves, y_halves, rt, gfin)


def _moe(x2, rt, cnt, gffn, wg, wu, wd, layer, gfin, final_norm):
    n, d = x2.shape
    dest2, block_e, n_valid = _dispatch_plan(rt, cnt, n)
    xs = _sc_scatter_rows(x2, dest2, block_e.shape[0] * MOE_BLOCK)
    yb = _expert_blocks(xs, gffn, wg, wu, wd, layer, block_e, n_valid)
    y_halves = _sc_gather_rows(yb, dest2.reshape(2 * n))
    return _combine(x2, y_halves, rt, gfin, final_norm)


def kernel(x, mem, positions, norm_mix, w_in, diff_lambda, hgrn_lb_logits, spatial_w, spatial_b, w_branch, w_out,
           norm_mem_q, norm_mem_kv, w_mem_q, w_mem_kv, w_mem_o, norm_ffn, w_router_group, b_router_group,
           w_router_expert, b_router_expert, w_exp_gate, w_exp_up, w_exp_down, norm_final):
    batch, seq, d = x.shape
    depth = w_in.shape[0]
    n = batch * seq
    xf = x.reshape(n, d)
    tabs = _rope_tables(positions)
    row = lambda v: v.reshape(1, -1).astype(F32)
    for l in range(depth):
        lam_init = 0.8 - 0.6 * math.exp(-0.3 * l)
        w1, w_gate = _split_w_in(w_in, l)
        sw = spatial_w[l].reshape(SGU_GROUPS * SGU_CHUNK, SGU_CHUNK)
        sb = jnp.repeat(spatial_b[l].T, SGU_GROUP_DIM, axis=1)
        qat, ka, vat, hb, y_c, qdt, iqt, dkv, dkvt, ikw, iwt = _projection(
            xf, row(norm_mix[l]), w1, tabs, sw, sb, batch, seq)
        y_a = _diff_attention(diff_lambda[l], qat, ka, vat, lam_init, batch, seq)
        y_b = _hgrn(hgrn_lb_logits, hb, l, batch, seq)
        y_d = _dsa(qdt, iqt, iwt, dkv, dkvt, ikw, batch, seq)
        mkt, mv = _mem_kv(mem, row(norm_mem_kv[l]), w_mem_kv[l].astype(BF16))
        e0, e1 = ROUTER_EXPERT_LANE, ROUTER_EXPERT_LANE + MOE_N_EXPERTS
        wr = jnp.zeros((d, LANES), F32)
        wr = wr.at[:, :MOE_GROUPS].set(w_router_group[l]).at[:, e0:e1].set(w_router_expert[l]).astype(BF16)
        br = jnp.zeros((1, LANES), F32)
        br = br.at[0, :MOE_GROUPS].set(b_router_group[l]).at[0, e0:e1].set(b_router_expert[l])
        x2, rt, cnt = _merge(xf, (y_a, y_b, y_c, y_d), row(norm_mix[l]), w_gate, w_branch[l].astype(BF16),
                             w_out[l].astype(BF16), row(norm_mem_q[l]), w_mem_q[l].astype(BF16), mkt, mv,
                             w_mem_o[l].astype(BF16), row(norm_ffn[l]), wr, br, batch, seq)
        xf = _moe(x2, rt, cnt, row(norm_ffn[l]), w_exp_gate, w_exp_up, w_exp_down, l,
                  row(norm_final), final_norm=(l == depth - 1))
    return xf.reshape(batch, seq, d)
```

```python
import functools
import math

import numpy as np
import jax
import jax.numpy as jnp
from jax import lax
from jax.experimental import pallas as pl
from jax.experimental.pallas import tpu as pltpu
from jax.experimental.pallas import tpu_sc as plsc

F32 = jnp.float32
BF16 = jnp.bfloat16

NORM_EPS = 1e-6
ROPE_THETA = 10000.0
NEG_BIG = -1e30

N_BRANCH = 4
BRANCH_WIDTH = 256
DIFF_HEADS = 4
DIFF_HEAD_DIM = 32
HGRN_HEADS = 4
HGRN_DIM = 64
HGRN_CHUNK = 32
HGRN_UNROLL = 8
HGRN_MIN_FORGET = 1e-30
SGU_GROUPS = 4
SGU_GROUP_DIM = 64
SGU_CHUNK = 128
DSA_HEADS = 4
DSA_HEAD_DIM = 64
DSA_IDX_HEADS = 4
DSA_IDX_DIM = 32
DSA_TOPK = 256
MEM_HEADS = 4
MEM_HEAD_DIM = 64
MOE_GROUPS = 4
MOE_EXPERTS_PER_GROUP = 8
MOE_N_EXPERTS = 32
MOE_BLOCK = 512
ROUTER_EXPERT_LANE = 32
SC_CORES = 2
SC_SUBCORES = 16
SC_GATHER_CHUNK = 16
SC_GATHER_BUFS = 4

LANES = 128
VMEM_LIMIT = 56 * 1024 * 1024

PROJ_TM = 512
DIFF_TQ = 256
DIFF_TK = 512
HGRN_TC = 512
DSA_TQ = 256
DSA_TK = 512
MERGE_TM = 512
COMBINE_TM = 512
ROPE_TM = 1024

C_AQ, C_AK, C_AV = 0, 256, 512
C_HB = 768
C_UV = 1792
C_DQ = 2304
C_DKV = 2560
C_IQ = 2688
C_IKW = 2816
IW_LANE = 32
C_GATE = 2852
C_TOTAL = 2944
LOG2E = math.log2(math.e)


def _params(sem):
    return pltpu.CompilerParams(dimension_semantics=sem, vmem_limit_bytes=VMEM_LIMIT)


def _const_spec(shape):
    nd = len(shape)
    return pl.BlockSpec(shape, lambda *_: (0,) * nd, pipeline_mode=pl.Buffered(1))


def _rms(xf, gain=None):
    y = xf * lax.rsqrt(jnp.mean(xf * xf, axis=-1, keepdims=True) + NORM_EPS)
    return y if gain is None else y * gain


def _sigmoid(x):
    return 0.5 * jnp.tanh(0.5 * x) + 0.5


def _pack_bf16_pairs(x):
    w = x.shape[-1] // 2
    xb = x.astype(BF16).astype(F32)
    lo = lax.shift_right_logical(pltpu.bitcast(xb[:, :w], jnp.int32), 16)
    hi = pltpu.bitcast(xb[:, w:], jnp.int32) & jnp.int32(-65536)
    return pltpu.bitcast(hi | lo, F32)


def _unpack_bf16_pairs(words):
    bits = pltpu.bitcast(words, jnp.int32)
    lo = pltpu.bitcast(bits << 16, F32)
    hi = pltpu.bitcast(bits & jnp.int32(-65536), F32)
    return jnp.concatenate([lo, hi], axis=1)


def _dot(a, b):
    return jnp.dot(a, b, preferred_element_type=F32)


def _dot_nt(a, b):
    return lax.dot_general(a, b, (((1,), (1,)), ((), ())), preferred_element_type=F32)


def _rope_table_kernel(pos_ref, frq_ref, sgn_ref, c32_ref, s32_ref, c64_ref, s64_ref):
    pos = pos_ref[...].astype(F32)
    twice = lambda t: jnp.concatenate([t, t], axis=1)
    a32 = pos * frq_ref[0:1, :LANES]
    a64 = pos * frq_ref[1:2, :LANES]
    c32_ref[...] = twice(jnp.cos(a32))
    s32_ref[...] = twice(jnp.sin(a32) * sgn_ref[0:1, :LANES])
    c64_ref[...] = twice(jnp.cos(a64))
    s64_ref[...] = twice(jnp.sin(a64) * sgn_ref[1:2, :LANES])


def _rope_tables(positions):
    n = positions.size
    pos = positions.reshape(n, 1).astype(jnp.int32)
    lane = np.arange(256)
    inv32 = ROPE_THETA ** (-jnp.arange(16, dtype=F32) * (2.0 / 32))
    inv64 = ROPE_THETA ** (-jnp.arange(32, dtype=F32) * (2.0 / 64))
    frq = jnp.stack([inv32[lane % 16], inv64[lane % 32]])
    sgn = jnp.asarray(np.stack([np.where(lane % 32 < 16, -1.0, 1.0),
                                np.where(lane % 64 < 32, -1.0, 1.0)]), F32)
    tm = ROPE_TM
    tab = jax.ShapeDtypeStruct((n, 256), F32)
    return pl.pallas_call(
        _rope_table_kernel,
        out_shape=(tab, tab, tab, tab),
        grid=(n // tm,),
        in_specs=[pl.BlockSpec((tm, 1), lambda i: (i, 0)), _const_spec((2, 256)), _const_spec((2, 256))],
        out_specs=tuple(pl.BlockSpec((tm, 256), lambda i: (i, 0)) for _ in range(4)),
        compiler_params=_params(("parallel",)),
        name="rope_tables",
    )(pos, frq, sgn)


def _w1_kernel(wt_ref, o_ref, *, layer):
    o_ref[...] = wt_ref[:, layer, :].T.astype(BF16)


def _wgate_kernel(a_ref, b_ref, o_ref, *, layer):
    off = C_GATE % LANES
    o_ref[...] = jnp.concatenate([a_ref[off:, layer, :], b_ref[:off, layer, :]], axis=0).T.astype(BF16)


def _split_w_in(w_in, layer):
    depth, d, width = w_in.shape
    wt = jnp.transpose(w_in, (2, 0, 1))
    rows = lambda f: pl.BlockSpec((LANES, depth, d), lambda i: (f(i), 0, 0))
    cols = pl.BlockSpec((d, LANES), lambda i: (0, i))
    w1 = pl.pallas_call(
        functools.partial(_w1_kernel, layer=layer), out_shape=jax.ShapeDtypeStruct((d, C_TOTAL), BF16),
        grid=(C_TOTAL // LANES,), in_specs=[rows(lambda i: i)], out_specs=cols,
        compiler_params=_params(("parallel",)), name="split_w_in",
    )(wt)
    a0 = C_GATE // LANES
    gate_w = width - C_GATE
    wg = pl.pallas_call(
        functools.partial(_wgate_kernel, layer=layer), out_shape=jax.ShapeDtypeStruct((d, gate_w), BF16),
        grid=(gate_w // LANES,), in_specs=[rows(lambda i: a0 + i), rows(lambda i: a0 + i + 1)], out_specs=cols,
        compiler_params=_params(("parallel",)), name="split_w_gate",
    )(wt, wt)
    return w1, wg


def _gelu_tanh(x):
    return 0.5 * x * (1.0 + jnp.tanh(math.sqrt(2.0 / math.pi) * (x + 0.044715 * (x * x * x))))


def _rope(x, cos, sin_signed, half):
    w = x.shape[-1]
    lane = lax.broadcasted_iota(jnp.int32, x.shape, 1)
    partner = jnp.where(lane % (2 * half) < half, pltpu.roll(x, w - half, 1), pltpu.roll(x, half, 1))
    return x * cos + partner * sin_signed


def _proj_kernel(x_ref, g_ref, w_ref, c32_ref, s32_ref, c64_ref, s64_ref, sw_ref, sb_ref,
                 qat_ref, ka_ref, vat_ref, hb_ref, yc_ref, qdt_ref, iqt_ref, dkv_ref, dkvt_ref, ikw_ref, iwt_ref,
                 *, tm):
    h = _rms(x_ref[...], g_ref[...]).astype(BF16)

    def proj(c0, width):
        return _dot(h, w_ref[:, c0:c0 + width])

    c32, s32, c64, s64 = c32_ref[...], s32_ref[...], c64_ref[...], s64_ref[...]
    qat_ref[0] = (_rope(proj(C_AQ, 256), c32, s32, 16) * (DIFF_HEAD_DIM ** -0.5 * LOG2E)).T.astype(BF16)
    ka_ref[...] = _rope(proj(C_AK, 256), c32, s32, 16).astype(BF16)
    vat_ref[0] = proj(C_AV, 256).astype(BF16).T
    hb_ref[...] = proj(C_HB, 1024)
    qdt_ref[0] = (_rope(proj(C_DQ, 256), c64, s64, 32) * (DSA_HEAD_DIM ** -0.5 * LOG2E)).T.astype(BF16)
    iqt_ref[0] = _rope(proj(C_IQ, 128), c32[:, :128], s32[:, :128], 16).T.astype(BF16)
    lane = lax.broadcasted_iota(jnp.int32, (tm, 128), 1)
    is_k = lane < DSA_HEAD_DIM
    dkv = _rope(proj(C_DKV, 128), jnp.where(is_k, c64[:, :128], 1.0), jnp.where(is_k, s64[:, :128], 0.0), 32)
    is_ik = lane < DSA_IDX_DIM
    ikw = _rope(proj(C_IKW, 128), jnp.where(is_ik, c32[:, :128], 1.0), jnp.where(is_ik, s32[:, :128], 0.0), 16)
    dkv_ref[...] = dkv.astype(BF16)
    dkvt_ref[0] = dkv.T.astype(BF16)
    ikw_ref[...] = ikw.astype(BF16)
    iw_scale = DSA_IDX_HEADS ** -0.5 * DSA_IDX_DIM ** -0.5
    iwt_ref[0] = (ikw * iw_scale).T[IW_LANE:IW_LANE + 8, :]
    uv = _gelu_tanh(proj(C_UV, 512))
    u, v = uv[:, :256], uv[:, 256:]
    mu = jnp.mean(v, axis=-1, keepdims=True)
    vc = v - mu
    vn = (vc * lax.rsqrt(jnp.mean(vc * vc, axis=-1, keepdims=True) + NORM_EPS)).astype(BF16)
    r = lax.broadcasted_iota(jnp.int32, (SGU_GROUPS * SGU_CHUNK, SGU_CHUNK), 0)
    c = lax.broadcasted_iota(jnp.int32, (SGU_GROUPS * SGU_CHUNK, SGU_CHUNK), 1)
    wt = jnp.where((r % SGU_CHUNK) >= c, sw_ref[...], 0.0).astype(BF16)
    lane_grp = lax.broadcasted_iota(jnp.int32, (SGU_CHUNK, 256), 1) // SGU_GROUP_DIM
    for ch in range(tm // SGU_CHUNK):
        r0 = ch * SGU_CHUNK
        full = _dot(wt, vn[r0:r0 + SGU_CHUNK, :])
        mixed = sb_ref[...]
        for g in range(SGU_GROUPS):
            mixed = mixed + jnp.where(lane_grp == g, full[g * SGU_CHUNK:(g + 1) * SGU_CHUNK, :], 0.0)
        yc_ref[r0:r0 + SGU_CHUNK, :] = (u[r0:r0 + SGU_CHUNK, :] * mixed).astype(BF16)


def _projection(x, gain, w1, tabs, sw, sb, batch, seq):
    n, d = x.shape
    tm = PROJ_TM
    spt = seq // tm
    tok = lambda w: pl.BlockSpec((tm, w), lambda i: (i, 0))
    tr = lambda rows: pl.BlockSpec((1, rows, tm), lambda i: (i // spt, 0, i % spt))
    out_shape = (
        jax.ShapeDtypeStruct((batch, 256, seq), BF16),
        jax.ShapeDtypeStruct((n, 256), BF16),
        jax.ShapeDtypeStruct((batch, 256, seq), BF16),
        jax.ShapeDtypeStruct((n, 1024), F32),
        jax.ShapeDtypeStruct((n, 256), BF16),
        jax.ShapeDtypeStruct((batch, 256, seq), BF16),
        jax.ShapeDtypeStruct((batch, 128, seq), BF16),
        jax.ShapeDtypeStruct((n, 128), BF16),
        jax.ShapeDtypeStruct((batch, 128, seq), BF16),
        jax.ShapeDtypeStruct((n, 128), BF16),
        jax.ShapeDtypeStruct((batch, 8, seq), F32),
    )
    return pl.pallas_call(
        functools.partial(_proj_kernel, tm=tm),
        out_shape=out_shape,
        grid=(n // tm,),
        in_specs=[tok(d), _const_spec((1, d)), _const_spec((d, C_TOTAL)),
                  tok(256), tok(256), tok(256), tok(256),
                  _const_spec((SGU_GROUPS * SGU_CHUNK, SGU_CHUNK)), _const_spec((SGU_CHUNK, 256))],
        out_specs=(tr(256), tok(256), tr(256), tok(1024), tok(256), tr(256), tr(128), tok(128), tr(128), tok(128), tr(8)),
        compiler_params=_params(("parallel",)),
        name="projection",
    )(x, gain, w1, *tabs, sw, sb)


def _diff_attn_kernel(lam_ref, qt_ref, k_ref, vt_ref, o_ref, sa_ref, sb_ref, *, lam_init, tq, tk):
    q0 = pl.program_id(1) * tq
    kb_diag = q0 // tk
    lv = lam_ref[...]
    lam = (jnp.exp(jnp.sum(lv[0:1] * lv[1:2], axis=-1, keepdims=True))
           - jnp.exp(jnp.sum(lv[2:3] * lv[3:4], axis=-1, keepdims=True)) + lam_init)
    qt = qt_ref[0]
    feat = lax.broadcasted_iota(jnp.int32, (256, tq), 0) // DIFF_HEAD_DIM
    n_maps = 2 * DIFF_HEADS
    qz = jnp.concatenate([jnp.where(feat == i, qt, jnp.zeros_like(qt)) for i in range(n_maps)], axis=1)
    wide = n_maps * tq
    key_i = lax.broadcasted_iota(jnp.int32, (tk, wide), 0)
    qry_i = q0 + lax.broadcasted_iota(jnp.int32, (tk, wide), 1) % tq

    def scores(kb, dst_ref):
        k0 = pl.multiple_of(kb * tk, tk)
        dst_ref[...] = _dot(k_ref[pl.ds(k0, tk), :], qz)

    def absorb(src_ref, kb, carry, masked):
        m_i, l_i, acc = carry
        k0 = pl.multiple_of(kb * tk, tk)
        s = src_ref[...]
        if masked:
            s = jnp.where(k0 + key_i <= qry_i, s, NEG_BIG)
        m_new = jnp.maximum(m_i, jnp.max(s, axis=0, keepdims=True))
        p = jnp.exp2(s - m_new)
        alpha = jnp.exp2(m_i - m_new)
        l_new = alpha * l_i + jnp.sum(p, axis=0, keepdims=True)
        pb = p.astype(BF16)
        pv = jnp.concatenate(
            [_dot(vt_ref[0, hd * 64:(hd + 1) * 64, pl.ds(k0, tk)], pb[:, 2 * hd * tq:(2 * hd + 2) * tq])
             for hd in range(DIFF_HEADS)], axis=1)
        return m_new, l_new, alpha * acc + pv

    def pair(j, carry):
        kb = 2 * j
        scores(kb + 1, sb_ref)
        carry = absorb(sa_ref, kb, carry, False)
        scores(kb + 2, sa_ref)
        return absorb(sb_ref, kb + 1, carry, False)

    init = (jnp.full((1, wide), NEG_BIG, F32), jnp.zeros((1, wide), F32), jnp.zeros((64, wide), F32))
    scores(0, sa_ref)
    carry = lax.fori_loop(0, kb_diag // 2, pair, init)

    def tail_odd(carry):
        scores(kb_diag, sb_ref)
        carry = absorb(sa_ref, kb_diag - 1, carry, False)
        return absorb(sb_ref, kb_diag, carry, True)

    def tail_even(carry):
        return absorb(sa_ref, kb_diag, carry, True)

    _, l_f, acc = lax.cond(kb_diag % 2 == 1, tail_odd, tail_even, carry)
    o_all = acc / l_f
    heads = []
    for hd in range(DIFF_HEADS):
        o0 = o_all[:, 2 * hd * tq:(2 * hd + 1) * tq]
        o1 = o_all[:, (2 * hd + 1) * tq:(2 * hd + 2) * tq]
        o_h = o0 - lam * o1
        ms = jnp.mean(o_h * o_h, axis=0, keepdims=True)
        heads.append(o_h * lax.rsqrt(ms + NORM_EPS) * (1.0 - lam_init))
    o_ref[...] = jnp.concatenate(heads, axis=0).T.astype(BF16)


def _diff_attention(lam_vec, qat, ka, vat, lam_init, batch, seq):
    tq, tk = DIFF_TQ, DIFF_TK
    nq = seq // tq
    return pl.pallas_call(
        functools.partial(_diff_attn_kernel, lam_init=lam_init, tq=tq, tk=tk),
        out_shape=jax.ShapeDtypeStruct((batch * seq, 256), BF16),
        grid=(batch, nq),
        in_specs=[_const_spec((4, DIFF_HEAD_DIM)),
                  pl.BlockSpec((1, 256, tq), lambda b, i: (b, 0, i)),
                  pl.BlockSpec((seq, 256), lambda b, i: (b, 0)),
                  pl.BlockSpec((1, 256, seq), lambda b, i: (b, 0, 0))],
        out_specs=pl.BlockSpec((tq, 256), lambda b, i: (b * nq + i, 0)),
        scratch_shapes=[pltpu.VMEM((tk, 2 * DIFF_HEADS * tq), F32), pltpu.VMEM((tk, 2 * DIFF_HEADS * tq), F32)],
        compiler_params=_params(("parallel", "parallel")),
        name="diff_attention",
    )(lam_vec, qat, ka, vat)


def _hgrn_kernel(lbl_ref, hb_ref, o_ref, st_ref, pstk_ref, *, layer, tc):
    cz = HGRN_CHUNK
    w = 256

    @pl.when(pl.program_id(1) == 0)
    def _():
        st_ref[...] = jnp.zeros_like(st_ref)

    lg = lbl_ref[...]
    e = jnp.exp(lg - jnp.max(lg, axis=0, keepdims=True))
    lw = e / jnp.sum(e, axis=0, keepdims=True)
    lb = jnp.sum(lw[0:layer + 1], axis=0, keepdims=True) - lw[0:1]

    ri = lax.broadcasted_iota(jnp.int32, (cz, cz), 0)
    ci = lax.broadcasted_iota(jnp.int32, (cz, cz), 1)
    tri = (ri >= ci).astype(F32)
    rb = lax.broadcasted_iota(jnp.int32, (w, w), 0) // HGRN_DIM
    cb = lax.broadcasted_iota(jnp.int32, (w, w), 1) // HGRN_DIM
    same_head = rb == cb
    head_ones = same_head.astype(BF16)
    trows = {r: r + lax.broadcasted_iota(jnp.int32, (cz - r, w), 0) for r in range(0, cz, 16)}

    def chunk(c, carry):
        r0 = pl.multiple_of(c * cz, cz)
        q = hb_ref[pl.ds(r0, cz), 0:256]
        fp = hb_ref[pl.ds(r0, cz), 256:512]
        v = hb_ref[pl.ds(r0, cz), 512:768]
        g = hb_ref[pl.ds(r0, cz), 768:1024]
        qf = q * _sigmoid(q)
        f = lb + (1.0 - lb) * jax.nn.sigmoid(fp)
        log_f = jnp.log(jnp.maximum(f, HGRN_MIN_FORGET))
        kf = (1.0 - lb) * jax.nn.sigmoid(-fp)
        bc = jnp.dot(tri, log_f, preferred_element_type=F32, precision=lax.Precision.HIGHEST)
        st = st_ref[...]
        o = _dot_nt((qf * jnp.exp(bc)).astype(BF16), st.astype(BF16))
        for s in range(cz):
            r_lo = (s // 16) * 16
            arg = bc[r_lo:, :] - bc[s:s + 1, :]
            if s > r_lo:
                arg = jnp.where(trows[r_lo] >= s, arg, NEG_BIG)
            p = qf[r_lo:, :] * kf[s:s + 1, :] * jnp.exp(arg)
            if r_lo:
                pstk_ref[s * cz:s * cz + r_lo, :] = jnp.zeros((r_lo, w), BF16)
            pstk_ref[s * cz + r_lo:(s + 1) * cz, :] = p.astype(BF16)
        accs = [jnp.zeros((16, w), F32) for _ in range(cz // 16)]
        for sg in range(cz // 16):
            att = _dot(pstk_ref[sg * 16 * cz:(sg + 1) * 16 * cz, :], head_ones)
            for sl in range(16):
                s = sg * 16 + sl
                for j in range(sg, cz // 16):
                    accs[j] = accs[j] + att[sl * cz + 16 * j:sl * cz + 16 * j + 16, :] * v[s:s + 1, :]
        o = o + jnp.concatenate(accs, axis=0)
        b_end = bc[cz - 1:cz, :]
        kd = kf * jnp.exp(b_end - bc)
        upd = _dot(v.T.astype(BF16), kd.astype(BF16))
        st_ref[...] = st * jnp.exp(b_end) + jnp.where(same_head, upd, 0.0)
        ms = _dot(o * o, head_ones.astype(F32)) * (1.0 / HGRN_DIM)
        y = o * lax.rsqrt(ms + NORM_EPS)
        o_ref[pl.ds(r0, cz), :] = (y * (g * _sigmoid(g))).astype(BF16)
        return carry

    def group(gi, carry):
        for u in range(HGRN_UNROLL):
            chunk(gi * HGRN_UNROLL + u, carry)
        return carry

    lax.fori_loop(0, tc // cz // HGRN_UNROLL, group, 0)


def _hgrn(lb_logits, hb, layer, batch, seq):
    tc = HGRN_TC
    nt = seq // tc
    cz = HGRN_CHUNK
    return pl.pallas_call(
        functools.partial(_hgrn_kernel, layer=layer, tc=tc),
        out_shape=jax.ShapeDtypeStruct((batch * seq, 256), BF16),
        grid=(batch, nt),
        in_specs=[_const_spec(lb_logits.shape),
                  pl.BlockSpec((tc, 1024), lambda b, i: (b * nt + i, 0))],
        out_specs=pl.BlockSpec((tc, 256), lambda b, i: (b * nt + i, 0)),
        scratch_shapes=[pltpu.VMEM((256, 256), F32), pltpu.VMEM((cz * cz, 256), BF16)],
        compiler_params=_params(("parallel", "arbitrary")),
        name="hgrn2",
    )(lb_logits, hb)


def _dsa_kernel(qdt_ref, iqt_ref, iwt_ref, dkv_ref, dkvt_ref, ikw_ref, o_ref, key_ref, bias_ref, half_ref,
                sa_ref, sb_ref, *, tq, tk, n_sel):
    q0 = pl.program_id(1) * tq
    nkb = q0 // tk + 1
    key_i = lax.broadcasted_iota(jnp.int32, (tk, tq), 0)
    qry_i = q0 + lax.broadcasted_iota(jnp.int32, (tk, tq), 1)
    grp = tk // 8
    rows8 = lambda x: x.reshape(grp, 8, tq)
    iqt = iqt_ref[0]
    zpad = jnp.zeros((LANES - DSA_IDX_DIM, tq), BF16)
    iqz = jnp.concatenate([jnp.concatenate([iqt[hd * DSA_IDX_DIM:(hd + 1) * DSA_IDX_DIM, :], zpad], axis=0)
                           for hd in range(DSA_IDX_HEADS)], axis=1)
    iw = iwt_ref[0]

    def score_block(kb, carry):
        k0 = pl.multiple_of(kb * tk, tk)
        sh = jnp.maximum(_dot(ikw_ref[pl.ds(k0, tk), :], iqz), 0.0)
        sc = jnp.zeros((tk, tq), F32)
        for hd in range(DSA_IDX_HEADS):
            sc = sc + sh[:, hd * tq:(hd + 1) * tq] * iw[hd:hd + 1, :]
        sc = jnp.where(k0 + key_i <= qry_i, sc + 0.0, -jnp.inf)
        bits = pltpu.bitcast(sc, jnp.int32)
        key = jnp.where(bits < 0, bits ^ jnp.int32(0x7FFFFFFF), bits)
        key_ref[pl.ds(k0, tk), :] = key
        half_ref[pl.ds(k0, tk), :] = (key >> 16).astype(jnp.int16)
        return carry

    lax.fori_loop(0, nkb, score_block, 0)

    one16, zero16 = jnp.ones((), jnp.int16), jnp.zeros((), jnp.int16)
    low16 = np.int16(-2 ** 15)

    def count16(limit, strict):
        def body(kb, acc):
            k0 = pl.multiple_of(kb * tk, tk)
            for c in range(tk // 128):
                blk = half_ref[pl.ds(k0 + 128 * c, 128), :].reshape(8, 16, tq)
                hit = jnp.where(blk > limit if strict else blk >= limit, one16, zero16)
                parts = [hit[j] for j in range(8)]
                while len(parts) > 1:
                    parts = [a + b for a, b in zip(parts[0::2], parts[1::2])]
                acc = acc + parts[0]
            return acc
        acc = lax.fori_loop(0, nkb, body, jnp.zeros((16, tq), jnp.int16))
        return jnp.broadcast_to(jnp.sum(acc.astype(jnp.int32), axis=0, keepdims=True), (16, tq))

    def search16(need):
        t = jnp.full((16, tq), -2 ** 15, jnp.int32)
        for bit in range(15, -1, -1):
            trial = t + 2 ** bit
            t = jnp.where(count16(trial.astype(jnp.int16), False) >= need, trial, t)
        return t

    t_hi = search16(n_sel)
    t_hi16 = t_hi.astype(jnp.int16)
    need_lo = n_sel - count16(t_hi16, True)

    def low_block(kb, carry):
        k0 = pl.multiple_of(kb * tk, tk)
        lo = ((key_ref[pl.ds(k0, tk), :] & 0xFFFF) - 2 ** 15).astype(jnp.int16).reshape(tk // 16, 16, tq)
        hi = half_ref[pl.ds(k0, tk), :].reshape(tk // 16, 16, tq)
        half_ref[pl.ds(k0, tk), :] = jnp.where(hi == t_hi16, lo, low16).reshape(tk, tq)
        return carry

    lax.fori_loop(0, nkb, low_block, 0)
    t_lo = search16(need_lo)
    thr = ((t_hi << 16) | (t_lo + 2 ** 15))[0:8, :]

    need = (need_lo - count16(t_lo.astype(jnp.int16), True))[0:8, :].astype(F32)
    ur = lax.broadcasted_iota(jnp.int32, (tk, tk), 0)
    uc = lax.broadcasted_iota(jnp.int32, (tk, tk), 1)
    earlier = (uc < ur).astype(BF16)
    ones8 = jnp.ones((8, tk), BF16)

    def select_block(kb, seen):
        k0 = pl.multiple_of(kb * tk, tk)
        blk = rows8(key_ref[pl.ds(k0, tk), :])
        eq = blk == thr[None]
        eqb = jnp.where(eq, 1.0, 0.0).reshape(tk, tq).astype(BF16)
        rank = rows8(_dot(earlier, eqb)) + seen[None]
        sel = (blk > thr[None]) | (eq & (rank < need[None]))
        bias = jnp.where(sel, 0.0, NEG_BIG).reshape(tk, tq)
        bias_ref[pl.ds(k0, tk), :] = jnp.where(k0 + key_i <= qry_i, bias, NEG_BIG)
        return seen + _dot(ones8, eqb)

    lax.fori_loop(0, nkb, select_block, jnp.zeros((8, tq), F32))

    qdt = qdt_ref[0]
    zq = jnp.zeros((LANES - DSA_HEAD_DIM, tq), BF16)
    qz = jnp.concatenate([jnp.concatenate([qdt[hd * DSA_HEAD_DIM:(hd + 1) * DSA_HEAD_DIM, :], zq], axis=0)
                          for hd in range(DSA_HEADS)], axis=1)
    wide = DSA_HEADS * tq

    def scores(kb, dst_ref):
        k0 = pl.multiple_of(kb * tk, tk)
        dst_ref[...] = _dot(dkv_ref[pl.ds(k0, tk), :], qz)

    def absorb(src_ref, kb, carry):
        m_i, l_i, acc = carry
        k0 = pl.multiple_of(kb * tk, tk)
        bias = bias_ref[pl.ds(k0, tk), :]
        s = src_ref[...] + jnp.concatenate([bias] * DSA_HEADS, axis=1)
        m_new = jnp.maximum(m_i, jnp.max(s, axis=0, keepdims=True))
        p = jnp.exp2(s - m_new)
        alpha = jnp.exp2(m_i - m_new)
        l_new = alpha * l_i + jnp.sum(p, axis=0, keepdims=True)
        pv = _dot(dkvt_ref[0, DSA_HEAD_DIM:, pl.ds(k0, tk)], p.astype(BF16))
        return m_new, l_new, alpha * acc + pv

    def pair(j, carry):
        kb = 2 * j
        scores(kb + 1, sb_ref)
        carry = absorb(sa_ref, kb, carry)
        scores(kb + 2, sa_ref)
        return absorb(sb_ref, kb + 1, carry)

    init = (jnp.full((1, wide), NEG_BIG, F32), jnp.zeros((1, wide), F32), jnp.zeros((DSA_HEAD_DIM, wide), F32))
    last = nkb - 1
    scores(0, sa_ref)
    carry = lax.fori_loop(0, last // 2, pair, init)

    def tail_odd(carry):
        scores(last, sb_ref)
        return absorb(sb_ref, last, absorb(sa_ref, last - 1, carry))

    def tail_even(carry):
        return absorb(sa_ref, last, carry)

    _, l_f, acc = lax.cond(last % 2 == 1, tail_odd, tail_even, carry)
    o_all = acc / l_f
    o_ref[...] = jnp.concatenate([o_all[:, hd * tq:(hd + 1) * tq] for hd in range(DSA_HEADS)],
                                 axis=0).T.astype(BF16)


def _dsa(qdt, iqt, iwt, dkv, dkvt, ikw, batch, seq):
    tq, tk = DSA_TQ, DSA_TK
    nq = seq // tq
    n_sel = min(DSA_TOPK, seq // 4)
    return pl.pallas_call(
        functools.partial(_dsa_kernel, tq=tq, tk=tk, n_sel=n_sel),
        out_shape=jax.ShapeDtypeStruct((batch * seq, 256), BF16),
        grid=(batch, nq),
        in_specs=[pl.BlockSpec((1, 256, tq), lambda b, i: (b, 0, i)),
                  pl.BlockSpec((1, 128, tq), lambda b, i: (b, 0, i)),
                  pl.BlockSpec((1, 8, tq), lambda b, i: (b, 0, i)),
                  pl.BlockSpec((seq, 128), lambda b, i: (b, 0)),
                  pl.BlockSpec((1, 128, seq), lambda b, i: (b, 0, 0)),
                  pl.BlockSpec((seq, 128), lambda b, i: (b, 0))],
        out_specs=pl.BlockSpec((tq, 256), lambda b, i: (b * nq + i, 0)),
        scratch_shapes=[pltpu.VMEM((seq, tq), jnp.int32), pltpu.VMEM((seq, tq), F32),
                        pltpu.VMEM((seq, tq), jnp.int16),
                        pltpu.VMEM((tk, DSA_HEADS * tq), F32), pltpu.VMEM((tk, DSA_HEADS * tq), F32)],
        compiler_params=_params(("parallel", "parallel")),
        name="dsa",
    )(qdt, iqt, iwt, dkv, dkvt, ikw)


def _mem_kv_kernel(mem_ref, g_ref, w_ref, kt_ref, v_ref):
    mn = _rms(mem_ref[0], g_ref[...]).astype(BF16)
    kv = _dot(mn, w_ref[...])
    kt_ref[0] = kv[:, :256].T.astype(BF16)
    v_ref[0] = kv[:, 256:].astype(BF16)


def _mem_kv(mem, gain, w_kv):
    b, m, d = mem.shape
    return pl.pallas_call(
        _mem_kv_kernel,
        out_shape=(jax.ShapeDtypeStruct((b, 256, m), BF16), jax.ShapeDtypeStruct((b, m, 256), BF16)),
        grid=(b,),
        in_specs=[pl.BlockSpec((1, m, d), lambda i: (i, 0, 0)), _const_spec((1, d)), _const_spec((d, 512))],
        out_specs=(pl.BlockSpec((1, 256, m), lambda i: (i, 0, 0)), pl.BlockSpec((1, m, 256), lambda i: (i, 0, 0))),
        compiler_params=_params(("parallel",)),
        name="mem_kv",
    )(mem, gain, w_kv)


def _merge_kernel(x_ref, ya_ref, yb_ref, yc_ref, yd_ref, gmix_ref, wg_ref, wbr_ref, wout_ref,
                  gq_ref, wq_ref, mkt_ref, mv_ref, wo_ref, gffn_ref, wr_ref, br_ref,
                  x2_ref, hp_ref, rt_ref, cnt_ref, run_ref, *, tm):
    x = x_ref[...]
    d = x.shape[-1]
    h = _rms(x, gmix_ref[...]).astype(BF16)
    merged = jnp.zeros((tm, d), F32)
    for n, y_ref in enumerate((ya_ref, yb_ref, yc_ref, yd_ref)):
        gate = _sigmoid(_dot(h, wg_ref[:, n * d:(n + 1) * d]))
        merged = merged + gate * _dot(y_ref[...], wbr_ref[n])
    x1 = x + _dot(merged.astype(BF16), wout_ref[...])
    h2 = _rms(x1, gq_ref[...]).astype(BF16)
    q = (_dot(h2, wq_ref[...]) * (MEM_HEAD_DIM ** -0.5)).astype(BF16)
    lane_head = lax.broadcasted_iota(jnp.int32, (tm, 256), 1) // MEM_HEAD_DIM
    mv = mv_ref[0]
    o = jnp.zeros((tm, 256), F32)
    for hd in range(MEM_HEADS):
        s = _dot(q[:, hd * MEM_HEAD_DIM:(hd + 1) * MEM_HEAD_DIM], mkt_ref[0, hd * MEM_HEAD_DIM:(hd + 1) * MEM_HEAD_DIM, :])
        p = jnp.exp(s - jnp.max(s, axis=-1, keepdims=True))
        p = p / jnp.sum(p, axis=-1, keepdims=True)
        o = o + jnp.where(lane_head == hd, _dot(p.astype(BF16), mv), 0.0)
    x2 = x1 + _dot(o.astype(BF16), wo_ref[...])
    x2_ref[...] = x2
    h3f = _rms(x2, gffn_ref[...])
    hp_ref[...] = _pack_bf16_pairs(h3f)
    h3 = h3f.astype(BF16)
    logits = _dot(h3, wr_ref[...]) + br_ref[...]
    lane = lax.broadcasted_iota(jnp.int32, (tm, LANES), 1)
    gl = jnp.where(lane < MOE_GROUPS, logits, -jnp.inf)
    gmax = jnp.max(gl, axis=-1, keepdims=True)
    gsel = jnp.min(jnp.where(gl == gmax, lane, LANES), axis=-1, keepdims=True)
    pg_sel = 1.0 / jnp.sum(jnp.exp(gl - gmax), axis=-1, keepdims=True)
    in_group = (lane - ROUTER_EXPERT_LANE) // MOE_EXPERTS_PER_GROUP == gsel
    el = jnp.where(in_group, logits, -jnp.inf)
    m1 = jnp.max(el, axis=-1, keepdims=True)
    i1 = jnp.min(jnp.where(el == m1, lane, LANES), axis=-1, keepdims=True)
    el2 = jnp.where(lane == i1, -jnp.inf, el)
    m2 = jnp.max(el2, axis=-1, keepdims=True)
    i2 = jnp.min(jnp.where(el2 == m2, lane, LANES), axis=-1, keepdims=True)
    e21 = jnp.exp(m2 - m1)
    c1 = pg_sel / (1.0 + e21)
    @pl.when(pl.program_id(0) == 0)
    def _():
        run_ref[...] = jnp.zeros_like(run_ref)

    oh1 = jnp.where(lane == i1, 1.0, 0.0)
    oh2 = jnp.where(lane == i2, 1.0, 0.0)
    both = oh1 + oh2
    tr = lax.broadcasted_iota(jnp.int32, (tm, tm), 0)
    tc = lax.broadcasted_iota(jnp.int32, (tm, tm), 1)
    before = _dot(jnp.where(tc < tr, 1.0, 0.0).astype(BF16), both.astype(BF16)) + run_ref[0:1, :]
    r1 = jnp.sum(oh1 * before, axis=-1, keepdims=True)
    r2 = jnp.sum(oh2 * before, axis=-1, keepdims=True)
    total = run_ref[...] + jnp.sum(both, axis=0, keepdims=True)
    run_ref[...] = total
    cnt_ref[...] = total
    ids = (jnp.where(lane == 0, i1, i2) - ROUTER_EXPERT_LANE).astype(F32)
    rt_ref[...] = jnp.where(lane < 2, ids, jnp.where(lane == 2, c1, jnp.where(lane == 3, c1 * e21,
                            jnp.where(lane == 4, r1, jnp.where(lane == 5, r2, 0.0)))))


def _merge(x, ys, gmix, wg, wbr, wout, gq, wq, mkt, mv, wo, gffn, wr, br, batch, seq):
    n, d = x.shape
    tm = MERGE_TM
    spt = seq // tm
    m = mv.shape[1]
    tok = lambda w: pl.BlockSpec((tm, w), lambda i: (i, 0))
    return pl.pallas_call(
        functools.partial(_merge_kernel, tm=tm),
        out_shape=(jax.ShapeDtypeStruct((n, d), F32), jax.ShapeDtypeStruct((n, d // 2), F32),
                   jax.ShapeDtypeStruct((n, LANES), F32), jax.ShapeDtypeStruct((8, LANES), F32)),
        grid=(n // tm,),
        in_specs=[tok(d), tok(256), tok(256), tok(256), tok(256),
                  _const_spec((1, d)), _const_spec((d, N_BRANCH * d)), _const_spec((N_BRANCH, 256, d)),
                  _const_spec((d, d)), _const_spec((1, d)), _const_spec((d, 256)),
                  pl.BlockSpec((1, 256, m), lambda i: (i // spt, 0, 0)),
                  pl.BlockSpec((1, m, 256), lambda i: (i // spt, 0, 0)),
                  _const_spec((256, d)), _const_spec((1, d)), _const_spec((d, LANES)), _const_spec((1, LANES))],
        out_specs=(tok(d), tok(d // 2), tok(LANES), pl.BlockSpec((8, LANES), lambda i: (0, 0))),
        scratch_shapes=[pltpu.VMEM((8, LANES), F32)],
        compiler_params=_params(("arbitrary",)),
        name="merge_mem_router",
    )(x, *ys, gmix, wg, wbr, wout, gq, wq, mkt, mv, wo, gffn, wr, br)


def _sc_gather_rows(table, idx):
    _, width = table.shape
    total = idx.shape[0]
    chunk, nbuf = SC_GATHER_CHUNK, SC_GATHER_BUFS
    workers = SC_CORES * SC_SUBCORES
    per_w = total // workers
    nch = per_w // chunk
    assert total % (workers * chunk * nbuf) == 0
    mesh = plsc.VectorSubcoreMesh(core_axis_name="c", subcore_axis_name="s")

    @functools.partial(
        pl.kernel, mesh=mesh, out_type=jax.ShapeDtypeStruct((total, width), table.dtype),
        scratch_types=[pltpu.VMEM((nch, chunk), jnp.int32), pltpu.VMEM((nbuf, chunk, width), table.dtype),
                       pltpu.SemaphoreType.DMA((nbuf,)), pltpu.SemaphoreType.DMA((nbuf,))])
    def gather_kernel(table_hbm, idx_hbm, out_hbm, idx_v, rows_v, gsem, wsem):
        wid = lax.axis_index("s") * SC_CORES + lax.axis_index("c")
        pltpu.sync_copy(idx_hbm.at[wid], idx_v)

        def gather(j, slot):
            return pltpu.make_async_copy(table_hbm.at[idx_v.at[j]], rows_v.at[slot], gsem.at[slot])

        def write(j, slot):
            off = pl.multiple_of(wid * per_w + j * chunk, chunk)
            return pltpu.make_async_copy(rows_v.at[slot], out_hbm.at[pl.ds(off, chunk)], wsem.at[slot])

        for slot in range(nbuf):
            gather(slot, slot).start()

        @pl.loop(0, nch // nbuf)
        def _(g):
            for slot in range(nbuf):
                j = g * nbuf + slot
                gather(j, slot).wait()
                write(j, slot).start()
                write(j, slot).wait()

                @pl.when(j + nbuf < nch)
                def _():
                    gather(j + nbuf, slot).start()

    return gather_kernel(table, idx.reshape(workers, nch, chunk))


def _sc_scatter_rows(table, dest2, total):
    n, width = table.shape
    chunk, nbuf = SC_GATHER_CHUNK, SC_GATHER_BUFS
    workers = SC_CORES * SC_SUBCORES
    per_w = n // workers
    nch = per_w // chunk
    assert n % (workers * chunk * nbuf) == 0
    mesh = plsc.VectorSubcoreMesh(core_axis_name="c", subcore_axis_name="s")

    @functools.partial(
        pl.kernel, mesh=mesh, out_type=jax.ShapeDtypeStruct((total, width), table.dtype),
        scratch_types=[pltpu.VMEM((2, nch, chunk), jnp.int32), pltpu.VMEM((nbuf, chunk, width), table.dtype),
                       pltpu.SemaphoreType.DMA((nbuf,)), pltpu.SemaphoreType.DMA((nbuf,))])
    def scatter_kernel(table_hbm, idx_hbm, out_hbm, idx_v, rows_v, rsem, wsem):
        wid = lax.axis_index("s") * SC_CORES + lax.axis_index("c")
        pltpu.sync_copy(idx_hbm.at[wid], idx_v)

        def read(j, slot):
            off = pl.multiple_of(wid * per_w + j * chunk, chunk)
            return pltpu.make_async_copy(table_hbm.at[pl.ds(off, chunk)], rows_v.at[slot], rsem.at[slot])

        def write(j, slot, k):
            return pltpu.make_async_copy(rows_v.at[slot], out_hbm.at[idx_v.at[k, j]], wsem.at[slot])

        for slot in range(nbuf):
            read(slot, slot).start()

        @pl.loop(0, nch // nbuf)
        def _(g):
            for slot in range(nbuf):
                j = g * nbuf + slot
                read(j, slot).wait()
                write(j, slot, 0).start()
                write(j, slot, 1).start()
                write(j, slot, 0).wait()
                write(j, slot, 1).wait()

                @pl.when(j + nbuf < nch)
                def _():
                    read(j + nbuf, slot).start()

    idx = dest2.reshape(2, workers, nch, chunk).transpose(1, 0, 2, 3)
    return scatter_kernel(table, idx)


def _dispatch_plan(rt, cnt, n):
    ne, blk = MOE_N_EXPERTS, MOE_BLOCK
    n_blocks = (2 * n) // blk + ne
    experts = jnp.arange(ne, dtype=jnp.int32)
    counts = cnt[0, ROUTER_EXPERT_LANE:ROUTER_EXPERT_LANE + ne].astype(jnp.int32)
    padded = (counts + blk - 1) // blk * blk
    pend = jnp.cumsum(padded)
    pstart = pend - padded
    ids = rt[:, 0:2].astype(jnp.int32)
    pos = rt[:, 4:6].astype(jnp.int32)
    first_row = jnp.sum(jnp.where(ids[:, :, None] == experts[None, None, :], pstart[None, None, :], 0), axis=-1)
    dest2 = (first_row + pos).T
    b0 = jnp.arange(n_blocks, dtype=jnp.int32) * blk
    block_e = jnp.minimum(jnp.sum((pend[None, :] <= b0[:, None]).astype(jnp.int32), axis=1), ne - 1)
    n_valid = jnp.clip(counts[block_e] - (b0 - pstart[block_e]), 0, blk).astype(jnp.int32)
    return dest2, block_e, n_valid


def _expert_block_kernel(be_ref, nv_ref, xs_ref, wg_ref, wu_ref, wd_ref, o_ref, wgb_ref, wub_ref, wdb_ref):
    b = pl.program_id(0)
    valid = nv_ref[b]

    @pl.when((b == 0) | (be_ref[b] != be_ref[jnp.maximum(b - 1, 0)]))
    def _():
        wgb_ref[...] = wg_ref[0].astype(BF16)
        wub_ref[...] = wu_ref[0].astype(BF16)
        wdb_ref[...] = wd_ref[0].astype(BF16)

    @pl.when(valid > 0)
    def _():
        row = lax.broadcasted_iota(jnp.int32, xs_ref.shape, 0)
        words = jnp.where(row < valid, xs_ref[...], 0.0)
        h = _unpack_bf16_pairs(words).astype(BF16)
        gt = _dot(h, wgb_ref[...])
        hid = gt * _sigmoid(gt) * _dot(h, wub_ref[...])
        o_ref[...] = _pack_bf16_pairs(_dot(hid.astype(BF16), wdb_ref[...]))

    @pl.when(valid == 0)
    def _():
        o_ref[...] = jnp.zeros_like(o_ref)


def _expert_blocks(xs, wg, wu, wd, layer, block_e, n_used):
    p_rows, half = xs.shape
    d, hid = wg.shape[-2:]
    blk = MOE_BLOCK
    grid_spec = pltpu.PrefetchScalarGridSpec(
        num_scalar_prefetch=2, grid=(p_rows // blk,),
        in_specs=[pl.BlockSpec((blk, half), lambda b, be, nu: (b, 0)),
                  pl.BlockSpec((None, 1, d, hid), lambda b, be, nu: (layer, be[b], 0, 0)),
                  pl.BlockSpec((None, 1, d, hid), lambda b, be, nu: (layer, be[b], 0, 0)),
                  pl.BlockSpec((None, 1, hid, d), lambda b, be, nu: (layer, be[b], 0, 0))],
        out_specs=pl.BlockSpec((blk, half), lambda b, be, nu: (b, 0)),
        scratch_shapes=[pltpu.VMEM((d, hid), BF16), pltpu.VMEM((d, hid), BF16), pltpu.VMEM((hid, d), BF16)])
    return pl.pallas_call(
        _expert_block_kernel, out_shape=jax.ShapeDtypeStruct((p_rows, half), F32), grid_spec=grid_spec,
        compiler_params=_params(("arbitrary",)),
        name="moe_expert_blocks",
    )(block_e, n_used, xs, wg, wu, wd)


def _combine_kernel(x_ref, y1_ref, y2_ref, rt_ref, gfin_ref, o_ref, *, final_norm):
    out = (x_ref[...] + rt_ref[:, 2:3] * _unpack_bf16_pairs(y1_ref[...])
           + rt_ref[:, 3:4] * _unpack_bf16_pairs(y2_ref[...]))
    o_ref[...] = _rms(out, gfin_ref[...]) if final_norm else out


def _combine(x2, y_halves, rt, gfin, final_norm):
    n, d = x2.shape
    tm = COMBINE_TM
    nt = n // tm
    return pl.pallas_call(
        functools.partial(_combine_kernel, final_norm=final_norm),
        out_shape=jax.ShapeDtypeStruct((n, d), F32),
        grid=(nt,),
        in_specs=[pl.BlockSpec((tm, d), lambda i: (i, 0)),
                  pl.BlockSpec((tm, d // 2), lambda i: (i, 0)),
                  pl.BlockSpec((tm, d // 2), lambda i: (i + nt, 0)),
                  pl.BlockSpec((tm, LANES), lambda i: (i, 0)), _const_spec((1, d))],
        out_specs=pl.BlockSpec((tm, d), lambda i: (i, 0)),
        compiler_params=_params(("parallel",)), name="moe_combine",
    )(x2, y_halves, y_halves, rt, gfin)


def _moe(x2, hp, rt, cnt, wg, wu, wd, layer, gfin, final_norm):
    n = x2.shape[0]
    dest2, block_e, n_valid = _dispatch_plan(rt, cnt, n)
    xs = _sc_scatter_rows(hp, dest2, block_e.shape[0] * MOE_BLOCK)
    yb = _expert_blocks(xs, wg, wu, wd, layer, block_e, n_valid)
    y_halves = _sc_gather_rows(yb, dest2.reshape(2 * n))
    return _combine(x2, y_halves, rt, gfin, final_norm)


def kernel(x, mem, positions, norm_mix, w_in, diff_lambda, hgrn_lb_logits, spatial_w, spatial_b, w_branch, w_out,
           norm_mem_q, norm_mem_kv, w_mem_q, w_mem_kv, w_mem_o, norm_ffn, w_router_group, b_router_group,
           w_router_expert, b_router_expert, w_exp_gate, w_exp_up, w_exp_down, norm_final):
    batch, seq, d = x.shape
    depth = w_in.shape[0]
    n = batch * seq
    xf = x.reshape(n, d)
    tabs = _rope_tables(positions)
    row = lambda v: v.reshape(1, -1).astype(F32)
    for l in range(depth):
        lam_init = 0.8 - 0.6 * math.exp(-0.3 * l)
        w1, w_gate = _split_w_in(w_in, l)
        sw = spatial_w[l].reshape(SGU_GROUPS * SGU_CHUNK, SGU_CHUNK)
        sb = jnp.repeat(spatial_b[l].T, SGU_GROUP_DIM, axis=1)
        qat, ka, vat, hb, y_c, qdt, iqt, dkv, dkvt, ikw, iwt = _projection(
            xf, row(norm_mix[l]), w1, tabs, sw, sb, batch, seq)
        y_a = _diff_attention(diff_lambda[l], qat, ka, vat, lam_init, batch, seq)
        y_b = _hgrn(hgrn_lb_logits, hb, l, batch, seq)
        y_d = _dsa(qdt, iqt, iwt, dkv, dkvt, ikw, batch, seq)
        mkt, mv = _mem_kv(mem, row(norm_mem_kv[l]), w_mem_kv[l].astype(BF16))
        e0, e1 = ROUTER_EXPERT_LANE, ROUTER_EXPERT_LANE + MOE_N_EXPERTS
        wr = jnp.zeros((d, LANES), F32)
        wr = wr.at[:, :MOE_GROUPS].set(w_router_group[l]).at[:, e0:e1].set(w_router_expert[l]).astype(BF16)
        br = jnp.zeros((1, LANES), F32)
        br = br.at[0, :MOE_GROUPS].set(b_router_group[l]).at[0, e0:e1].set(b_router_expert[l])
        x2, hp, rt, cnt = _merge(xf, (y_a, y_b, y_c, y_d), row(norm_mix[l]), w_gate, w_branch[l].astype(BF16),
                                 w_out[l].astype(BF16), row(norm_mem_q[l]), w_mem_q[l].astype(BF16), mkt, mv,
                                 w_mem_o[l].astype(BF16), row(norm_ffn[l]), wr, br, batch, seq)
        xf = _moe(x2, hp, rt, cnt, w_exp_gate, w_exp_up, w_exp_down, l,
                  row(norm_final), final_norm=(l == depth - 1))
    return xf.reshape(batch, seq, d)
```

```python
import functools
import math

import numpy as np
import jax
import jax.numpy as jnp
from jax import lax
from jax.experimental import pallas as pl
from jax.experimental.pallas import tpu as pltpu
from jax.experimental.pallas import tpu_sc as plsc

F32 = jnp.float32
BF16 = jnp.bfloat16

NORM_EPS = 1e-6
ROPE_THETA = 10000.0
NEG_BIG = -1e30

N_BRANCH = 4
DIFF_HEADS = 4
DIFF_HEAD_DIM = 32
HGRN_DIM = 64
HGRN_CHUNK = 32
HGRN_UNROLL = 8
HGRN_MIN_FORGET = 1e-30
SGU_GROUPS = 4
SGU_GROUP_DIM = 64
SGU_CHUNK = 128
DSA_HEADS = 4
DSA_HEAD_DIM = 64
DSA_IDX_HEADS = 4
DSA_IDX_DIM = 32
DSA_TOPK = 256
MEM_HEADS = 4
MEM_HEAD_DIM = 64
MOE_GROUPS = 4
MOE_EXPERTS_PER_GROUP = 8
MOE_N_EXPERTS = 32
MOE_BLOCK = 512
ROUTER_EXPERT_LANE = 32
SC_CORES = 2
SC_SUBCORES = 16
SC_GATHER_CHUNK = 16
SC_GATHER_BUFS = 4

LANES = 128
VMEM_LIMIT = 56 * 1024 * 1024

PROJ_TM = 1024
DIFF_TQ = 256
DIFF_TK = 512
HGRN_TC = 512
DSA_TQ = 256
DSA_TK = 512
MERGE_TM = 1024
COMBINE_TM = 512
ROPE_TM = 1024

C_AQ, C_AK, C_AV = 0, 256, 512
C_HB = 768
C_UV = 1792
C_DQ = 2304
C_DKV = 2560
C_IQ = 2688
C_IKW = 2816
IW_LANE = 32
C_GATE = 2852
C_TOTAL = 2944
LOG2E = math.log2(math.e)


def _params(sem):
    return pltpu.CompilerParams(dimension_semantics=sem, vmem_limit_bytes=VMEM_LIMIT)


def _const_spec(shape):
    nd = len(shape)
    return pl.BlockSpec(shape, lambda *_: (0,) * nd, pipeline_mode=pl.Buffered(1))


def _rms(xf, gain=None):
    y = xf * lax.rsqrt(jnp.mean(xf * xf, axis=-1, keepdims=True) + NORM_EPS)
    return y if gain is None else y * gain


def _sigmoid(x):
    return 0.5 * jnp.tanh(0.5 * x) + 0.5


def _pack_bf16_pairs(x):
    w = x.shape[-1] // 2
    xb = x.astype(BF16).astype(F32)
    lo = lax.shift_right_logical(pltpu.bitcast(xb[:, :w], jnp.int32), 16)
    hi = pltpu.bitcast(xb[:, w:], jnp.int32) & jnp.int32(-65536)
    return pltpu.bitcast(hi | lo, F32)


def _unpack_bf16_pairs(words):
    bits = pltpu.bitcast(words, jnp.int32)
    lo = pltpu.bitcast(bits << 16, F32)
    hi = pltpu.bitcast(bits & jnp.int32(-65536), F32)
    return jnp.concatenate([lo, hi], axis=1)


def _dot(a, b):
    return jnp.dot(a, b, preferred_element_type=F32)


def _dot_nt(a, b):
    return lax.dot_general(a, b, (((1,), (1,)), ((), ())), preferred_element_type=F32)


def _rope_table_kernel(pos_ref, frq_ref, sgn_ref, c32_ref, s32_ref, c64_ref, s64_ref):
    pos = pos_ref[...].astype(F32)
    twice = lambda t: jnp.concatenate([t, t], axis=1)
    a32 = pos * frq_ref[0:1, :LANES]
    a64 = pos * frq_ref[1:2, :LANES]
    c32_ref[...] = twice(jnp.cos(a32))
    s32_ref[...] = twice(jnp.sin(a32) * sgn_ref[0:1, :LANES])
    c64_ref[...] = twice(jnp.cos(a64))
    s64_ref[...] = twice(jnp.sin(a64) * sgn_ref[1:2, :LANES])


def _rope_tables(positions):
    n = positions.size
    pos = positions.reshape(n, 1).astype(jnp.int32)
    lane = np.arange(256)
    inv32 = ROPE_THETA ** (-jnp.arange(16, dtype=F32) * (2.0 / 32))
    inv64 = ROPE_THETA ** (-jnp.arange(32, dtype=F32) * (2.0 / 64))
    frq = jnp.stack([inv32[lane % 16], inv64[lane % 32]])
    sgn = jnp.asarray(np.stack([np.where(lane % 32 < 16, -1.0, 1.0),
                                np.where(lane % 64 < 32, -1.0, 1.0)]), F32)
    tm = ROPE_TM
    tab = jax.ShapeDtypeStruct((n, 256), F32)
    return pl.pallas_call(
        _rope_table_kernel,
        out_shape=(tab, tab, tab, tab),
        grid=(n // tm,),
        in_specs=[pl.BlockSpec((tm, 1), lambda i: (i, 0)), _const_spec((2, 256)), _const_spec((2, 256))],
        out_specs=tuple(pl.BlockSpec((tm, 256), lambda i: (i, 0)) for _ in range(4)),
        compiler_params=_params(("parallel",)),
        name="rope_tables",
    )(pos, frq, sgn)


def _w1_kernel(wt_ref, o_ref, *, layer):
    o_ref[...] = wt_ref[:, layer, :].T.astype(BF16)


def _wgate_kernel(a_ref, b_ref, o_ref, *, layer):
    off = C_GATE % LANES
    o_ref[...] = jnp.concatenate([a_ref[off:, layer, :], b_ref[:off, layer, :]], axis=0).T.astype(BF16)


def _split_w_in(w_in, layer):
    depth, d, width = w_in.shape
    wt = jnp.transpose(w_in, (2, 0, 1))
    rows = lambda f: pl.BlockSpec((LANES, depth, d), lambda i: (f(i), 0, 0))
    cols = pl.BlockSpec((d, LANES), lambda i: (0, i))
    w1 = pl.pallas_call(
        functools.partial(_w1_kernel, layer=layer), out_shape=jax.ShapeDtypeStruct((d, C_TOTAL), BF16),
        grid=(C_TOTAL // LANES,), in_specs=[rows(lambda i: i)], out_specs=cols,
        compiler_params=_params(("parallel",)), name="split_w_in",
    )(wt)
    a0 = C_GATE // LANES
    gate_w = width - C_GATE
    wg = pl.pallas_call(
        functools.partial(_wgate_kernel, layer=layer), out_shape=jax.ShapeDtypeStruct((d, gate_w), BF16),
        grid=(gate_w // LANES,), in_specs=[rows(lambda i: a0 + i), rows(lambda i: a0 + i + 1)], out_specs=cols,
        compiler_params=_params(("parallel",)), name="split_w_gate",
    )(wt, wt)
    return w1, wg


def _gelu_tanh(x):
    return 0.5 * x * (1.0 + jnp.tanh(math.sqrt(2.0 / math.pi) * (x + 0.044715 * (x * x * x))))


def _rope(x, cos, sin_signed, half):
    w = x.shape[-1]
    lane = lax.broadcasted_iota(jnp.int32, x.shape, 1)
    partner = jnp.where(lane % (2 * half) < half, pltpu.roll(x, w - half, 1), pltpu.roll(x, half, 1))
    return x * cos + partner * sin_signed


def _proj_kernel(x_ref, g_ref, w_ref, c32_ref, s32_ref, c64_ref, s64_ref, sw_ref, sb_ref,
                 qat_ref, ka_ref, vat_ref, hb_ref, yc_ref, qdt_ref, iqt_ref, dkv_ref, dkvt_ref, ikw_ref, iwt_ref,
                 *, tm):
    h = _rms(x_ref[...], g_ref[...]).astype(BF16)

    def proj(c0, width):
        return _dot(h, w_ref[:, c0:c0 + width])

    c32, s32, c64, s64 = c32_ref[...], s32_ref[...], c64_ref[...], s64_ref[...]
    qat_ref[0] = (_rope(proj(C_AQ, 256), c32, s32, 16) * (DIFF_HEAD_DIM ** -0.5 * LOG2E)).T.astype(BF16)
    ka_ref[...] = _rope(proj(C_AK, 256), c32, s32, 16).astype(BF16)
    vat_ref[0] = proj(C_AV, 256).astype(BF16).T
    hb_ref[...] = proj(C_HB, 1024)
    qdt_ref[0] = (_rope(proj(C_DQ, 256), c64, s64, 32) * (DSA_HEAD_DIM ** -0.5 * LOG2E)).T.astype(BF16)
    iqt_ref[0] = _rope(proj(C_IQ, 128), c32[:, :128], s32[:, :128], 16).T.astype(BF16)
    lane = lax.broadcasted_iota(jnp.int32, (tm, 128), 1)
    is_k = lane < DSA_HEAD_DIM
    dkv = _rope(proj(C_DKV, 128), jnp.where(is_k, c64[:, :128], 1.0), jnp.where(is_k, s64[:, :128], 0.0), 32)
    is_ik = lane < DSA_IDX_DIM
    ikw = _rope(proj(C_IKW, 128), jnp.where(is_ik, c32[:, :128], 1.0), jnp.where(is_ik, s32[:, :128], 0.0), 16)
    dkv_ref[...] = dkv.astype(BF16)
    dkvt_ref[0] = dkv.T.astype(BF16)
    ikw_ref[...] = ikw.astype(BF16)
    iw_scale = DSA_IDX_HEADS ** -0.5 * DSA_IDX_DIM ** -0.5
    iwt_ref[0] = (ikw * iw_scale).T[IW_LANE:IW_LANE + 8, :]
    uv = _gelu_tanh(proj(C_UV, 512))
    u, v = uv[:, :256], uv[:, 256:]
    mu = jnp.mean(v, axis=-1, keepdims=True)
    vc = v - mu
    vn = (vc * lax.rsqrt(jnp.mean(vc * vc, axis=-1, keepdims=True) + NORM_EPS)).astype(BF16)
    r = lax.broadcasted_iota(jnp.int32, (SGU_GROUPS * SGU_CHUNK, SGU_CHUNK), 0)
    c = lax.broadcasted_iota(jnp.int32, (SGU_GROUPS * SGU_CHUNK, SGU_CHUNK), 1)
    wt = jnp.where((r % SGU_CHUNK) >= c, sw_ref[...], 0.0).astype(BF16)
    lane_grp = lax.broadcasted_iota(jnp.int32, (SGU_CHUNK, 256), 1) // SGU_GROUP_DIM
    for ch in range(tm // SGU_CHUNK):
        r0 = ch * SGU_CHUNK
        full = _dot(wt, vn[r0:r0 + SGU_CHUNK, :])
        mixed = sb_ref[...]
        for g in range(SGU_GROUPS):
            mixed = mixed + jnp.where(lane_grp == g, full[g * SGU_CHUNK:(g + 1) * SGU_CHUNK, :], 0.0)
        yc_ref[r0:r0 + SGU_CHUNK, :] = (u[r0:r0 + SGU_CHUNK, :] * mixed).astype(BF16)


def _projection(x, gain, w1, tabs, sw, sb, batch, seq):
    n, d = x.shape
    tm = PROJ_TM
    spt = seq // tm
    tok = lambda w: pl.BlockSpec((tm, w), lambda i: (i, 0))
    tr = lambda rows: pl.BlockSpec((1, rows, tm), lambda i: (i // spt, 0, i % spt))
    out_shape = (
        jax.ShapeDtypeStruct((batch, 256, seq), BF16),
        jax.ShapeDtypeStruct((n, 256), BF16),
        jax.ShapeDtypeStruct((batch, 256, seq), BF16),
        jax.ShapeDtypeStruct((n, 1024), F32),
        jax.ShapeDtypeStruct((n, 256), BF16),
        jax.ShapeDtypeStruct((batch, 256, seq), BF16),
        jax.ShapeDtypeStruct((batch, 128, seq), BF16),
        jax.ShapeDtypeStruct((n, 128), BF16),
        jax.ShapeDtypeStruct((batch, 128, seq), BF16),
        jax.ShapeDtypeStruct((n, 128), BF16),
        jax.ShapeDtypeStruct((batch, 8, seq), F32),
    )
    return pl.pallas_call(
        functools.partial(_proj_kernel, tm=tm),
        out_shape=out_shape,
        grid=(n // tm,),
        in_specs=[tok(d), _const_spec((1, d)), _const_spec((d, C_TOTAL)),
                  tok(256), tok(256), tok(256), tok(256),
                  _const_spec((SGU_GROUPS * SGU_CHUNK, SGU_CHUNK)), _const_spec((SGU_CHUNK, 256))],
        out_specs=(tr(256), tok(256), tr(256), tok(1024), tok(256), tr(256), tr(128), tok(128), tr(128), tok(128), tr(8)),
        compiler_params=_params(("parallel",)),
        name="projection",
    )(x, gain, w1, *tabs, sw, sb)


def _diff_attn_kernel(lam_ref, qt_ref, k_ref, vt_ref, o_ref, sa_ref, sb_ref, *, lam_init, tq, tk):
    q0 = pl.program_id(1) * tq
    kb_diag = q0 // tk
    lv = lam_ref[...]
    lam = (jnp.exp(jnp.sum(lv[0:1] * lv[1:2], axis=-1, keepdims=True))
           - jnp.exp(jnp.sum(lv[2:3] * lv[3:4], axis=-1, keepdims=True)) + lam_init)
    qt = qt_ref[0]
    feat = lax.broadcasted_iota(jnp.int32, (256, tq), 0) // DIFF_HEAD_DIM
    n_maps = 2 * DIFF_HEADS
    qz = jnp.concatenate([jnp.where(feat == i, qt, jnp.zeros_like(qt)) for i in range(n_maps)], axis=1)
    wide = n_maps * tq
    key_i = lax.broadcasted_iota(jnp.int32, (tk, wide), 0)
    qry_i = q0 + lax.broadcasted_iota(jnp.int32, (tk, wide), 1) % tq

    def scores(kb, dst_ref):
        k0 = pl.multiple_of(kb * tk, tk)
        dst_ref[...] = _dot(k_ref[pl.ds(k0, tk), :], qz)

    def absorb(src_ref, kb, carry, masked):
        m_i, l_i, acc = carry
        k0 = pl.multiple_of(kb * tk, tk)
        s = src_ref[...]
        if masked:
            s = jnp.where(k0 + key_i <= qry_i, s, NEG_BIG)
        m_new = jnp.maximum(m_i, jnp.max(s, axis=0, keepdims=True))
        p = jnp.exp2(s - m_new)
        alpha = jnp.exp2(m_i - m_new)
        l_new = alpha * l_i + jnp.sum(p, axis=0, keepdims=True)
        pb = p.astype(BF16)
        pv = jnp.concatenate(
            [_dot(vt_ref[0, hd * 64:(hd + 1) * 64, pl.ds(k0, tk)], pb[:, 2 * hd * tq:(2 * hd + 2) * tq])
             for hd in range(DIFF_HEADS)], axis=1)
        return m_new, l_new, alpha * acc + pv

    def pair(j, carry):
        kb = 2 * j
        scores(kb + 1, sb_ref)
        carry = absorb(sa_ref, kb, carry, False)
        scores(kb + 2, sa_ref)
        return absorb(sb_ref, kb + 1, carry, False)

    init = (jnp.full((1, wide), NEG_BIG, F32), jnp.zeros((1, wide), F32), jnp.zeros((64, wide), F32))
    scores(0, sa_ref)
    carry = lax.fori_loop(0, kb_diag // 2, pair, init)

    def tail_odd(carry):
        scores(kb_diag, sb_ref)
        carry = absorb(sa_ref, kb_diag - 1, carry, False)
        return absorb(sb_ref, kb_diag, carry, True)

    def tail_even(carry):
        return absorb(sa_ref, kb_diag, carry, True)

    _, l_f, acc = lax.cond(kb_diag % 2 == 1, tail_odd, tail_even, carry)
    o_all = acc / l_f
    heads = []
    for hd in range(DIFF_HEADS):
        o0 = o_all[:, 2 * hd * tq:(2 * hd + 1) * tq]
        o1 = o_all[:, (2 * hd + 1) * tq:(2 * hd + 2) * tq]
        o_h = o0 - lam * o1
        ms = jnp.mean(o_h * o_h, axis=0, keepdims=True)
        heads.append(o_h * lax.rsqrt(ms + NORM_EPS) * (1.0 - lam_init))
    o_ref[...] = jnp.concatenate(heads, axis=0).T.astype(BF16)


def _diff_attention(lam_vec, qat, ka, vat, lam_init, batch, seq):
    tq, tk = DIFF_TQ, DIFF_TK
    nq = seq // tq
    return pl.pallas_call(
        functools.partial(_diff_attn_kernel, lam_init=lam_init, tq=tq, tk=tk),
        out_shape=jax.ShapeDtypeStruct((batch * seq, 256), BF16),
        grid=(batch, nq),
        in_specs=[_const_spec((4, DIFF_HEAD_DIM)),
                  pl.BlockSpec((1, 256, tq), lambda b, i: (b, 0, i)),
                  pl.BlockSpec((seq, 256), lambda b, i: (b, 0)),
                  pl.BlockSpec((1, 256, seq), lambda b, i: (b, 0, 0))],
        out_specs=pl.BlockSpec((tq, 256), lambda b, i: (b * nq + i, 0)),
        scratch_shapes=[pltpu.VMEM((tk, 2 * DIFF_HEADS * tq), F32), pltpu.VMEM((tk, 2 * DIFF_HEADS * tq), F32)],
        compiler_params=_params(("parallel", "parallel")),
        name="diff_attention",
    )(lam_vec, qat, ka, vat)


def _hgrn_kernel(lbl_ref, hb_ref, o_ref, st_ref, pstk_ref, *, layer, tc):
    cz = HGRN_CHUNK
    w = 256

    @pl.when(pl.program_id(1) == 0)
    def _():
        st_ref[...] = jnp.zeros_like(st_ref)

    lg = lbl_ref[...]
    e = jnp.exp(lg - jnp.max(lg, axis=0, keepdims=True))
    lw = e / jnp.sum(e, axis=0, keepdims=True)
    lb = jnp.sum(lw[0:layer + 1], axis=0, keepdims=True) - lw[0:1]

    ri = lax.broadcasted_iota(jnp.int32, (cz, cz), 0)
    ci = lax.broadcasted_iota(jnp.int32, (cz, cz), 1)
    tri = (ri >= ci).astype(F32)
    rb = lax.broadcasted_iota(jnp.int32, (w, w), 0) // HGRN_DIM
    cb = lax.broadcasted_iota(jnp.int32, (w, w), 1) // HGRN_DIM
    same_head = rb == cb
    head_ones = same_head.astype(BF16)
    trows = {r: r + lax.broadcasted_iota(jnp.int32, (cz - r, w), 0) for r in range(0, cz, 16)}

    def chunk(c, carry):
        r0 = pl.multiple_of(c * cz, cz)
        q = hb_ref[pl.ds(r0, cz), 0:256]
        fp = hb_ref[pl.ds(r0, cz), 256:512]
        v = hb_ref[pl.ds(r0, cz), 512:768]
        g = hb_ref[pl.ds(r0, cz), 768:1024]
        qf = q * _sigmoid(q)
        f = lb + (1.0 - lb) * jax.nn.sigmoid(fp)
        log_f = jnp.log(jnp.maximum(f, HGRN_MIN_FORGET))
        kf = (1.0 - lb) * jax.nn.sigmoid(-fp)
        bc = jnp.dot(tri, log_f, preferred_element_type=F32, precision=lax.Precision.HIGHEST)
        st = st_ref[...]
        o = _dot_nt((qf * jnp.exp(bc)).astype(BF16), st.astype(BF16))
        for s in range(cz):
            r_lo = (s // 16) * 16
            arg = bc[r_lo:, :] - bc[s:s + 1, :]
            if s > r_lo:
                arg = jnp.where(trows[r_lo] >= s, arg, NEG_BIG)
            p = qf[r_lo:, :] * kf[s:s + 1, :] * jnp.exp(arg)
            if r_lo:
                pstk_ref[s * cz:s * cz + r_lo, :] = jnp.zeros((r_lo, w), BF16)
            pstk_ref[s * cz + r_lo:(s + 1) * cz, :] = p.astype(BF16)
        accs = [jnp.zeros((16, w), F32) for _ in range(cz // 16)]
        for sg in range(cz // 16):
            att = _dot(pstk_ref[sg * 16 * cz:(sg + 1) * 16 * cz, :], head_ones)
            for sl in range(16):
                s = sg * 16 + sl
                for j in range(sg, cz // 16):
                    accs[j] = accs[j] + att[sl * cz + 16 * j:sl * cz + 16 * j + 16, :] * v[s:s + 1, :]
        o = o + jnp.concatenate(accs, axis=0)
        b_end = bc[cz - 1:cz, :]
        kd = kf * jnp.exp(b_end - bc)
        upd = _dot(v.T.astype(BF16), kd.astype(BF16))
        st_ref[...] = st * jnp.exp(b_end) + jnp.where(same_head, upd, 0.0)
        ms = _dot(o * o, head_ones.astype(F32)) * (1.0 / HGRN_DIM)
        y = o * lax.rsqrt(ms + NORM_EPS)
        o_ref[pl.ds(r0, cz), :] = (y * (g * _sigmoid(g))).astype(BF16)
        return carry

    def group(gi, carry):
        for u in range(HGRN_UNROLL):
            chunk(gi * HGRN_UNROLL + u, carry)
        return carry

    lax.fori_loop(0, tc // cz // HGRN_UNROLL, group, 0)


def _hgrn(lb_logits, hb, layer, batch, seq):
    tc = HGRN_TC
    nt = seq // tc
    cz = HGRN_CHUNK
    return pl.pallas_call(
        functools.partial(_hgrn_kernel, layer=layer, tc=tc),
        out_shape=jax.ShapeDtypeStruct((batch * seq, 256), BF16),
        grid=(batch, nt),
        in_specs=[_const_spec(lb_logits.shape),
                  pl.BlockSpec((tc, 1024), lambda b, i: (b * nt + i, 0))],
        out_specs=pl.BlockSpec((tc, 256), lambda b, i: (b * nt + i, 0)),
        scratch_shapes=[pltpu.VMEM((256, 256), F32), pltpu.VMEM((cz * cz, 256), BF16)],
        compiler_params=_params(("parallel", "arbitrary")),
        name="hgrn2",
    )(lb_logits, hb)


def _dsa_kernel(qdt_ref, iqt_ref, iwt_ref, dkv_ref, dkvt_ref, ikw_ref, o_ref, key_ref, bias_ref, half_ref,
                sa_ref, sb_ref, *, tq, tk, n_sel):
    q0 = pl.program_id(1) * tq
    nkb = q0 // tk + 1
    key_i = lax.broadcasted_iota(jnp.int32, (tk, tq), 0)
    qry_i = q0 + lax.broadcasted_iota(jnp.int32, (tk, tq), 1)
    grp = tk // 8
    rows8 = lambda x: x.reshape(grp, 8, tq)
    iqt = iqt_ref[0]
    zpad = jnp.zeros((LANES - DSA_IDX_DIM, tq), BF16)
    iqz = jnp.concatenate([jnp.concatenate([iqt[hd * DSA_IDX_DIM:(hd + 1) * DSA_IDX_DIM, :], zpad], axis=0)
                           for hd in range(DSA_IDX_HEADS)], axis=1)
    iw = iwt_ref[0]

    def score_block(kb, carry):
        k0 = pl.multiple_of(kb * tk, tk)
        sh = jnp.maximum(_dot(ikw_ref[pl.ds(k0, tk), :], iqz), 0.0)
        sc = jnp.zeros((tk, tq), F32)
        for hd in range(DSA_IDX_HEADS):
            sc = sc + sh[:, hd * tq:(hd + 1) * tq] * iw[hd:hd + 1, :]
        sc = jnp.where(k0 + key_i <= qry_i, sc + 0.0, -jnp.inf)
        bits = pltpu.bitcast(sc, jnp.int32)
        key = jnp.where(bits < 0, bits ^ jnp.int32(0x7FFFFFFF), bits)
        key_ref[pl.ds(k0, tk), :] = key
        half_ref[pl.ds(k0, tk), :] = (key >> 16).astype(jnp.int16)
        return carry

    lax.fori_loop(0, nkb, score_block, 0)

    one16, zero16 = jnp.ones((), jnp.int16), jnp.zeros((), jnp.int16)
    low16 = np.int16(-2 ** 15)

    def count16(limit, strict):
        def body(kb, acc):
            k0 = pl.multiple_of(kb * tk, tk)
            for c in range(tk // 128):
                blk = half_ref[pl.ds(k0 + 128 * c, 128), :].reshape(8, 16, tq)
                hit = jnp.where(blk > limit if strict else blk >= limit, one16, zero16)
                parts = [hit[j] for j in range(8)]
                while len(parts) > 1:
                    parts = [a + b for a, b in zip(parts[0::2], parts[1::2])]
                acc = acc + parts[0]
            return acc
        acc = lax.fori_loop(0, nkb, body, jnp.zeros((16, tq), jnp.int16))
        return jnp.broadcast_to(jnp.sum(acc.astype(jnp.int32), axis=0, keepdims=True), (16, tq))

    def search16(need):
        t = jnp.full((16, tq), -2 ** 15, jnp.int32)
        for bit in range(15, -1, -1):
            trial = t + 2 ** bit
            t = jnp.where(count16(trial.astype(jnp.int16), False) >= need, trial, t)
        return t

    t_hi = search16(n_sel)
    t_hi16 = t_hi.astype(jnp.int16)
    need_lo = n_sel - count16(t_hi16, True)

    def low_block(kb, carry):
        k0 = pl.multiple_of(kb * tk, tk)
        lo = ((key_ref[pl.ds(k0, tk), :] & 0xFFFF) - 2 ** 15).astype(jnp.int16).reshape(tk // 16, 16, tq)
        hi = half_ref[pl.ds(k0, tk), :].reshape(tk // 16, 16, tq)
        half_ref[pl.ds(k0, tk), :] = jnp.where(hi == t_hi16, lo, low16).reshape(tk, tq)
        return carry

    lax.fori_loop(0, nkb, low_block, 0)
    t_lo = search16(need_lo)
    thr = ((t_hi << 16) | (t_lo + 2 ** 15))[0:8, :]

    need = (need_lo - count16(t_lo.astype(jnp.int16), True))[0:8, :].astype(F32)
    ur = lax.broadcasted_iota(jnp.int32, (tk, tk), 0)
    uc = lax.broadcasted_iota(jnp.int32, (tk, tk), 1)
    earlier = (uc < ur).astype(BF16)
    ones8 = jnp.ones((8, tk), BF16)

    def select_block(kb, seen):
        k0 = pl.multiple_of(kb * tk, tk)
        blk = rows8(key_ref[pl.ds(k0, tk), :])
        eq = blk == thr[None]
        eqb = jnp.where(eq, 1.0, 0.0).reshape(tk, tq).astype(BF16)
        rank = rows8(_dot(earlier, eqb)) + seen[None]
        sel = (blk > thr[None]) | (eq & (rank < need[None]))
        bias = jnp.where(sel, 0.0, NEG_BIG).reshape(tk, tq)
        bias_ref[pl.ds(k0, tk), :] = jnp.where(k0 + key_i <= qry_i, bias, NEG_BIG)
        return seen + _dot(ones8, eqb)

    lax.fori_loop(0, nkb, select_block, jnp.zeros((8, tq), F32))

    qdt = qdt_ref[0]
    zq = jnp.zeros((LANES - DSA_HEAD_DIM, tq), BF16)
    qz = jnp.concatenate([jnp.concatenate([qdt[hd * DSA_HEAD_DIM:(hd + 1) * DSA_HEAD_DIM, :], zq], axis=0)
                          for hd in range(DSA_HEADS)], axis=1)
    wide = DSA_HEADS * tq

    def scores(kb, dst_ref):
        k0 = pl.multiple_of(kb * tk, tk)
        dst_ref[...] = _dot(dkv_ref[pl.ds(k0, tk), :], qz)

    def absorb(src_ref, kb, carry):
        m_i, l_i, acc = carry
        k0 = pl.multiple_of(kb * tk, tk)
        bias = bias_ref[pl.ds(k0, tk), :]
        s = src_ref[...] + jnp.concatenate([bias] * DSA_HEADS, axis=1)
        m_new = jnp.maximum(m_i, jnp.max(s, axis=0, keepdims=True))
        p = jnp.exp2(s - m_new)
        alpha = jnp.exp2(m_i - m_new)
        l_new = alpha * l_i + jnp.sum(p, axis=0, keepdims=True)
        pv = _dot(dkvt_ref[0, DSA_HEAD_DIM:, pl.ds(k0, tk)], p.astype(BF16))
        return m_new, l_new, alpha * acc + pv

    def pair(j, carry):
        kb = 2 * j
        scores(kb + 1, sb_ref)
        carry = absorb(sa_ref, kb, carry)
        scores(kb + 2, sa_ref)
        return absorb(sb_ref, kb + 1, carry)

    init = (jnp.full((1, wide), NEG_BIG, F32), jnp.zeros((1, wide), F32), jnp.zeros((DSA_HEAD_DIM, wide), F32))
    last = nkb - 1
    scores(0, sa_ref)
    carry = lax.fori_loop(0, last // 2, pair, init)

    def tail_odd(carry):
        scores(last, sb_ref)
        return absorb(sb_ref, last, absorb(sa_ref, last - 1, carry))

    def tail_even(carry):
        return absorb(sa_ref, last, carry)

    _, l_f, acc = lax.cond(last % 2 == 1, tail_odd, tail_even, carry)
    o_all = acc / l_f
    o_ref[...] = jnp.concatenate([o_all[:, hd * tq:(hd + 1) * tq] for hd in range(DSA_HEADS)],
                                 axis=0).T.astype(BF16)


def _dsa(qdt, iqt, iwt, dkv, dkvt, ikw, batch, seq):
    tq, tk = DSA_TQ, DSA_TK
    nq = seq // tq
    n_sel = min(DSA_TOPK, seq // 4)
    return pl.pallas_call(
        functools.partial(_dsa_kernel, tq=tq, tk=tk, n_sel=n_sel),
        out_shape=jax.ShapeDtypeStruct((batch * seq, 256), BF16),
        grid=(batch, nq),
        in_specs=[pl.BlockSpec((1, 256, tq), lambda b, i: (b, 0, i)),
                  pl.BlockSpec((1, 128, tq), lambda b, i: (b, 0, i)),
                  pl.BlockSpec((1, 8, tq), lambda b, i: (b, 0, i)),
                  pl.BlockSpec((seq, 128), lambda b, i: (b, 0)),
                  pl.BlockSpec((1, 128, seq), lambda b, i: (b, 0, 0)),
                  pl.BlockSpec((seq, 128), lambda b, i: (b, 0))],
        out_specs=pl.BlockSpec((tq, 256), lambda b, i: (b * nq + i, 0)),
        scratch_shapes=[pltpu.VMEM((seq, tq), jnp.int32), pltpu.VMEM((seq, tq), F32),
                        pltpu.VMEM((seq, tq), jnp.int16),
                        pltpu.VMEM((tk, DSA_HEADS * tq), F32), pltpu.VMEM((tk, DSA_HEADS * tq), F32)],
        compiler_params=_params(("parallel", "parallel")),
        name="dsa",
    )(qdt, iqt, iwt, dkv, dkvt, ikw)


def _mem_kv_kernel(mem_ref, g_ref, w_ref, kt_ref, v_ref):
    mn = _rms(mem_ref[0], g_ref[...]).astype(BF16)
    kv = _dot(mn, w_ref[...])
    kt_ref[0] = kv[:, :256].T.astype(BF16)
    v_ref[0] = kv[:, 256:].astype(BF16)


def _mem_kv(mem, gain, w_kv):
    b, m, d = mem.shape
    return pl.pallas_call(
        _mem_kv_kernel,
        out_shape=(jax.ShapeDtypeStruct((b, 256, m), BF16), jax.ShapeDtypeStruct((b, m, 256), BF16)),
        grid=(b,),
        in_specs=[pl.BlockSpec((1, m, d), lambda i: (i, 0, 0)), _const_spec((1, d)), _const_spec((d, 512))],
        out_specs=(pl.BlockSpec((1, 256, m), lambda i: (i, 0, 0)), pl.BlockSpec((1, m, 256), lambda i: (i, 0, 0))),
        compiler_params=_params(("parallel",)),
        name="mem_kv",
    )(mem, gain, w_kv)


def _merge_kernel(x_ref, ya_ref, yb_ref, yc_ref, yd_ref, gmix_ref, wg_ref, wbr_ref, wout_ref,
                  gq_ref, wq_ref, mkt_ref, mv_ref, wo_ref, gffn_ref, wr_ref, br_ref,
                  x2_ref, hp_ref, rt_ref, cnt_ref, run_ref, *, tm):
    x = x_ref[...]
    d = x.shape[-1]
    h = _rms(x, gmix_ref[...]).astype(BF16)
    merged = jnp.zeros((tm, d), F32)
    for n, y_ref in enumerate((ya_ref, yb_ref, yc_ref, yd_ref)):
        gate = _sigmoid(_dot(h, wg_ref[:, n * d:(n + 1) * d]))
        merged = merged + gate * _dot(y_ref[...], wbr_ref[n])
    x1 = x + _dot(merged.astype(BF16), wout_ref[...])
    h2 = _rms(x1, gq_ref[...]).astype(BF16)
    q = (_dot(h2, wq_ref[...]) * (MEM_HEAD_DIM ** -0.5)).astype(BF16)
    lane_head = lax.broadcasted_iota(jnp.int32, (tm, 256), 1) // MEM_HEAD_DIM
    mv = mv_ref[0]
    o = jnp.zeros((tm, 256), F32)
    for hd in range(MEM_HEADS):
        s = _dot(q[:, hd * MEM_HEAD_DIM:(hd + 1) * MEM_HEAD_DIM], mkt_ref[0, hd * MEM_HEAD_DIM:(hd + 1) * MEM_HEAD_DIM, :])
        p = jnp.exp(s - jnp.max(s, axis=-1, keepdims=True))
        p = p / jnp.sum(p, axis=-1, keepdims=True)
        o = o + jnp.where(lane_head == hd, _dot(p.astype(BF16), mv), 0.0)
    x2 = x1 + _dot(o.astype(BF16), wo_ref[...])
    x2_ref[...] = x2
    h3f = _rms(x2, gffn_ref[...])
    hp_ref[...] = _pack_bf16_pairs(h3f)
    h3 = h3f.astype(BF16)
    logits = _dot(h3, wr_ref[...]) + br_ref[...]
    lane = lax.broadcasted_iota(jnp.int32, (tm, LANES), 1)
    gl = jnp.where(lane < MOE_GROUPS, logits, -jnp.inf)
    gmax = jnp.max(gl, axis=-1, keepdims=True)
    gsel = jnp.min(jnp.where(gl == gmax, lane, LANES), axis=-1, keepdims=True)
    pg_sel = 1.0 / jnp.sum(jnp.exp(gl - gmax), axis=-1, keepdims=True)
    in_group = (lane - ROUTER_EXPERT_LANE) // MOE_EXPERTS_PER_GROUP == gsel
    el = jnp.where(in_group, logits, -jnp.inf)
    m1 = jnp.max(el, axis=-1, keepdims=True)
    i1 = jnp.min(jnp.where(el == m1, lane, LANES), axis=-1, keepdims=True)
    el2 = jnp.where(lane == i1, -jnp.inf, el)
    m2 = jnp.max(el2, axis=-1, keepdims=True)
    i2 = jnp.min(jnp.where(el2 == m2, lane, LANES), axis=-1, keepdims=True)
    e21 = jnp.exp(m2 - m1)
    c1 = pg_sel / (1.0 + e21)
    @pl.when(pl.program_id(0) == 0)
    def _():
        run_ref[...] = jnp.zeros_like(run_ref)

    oh1 = jnp.where(lane == i1, 1.0, 0.0)
    oh2 = jnp.where(lane == i2, 1.0, 0.0)
    both = oh1 + oh2
    tr = lax.broadcasted_iota(jnp.int32, (tm, tm), 0)
    tc = lax.broadcasted_iota(jnp.int32, (tm, tm), 1)
    before = _dot(jnp.where(tc < tr, 1.0, 0.0).astype(BF16), both.astype(BF16)) + run_ref[0:1, :]
    r1 = jnp.sum(oh1 * before, axis=-1, keepdims=True)
    r2 = jnp.sum(oh2 * before, axis=-1, keepdims=True)
    total = run_ref[...] + jnp.sum(both, axis=0, keepdims=True)
    run_ref[...] = total
    cnt_ref[...] = total
    ids = (jnp.where(lane == 0, i1, i2) - ROUTER_EXPERT_LANE).astype(F32)
    rt_ref[...] = jnp.where(lane < 2, ids, jnp.where(lane == 2, c1, jnp.where(lane == 3, c1 * e21,
                            jnp.where(lane == 4, r1, jnp.where(lane == 5, r2, 0.0)))))


def _merge(x, ys, gmix, wg, wbr, wout, gq, wq, mkt, mv, wo, gffn, wr, br, batch, seq):
    n, d = x.shape
    tm = MERGE_TM
    spt = seq // tm
    m = mv.shape[1]
    tok = lambda w: pl.BlockSpec((tm, w), lambda i: (i, 0))
    return pl.pallas_call(
        functools.partial(_merge_kernel, tm=tm),
        out_shape=(jax.ShapeDtypeStruct((n, d), F32), jax.ShapeDtypeStruct((n, d // 2), F32),
                   jax.ShapeDtypeStruct((n, LANES), F32), jax.ShapeDtypeStruct((8, LANES), F32)),
        grid=(n // tm,),
        in_specs=[tok(d), tok(256), tok(256), tok(256), tok(256),
                  _const_spec((1, d)), _const_spec((d, N_BRANCH * d)), _const_spec((N_BRANCH, 256, d)),
                  _const_spec((d, d)), _const_spec((1, d)), _const_spec((d, 256)),
                  pl.BlockSpec((1, 256, m), lambda i: (i // spt, 0, 0)),
                  pl.BlockSpec((1, m, 256), lambda i: (i // spt, 0, 0)),
                  _const_spec((256, d)), _const_spec((1, d)), _const_spec((d, LANES)), _const_spec((1, LANES))],
        out_specs=(tok(d), tok(d // 2), tok(LANES), pl.BlockSpec((8, LANES), lambda i: (0, 0))),
        scratch_shapes=[pltpu.VMEM((8, LANES), F32)],
        compiler_params=_params(("arbitrary",)),
        name="merge_mem_router",
    )(x, *ys, gmix, wg, wbr, wout, gq, wq, mkt, mv, wo, gffn, wr, br)


def _sc_gather_rows(table, idx):
    _, width = table.shape
    total = idx.shape[0]
    chunk, nbuf = SC_GATHER_CHUNK, SC_GATHER_BUFS
    workers = SC_CORES * SC_SUBCORES
    per_w = total // workers
    nch = per_w // chunk
    assert total % (workers * chunk * nbuf) == 0
    mesh = plsc.VectorSubcoreMesh(core_axis_name="c", subcore_axis_name="s")

    @functools.partial(
        pl.kernel, mesh=mesh, out_type=jax.ShapeDtypeStruct((total, width), table.dtype),
        scratch_types=[pltpu.VMEM((nch, chunk), jnp.int32), pltpu.VMEM((nbuf, chunk, width), table.dtype),
                       pltpu.SemaphoreType.DMA((nbuf,)), pltpu.SemaphoreType.DMA((nbuf,))])
    def gather_kernel(table_hbm, idx_hbm, out_hbm, idx_v, rows_v, gsem, wsem):
        wid = lax.axis_index("s") * SC_CORES + lax.axis_index("c")
        pltpu.sync_copy(idx_hbm.at[wid], idx_v)

        def gather(j, slot):
            return pltpu.make_async_copy(table_hbm.at[idx_v.at[j]], rows_v.at[slot], gsem.at[slot])

        def write(j, slot):
            off = pl.multiple_of(wid * per_w + j * chunk, chunk)
            return pltpu.make_async_copy(rows_v.at[slot], out_hbm.at[pl.ds(off, chunk)], wsem.at[slot])

        for slot in range(nbuf):
            gather(slot, slot).start()

        @pl.loop(0, nch // nbuf)
        def _(g):
            for slot in range(nbuf):
                j = g * nbuf + slot
                gather(j, slot).wait()
                write(j, slot).start()
                write(j, slot).wait()

                @pl.when(j + nbuf < nch)
                def _():
                    gather(j + nbuf, slot).start()

    return gather_kernel(table, idx.reshape(workers, nch, chunk))


def _sc_scatter_rows(table, dest2, total):
    n, width = table.shape
    chunk, nbuf = SC_GATHER_CHUNK, SC_GATHER_BUFS
    workers = SC_CORES * SC_SUBCORES
    per_w = n // workers
    nch = per_w // chunk
    assert n % (workers * chunk * nbuf) == 0
    mesh = plsc.VectorSubcoreMesh(core_axis_name="c", subcore_axis_name="s")

    @functools.partial(
        pl.kernel, mesh=mesh, out_type=jax.ShapeDtypeStruct((total, width), table.dtype),
        scratch_types=[pltpu.VMEM((2, nch, chunk), jnp.int32), pltpu.VMEM((nbuf, chunk, width), table.dtype),
                       pltpu.SemaphoreType.DMA((nbuf,)), pltpu.SemaphoreType.DMA((nbuf,))])
    def scatter_kernel(table_hbm, idx_hbm, out_hbm, idx_v, rows_v, rsem, wsem):
        wid = lax.axis_index("s") * SC_CORES + lax.axis_index("c")
        pltpu.sync_copy(idx_hbm.at[wid], idx_v)

        def read(j, slot):
            off = pl.multiple_of(wid * per_w + j * chunk, chunk)
            return pltpu.make_async_copy(table_hbm.at[pl.ds(off, chunk)], rows_v.at[slot], rsem.at[slot])

        def write(j, slot, k):
            return pltpu.make_async_copy(rows_v.at[slot], out_hbm.at[idx_v.at[k, j]], wsem.at[slot])

        for slot in range(nbuf):
            read(slot, slot).start()

        @pl.loop(0, nch // nbuf)
        def _(g):
            for slot in range(nbuf):
                j = g * nbuf + slot
                read(j, slot).wait()
                write(j, slot, 0).start()
                write(j, slot, 1).start()
                write(j, slot, 0).wait()
                write(j, slot, 1).wait()

                @pl.when(j + nbuf < nch)
                def _():
                    read(j + nbuf, slot).start()

    idx = dest2.reshape(2, workers, nch, chunk).transpose(1, 0, 2, 3)
    return scatter_kernel(table, idx)


def _dispatch_plan(rt, cnt, n):
    ne, blk = MOE_N_EXPERTS, MOE_BLOCK
    n_blocks = (2 * n) // blk + ne
    experts = jnp.arange(ne, dtype=jnp.int32)
    counts = cnt[0, ROUTER_EXPERT_LANE:ROUTER_EXPERT_LANE + ne].astype(jnp.int32)
    padded = (counts + blk - 1) // blk * blk
    pend = jnp.cumsum(padded)
    pstart = pend - padded
    ids = rt[:, 0:2].astype(jnp.int32)
    pos = rt[:, 4:6].astype(jnp.int32)
    first_row = jnp.sum(jnp.where(ids[:, :, None] == experts[None, None, :], pstart[None, None, :], 0), axis=-1)
    dest2 = (first_row + pos).T
    b0 = jnp.arange(n_blocks, dtype=jnp.int32) * blk
    block_e = jnp.minimum(jnp.sum((pend[None, :] <= b0[:, None]).astype(jnp.int32), axis=1), ne - 1)
    n_valid = jnp.clip(counts[block_e] - (b0 - pstart[block_e]), 0, blk).astype(jnp.int32)
    return dest2, block_e, n_valid


def _expert_block_kernel(be_ref, nv_ref, xs_ref, wg_ref, wu_ref, wd_ref, o_ref, wgb_ref, wub_ref, wdb_ref):
    b = pl.program_id(0)
    valid = nv_ref[b]

    @pl.when((b == 0) | (be_ref[b] != be_ref[jnp.maximum(b - 1, 0)]))
    def _():
        wgb_ref[...] = wg_ref[0].astype(BF16)
        wub_ref[...] = wu_ref[0].astype(BF16)
        wdb_ref[...] = wd_ref[0].astype(BF16)

    @pl.when(valid > 0)
    def _():
        row = lax.broadcasted_iota(jnp.int32, xs_ref.shape, 0)
        words = jnp.where(row < valid, xs_ref[...], 0.0)
        h = _unpack_bf16_pairs(words).astype(BF16)
        gt = _dot(h, wgb_ref[...])
        hid = gt * _sigmoid(gt) * _dot(h, wub_ref[...])
        o_ref[...] = _pack_bf16_pairs(_dot(hid.astype(BF16), wdb_ref[...]))

    @pl.when(valid == 0)
    def _():
        o_ref[...] = jnp.zeros_like(o_ref)


def _expert_blocks(xs, wg, wu, wd, layer, block_e, n_used):
    p_rows, half = xs.shape
    d, hid = wg.shape[-2:]
    blk = MOE_BLOCK
    grid_spec = pltpu.PrefetchScalarGridSpec(
        num_scalar_prefetch=2, grid=(p_rows // blk,),
        in_specs=[pl.BlockSpec((blk, half), lambda b, be, nu: (b, 0)),
                  pl.BlockSpec((None, 1, d, hid), lambda b, be, nu: (layer, be[b], 0, 0)),
                  pl.BlockSpec((None, 1, d, hid), lambda b, be, nu: (layer, be[b], 0, 0)),
                  pl.BlockSpec((None, 1, hid, d), lambda b, be, nu: (layer, be[b], 0, 0))],
        out_specs=pl.BlockSpec((blk, half), lambda b, be, nu: (b, 0)),
        scratch_shapes=[pltpu.VMEM((d, hid), BF16), pltpu.VMEM((d, hid), BF16), pltpu.VMEM((hid, d), BF16)])
    return pl.pallas_call(
        _expert_block_kernel, out_shape=jax.ShapeDtypeStruct((p_rows, half), F32), grid_spec=grid_spec,
        compiler_params=_params(("arbitrary",)),
        name="moe_expert_blocks",
    )(block_e, n_used, xs, wg, wu, wd)


def _combine_kernel(x_ref, y1_ref, y2_ref, rt_ref, gfin_ref, o_ref, *, final_norm):
    out = (x_ref[...] + rt_ref[:, 2:3] * _unpack_bf16_pairs(y1_ref[...])
           + rt_ref[:, 3:4] * _unpack_bf16_pairs(y2_ref[...]))
    o_ref[...] = _rms(out, gfin_ref[...]) if final_norm else out


def _combine(x2, y_halves, rt, gfin, final_norm):
    n, d = x2.shape
    tm = COMBINE_TM
    nt = n // tm
    return pl.pallas_call(
        functools.partial(_combine_kernel, final_norm=final_norm),
        out_shape=jax.ShapeDtypeStruct((n, d), F32),
        grid=(nt,),
        in_specs=[pl.BlockSpec((tm, d), lambda i: (i, 0)),
                  pl.BlockSpec((tm, d // 2), lambda i: (i, 0)),
                  pl.BlockSpec((tm, d // 2), lambda i: (i + nt, 0)),
                  pl.BlockSpec((tm, LANES), lambda i: (i, 0)), _const_spec((1, d))],
        out_specs=pl.BlockSpec((tm, d), lambda i: (i, 0)),
        compiler_params=_params(("parallel",)), name="moe_combine",
    )(x2, y_halves, y_halves, rt, gfin)


def _moe(x2, hp, rt, cnt, wg, wu, wd, layer, gfin, final_norm):
    n = x2.shape[0]
    dest2, block_e, n_valid = _dispatch_plan(rt, cnt, n)
    xs = _sc_scatter_rows(hp, dest2, block_e.shape[0] * MOE_BLOCK)
    yb = _expert_blocks(xs, wg, wu, wd, layer, block_e, n_valid)
    y_halves = _sc_gather_rows(yb, dest2.reshape(2 * n))
    return _combine(x2, y_halves, rt, gfin, final_norm)


def kernel(x, mem, positions, norm_mix, w_in, diff_lambda, hgrn_lb_logits, spatial_w, spatial_b, w_branch, w_out,
           norm_mem_q, norm_mem_kv, w_mem_q, w_mem_kv, w_mem_o, norm_ffn, w_router_group, b_router_group,
           w_router_expert, b_router_expert, w_exp_gate, w_exp_up, w_exp_down, norm_final):
    batch, seq, d = x.shape
    depth = w_in.shape[0]
    n = batch * seq
    xf = x.reshape(n, d)
    tabs = _rope_tables(positions)
    row = lambda v: v.reshape(1, -1).astype(F32)
    for l in range(depth):
        lam_init = 0.8 - 0.6 * math.exp(-0.3 * l)
        w1, w_gate = _split_w_in(w_in, l)
        sw = spatial_w[l].reshape(SGU_GROUPS * SGU_CHUNK, SGU_CHUNK)
        sb = jnp.repeat(spatial_b[l].T, SGU_GROUP_DIM, axis=1)
        qat, ka, vat, hb, y_c, qdt, iqt, dkv, dkvt, ikw, iwt = _projection(
            xf, row(norm_mix[l]), w1, tabs, sw, sb, batch, seq)
        y_a = _diff_attention(diff_lambda[l], qat, ka, vat, lam_init, batch, seq)
        y_b = _hgrn(hgrn_lb_logits, hb, l, batch, seq)
        y_d = _dsa(qdt, iqt, iwt, dkv, dkvt, ikw, batch, seq)
        mkt, mv = _mem_kv(mem, row(norm_mem_kv[l]), w_mem_kv[l].astype(BF16))
        e0, e1 = ROUTER_EXPERT_LANE, ROUTER_EXPERT_LANE + MOE_N_EXPERTS
        wr = jnp.zeros((d, LANES), F32)
        wr = wr.at[:, :MOE_GROUPS].set(w_router_group[l]).at[:, e0:e1].set(w_router_expert[l]).astype(BF16)
        br = jnp.zeros((1, LANES), F32)
        br = br.at[0, :MOE_GROUPS].set(b_router_group[l]).at[0, e0:e1].set(b_router_expert[l])
        x2, hp, rt, cnt = _merge(xf, (y_a, y_b, y_c, y_d), row(norm_mix[l]), w_gate, w_branch[l].astype(BF16),
                                 w_out[l].astype(BF16), row(norm_mem_q[l]), w_mem_q[l].astype(BF16), mkt, mv,
                                 w_mem_o[l].astype(BF16), row(norm_ffn[l]), wr, br, batch, seq)
        xf = _moe(x2, hp, rt, cnt, w_exp_gate, w_exp_up, w_exp_down, l,
                  row(norm_final), final_norm=(l == depth - 1))
    return xf.reshape(batch, seq, d)
```

```python
import functools
import math

import numpy as np
import jax
import jax.numpy as jnp
from jax import lax
from jax.experimental import pallas as pl
from jax.experimental.pallas import tpu as pltpu
from jax.experimental.pallas import tpu_sc as plsc

F32 = jnp.float32
BF16 = jnp.bfloat16

NORM_EPS = 1e-6
ROPE_THETA = 10000.0
NEG_BIG = -1e30

N_BRANCH = 4
DIFF_HEADS = 4
DIFF_HEAD_DIM = 32
HGRN_DIM = 64
HGRN_CHUNK = 32
HGRN_UNROLL = 8
HGRN_MIN_FORGET = 1e-30
SGU_GROUPS = 4
SGU_GROUP_DIM = 64
SGU_CHUNK = 128
DSA_HEADS = 4
DSA_HEAD_DIM = 64
DSA_IDX_HEADS = 4
DSA_IDX_DIM = 32
DSA_TOPK = 256
MEM_HEADS = 4
MEM_HEAD_DIM = 64
MOE_GROUPS = 4
MOE_EXPERTS_PER_GROUP = 8
MOE_N_EXPERTS = 32
MOE_BLOCK = 512
ROUTER_EXPERT_LANE = 32
SC_CORES = 2
SC_SUBCORES = 16
SC_GATHER_CHUNK = 16
SC_GATHER_BUFS = 4

LANES = 128
VMEM_LIMIT = 56 * 1024 * 1024

PROJ_TM = 1024
DIFF_TQ = 256
DIFF_TK = 512
HGRN_TC = 512
DSA_TQ = 256
DSA_TK = 512
MERGE_TM = 1024
COMBINE_TM = 512
ROPE_TM = 1024

C_AQ, C_AK, C_AV = 0, 256, 512
C_HB = 768
C_UV = 1792
C_DQ = 2304
C_DKV = 2560
C_IQ = 2688
C_IKW = 2816
IW_LANE = 32
C_GATE = 2852
C_TOTAL = 2944
LOG2E = math.log2(math.e)


def _params(sem):
    return pltpu.CompilerParams(dimension_semantics=sem, vmem_limit_bytes=VMEM_LIMIT)


def _const_spec(shape):
    nd = len(shape)
    return pl.BlockSpec(shape, lambda *_: (0,) * nd, pipeline_mode=pl.Buffered(1))


def _rms(xf, gain=None):
    y = xf * lax.rsqrt(jnp.mean(xf * xf, axis=-1, keepdims=True) + NORM_EPS)
    return y if gain is None else y * gain


def _sigmoid(x):
    return 0.5 * jnp.tanh(0.5 * x) + 0.5


def _pack_bf16_pairs(x):
    w = x.shape[-1] // 2
    xb = x.astype(BF16).astype(F32)
    lo = lax.shift_right_logical(pltpu.bitcast(xb[:, :w], jnp.int32), 16)
    hi = pltpu.bitcast(xb[:, w:], jnp.int32) & jnp.int32(-65536)
    return pltpu.bitcast(hi | lo, F32)


def _unpack_bf16_pairs(words):
    bits = pltpu.bitcast(words, jnp.int32)
    lo = pltpu.bitcast(bits << 16, F32)
    hi = pltpu.bitcast(bits & jnp.int32(-65536), F32)
    return jnp.concatenate([lo, hi], axis=1)


def _dot(a, b):
    return jnp.dot(a, b, preferred_element_type=F32)


def _dot_nt(a, b):
    return lax.dot_general(a, b, (((1,), (1,)), ((), ())), preferred_element_type=F32)


def _rope_table_kernel(pos_ref, frq_ref, sgn_ref, c32_ref, s32_ref, c64_ref, s64_ref):
    pos = pos_ref[...].astype(F32)
    twice = lambda t: jnp.concatenate([t, t], axis=1)
    a32 = pos * frq_ref[0:1, :LANES]
    a64 = pos * frq_ref[1:2, :LANES]
    c32_ref[...] = twice(jnp.cos(a32))
    s32_ref[...] = twice(jnp.sin(a32) * sgn_ref[0:1, :LANES])
    c64_ref[...] = twice(jnp.cos(a64))
    s64_ref[...] = twice(jnp.sin(a64) * sgn_ref[1:2, :LANES])


def _rope_tables(positions):
    n = positions.size
    pos = positions.reshape(n, 1).astype(jnp.int32)
    lane = np.arange(256)
    inv32 = ROPE_THETA ** (-jnp.arange(16, dtype=F32) * (2.0 / 32))
    inv64 = ROPE_THETA ** (-jnp.arange(32, dtype=F32) * (2.0 / 64))
    frq = jnp.stack([inv32[lane % 16], inv64[lane % 32]])
    sgn = jnp.asarray(np.stack([np.where(lane % 32 < 16, -1.0, 1.0),
                                np.where(lane % 64 < 32, -1.0, 1.0)]), F32)
    tm = ROPE_TM
    tab = jax.ShapeDtypeStruct((n, 256), F32)
    return pl.pallas_call(
        _rope_table_kernel,
        out_shape=(tab, tab, tab, tab),
        grid=(n // tm,),
        in_specs=[pl.BlockSpec((tm, 1), lambda i: (i, 0)), _const_spec((2, 256)), _const_spec((2, 256))],
        out_specs=tuple(pl.BlockSpec((tm, 256), lambda i: (i, 0)) for _ in range(4)),
        compiler_params=_params(("parallel",)),
        name="rope_tables",
    )(pos, frq, sgn)


def _w1_kernel(wt_ref, o_ref, *, layer):
    o_ref[...] = wt_ref[:, layer, :].T.astype(BF16)


def _wgate_kernel(a_ref, b_ref, o_ref, *, layer):
    off = C_GATE % LANES
    o_ref[...] = jnp.concatenate([a_ref[off:, layer, :], b_ref[:off, layer, :]], axis=0).T.astype(BF16)


def _split_w_in(w_in, layer):
    depth, d, width = w_in.shape
    wt = jnp.transpose(w_in, (2, 0, 1))
    rows = lambda f: pl.BlockSpec((LANES, depth, d), lambda i: (f(i), 0, 0))
    cols = pl.BlockSpec((d, LANES), lambda i: (0, i))
    w1 = pl.pallas_call(
        functools.partial(_w1_kernel, layer=layer), out_shape=jax.ShapeDtypeStruct((d, C_TOTAL), BF16),
        grid=(C_TOTAL // LANES,), in_specs=[rows(lambda i: i)], out_specs=cols,
        compiler_params=_params(("parallel",)), name="split_w_in",
    )(wt)
    a0 = C_GATE // LANES
    gate_w = width - C_GATE
    wg = pl.pallas_call(
        functools.partial(_wgate_kernel, layer=layer), out_shape=jax.ShapeDtypeStruct((d, gate_w), BF16),
        grid=(gate_w // LANES,), in_specs=[rows(lambda i: a0 + i), rows(lambda i: a0 + i + 1)], out_specs=cols,
        compiler_params=_params(("parallel",)), name="split_w_gate",
    )(wt, wt)
    return w1, wg


def _gelu_tanh(x):
    return 0.5 * x * (1.0 + jnp.tanh(math.sqrt(2.0 / math.pi) * (x + 0.044715 * (x * x * x))))


def _rope(x, cos, sin_signed, half):
    w = x.shape[-1]
    lane = lax.broadcasted_iota(jnp.int32, x.shape, 1)
    partner = jnp.where(lane % (2 * half) < half, pltpu.roll(x, w - half, 1), pltpu.roll(x, half, 1))
    return x * cos + partner * sin_signed


def _proj_kernel(x_ref, g_ref, w_ref, c32_ref, s32_ref, c64_ref, s64_ref, sw_ref, sb_ref,
                 qat_ref, ka_ref, vat_ref, hb_ref, yc_ref, qdt_ref, iqt_ref, dkv_ref, dkvt_ref, ikw_ref, iwt_ref,
                 *, tm):
    h = _rms(x_ref[...], g_ref[...]).astype(BF16)

    def proj(c0, width):
        return _dot(h, w_ref[:, c0:c0 + width])

    c32, s32, c64, s64 = c32_ref[...], s32_ref[...], c64_ref[...], s64_ref[...]
    qat_ref[0] = (_rope(proj(C_AQ, 256), c32, s32, 16) * (DIFF_HEAD_DIM ** -0.5 * LOG2E)).T.astype(BF16)
    ka_ref[...] = _rope(proj(C_AK, 256), c32, s32, 16).astype(BF16)
    vat_ref[0] = proj(C_AV, 256).astype(BF16).T
    hb_ref[...] = proj(C_HB, 1024)
    qdt_ref[0] = (_rope(proj(C_DQ, 256), c64, s64, 32) * (DSA_HEAD_DIM ** -0.5 * LOG2E)).T.astype(BF16)
    iqt_ref[0] = _rope(proj(C_IQ, 128), c32[:, :128], s32[:, :128], 16).T.astype(BF16)
    lane = lax.broadcasted_iota(jnp.int32, (tm, 128), 1)
    is_k = lane < DSA_HEAD_DIM
    dkv = _rope(proj(C_DKV, 128), jnp.where(is_k, c64[:, :128], 1.0), jnp.where(is_k, s64[:, :128], 0.0), 32)
    is_ik = lane < DSA_IDX_DIM
    ikw = _rope(proj(C_IKW, 128), jnp.where(is_ik, c32[:, :128], 1.0), jnp.where(is_ik, s32[:, :128], 0.0), 16)
    dkv_ref[...] = dkv.astype(BF16)
    dkvt_ref[0] = dkv.T.astype(BF16)
    ikw_ref[...] = ikw.astype(BF16)
    iw_scale = DSA_IDX_HEADS ** -0.5 * DSA_IDX_DIM ** -0.5
    iwt_ref[0] = (ikw * iw_scale).T[IW_LANE:IW_LANE + 8, :]
    uv = _gelu_tanh(proj(C_UV, 512))
    u, v = uv[:, :256], uv[:, 256:]
    mu = jnp.mean(v, axis=-1, keepdims=True)
    vc = v - mu
    vn = (vc * lax.rsqrt(jnp.mean(vc * vc, axis=-1, keepdims=True) + NORM_EPS)).astype(BF16)
    r = lax.broadcasted_iota(jnp.int32, (SGU_GROUPS * SGU_CHUNK, SGU_CHUNK), 0)
    c = lax.broadcasted_iota(jnp.int32, (SGU_GROUPS * SGU_CHUNK, SGU_CHUNK), 1)
    wt = jnp.where((r % SGU_CHUNK) >= c, sw_ref[...], 0.0).astype(BF16)
    lane_grp = lax.broadcasted_iota(jnp.int32, (SGU_CHUNK, 256), 1) // SGU_GROUP_DIM
    for ch in range(tm // SGU_CHUNK):
        r0 = ch * SGU_CHUNK
        full = _dot(wt, vn[r0:r0 + SGU_CHUNK, :])
        mixed = sb_ref[...]
        for g in range(SGU_GROUPS):
            mixed = mixed + jnp.where(lane_grp == g, full[g * SGU_CHUNK:(g + 1) * SGU_CHUNK, :], 0.0)
        yc_ref[r0:r0 + SGU_CHUNK, :] = (u[r0:r0 + SGU_CHUNK, :] * mixed).astype(BF16)


def _projection(x, gain, w1, tabs, sw, sb, batch, seq):
    n, d = x.shape
    tm = PROJ_TM
    spt = seq // tm
    tok = lambda w: pl.BlockSpec((tm, w), lambda i: (i, 0))
    tr = lambda rows: pl.BlockSpec((1, rows, tm), lambda i: (i // spt, 0, i % spt))
    out_shape = (
        jax.ShapeDtypeStruct((batch, 256, seq), BF16),
        jax.ShapeDtypeStruct((n, 256), BF16),
        jax.ShapeDtypeStruct((batch, 256, seq), BF16),
        jax.ShapeDtypeStruct((n, 1024), F32),
        jax.ShapeDtypeStruct((n, 256), BF16),
        jax.ShapeDtypeStruct((batch, 256, seq), BF16),
        jax.ShapeDtypeStruct((batch, 128, seq), BF16),
        jax.ShapeDtypeStruct((n, 128), BF16),
        jax.ShapeDtypeStruct((batch, 128, seq), BF16),
        jax.ShapeDtypeStruct((n, 128), BF16),
        jax.ShapeDtypeStruct((batch, 8, seq), F32),
    )
    return pl.pallas_call(
        functools.partial(_proj_kernel, tm=tm),
        out_shape=out_shape,
        grid=(n // tm,),
        in_specs=[tok(d), _const_spec((1, d)), _const_spec((d, C_TOTAL)),
                  tok(256), tok(256), tok(256), tok(256),
                  _const_spec((SGU_GROUPS * SGU_CHUNK, SGU_CHUNK)), _const_spec((SGU_CHUNK, 256))],
        out_specs=(tr(256), tok(256), tr(256), tok(1024), tok(256), tr(256), tr(128), tok(128), tr(128), tok(128), tr(8)),
        compiler_params=_params(("parallel",)),
        name="projection",
    )(x, gain, w1, *tabs, sw, sb)


def _diff_attn_kernel(lam_ref, qt_ref, k_ref, vt_ref, o_ref, sa_ref, sb_ref, *, lam_init, tq, tk):
    q0 = pl.program_id(1) * tq
    kb_diag = q0 // tk
    lv = lam_ref[...]
    lam = (jnp.exp(jnp.sum(lv[0:1] * lv[1:2], axis=-1, keepdims=True))
           - jnp.exp(jnp.sum(lv[2:3] * lv[3:4], axis=-1, keepdims=True)) + lam_init)
    qt = qt_ref[0]
    feat = lax.broadcasted_iota(jnp.int32, (256, tq), 0) // DIFF_HEAD_DIM
    n_maps = 2 * DIFF_HEADS
    qz = jnp.concatenate([jnp.where(feat == i, qt, jnp.zeros_like(qt)) for i in range(n_maps)], axis=1)
    wide = n_maps * tq
    key_i = lax.broadcasted_iota(jnp.int32, (tk, wide), 0)
    qry_i = q0 + lax.broadcasted_iota(jnp.int32, (tk, wide), 1) % tq

    def scores(kb, dst_ref):
        k0 = pl.multiple_of(kb * tk, tk)
        dst_ref[...] = _dot(k_ref[pl.ds(k0, tk), :], qz)

    def absorb(src_ref, kb, carry, masked):
        m_i, l_i, acc = carry
        k0 = pl.multiple_of(kb * tk, tk)
        s = src_ref[...]
        if masked:
            s = jnp.where(k0 + key_i <= qry_i, s, NEG_BIG)
        m_new = jnp.maximum(m_i, jnp.max(s, axis=0, keepdims=True))
        p = jnp.exp2(s - m_new)
        alpha = jnp.exp2(m_i - m_new)
        l_new = alpha * l_i + jnp.sum(p, axis=0, keepdims=True)
        pb = p.astype(BF16)
        pv = jnp.concatenate(
            [_dot(vt_ref[0, hd * 64:(hd + 1) * 64, pl.ds(k0, tk)], pb[:, 2 * hd * tq:(2 * hd + 2) * tq])
             for hd in range(DIFF_HEADS)], axis=1)
        return m_new, l_new, alpha * acc + pv

    def pair(j, carry):
        kb = 2 * j
        scores(kb + 1, sb_ref)
        carry = absorb(sa_ref, kb, carry, False)
        scores(kb + 2, sa_ref)
        return absorb(sb_ref, kb + 1, carry, False)

    init = (jnp.full((1, wide), NEG_BIG, F32), jnp.zeros((1, wide), F32), jnp.zeros((64, wide), F32))
    scores(0, sa_ref)
    carry = lax.fori_loop(0, kb_diag // 2, pair, init)

    def tail_odd(carry):
        scores(kb_diag, sb_ref)
        carry = absorb(sa_ref, kb_diag - 1, carry, False)
        return absorb(sb_ref, kb_diag, carry, True)

    def tail_even(carry):
        return absorb(sa_ref, kb_diag, carry, True)

    _, l_f, acc = lax.cond(kb_diag % 2 == 1, tail_odd, tail_even, carry)
    o_all = acc / l_f
    heads = []
    for hd in range(DIFF_HEADS):
        o0 = o_all[:, 2 * hd * tq:(2 * hd + 1) * tq]
        o1 = o_all[:, (2 * hd + 1) * tq:(2 * hd + 2) * tq]
        o_h = o0 - lam * o1
        ms = jnp.mean(o_h * o_h, axis=0, keepdims=True)
        heads.append(o_h * lax.rsqrt(ms + NORM_EPS) * (1.0 - lam_init))
    o_ref[...] = jnp.concatenate(heads, axis=0).T.astype(BF16)


def _diff_attention(lam_vec, qat, ka, vat, lam_init, batch, seq):
    tq, tk = DIFF_TQ, DIFF_TK
    nq = seq // tq
    return pl.pallas_call(
        functools.partial(_diff_attn_kernel, lam_init=lam_init, tq=tq, tk=tk),
        out_shape=jax.ShapeDtypeStruct((batch * seq, 256), BF16),
        grid=(batch, nq),
        in_specs=[_const_spec((4, DIFF_HEAD_DIM)),
                  pl.BlockSpec((1, 256, tq), lambda b, i: (b, 0, i)),
                  pl.BlockSpec((seq, 256), lambda b, i: (b, 0)),
                  pl.BlockSpec((1, 256, seq), lambda b, i: (b, 0, 0))],
        out_specs=pl.BlockSpec((tq, 256), lambda b, i: (b * nq + i, 0)),
        scratch_shapes=[pltpu.VMEM((tk, 2 * DIFF_HEADS * tq), F32), pltpu.VMEM((tk, 2 * DIFF_HEADS * tq), F32)],
        compiler_params=_params(("parallel", "parallel")),
        name="diff_attention",
    )(lam_vec, qat, ka, vat)


def _hgrn_kernel(lbl_ref, hb_ref, o_ref, st_ref, pstk_ref, *, layer, tc):
    cz = HGRN_CHUNK
    w = 256

    @pl.when(pl.program_id(1) == 0)
    def _():
        st_ref[...] = jnp.zeros_like(st_ref)

    lg = lbl_ref[...]
    e = jnp.exp(lg - jnp.max(lg, axis=0, keepdims=True))
    lw = e / jnp.sum(e, axis=0, keepdims=True)
    lb = jnp.sum(lw[0:layer + 1], axis=0, keepdims=True) - lw[0:1]

    ri = lax.broadcasted_iota(jnp.int32, (cz, cz), 0)
    ci = lax.broadcasted_iota(jnp.int32, (cz, cz), 1)
    tri = (ri >= ci).astype(F32)
    rb = lax.broadcasted_iota(jnp.int32, (w, w), 0) // HGRN_DIM
    cb = lax.broadcasted_iota(jnp.int32, (w, w), 1) // HGRN_DIM
    same_head = rb == cb
    head_ones = same_head.astype(BF16)
    trows = {r: r + lax.broadcasted_iota(jnp.int32, (cz - r, w), 0) for r in range(0, cz, 16)}

    def chunk(c, carry):
        r0 = pl.multiple_of(c * cz, cz)
        q = hb_ref[pl.ds(r0, cz), 0:256]
        fp = hb_ref[pl.ds(r0, cz), 256:512]
        v = hb_ref[pl.ds(r0, cz), 512:768]
        g = hb_ref[pl.ds(r0, cz), 768:1024]
        qf = q * _sigmoid(q)
        f = lb + (1.0 - lb) * jax.nn.sigmoid(fp)
        log_f = jnp.log(jnp.maximum(f, HGRN_MIN_FORGET))
        kf = (1.0 - lb) * jax.nn.sigmoid(-fp)
        bc = jnp.dot(tri, log_f, preferred_element_type=F32, precision=lax.Precision.HIGHEST)
        st = st_ref[...]
        o = _dot_nt((qf * jnp.exp(bc)).astype(BF16), st.astype(BF16))
        for s in range(cz):
            r_lo = (s // 16) * 16
            arg = bc[r_lo:, :] - bc[s:s + 1, :]
            if s > r_lo:
                arg = jnp.where(trows[r_lo] >= s, arg, NEG_BIG)
            p = qf[r_lo:, :] * kf[s:s + 1, :] * jnp.exp(arg)
            if r_lo:
                pstk_ref[s * cz:s * cz + r_lo, :] = jnp.zeros((r_lo, w), BF16)
            pstk_ref[s * cz + r_lo:(s + 1) * cz, :] = p.astype(BF16)
        accs = [jnp.zeros((16, w), F32) for _ in range(cz // 16)]
        for sg in range(cz // 16):
            att = _dot(pstk_ref[sg * 16 * cz:(sg + 1) * 16 * cz, :], head_ones)
            for sl in range(16):
                s = sg * 16 + sl
                for j in range(sg, cz // 16):
                    accs[j] = accs[j] + att[sl * cz + 16 * j:sl * cz + 16 * j + 16, :] * v[s:s + 1, :]
        o = o + jnp.concatenate(accs, axis=0)
        b_end = bc[cz - 1:cz, :]
        kd = kf * jnp.exp(b_end - bc)
        upd = _dot(v.T.astype(BF16), kd.astype(BF16))
        st_ref[...] = st * jnp.exp(b_end) + jnp.where(same_head, upd, 0.0)
        ms = _dot(o * o, head_ones.astype(F32)) * (1.0 / HGRN_DIM)
        y = o * lax.rsqrt(ms + NORM_EPS)
        o_ref[pl.ds(r0, cz), :] = (y * (g * _sigmoid(g))).astype(BF16)
        return carry

    def group(gi, carry):
        for u in range(HGRN_UNROLL):
            chunk(gi * HGRN_UNROLL + u, carry)
        return carry

    lax.fori_loop(0, tc // cz // HGRN_UNROLL, group, 0)


def _hgrn(lb_logits, hb, layer, batch, seq):
    tc = HGRN_TC
    nt = seq // tc
    cz = HGRN_CHUNK
    return pl.pallas_call(
        functools.partial(_hgrn_kernel, layer=layer, tc=tc),
        out_shape=jax.ShapeDtypeStruct((batch * seq, 256), BF16),
        grid=(batch, nt),
        in_specs=[_const_spec(lb_logits.shape),
                  pl.BlockSpec((tc, 1024), lambda b, i: (b * nt + i, 0))],
        out_specs=pl.BlockSpec((tc, 256), lambda b, i: (b * nt + i, 0)),
        scratch_shapes=[pltpu.VMEM((256, 256), F32), pltpu.VMEM((cz * cz, 256), BF16)],
        compiler_params=_params(("parallel", "arbitrary")),
        name="hgrn2",
    )(lb_logits, hb)


def _dsa_kernel(qdt_ref, iqt_ref, iwt_ref, dkv_ref, dkvt_ref, ikw_ref, o_ref, key_ref, bias_ref, half_ref,
                sa_ref, sb_ref, *, tq, tk, n_sel):
    q0 = pl.program_id(1) * tq
    nkb = q0 // tk + 1
    key_i = lax.broadcasted_iota(jnp.int32, (tk, tq), 0)
    qry_i = q0 + lax.broadcasted_iota(jnp.int32, (tk, tq), 1)
    grp = tk // 8
    rows8 = lambda x: x.reshape(grp, 8, tq)
    iqt = iqt_ref[0]
    zpad = jnp.zeros((LANES - DSA_IDX_DIM, tq), BF16)
    iqz = jnp.concatenate([jnp.concatenate([iqt[hd * DSA_IDX_DIM:(hd + 1) * DSA_IDX_DIM, :], zpad], axis=0)
                           for hd in range(DSA_IDX_HEADS)], axis=1)
    iw = iwt_ref[0]

    last = nkb - 1

    def score_block(kb, carry, masked):
        k0 = pl.multiple_of(kb * tk, tk)
        sh = jnp.maximum(_dot(ikw_ref[pl.ds(k0, tk), :], iqz), 0.0)
        sc = jnp.zeros((tk, tq), F32)
        for hd in range(DSA_IDX_HEADS):
            sc = sc + sh[:, hd * tq:(hd + 1) * tq] * iw[hd:hd + 1, :]
        sc = sc + 0.0
        if masked:
            sc = jnp.where(k0 + key_i <= qry_i, sc, -jnp.inf)
        bits = pltpu.bitcast(sc, jnp.int32)
        key = jnp.where(bits < 0, bits ^ jnp.int32(0x7FFFFFFF), bits)
        key_ref[pl.ds(k0, tk), :] = key
        half_ref[pl.ds(k0, tk), :] = (key >> 16).astype(jnp.int16)
        return carry

    lax.fori_loop(0, last, functools.partial(score_block, masked=False), 0)
    score_block(last, 0, True)

    one16, zero16 = jnp.ones((), jnp.int16), jnp.zeros((), jnp.int16)
    low16 = np.int16(-2 ** 15)

    def count16(limit, strict):
        def body(kb, acc):
            k0 = pl.multiple_of(kb * tk, tk)
            for c in range(tk // 128):
                blk = half_ref[pl.ds(k0 + 128 * c, 128), :].reshape(8, 16, tq)
                hit = jnp.where(blk > limit if strict else blk >= limit, one16, zero16)
                parts = [hit[j] for j in range(8)]
                while len(parts) > 1:
                    parts = [a + b for a, b in zip(parts[0::2], parts[1::2])]
                acc = acc + parts[0]
            return acc
        acc = lax.fori_loop(0, nkb, body, jnp.zeros((16, tq), jnp.int16))
        return jnp.broadcast_to(jnp.sum(acc.astype(jnp.int32), axis=0, keepdims=True), (16, tq))

    def search16(need):
        t = jnp.full((16, tq), -2 ** 15, jnp.int32)
        for bit in range(15, -1, -1):
            trial = t + 2 ** bit
            t = jnp.where(count16(trial.astype(jnp.int16), False) >= need, trial, t)
        return t

    t_hi = search16(n_sel)
    t_hi16 = t_hi.astype(jnp.int16)
    need_lo = n_sel - count16(t_hi16, True)

    def low_block(kb, carry):
        k0 = pl.multiple_of(kb * tk, tk)
        lo = ((key_ref[pl.ds(k0, tk), :] & 0xFFFF) - 2 ** 15).astype(jnp.int16).reshape(tk // 16, 16, tq)
        hi = half_ref[pl.ds(k0, tk), :].reshape(tk // 16, 16, tq)
        half_ref[pl.ds(k0, tk), :] = jnp.where(hi == t_hi16, lo, low16).reshape(tk, tq)
        return carry

    lax.fori_loop(0, nkb, low_block, 0)
    t_lo = search16(need_lo)
    thr = ((t_hi << 16) | (t_lo + 2 ** 15))[0:8, :]

    need = (need_lo - count16(t_lo.astype(jnp.int16), True))[0:8, :].astype(F32)
    ur = lax.broadcasted_iota(jnp.int32, (LANES, LANES), 0)
    uc = lax.broadcasted_iota(jnp.int32, (LANES, LANES), 1)
    earlier = (uc < ur).astype(BF16)
    ones8 = jnp.ones((8, LANES), BF16)
    key_s = lax.broadcasted_iota(jnp.int32, (LANES, tq), 0)
    qry_s = q0 + lax.broadcasted_iota(jnp.int32, (LANES, tq), 1)

    def select_block(kb, seen, masked):
        k0 = pl.multiple_of(kb * tk, tk)
        for g in range(tk // LANES):
            r0 = g * LANES
            blk = key_ref[pl.ds(k0 + r0, LANES), :].reshape(LANES // 8, 8, tq)
            eq = blk == thr[None]
            eqb = jnp.where(eq, 1.0, 0.0).reshape(LANES, tq).astype(BF16)
            rank = _dot(earlier, eqb).reshape(LANES // 8, 8, tq) + seen[None]
            tie_bias = jnp.where(rank < need[None], 0.0, NEG_BIG)
            bias = jnp.where(blk > thr[None], 0.0, jnp.where(eq, tie_bias, NEG_BIG)).reshape(LANES, tq)
            if masked:
                bias = jnp.where(k0 + r0 + key_s <= qry_s, bias, NEG_BIG)
            bias_ref[pl.ds(k0 + r0, LANES), :] = bias
            seen = seen + _dot(ones8, eqb)
        return seen

    seen = lax.fori_loop(0, last, functools.partial(select_block, masked=False), jnp.zeros((8, tq), F32))
    select_block(last, seen, True)

    qdt = qdt_ref[0]
    zq = jnp.zeros((LANES - DSA_HEAD_DIM, tq), BF16)
    qz = jnp.concatenate([jnp.concatenate([qdt[hd * DSA_HEAD_DIM:(hd + 1) * DSA_HEAD_DIM, :], zq], axis=0)
                          for hd in range(DSA_HEADS)], axis=1)
    wide = DSA_HEADS * tq

    def scores(kb, dst_ref):
        k0 = pl.multiple_of(kb * tk, tk)
        dst_ref[...] = _dot(dkv_ref[pl.ds(k0, tk), :], qz)

    def absorb(src_ref, kb, carry):
        m_i, l_i, acc = carry
        k0 = pl.multiple_of(kb * tk, tk)
        bias = bias_ref[pl.ds(k0, tk), :]
        s = src_ref[...] + jnp.concatenate([bias] * DSA_HEADS, axis=1)
        m_new = jnp.maximum(m_i, jnp.max(s, axis=0, keepdims=True))
        p = jnp.exp2(s - m_new)
        alpha = jnp.exp2(m_i - m_new)
        l_new = alpha * l_i + jnp.sum(p, axis=0, keepdims=True)
        pv = _dot(dkvt_ref[0, DSA_HEAD_DIM:, pl.ds(k0, tk)], p.astype(BF16))
        return m_new, l_new, alpha * acc + pv

    def pair(j, carry):
        kb = 2 * j
        scores(kb + 1, sb_ref)
        carry = absorb(sa_ref, kb, carry)
        scores(kb + 2, sa_ref)
        return absorb(sb_ref, kb + 1, carry)

    init = (jnp.full((1, wide), NEG_BIG, F32), jnp.zeros((1, wide), F32), jnp.zeros((DSA_HEAD_DIM, wide), F32))
    last = nkb - 1
    scores(0, sa_ref)
    carry = lax.fori_loop(0, last // 2, pair, init)

    def tail_odd(carry):
        scores(last, sb_ref)
        return absorb(sb_ref, last, absorb(sa_ref, last - 1, carry))

    def tail_even(carry):
        return absorb(sa_ref, last, carry)

    _, l_f, acc = lax.cond(last % 2 == 1, tail_odd, tail_even, carry)
    o_all = acc / l_f
    o_ref[...] = jnp.concatenate([o_all[:, hd * tq:(hd + 1) * tq] for hd in range(DSA_HEADS)],
                                 axis=0).T.astype(BF16)


def _dsa(qdt, iqt, iwt, dkv, dkvt, ikw, batch, seq):
    tq, tk = DSA_TQ, DSA_TK
    nq = seq // tq
    n_sel = min(DSA_TOPK, seq // 4)
    return pl.pallas_call(
        functools.partial(_dsa_kernel, tq=tq, tk=tk, n_sel=n_sel),
        out_shape=jax.ShapeDtypeStruct((batch * seq, 256), BF16),
        grid=(batch, nq),
        in_specs=[pl.BlockSpec((1, 256, tq), lambda b, i: (b, 0, i)),
                  pl.BlockSpec((1, 128, tq), lambda b, i: (b, 0, i)),
                  pl.BlockSpec((1, 8, tq), lambda b, i: (b, 0, i)),
                  pl.BlockSpec((seq, 128), lambda b, i: (b, 0)),
                  pl.BlockSpec((1, 128, seq), lambda b, i: (b, 0, 0)),
                  pl.BlockSpec((seq, 128), lambda b, i: (b, 0))],
        out_specs=pl.BlockSpec((tq, 256), lambda b, i: (b * nq + i, 0)),
        scratch_shapes=[pltpu.VMEM((seq, tq), jnp.int32), pltpu.VMEM((seq, tq), F32),
                        pltpu.VMEM((seq, tq), jnp.int16),
                        pltpu.VMEM((tk, DSA_HEADS * tq), F32), pltpu.VMEM((tk, DSA_HEADS * tq), F32)],
        compiler_params=_params(("parallel", "parallel")),
        name="dsa",
    )(qdt, iqt, iwt, dkv, dkvt, ikw)


def _mem_kv_kernel(mem_ref, g_ref, w_ref, kt_ref, v_ref):
    mn = _rms(mem_ref[0], g_ref[...]).astype(BF16)
    kv = _dot(mn, w_ref[...])
    kt_ref[0] = kv[:, :256].T.astype(BF16)
    v_ref[0] = kv[:, 256:].astype(BF16)


def _mem_kv(mem, gain, w_kv):
    b, m, d = mem.shape
    return pl.pallas_call(
        _mem_kv_kernel,
        out_shape=(jax.ShapeDtypeStruct((b, 256, m), BF16), jax.ShapeDtypeStruct((b, m, 256), BF16)),
        grid=(b,),
        in_specs=[pl.BlockSpec((1, m, d), lambda i: (i, 0, 0)), _const_spec((1, d)), _const_spec((d, 512))],
        out_specs=(pl.BlockSpec((1, 256, m), lambda i: (i, 0, 0)), pl.BlockSpec((1, m, 256), lambda i: (i, 0, 0))),
        compiler_params=_params(("parallel",)),
        name="mem_kv",
    )(mem, gain, w_kv)


def _merge_kernel(x_ref, ya_ref, yb_ref, yc_ref, yd_ref, gmix_ref, wg_ref, wbr_ref, wout_ref,
                  gq_ref, wq_ref, mkt_ref, mv_ref, wo_ref, gffn_ref, wr_ref, br_ref,
                  x2_ref, hp_ref, rt_ref, cnt_ref, run_ref, *, tm):
    x = x_ref[...]
    d = x.shape[-1]
    h = _rms(x, gmix_ref[...]).astype(BF16)
    merged = jnp.zeros((tm, d), F32)
    for n, y_ref in enumerate((ya_ref, yb_ref, yc_ref, yd_ref)):
        gate = _sigmoid(_dot(h, wg_ref[:, n * d:(n + 1) * d]))
        merged = merged + gate * _dot(y_ref[...], wbr_ref[n])
    x1 = x + _dot(merged.astype(BF16), wout_ref[...])
    h2 = _rms(x1, gq_ref[...]).astype(BF16)
    q = (_dot(h2, wq_ref[...]) * (MEM_HEAD_DIM ** -0.5)).astype(BF16)
    lane_head = lax.broadcasted_iota(jnp.int32, (tm, 256), 1) // MEM_HEAD_DIM
    mv = mv_ref[0]
    o = jnp.zeros((tm, 256), F32)
    for hd in range(MEM_HEADS):
        s = _dot(q[:, hd * MEM_HEAD_DIM:(hd + 1) * MEM_HEAD_DIM], mkt_ref[0, hd * MEM_HEAD_DIM:(hd + 1) * MEM_HEAD_DIM, :])
        p = jnp.exp(s - jnp.max(s, axis=-1, keepdims=True))
        p = p / jnp.sum(p, axis=-1, keepdims=True)
        o = o + jnp.where(lane_head == hd, _dot(p.astype(BF16), mv), 0.0)
    x2 = x1 + _dot(o.astype(BF16), wo_ref[...])
    x2_ref[...] = x2
    h3f = _rms(x2, gffn_ref[...])
    hp_ref[...] = _pack_bf16_pairs(h3f)
    h3 = h3f.astype(BF16)
    logits = _dot(h3, wr_ref[...]) + br_ref[...]
    lane = lax.broadcasted_iota(jnp.int32, (tm, LANES), 1)
    gl = jnp.where(lane < MOE_GROUPS, logits, -jnp.inf)
    gmax = jnp.max(gl, axis=-1, keepdims=True)
    gsel = jnp.min(jnp.where(gl == gmax, lane, LANES), axis=-1, keepdims=True)
    pg_sel = 1.0 / jnp.sum(jnp.exp(gl - gmax), axis=-1, keepdims=True)
    in_group = (lane - ROUTER_EXPERT_LANE) // MOE_EXPERTS_PER_GROUP == gsel
    el = jnp.where(in_group, logits, -jnp.inf)
    m1 = jnp.max(el, axis=-1, keepdims=True)
    i1 = jnp.min(jnp.where(el == m1, lane, LANES), axis=-1, keepdims=True)
    el2 = jnp.where(lane == i1, -jnp.inf, el)
    m2 = jnp.max(el2, axis=-1, keepdims=True)
    i2 = jnp.min(jnp.where(el2 == m2, lane, LANES), axis=-1, keepdims=True)
    e21 = jnp.exp(m2 - m1)
    c1 = pg_sel / (1.0 + e21)
    @pl.when(pl.program_id(0) == 0)
    def _():
        run_ref[...] = jnp.zeros_like(run_ref)

    oh1 = jnp.where(lane == i1, 1.0, 0.0)
    oh2 = jnp.where(lane == i2, 1.0, 0.0)
    both = oh1 + oh2
    tr = lax.broadcasted_iota(jnp.int32, (tm, tm), 0)
    tc = lax.broadcasted_iota(jnp.int32, (tm, tm), 1)
    before = _dot(jnp.where(tc < tr, 1.0, 0.0).astype(BF16), both.astype(BF16)) + run_ref[0:1, :]
    r1 = jnp.sum(oh1 * before, axis=-1, keepdims=True)
    r2 = jnp.sum(oh2 * before, axis=-1, keepdims=True)
    total = run_ref[...] + jnp.sum(both, axis=0, keepdims=True)
    run_ref[...] = total
    cnt_ref[...] = total
    ids = (jnp.where(lane == 0, i1, i2) - ROUTER_EXPERT_LANE).astype(F32)
    rt_ref[...] = jnp.where(lane < 2, ids, jnp.where(lane == 2, c1, jnp.where(lane == 3, c1 * e21,
                            jnp.where(lane == 4, r1, jnp.where(lane == 5, r2, 0.0)))))


def _merge(x, ys, gmix, wg, wbr, wout, gq, wq, mkt, mv, wo, gffn, wr, br, batch, seq):
    n, d = x.shape
    tm = MERGE_TM
    spt = seq // tm
    m = mv.shape[1]
    tok = lambda w: pl.BlockSpec((tm, w), lambda i: (i, 0))
    return pl.pallas_call(
        functools.partial(_merge_kernel, tm=tm),
        out_shape=(jax.ShapeDtypeStruct((n, d), F32), jax.ShapeDtypeStruct((n, d // 2), F32),
                   jax.ShapeDtypeStruct((n, LANES), F32), jax.ShapeDtypeStruct((8, LANES), F32)),
        grid=(n // tm,),
        in_specs=[tok(d), tok(256), tok(256), tok(256), tok(256),
                  _const_spec((1, d)), _const_spec((d, N_BRANCH * d)), _const_spec((N_BRANCH, 256, d)),
                  _const_spec((d, d)), _const_spec((1, d)), _const_spec((d, 256)),
                  pl.BlockSpec((1, 256, m), lambda i: (i // spt, 0, 0)),
                  pl.BlockSpec((1, m, 256), lambda i: (i // spt, 0, 0)),
                  _const_spec((256, d)), _const_spec((1, d)), _const_spec((d, LANES)), _const_spec((1, LANES))],
        out_specs=(tok(d), tok(d // 2), tok(LANES), pl.BlockSpec((8, LANES), lambda i: (0, 0))),
        scratch_shapes=[pltpu.VMEM((8, LANES), F32)],
        compiler_params=_params(("arbitrary",)),
        name="merge_mem_router",
    )(x, *ys, gmix, wg, wbr, wout, gq, wq, mkt, mv, wo, gffn, wr, br)


def _sc_gather_rows(table, idx):
    _, width = table.shape
    total = idx.shape[0]
    chunk, nbuf = SC_GATHER_CHUNK, SC_GATHER_BUFS
    workers = SC_CORES * SC_SUBCORES
    per_w = total // workers
    nch = per_w // chunk
    assert total % (workers * chunk * nbuf) == 0
    mesh = plsc.VectorSubcoreMesh(core_axis_name="c", subcore_axis_name="s")

    @functools.partial(
        pl.kernel, mesh=mesh, out_type=jax.ShapeDtypeStruct((total, width), table.dtype),
        scratch_types=[pltpu.VMEM((nch, chunk), jnp.int32), pltpu.VMEM((nbuf, chunk, width), table.dtype),
                       pltpu.SemaphoreType.DMA((nbuf,)), pltpu.SemaphoreType.DMA((nbuf,))])
    def gather_kernel(table_hbm, idx_hbm, out_hbm, idx_v, rows_v, gsem, wsem):
        wid = lax.axis_index("s") * SC_CORES + lax.axis_index("c")
        pltpu.sync_copy(idx_hbm.at[wid], idx_v)

        def gather(j, slot):
            return pltpu.make_async_copy(table_hbm.at[idx_v.at[j]], rows_v.at[slot], gsem.at[slot])

        def write(j, slot):
            off = pl.multiple_of(wid * per_w + j * chunk, chunk)
            return pltpu.make_async_copy(rows_v.at[slot], out_hbm.at[pl.ds(off, chunk)], wsem.at[slot])

        for slot in range(nbuf):
            gather(slot, slot).start()

        @pl.loop(0, nch // nbuf)
        def _(g):
            for slot in range(nbuf):
                j = g * nbuf + slot
                gather(j, slot).wait()
                write(j, slot).start()
                write(j, slot).wait()

                @pl.when(j + nbuf < nch)
                def _():
                    gather(j + nbuf, slot).start()

    return gather_kernel(table, idx.reshape(workers, nch, chunk))


def _sc_scatter_rows(table, dest2, total):
    n, width = table.shape
    chunk, nbuf = SC_GATHER_CHUNK, SC_GATHER_BUFS
    workers = SC_CORES * SC_SUBCORES
    per_w = n // workers
    nch = per_w // chunk
    assert n % (workers * chunk * nbuf) == 0
    mesh = plsc.VectorSubcoreMesh(core_axis_name="c", subcore_axis_name="s")

    @functools.partial(
        pl.kernel, mesh=mesh, out_type=jax.ShapeDtypeStruct((total, width), table.dtype),
        scratch_types=[pltpu.VMEM((2, nch, chunk), jnp.int32), pltpu.VMEM((nbuf, chunk, width), table.dtype),
                       pltpu.SemaphoreType.DMA((nbuf,)), pltpu.SemaphoreType.DMA((nbuf,))])
    def scatter_kernel(table_hbm, idx_hbm, out_hbm, idx_v, rows_v, rsem, wsem):
        wid = lax.axis_index("s") * SC_CORES + lax.axis_index("c")
        pltpu.sync_copy(idx_hbm.at[wid], idx_v)

        def read(j, slot):
            off = pl.multiple_of(wid * per_w + j * chunk, chunk)
            return pltpu.make_async_copy(table_hbm.at[pl.ds(off, chunk)], rows_v.at[slot], rsem.at[slot])

        def write(j, slot, k):
            return pltpu.make_async_copy(rows_v.at[slot], out_hbm.at[idx_v.at[k, j]], wsem.at[slot])

        for slot in range(nbuf):
            read(slot, slot).start()

        @pl.loop(0, nch // nbuf)
        def _(g):
            for slot in range(nbuf):
                j = g * nbuf + slot
                read(j, slot).wait()
                write(j, slot, 0).start()
                write(j, slot, 1).start()
                write(j, slot, 0).wait()
                write(j, slot, 1).wait()

                @pl.when(j + nbuf < nch)
                def _():
                    read(j + nbuf, slot).start()

    idx = dest2.reshape(2, workers, nch, chunk).transpose(1, 0, 2, 3)
    return scatter_kernel(table, idx)


def _dispatch_plan(rt, cnt, n):
    ne, blk = MOE_N_EXPERTS, MOE_BLOCK
    n_blocks = (2 * n) // blk + ne
    experts = jnp.arange(ne, dtype=jnp.int32)
    counts = cnt[0, ROUTER_EXPERT_LANE:ROUTER_EXPERT_LANE + ne].astype(jnp.int32)
    padded = (counts + blk - 1) // blk * blk
    pend = jnp.cumsum(padded)
    pstart = pend - padded
    ids = rt[:, 0:2].astype(jnp.int32)
    pos = rt[:, 4:6].astype(jnp.int32)
    first_row = jnp.sum(jnp.where(ids[:, :, None] == experts[None, None, :], pstart[None, None, :], 0), axis=-1)
    dest2 = (first_row + pos).T
    b0 = jnp.arange(n_blocks, dtype=jnp.int32) * blk
    block_e = jnp.minimum(jnp.sum((pend[None, :] <= b0[:, None]).astype(jnp.int32), axis=1), ne - 1)
    n_valid = jnp.clip(counts[block_e] - (b0 - pstart[block_e]), 0, blk).astype(jnp.int32)
    return dest2, block_e, n_valid


def _expert_block_kernel(be_ref, nv_ref, xs_ref, wg_ref, wu_ref, wd_ref, o_ref, wgb_ref, wub_ref, wdb_ref):
    b = pl.program_id(0)
    valid = nv_ref[b]

    @pl.when((b == 0) | (be_ref[b] != be_ref[jnp.maximum(b - 1, 0)]))
    def _():
        wgb_ref[...] = wg_ref[0].astype(BF16)
        wub_ref[...] = wu_ref[0].astype(BF16)
        wdb_ref[...] = wd_ref[0].astype(BF16)

    @pl.when(valid > 0)
    def _():
        row = lax.broadcasted_iota(jnp.int32, xs_ref.shape, 0)
        words = jnp.where(row < valid, xs_ref[...], 0.0)
        h = _unpack_bf16_pairs(words).astype(BF16)
        gt = _dot(h, wgb_ref[...])
        hid = gt * _sigmoid(gt) * _dot(h, wub_ref[...])
        o_ref[...] = _pack_bf16_pairs(_dot(hid.astype(BF16), wdb_ref[...]))

    @pl.when(valid == 0)
    def _():
        o_ref[...] = jnp.zeros_like(o_ref)


def _expert_blocks(xs, wg, wu, wd, layer, block_e, n_used):
    p_rows, half = xs.shape
    d, hid = wg.shape[-2:]
    blk = MOE_BLOCK
    grid_spec = pltpu.PrefetchScalarGridSpec(
        num_scalar_prefetch=2, grid=(p_rows // blk,),
        in_specs=[pl.BlockSpec((blk, half), lambda b, be, nu: (b, 0)),
                  pl.BlockSpec((None, 1, d, hid), lambda b, be, nu: (layer, be[b], 0, 0)),
                  pl.BlockSpec((None, 1, d, hid), lambda b, be, nu: (layer, be[b], 0, 0)),
                  pl.BlockSpec((None, 1, hid, d), lambda b, be, nu: (layer, be[b], 0, 0))],
        out_specs=pl.BlockSpec((blk, half), lambda b, be, nu: (b, 0)),
        scratch_shapes=[pltpu.VMEM((d, hid), BF16), pltpu.VMEM((d, hid), BF16), pltpu.VMEM((hid, d), BF16)])
    return pl.pallas_call(
        _expert_block_kernel, out_shape=jax.ShapeDtypeStruct((p_rows, half), F32), grid_spec=grid_spec,
        compiler_params=_params(("arbitrary",)),
        name="moe_expert_blocks",
    )(block_e, n_used, xs, wg, wu, wd)


def _combine_kernel(x_ref, y1_ref, y2_ref, rt_ref, gfin_ref, o_ref, *, final_norm):
    out = (x_ref[...] + rt_ref[:, 2:3] * _unpack_bf16_pairs(y1_ref[...])
           + rt_ref[:, 3:4] * _unpack_bf16_pairs(y2_ref[...]))
    o_ref[...] = _rms(out, gfin_ref[...]) if final_norm else out


def _combine(x2, y_halves, rt, gfin, final_norm):
    n, d = x2.shape
    tm = COMBINE_TM
    nt = n // tm
    return pl.pallas_call(
        functools.partial(_combine_kernel, final_norm=final_norm),
        out_shape=jax.ShapeDtypeStruct((n, d), F32),
        grid=(nt,),
        in_specs=[pl.BlockSpec((tm, d), lambda i: (i, 0)),
                  pl.BlockSpec((tm, d // 2), lambda i: (i, 0)),
                  pl.BlockSpec((tm, d // 2), lambda i: (i + nt, 0)),
                  pl.BlockSpec((tm, LANES), lambda i: (i, 0)), _const_spec((1, d))],
        out_specs=pl.BlockSpec((tm, d), lambda i: (i, 0)),
        compiler_params=_params(("parallel",)), name="moe_combine",
    )(x2, y_halves, y_halves, rt, gfin)


def _moe(x2, hp, rt, cnt, wg, wu, wd, layer, gfin, final_norm):
    n = x2.shape[0]
    dest2, block_e, n_valid = _dispatch_plan(rt, cnt, n)
    xs = _sc_scatter_rows(hp, dest2, block_e.shape[0] * MOE_BLOCK)
    yb = _expert_blocks(xs, wg, wu, wd, layer, block_e, n_valid)
    y_halves = _sc_gather_rows(yb, dest2.reshape(2 * n))
    return _combine(x2, y_halves, rt, gfin, final_norm)


def kernel(x, mem, positions, norm_mix, w_in, diff_lambda, hgrn_lb_logits, spatial_w, spatial_b, w_branch, w_out,
           norm_mem_q, norm_mem_kv, w_mem_q, w_mem_kv, w_mem_o, norm_ffn, w_router_group, b_router_group,
           w_router_expert, b_router_expert, w_exp_gate, w_exp_up, w_exp_down, norm_final):
    batch, seq, d = x.shape
    depth = w_in.shape[0]
    n = batch * seq
    xf = x.reshape(n, d)
    tabs = _rope_tables(positions)
    row = lambda v: v.reshape(1, -1).astype(F32)
    for l in range(depth):
        lam_init = 0.8 - 0.6 * math.exp(-0.3 * l)
        w1, w_gate = _split_w_in(w_in, l)
        sw = spatial_w[l].reshape(SGU_GROUPS * SGU_CHUNK, SGU_CHUNK)
        sb = jnp.repeat(spatial_b[l].T, SGU_GROUP_DIM, axis=1)
        qat, ka, vat, hb, y_c, qdt, iqt, dkv, dkvt, ikw, iwt = _projection(
            xf, row(norm_mix[l]), w1, tabs, sw, sb, batch, seq)
        y_a = _diff_attention(diff_lambda[l], qat, ka, vat, lam_init, batch, seq)
        y_b = _hgrn(hgrn_lb_logits, hb, l, batch, seq)
        y_d = _dsa(qdt, iqt, iwt, dkv, dkvt, ikw, batch, seq)
        mkt, mv = _mem_kv(mem, row(norm_mem_kv[l]), w_mem_kv[l].astype(BF16))
        e0, e1 = ROUTER_EXPERT_LANE, ROUTER_EXPERT_LANE + MOE_N_EXPERTS
        wr = jnp.zeros((d, LANES), F32)
        wr = wr.at[:, :MOE_GROUPS].set(w_router_group[l]).at[:, e0:e1].set(w_router_expert[l]).astype(BF16)
        br = jnp.zeros((1, LANES), F32)
        br = br.at[0, :MOE_GROUPS].set(b_router_group[l]).at[0, e0:e1].set(b_router_expert[l])
        x2, hp, rt, cnt = _merge(xf, (y_a, y_b, y_c, y_d), row(norm_mix[l]), w_gate, w_branch[l].astype(BF16),
                                 w_out[l].astype(BF16), row(norm_mem_q[l]), w_mem_q[l].astype(BF16), mkt, mv,
                                 w_mem_o[l].astype(BF16), row(norm_ffn[l]), wr, br, batch, seq)
        xf = _moe(x2, hp, rt, cnt, w_exp_gate, w_exp_up, w_exp_down, l,
                  row(norm_final), final_norm=(l == depth - 1))
    return xf.reshape(batch, seq, d)
```

```python
import functools
import math

import numpy as np
import jax
import jax.numpy as jnp
from jax import lax
from jax.experimental import pallas as pl
from jax.experimental.pallas import tpu as pltpu
from jax.experimental.pallas import tpu_sc as plsc

F32 = jnp.float32
BF16 = jnp.bfloat16

NORM_EPS = 1e-6
ROPE_THETA = 10000.0
NEG_BIG = -1e30

N_BRANCH = 4
DIFF_HEADS = 4
DIFF_HEAD_DIM = 32
HGRN_DIM = 64
HGRN_CHUNK = 32
HGRN_UNROLL = 8
HGRN_MIN_FORGET = 1e-30
SGU_GROUPS = 4
SGU_GROUP_DIM = 64
SGU_CHUNK = 128
DSA_HEADS = 4
DSA_HEAD_DIM = 64
DSA_IDX_HEADS = 4
DSA_IDX_DIM = 32
DSA_TOPK = 256
MEM_HEADS = 4
MEM_HEAD_DIM = 64
MOE_GROUPS = 4
MOE_EXPERTS_PER_GROUP = 8
MOE_N_EXPERTS = 32
MOE_BLOCK = 512
ROUTER_EXPERT_LANE = 32
SC_CORES = 2
SC_SUBCORES = 16
SC_GATHER_CHUNK = 16
SC_GATHER_BUFS = 4

LANES = 128
VMEM_LIMIT = 56 * 1024 * 1024

PROJ_TM = 1024
DIFF_TQ = 256
DIFF_TK = 512
HGRN_TC = 512
DSA_TQ = 512
DSA_TK = 512
MERGE_TM = 1024
COMBINE_TM = 512
ROPE_TM = 1024

C_AQ, C_AK, C_AV = 0, 256, 512
C_HB = 768
C_UV = 1792
C_DQ = 2304
C_DKV = 2560
C_IQ = 2688
C_IKW = 2816
IW_LANE = 32
C_GATE = 2852
C_TOTAL = 2944
LOG2E = math.log2(math.e)


def _params(sem):
    return pltpu.CompilerParams(dimension_semantics=sem, vmem_limit_bytes=VMEM_LIMIT)


def _const_spec(shape):
    nd = len(shape)
    return pl.BlockSpec(shape, lambda *_: (0,) * nd, pipeline_mode=pl.Buffered(1))


def _rms(xf, gain=None):
    y = xf * lax.rsqrt(jnp.mean(xf * xf, axis=-1, keepdims=True) + NORM_EPS)
    return y if gain is None else y * gain


def _sigmoid(x):
    return 0.5 * jnp.tanh(0.5 * x) + 0.5


def _pack_bf16_pairs(x):
    w = x.shape[-1] // 2
    xb = x.astype(BF16).astype(F32)
    lo = lax.shift_right_logical(pltpu.bitcast(xb[:, :w], jnp.int32), 16)
    hi = pltpu.bitcast(xb[:, w:], jnp.int32) & jnp.int32(-65536)
    return pltpu.bitcast(hi | lo, F32)


def _unpack_bf16_pairs(words):
    bits = pltpu.bitcast(words, jnp.int32)
    lo = pltpu.bitcast(bits << 16, F32)
    hi = pltpu.bitcast(bits & jnp.int32(-65536), F32)
    return jnp.concatenate([lo, hi], axis=1)


def _dot(a, b):
    return jnp.dot(a, b, preferred_element_type=F32)


def _dot_nt(a, b):
    return lax.dot_general(a, b, (((1,), (1,)), ((), ())), preferred_element_type=F32)


def _rope_table_kernel(pos_ref, frq_ref, sgn_ref, c32_ref, s32_ref, c64_ref, s64_ref):
    pos = pos_ref[...].astype(F32)
    twice = lambda t: jnp.concatenate([t, t], axis=1)
    a32 = pos * frq_ref[0:1, :LANES]
    a64 = pos * frq_ref[1:2, :LANES]
    c32_ref[...] = twice(jnp.cos(a32))
    s32_ref[...] = twice(jnp.sin(a32) * sgn_ref[0:1, :LANES])
    c64_ref[...] = twice(jnp.cos(a64))
    s64_ref[...] = twice(jnp.sin(a64) * sgn_ref[1:2, :LANES])


def _rope_tables(positions):
    n = positions.size
    pos = positions.reshape(n, 1).astype(jnp.int32)
    lane = np.arange(256)
    inv32 = ROPE_THETA ** (-jnp.arange(16, dtype=F32) * (2.0 / 32))
    inv64 = ROPE_THETA ** (-jnp.arange(32, dtype=F32) * (2.0 / 64))
    frq = jnp.stack([inv32[lane % 16], inv64[lane % 32]])
    sgn = jnp.asarray(np.stack([np.where(lane % 32 < 16, -1.0, 1.0),
                                np.where(lane % 64 < 32, -1.0, 1.0)]), F32)
    tm = ROPE_TM
    tab = jax.ShapeDtypeStruct((n, 256), F32)
    return pl.pallas_call(
        _rope_table_kernel,
        out_shape=(tab, tab, tab, tab),
        grid=(n // tm,),
        in_specs=[pl.BlockSpec((tm, 1), lambda i: (i, 0)), _const_spec((2, 256)), _const_spec((2, 256))],
        out_specs=tuple(pl.BlockSpec((tm, 256), lambda i: (i, 0)) for _ in range(4)),
        compiler_params=_params(("parallel",)),
        name="rope_tables",
    )(pos, frq, sgn)


def _w1_kernel(wt_ref, o_ref, *, layer):
    o_ref[...] = wt_ref[:, layer, :].T.astype(BF16)


def _wgate_kernel(a_ref, b_ref, o_ref, *, layer):
    off = C_GATE % LANES
    o_ref[...] = jnp.concatenate([a_ref[off:, layer, :], b_ref[:off, layer, :]], axis=0).T.astype(BF16)


def _split_w_in(w_in, layer):
    depth, d, width = w_in.shape
    wt = jnp.transpose(w_in, (2, 0, 1))
    rows = lambda f: pl.BlockSpec((LANES, depth, d), lambda i: (f(i), 0, 0))
    cols = pl.BlockSpec((d, LANES), lambda i: (0, i))
    w1 = pl.pallas_call(
        functools.partial(_w1_kernel, layer=layer), out_shape=jax.ShapeDtypeStruct((d, C_TOTAL), BF16),
        grid=(C_TOTAL // LANES,), in_specs=[rows(lambda i: i)], out_specs=cols,
        compiler_params=_params(("parallel",)), name="split_w_in",
    )(wt)
    a0 = C_GATE // LANES
    gate_w = width - C_GATE
    wg = pl.pallas_call(
        functools.partial(_wgate_kernel, layer=layer), out_shape=jax.ShapeDtypeStruct((d, gate_w), BF16),
        grid=(gate_w // LANES,), in_specs=[rows(lambda i: a0 + i), rows(lambda i: a0 + i + 1)], out_specs=cols,
        compiler_params=_params(("parallel",)), name="split_w_gate",
    )(wt, wt)
    return w1, wg


def _gelu_tanh(x):
    return 0.5 * x * (1.0 + jnp.tanh(math.sqrt(2.0 / math.pi) * (x + 0.044715 * (x * x * x))))


def _rope(x, cos, sin_signed, half):
    w = x.shape[-1]
    lane = lax.broadcasted_iota(jnp.int32, x.shape, 1)
    partner = jnp.where(lane % (2 * half) < half, pltpu.roll(x, w - half, 1), pltpu.roll(x, half, 1))
    return x * cos + partner * sin_signed


def _proj_kernel(x_ref, g_ref, w_ref, c32_ref, s32_ref, c64_ref, s64_ref, sw_ref, sb_ref,
                 qat_ref, ka_ref, vat_ref, hb_ref, yc_ref, qdt_ref, iqt_ref, dkv_ref, dkvt_ref, ikw_ref, iwt_ref,
                 *, tm):
    h = _rms(x_ref[...], g_ref[...]).astype(BF16)

    def proj(c0, width):
        return _dot(h, w_ref[:, c0:c0 + width])

    c32, s32, c64, s64 = c32_ref[...], s32_ref[...], c64_ref[...], s64_ref[...]
    qat_ref[0] = (_rope(proj(C_AQ, 256), c32, s32, 16) * (DIFF_HEAD_DIM ** -0.5 * LOG2E)).T.astype(BF16)
    ka_ref[...] = _rope(proj(C_AK, 256), c32, s32, 16).astype(BF16)
    vat_ref[0] = proj(C_AV, 256).astype(BF16).T
    hb_ref[...] = proj(C_HB, 1024)
    qdt_ref[0] = (_rope(proj(C_DQ, 256), c64, s64, 32) * (DSA_HEAD_DIM ** -0.5 * LOG2E)).T.astype(BF16)
    iqt_ref[0] = _rope(proj(C_IQ, 128), c32[:, :128], s32[:, :128], 16).T.astype(BF16)
    lane = lax.broadcasted_iota(jnp.int32, (tm, 128), 1)
    is_k = lane < DSA_HEAD_DIM
    dkv = _rope(proj(C_DKV, 128), jnp.where(is_k, c64[:, :128], 1.0), jnp.where(is_k, s64[:, :128], 0.0), 32)
    is_ik = lane < DSA_IDX_DIM
    ikw = _rope(proj(C_IKW, 128), jnp.where(is_ik, c32[:, :128], 1.0), jnp.where(is_ik, s32[:, :128], 0.0), 16)
    dkv_ref[...] = dkv.astype(BF16)
    dkvt_ref[0] = dkv.T.astype(BF16)
    ikw_ref[...] = ikw.astype(BF16)
    iw_scale = DSA_IDX_HEADS ** -0.5 * DSA_IDX_DIM ** -0.5
    iwt_ref[0] = (ikw * iw_scale).T[IW_LANE:IW_LANE + 8, :]
    uv = _gelu_tanh(proj(C_UV, 512))
    u, v = uv[:, :256], uv[:, 256:]
    mu = jnp.mean(v, axis=-1, keepdims=True)
    vc = v - mu
    vn = (vc * lax.rsqrt(jnp.mean(vc * vc, axis=-1, keepdims=True) + NORM_EPS)).astype(BF16)
    r = lax.broadcasted_iota(jnp.int32, (SGU_GROUPS * SGU_CHUNK, SGU_CHUNK), 0)
    c = lax.broadcasted_iota(jnp.int32, (SGU_GROUPS * SGU_CHUNK, SGU_CHUNK), 1)
    wt = jnp.where((r % SGU_CHUNK) >= c, sw_ref[...], 0.0).astype(BF16)
    lane_grp = lax.broadcasted_iota(jnp.int32, (SGU_CHUNK, 256), 1) // SGU_GROUP_DIM
    for ch in range(tm // SGU_CHUNK):
        r0 = ch * SGU_CHUNK
        full = _dot(wt, vn[r0:r0 + SGU_CHUNK, :])
        mixed = sb_ref[...]
        for g in range(SGU_GROUPS):
            mixed = mixed + jnp.where(lane_grp == g, full[g * SGU_CHUNK:(g + 1) * SGU_CHUNK, :], 0.0)
        yc_ref[r0:r0 + SGU_CHUNK, :] = (u[r0:r0 + SGU_CHUNK, :] * mixed).astype(BF16)


def _projection(x, gain, w1, tabs, sw, sb, batch, seq):
    n, d = x.shape
    tm = PROJ_TM
    spt = seq // tm
    tok = lambda w: pl.BlockSpec((tm, w), lambda i: (i, 0))
    tr = lambda rows: pl.BlockSpec((1, rows, tm), lambda i: (i // spt, 0, i % spt))
    out_shape = (
        jax.ShapeDtypeStruct((batch, 256, seq), BF16),
        jax.ShapeDtypeStruct((n, 256), BF16),
        jax.ShapeDtypeStruct((batch, 256, seq), BF16),
        jax.ShapeDtypeStruct((n, 1024), F32),
        jax.ShapeDtypeStruct((n, 256), BF16),
        jax.ShapeDtypeStruct((batch, 256, seq), BF16),
        jax.ShapeDtypeStruct((batch, 128, seq), BF16),
        jax.ShapeDtypeStruct((n, 128), BF16),
        jax.ShapeDtypeStruct((batch, 128, seq), BF16),
        jax.ShapeDtypeStruct((n, 128), BF16),
        jax.ShapeDtypeStruct((batch, 8, seq), F32),
    )
    return pl.pallas_call(
        functools.partial(_proj_kernel, tm=tm),
        out_shape=out_shape,
        grid=(n // tm,),
        in_specs=[tok(d), _const_spec((1, d)), _const_spec((d, C_TOTAL)),
                  tok(256), tok(256), tok(256), tok(256),
                  _const_spec((SGU_GROUPS * SGU_CHUNK, SGU_CHUNK)), _const_spec((SGU_CHUNK, 256))],
        out_specs=(tr(256), tok(256), tr(256), tok(1024), tok(256), tr(256), tr(128), tok(128), tr(128), tok(128), tr(8)),
        compiler_params=_params(("parallel",)),
        name="projection",
    )(x, gain, w1, *tabs, sw, sb)


def _diff_attn_kernel(lam_ref, qt_ref, k_ref, vt_ref, o_ref, sa_ref, sb_ref, *, lam_init, tq, tk):
    q0 = pl.program_id(1) * tq
    kb_diag = q0 // tk
    lv = lam_ref[...]
    lam = (jnp.exp(jnp.sum(lv[0:1] * lv[1:2], axis=-1, keepdims=True))
           - jnp.exp(jnp.sum(lv[2:3] * lv[3:4], axis=-1, keepdims=True)) + lam_init)
    qt = qt_ref[0]
    feat = lax.broadcasted_iota(jnp.int32, (256, tq), 0) // DIFF_HEAD_DIM
    n_maps = 2 * DIFF_HEADS
    qz = jnp.concatenate([jnp.where(feat == i, qt, jnp.zeros_like(qt)) for i in range(n_maps)], axis=1)
    wide = n_maps * tq
    key_i = lax.broadcasted_iota(jnp.int32, (tk, wide), 0)
    qry_i = q0 + lax.broadcasted_iota(jnp.int32, (tk, wide), 1) % tq

    def scores(kb, dst_ref):
        k0 = pl.multiple_of(kb * tk, tk)
        dst_ref[...] = _dot(k_ref[pl.ds(k0, tk), :], qz)

    def absorb(src_ref, kb, carry, masked):
        m_i, l_i, acc = carry
        k0 = pl.multiple_of(kb * tk, tk)
        s = src_ref[...]
        if masked:
            s = jnp.where(k0 + key_i <= qry_i, s, NEG_BIG)
        m_new = jnp.maximum(m_i, jnp.max(s, axis=0, keepdims=True))
        p = jnp.exp2(s - m_new)
        alpha = jnp.exp2(m_i - m_new)
        l_new = alpha * l_i + jnp.sum(p, axis=0, keepdims=True)
        pb = p.astype(BF16)
        pv = jnp.concatenate(
            [_dot(vt_ref[0, hd * 64:(hd + 1) * 64, pl.ds(k0, tk)], pb[:, 2 * hd * tq:(2 * hd + 2) * tq])
             for hd in range(DIFF_HEADS)], axis=1)
        return m_new, l_new, alpha * acc + pv

    def pair(j, carry):
        kb = 2 * j
        scores(kb + 1, sb_ref)
        carry = absorb(sa_ref, kb, carry, False)
        scores(kb + 2, sa_ref)
        return absorb(sb_ref, kb + 1, carry, False)

    init = (jnp.full((1, wide), NEG_BIG, F32), jnp.zeros((1, wide), F32), jnp.zeros((64, wide), F32))
    scores(0, sa_ref)
    carry = lax.fori_loop(0, kb_diag // 2, pair, init)

    def tail_odd(carry):
        scores(kb_diag, sb_ref)
        carry = absorb(sa_ref, kb_diag - 1, carry, False)
        return absorb(sb_ref, kb_diag, carry, True)

    def tail_even(carry):
        return absorb(sa_ref, kb_diag, carry, True)

    _, l_f, acc = lax.cond(kb_diag % 2 == 1, tail_odd, tail_even, carry)
    o_all = acc / l_f
    heads = []
    for hd in range(DIFF_HEADS):
        o0 = o_all[:, 2 * hd * tq:(2 * hd + 1) * tq]
        o1 = o_all[:, (2 * hd + 1) * tq:(2 * hd + 2) * tq]
        o_h = o0 - lam * o1
        ms = jnp.mean(o_h * o_h, axis=0, keepdims=True)
        heads.append(o_h * lax.rsqrt(ms + NORM_EPS) * (1.0 - lam_init))
    o_ref[...] = jnp.concatenate(heads, axis=0).T.astype(BF16)


def _diff_attention(lam_vec, qat, ka, vat, lam_init, batch, seq):
    tq, tk = DIFF_TQ, DIFF_TK
    nq = seq // tq
    return pl.pallas_call(
        functools.partial(_diff_attn_kernel, lam_init=lam_init, tq=tq, tk=tk),
        out_shape=jax.ShapeDtypeStruct((batch * seq, 256), BF16),
        grid=(batch, nq),
        in_specs=[_const_spec((4, DIFF_HEAD_DIM)),
                  pl.BlockSpec((1, 256, tq), lambda b, i: (b, 0, i)),
                  pl.BlockSpec((seq, 256), lambda b, i: (b, 0)),
                  pl.BlockSpec((1, 256, seq), lambda b, i: (b, 0, 0))],
        out_specs=pl.BlockSpec((tq, 256), lambda b, i: (b * nq + i, 0)),
        scratch_shapes=[pltpu.VMEM((tk, 2 * DIFF_HEADS * tq), F32), pltpu.VMEM((tk, 2 * DIFF_HEADS * tq), F32)],
        compiler_params=_params(("parallel", "parallel")),
        name="diff_attention",
    )(lam_vec, qat, ka, vat)


def _hgrn_kernel(lbl_ref, hb_ref, o_ref, st_ref, pstk_ref, *, layer, tc):
    cz = HGRN_CHUNK
    w = 256

    @pl.when(pl.program_id(1) == 0)
    def _():
        st_ref[...] = jnp.zeros_like(st_ref)

    lg = lbl_ref[...]
    e = jnp.exp(lg - jnp.max(lg, axis=0, keepdims=True))
    lw = e / jnp.sum(e, axis=0, keepdims=True)
    lb = jnp.sum(lw[0:layer + 1], axis=0, keepdims=True) - lw[0:1]

    ri = lax.broadcasted_iota(jnp.int32, (cz, cz), 0)
    ci = lax.broadcasted_iota(jnp.int32, (cz, cz), 1)
    tri = (ri >= ci).astype(F32)
    rb = lax.broadcasted_iota(jnp.int32, (w, w), 0) // HGRN_DIM
    cb = lax.broadcasted_iota(jnp.int32, (w, w), 1) // HGRN_DIM
    same_head = rb == cb
    head_ones = same_head.astype(BF16)
    trows = {r: r + lax.broadcasted_iota(jnp.int32, (cz - r, w), 0) for r in range(0, cz, 16)}

    def chunk(c, carry):
        r0 = pl.multiple_of(c * cz, cz)
        q = hb_ref[pl.ds(r0, cz), 0:256]
        fp = hb_ref[pl.ds(r0, cz), 256:512]
        v = hb_ref[pl.ds(r0, cz), 512:768]
        g = hb_ref[pl.ds(r0, cz), 768:1024]
        qf = q * _sigmoid(q)
        f = lb + (1.0 - lb) * jax.nn.sigmoid(fp)
        log_f = jnp.log(jnp.maximum(f, HGRN_MIN_FORGET))
        kf = (1.0 - lb) * jax.nn.sigmoid(-fp)
        bc = jnp.dot(tri, log_f, preferred_element_type=F32, precision=lax.Precision.HIGHEST)
        st = st_ref[...]
        o = _dot_nt((qf * jnp.exp(bc)).astype(BF16), st.astype(BF16))
        for s in range(cz):
            r_lo = (s // 16) * 16
            arg = bc[r_lo:, :] - bc[s:s + 1, :]
            if s > r_lo:
                arg = jnp.where(trows[r_lo] >= s, arg, NEG_BIG)
            p = qf[r_lo:, :] * kf[s:s + 1, :] * jnp.exp(arg)
            if r_lo:
                pstk_ref[s * cz:s * cz + r_lo, :] = jnp.zeros((r_lo, w), BF16)
            pstk_ref[s * cz + r_lo:(s + 1) * cz, :] = p.astype(BF16)
        accs = [jnp.zeros((16, w), F32) for _ in range(cz // 16)]
        for sg in range(cz // 16):
            att = _dot(pstk_ref[sg * 16 * cz:(sg + 1) * 16 * cz, :], head_ones)
            for sl in range(16):
                s = sg * 16 + sl
                for j in range(sg, cz // 16):
                    accs[j] = accs[j] + att[sl * cz + 16 * j:sl * cz + 16 * j + 16, :] * v[s:s + 1, :]
        o = o + jnp.concatenate(accs, axis=0)
        b_end = bc[cz - 1:cz, :]
        kd = kf * jnp.exp(b_end - bc)
        upd = _dot(v.T.astype(BF16), kd.astype(BF16))
        st_ref[...] = st * jnp.exp(b_end) + jnp.where(same_head, upd, 0.0)
        ms = _dot(o * o, head_ones.astype(F32)) * (1.0 / HGRN_DIM)
        y = o * lax.rsqrt(ms + NORM_EPS)
        o_ref[pl.ds(r0, cz), :] = (y * (g * _sigmoid(g))).astype(BF16)
        return carry

    def group(gi, carry):
        for u in range(HGRN_UNROLL):
            chunk(gi * HGRN_UNROLL + u, carry)
        return carry

    lax.fori_loop(0, tc // cz // HGRN_UNROLL, group, 0)


def _hgrn(lb_logits, hb, layer, batch, seq):
    tc = HGRN_TC
    nt = seq // tc
    cz = HGRN_CHUNK
    return pl.pallas_call(
        functools.partial(_hgrn_kernel, layer=layer, tc=tc),
        out_shape=jax.ShapeDtypeStruct((batch * seq, 256), BF16),
        grid=(batch, nt),
        in_specs=[_const_spec(lb_logits.shape),
                  pl.BlockSpec((tc, 1024), lambda b, i: (b * nt + i, 0))],
        out_specs=pl.BlockSpec((tc, 256), lambda b, i: (b * nt + i, 0)),
        scratch_shapes=[pltpu.VMEM((256, 256), F32), pltpu.VMEM((cz * cz, 256), BF16)],
        compiler_params=_params(("parallel", "arbitrary")),
        name="hgrn2",
    )(lb_logits, hb)


def _dsa_kernel(qdt_ref, iqt_ref, iwt_ref, dkv_ref, dkvt_ref, ikw_ref, o_ref, key_ref, bias_ref, half_ref,
                sa_ref, sb_ref, *, tq, tk, n_sel):
    q0 = pl.program_id(1) * tq
    nkb = q0 // tk + 1
    key_i = lax.broadcasted_iota(jnp.int32, (tk, tq), 0)
    qry_i = q0 + lax.broadcasted_iota(jnp.int32, (tk, tq), 1)
    grp = tk // 8
    rows8 = lambda x: x.reshape(grp, 8, tq)
    iqt = iqt_ref[0]
    zpad = jnp.zeros((LANES - DSA_IDX_DIM, tq), BF16)
    iqz = jnp.concatenate([jnp.concatenate([iqt[hd * DSA_IDX_DIM:(hd + 1) * DSA_IDX_DIM, :], zpad], axis=0)
                           for hd in range(DSA_IDX_HEADS)], axis=1)
    iw = iwt_ref[0]

    last = nkb - 1

    def score_block(kb, carry, masked):
        k0 = pl.multiple_of(kb * tk, tk)
        sh = jnp.maximum(_dot(ikw_ref[pl.ds(k0, tk), :], iqz), 0.0)
        sc = jnp.zeros((tk, tq), F32)
        for hd in range(DSA_IDX_HEADS):
            sc = sc + sh[:, hd * tq:(hd + 1) * tq] * iw[hd:hd + 1, :]
        sc = sc + 0.0
        if masked:
            sc = jnp.where(k0 + key_i <= qry_i, sc, -jnp.inf)
        bits = pltpu.bitcast(sc, jnp.int32)
        key = jnp.where(bits < 0, bits ^ jnp.int32(0x7FFFFFFF), bits)
        key_ref[pl.ds(k0, tk), :] = key
        half_ref[pl.ds(k0, tk), :] = (key >> 16).astype(jnp.int16)
        return carry

    lax.fori_loop(0, last, functools.partial(score_block, masked=False), 0)
    score_block(last, 0, True)

    one16, zero16 = jnp.ones((), jnp.int16), jnp.zeros((), jnp.int16)
    low16 = np.int16(-2 ** 15)

    def count16(limit, strict):
        def body(kb, acc):
            k0 = pl.multiple_of(kb * tk, tk)
            for c in range(tk // 128):
                blk = half_ref[pl.ds(k0 + 128 * c, 128), :].reshape(8, 16, tq)
                hit = jnp.where(blk > limit if strict else blk >= limit, one16, zero16)
                parts = [hit[j] for j in range(8)]
                while len(parts) > 1:
                    parts = [a + b for a, b in zip(parts[0::2], parts[1::2])]
                acc = acc + parts[0]
            return acc
        acc = lax.fori_loop(0, nkb, body, jnp.zeros((16, tq), jnp.int16))
        return jnp.broadcast_to(jnp.sum(acc.astype(jnp.int32), axis=0, keepdims=True), (16, tq))

    def search16(need):
        t = jnp.full((16, tq), -2 ** 15, jnp.int32)
        for bit in range(15, -1, -1):
            trial = t + 2 ** bit
            t = jnp.where(count16(trial.astype(jnp.int16), False) >= need, trial, t)
        return t

    t_hi = search16(n_sel)
    t_hi16 = t_hi.astype(jnp.int16)
    need_lo = n_sel - count16(t_hi16, True)

    def low_block(kb, carry):
        k0 = pl.multiple_of(kb * tk, tk)
        lo = ((key_ref[pl.ds(k0, tk), :] & 0xFFFF) - 2 ** 15).astype(jnp.int16).reshape(tk // 16, 16, tq)
        hi = half_ref[pl.ds(k0, tk), :].reshape(tk // 16, 16, tq)
        half_ref[pl.ds(k0, tk), :] = jnp.where(hi == t_hi16, lo, low16).reshape(tk, tq)
        return carry

    lax.fori_loop(0, nkb, low_block, 0)
    t_lo = search16(need_lo)
    thr = ((t_hi << 16) | (t_lo + 2 ** 15))[0:8, :]

    need = (need_lo - count16(t_lo.astype(jnp.int16), True))[0:8, :].astype(F32)
    ur = lax.broadcasted_iota(jnp.int32, (LANES, LANES), 0)
    uc = lax.broadcasted_iota(jnp.int32, (LANES, LANES), 1)
    earlier = (uc < ur).astype(BF16)
    ones8 = jnp.ones((8, LANES), BF16)
    key_s = lax.broadcasted_iota(jnp.int32, (LANES, tq), 0)
    qry_s = q0 + lax.broadcasted_iota(jnp.int32, (LANES, tq), 1)

    def select_block(kb, seen, masked):
        k0 = pl.multiple_of(kb * tk, tk)
        for g in range(tk // LANES):
            r0 = g * LANES
            blk = key_ref[pl.ds(k0 + r0, LANES), :].reshape(LANES // 8, 8, tq)
            eq = blk == thr[None]
            eqb = jnp.where(eq, 1.0, 0.0).reshape(LANES, tq).astype(BF16)
            rank = _dot(earlier, eqb).reshape(LANES // 8, 8, tq) + seen[None]
            tie_bias = jnp.where(rank < need[None], 0.0, NEG_BIG)
            bias = jnp.where(blk > thr[None], 0.0, jnp.where(eq, tie_bias, NEG_BIG)).reshape(LANES, tq)
            if masked:
                bias = jnp.where(k0 + r0 + key_s <= qry_s, bias, NEG_BIG)
            bias_ref[pl.ds(k0 + r0, LANES), :] = bias
            seen = seen + _dot(ones8, eqb)
        return seen

    seen = lax.fori_loop(0, last, functools.partial(select_block, masked=False), jnp.zeros((8, tq), F32))
    select_block(last, seen, True)

    qdt = qdt_ref[0]
    zq = jnp.zeros((LANES - DSA_HEAD_DIM, tq), BF16)
    qz = jnp.concatenate([jnp.concatenate([qdt[hd * DSA_HEAD_DIM:(hd + 1) * DSA_HEAD_DIM, :], zq], axis=0)
                          for hd in range(DSA_HEADS)], axis=1)
    wide = DSA_HEADS * tq

    def scores(kb, dst_ref):
        k0 = pl.multiple_of(kb * tk, tk)
        dst_ref[...] = _dot(dkv_ref[pl.ds(k0, tk), :], qz)

    def absorb(src_ref, kb, carry):
        m_i, l_i, acc = carry
        k0 = pl.multiple_of(kb * tk, tk)
        bias = bias_ref[pl.ds(k0, tk), :]
        s = src_ref[...] + jnp.concatenate([bias] * DSA_HEADS, axis=1)
        m_new = jnp.maximum(m_i, jnp.max(s, axis=0, keepdims=True))
        p = jnp.exp2(s - m_new)
        alpha = jnp.exp2(m_i - m_new)
        l_new = alpha * l_i + jnp.sum(p, axis=0, keepdims=True)
        pv = _dot(dkvt_ref[0, DSA_HEAD_DIM:, pl.ds(k0, tk)], p.astype(BF16))
        return m_new, l_new, alpha * acc + pv

    def pair(j, carry):
        kb = 2 * j
        scores(kb + 1, sb_ref)
        carry = absorb(sa_ref, kb, carry)
        scores(kb + 2, sa_ref)
        return absorb(sb_ref, kb + 1, carry)

    init = (jnp.full((1, wide), NEG_BIG, F32), jnp.zeros((1, wide), F32), jnp.zeros((DSA_HEAD_DIM, wide), F32))
    last = nkb - 1
    scores(0, sa_ref)
    carry = lax.fori_loop(0, last // 2, pair, init)

    def tail_odd(carry):
        scores(last, sb_ref)
        return absorb(sb_ref, last, absorb(sa_ref, last - 1, carry))

    def tail_even(carry):
        return absorb(sa_ref, last, carry)

    _, l_f, acc = lax.cond(last % 2 == 1, tail_odd, tail_even, carry)
    o_all = acc / l_f
    o_ref[...] = jnp.concatenate([o_all[:, hd * tq:(hd + 1) * tq] for hd in range(DSA_HEADS)],
                                 axis=0).T.astype(BF16)


def _dsa(qdt, iqt, iwt, dkv, dkvt, ikw, batch, seq):
    tq, tk = DSA_TQ, DSA_TK
    nq = seq // tq
    n_sel = min(DSA_TOPK, seq // 4)
    return pl.pallas_call(
        functools.partial(_dsa_kernel, tq=tq, tk=tk, n_sel=n_sel),
        out_shape=jax.ShapeDtypeStruct((batch * seq, 256), BF16),
        grid=(batch, nq),
        in_specs=[pl.BlockSpec((1, 256, tq), lambda b, i: (b, 0, i)),
                  pl.BlockSpec((1, 128, tq), lambda b, i: (b, 0, i)),
                  pl.BlockSpec((1, 8, tq), lambda b, i: (b, 0, i)),
                  pl.BlockSpec((seq, 128), lambda b, i: (b, 0)),
                  pl.BlockSpec((1, 128, seq), lambda b, i: (b, 0, 0)),
                  pl.BlockSpec((seq, 128), lambda b, i: (b, 0))],
        out_specs=pl.BlockSpec((tq, 256), lambda b, i: (b * nq + i, 0)),
        scratch_shapes=[pltpu.VMEM((seq, tq), jnp.int32), pltpu.VMEM((seq, tq), F32),
                        pltpu.VMEM((seq, tq), jnp.int16),
                        pltpu.VMEM((tk, DSA_HEADS * tq), F32), pltpu.VMEM((tk, DSA_HEADS * tq), F32)],
        compiler_params=_params(("parallel", "parallel")),
        name="dsa",
    )(qdt, iqt, iwt, dkv, dkvt, ikw)


def _mem_kv_kernel(mem_ref, g_ref, w_ref, kt_ref, v_ref):
    mn = _rms(mem_ref[0], g_ref[...]).astype(BF16)
    kv = _dot(mn, w_ref[...])
    kt_ref[0] = kv[:, :256].T.astype(BF16)
    v_ref[0] = kv[:, 256:].astype(BF16)


def _mem_kv(mem, gain, w_kv):
    b, m, d = mem.shape
    return pl.pallas_call(
        _mem_kv_kernel,
        out_shape=(jax.ShapeDtypeStruct((b, 256, m), BF16), jax.ShapeDtypeStruct((b, m, 256), BF16)),
        grid=(b,),
        in_specs=[pl.BlockSpec((1, m, d), lambda i: (i, 0, 0)), _const_spec((1, d)), _const_spec((d, 512))],
        out_specs=(pl.BlockSpec((1, 256, m), lambda i: (i, 0, 0)), pl.BlockSpec((1, m, 256), lambda i: (i, 0, 0))),
        compiler_params=_params(("parallel",)),
        name="mem_kv",
    )(mem, gain, w_kv)


def _merge_kernel(x_ref, ya_ref, yb_ref, yc_ref, yd_ref, gmix_ref, wg_ref, wbr_ref, wout_ref,
                  gq_ref, wq_ref, mkt_ref, mv_ref, wo_ref, gffn_ref, wr_ref, br_ref,
                  x2_ref, hp_ref, rt_ref, cnt_ref, run_ref, *, tm):
    x = x_ref[...]
    d = x.shape[-1]
    h = _rms(x, gmix_ref[...]).astype(BF16)
    merged = jnp.zeros((tm, d), F32)
    for n, y_ref in enumerate((ya_ref, yb_ref, yc_ref, yd_ref)):
        gate = _sigmoid(_dot(h, wg_ref[:, n * d:(n + 1) * d]))
        merged = merged + gate * _dot(y_ref[...], wbr_ref[n])
    x1 = x + _dot(merged.astype(BF16), wout_ref[...])
    h2 = _rms(x1, gq_ref[...]).astype(BF16)
    q = (_dot(h2, wq_ref[...]) * (MEM_HEAD_DIM ** -0.5)).astype(BF16)
    lane_head = lax.broadcasted_iota(jnp.int32, (tm, 256), 1) // MEM_HEAD_DIM
    mv = mv_ref[0]
    o = jnp.zeros((tm, 256), F32)
    for hd in range(MEM_HEADS):
        s = _dot(q[:, hd * MEM_HEAD_DIM:(hd + 1) * MEM_HEAD_DIM], mkt_ref[0, hd * MEM_HEAD_DIM:(hd + 1) * MEM_HEAD_DIM, :])
        p = jnp.exp(s - jnp.max(s, axis=-1, keepdims=True))
        p = p / jnp.sum(p, axis=-1, keepdims=True)
        o = o + jnp.where(lane_head == hd, _dot(p.astype(BF16), mv), 0.0)
    x2 = x1 + _dot(o.astype(BF16), wo_ref[...])
    x2_ref[...] = x2
    h3f = _rms(x2, gffn_ref[...])
    hp_ref[...] = _pack_bf16_pairs(h3f)
    h3 = h3f.astype(BF16)
    logits = _dot(h3, wr_ref[...]) + br_ref[...]
    lane = lax.broadcasted_iota(jnp.int32, (tm, LANES), 1)
    gl = jnp.where(lane < MOE_GROUPS, logits, -jnp.inf)
    gmax = jnp.max(gl, axis=-1, keepdims=True)
    gsel = jnp.min(jnp.where(gl == gmax, lane, LANES), axis=-1, keepdims=True)
    pg_sel = 1.0 / jnp.sum(jnp.exp(gl - gmax), axis=-1, keepdims=True)
    in_group = (lane - ROUTER_EXPERT_LANE) // MOE_EXPERTS_PER_GROUP == gsel
    el = jnp.where(in_group, logits, -jnp.inf)
    m1 = jnp.max(el, axis=-1, keepdims=True)
    i1 = jnp.min(jnp.where(el == m1, lane, LANES), axis=-1, keepdims=True)
    el2 = jnp.where(lane == i1, -jnp.inf, el)
    m2 = jnp.max(el2, axis=-1, keepdims=True)
    i2 = jnp.min(jnp.where(el2 == m2, lane, LANES), axis=-1, keepdims=True)
    e21 = jnp.exp(m2 - m1)
    c1 = pg_sel / (1.0 + e21)
    @pl.when(pl.program_id(0) == 0)
    def _():
        run_ref[...] = jnp.zeros_like(run_ref)

    oh1 = jnp.where(lane == i1, 1.0, 0.0)
    oh2 = jnp.where(lane == i2, 1.0, 0.0)
    both = oh1 + oh2
    tr = lax.broadcasted_iota(jnp.int32, (tm, tm), 0)
    tc = lax.broadcasted_iota(jnp.int32, (tm, tm), 1)
    before = _dot(jnp.where(tc < tr, 1.0, 0.0).astype(BF16), both.astype(BF16)) + run_ref[0:1, :]
    r1 = jnp.sum(oh1 * before, axis=-1, keepdims=True)
    r2 = jnp.sum(oh2 * before, axis=-1, keepdims=True)
    total = run_ref[...] + jnp.sum(both, axis=0, keepdims=True)
    run_ref[...] = total
    cnt_ref[...] = total
    ids = (jnp.where(lane == 0, i1, i2) - ROUTER_EXPERT_LANE).astype(F32)
    rt_ref[...] = jnp.where(lane < 2, ids, jnp.where(lane == 2, c1, jnp.where(lane == 3, c1 * e21,
                            jnp.where(lane == 4, r1, jnp.where(lane == 5, r2, 0.0)))))


def _merge(x, ys, gmix, wg, wbr, wout, gq, wq, mkt, mv, wo, gffn, wr, br, batch, seq):
    n, d = x.shape
    tm = MERGE_TM
    spt = seq // tm
    m = mv.shape[1]
    tok = lambda w: pl.BlockSpec((tm, w), lambda i: (i, 0))
    return pl.pallas_call(
        functools.partial(_merge_kernel, tm=tm),
        out_shape=(jax.ShapeDtypeStruct((n, d), F32), jax.ShapeDtypeStruct((n, d // 2), F32),
                   jax.ShapeDtypeStruct((n, LANES), F32), jax.ShapeDtypeStruct((8, LANES), F32)),
        grid=(n // tm,),
        in_specs=[tok(d), tok(256), tok(256), tok(256), tok(256),
                  _const_spec((1, d)), _const_spec((d, N_BRANCH * d)), _const_spec((N_BRANCH, 256, d)),
                  _const_spec((d, d)), _const_spec((1, d)), _const_spec((d, 256)),
                  pl.BlockSpec((1, 256, m), lambda i: (i // spt, 0, 0)),
                  pl.BlockSpec((1, m, 256), lambda i: (i // spt, 0, 0)),
                  _const_spec((256, d)), _const_spec((1, d)), _const_spec((d, LANES)), _const_spec((1, LANES))],
        out_specs=(tok(d), tok(d // 2), tok(LANES), pl.BlockSpec((8, LANES), lambda i: (0, 0))),
        scratch_shapes=[pltpu.VMEM((8, LANES), F32)],
        compiler_params=_params(("arbitrary",)),
        name="merge_mem_router",
    )(x, *ys, gmix, wg, wbr, wout, gq, wq, mkt, mv, wo, gffn, wr, br)


def _sc_gather_rows(table, idx):
    _, width = table.shape
    total = idx.shape[0]
    chunk, nbuf = SC_GATHER_CHUNK, SC_GATHER_BUFS
    workers = SC_CORES * SC_SUBCORES
    per_w = total // workers
    nch = per_w // chunk
    assert total % (workers * chunk * nbuf) == 0
    mesh = plsc.VectorSubcoreMesh(core_axis_name="c", subcore_axis_name="s")

    @functools.partial(
        pl.kernel, mesh=mesh, out_type=jax.ShapeDtypeStruct((total, width), table.dtype),
        scratch_types=[pltpu.VMEM((nch, chunk), jnp.int32), pltpu.VMEM((nbuf, chunk, width), table.dtype),
                       pltpu.SemaphoreType.DMA((nbuf,)), pltpu.SemaphoreType.DMA((nbuf,))])
    def gather_kernel(table_hbm, idx_hbm, out_hbm, idx_v, rows_v, gsem, wsem):
        wid = lax.axis_index("s") * SC_CORES + lax.axis_index("c")
        pltpu.sync_copy(idx_hbm.at[wid], idx_v)

        def gather(j, slot):
            return pltpu.make_async_copy(table_hbm.at[idx_v.at[j]], rows_v.at[slot], gsem.at[slot])

        def write(j, slot):
            off = pl.multiple_of(wid * per_w + j * chunk, chunk)
            return pltpu.make_async_copy(rows_v.at[slot], out_hbm.at[pl.ds(off, chunk)], wsem.at[slot])

        for slot in range(nbuf):
            gather(slot, slot).start()

        @pl.loop(0, nch // nbuf)
        def _(g):
            for slot in range(nbuf):
                j = g * nbuf + slot
                gather(j, slot).wait()
                write(j, slot).start()
                write(j, slot).wait()

                @pl.when(j + nbuf < nch)
                def _():
                    gather(j + nbuf, slot).start()

    return gather_kernel(table, idx.reshape(workers, nch, chunk))


def _sc_scatter_rows(table, dest2, total):
    n, width = table.shape
    chunk, nbuf = SC_GATHER_CHUNK, SC_GATHER_BUFS
    workers = SC_CORES * SC_SUBCORES
    per_w = n // workers
    nch = per_w // chunk
    assert n % (workers * chunk * nbuf) == 0
    mesh = plsc.VectorSubcoreMesh(core_axis_name="c", subcore_axis_name="s")

    @functools.partial(
        pl.kernel, mesh=mesh, out_type=jax.ShapeDtypeStruct((total, width), table.dtype),
        scratch_types=[pltpu.VMEM((2, nch, chunk), jnp.int32), pltpu.VMEM((nbuf, chunk, width), table.dtype),
                       pltpu.SemaphoreType.DMA((nbuf,)), pltpu.SemaphoreType.DMA((nbuf,))])
    def scatter_kernel(table_hbm, idx_hbm, out_hbm, idx_v, rows_v, rsem, wsem):
        wid = lax.axis_index("s") * SC_CORES + lax.axis_index("c")
        pltpu.sync_copy(idx_hbm.at[wid], idx_v)

        def read(j, slot):
            off = pl.multiple_of(wid * per_w + j * chunk, chunk)
            return pltpu.make_async_copy(table_hbm.at[pl.ds(off, chunk)], rows_v.at[slot], rsem.at[slot])

        def write(j, slot, k):
            return pltpu.make_async_copy(rows_v.at[slot], out_hbm.at[idx_v.at[k, j]], wsem.at[slot])

        for slot in range(nbuf):
            read(slot, slot).start()

        @pl.loop(0, nch // nbuf)
        def _(g):
            for slot in range(nbuf):
                j = g * nbuf + slot
                read(j, slot).wait()
                write(j, slot, 0).start()
                write(j, slot, 1).start()
                write(j, slot, 0).wait()
                write(j, slot, 1).wait()

                @pl.when(j + nbuf < nch)
                def _():
                    read(j + nbuf, slot).start()

    idx = dest2.reshape(2, workers, nch, chunk).transpose(1, 0, 2, 3)
    return scatter_kernel(table, idx)


def _dispatch_plan(rt, cnt, n):
    ne, blk = MOE_N_EXPERTS, MOE_BLOCK
    n_blocks = (2 * n) // blk + ne
    experts = jnp.arange(ne, dtype=jnp.int32)
    counts = cnt[0, ROUTER_EXPERT_LANE:ROUTER_EXPERT_LANE + ne].astype(jnp.int32)
    padded = (counts + blk - 1) // blk * blk
    pend = jnp.cumsum(padded)
    pstart = pend - padded
    ids = rt[:, 0:2].astype(jnp.int32)
    pos = rt[:, 4:6].astype(jnp.int32)
    first_row = jnp.sum(jnp.where(ids[:, :, None] == experts[None, None, :], pstart[None, None, :], 0), axis=-1)
    dest2 = (first_row + pos).T
    b0 = jnp.arange(n_blocks, dtype=jnp.int32) * blk
    block_e = jnp.minimum(jnp.sum((pend[None, :] <= b0[:, None]).astype(jnp.int32), axis=1), ne - 1)
    n_valid = jnp.clip(counts[block_e] - (b0 - pstart[block_e]), 0, blk).astype(jnp.int32)
    return dest2, block_e, n_valid


def _expert_block_kernel(be_ref, nv_ref, xs_ref, wg_ref, wu_ref, wd_ref, o_ref, wgb_ref, wub_ref, wdb_ref):
    b = pl.program_id(0)
    valid = nv_ref[b]

    @pl.when((b == 0) | (be_ref[b] != be_ref[jnp.maximum(b - 1, 0)]))
    def _():
        wgb_ref[...] = wg_ref[0].astype(BF16)
        wub_ref[...] = wu_ref[0].astype(BF16)
        wdb_ref[...] = wd_ref[0].astype(BF16)

    @pl.when(valid > 0)
    def _():
        row = lax.broadcasted_iota(jnp.int32, xs_ref.shape, 0)
        words = jnp.where(row < valid, xs_ref[...], 0.0)
        h = _unpack_bf16_pairs(words).astype(BF16)
        gt = _dot(h, wgb_ref[...])
        hid = gt * _sigmoid(gt) * _dot(h, wub_ref[...])
        o_ref[...] = _pack_bf16_pairs(_dot(hid.astype(BF16), wdb_ref[...]))

    @pl.when(valid == 0)
    def _():
        o_ref[...] = jnp.zeros_like(o_ref)


def _expert_blocks(xs, wg, wu, wd, layer, block_e, n_used):
    p_rows, half = xs.shape
    d, hid = wg.shape[-2:]
    blk = MOE_BLOCK
    grid_spec = pltpu.PrefetchScalarGridSpec(
        num_scalar_prefetch=2, grid=(p_rows // blk,),
        in_specs=[pl.BlockSpec((blk, half), lambda b, be, nu: (b, 0)),
                  pl.BlockSpec((None, 1, d, hid), lambda b, be, nu: (layer, be[b], 0, 0)),
                  pl.BlockSpec((None, 1, d, hid), lambda b, be, nu: (layer, be[b], 0, 0)),
                  pl.BlockSpec((None, 1, hid, d), lambda b, be, nu: (layer, be[b], 0, 0))],
        out_specs=pl.BlockSpec((blk, half), lambda b, be, nu: (b, 0)),
        scratch_shapes=[pltpu.VMEM((d, hid), BF16), pltpu.VMEM((d, hid), BF16), pltpu.VMEM((hid, d), BF16)])
    return pl.pallas_call(
        _expert_block_kernel, out_shape=jax.ShapeDtypeStruct((p_rows, half), F32), grid_spec=grid_spec,
        compiler_params=_params(("arbitrary",)),
        name="moe_expert_blocks",
    )(block_e, n_used, xs, wg, wu, wd)


def _combine_kernel(x_ref, y1_ref, y2_ref, rt_ref, gfin_ref, o_ref, *, final_norm):
    out = (x_ref[...] + rt_ref[:, 2:3] * _unpack_bf16_pairs(y1_ref[...])
           + rt_ref[:, 3:4] * _unpack_bf16_pairs(y2_ref[...]))
    o_ref[...] = _rms(out, gfin_ref[...]) if final_norm else out


def _combine(x2, y_halves, rt, gfin, final_norm):
    n, d = x2.shape
    tm = COMBINE_TM
    nt = n // tm
    return pl.pallas_call(
        functools.partial(_combine_kernel, final_norm=final_norm),
        out_shape=jax.ShapeDtypeStruct((n, d), F32),
        grid=(nt,),
        in_specs=[pl.BlockSpec((tm, d), lambda i: (i, 0)),
                  pl.BlockSpec((tm, d // 2), lambda i: (i, 0)),
                  pl.BlockSpec((tm, d // 2), lambda i: (i + nt, 0)),
                  pl.BlockSpec((tm, LANES), lambda i: (i, 0)), _const_spec((1, d))],
        out_specs=pl.BlockSpec((tm, d), lambda i: (i, 0)),
        compiler_params=_params(("parallel",)), name="moe_combine",
    )(x2, y_halves, y_halves, rt, gfin)


def _moe(x2, hp, rt, cnt, wg, wu, wd, layer, gfin, final_norm):
    n = x2.shape[0]
    dest2, block_e, n_valid = _dispatch_plan(rt, cnt, n)
    xs = _sc_scatter_rows(hp, dest2, block_e.shape[0] * MOE_BLOCK)
    yb = _expert_blocks(xs, wg, wu, wd, layer, block_e, n_valid)
    y_halves = _sc_gather_rows(yb, dest2.reshape(2 * n))
    return _combine(x2, y_halves, rt, gfin, final_norm)


def kernel(x, mem, positions, norm_mix, w_in, diff_lambda, hgrn_lb_logits, spatial_w, spatial_b, w_branch, w_out,
           norm_mem_q, norm_mem_kv, w_mem_q, w_mem_kv, w_mem_o, norm_ffn, w_router_group, b_router_group,
           w_router_expert, b_router_expert, w_exp_gate, w_exp_up, w_exp_down, norm_final):
    batch, seq, d = x.shape
    depth = w_in.shape[0]
    n = batch * seq
    xf = x.reshape(n, d)
    tabs = _rope_tables(positions)
    row = lambda v: v.reshape(1, -1).astype(F32)
    for l in range(depth):
        lam_init = 0.8 - 0.6 * math.exp(-0.3 * l)
        w1, w_gate = _split_w_in(w_in, l)
        sw = spatial_w[l].reshape(SGU_GROUPS * SGU_CHUNK, SGU_CHUNK)
        sb = jnp.repeat(spatial_b[l].T, SGU_GROUP_DIM, axis=1)
        qat, ka, vat, hb, y_c, qdt, iqt, dkv, dkvt, ikw, iwt = _projection(
            xf, row(norm_mix[l]), w1, tabs, sw, sb, batch, seq)
        y_a = _diff_attention(diff_lambda[l], qat, ka, vat, lam_init, batch, seq)
        y_b = _hgrn(hgrn_lb_logits, hb, l, batch, seq)
        y_d = _dsa(qdt, iqt, iwt, dkv, dkvt, ikw, batch, seq)
        mkt, mv = _mem_kv(mem, row(norm_mem_kv[l]), w_mem_kv[l].astype(BF16))
        e0, e1 = ROUTER_EXPERT_LANE, ROUTER_EXPERT_LANE + MOE_N_EXPERTS
        wr = jnp.zeros((d, LANES), F32)
        wr = wr.at[:, :MOE_GROUPS].set(w_router_group[l]).at[:, e0:e1].set(w_router_expert[l]).astype(BF16)
        br = jnp.zeros((1, LANES), F32)
        br = br.at[0, :MOE_GROUPS].set(b_router_group[l]).at[0, e0:e1].set(b_router_expert[l])
        x2, hp, rt, cnt = _merge(xf, (y_a, y_b, y_c, y_d), row(norm_mix[l]), w_gate, w_branch[l].astype(BF16),
                                 w_out[l].astype(BF16), row(norm_mem_q[l]), w_mem_q[l].astype(BF16), mkt, mv,
                                 w_mem_o[l].astype(BF16), row(norm_ffn[l]), wr, br, batch, seq)
        xf = _moe(x2, hp, rt, cnt, w_exp_gate, w_exp_up, w_exp_down, l,
                  row(norm_final), final_norm=(l == depth - 1))
    return xf.reshape(batch, seq, d)
```

```python
import functools
import math

import numpy as np
import jax
import jax.numpy as jnp
from jax import lax
from jax.experimental import pallas as pl
from jax.experimental.pallas import tpu as pltpu
from jax.experimental.pallas import tpu_sc as plsc

F32 = jnp.float32
BF16 = jnp.bfloat16

NORM_EPS = 1e-6
ROPE_THETA = 10000.0
NEG_BIG = -1e30

N_BRANCH = 4
DIFF_HEADS = 4
DIFF_HEAD_DIM = 32
HGRN_DIM = 64
HGRN_CHUNK = 32
HGRN_UNROLL = 8
HGRN_MIN_FORGET = 1e-30
SGU_GROUPS = 4
SGU_GROUP_DIM = 64
SGU_CHUNK = 128
DSA_HEADS = 4
DSA_HEAD_DIM = 64
DSA_IDX_HEADS = 4
DSA_IDX_DIM = 32
DSA_TOPK = 256
MEM_HEADS = 4
MEM_HEAD_DIM = 64
MOE_GROUPS = 4
MOE_EXPERTS_PER_GROUP = 8
MOE_N_EXPERTS = 32
MOE_BLOCK = 512
ROUTER_EXPERT_LANE = 32
SC_CORES = 2
SC_SUBCORES = 16
SC_GATHER_CHUNK = 16
SC_GATHER_BUFS = 4

LANES = 128
VMEM_LIMIT = 56 * 1024 * 1024

PROJ_TM = 1024
DIFF_TQ = 512
DIFF_TK = 512
HGRN_TC = 512
DSA_TQ = 512
DSA_TK = 512
MERGE_TM = 1024
COMBINE_TM = 512
ROPE_TM = 1024

C_AQ, C_AK, C_AV = 0, 256, 512
C_HB = 768
C_UV = 1792
C_DQ = 2304
C_DKV = 2560
C_IQ = 2688
C_IKW = 2816
IW_LANE = 32
C_GATE = 2852
C_TOTAL = 2944
LOG2E = math.log2(math.e)


def _params(sem):
    return pltpu.CompilerParams(dimension_semantics=sem, vmem_limit_bytes=VMEM_LIMIT)


def _const_spec(shape):
    nd = len(shape)
    return pl.BlockSpec(shape, lambda *_: (0,) * nd, pipeline_mode=pl.Buffered(1))


def _rms(xf, gain=None):
    y = xf * lax.rsqrt(jnp.mean(xf * xf, axis=-1, keepdims=True) + NORM_EPS)
    return y if gain is None else y * gain


def _sigmoid(x):
    return 0.5 * jnp.tanh(0.5 * x) + 0.5


def _pack_bf16_pairs(x):
    w = x.shape[-1] // 2
    xb = x.astype(BF16).astype(F32)
    lo = lax.shift_right_logical(pltpu.bitcast(xb[:, :w], jnp.int32), 16)
    hi = pltpu.bitcast(xb[:, w:], jnp.int32) & jnp.int32(-65536)
    return pltpu.bitcast(hi | lo, F32)


def _unpack_bf16_pairs(words):
    bits = pltpu.bitcast(words, jnp.int32)
    lo = pltpu.bitcast(bits << 16, F32)
    hi = pltpu.bitcast(bits & jnp.int32(-65536), F32)
    return jnp.concatenate([lo, hi], axis=1)


def _dot(a, b):
    return jnp.dot(a, b, preferred_element_type=F32)


def _dot_nt(a, b):
    return lax.dot_general(a, b, (((1,), (1,)), ((), ())), preferred_element_type=F32)


def _rope_table_kernel(pos_ref, frq_ref, sgn_ref, c32_ref, s32_ref, c64_ref, s64_ref):
    pos = pos_ref[...].astype(F32)
    twice = lambda t: jnp.concatenate([t, t], axis=1)
    a32 = pos * frq_ref[0:1, :LANES]
    a64 = pos * frq_ref[1:2, :LANES]
    c32_ref[...] = twice(jnp.cos(a32))
    s32_ref[...] = twice(jnp.sin(a32) * sgn_ref[0:1, :LANES])
    c64_ref[...] = twice(jnp.cos(a64))
    s64_ref[...] = twice(jnp.sin(a64) * sgn_ref[1:2, :LANES])


def _rope_tables(positions):
    n = positions.size
    pos = positions.reshape(n, 1).astype(jnp.int32)
    lane = np.arange(256)
    inv32 = ROPE_THETA ** (-jnp.arange(16, dtype=F32) * (2.0 / 32))
    inv64 = ROPE_THETA ** (-jnp.arange(32, dtype=F32) * (2.0 / 64))
    frq = jnp.stack([inv32[lane % 16], inv64[lane % 32]])
    sgn = jnp.asarray(np.stack([np.where(lane % 32 < 16, -1.0, 1.0),
                                np.where(lane % 64 < 32, -1.0, 1.0)]), F32)
    tm = ROPE_TM
    tab = jax.ShapeDtypeStruct((n, 256), F32)
    return pl.pallas_call(
        _rope_table_kernel,
        out_shape=(tab, tab, tab, tab),
        grid=(n // tm,),
        in_specs=[pl.BlockSpec((tm, 1), lambda i: (i, 0)), _const_spec((2, 256)), _const_spec((2, 256))],
        out_specs=tuple(pl.BlockSpec((tm, 256), lambda i: (i, 0)) for _ in range(4)),
        compiler_params=_params(("parallel",)),
        name="rope_tables",
    )(pos, frq, sgn)


def _w1_kernel(wt_ref, o_ref, *, layer):
    o_ref[...] = wt_ref[:, layer, :].T.astype(BF16)


def _wgate_kernel(a_ref, b_ref, o_ref, *, layer):
    off = C_GATE % LANES
    o_ref[...] = jnp.concatenate([a_ref[off:, layer, :], b_ref[:off, layer, :]], axis=0).T.astype(BF16)


def _split_w_in(w_in, layer):
    depth, d, width = w_in.shape
    wt = jnp.transpose(w_in, (2, 0, 1))
    rows = lambda f: pl.BlockSpec((LANES, depth, d), lambda i: (f(i), 0, 0))
    cols = pl.BlockSpec((d, LANES), lambda i: (0, i))
    w1 = pl.pallas_call(
        functools.partial(_w1_kernel, layer=layer), out_shape=jax.ShapeDtypeStruct((d, C_TOTAL), BF16),
        grid=(C_TOTAL // LANES,), in_specs=[rows(lambda i: i)], out_specs=cols,
        compiler_params=_params(("parallel",)), name="split_w_in",
    )(wt)
    a0 = C_GATE // LANES
    gate_w = width - C_GATE
    wg = pl.pallas_call(
        functools.partial(_wgate_kernel, layer=layer), out_shape=jax.ShapeDtypeStruct((d, gate_w), BF16),
        grid=(gate_w // LANES,), in_specs=[rows(lambda i: a0 + i), rows(lambda i: a0 + i + 1)], out_specs=cols,
        compiler_params=_params(("parallel",)), name="split_w_gate",
    )(wt, wt)
    return w1, wg


def _gelu_tanh(x):
    return 0.5 * x * (1.0 + jnp.tanh(math.sqrt(2.0 / math.pi) * (x + 0.044715 * (x * x * x))))


def _rope(x, cos, sin_signed, half):
    w = x.shape[-1]
    lane = lax.broadcasted_iota(jnp.int32, x.shape, 1)
    partner = jnp.where(lane % (2 * half) < half, pltpu.roll(x, w - half, 1), pltpu.roll(x, half, 1))
    return x * cos + partner * sin_signed


def _proj_kernel(x_ref, g_ref, w_ref, c32_ref, s32_ref, c64_ref, s64_ref, sw_ref, sb_ref,
                 qat_ref, ka_ref, vat_ref, hb_ref, yc_ref, qdt_ref, iqt_ref, dkv_ref, dkvt_ref, ikw_ref, iwt_ref,
                 *, tm):
    h = _rms(x_ref[...], g_ref[...]).astype(BF16)

    def proj(c0, width):
        return _dot(h, w_ref[:, c0:c0 + width])

    c32, s32, c64, s64 = c32_ref[...], s32_ref[...], c64_ref[...], s64_ref[...]
    qat_ref[0] = (_rope(proj(C_AQ, 256), c32, s32, 16) * (DIFF_HEAD_DIM ** -0.5 * LOG2E)).T.astype(BF16)
    ka_ref[...] = _rope(proj(C_AK, 256), c32, s32, 16).astype(BF16)
    vat_ref[0] = proj(C_AV, 256).astype(BF16).T
    hb_ref[...] = proj(C_HB, 1024)
    qdt_ref[0] = (_rope(proj(C_DQ, 256), c64, s64, 32) * (DSA_HEAD_DIM ** -0.5 * LOG2E)).T.astype(BF16)
    iqt_ref[0] = _rope(proj(C_IQ, 128), c32[:, :128], s32[:, :128], 16).T.astype(BF16)
    lane = lax.broadcasted_iota(jnp.int32, (tm, 128), 1)
    is_k = lane < DSA_HEAD_DIM
    dkv = _rope(proj(C_DKV, 128), jnp.where(is_k, c64[:, :128], 1.0), jnp.where(is_k, s64[:, :128], 0.0), 32)
    is_ik = lane < DSA_IDX_DIM
    ikw = _rope(proj(C_IKW, 128), jnp.where(is_ik, c32[:, :128], 1.0), jnp.where(is_ik, s32[:, :128], 0.0), 16)
    dkv_ref[...] = dkv.astype(BF16)
    dkvt_ref[0] = dkv.T.astype(BF16)
    ikw_ref[...] = ikw.astype(BF16)
    iw_scale = DSA_IDX_HEADS ** -0.5 * DSA_IDX_DIM ** -0.5
    iwt_ref[0] = (ikw * iw_scale).T[IW_LANE:IW_LANE + 8, :]
    uv = _gelu_tanh(proj(C_UV, 512))
    u, v = uv[:, :256], uv[:, 256:]
    mu = jnp.mean(v, axis=-1, keepdims=True)
    vc = v - mu
    vn = (vc * lax.rsqrt(jnp.mean(vc * vc, axis=-1, keepdims=True) + NORM_EPS)).astype(BF16)
    r = lax.broadcasted_iota(jnp.int32, (SGU_GROUPS * SGU_CHUNK, SGU_CHUNK), 0)
    c = lax.broadcasted_iota(jnp.int32, (SGU_GROUPS * SGU_CHUNK, SGU_CHUNK), 1)
    wt = jnp.where((r % SGU_CHUNK) >= c, sw_ref[...], 0.0).astype(BF16)
    lane_grp = lax.broadcasted_iota(jnp.int32, (SGU_CHUNK, 256), 1) // SGU_GROUP_DIM
    for ch in range(tm // SGU_CHUNK):
        r0 = ch * SGU_CHUNK
        full = _dot(wt, vn[r0:r0 + SGU_CHUNK, :])
        mixed = sb_ref[...]
        for g in range(SGU_GROUPS):
            mixed = mixed + jnp.where(lane_grp == g, full[g * SGU_CHUNK:(g + 1) * SGU_CHUNK, :], 0.0)
        yc_ref[r0:r0 + SGU_CHUNK, :] = (u[r0:r0 + SGU_CHUNK, :] * mixed).astype(BF16)


def _projection(x, gain, w1, tabs, sw, sb, batch, seq):
    n, d = x.shape
    tm = PROJ_TM
    spt = seq // tm
    tok = lambda w: pl.BlockSpec((tm, w), lambda i: (i, 0))
    tr = lambda rows: pl.BlockSpec((1, rows, tm), lambda i: (i // spt, 0, i % spt))
    out_shape = (
        jax.ShapeDtypeStruct((batch, 256, seq), BF16),
        jax.ShapeDtypeStruct((n, 256), BF16),
        jax.ShapeDtypeStruct((batch, 256, seq), BF16),
        jax.ShapeDtypeStruct((n, 1024), F32),
        jax.ShapeDtypeStruct((n, 256), BF16),
        jax.ShapeDtypeStruct((batch, 256, seq), BF16),
        jax.ShapeDtypeStruct((batch, 128, seq), BF16),
        jax.ShapeDtypeStruct((n, 128), BF16),
        jax.ShapeDtypeStruct((batch, 128, seq), BF16),
        jax.ShapeDtypeStruct((n, 128), BF16),
        jax.ShapeDtypeStruct((batch, 8, seq), F32),
    )
    return pl.pallas_call(
        functools.partial(_proj_kernel, tm=tm),
        out_shape=out_shape,
        grid=(n // tm,),
        in_specs=[tok(d), _const_spec((1, d)), _const_spec((d, C_TOTAL)),
                  tok(256), tok(256), tok(256), tok(256),
                  _const_spec((SGU_GROUPS * SGU_CHUNK, SGU_CHUNK)), _const_spec((SGU_CHUNK, 256))],
        out_specs=(tr(256), tok(256), tr(256), tok(1024), tok(256), tr(256), tr(128), tok(128), tr(128), tok(128), tr(8)),
        compiler_params=_params(("parallel",)),
        name="projection",
    )(x, gain, w1, *tabs, sw, sb)


def _diff_attn_kernel(lam_ref, qt_ref, k_ref, vt_ref, o_ref, sa_ref, sb_ref, *, lam_init, tq, tk):
    q0 = pl.program_id(1) * tq
    kb_diag = q0 // tk
    lv = lam_ref[...]
    lam = (jnp.exp(jnp.sum(lv[0:1] * lv[1:2], axis=-1, keepdims=True))
           - jnp.exp(jnp.sum(lv[2:3] * lv[3:4], axis=-1, keepdims=True)) + lam_init)
    qt = qt_ref[0]
    feat = lax.broadcasted_iota(jnp.int32, (256, tq), 0) // DIFF_HEAD_DIM
    n_maps = 2 * DIFF_HEADS
    qz = jnp.concatenate([jnp.where(feat == i, qt, jnp.zeros_like(qt)) for i in range(n_maps)], axis=1)
    wide = n_maps * tq
    key_i = lax.broadcasted_iota(jnp.int32, (tk, wide), 0)
    qry_i = q0 + lax.broadcasted_iota(jnp.int32, (tk, wide), 1) % tq

    def scores(kb, dst_ref):
        k0 = pl.multiple_of(kb * tk, tk)
        dst_ref[...] = _dot(k_ref[pl.ds(k0, tk), :], qz)

    def absorb(src_ref, kb, carry, masked):
        m_i, l_i, acc = carry
        k0 = pl.multiple_of(kb * tk, tk)
        s = src_ref[...]
        if masked:
            s = jnp.where(k0 + key_i <= qry_i, s, NEG_BIG)
        m_new = jnp.maximum(m_i, jnp.max(s, axis=0, keepdims=True))
        p = jnp.exp2(s - m_new)
        alpha = jnp.exp2(m_i - m_new)
        l_new = alpha * l_i + jnp.sum(p, axis=0, keepdims=True)
        pb = p.astype(BF16)
        pv = jnp.concatenate(
            [_dot(vt_ref[0, hd * 64:(hd + 1) * 64, pl.ds(k0, tk)], pb[:, 2 * hd * tq:(2 * hd + 2) * tq])
             for hd in range(DIFF_HEADS)], axis=1)
        return m_new, l_new, alpha * acc + pv

    def pair(j, carry):
        kb = 2 * j
        scores(kb + 1, sb_ref)
        carry = absorb(sa_ref, kb, carry, False)
        scores(kb + 2, sa_ref)
        return absorb(sb_ref, kb + 1, carry, False)

    init = (jnp.full((1, wide), NEG_BIG, F32), jnp.zeros((1, wide), F32), jnp.zeros((64, wide), F32))
    scores(0, sa_ref)
    carry = lax.fori_loop(0, kb_diag // 2, pair, init)

    def tail_odd(carry):
        scores(kb_diag, sb_ref)
        carry = absorb(sa_ref, kb_diag - 1, carry, False)
        return absorb(sb_ref, kb_diag, carry, True)

    def tail_even(carry):
        return absorb(sa_ref, kb_diag, carry, True)

    _, l_f, acc = lax.cond(kb_diag % 2 == 1, tail_odd, tail_even, carry)
    o_all = acc / l_f
    heads = []
    for hd in range(DIFF_HEADS):
        o0 = o_all[:, 2 * hd * tq:(2 * hd + 1) * tq]
        o1 = o_all[:, (2 * hd + 1) * tq:(2 * hd + 2) * tq]
        o_h = o0 - lam * o1
        ms = jnp.mean(o_h * o_h, axis=0, keepdims=True)
        heads.append(o_h * lax.rsqrt(ms + NORM_EPS) * (1.0 - lam_init))
    o_ref[...] = jnp.concatenate(heads, axis=0).T.astype(BF16)


def _diff_attention(lam_vec, qat, ka, vat, lam_init, batch, seq):
    tq, tk = DIFF_TQ, DIFF_TK
    nq = seq // tq
    return pl.pallas_call(
        functools.partial(_diff_attn_kernel, lam_init=lam_init, tq=tq, tk=tk),
        out_shape=jax.ShapeDtypeStruct((batch * seq, 256), BF16),
        grid=(batch, nq),
        in_specs=[_const_spec((4, DIFF_HEAD_DIM)),
                  pl.BlockSpec((1, 256, tq), lambda b, i: (b, 0, i)),
                  pl.BlockSpec((seq, 256), lambda b, i: (b, 0)),
                  pl.BlockSpec((1, 256, seq), lambda b, i: (b, 0, 0))],
        out_specs=pl.BlockSpec((tq, 256), lambda b, i: (b * nq + i, 0)),
        scratch_shapes=[pltpu.VMEM((tk, 2 * DIFF_HEADS * tq), F32), pltpu.VMEM((tk, 2 * DIFF_HEADS * tq), F32)],
        compiler_params=_params(("parallel", "parallel")),
        name="diff_attention",
    )(lam_vec, qat, ka, vat)


def _hgrn_kernel(lbl_ref, hb_ref, o_ref, st_ref, pstk_ref, *, layer, tc):
    cz = HGRN_CHUNK
    w = 256

    @pl.when(pl.program_id(1) == 0)
    def _():
        st_ref[...] = jnp.zeros_like(st_ref)

    lg = lbl_ref[...]
    e = jnp.exp(lg - jnp.max(lg, axis=0, keepdims=True))
    lw = e / jnp.sum(e, axis=0, keepdims=True)
    lb = jnp.sum(lw[0:layer + 1], axis=0, keepdims=True) - lw[0:1]

    ri = lax.broadcasted_iota(jnp.int32, (cz, cz), 0)
    ci = lax.broadcasted_iota(jnp.int32, (cz, cz), 1)
    tri = (ri >= ci).astype(F32)
    rb = lax.broadcasted_iota(jnp.int32, (w, w), 0) // HGRN_DIM
    cb = lax.broadcasted_iota(jnp.int32, (w, w), 1) // HGRN_DIM
    same_head = rb == cb
    head_ones = same_head.astype(BF16)
    trows = {r: r + lax.broadcasted_iota(jnp.int32, (cz - r, w), 0) for r in range(0, cz, 16)}

    def chunk(c, carry):
        r0 = pl.multiple_of(c * cz, cz)
        q = hb_ref[pl.ds(r0, cz), 0:256]
        fp = hb_ref[pl.ds(r0, cz), 256:512]
        v = hb_ref[pl.ds(r0, cz), 512:768]
        g = hb_ref[pl.ds(r0, cz), 768:1024]
        qf = q * _sigmoid(q)
        f = lb + (1.0 - lb) * jax.nn.sigmoid(fp)
        log_f = jnp.log(jnp.maximum(f, HGRN_MIN_FORGET))
        kf = (1.0 - lb) * jax.nn.sigmoid(-fp)
        bc = jnp.dot(tri, log_f, preferred_element_type=F32, precision=lax.Precision.HIGHEST)
        st = st_ref[...]
        o = _dot_nt((qf * jnp.exp(bc)).astype(BF16), st.astype(BF16))
        for s in range(cz):
            r_lo = (s // 16) * 16
            arg = bc[r_lo:, :] - bc[s:s + 1, :]
            if s > r_lo:
                arg = jnp.where(trows[r_lo] >= s, arg, NEG_BIG)
            p = qf[r_lo:, :] * kf[s:s + 1, :] * jnp.exp(arg)
            if r_lo:
                pstk_ref[s * cz:s * cz + r_lo, :] = jnp.zeros((r_lo, w), BF16)
            pstk_ref[s * cz + r_lo:(s + 1) * cz, :] = p.astype(BF16)
        accs = [jnp.zeros((16, w), F32) for _ in range(cz // 16)]
        for sg in range(cz // 16):
            att = _dot(pstk_ref[sg * 16 * cz:(sg + 1) * 16 * cz, :], head_ones)
            for sl in range(16):
                s = sg * 16 + sl
                for j in range(sg, cz // 16):
                    accs[j] = accs[j] + att[sl * cz + 16 * j:sl * cz + 16 * j + 16, :] * v[s:s + 1, :]
        o = o + jnp.concatenate(accs, axis=0)
        b_end = bc[cz - 1:cz, :]
        kd = kf * jnp.exp(b_end - bc)
        upd = _dot(v.T.astype(BF16), kd.astype(BF16))
        st_ref[...] = st * jnp.exp(b_end) + jnp.where(same_head, upd, 0.0)
        ms = _dot(o * o, head_ones.astype(F32)) * (1.0 / HGRN_DIM)
        y = o * lax.rsqrt(ms + NORM_EPS)
        o_ref[pl.ds(r0, cz), :] = (y * (g * _sigmoid(g))).astype(BF16)
        return carry

    def group(gi, carry):
        for u in range(HGRN_UNROLL):
            chunk(gi * HGRN_UNROLL + u, carry)
        return carry

    lax.fori_loop(0, tc // cz // HGRN_UNROLL, group, 0)


def _hgrn(lb_logits, hb, layer, batch, seq):
    tc = HGRN_TC
    nt = seq // tc
    cz = HGRN_CHUNK
    return pl.pallas_call(
        functools.partial(_hgrn_kernel, layer=layer, tc=tc),
        out_shape=jax.ShapeDtypeStruct((batch * seq, 256), BF16),
        grid=(batch, nt),
        in_specs=[_const_spec(lb_logits.shape),
                  pl.BlockSpec((tc, 1024), lambda b, i: (b * nt + i, 0))],
        out_specs=pl.BlockSpec((tc, 256), lambda b, i: (b * nt + i, 0)),
        scratch_shapes=[pltpu.VMEM((256, 256), F32), pltpu.VMEM((cz * cz, 256), BF16)],
        compiler_params=_params(("parallel", "arbitrary")),
        name="hgrn2",
    )(lb_logits, hb)


def _dsa_kernel(qdt_ref, iqt_ref, iwt_ref, dkv_ref, dkvt_ref, ikw_ref, o_ref, key_ref, bias_ref, half_ref,
                sa_ref, sb_ref, *, tq, tk, n_sel):
    q0 = pl.program_id(1) * tq
    nkb = q0 // tk + 1
    key_i = lax.broadcasted_iota(jnp.int32, (tk, tq), 0)
    qry_i = q0 + lax.broadcasted_iota(jnp.int32, (tk, tq), 1)
    grp = tk // 8
    rows8 = lambda x: x.reshape(grp, 8, tq)
    iqt = iqt_ref[0]
    zpad = jnp.zeros((LANES - DSA_IDX_DIM, tq), BF16)
    iqz = jnp.concatenate([jnp.concatenate([iqt[hd * DSA_IDX_DIM:(hd + 1) * DSA_IDX_DIM, :], zpad], axis=0)
                           for hd in range(DSA_IDX_HEADS)], axis=1)
    iw = iwt_ref[0]

    last = nkb - 1

    def score_block(kb, carry, masked):
        k0 = pl.multiple_of(kb * tk, tk)
        sh = jnp.maximum(_dot(ikw_ref[pl.ds(k0, tk), :], iqz), 0.0)
        sc = jnp.zeros((tk, tq), F32)
        for hd in range(DSA_IDX_HEADS):
            sc = sc + sh[:, hd * tq:(hd + 1) * tq] * iw[hd:hd + 1, :]
        sc = sc + 0.0
        if masked:
            sc = jnp.where(k0 + key_i <= qry_i, sc, -jnp.inf)
        bits = pltpu.bitcast(sc, jnp.int32)
        key = jnp.where(bits < 0, bits ^ jnp.int32(0x7FFFFFFF), bits)
        key_ref[pl.ds(k0, tk), :] = key
        half_ref[pl.ds(k0, tk), :] = (key >> 16).astype(jnp.int16)
        return carry

    lax.fori_loop(0, last, functools.partial(score_block, masked=False), 0)
    score_block(last, 0, True)

    one16, zero16 = jnp.ones((), jnp.int16), jnp.zeros((), jnp.int16)
    low16 = np.int16(-2 ** 15)

    def count16(limit, strict):
        def body(kb, acc):
            k0 = pl.multiple_of(kb * tk, tk)
            for c in range(tk // 128):
                blk = half_ref[pl.ds(k0 + 128 * c, 128), :].reshape(8, 16, tq)
                hit = jnp.where(blk > limit if strict else blk >= limit, one16, zero16)
                parts = [hit[j] for j in range(8)]
                while len(parts) > 1:
                    parts = [a + b for a, b in zip(parts[0::2], parts[1::2])]
                acc = acc + parts[0]
            return acc
        acc = lax.fori_loop(0, nkb, body, jnp.zeros((16, tq), jnp.int16))
        return jnp.broadcast_to(jnp.sum(acc.astype(jnp.int32), axis=0, keepdims=True), (16, tq))

    def search16(need):
        t = jnp.full((16, tq), -2 ** 15, jnp.int32)
        for bit in range(15, -1, -1):
            trial = t + 2 ** bit
            t = jnp.where(count16(trial.astype(jnp.int16), False) >= need, trial, t)
        return t

    t_hi = search16(n_sel)
    t_hi16 = t_hi.astype(jnp.int16)
    need_lo = n_sel - count16(t_hi16, True)

    def low_block(kb, carry):
        k0 = pl.multiple_of(kb * tk, tk)
        lo = ((key_ref[pl.ds(k0, tk), :] & 0xFFFF) - 2 ** 15).astype(jnp.int16).reshape(tk // 16, 16, tq)
        hi = half_ref[pl.ds(k0, tk), :].reshape(tk // 16, 16, tq)
        half_ref[pl.ds(k0, tk), :] = jnp.where(hi == t_hi16, lo, low16).reshape(tk, tq)
        return carry

    lax.fori_loop(0, nkb, low_block, 0)
    t_lo = search16(need_lo)
    thr = ((t_hi << 16) | (t_lo + 2 ** 15))[0:8, :]

    need = (need_lo - count16(t_lo.astype(jnp.int16), True))[0:8, :].astype(F32)
    ur = lax.broadcasted_iota(jnp.int32, (LANES, LANES), 0)
    uc = lax.broadcasted_iota(jnp.int32, (LANES, LANES), 1)
    earlier = (uc < ur).astype(BF16)
    ones8 = jnp.ones((8, LANES), BF16)
    key_s = lax.broadcasted_iota(jnp.int32, (LANES, tq), 0)
    qry_s = q0 + lax.broadcasted_iota(jnp.int32, (LANES, tq), 1)

    def select_block(kb, seen, masked):
        k0 = pl.multiple_of(kb * tk, tk)
        for g in range(tk // LANES):
            r0 = g * LANES
            blk = key_ref[pl.ds(k0 + r0, LANES), :].reshape(LANES // 8, 8, tq)
            eq = blk == thr[None]
            eqb = jnp.where(eq, 1.0, 0.0).reshape(LANES, tq).astype(BF16)
            rank = _dot(earlier, eqb).reshape(LANES // 8, 8, tq) + seen[None]
            tie_bias = jnp.where(rank < need[None], 0.0, NEG_BIG)
            bias = jnp.where(blk > thr[None], 0.0, jnp.where(eq, tie_bias, NEG_BIG)).reshape(LANES, tq)
            if masked:
                bias = jnp.where(k0 + r0 + key_s <= qry_s, bias, NEG_BIG)
            bias_ref[pl.ds(k0 + r0, LANES), :] = bias
            seen = seen + _dot(ones8, eqb)
        return seen

    seen = lax.fori_loop(0, last, functools.partial(select_block, masked=False), jnp.zeros((8, tq), F32))
    select_block(last, seen, True)

    qdt = qdt_ref[0]
    zq = jnp.zeros((LANES - DSA_HEAD_DIM, tq), BF16)
    qz = jnp.concatenate([jnp.concatenate([qdt[hd * DSA_HEAD_DIM:(hd + 1) * DSA_HEAD_DIM, :], zq], axis=0)
                          for hd in range(DSA_HEADS)], axis=1)
    wide = DSA_HEADS * tq

    def scores(kb, dst_ref):
        k0 = pl.multiple_of(kb * tk, tk)
        dst_ref[...] = _dot(dkv_ref[pl.ds(k0, tk), :], qz)

    def absorb(src_ref, kb, carry):
        m_i, l_i, acc = carry
        k0 = pl.multiple_of(kb * tk, tk)
        bias = bias_ref[pl.ds(k0, tk), :]
        s = src_ref[...] + jnp.concatenate([bias] * DSA_HEADS, axis=1)
        m_new = jnp.maximum(m_i, jnp.max(s, axis=0, keepdims=True))
        p = jnp.exp2(s - m_new)
        alpha = jnp.exp2(m_i - m_new)
        l_new = alpha * l_i + jnp.sum(p, axis=0, keepdims=True)
        pv = _dot(dkvt_ref[0, DSA_HEAD_DIM:, pl.ds(k0, tk)], p.astype(BF16))
        return m_new, l_new, alpha * acc + pv

    def pair(j, carry):
        kb = 2 * j
        scores(kb + 1, sb_ref)
        carry = absorb(sa_ref, kb, carry)
        scores(kb + 2, sa_ref)
        return absorb(sb_ref, kb + 1, carry)

    init = (jnp.full((1, wide), NEG_BIG, F32), jnp.zeros((1, wide), F32), jnp.zeros((DSA_HEAD_DIM, wide), F32))
    last = nkb - 1
    scores(0, sa_ref)
    carry = lax.fori_loop(0, last // 2, pair, init)

    def tail_odd(carry):
        scores(last, sb_ref)
        return absorb(sb_ref, last, absorb(sa_ref, last - 1, carry))

    def tail_even(carry):
        return absorb(sa_ref, last, carry)

    _, l_f, acc = lax.cond(last % 2 == 1, tail_odd, tail_even, carry)
    o_all = acc / l_f
    o_ref[...] = jnp.concatenate([o_all[:, hd * tq:(hd + 1) * tq] for hd in range(DSA_HEADS)],
                                 axis=0).T.astype(BF16)


def _dsa(qdt, iqt, iwt, dkv, dkvt, ikw, batch, seq):
    tq, tk = DSA_TQ, DSA_TK
    nq = seq // tq
    n_sel = min(DSA_TOPK, seq // 4)
    return pl.pallas_call(
        functools.partial(_dsa_kernel, tq=tq, tk=tk, n_sel=n_sel),
        out_shape=jax.ShapeDtypeStruct((batch * seq, 256), BF16),
        grid=(batch, nq),
        in_specs=[pl.BlockSpec((1, 256, tq), lambda b, i: (b, 0, i)),
                  pl.BlockSpec((1, 128, tq), lambda b, i: (b, 0, i)),
                  pl.BlockSpec((1, 8, tq), lambda b, i: (b, 0, i)),
                  pl.BlockSpec((seq, 128), lambda b, i: (b, 0)),
                  pl.BlockSpec((1, 128, seq), lambda b, i: (b, 0, 0)),
                  pl.BlockSpec((seq, 128), lambda b, i: (b, 0))],
        out_specs=pl.BlockSpec((tq, 256), lambda b, i: (b * nq + i, 0)),
        scratch_shapes=[pltpu.VMEM((seq, tq), jnp.int32), pltpu.VMEM((seq, tq), F32),
                        pltpu.VMEM((seq, tq), jnp.int16),
                        pltpu.VMEM((tk, DSA_HEADS * tq), F32), pltpu.VMEM((tk, DSA_HEADS * tq), F32)],
        compiler_params=_params(("parallel", "parallel")),
        name="dsa",
    )(qdt, iqt, iwt, dkv, dkvt, ikw)


def _mem_kv_kernel(mem_ref, g_ref, w_ref, kt_ref, v_ref):
    mn = _rms(mem_ref[0], g_ref[...]).astype(BF16)
    kv = _dot(mn, w_ref[...])
    kt_ref[0] = kv[:, :256].T.astype(BF16)
    v_ref[0] = kv[:, 256:].astype(BF16)


def _mem_kv(mem, gain, w_kv):
    b, m, d = mem.shape
    return pl.pallas_call(
        _mem_kv_kernel,
        out_shape=(jax.ShapeDtypeStruct((b, 256, m), BF16), jax.ShapeDtypeStruct((b, m, 256), BF16)),
        grid=(b,),
        in_specs=[pl.BlockSpec((1, m, d), lambda i: (i, 0, 0)), _const_spec((1, d)), _const_spec((d, 512))],
        out_specs=(pl.BlockSpec((1, 256, m), lambda i: (i, 0, 0)), pl.BlockSpec((1, m, 256), lambda i: (i, 0, 0))),
        compiler_params=_params(("parallel",)),
        name="mem_kv",
    )(mem, gain, w_kv)


def _merge_kernel(x_ref, ya_ref, yb_ref, yc_ref, yd_ref, gmix_ref, wg_ref, wbr_ref, wout_ref,
                  gq_ref, wq_ref, mkt_ref, mv_ref, wo_ref, gffn_ref, wr_ref, br_ref,
                  x2_ref, hp_ref, rt_ref, cnt_ref, run_ref, *, tm):
    x = x_ref[...]
    d = x.shape[-1]
    h = _rms(x, gmix_ref[...]).astype(BF16)
    merged = jnp.zeros((tm, d), F32)
    for n, y_ref in enumerate((ya_ref, yb_ref, yc_ref, yd_ref)):
        gate = _sigmoid(_dot(h, wg_ref[:, n * d:(n + 1) * d]))
        merged = merged + gate * _dot(y_ref[...], wbr_ref[n])
    x1 = x + _dot(merged.astype(BF16), wout_ref[...])
    h2 = _rms(x1, gq_ref[...]).astype(BF16)
    q = (_dot(h2, wq_ref[...]) * (MEM_HEAD_DIM ** -0.5)).astype(BF16)
    lane_head = lax.broadcasted_iota(jnp.int32, (tm, 256), 1) // MEM_HEAD_DIM
    mv = mv_ref[0]
    o = jnp.zeros((tm, 256), F32)
    for hd in range(MEM_HEADS):
        s = _dot(q[:, hd * MEM_HEAD_DIM:(hd + 1) * MEM_HEAD_DIM], mkt_ref[0, hd * MEM_HEAD_DIM:(hd + 1) * MEM_HEAD_DIM, :])
        p = jnp.exp(s - jnp.max(s, axis=-1, keepdims=True))
        p = p / jnp.sum(p, axis=-1, keepdims=True)
        o = o + jnp.where(lane_head == hd, _dot(p.astype(BF16), mv), 0.0)
    x2 = x1 + _dot(o.astype(BF16), wo_ref[...])
    x2_ref[...] = x2
    h3f = _rms(x2, gffn_ref[...])
    hp_ref[...] = _pack_bf16_pairs(h3f)
    h3 = h3f.astype(BF16)
    logits = _dot(h3, wr_ref[...]) + br_ref[...]
    lane = lax.broadcasted_iota(jnp.int32, (tm, LANES), 1)
    gl = jnp.where(lane < MOE_GROUPS, logits, -jnp.inf)
    gmax = jnp.max(gl, axis=-1, keepdims=True)
    gsel = jnp.min(jnp.where(gl == gmax, lane, LANES), axis=-1, keepdims=True)
    pg_sel = 1.0 / jnp.sum(jnp.exp(gl - gmax), axis=-1, keepdims=True)
    in_group = (lane - ROUTER_EXPERT_LANE) // MOE_EXPERTS_PER_GROUP == gsel
    el = jnp.where(in_group, logits, -jnp.inf)
    m1 = jnp.max(el, axis=-1, keepdims=True)
    i1 = jnp.min(jnp.where(el == m1, lane, LANES), axis=-1, keepdims=True)
    el2 = jnp.where(lane == i1, -jnp.inf, el)
    m2 = jnp.max(el2, axis=-1, keepdims=True)
    i2 = jnp.min(jnp.where(el2 == m2, lane, LANES), axis=-1, keepdims=True)
    e21 = jnp.exp(m2 - m1)
    c1 = pg_sel / (1.0 + e21)
    @pl.when(pl.program_id(0) == 0)
    def _():
        run_ref[...] = jnp.zeros_like(run_ref)

    oh1 = jnp.where(lane == i1, 1.0, 0.0)
    oh2 = jnp.where(lane == i2, 1.0, 0.0)
    both = oh1 + oh2
    tr = lax.broadcasted_iota(jnp.int32, (tm, tm), 0)
    tc = lax.broadcasted_iota(jnp.int32, (tm, tm), 1)
    before = _dot(jnp.where(tc < tr, 1.0, 0.0).astype(BF16), both.astype(BF16)) + run_ref[0:1, :]
    r1 = jnp.sum(oh1 * before, axis=-1, keepdims=True)
    r2 = jnp.sum(oh2 * before, axis=-1, keepdims=True)
    total = run_ref[...] + jnp.sum(both, axis=0, keepdims=True)
    run_ref[...] = total
    cnt_ref[...] = total
    ids = (jnp.where(lane == 0, i1, i2) - ROUTER_EXPERT_LANE).astype(F32)
    rt_ref[...] = jnp.where(lane < 2, ids, jnp.where(lane == 2, c1, jnp.where(lane == 3, c1 * e21,
                            jnp.where(lane == 4, r1, jnp.where(lane == 5, r2, 0.0)))))


def _merge(x, ys, gmix, wg, wbr, wout, gq, wq, mkt, mv, wo, gffn, wr, br, batch, seq):
    n, d = x.shape
    tm = MERGE_TM
    spt = seq // tm
    m = mv.shape[1]
    tok = lambda w: pl.BlockSpec((tm, w), lambda i: (i, 0))
    return pl.pallas_call(
        functools.partial(_merge_kernel, tm=tm),
        out_shape=(jax.ShapeDtypeStruct((n, d), F32), jax.ShapeDtypeStruct((n, d // 2), F32),
                   jax.ShapeDtypeStruct((n, LANES), F32), jax.ShapeDtypeStruct((8, LANES), F32)),
        grid=(n // tm,),
        in_specs=[tok(d), tok(256), tok(256), tok(256), tok(256),
                  _const_spec((1, d)), _const_spec((d, N_BRANCH * d)), _const_spec((N_BRANCH, 256, d)),
                  _const_spec((d, d)), _const_spec((1, d)), _const_spec((d, 256)),
                  pl.BlockSpec((1, 256, m), lambda i: (i // spt, 0, 0)),
                  pl.BlockSpec((1, m, 256), lambda i: (i // spt, 0, 0)),
                  _const_spec((256, d)), _const_spec((1, d)), _const_spec((d, LANES)), _const_spec((1, LANES))],
        out_specs=(tok(d), tok(d // 2), tok(LANES), pl.BlockSpec((8, LANES), lambda i: (0, 0))),
        scratch_shapes=[pltpu.VMEM((8, LANES), F32)],
        compiler_params=_params(("arbitrary",)),
        name="merge_mem_router",
    )(x, *ys, gmix, wg, wbr, wout, gq, wq, mkt, mv, wo, gffn, wr, br)


def _sc_gather_rows(table, idx):
    _, width = table.shape
    total = idx.shape[0]
    chunk, nbuf = SC_GATHER_CHUNK, SC_GATHER_BUFS
    workers = SC_CORES * SC_SUBCORES
    per_w = total // workers
    nch = per_w // chunk
    assert total % (workers * chunk * nbuf) == 0
    mesh = plsc.VectorSubcoreMesh(core_axis_name="c", subcore_axis_name="s")

    @functools.partial(
        pl.kernel, mesh=mesh, out_type=jax.ShapeDtypeStruct((total, width), table.dtype),
        scratch_types=[pltpu.VMEM((nch, chunk), jnp.int32), pltpu.VMEM((nbuf, chunk, width), table.dtype),
                       pltpu.SemaphoreType.DMA((nbuf,)), pltpu.SemaphoreType.DMA((nbuf,))])
    def gather_kernel(table_hbm, idx_hbm, out_hbm, idx_v, rows_v, gsem, wsem):
        wid = lax.axis_index("s") * SC_CORES + lax.axis_index("c")
        pltpu.sync_copy(idx_hbm.at[wid], idx_v)

        def gather(j, slot):
            return pltpu.make_async_copy(table_hbm.at[idx_v.at[j]], rows_v.at[slot], gsem.at[slot])

        def write(j, slot):
            off = pl.multiple_of(wid * per_w + j * chunk, chunk)
            return pltpu.make_async_copy(rows_v.at[slot], out_hbm.at[pl.ds(off, chunk)], wsem.at[slot])

        for slot in range(nbuf):
            gather(slot, slot).start()

        @pl.loop(0, nch // nbuf)
        def _(g):
            for slot in range(nbuf):
                j = g * nbuf + slot
                gather(j, slot).wait()
                write(j, slot).start()
                write(j, slot).wait()

                @pl.when(j + nbuf < nch)
                def _():
                    gather(j + nbuf, slot).start()

    return gather_kernel(table, idx.reshape(workers, nch, chunk))


def _sc_scatter_rows(table, dest2, total):
    n, width = table.shape
    chunk, nbuf = SC_GATHER_CHUNK, SC_GATHER_BUFS
    workers = SC_CORES * SC_SUBCORES
    per_w = n // workers
    nch = per_w // chunk
    assert n % (workers * chunk * nbuf) == 0
    mesh = plsc.VectorSubcoreMesh(core_axis_name="c", subcore_axis_name="s")

    @functools.partial(
        pl.kernel, mesh=mesh, out_type=jax.ShapeDtypeStruct((total, width), table.dtype),
        scratch_types=[pltpu.VMEM((2, nch, chunk), jnp.int32), pltpu.VMEM((nbuf, chunk, width), table.dtype),
                       pltpu.SemaphoreType.DMA((nbuf,)), pltpu.SemaphoreType.DMA((nbuf,))])
    def scatter_kernel(table_hbm, idx_hbm, out_hbm, idx_v, rows_v, rsem, wsem):
        wid = lax.axis_index("s") * SC_CORES + lax.axis_index("c")
        pltpu.sync_copy(idx_hbm.at[wid], idx_v)

        def read(j, slot):
            off = pl.multiple_of(wid * per_w + j * chunk, chunk)
            return pltpu.make_async_copy(table_hbm.at[pl.ds(off, chunk)], rows_v.at[slot], rsem.at[slot])

        def write(j, slot, k):
            return pltpu.make_async_copy(rows_v.at[slot], out_hbm.at[idx_v.at[k, j]], wsem.at[slot])

        for slot in range(nbuf):
            read(slot, slot).start()

        @pl.loop(0, nch // nbuf)
        def _(g):
            for slot in range(nbuf):
                j = g * nbuf + slot
                read(j, slot).wait()
                write(j, slot, 0).start()
                write(j, slot, 1).start()
                write(j, slot, 0).wait()
                write(j, slot, 1).wait()

                @pl.when(j + nbuf < nch)
                def _():
                    read(j + nbuf, slot).start()

    idx = dest2.reshape(2, workers, nch, chunk).transpose(1, 0, 2, 3)
    return scatter_kernel(table, idx)


def _dispatch_plan(rt, cnt, n):
    ne, blk = MOE_N_EXPERTS, MOE_BLOCK
    n_blocks = (2 * n) // blk + ne
    experts = jnp.arange(ne, dtype=jnp.int32)
    counts = cnt[0, ROUTER_EXPERT_LANE:ROUTER_EXPERT_LANE + ne].astype(jnp.int32)
    padded = (counts + blk - 1) // blk * blk
    pend = jnp.cumsum(padded)
    pstart = pend - padded
    ids = rt[:, 0:2].astype(jnp.int32)
    pos = rt[:, 4:6].astype(jnp.int32)
    first_row = jnp.sum(jnp.where(ids[:, :, None] == experts[None, None, :], pstart[None, None, :], 0), axis=-1)
    dest2 = (first_row + pos).T
    b0 = jnp.arange(n_blocks, dtype=jnp.int32) * blk
    block_e = jnp.minimum(jnp.sum((pend[None, :] <= b0[:, None]).astype(jnp.int32), axis=1), ne - 1)
    n_valid = jnp.clip(counts[block_e] - (b0 - pstart[block_e]), 0, blk).astype(jnp.int32)
    return dest2, block_e, n_valid


def _expert_block_kernel(be_ref, nv_ref, xs_ref, wg_ref, wu_ref, wd_ref, o_ref, wgb_ref, wub_ref, wdb_ref):
    b = pl.program_id(0)
    valid = nv_ref[b]

    @pl.when((b == 0) | (be_ref[b] != be_ref[jnp.maximum(b - 1, 0)]))
    def _():
        wgb_ref[...] = wg_ref[0].astype(BF16)
        wub_ref[...] = wu_ref[0].astype(BF16)
        wdb_ref[...] = wd_ref[0].astype(BF16)

    @pl.when(valid > 0)
    def _():
        row = lax.broadcasted_iota(jnp.int32, xs_ref.shape, 0)
        words = jnp.where(row < valid, xs_ref[...], 0.0)
        h = _unpack_bf16_pairs(words).astype(BF16)
        gt = _dot(h, wgb_ref[...])
        hid = gt * _sigmoid(gt) * _dot(h, wub_ref[...])
        o_ref[...] = _pack_bf16_pairs(_dot(hid.astype(BF16), wdb_ref[...]))

    @pl.when(valid == 0)
    def _():
        o_ref[...] = jnp.zeros_like(o_ref)


def _expert_blocks(xs, wg, wu, wd, layer, block_e, n_used):
    p_rows, half = xs.shape
    d, hid = wg.shape[-2:]
    blk = MOE_BLOCK
    grid_spec = pltpu.PrefetchScalarGridSpec(
        num_scalar_prefetch=2, grid=(p_rows // blk,),
        in_specs=[pl.BlockSpec((blk, half), lambda b, be, nu: (b, 0)),
                  pl.BlockSpec((None, 1, d, hid), lambda b, be, nu: (layer, be[b], 0, 0)),
                  pl.BlockSpec((None, 1, d, hid), lambda b, be, nu: (layer, be[b], 0, 0)),
                  pl.BlockSpec((None, 1, hid, d), lambda b, be, nu: (layer, be[b], 0, 0))],
        out_specs=pl.BlockSpec((blk, half), lambda b, be, nu: (b, 0)),
        scratch_shapes=[pltpu.VMEM((d, hid), BF16), pltpu.VMEM((d, hid), BF16), pltpu.VMEM((hid, d), BF16)])
    return pl.pallas_call(
        _expert_block_kernel, out_shape=jax.ShapeDtypeStruct((p_rows, half), F32), grid_spec=grid_spec,
        compiler_params=_params(("arbitrary",)),
        name="moe_expert_blocks",
    )(block_e, n_used, xs, wg, wu, wd)


def _combine_kernel(x_ref, y1_ref, y2_ref, rt_ref, gfin_ref, o_ref, *, final_norm):
    out = (x_ref[...] + rt_ref[:, 2:3] * _unpack_bf16_pairs(y1_ref[...])
           + rt_ref[:, 3:4] * _unpack_bf16_pairs(y2_ref[...]))
    o_ref[...] = _rms(out, gfin_ref[...]) if final_norm else out


def _combine(x2, y_halves, rt, gfin, final_norm):
    n, d = x2.shape
    tm = COMBINE_TM
    nt = n // tm
    return pl.pallas_call(
        functools.partial(_combine_kernel, final_norm=final_norm),
        out_shape=jax.ShapeDtypeStruct((n, d), F32),
        grid=(nt,),
        in_specs=[pl.BlockSpec((tm, d), lambda i: (i, 0)),
                  pl.BlockSpec((tm, d // 2), lambda i: (i, 0)),
                  pl.BlockSpec((tm, d // 2), lambda i: (i + nt, 0)),
                  pl.BlockSpec((tm, LANES), lambda i: (i, 0)), _const_spec((1, d))],
        out_specs=pl.BlockSpec((tm, d), lambda i: (i, 0)),
        compiler_params=_params(("parallel",)), name="moe_combine",
    )(x2, y_halves, y_halves, rt, gfin)


def _moe(x2, hp, rt, cnt, wg, wu, wd, layer, gfin, final_norm):
    n = x2.shape[0]
    dest2, block_e, n_valid = _dispatch_plan(rt, cnt, n)
    xs = _sc_scatter_rows(hp, dest2, block_e.shape[0] * MOE_BLOCK)
    yb = _expert_blocks(xs, wg, wu, wd, layer, block_e, n_valid)
    y_halves = _sc_gather_rows(yb, dest2.reshape(2 * n))
    return _combine(x2, y_halves, rt, gfin, final_norm)


def kernel(x, mem, positions, norm_mix, w_in, diff_lambda, hgrn_lb_logits, spatial_w, spatial_b, w_branch, w_out,
           norm_mem_q, norm_mem_kv, w_mem_q, w_mem_kv, w_mem_o, norm_ffn, w_router_group, b_router_group,
           w_router_expert, b_router_expert, w_exp_gate, w_exp_up, w_exp_down, norm_final):
    batch, seq, d = x.shape
    depth = w_in.shape[0]
    n = batch * seq
    xf = x.reshape(n, d)
    tabs = _rope_tables(positions)
    row = lambda v: v.reshape(1, -1).astype(F32)
    for l in range(depth):
        lam_init = 0.8 - 0.6 * math.exp(-0.3 * l)
        w1, w_gate = _split_w_in(w_in, l)
        sw = spatial_w[l].reshape(SGU_GROUPS * SGU_CHUNK, SGU_CHUNK)
        sb = jnp.repeat(spatial_b[l].T, SGU_GROUP_DIM, axis=1)
        qat, ka, vat, hb, y_c, qdt, iqt, dkv, dkvt, ikw, iwt = _projection(
            xf, row(norm_mix[l]), w1, tabs, sw, sb, batch, seq)
        y_a = _diff_attention(diff_lambda[l], qat, ka, vat, lam_init, batch, seq)
        y_b = _hgrn(hgrn_lb_logits, hb, l, batch, seq)
        y_d = _dsa(qdt, iqt, iwt, dkv, dkvt, ikw, batch, seq)
        mkt, mv = _mem_kv(mem, row(norm_mem_kv[l]), w_mem_kv[l].astype(BF16))
        e0, e1 = ROUTER_EXPERT_LANE, ROUTER_EXPERT_LANE + MOE_N_EXPERTS
        wr = jnp.zeros((d, LANES), F32)
        wr = wr.at[:, :MOE_GROUPS].set(w_router_group[l]).at[:, e0:e1].set(w_router_expert[l]).astype(BF16)
        br = jnp.zeros((1, LANES), F32)
        br = br.at[0, :MOE_GROUPS].set(b_router_group[l]).at[0, e0:e1].set(b_router_expert[l])
        x2, hp, rt, cnt = _merge(xf, (y_a, y_b, y_c, y_d), row(norm_mix[l]), w_gate, w_branch[l].astype(BF16),
                                 w_out[l].astype(BF16), row(norm_mem_q[l]), w_mem_q[l].astype(BF16), mkt, mv,
                                 w_mem_o[l].astype(BF16), row(norm_ffn[l]), wr, br, batch, seq)
        xf = _moe(x2, hp, rt, cnt, w_exp_gate, w_exp_up, w_exp_down, l,
                  row(norm_final), final_norm=(l == depth - 1))
    return xf.reshape(batch, seq, d)
```

```python
import functools
import math

import numpy as np
import jax
import jax.numpy as jnp
from jax import lax
from jax.experimental import pallas as pl
from jax.experimental.pallas import tpu as pltpu
from jax.experimental.pallas import tpu_sc as plsc

F32 = jnp.float32
BF16 = jnp.bfloat16

NORM_EPS = 1e-6
ROPE_THETA = 10000.0
NEG_BIG = -1e30

N_BRANCH = 4
DIFF_HEADS = 4
DIFF_HEAD_DIM = 32
HGRN_DIM = 64
HGRN_CHUNK = 32
HGRN_UNROLL = 8
HGRN_MIN_FORGET = 1e-30
SGU_GROUPS = 4
SGU_GROUP_DIM = 64
SGU_CHUNK = 128
DSA_HEADS = 4
DSA_HEAD_DIM = 64
DSA_IDX_HEADS = 4
DSA_IDX_DIM = 32
DSA_TOPK = 256
MEM_HEADS = 4
MEM_HEAD_DIM = 64
MOE_GROUPS = 4
MOE_EXPERTS_PER_GROUP = 8
MOE_N_EXPERTS = 32
MOE_BLOCK = 512
ROUTER_EXPERT_LANE = 32
SC_CORES = 2
SC_SUBCORES = 16
SC_GATHER_CHUNK = 16
SC_GATHER_BUFS = 4

LANES = 128
VMEM_LIMIT = 56 * 1024 * 1024

PROJ_TM = 1024
DIFF_TQ = 512
DIFF_TK = 512
HGRN_TC = 512
DSA_TQ = 512
DSA_TK = 512
MERGE_TM = 1024
COMBINE_TM = 512
ROPE_TM = 1024

C_AQ, C_AK, C_AV = 0, 256, 512
C_HB = 768
C_UV = 1792
C_DQ = 2304
C_DKV = 2560
C_IQ = 2688
C_IKW = 2816
IW_LANE = 32
C_GATE = 2852
GATE_SHIFT = C_GATE % 128
C_TOTAL = 2944
LOG2E = math.log2(math.e)


def _params(sem):
    return pltpu.CompilerParams(dimension_semantics=sem, vmem_limit_bytes=VMEM_LIMIT)


def _const_spec(shape):
    nd = len(shape)
    return pl.BlockSpec(shape, lambda *_: (0,) * nd, pipeline_mode=pl.Buffered(1))


def _rms(xf, gain=None):
    y = xf * lax.rsqrt(jnp.mean(xf * xf, axis=-1, keepdims=True) + NORM_EPS)
    return y if gain is None else y * gain


def _sigmoid(x):
    return 0.5 * jnp.tanh(0.5 * x) + 0.5


def _pack_bf16_pairs(x):
    w = x.shape[-1] // 2
    xb = x.astype(BF16).astype(F32)
    lo = lax.shift_right_logical(pltpu.bitcast(xb[:, :w], jnp.int32), 16)
    hi = pltpu.bitcast(xb[:, w:], jnp.int32) & jnp.int32(-65536)
    return pltpu.bitcast(hi | lo, F32)


def _unpack_bf16_pairs(words):
    bits = pltpu.bitcast(words, jnp.int32)
    lo = pltpu.bitcast(bits << 16, F32)
    hi = pltpu.bitcast(bits & jnp.int32(-65536), F32)
    return jnp.concatenate([lo, hi], axis=1)


def _dot(a, b):
    return jnp.dot(a, b, preferred_element_type=F32)


def _dot_nt(a, b):
    return lax.dot_general(a, b, (((1,), (1,)), ((), ())), preferred_element_type=F32)


def _rope_table_kernel(pos_ref, frq_ref, sgn_ref, c32_ref, s32_ref, c64_ref, s64_ref):
    pos = pos_ref[...].astype(F32)
    twice = lambda t: jnp.concatenate([t, t], axis=1)
    a32 = pos * frq_ref[0:1, :LANES]
    a64 = pos * frq_ref[1:2, :LANES]
    c32_ref[...] = twice(jnp.cos(a32))
    s32_ref[...] = twice(jnp.sin(a32) * sgn_ref[0:1, :LANES])
    c64_ref[...] = twice(jnp.cos(a64))
    s64_ref[...] = twice(jnp.sin(a64) * sgn_ref[1:2, :LANES])


def _rope_tables(positions):
    n = positions.size
    pos = positions.reshape(n, 1).astype(jnp.int32)
    lane = np.arange(256)
    inv32 = ROPE_THETA ** (-jnp.arange(16, dtype=F32) * (2.0 / 32))
    inv64 = ROPE_THETA ** (-jnp.arange(32, dtype=F32) * (2.0 / 64))
    frq = jnp.stack([inv32[lane % 16], inv64[lane % 32]])
    sgn = jnp.asarray(np.stack([np.where(lane % 32 < 16, -1.0, 1.0),
                                np.where(lane % 64 < 32, -1.0, 1.0)]), F32)
    tm = ROPE_TM
    tab = jax.ShapeDtypeStruct((n, 256), F32)
    return pl.pallas_call(
        _rope_table_kernel,
        out_shape=(tab, tab, tab, tab),
        grid=(n // tm,),
        in_specs=[pl.BlockSpec((tm, 1), lambda i: (i, 0)), _const_spec((2, 256)), _const_spec((2, 256))],
        out_specs=tuple(pl.BlockSpec((tm, 256), lambda i: (i, 0)) for _ in range(4)),
        compiler_params=_params(("parallel",)),
        name="rope_tables",
    )(pos, frq, sgn)


def _w_in_kernel(wt_ref, o_ref):
    for layer in range(o_ref.shape[0]):
        o_ref[layer] = wt_ref[:, layer, :].T.astype(BF16)


def _w_in_bf16(w_in):
    depth, d, width = w_in.shape
    nblk = pl.cdiv(width, LANES)
    return pl.pallas_call(
        _w_in_kernel, out_shape=jax.ShapeDtypeStruct((depth, d, nblk * LANES), BF16),
        grid=(nblk,), in_specs=[pl.BlockSpec((LANES, depth, d), lambda i: (i, 0, 0))],
        out_specs=pl.BlockSpec((depth, d, LANES), lambda i: (0, 0, i)),
        compiler_params=_params(("parallel",)), name="w_in_bf16",
    )(jnp.transpose(w_in, (2, 0, 1)))


def _gelu_tanh(x):
    return 0.5 * x * (1.0 + jnp.tanh(math.sqrt(2.0 / math.pi) * (x + 0.044715 * (x * x * x))))


def _rope(x, cos, sin_signed, half):
    w = x.shape[-1]
    lane = lax.broadcasted_iota(jnp.int32, x.shape, 1)
    partner = jnp.where(lane % (2 * half) < half, pltpu.roll(x, w - half, 1), pltpu.roll(x, half, 1))
    return x * cos + partner * sin_signed


def _proj_kernel(x_ref, g_ref, w_ref, c32_ref, s32_ref, c64_ref, s64_ref, sw_ref, sb_ref,
                 qat_ref, ka_ref, vat_ref, hb_ref, yc_ref, qdt_ref, iqt_ref, dkv_ref, dkvt_ref, ikw_ref, iwt_ref,
                 *, tm):
    h = _rms(x_ref[...], g_ref[...]).astype(BF16)

    def proj(c0, width):
        return _dot(h, w_ref[:, c0:c0 + width])

    c32, s32, c64, s64 = c32_ref[...], s32_ref[...], c64_ref[...], s64_ref[...]
    qat_ref[0] = (_rope(proj(C_AQ, 256), c32, s32, 16) * (DIFF_HEAD_DIM ** -0.5 * LOG2E)).T.astype(BF16)
    ka_ref[...] = _rope(proj(C_AK, 256), c32, s32, 16).astype(BF16)
    vat_ref[0] = proj(C_AV, 256).astype(BF16).T
    hb_ref[...] = proj(C_HB, 1024)
    qdt_ref[0] = (_rope(proj(C_DQ, 256), c64, s64, 32) * (DSA_HEAD_DIM ** -0.5 * LOG2E)).T.astype(BF16)
    iqt_ref[0] = _rope(proj(C_IQ, 128), c32[:, :128], s32[:, :128], 16).T.astype(BF16)
    lane = lax.broadcasted_iota(jnp.int32, (tm, 128), 1)
    is_k = lane < DSA_HEAD_DIM
    dkv = _rope(proj(C_DKV, 128), jnp.where(is_k, c64[:, :128], 1.0), jnp.where(is_k, s64[:, :128], 0.0), 32)
    is_ik = lane < DSA_IDX_DIM
    ikw = _rope(proj(C_IKW, 128), jnp.where(is_ik, c32[:, :128], 1.0), jnp.where(is_ik, s32[:, :128], 0.0), 16)
    dkv_ref[...] = dkv.astype(BF16)
    dkvt_ref[0] = dkv.T.astype(BF16)
    ikw_ref[...] = ikw.astype(BF16)
    iw_scale = DSA_IDX_HEADS ** -0.5 * DSA_IDX_DIM ** -0.5
    iwt_ref[0] = (ikw * iw_scale).T[IW_LANE:IW_LANE + 8, :]
    uv = _gelu_tanh(proj(C_UV, 512))
    u, v = uv[:, :256], uv[:, 256:]
    mu = jnp.mean(v, axis=-1, keepdims=True)
    vc = v - mu
    vn = (vc * lax.rsqrt(jnp.mean(vc * vc, axis=-1, keepdims=True) + NORM_EPS)).astype(BF16)
    r = lax.broadcasted_iota(jnp.int32, (SGU_GROUPS * SGU_CHUNK, SGU_CHUNK), 0)
    c = lax.broadcasted_iota(jnp.int32, (SGU_GROUPS * SGU_CHUNK, SGU_CHUNK), 1)
    wt = jnp.where((r % SGU_CHUNK) >= c, sw_ref[...], 0.0).astype(BF16)
    lane_grp = lax.broadcasted_iota(jnp.int32, (SGU_CHUNK, 256), 1) // SGU_GROUP_DIM
    for ch in range(tm // SGU_CHUNK):
        r0 = ch * SGU_CHUNK
        full = _dot(wt, vn[r0:r0 + SGU_CHUNK, :])
        mixed = sb_ref[...]
        for g in range(SGU_GROUPS):
            mixed = mixed + jnp.where(lane_grp == g, full[g * SGU_CHUNK:(g + 1) * SGU_CHUNK, :], 0.0)
        yc_ref[r0:r0 + SGU_CHUNK, :] = (u[r0:r0 + SGU_CHUNK, :] * mixed).astype(BF16)


def _projection(x, gain, w_all, layer, tabs, sw, sb, batch, seq):
    n, d = x.shape
    tm = PROJ_TM
    spt = seq // tm
    tok = lambda w: pl.BlockSpec((tm, w), lambda i: (i, 0))
    tr = lambda rows: pl.BlockSpec((1, rows, tm), lambda i: (i // spt, 0, i % spt))
    out_shape = (
        jax.ShapeDtypeStruct((batch, 256, seq), BF16),
        jax.ShapeDtypeStruct((n, 256), BF16),
        jax.ShapeDtypeStruct((batch, 256, seq), BF16),
        jax.ShapeDtypeStruct((n, 1024), F32),
        jax.ShapeDtypeStruct((n, 256), BF16),
        jax.ShapeDtypeStruct((batch, 256, seq), BF16),
        jax.ShapeDtypeStruct((batch, 128, seq), BF16),
        jax.ShapeDtypeStruct((n, 128), BF16),
        jax.ShapeDtypeStruct((batch, 128, seq), BF16),
        jax.ShapeDtypeStruct((n, 128), BF16),
        jax.ShapeDtypeStruct((batch, 8, seq), F32),
    )
    return pl.pallas_call(
        functools.partial(_proj_kernel, tm=tm),
        out_shape=out_shape,
        grid=(n // tm,),
        in_specs=[tok(d), _const_spec((1, d)),
                  pl.BlockSpec((None, d, C_TOTAL), lambda i: (layer, 0, 0), pipeline_mode=pl.Buffered(1)),
                  tok(256), tok(256), tok(256), tok(256),
                  _const_spec((SGU_GROUPS * SGU_CHUNK, SGU_CHUNK)), _const_spec((SGU_CHUNK, 256))],
        out_specs=(tr(256), tok(256), tr(256), tok(1024), tok(256), tr(256), tr(128), tok(128), tr(128), tok(128), tr(8)),
        compiler_params=_params(("parallel",)),
        name="projection",
    )(x, gain, w_all, *tabs, sw, sb)


def _diff_attn_kernel(lam_ref, qt_ref, k_ref, vt_ref, o_ref, sa_ref, sb_ref, *, lam_init, tq, tk):
    q0 = pl.program_id(1) * tq
    kb_diag = q0 // tk
    lv = lam_ref[...]
    lam = (jnp.exp(jnp.sum(lv[0:1] * lv[1:2], axis=-1, keepdims=True))
           - jnp.exp(jnp.sum(lv[2:3] * lv[3:4], axis=-1, keepdims=True)) + lam_init)
    qt = qt_ref[0]
    feat = lax.broadcasted_iota(jnp.int32, (256, tq), 0) // DIFF_HEAD_DIM
    n_maps = 2 * DIFF_HEADS
    qz = jnp.concatenate([jnp.where(feat == i, qt, jnp.zeros_like(qt)) for i in range(n_maps)], axis=1)
    wide = n_maps * tq
    key_i = lax.broadcasted_iota(jnp.int32, (tk, wide), 0)
    qry_i = q0 + lax.broadcasted_iota(jnp.int32, (tk, wide), 1) % tq

    def scores(kb, dst_ref):
        k0 = pl.multiple_of(kb * tk, tk)
        dst_ref[...] = _dot(k_ref[pl.ds(k0, tk), :], qz)

    def absorb(src_ref, kb, carry, masked):
        m_i, l_i, acc = carry
        k0 = pl.multiple_of(kb * tk, tk)
        s = src_ref[...]
        if masked:
            s = jnp.where(k0 + key_i <= qry_i, s, NEG_BIG)
        m_new = jnp.maximum(m_i, jnp.max(s, axis=0, keepdims=True))
        p = jnp.exp2(s - m_new)
        alpha = jnp.exp2(m_i - m_new)
        l_new = alpha * l_i + jnp.sum(p, axis=0, keepdims=True)
        pb = p.astype(BF16)
        pv = jnp.concatenate(
            [_dot(vt_ref[0, hd * 64:(hd + 1) * 64, pl.ds(k0, tk)], pb[:, 2 * hd * tq:(2 * hd + 2) * tq])
             for hd in range(DIFF_HEADS)], axis=1)
        return m_new, l_new, alpha * acc + pv

    def pair(j, carry):
        kb = 2 * j
        scores(kb + 1, sb_ref)
        carry = absorb(sa_ref, kb, carry, False)
        scores(kb + 2, sa_ref)
        return absorb(sb_ref, kb + 1, carry, False)

    init = (jnp.full((1, wide), NEG_BIG, F32), jnp.zeros((1, wide), F32), jnp.zeros((64, wide), F32))
    scores(0, sa_ref)
    carry = lax.fori_loop(0, kb_diag // 2, pair, init)

    def tail_odd(carry):
        scores(kb_diag, sb_ref)
        carry = absorb(sa_ref, kb_diag - 1, carry, False)
        return absorb(sb_ref, kb_diag, carry, True)

    def tail_even(carry):
        return absorb(sa_ref, kb_diag, carry, True)

    _, l_f, acc = lax.cond(kb_diag % 2 == 1, tail_odd, tail_even, carry)
    o_all = acc / l_f
    heads = []
    for hd in range(DIFF_HEADS):
        o0 = o_all[:, 2 * hd * tq:(2 * hd + 1) * tq]
        o1 = o_all[:, (2 * hd + 1) * tq:(2 * hd + 2) * tq]
        o_h = o0 - lam * o1
        ms = jnp.mean(o_h * o_h, axis=0, keepdims=True)
        heads.append(o_h * lax.rsqrt(ms + NORM_EPS) * (1.0 - lam_init))
    o_ref[...] = jnp.concatenate(heads, axis=0).T.astype(BF16)


def _diff_attention(lam_vec, qat, ka, vat, lam_init, batch, seq):
    tq, tk = DIFF_TQ, DIFF_TK
    nq = seq // tq
    return pl.pallas_call(
        functools.partial(_diff_attn_kernel, lam_init=lam_init, tq=tq, tk=tk),
        out_shape=jax.ShapeDtypeStruct((batch * seq, 256), BF16),
        grid=(batch, nq),
        in_specs=[_const_spec((4, DIFF_HEAD_DIM)),
                  pl.BlockSpec((1, 256, tq), lambda b, i: (b, 0, i)),
                  pl.BlockSpec((seq, 256), lambda b, i: (b, 0)),
                  pl.BlockSpec((1, 256, seq), lambda b, i: (b, 0, 0))],
        out_specs=pl.BlockSpec((tq, 256), lambda b, i: (b * nq + i, 0)),
        scratch_shapes=[pltpu.VMEM((tk, 2 * DIFF_HEADS * tq), F32), pltpu.VMEM((tk, 2 * DIFF_HEADS * tq), F32)],
        compiler_params=_params(("parallel", "parallel")),
        name="diff_attention",
    )(lam_vec, qat, ka, vat)


def _hgrn_kernel(lbl_ref, hb_ref, o_ref, st_ref, pstk_ref, *, layer, tc):
    cz = HGRN_CHUNK
    w = 256

    @pl.when(pl.program_id(1) == 0)
    def _():
        st_ref[...] = jnp.zeros_like(st_ref)

    lg = lbl_ref[...]
    e = jnp.exp(lg - jnp.max(lg, axis=0, keepdims=True))
    lw = e / jnp.sum(e, axis=0, keepdims=True)
    lb = jnp.sum(lw[0:layer + 1], axis=0, keepdims=True) - lw[0:1]

    ri = lax.broadcasted_iota(jnp.int32, (cz, cz), 0)
    ci = lax.broadcasted_iota(jnp.int32, (cz, cz), 1)
    tri = (ri >= ci).astype(F32)
    rb = lax.broadcasted_iota(jnp.int32, (w, w), 0) // HGRN_DIM
    cb = lax.broadcasted_iota(jnp.int32, (w, w), 1) // HGRN_DIM
    same_head = rb == cb
    head_ones = same_head.astype(BF16)
    trows = {r: r + lax.broadcasted_iota(jnp.int32, (cz - r, w), 0) for r in range(0, cz, 16)}

    def chunk(c, carry):
        r0 = pl.multiple_of(c * cz, cz)
        q = hb_ref[pl.ds(r0, cz), 0:256]
        fp = hb_ref[pl.ds(r0, cz), 256:512]
        v = hb_ref[pl.ds(r0, cz), 512:768]
        g = hb_ref[pl.ds(r0, cz), 768:1024]
        qf = q * _sigmoid(q)
        f = lb + (1.0 - lb) * jax.nn.sigmoid(fp)
        log_f = jnp.log(jnp.maximum(f, HGRN_MIN_FORGET))
        kf = (1.0 - lb) * jax.nn.sigmoid(-fp)
        bc = jnp.dot(tri, log_f, preferred_element_type=F32, precision=lax.Precision.HIGHEST)
        st = st_ref[...]
        o = _dot_nt((qf * jnp.exp(bc)).astype(BF16), st.astype(BF16))
        for s in range(cz):
            r_lo = (s // 16) * 16
            arg = bc[r_lo:, :] - bc[s:s + 1, :]
            if s > r_lo:
                arg = jnp.where(trows[r_lo] >= s, arg, NEG_BIG)
            p = qf[r_lo:, :] * kf[s:s + 1, :] * jnp.exp(arg)
            if r_lo:
                pstk_ref[s * cz:s * cz + r_lo, :] = jnp.zeros((r_lo, w), BF16)
            pstk_ref[s * cz + r_lo:(s + 1) * cz, :] = p.astype(BF16)
        accs = [jnp.zeros((16, w), F32) for _ in range(cz // 16)]
        for sg in range(cz // 16):
            att = _dot(pstk_ref[sg * 16 * cz:(sg + 1) * 16 * cz, :], head_ones)
            for sl in range(16):
                s = sg * 16 + sl
                for j in range(sg, cz // 16):
                    accs[j] = accs[j] + att[sl * cz + 16 * j:sl * cz + 16 * j + 16, :] * v[s:s + 1, :]
        o = o + jnp.concatenate(accs, axis=0)
        b_end = bc[cz - 1:cz, :]
        kd = kf * jnp.exp(b_end - bc)
        upd = _dot(v.T.astype(BF16), kd.astype(BF16))
        st_ref[...] = st * jnp.exp(b_end) + jnp.where(same_head, upd, 0.0)
        ms = _dot(o * o, head_ones.astype(F32)) * (1.0 / HGRN_DIM)
        y = o * lax.rsqrt(ms + NORM_EPS)
        o_ref[pl.ds(r0, cz), :] = (y * (g * _sigmoid(g))).astype(BF16)
        return carry

    def group(gi, carry):
        for u in range(HGRN_UNROLL):
            chunk(gi * HGRN_UNROLL + u, carry)
        return carry

    lax.fori_loop(0, tc // cz // HGRN_UNROLL, group, 0)


def _hgrn(lb_logits, hb, layer, batch, seq):
    tc = HGRN_TC
    nt = seq // tc
    cz = HGRN_CHUNK
    return pl.pallas_call(
        functools.partial(_hgrn_kernel, layer=layer, tc=tc),
        out_shape=jax.ShapeDtypeStruct((batch * seq, 256), BF16),
        grid=(batch, nt),
        in_specs=[_const_spec(lb_logits.shape),
                  pl.BlockSpec((tc, 1024), lambda b, i: (b * nt + i, 0))],
        out_specs=pl.BlockSpec((tc, 256), lambda b, i: (b * nt + i, 0)),
        scratch_shapes=[pltpu.VMEM((256, 256), F32), pltpu.VMEM((cz * cz, 256), BF16)],
        compiler_params=_params(("parallel", "arbitrary")),
        name="hgrn2",
    )(lb_logits, hb)


def _dsa_kernel(qdt_ref, iqt_ref, iwt_ref, dkv_ref, dkvt_ref, ikw_ref, o_ref, key_ref, bias_ref, half_ref,
                sa_ref, sb_ref, *, tq, tk, n_sel):
    q0 = pl.program_id(1) * tq
    nkb = q0 // tk + 1
    key_i = lax.broadcasted_iota(jnp.int32, (tk, tq), 0)
    qry_i = q0 + lax.broadcasted_iota(jnp.int32, (tk, tq), 1)
    grp = tk // 8
    rows8 = lambda x: x.reshape(grp, 8, tq)
    iqt = iqt_ref[0]
    zpad = jnp.zeros((LANES - DSA_IDX_DIM, tq), BF16)
    iqz = jnp.concatenate([jnp.concatenate([iqt[hd * DSA_IDX_DIM:(hd + 1) * DSA_IDX_DIM, :], zpad], axis=0)
                           for hd in range(DSA_IDX_HEADS)], axis=1)
    iw = iwt_ref[0]

    last = nkb - 1

    def score_block(kb, carry, masked):
        k0 = pl.multiple_of(kb * tk, tk)
        sh = jnp.maximum(_dot(ikw_ref[pl.ds(k0, tk), :], iqz), 0.0)
        sc = jnp.zeros((tk, tq), F32)
        for hd in range(DSA_IDX_HEADS):
            sc = sc + sh[:, hd * tq:(hd + 1) * tq] * iw[hd:hd + 1, :]
        sc = sc + 0.0
        if masked:
            sc = jnp.where(k0 + key_i <= qry_i, sc, -jnp.inf)
        bits = pltpu.bitcast(sc, jnp.int32)
        key = jnp.where(bits < 0, bits ^ jnp.int32(0x7FFFFFFF), bits)
        key_ref[pl.ds(k0, tk), :] = key
        half_ref[pl.ds(k0, tk), :] = (key >> 16).astype(jnp.int16)
        return carry

    lax.fori_loop(0, last, functools.partial(score_block, masked=False), 0)
    score_block(last, 0, True)

    one16, zero16 = jnp.ones((), jnp.int16), jnp.zeros((), jnp.int16)
    low16 = np.int16(-2 ** 15)

    def count16(limit, strict):
        def body(kb, acc):
            k0 = pl.multiple_of(kb * tk, tk)
            for c in range(tk // 128):
                blk = half_ref[pl.ds(k0 + 128 * c, 128), :].reshape(8, 16, tq)
                hit = jnp.where(blk > limit if strict else blk >= limit, one16, zero16)
                parts = [hit[j] for j in range(8)]
                while len(parts) > 1:
                    parts = [a + b for a, b in zip(parts[0::2], parts[1::2])]
                acc = acc + parts[0]
            return acc
        acc = lax.fori_loop(0, nkb, body, jnp.zeros((16, tq), jnp.int16))
        return jnp.broadcast_to(jnp.sum(acc.astype(jnp.int32), axis=0, keepdims=True), (16, tq))

    def search16(need):
        t = jnp.full((16, tq), -2 ** 15, jnp.int32)
        for bit in range(15, -1, -1):
            trial = t + 2 ** bit
            t = jnp.where(count16(trial.astype(jnp.int16), False) >= need, trial, t)
        return t

    t_hi = search16(n_sel)
    t_hi16 = t_hi.astype(jnp.int16)
    need_lo = n_sel - count16(t_hi16, True)

    def low_block(kb, carry):
        k0 = pl.multiple_of(kb * tk, tk)
        lo = ((key_ref[pl.ds(k0, tk), :] & 0xFFFF) - 2 ** 15).astype(jnp.int16).reshape(tk // 16, 16, tq)
        hi = half_ref[pl.ds(k0, tk), :].reshape(tk // 16, 16, tq)
        half_ref[pl.ds(k0, tk), :] = jnp.where(hi == t_hi16, lo, low16).reshape(tk, tq)
        return carry

    lax.fori_loop(0, nkb, low_block, 0)
    t_lo = search16(need_lo)
    thr = ((t_hi << 16) | (t_lo + 2 ** 15))[0:8, :]

    need = (need_lo - count16(t_lo.astype(jnp.int16), True))[0:8, :].astype(F32)
    ur = lax.broadcasted_iota(jnp.int32, (LANES, LANES), 0)
    uc = lax.broadcasted_iota(jnp.int32, (LANES, LANES), 1)
    earlier = (uc < ur).astype(BF16)
    ones8 = jnp.ones((8, LANES), BF16)
    key_s = lax.broadcasted_iota(jnp.int32, (LANES, tq), 0)
    qry_s = q0 + lax.broadcasted_iota(jnp.int32, (LANES, tq), 1)

    def select_block(kb, seen, masked):
        k0 = pl.multiple_of(kb * tk, tk)
        for g in range(tk // LANES):
            r0 = g * LANES
            blk = key_ref[pl.ds(k0 + r0, LANES), :].reshape(LANES // 8, 8, tq)
            eq = blk == thr[None]
            eqb = jnp.where(eq, 1.0, 0.0).reshape(LANES, tq).astype(BF16)
            rank = _dot(earlier, eqb).reshape(LANES // 8, 8, tq) + seen[None]
            tie_bias = jnp.where(rank < need[None], 0.0, NEG_BIG)
            bias = jnp.where(blk > thr[None], 0.0, jnp.where(eq, tie_bias, NEG_BIG)).reshape(LANES, tq)
            if masked:
                bias = jnp.where(k0 + r0 + key_s <= qry_s, bias, NEG_BIG)
            bias_ref[pl.ds(k0 + r0, LANES), :] = bias
            seen = seen + _dot(ones8, eqb)
        return seen

    seen = lax.fori_loop(0, last, functools.partial(select_block, masked=False), jnp.zeros((8, tq), F32))
    select_block(last, seen, True)

    qdt = qdt_ref[0]
    zq = jnp.zeros((LANES - DSA_HEAD_DIM, tq), BF16)
    qz = jnp.concatenate([jnp.concatenate([qdt[hd * DSA_HEAD_DIM:(hd + 1) * DSA_HEAD_DIM, :], zq], axis=0)
                          for hd in range(DSA_HEADS)], axis=1)
    wide = DSA_HEADS * tq

    def scores(kb, dst_ref):
        k0 = pl.multiple_of(kb * tk, tk)
        dst_ref[...] = _dot(dkv_ref[pl.ds(k0, tk), :], qz)

    def absorb(src_ref, kb, carry):
        m_i, l_i, acc = carry
        k0 = pl.multiple_of(kb * tk, tk)
        bias = bias_ref[pl.ds(k0, tk), :]
        s = src_ref[...] + jnp.concatenate([bias] * DSA_HEADS, axis=1)
        m_new = jnp.maximum(m_i, jnp.max(s, axis=0, keepdims=True))
        p = jnp.exp2(s - m_new)
        alpha = jnp.exp2(m_i - m_new)
        l_new = alpha * l_i + jnp.sum(p, axis=0, keepdims=True)
        pv = _dot(dkvt_ref[0, DSA_HEAD_DIM:, pl.ds(k0, tk)], p.astype(BF16))
        return m_new, l_new, alpha * acc + pv

    def pair(j, carry):
        kb = 2 * j
        scores(kb + 1, sb_ref)
        carry = absorb(sa_ref, kb, carry)
        scores(kb + 2, sa_ref)
        return absorb(sb_ref, kb + 1, carry)

    init = (jnp.full((1, wide), NEG_BIG, F32), jnp.zeros((1, wide), F32), jnp.zeros((DSA_HEAD_DIM, wide), F32))
    last = nkb - 1
    scores(0, sa_ref)
    carry = lax.fori_loop(0, last // 2, pair, init)

    def tail_odd(carry):
        scores(last, sb_ref)
        return absorb(sb_ref, last, absorb(sa_ref, last - 1, carry))

    def tail_even(carry):
        return absorb(sa_ref, last, carry)

    _, l_f, acc = lax.cond(last % 2 == 1, tail_odd, tail_even, carry)
    o_all = acc / l_f
    o_ref[...] = jnp.concatenate([o_all[:, hd * tq:(hd + 1) * tq] for hd in range(DSA_HEADS)],
                                 axis=0).T.astype(BF16)


def _dsa(qdt, iqt, iwt, dkv, dkvt, ikw, batch, seq):
    tq, tk = DSA_TQ, DSA_TK
    nq = seq // tq
    n_sel = min(DSA_TOPK, seq // 4)
    return pl.pallas_call(
        functools.partial(_dsa_kernel, tq=tq, tk=tk, n_sel=n_sel),
        out_shape=jax.ShapeDtypeStruct((batch * seq, 256), BF16),
        grid=(batch, nq),
        in_specs=[pl.BlockSpec((1, 256, tq), lambda b, i: (b, 0, i)),
                  pl.BlockSpec((1, 128, tq), lambda b, i: (b, 0, i)),
                  pl.BlockSpec((1, 8, tq), lambda b, i: (b, 0, i)),
                  pl.BlockSpec((seq, 128), lambda b, i: (b, 0)),
                  pl.BlockSpec((1, 128, seq), lambda b, i: (b, 0, 0)),
                  pl.BlockSpec((seq, 128), lambda b, i: (b, 0))],
        out_specs=pl.BlockSpec((tq, 256), lambda b, i: (b * nq + i, 0)),
        scratch_shapes=[pltpu.VMEM((seq, tq), jnp.int32), pltpu.VMEM((seq, tq), F32),
                        pltpu.VMEM((seq, tq), jnp.int16),
                        pltpu.VMEM((tk, DSA_HEADS * tq), F32), pltpu.VMEM((tk, DSA_HEADS * tq), F32)],
        compiler_params=_params(("parallel", "parallel")),
        name="dsa",
    )(qdt, iqt, iwt, dkv, dkvt, ikw)


def _mem_kv_kernel(mem_ref, g_ref, w_ref, kt_ref, v_ref):
    mn = _rms(mem_ref[0], g_ref[...]).astype(BF16)
    kv = _dot(mn, w_ref[...])
    kt_ref[0] = kv[:, :256].T.astype(BF16)
    v_ref[0] = kv[:, 256:].astype(BF16)


def _mem_kv(mem, gain, w_kv):
    b, m, d = mem.shape
    return pl.pallas_call(
        _mem_kv_kernel,
        out_shape=(jax.ShapeDtypeStruct((b, 256, m), BF16), jax.ShapeDtypeStruct((b, m, 256), BF16)),
        grid=(b,),
        in_specs=[pl.BlockSpec((1, m, d), lambda i: (i, 0, 0)), _const_spec((1, d)), _const_spec((d, 512))],
        out_specs=(pl.BlockSpec((1, 256, m), lambda i: (i, 0, 0)), pl.BlockSpec((1, m, 256), lambda i: (i, 0, 0))),
        compiler_params=_params(("parallel",)),
        name="mem_kv",
    )(mem, gain, w_kv)


def _merge_kernel(x_ref, ya_ref, yb_ref, yc_ref, yd_ref, gmix_ref, wg_ref, wbr_ref, wout_ref,
                  gq_ref, wq_ref, mkt_ref, mv_ref, wo_ref, gffn_ref, wr_ref, br_ref,
                  x2_ref, hp_ref, rt_ref, cnt_ref, run_ref, *, tm):
    x = x_ref[...]
    d = x.shape[-1]
    h = _rms(x, gmix_ref[...]).astype(BF16)
    merged = jnp.zeros((tm, d), F32)
    for n, y_ref in enumerate((ya_ref, yb_ref, yc_ref, yd_ref)):
        wn = wg_ref[:, n * d:n * d + d + LANES][:, GATE_SHIFT:GATE_SHIFT + d]
        gate = _sigmoid(_dot(h, wn))
        merged = merged + gate * _dot(y_ref[...], wbr_ref[n])
    x1 = x + _dot(merged.astype(BF16), wout_ref[...])
    h2 = _rms(x1, gq_ref[...]).astype(BF16)
    q = (_dot(h2, wq_ref[...]) * (MEM_HEAD_DIM ** -0.5)).astype(BF16)
    lane_head = lax.broadcasted_iota(jnp.int32, (tm, 256), 1) // MEM_HEAD_DIM
    mv = mv_ref[0]
    o = jnp.zeros((tm, 256), F32)
    for hd in range(MEM_HEADS):
        s = _dot(q[:, hd * MEM_HEAD_DIM:(hd + 1) * MEM_HEAD_DIM], mkt_ref[0, hd * MEM_HEAD_DIM:(hd + 1) * MEM_HEAD_DIM, :])
        p = jnp.exp(s - jnp.max(s, axis=-1, keepdims=True))
        p = p / jnp.sum(p, axis=-1, keepdims=True)
        o = o + jnp.where(lane_head == hd, _dot(p.astype(BF16), mv), 0.0)
    x2 = x1 + _dot(o.astype(BF16), wo_ref[...])
    x2_ref[...] = x2
    h3f = _rms(x2, gffn_ref[...])
    hp_ref[...] = _pack_bf16_pairs(h3f)
    h3 = h3f.astype(BF16)
    logits = _dot(h3, wr_ref[...]) + br_ref[...]
    lane = lax.broadcasted_iota(jnp.int32, (tm, LANES), 1)
    gl = jnp.where(lane < MOE_GROUPS, logits, -jnp.inf)
    gmax = jnp.max(gl, axis=-1, keepdims=True)
    gsel = jnp.min(jnp.where(gl == gmax, lane, LANES), axis=-1, keepdims=True)
    pg_sel = 1.0 / jnp.sum(jnp.exp(gl - gmax), axis=-1, keepdims=True)
    in_group = (lane - ROUTER_EXPERT_LANE) // MOE_EXPERTS_PER_GROUP == gsel
    el = jnp.where(in_group, logits, -jnp.inf)
    m1 = jnp.max(el, axis=-1, keepdims=True)
    i1 = jnp.min(jnp.where(el == m1, lane, LANES), axis=-1, keepdims=True)
    el2 = jnp.where(lane == i1, -jnp.inf, el)
    m2 = jnp.max(el2, axis=-1, keepdims=True)
    i2 = jnp.min(jnp.where(el2 == m2, lane, LANES), axis=-1, keepdims=True)
    e21 = jnp.exp(m2 - m1)
    c1 = pg_sel / (1.0 + e21)
    @pl.when(pl.program_id(0) == 0)
    def _():
        run_ref[...] = jnp.zeros_like(run_ref)

    oh1 = jnp.where(lane == i1, 1.0, 0.0)
    oh2 = jnp.where(lane == i2, 1.0, 0.0)
    both = oh1 + oh2
    tr = lax.broadcasted_iota(jnp.int32, (tm, tm), 0)
    tc = lax.broadcasted_iota(jnp.int32, (tm, tm), 1)
    before = _dot(jnp.where(tc < tr, 1.0, 0.0).astype(BF16), both.astype(BF16)) + run_ref[0:1, :]
    r1 = jnp.sum(oh1 * before, axis=-1, keepdims=True)
    r2 = jnp.sum(oh2 * before, axis=-1, keepdims=True)
    total = run_ref[...] + jnp.sum(both, axis=0, keepdims=True)
    run_ref[...] = total
    cnt_ref[...] = total
    ids = (jnp.where(lane == 0, i1, i2) - ROUTER_EXPERT_LANE).astype(F32)
    rt_ref[...] = jnp.where(lane < 2, ids, jnp.where(lane == 2, c1, jnp.where(lane == 3, c1 * e21,
                            jnp.where(lane == 4, r1, jnp.where(lane == 5, r2, 0.0)))))


def _merge(x, ys, gmix, wg, wbr, wout, gq, wq, mkt, mv, wo, gffn, wr, br, batch, seq):
    n, d = x.shape
    tm = MERGE_TM
    spt = seq // tm
    m = mv.shape[1]
    tok = lambda w: pl.BlockSpec((tm, w), lambda i: (i, 0))
    return pl.pallas_call(
        functools.partial(_merge_kernel, tm=tm),
        out_shape=(jax.ShapeDtypeStruct((n, d), F32), jax.ShapeDtypeStruct((n, d // 2), F32),
                   jax.ShapeDtypeStruct((n, LANES), F32), jax.ShapeDtypeStruct((8, LANES), F32)),
        grid=(n // tm,),
        in_specs=[tok(d), tok(256), tok(256), tok(256), tok(256),
                  _const_spec((1, d)), _const_spec((d, N_BRANCH * d + LANES)), _const_spec((N_BRANCH, 256, d)),
                  _const_spec((d, d)), _const_spec((1, d)), _const_spec((d, 256)),
                  pl.BlockSpec((1, 256, m), lambda i: (i // spt, 0, 0)),
                  pl.BlockSpec((1, m, 256), lambda i: (i // spt, 0, 0)),
                  _const_spec((256, d)), _const_spec((1, d)), _const_spec((d, LANES)), _const_spec((1, LANES))],
        out_specs=(tok(d), tok(d // 2), tok(LANES), pl.BlockSpec((8, LANES), lambda i: (0, 0))),
        scratch_shapes=[pltpu.VMEM((8, LANES), F32)],
        compiler_params=_params(("arbitrary",)),
        name="merge_mem_router",
    )(x, *ys, gmix, wg, wbr, wout, gq, wq, mkt, mv, wo, gffn, wr, br)


def _sc_gather_rows(table, idx):
    _, width = table.shape
    total = idx.shape[0]
    chunk, nbuf = SC_GATHER_CHUNK, SC_GATHER_BUFS
    workers = SC_CORES * SC_SUBCORES
    per_w = total // workers
    nch = per_w // chunk
    assert total % (workers * chunk * nbuf) == 0
    mesh = plsc.VectorSubcoreMesh(core_axis_name="c", subcore_axis_name="s")

    @functools.partial(
        pl.kernel, mesh=mesh, out_type=jax.ShapeDtypeStruct((total, width), table.dtype),
        scratch_types=[pltpu.VMEM((nch, chunk), jnp.int32), pltpu.VMEM((nbuf, chunk, width), table.dtype),
                       pltpu.SemaphoreType.DMA((nbuf,)), pltpu.SemaphoreType.DMA((nbuf,))])
    def gather_kernel(table_hbm, idx_hbm, out_hbm, idx_v, rows_v, gsem, wsem):
        wid = lax.axis_index("s") * SC_CORES + lax.axis_index("c")
        pltpu.sync_copy(idx_hbm.at[wid], idx_v)

        def gather(j, slot):
            return pltpu.make_async_copy(table_hbm.at[idx_v.at[j]], rows_v.at[slot], gsem.at[slot])

        def write(j, slot):
            off = pl.multiple_of(wid * per_w + j * chunk, chunk)
            return pltpu.make_async_copy(rows_v.at[slot], out_hbm.at[pl.ds(off, chunk)], wsem.at[slot])

        for slot in range(nbuf):
            gather(slot, slot).start()

        @pl.loop(0, nch // nbuf)
        def _(g):
            for slot in range(nbuf):
                j = g * nbuf + slot
                gather(j, slot).wait()
                write(j, slot).start()
                write(j, slot).wait()

                @pl.when(j + nbuf < nch)
                def _():
                    gather(j + nbuf, slot).start()

    return gather_kernel(table, idx.reshape(workers, nch, chunk))


def _sc_scatter_rows(table, dest2, total):
    n, width = table.shape
    chunk, nbuf = SC_GATHER_CHUNK, SC_GATHER_BUFS
    workers = SC_CORES * SC_SUBCORES
    per_w = n // workers
    nch = per_w // chunk
    assert n % (workers * chunk * nbuf) == 0
    mesh = plsc.VectorSubcoreMesh(core_axis_name="c", subcore_axis_name="s")

    @functools.partial(
        pl.kernel, mesh=mesh, out_type=jax.ShapeDtypeStruct((total, width), table.dtype),
        scratch_types=[pltpu.VMEM((2, nch, chunk), jnp.int32), pltpu.VMEM((nbuf, chunk, width), table.dtype),
                       pltpu.SemaphoreType.DMA((nbuf,)), pltpu.SemaphoreType.DMA((nbuf,))])
    def scatter_kernel(table_hbm, idx_hbm, out_hbm, idx_v, rows_v, rsem, wsem):
        wid = lax.axis_index("s") * SC_CORES + lax.axis_index("c")
        pltpu.sync_copy(idx_hbm.at[wid], idx_v)

        def read(j, slot):
            off = pl.multiple_of(wid * per_w + j * chunk, chunk)
            return pltpu.make_async_copy(table_hbm.at[pl.ds(off, chunk)], rows_v.at[slot], rsem.at[slot])

        def write(j, slot, k):
            return pltpu.make_async_copy(rows_v.at[slot], out_hbm.at[idx_v.at[k, j]], wsem.at[slot])

        for slot in range(nbuf):
            read(slot, slot).start()

        @pl.loop(0, nch // nbuf)
        def _(g):
            for slot in range(nbuf):
                j = g * nbuf + slot
                read(j, slot).wait()
                write(j, slot, 0).start()
                write(j, slot, 1).start()
                write(j, slot, 0).wait()
                write(j, slot, 1).wait()

                @pl.when(j + nbuf < nch)
                def _():
                    read(j + nbuf, slot).start()

    idx = dest2.reshape(2, workers, nch, chunk).transpose(1, 0, 2, 3)
    return scatter_kernel(table, idx)


def _dispatch_plan(rt, cnt, n):
    ne, blk = MOE_N_EXPERTS, MOE_BLOCK
    n_blocks = (2 * n) // blk + ne
    experts = jnp.arange(ne, dtype=jnp.int32)
    counts = cnt[0, ROUTER_EXPERT_LANE:ROUTER_EXPERT_LANE + ne].astype(jnp.int32)
    padded = (counts + blk - 1) // blk * blk
    pend = jnp.cumsum(padded)
    pstart = pend - padded
    ids = rt[:, 0:2].astype(jnp.int32)
    pos = rt[:, 4:6].astype(jnp.int32)
    first_row = jnp.sum(jnp.where(ids[:, :, None] == experts[None, None, :], pstart[None, None, :], 0), axis=-1)
    dest2 = (first_row + pos).T
    b0 = jnp.arange(n_blocks, dtype=jnp.int32) * blk
    block_e = jnp.minimum(jnp.sum((pend[None, :] <= b0[:, None]).astype(jnp.int32), axis=1), ne - 1)
    n_valid = jnp.clip(counts[block_e] - (b0 - pstart[block_e]), 0, blk).astype(jnp.int32)
    return dest2, block_e, n_valid


def _expert_block_kernel(be_ref, nv_ref, xs_ref, wg_ref, wu_ref, wd_ref, o_ref, wgb_ref, wub_ref, wdb_ref):
    b = pl.program_id(0)
    valid = nv_ref[b]

    @pl.when((b == 0) | (be_ref[b] != be_ref[jnp.maximum(b - 1, 0)]))
    def _():
        wgb_ref[...] = wg_ref[0].astype(BF16)
        wub_ref[...] = wu_ref[0].astype(BF16)
        wdb_ref[...] = wd_ref[0].astype(BF16)

    @pl.when(valid > 0)
    def _():
        row = lax.broadcasted_iota(jnp.int32, xs_ref.shape, 0)
        words = jnp.where(row < valid, xs_ref[...], 0.0)
        h = _unpack_bf16_pairs(words).astype(BF16)
        gt = _dot(h, wgb_ref[...])
        hid = gt * _sigmoid(gt) * _dot(h, wub_ref[...])
        o_ref[...] = _pack_bf16_pairs(_dot(hid.astype(BF16), wdb_ref[...]))

    @pl.when(valid == 0)
    def _():
        o_ref[...] = jnp.zeros_like(o_ref)


def _expert_blocks(xs, wg, wu, wd, layer, block_e, n_used):
    p_rows, half = xs.shape
    d, hid = wg.shape[-2:]
    blk = MOE_BLOCK
    grid_spec = pltpu.PrefetchScalarGridSpec(
        num_scalar_prefetch=2, grid=(p_rows // blk,),
        in_specs=[pl.BlockSpec((blk, half), lambda b, be, nu: (b, 0)),
                  pl.BlockSpec((None, 1, d, hid), lambda b, be, nu: (layer, be[b], 0, 0)),
                  pl.BlockSpec((None, 1, d, hid), lambda b, be, nu: (layer, be[b], 0, 0)),
                  pl.BlockSpec((None, 1, hid, d), lambda b, be, nu: (layer, be[b], 0, 0))],
        out_specs=pl.BlockSpec((blk, half), lambda b, be, nu: (b, 0)),
        scratch_shapes=[pltpu.VMEM((d, hid), BF16), pltpu.VMEM((d, hid), BF16), pltpu.VMEM((hid, d), BF16)])
    return pl.pallas_call(
        _expert_block_kernel, out_shape=jax.ShapeDtypeStruct((p_rows, half), F32), grid_spec=grid_spec,
        compiler_params=_params(("arbitrary",)),
        name="moe_expert_blocks",
    )(block_e, n_used, xs, wg, wu, wd)


def _combine_kernel(x_ref, y1_ref, y2_ref, rt_ref, gfin_ref, o_ref, *, final_norm):
    out = (x_ref[...] + rt_ref[:, 2:3] * _unpack_bf16_pairs(y1_ref[...])
           + rt_ref[:, 3:4] * _unpack_bf16_pairs(y2_ref[...]))
    o_ref[...] = _rms(out, gfin_ref[...]) if final_norm else out


def _combine(x2, y_halves, rt, gfin, final_norm):
    n, d = x2.shape
    tm = COMBINE_TM
    nt = n // tm
    return pl.pallas_call(
        functools.partial(_combine_kernel, final_norm=final_norm),
        out_shape=jax.ShapeDtypeStruct((n, d), F32),
        grid=(nt,),
        in_specs=[pl.BlockSpec((tm, d), lambda i: (i, 0)),
                  pl.BlockSpec((tm, d // 2), lambda i: (i, 0)),
                  pl.BlockSpec((tm, d // 2), lambda i: (i + nt, 0)),
                  pl.BlockSpec((tm, LANES), lambda i: (i, 0)), _const_spec((1, d))],
        out_specs=pl.BlockSpec((tm, d), lambda i: (i, 0)),
        compiler_params=_params(("parallel",)), name="moe_combine",
    )(x2, y_halves, y_halves, rt, gfin)


def _moe(x2, hp, rt, cnt, wg, wu, wd, layer, gfin, final_norm):
    n = x2.shape[0]
    dest2, block_e, n_valid = _dispatch_plan(rt, cnt, n)
    xs = _sc_scatter_rows(hp, dest2, block_e.shape[0] * MOE_BLOCK)
    yb = _expert_blocks(xs, wg, wu, wd, layer, block_e, n_valid)
    y_halves = _sc_gather_rows(yb, dest2.reshape(2 * n))
    return _combine(x2, y_halves, rt, gfin, final_norm)


def kernel(x, mem, positions, norm_mix, w_in, diff_lambda, hgrn_lb_logits, spatial_w, spatial_b, w_branch, w_out,
           norm_mem_q, norm_mem_kv, w_mem_q, w_mem_kv, w_mem_o, norm_ffn, w_router_group, b_router_group,
           w_router_expert, b_router_expert, w_exp_gate, w_exp_up, w_exp_down, norm_final):
    batch, seq, d = x.shape
    depth = w_in.shape[0]
    n = batch * seq
    xf = x.reshape(n, d)
    tabs = _rope_tables(positions)
    w_all = _w_in_bf16(w_in)
    row = lambda v: v.reshape(1, -1).astype(F32)
    for l in range(depth):
        lam_init = 0.8 - 0.6 * math.exp(-0.3 * l)
        g0 = C_GATE - GATE_SHIFT
        w_gate = lax.slice(w_all, (l, 0, g0), (l + 1, d, g0 + N_BRANCH * d + LANES)).reshape(d, -1)
        sw = spatial_w[l].reshape(SGU_GROUPS * SGU_CHUNK, SGU_CHUNK)
        sb = jnp.repeat(spatial_b[l].T, SGU_GROUP_DIM, axis=1)
        qat, ka, vat, hb, y_c, qdt, iqt, dkv, dkvt, ikw, iwt = _projection(
            xf, row(norm_mix[l]), w_all, l, tabs, sw, sb, batch, seq)
        y_a = _diff_attention(diff_lambda[l], qat, ka, vat, lam_init, batch, seq)
        y_b = _hgrn(hgrn_lb_logits, hb, l, batch, seq)
        y_d = _dsa(qdt, iqt, iwt, dkv, dkvt, ikw, batch, seq)
        mkt, mv = _mem_kv(mem, row(norm_mem_kv[l]), w_mem_kv[l].astype(BF16))
        e0, e1 = ROUTER_EXPERT_LANE, ROUTER_EXPERT_LANE + MOE_N_EXPERTS
        wr = jnp.zeros((d, LANES), F32)
        wr = wr.at[:, :MOE_GROUPS].set(w_router_group[l]).at[:, e0:e1].set(w_router_expert[l]).astype(BF16)
        br = jnp.zeros((1, LANES), F32)
        br = br.at[0, :MOE_GROUPS].set(b_router_group[l]).at[0, e0:e1].set(b_router_expert[l])
        x2, hp, rt, cnt = _merge(xf, (y_a, y_b, y_c, y_d), row(norm_mix[l]), w_gate, w_branch[l].astype(BF16),
                                 w_out[l].astype(BF16), row(norm_mem_q[l]), w_mem_q[l].astype(BF16), mkt, mv,
                                 w_mem_o[l].astype(BF16), row(norm_ffn[l]), wr, br, batch, seq)
        xf = _moe(x2, hp, rt, cnt, w_exp_gate, w_exp_up, w_exp_down, l,
                  row(norm_final), final_norm=(l == depth - 1))
    return xf.reshape(batch, seq, d)
```

```python
import functools
import math

import numpy as np
import jax
import jax.numpy as jnp
from jax import lax
from jax.experimental import pallas as pl
from jax.experimental.pallas import tpu as pltpu
from jax.experimental.pallas import tpu_sc as plsc

F32 = jnp.float32
BF16 = jnp.bfloat16

NORM_EPS = 1e-6
ROPE_THETA = 10000.0
NEG_BIG = -1e30

N_BRANCH = 4
DIFF_HEADS = 4
DIFF_HEAD_DIM = 32
HGRN_DIM = 64
HGRN_CHUNK = 32
HGRN_UNROLL = 8
HGRN_MIN_FORGET = 1e-30
SGU_GROUPS = 4
SGU_GROUP_DIM = 64
SGU_CHUNK = 128
DSA_HEADS = 4
DSA_HEAD_DIM = 64
DSA_IDX_HEADS = 4
DSA_IDX_DIM = 32
DSA_TOPK = 256
MEM_HEADS = 4
MEM_HEAD_DIM = 64
MOE_GROUPS = 4
MOE_EXPERTS_PER_GROUP = 8
MOE_N_EXPERTS = 32
MOE_BLOCK = 512
ROUTER_EXPERT_LANE = 32
SC_CORES = 2
SC_SUBCORES = 16
SC_GATHER_CHUNK = 16
SC_GATHER_BUFS = 4

LANES = 128
VMEM_LIMIT = 56 * 1024 * 1024

PROJ_TM = 1024
DIFF_TQ = 512
DIFF_TK = 512
HGRN_TC = 512
DSA_TQ = 512
DSA_TK = 512
MERGE_TM = 1024
COMBINE_TM = 512
ROPE_TM = 1024

C_AQ, C_AK, C_AV = 0, 256, 512
C_HB = 768
C_UV = 1792
C_DQ = 2304
C_DKV = 2560
C_IQ = 2688
C_IKW = 2816
IW_LANE = 32
C_GATE = 2852
GATE_SHIFT = C_GATE % 128
C_TOTAL = 2944
LOG2E = math.log2(math.e)


def _params(sem):
    return pltpu.CompilerParams(dimension_semantics=sem, vmem_limit_bytes=VMEM_LIMIT)


def _const_spec(shape):
    nd = len(shape)
    return pl.BlockSpec(shape, lambda *_: (0,) * nd, pipeline_mode=pl.Buffered(1))


def _rms(xf, gain=None):
    y = xf * lax.rsqrt(jnp.mean(xf * xf, axis=-1, keepdims=True) + NORM_EPS)
    return y if gain is None else y * gain


def _sigmoid(x):
    return 0.5 * jnp.tanh(0.5 * x) + 0.5


def _pack_bf16_pairs(x):
    w = x.shape[-1] // 2
    xb = x.astype(BF16).astype(F32)
    lo = lax.shift_right_logical(pltpu.bitcast(xb[:, :w], jnp.int32), 16)
    hi = pltpu.bitcast(xb[:, w:], jnp.int32) & jnp.int32(-65536)
    return pltpu.bitcast(hi | lo, F32)


def _unpack_bf16_pairs(words):
    bits = pltpu.bitcast(words, jnp.int32)
    lo = pltpu.bitcast(bits << 16, F32)
    hi = pltpu.bitcast(bits & jnp.int32(-65536), F32)
    return jnp.concatenate([lo, hi], axis=1)


def _dot(a, b):
    return jnp.dot(a, b, preferred_element_type=F32)


def _dot_nt(a, b):
    return lax.dot_general(a, b, (((1,), (1,)), ((), ())), preferred_element_type=F32)


def _rope_table_kernel(pos_ref, frq_ref, sgn_ref, c32_ref, s32_ref, c64_ref, s64_ref):
    pos = pos_ref[...].astype(F32)
    twice = lambda t: jnp.concatenate([t, t], axis=1)
    a32 = pos * frq_ref[0:1, :LANES]
    a64 = pos * frq_ref[1:2, :LANES]
    c32_ref[...] = twice(jnp.cos(a32))
    s32_ref[...] = twice(jnp.sin(a32) * sgn_ref[0:1, :LANES])
    c64_ref[...] = twice(jnp.cos(a64))
    s64_ref[...] = twice(jnp.sin(a64) * sgn_ref[1:2, :LANES])


def _rope_tables(positions):
    n = positions.size
    pos = positions.reshape(n, 1).astype(jnp.int32)
    lane = np.arange(256)
    inv32 = ROPE_THETA ** (-jnp.arange(16, dtype=F32) * (2.0 / 32))
    inv64 = ROPE_THETA ** (-jnp.arange(32, dtype=F32) * (2.0 / 64))
    frq = jnp.stack([inv32[lane % 16], inv64[lane % 32]])
    sgn = jnp.asarray(np.stack([np.where(lane % 32 < 16, -1.0, 1.0),
                                np.where(lane % 64 < 32, -1.0, 1.0)]), F32)
    tm = ROPE_TM
    tab = jax.ShapeDtypeStruct((n, 256), F32)
    return pl.pallas_call(
        _rope_table_kernel,
        out_shape=(tab, tab, tab, tab),
        grid=(n // tm,),
        in_specs=[pl.BlockSpec((tm, 1), lambda i: (i, 0)), _const_spec((2, 256)), _const_spec((2, 256))],
        out_specs=tuple(pl.BlockSpec((tm, 256), lambda i: (i, 0)) for _ in range(4)),
        compiler_params=_params(("parallel",)),
        name="rope_tables",
    )(pos, frq, sgn)


def _w_in_kernel(wt_ref, o_ref, *, width):
    row = lax.broadcasted_iota(jnp.int32, (LANES, wt_ref.shape[-1]), 0)
    valid = row < width - pl.program_id(0) * LANES
    for layer in range(o_ref.shape[0]):
        o_ref[layer] = jnp.where(valid, wt_ref[:, layer, :], 0.0).T.astype(BF16)


def _w_in_bf16(w_in):
    depth, d, width = w_in.shape
    nblk = pl.cdiv(width, LANES)
    return pl.pallas_call(
        functools.partial(_w_in_kernel, width=width), out_shape=jax.ShapeDtypeStruct((depth, d, nblk * LANES), BF16),
        grid=(nblk,), in_specs=[pl.BlockSpec((LANES, depth, d), lambda i: (i, 0, 0))],
        out_specs=pl.BlockSpec((depth, d, LANES), lambda i: (0, 0, i)),
        compiler_params=_params(("parallel",)), name="w_in_bf16",
    )(jnp.transpose(w_in, (2, 0, 1)))


def _gelu_tanh(x):
    return 0.5 * x * (1.0 + jnp.tanh(math.sqrt(2.0 / math.pi) * (x + 0.044715 * (x * x * x))))


def _rope(x, cos, sin_signed, half):
    w = x.shape[-1]
    lane = lax.broadcasted_iota(jnp.int32, x.shape, 1)
    partner = jnp.where(lane % (2 * half) < half, pltpu.roll(x, w - half, 1), pltpu.roll(x, half, 1))
    return x * cos + partner * sin_signed


def _proj_kernel(x_ref, g_ref, w_ref, c32_ref, s32_ref, c64_ref, s64_ref, sw_ref, sb_ref,
                 qat_ref, ka_ref, vat_ref, hb_ref, yc_ref, qdt_ref, iqt_ref, dkv_ref, dkvt_ref, ikw_ref, iwt_ref,
                 *, tm):
    h = _rms(x_ref[...], g_ref[...]).astype(BF16)

    def proj(c0, width):
        return _dot(h, w_ref[:, c0:c0 + width])

    c32, s32, c64, s64 = c32_ref[...], s32_ref[...], c64_ref[...], s64_ref[...]
    qat_ref[0] = (_rope(proj(C_AQ, 256), c32, s32, 16) * (DIFF_HEAD_DIM ** -0.5 * LOG2E)).T.astype(BF16)
    ka_ref[...] = _rope(proj(C_AK, 256), c32, s32, 16).astype(BF16)
    vat_ref[0] = proj(C_AV, 256).astype(BF16).T
    hb_ref[...] = proj(C_HB, 1024)
    qdt_ref[0] = (_rope(proj(C_DQ, 256), c64, s64, 32) * (DSA_HEAD_DIM ** -0.5 * LOG2E)).T.astype(BF16)
    iqt_ref[0] = _rope(proj(C_IQ, 128), c32[:, :128], s32[:, :128], 16).T.astype(BF16)
    lane = lax.broadcasted_iota(jnp.int32, (tm, 128), 1)
    is_k = lane < DSA_HEAD_DIM
    dkv = _rope(proj(C_DKV, 128), jnp.where(is_k, c64[:, :128], 1.0), jnp.where(is_k, s64[:, :128], 0.0), 32)
    is_ik = lane < DSA_IDX_DIM
    ikw = _rope(proj(C_IKW, 128), jnp.where(is_ik, c32[:, :128], 1.0), jnp.where(is_ik, s32[:, :128], 0.0), 16)
    dkv_ref[...] = dkv.astype(BF16)
    dkvt_ref[0] = dkv.T.astype(BF16)
    ikw_ref[...] = ikw.astype(BF16)
    iw_scale = DSA_IDX_HEADS ** -0.5 * DSA_IDX_DIM ** -0.5
    iwt_ref[0] = (ikw * iw_scale).T[IW_LANE:IW_LANE + 8, :]
    uv = _gelu_tanh(proj(C_UV, 512))
    u, v = uv[:, :256], uv[:, 256:]
    mu = jnp.mean(v, axis=-1, keepdims=True)
    vc = v - mu
    vn = (vc * lax.rsqrt(jnp.mean(vc * vc, axis=-1, keepdims=True) + NORM_EPS)).astype(BF16)
    r = lax.broadcasted_iota(jnp.int32, (SGU_GROUPS * SGU_CHUNK, SGU_CHUNK), 0)
    c = lax.broadcasted_iota(jnp.int32, (SGU_GROUPS * SGU_CHUNK, SGU_CHUNK), 1)
    wt = jnp.where((r % SGU_CHUNK) >= c, sw_ref[...], 0.0).astype(BF16)
    lane_grp = lax.broadcasted_iota(jnp.int32, (SGU_CHUNK, 256), 1) // SGU_GROUP_DIM
    for ch in range(tm // SGU_CHUNK):
        r0 = ch * SGU_CHUNK
        full = _dot(wt, vn[r0:r0 + SGU_CHUNK, :])
        mixed = sb_ref[...]
        for g in range(SGU_GROUPS):
            mixed = mixed + jnp.where(lane_grp == g, full[g * SGU_CHUNK:(g + 1) * SGU_CHUNK, :], 0.0)
        yc_ref[r0:r0 + SGU_CHUNK, :] = (u[r0:r0 + SGU_CHUNK, :] * mixed).astype(BF16)


def _projection(x, gain, w_all, layer, tabs, sw, sb, batch, seq):
    n, d = x.shape
    tm = PROJ_TM
    spt = seq // tm
    tok = lambda w: pl.BlockSpec((tm, w), lambda i: (i, 0))
    tr = lambda rows: pl.BlockSpec((1, rows, tm), lambda i: (i // spt, 0, i % spt))
    out_shape = (
        jax.ShapeDtypeStruct((batch, 256, seq), BF16),
        jax.ShapeDtypeStruct((n, 256), BF16),
        jax.ShapeDtypeStruct((batch, 256, seq), BF16),
        jax.ShapeDtypeStruct((n, 1024), F32),
        jax.ShapeDtypeStruct((n, 256), BF16),
        jax.ShapeDtypeStruct((batch, 256, seq), BF16),
        jax.ShapeDtypeStruct((batch, 128, seq), BF16),
        jax.ShapeDtypeStruct((n, 128), BF16),
        jax.ShapeDtypeStruct((batch, 128, seq), BF16),
        jax.ShapeDtypeStruct((n, 128), BF16),
        jax.ShapeDtypeStruct((batch, 8, seq), F32),
    )
    return pl.pallas_call(
        functools.partial(_proj_kernel, tm=tm),
        out_shape=out_shape,
        grid=(n // tm,),
        in_specs=[tok(d), _const_spec((1, d)),
                  pl.BlockSpec((None, d, C_TOTAL), lambda i: (layer, 0, 0), pipeline_mode=pl.Buffered(1)),
                  tok(256), tok(256), tok(256), tok(256),
                  _const_spec((SGU_GROUPS * SGU_CHUNK, SGU_CHUNK)), _const_spec((SGU_CHUNK, 256))],
        out_specs=(tr(256), tok(256), tr(256), tok(1024), tok(256), tr(256), tr(128), tok(128), tr(128), tok(128), tr(8)),
        compiler_params=_params(("parallel",)),
        name="projection",
    )(x, gain, w_all, *tabs, sw, sb)


def _diff_attn_kernel(lam_ref, qt_ref, k_ref, vt_ref, o_ref, sa_ref, sb_ref, *, lam_init, tq, tk):
    q0 = pl.program_id(1) * tq
    kb_diag = q0 // tk
    lv = lam_ref[...]
    lam = (jnp.exp(jnp.sum(lv[0:1] * lv[1:2], axis=-1, keepdims=True))
           - jnp.exp(jnp.sum(lv[2:3] * lv[3:4], axis=-1, keepdims=True)) + lam_init)
    qt = qt_ref[0]
    feat = lax.broadcasted_iota(jnp.int32, (256, tq), 0) // DIFF_HEAD_DIM
    n_maps = 2 * DIFF_HEADS
    qz = jnp.concatenate([jnp.where(feat == i, qt, jnp.zeros_like(qt)) for i in range(n_maps)], axis=1)
    wide = n_maps * tq
    key_i = lax.broadcasted_iota(jnp.int32, (tk, wide), 0)
    qry_i = q0 + lax.broadcasted_iota(jnp.int32, (tk, wide), 1) % tq

    def scores(kb, dst_ref):
        k0 = pl.multiple_of(kb * tk, tk)
        dst_ref[...] = _dot(k_ref[pl.ds(k0, tk), :], qz)

    def absorb(src_ref, kb, carry, masked):
        m_i, l_i, acc = carry
        k0 = pl.multiple_of(kb * tk, tk)
        s = src_ref[...]
        if masked:
            s = jnp.where(k0 + key_i <= qry_i, s, NEG_BIG)
        m_new = jnp.maximum(m_i, jnp.max(s, axis=0, keepdims=True))
        p = jnp.exp2(s - m_new)
        alpha = jnp.exp2(m_i - m_new)
        l_new = alpha * l_i + jnp.sum(p, axis=0, keepdims=True)
        pb = p.astype(BF16)
        pv = jnp.concatenate(
            [_dot(vt_ref[0, hd * 64:(hd + 1) * 64, pl.ds(k0, tk)], pb[:, 2 * hd * tq:(2 * hd + 2) * tq])
             for hd in range(DIFF_HEADS)], axis=1)
        return m_new, l_new, alpha * acc + pv

    def pair(j, carry):
        kb = 2 * j
        scores(kb + 1, sb_ref)
        carry = absorb(sa_ref, kb, carry, False)
        scores(kb + 2, sa_ref)
        return absorb(sb_ref, kb + 1, carry, False)

    init = (jnp.full((1, wide), NEG_BIG, F32), jnp.zeros((1, wide), F32), jnp.zeros((64, wide), F32))
    scores(0, sa_ref)
    carry = lax.fori_loop(0, kb_diag // 2, pair, init)

    def tail_odd(carry):
        scores(kb_diag, sb_ref)
        carry = absorb(sa_ref, kb_diag - 1, carry, False)
        return absorb(sb_ref, kb_diag, carry, True)

    def tail_even(carry):
        return absorb(sa_ref, kb_diag, carry, True)

    _, l_f, acc = lax.cond(kb_diag % 2 == 1, tail_odd, tail_even, carry)
    o_all = acc / l_f
    heads = []
    for hd in range(DIFF_HEADS):
        o0 = o_all[:, 2 * hd * tq:(2 * hd + 1) * tq]
        o1 = o_all[:, (2 * hd + 1) * tq:(2 * hd + 2) * tq]
        o_h = o0 - lam * o1
        ms = jnp.mean(o_h * o_h, axis=0, keepdims=True)
        heads.append(o_h * lax.rsqrt(ms + NORM_EPS) * (1.0 - lam_init))
    o_ref[...] = jnp.concatenate(heads, axis=0).T.astype(BF16)


def _diff_attention(lam_vec, qat, ka, vat, lam_init, batch, seq):
    tq, tk = DIFF_TQ, DIFF_TK
    nq = seq // tq
    return pl.pallas_call(
        functools.partial(_diff_attn_kernel, lam_init=lam_init, tq=tq, tk=tk),
        out_shape=jax.ShapeDtypeStruct((batch * seq, 256), BF16),
        grid=(batch, nq),
        in_specs=[_const_spec((4, DIFF_HEAD_DIM)),
                  pl.BlockSpec((1, 256, tq), lambda b, i: (b, 0, i)),
                  pl.BlockSpec((seq, 256), lambda b, i: (b, 0)),
                  pl.BlockSpec((1, 256, seq), lambda b, i: (b, 0, 0))],
        out_specs=pl.BlockSpec((tq, 256), lambda b, i: (b * nq + i, 0)),
        scratch_shapes=[pltpu.VMEM((tk, 2 * DIFF_HEADS * tq), F32), pltpu.VMEM((tk, 2 * DIFF_HEADS * tq), F32)],
        compiler_params=_params(("parallel", "parallel")),
        name="diff_attention",
    )(lam_vec, qat, ka, vat)


def _hgrn_kernel(lbl_ref, hb_ref, o_ref, st_ref, pstk_ref, *, layer, tc):
    cz = HGRN_CHUNK
    w = 256

    @pl.when(pl.program_id(1) == 0)
    def _():
        st_ref[...] = jnp.zeros_like(st_ref)

    lg = lbl_ref[...]
    e = jnp.exp(lg - jnp.max(lg, axis=0, keepdims=True))
    lw = e / jnp.sum(e, axis=0, keepdims=True)
    lb = jnp.sum(lw[0:layer + 1], axis=0, keepdims=True) - lw[0:1]

    ri = lax.broadcasted_iota(jnp.int32, (cz, cz), 0)
    ci = lax.broadcasted_iota(jnp.int32, (cz, cz), 1)
    tri = (ri >= ci).astype(F32)
    rb = lax.broadcasted_iota(jnp.int32, (w, w), 0) // HGRN_DIM
    cb = lax.broadcasted_iota(jnp.int32, (w, w), 1) // HGRN_DIM
    same_head = rb == cb
    head_ones = same_head.astype(BF16)
    trows = {r: r + lax.broadcasted_iota(jnp.int32, (cz - r, w), 0) for r in range(0, cz, 16)}

    def chunk(c, carry):
        r0 = pl.multiple_of(c * cz, cz)
        q = hb_ref[pl.ds(r0, cz), 0:256]
        fp = hb_ref[pl.ds(r0, cz), 256:512]
        v = hb_ref[pl.ds(r0, cz), 512:768]
        g = hb_ref[pl.ds(r0, cz), 768:1024]
        qf = q * _sigmoid(q)
        f = lb + (1.0 - lb) * jax.nn.sigmoid(fp)
        log_f = jnp.log(jnp.maximum(f, HGRN_MIN_FORGET))
        kf = (1.0 - lb) * jax.nn.sigmoid(-fp)
        bc = jnp.dot(tri, log_f, preferred_element_type=F32, precision=lax.Precision.HIGHEST)
        st = st_ref[...]
        o = _dot_nt((qf * jnp.exp(bc)).astype(BF16), st.astype(BF16))
        for s in range(cz):
            r_lo = (s // 16) * 16
            arg = bc[r_lo:, :] - bc[s:s + 1, :]
            if s > r_lo:
                arg = jnp.where(trows[r_lo] >= s, arg, NEG_BIG)
            p = qf[r_lo:, :] * kf[s:s + 1, :] * jnp.exp(arg)
            if r_lo:
                pstk_ref[s * cz:s * cz + r_lo, :] = jnp.zeros((r_lo, w), BF16)
            pstk_ref[s * cz + r_lo:(s + 1) * cz, :] = p.astype(BF16)
        accs = [jnp.zeros((16, w), F32) for _ in range(cz // 16)]
        for sg in range(cz // 16):
            att = _dot(pstk_ref[sg * 16 * cz:(sg + 1) * 16 * cz, :], head_ones)
            for sl in range(16):
                s = sg * 16 + sl
                for j in range(sg, cz // 16):
                    accs[j] = accs[j] + att[sl * cz + 16 * j:sl * cz + 16 * j + 16, :] * v[s:s + 1, :]
        o = o + jnp.concatenate(accs, axis=0)
        b_end = bc[cz - 1:cz, :]
        kd = kf * jnp.exp(b_end - bc)
        upd = _dot(v.T.astype(BF16), kd.astype(BF16))
        st_ref[...] = st * jnp.exp(b_end) + jnp.where(same_head, upd, 0.0)
        ms = _dot(o * o, head_ones.astype(F32)) * (1.0 / HGRN_DIM)
        y = o * lax.rsqrt(ms + NORM_EPS)
        o_ref[pl.ds(r0, cz), :] = (y * (g * _sigmoid(g))).astype(BF16)
        return carry

    def group(gi, carry):
        for u in range(HGRN_UNROLL):
            chunk(gi * HGRN_UNROLL + u, carry)
        return carry

    lax.fori_loop(0, tc // cz // HGRN_UNROLL, group, 0)


def _hgrn(lb_logits, hb, layer, batch, seq):
    tc = HGRN_TC
    nt = seq // tc
    cz = HGRN_CHUNK
    return pl.pallas_call(
        functools.partial(_hgrn_kernel, layer=layer, tc=tc),
        out_shape=jax.ShapeDtypeStruct((batch * seq, 256), BF16),
        grid=(batch, nt),
        in_specs=[_const_spec(lb_logits.shape),
                  pl.BlockSpec((tc, 1024), lambda b, i: (b * nt + i, 0))],
        out_specs=pl.BlockSpec((tc, 256), lambda b, i: (b * nt + i, 0)),
        scratch_shapes=[pltpu.VMEM((256, 256), F32), pltpu.VMEM((cz * cz, 256), BF16)],
        compiler_params=_params(("parallel", "arbitrary")),
        name="hgrn2",
    )(lb_logits, hb)


def _dsa_kernel(qdt_ref, iqt_ref, iwt_ref, dkv_ref, dkvt_ref, ikw_ref, o_ref, key_ref, bias_ref, half_ref,
                sa_ref, sb_ref, *, tq, tk, n_sel):
    q0 = pl.program_id(1) * tq
    nkb = q0 // tk + 1
    key_i = lax.broadcasted_iota(jnp.int32, (tk, tq), 0)
    qry_i = q0 + lax.broadcasted_iota(jnp.int32, (tk, tq), 1)
    grp = tk // 8
    rows8 = lambda x: x.reshape(grp, 8, tq)
    iqt = iqt_ref[0]
    zpad = jnp.zeros((LANES - DSA_IDX_DIM, tq), BF16)
    iqz = jnp.concatenate([jnp.concatenate([iqt[hd * DSA_IDX_DIM:(hd + 1) * DSA_IDX_DIM, :], zpad], axis=0)
                           for hd in range(DSA_IDX_HEADS)], axis=1)
    iw = iwt_ref[0]

    last = nkb - 1

    def score_block(kb, carry, masked):
        k0 = pl.multiple_of(kb * tk, tk)
        sh = jnp.maximum(_dot(ikw_ref[pl.ds(k0, tk), :], iqz), 0.0)
        sc = jnp.zeros((tk, tq), F32)
        for hd in range(DSA_IDX_HEADS):
            sc = sc + sh[:, hd * tq:(hd + 1) * tq] * iw[hd:hd + 1, :]
        sc = sc + 0.0
        if masked:
            sc = jnp.where(k0 + key_i <= qry_i, sc, -jnp.inf)
        bits = pltpu.bitcast(sc, jnp.int32)
        key = jnp.where(bits < 0, bits ^ jnp.int32(0x7FFFFFFF), bits)
        key_ref[pl.ds(k0, tk), :] = key
        half_ref[pl.ds(k0, tk), :] = (key >> 16).astype(jnp.int16)
        return carry

    lax.fori_loop(0, last, functools.partial(score_block, masked=False), 0)
    score_block(last, 0, True)

    one16, zero16 = jnp.ones((), jnp.int16), jnp.zeros((), jnp.int16)
    low16 = np.int16(-2 ** 15)

    def count16(limit, strict):
        def body(kb, acc):
            k0 = pl.multiple_of(kb * tk, tk)
            for c in range(tk // 128):
                blk = half_ref[pl.ds(k0 + 128 * c, 128), :].reshape(8, 16, tq)
                hit = jnp.where(blk > limit if strict else blk >= limit, one16, zero16)
                parts = [hit[j] for j in range(8)]
                while len(parts) > 1:
                    parts = [a + b for a, b in zip(parts[0::2], parts[1::2])]
                acc = acc + parts[0]
            return acc
        acc = lax.fori_loop(0, nkb, body, jnp.zeros((16, tq), jnp.int16))
        return jnp.broadcast_to(jnp.sum(acc.astype(jnp.int32), axis=0, keepdims=True), (16, tq))

    def search16(need):
        t = jnp.full((16, tq), -2 ** 15, jnp.int32)
        for bit in range(15, -1, -1):
            trial = t + 2 ** bit
            t = jnp.where(count16(trial.astype(jnp.int16), False) >= need, trial, t)
        return t

    t_hi = search16(n_sel)
    t_hi16 = t_hi.astype(jnp.int16)
    need_lo = n_sel - count16(t_hi16, True)

    def low_block(kb, carry):
        k0 = pl.multiple_of(kb * tk, tk)
        lo = ((key_ref[pl.ds(k0, tk), :] & 0xFFFF) - 2 ** 15).astype(jnp.int16).reshape(tk // 16, 16, tq)
        hi = half_ref[pl.ds(k0, tk), :].reshape(tk // 16, 16, tq)
        half_ref[pl.ds(k0, tk), :] = jnp.where(hi == t_hi16, lo, low16).reshape(tk, tq)
        return carry

    lax.fori_loop(0, nkb, low_block, 0)
    t_lo = search16(need_lo)
    thr = ((t_hi << 16) | (t_lo + 2 ** 15))[0:8, :]

    need = (need_lo - count16(t_lo.astype(jnp.int16), True))[0:8, :].astype(F32)
    ur = lax.broadcasted_iota(jnp.int32, (LANES, LANES), 0)
    uc = lax.broadcasted_iota(jnp.int32, (LANES, LANES), 1)
    earlier = (uc < ur).astype(BF16)
    ones8 = jnp.ones((8, LANES), BF16)
    key_s = lax.broadcasted_iota(jnp.int32, (LANES, tq), 0)
    qry_s = q0 + lax.broadcasted_iota(jnp.int32, (LANES, tq), 1)

    def select_block(kb, seen, masked):
        k0 = pl.multiple_of(kb * tk, tk)
        for g in range(tk // LANES):
            r0 = g * LANES
            blk = key_ref[pl.ds(k0 + r0, LANES), :].reshape(LANES // 8, 8, tq)
            eq = blk == thr[None]
            eqb = jnp.where(eq, 1.0, 0.0).reshape(LANES, tq).astype(BF16)
            rank = _dot(earlier, eqb).reshape(LANES // 8, 8, tq) + seen[None]
            tie_bias = jnp.where(rank < need[None], 0.0, NEG_BIG)
            bias = jnp.where(blk > thr[None], 0.0, jnp.where(eq, tie_bias, NEG_BIG)).reshape(LANES, tq)
            if masked:
                bias = jnp.where(k0 + r0 + key_s <= qry_s, bias, NEG_BIG)
            bias_ref[pl.ds(k0 + r0, LANES), :] = bias
            seen = seen + _dot(ones8, eqb)
        return seen

    seen = lax.fori_loop(0, last, functools.partial(select_block, masked=False), jnp.zeros((8, tq), F32))
    select_block(last, seen, True)

    qdt = qdt_ref[0]
    zq = jnp.zeros((LANES - DSA_HEAD_DIM, tq), BF16)
    qz = jnp.concatenate([jnp.concatenate([qdt[hd * DSA_HEAD_DIM:(hd + 1) * DSA_HEAD_DIM, :], zq], axis=0)
                          for hd in range(DSA_HEADS)], axis=1)
    wide = DSA_HEADS * tq

    def scores(kb, dst_ref):
        k0 = pl.multiple_of(kb * tk, tk)
        dst_ref[...] = _dot(dkv_ref[pl.ds(k0, tk), :], qz)

    def absorb(src_ref, kb, carry):
        m_i, l_i, acc = carry
        k0 = pl.multiple_of(kb * tk, tk)
        bias = bias_ref[pl.ds(k0, tk), :]
        s = src_ref[...] + jnp.concatenate([bias] * DSA_HEADS, axis=1)
        m_new = jnp.maximum(m_i, jnp.max(s, axis=0, keepdims=True))
        p = jnp.exp2(s - m_new)
        alpha = jnp.exp2(m_i - m_new)
        l_new = alpha * l_i + jnp.sum(p, axis=0, keepdims=True)
        pv = _dot(dkvt_ref[0, DSA_HEAD_DIM:, pl.ds(k0, tk)], p.astype(BF16))
        return m_new, l_new, alpha * acc + pv

    def pair(j, carry):
        kb = 2 * j
        scores(kb + 1, sb_ref)
        carry = absorb(sa_ref, kb, carry)
        scores(kb + 2, sa_ref)
        return absorb(sb_ref, kb + 1, carry)

    init = (jnp.full((1, wide), NEG_BIG, F32), jnp.zeros((1, wide), F32), jnp.zeros((DSA_HEAD_DIM, wide), F32))
    last = nkb - 1
    scores(0, sa_ref)
    carry = lax.fori_loop(0, last // 2, pair, init)

    def tail_odd(carry):
        scores(last, sb_ref)
        return absorb(sb_ref, last, absorb(sa_ref, last - 1, carry))

    def tail_even(carry):
        return absorb(sa_ref, last, carry)

    _, l_f, acc = lax.cond(last % 2 == 1, tail_odd, tail_even, carry)
    o_all = acc / l_f
    o_ref[...] = jnp.concatenate([o_all[:, hd * tq:(hd + 1) * tq] for hd in range(DSA_HEADS)],
                                 axis=0).T.astype(BF16)


def _dsa(qdt, iqt, iwt, dkv, dkvt, ikw, batch, seq):
    tq, tk = DSA_TQ, DSA_TK
    nq = seq // tq
    n_sel = min(DSA_TOPK, seq // 4)
    return pl.pallas_call(
        functools.partial(_dsa_kernel, tq=tq, tk=tk, n_sel=n_sel),
        out_shape=jax.ShapeDtypeStruct((batch * seq, 256), BF16),
        grid=(batch, nq),
        in_specs=[pl.BlockSpec((1, 256, tq), lambda b, i: (b, 0, i)),
                  pl.BlockSpec((1, 128, tq), lambda b, i: (b, 0, i)),
                  pl.BlockSpec((1, 8, tq), lambda b, i: (b, 0, i)),
                  pl.BlockSpec((seq, 128), lambda b, i: (b, 0)),
                  pl.BlockSpec((1, 128, seq), lambda b, i: (b, 0, 0)),
                  pl.BlockSpec((seq, 128), lambda b, i: (b, 0))],
        out_specs=pl.BlockSpec((tq, 256), lambda b, i: (b * nq + i, 0)),
        scratch_shapes=[pltpu.VMEM((seq, tq), jnp.int32), pltpu.VMEM((seq, tq), F32),
                        pltpu.VMEM((seq, tq), jnp.int16),
                        pltpu.VMEM((tk, DSA_HEADS * tq), F32), pltpu.VMEM((tk, DSA_HEADS * tq), F32)],
        compiler_params=_params(("parallel", "parallel")),
        name="dsa",
    )(qdt, iqt, iwt, dkv, dkvt, ikw)


def _mem_kv_kernel(mem_ref, g_ref, w_ref, kt_ref, v_ref):
    mn = _rms(mem_ref[0], g_ref[...]).astype(BF16)
    kv = _dot(mn, w_ref[...])
    kt_ref[0] = kv[:, :256].T.astype(BF16)
    v_ref[0] = kv[:, 256:].astype(BF16)


def _mem_kv(mem, gain, w_kv):
    b, m, d = mem.shape
    return pl.pallas_call(
        _mem_kv_kernel,
        out_shape=(jax.ShapeDtypeStruct((b, 256, m), BF16), jax.ShapeDtypeStruct((b, m, 256), BF16)),
        grid=(b,),
        in_specs=[pl.BlockSpec((1, m, d), lambda i: (i, 0, 0)), _const_spec((1, d)), _const_spec((d, 512))],
        out_specs=(pl.BlockSpec((1, 256, m), lambda i: (i, 0, 0)), pl.BlockSpec((1, m, 256), lambda i: (i, 0, 0))),
        compiler_params=_params(("parallel",)),
        name="mem_kv",
    )(mem, gain, w_kv)


def _merge_kernel(x_ref, ya_ref, yb_ref, yc_ref, yd_ref, gmix_ref, wg_ref, wbr_ref, wout_ref,
                  gq_ref, wq_ref, mkt_ref, mv_ref, wo_ref, gffn_ref, wr_ref, br_ref,
                  x2_ref, hp_ref, rt_ref, cnt_ref, run_ref, *, tm):
    x = x_ref[...]
    d = x.shape[-1]
    h = _rms(x, gmix_ref[...]).astype(BF16)
    merged = jnp.zeros((tm, d), F32)
    for n, y_ref in enumerate((ya_ref, yb_ref, yc_ref, yd_ref)):
        wn = wg_ref[:, n * d:n * d + d + LANES][:, GATE_SHIFT:GATE_SHIFT + d]
        gate = _sigmoid(_dot(h, wn))
        merged = merged + gate * _dot(y_ref[...], wbr_ref[n])
    x1 = x + _dot(merged.astype(BF16), wout_ref[...])
    h2 = _rms(x1, gq_ref[...]).astype(BF16)
    q = (_dot(h2, wq_ref[...]) * (MEM_HEAD_DIM ** -0.5)).astype(BF16)
    lane_head = lax.broadcasted_iota(jnp.int32, (tm, 256), 1) // MEM_HEAD_DIM
    mv = mv_ref[0]
    o = jnp.zeros((tm, 256), F32)
    for hd in range(MEM_HEADS):
        s = _dot(q[:, hd * MEM_HEAD_DIM:(hd + 1) * MEM_HEAD_DIM], mkt_ref[0, hd * MEM_HEAD_DIM:(hd + 1) * MEM_HEAD_DIM, :])
        p = jnp.exp(s - jnp.max(s, axis=-1, keepdims=True))
        p = p / jnp.sum(p, axis=-1, keepdims=True)
        o = o + jnp.where(lane_head == hd, _dot(p.astype(BF16), mv), 0.0)
    x2 = x1 + _dot(o.astype(BF16), wo_ref[...])
    x2_ref[...] = x2
    h3f = _rms(x2, gffn_ref[...])
    hp_ref[...] = _pack_bf16_pairs(h3f)
    h3 = h3f.astype(BF16)
    logits = _dot(h3, wr_ref[...]) + br_ref[...]
    lane = lax.broadcasted_iota(jnp.int32, (tm, LANES), 1)
    gl = jnp.where(lane < MOE_GROUPS, logits, -jnp.inf)
    gmax = jnp.max(gl, axis=-1, keepdims=True)
    gsel = jnp.min(jnp.where(gl == gmax, lane, LANES), axis=-1, keepdims=True)
    pg_sel = 1.0 / jnp.sum(jnp.exp(gl - gmax), axis=-1, keepdims=True)
    in_group = (lane - ROUTER_EXPERT_LANE) // MOE_EXPERTS_PER_GROUP == gsel
    el = jnp.where(in_group, logits, -jnp.inf)
    m1 = jnp.max(el, axis=-1, keepdims=True)
    i1 = jnp.min(jnp.where(el == m1, lane, LANES), axis=-1, keepdims=True)
    el2 = jnp.where(lane == i1, -jnp.inf, el)
    m2 = jnp.max(el2, axis=-1, keepdims=True)
    i2 = jnp.min(jnp.where(el2 == m2, lane, LANES), axis=-1, keepdims=True)
    e21 = jnp.exp(m2 - m1)
    c1 = pg_sel / (1.0 + e21)
    @pl.when(pl.program_id(0) == 0)
    def _():
        run_ref[...] = jnp.zeros_like(run_ref)

    oh1 = jnp.where(lane == i1, 1.0, 0.0)
    oh2 = jnp.where(lane == i2, 1.0, 0.0)
    both = oh1 + oh2
    tr = lax.broadcasted_iota(jnp.int32, (tm, tm), 0)
    tc = lax.broadcasted_iota(jnp.int32, (tm, tm), 1)
    before = _dot(jnp.where(tc < tr, 1.0, 0.0).astype(BF16), both.astype(BF16)) + run_ref[0:1, :]
    r1 = jnp.sum(oh1 * before, axis=-1, keepdims=True)
    r2 = jnp.sum(oh2 * before, axis=-1, keepdims=True)
    total = run_ref[...] + jnp.sum(both, axis=0, keepdims=True)
    run_ref[...] = total
    cnt_ref[...] = total
    ids = (jnp.where(lane == 0, i1, i2) - ROUTER_EXPERT_LANE).astype(F32)
    rt_ref[...] = jnp.where(lane < 2, ids, jnp.where(lane == 2, c1, jnp.where(lane == 3, c1 * e21,
                            jnp.where(lane == 4, r1, jnp.where(lane == 5, r2, 0.0)))))


def _merge(x, ys, gmix, wg, wbr, wout, gq, wq, mkt, mv, wo, gffn, wr, br, batch, seq):
    n, d = x.shape
    tm = MERGE_TM
    spt = seq // tm
    m = mv.shape[1]
    tok = lambda w: pl.BlockSpec((tm, w), lambda i: (i, 0))
    return pl.pallas_call(
        functools.partial(_merge_kernel, tm=tm),
        out_shape=(jax.ShapeDtypeStruct((n, d), F32), jax.ShapeDtypeStruct((n, d // 2), F32),
                   jax.ShapeDtypeStruct((n, LANES), F32), jax.ShapeDtypeStruct((8, LANES), F32)),
        grid=(n // tm,),
        in_specs=[tok(d), tok(256), tok(256), tok(256), tok(256),
                  _const_spec((1, d)), _const_spec((d, N_BRANCH * d + LANES)), _const_spec((N_BRANCH, 256, d)),
                  _const_spec((d, d)), _const_spec((1, d)), _const_spec((d, 256)),
                  pl.BlockSpec((1, 256, m), lambda i: (i // spt, 0, 0)),
                  pl.BlockSpec((1, m, 256), lambda i: (i // spt, 0, 0)),
                  _const_spec((256, d)), _const_spec((1, d)), _const_spec((d, LANES)), _const_spec((1, LANES))],
        out_specs=(tok(d), tok(d // 2), tok(LANES), pl.BlockSpec((8, LANES), lambda i: (0, 0))),
        scratch_shapes=[pltpu.VMEM((8, LANES), F32)],
        compiler_params=_params(("arbitrary",)),
        name="merge_mem_router",
    )(x, *ys, gmix, wg, wbr, wout, gq, wq, mkt, mv, wo, gffn, wr, br)


def _sc_gather_rows(table, idx):
    _, width = table.shape
    total = idx.shape[0]
    chunk, nbuf = SC_GATHER_CHUNK, SC_GATHER_BUFS
    workers = SC_CORES * SC_SUBCORES
    per_w = total // workers
    nch = per_w // chunk
    assert total % (workers * chunk * nbuf) == 0
    mesh = plsc.VectorSubcoreMesh(core_axis_name="c", subcore_axis_name="s")

    @functools.partial(
        pl.kernel, mesh=mesh, out_type=jax.ShapeDtypeStruct((total, width), table.dtype),
        scratch_types=[pltpu.VMEM((nch, chunk), jnp.int32), pltpu.VMEM((nbuf, chunk, width), table.dtype),
                       pltpu.SemaphoreType.DMA((nbuf,)), pltpu.SemaphoreType.DMA((nbuf,))])
    def gather_kernel(table_hbm, idx_hbm, out_hbm, idx_v, rows_v, gsem, wsem):
        wid = lax.axis_index("s") * SC_CORES + lax.axis_index("c")
        pltpu.sync_copy(idx_hbm.at[wid], idx_v)

        def gather(j, slot):
            return pltpu.make_async_copy(table_hbm.at[idx_v.at[j]], rows_v.at[slot], gsem.at[slot])

        def write(j, slot):
            off = pl.multiple_of(wid * per_w + j * chunk, chunk)
            return pltpu.make_async_copy(rows_v.at[slot], out_hbm.at[pl.ds(off, chunk)], wsem.at[slot])

        for slot in range(nbuf):
            gather(slot, slot).start()

        @pl.loop(0, nch // nbuf)
        def _(g):
            for slot in range(nbuf):
                j = g * nbuf + slot
                gather(j, slot).wait()
                write(j, slot).start()
                write(j, slot).wait()

                @pl.when(j + nbuf < nch)
                def _():
                    gather(j + nbuf, slot).start()

    return gather_kernel(table, idx.reshape(workers, nch, chunk))


def _sc_scatter_rows(table, dest2, total):
    n, width = table.shape
    chunk, nbuf = SC_GATHER_CHUNK, SC_GATHER_BUFS
    workers = SC_CORES * SC_SUBCORES
    per_w = n // workers
    nch = per_w // chunk
    assert n % (workers * chunk * nbuf) == 0
    mesh = plsc.VectorSubcoreMesh(core_axis_name="c", subcore_axis_name="s")

    @functools.partial(
        pl.kernel, mesh=mesh, out_type=jax.ShapeDtypeStruct((total, width), table.dtype),
        scratch_types=[pltpu.VMEM((2, nch, chunk), jnp.int32), pltpu.VMEM((nbuf, chunk, width), table.dtype),
                       pltpu.SemaphoreType.DMA((nbuf,)), pltpu.SemaphoreType.DMA((nbuf,))])
    def scatter_kernel(table_hbm, idx_hbm, out_hbm, idx_v, rows_v, rsem, wsem):
        wid = lax.axis_index("s") * SC_CORES + lax.axis_index("c")
        pltpu.sync_copy(idx_hbm.at[wid], idx_v)

        def read(j, slot):
            off = pl.multiple_of(wid * per_w + j * chunk, chunk)
            return pltpu.make_async_copy(table_hbm.at[pl.ds(off, chunk)], rows_v.at[slot], rsem.at[slot])

        def write(j, slot, k):
            return pltpu.make_async_copy(rows_v.at[slot], out_hbm.at[idx_v.at[k, j]], wsem.at[slot])

        for slot in range(nbuf):
            read(slot, slot).start()

        @pl.loop(0, nch // nbuf)
        def _(g):
            for slot in range(nbuf):
                j = g * nbuf + slot
                read(j, slot).wait()
                write(j, slot, 0).start()
                write(j, slot, 1).start()
                write(j, slot, 0).wait()
                write(j, slot, 1).wait()

                @pl.when(j + nbuf < nch)
                def _():
                    read(j + nbuf, slot).start()

    idx = dest2.reshape(2, workers, nch, chunk).transpose(1, 0, 2, 3)
    return scatter_kernel(table, idx)


def _dispatch_plan(rt, cnt, n):
    ne, blk = MOE_N_EXPERTS, MOE_BLOCK
    n_blocks = (2 * n) // blk + ne
    experts = jnp.arange(ne, dtype=jnp.int32)
    counts = cnt[0, ROUTER_EXPERT_LANE:ROUTER_EXPERT_LANE + ne].astype(jnp.int32)
    padded = (counts + blk - 1) // blk * blk
    pend = jnp.cumsum(padded)
    pstart = pend - padded
    ids = rt[:, 0:2].astype(jnp.int32)
    pos = rt[:, 4:6].astype(jnp.int32)
    first_row = jnp.sum(jnp.where(ids[:, :, None] == experts[None, None, :], pstart[None, None, :], 0), axis=-1)
    dest2 = (first_row + pos).T
    b0 = jnp.arange(n_blocks, dtype=jnp.int32) * blk
    block_e = jnp.minimum(jnp.sum((pend[None, :] <= b0[:, None]).astype(jnp.int32), axis=1), ne - 1)
    n_valid = jnp.clip(counts[block_e] - (b0 - pstart[block_e]), 0, blk).astype(jnp.int32)
    return dest2, block_e, n_valid


def _expert_block_kernel(be_ref, nv_ref, xs_ref, wg_ref, wu_ref, wd_ref, o_ref, wgb_ref, wub_ref, wdb_ref):
    b = pl.program_id(0)
    valid = nv_ref[b]

    @pl.when((b == 0) | (be_ref[b] != be_ref[jnp.maximum(b - 1, 0)]))
    def _():
        wgb_ref[...] = wg_ref[0].astype(BF16)
        wub_ref[...] = wu_ref[0].astype(BF16)
        wdb_ref[...] = wd_ref[0].astype(BF16)

    @pl.when(valid > 0)
    def _():
        row = lax.broadcasted_iota(jnp.int32, xs_ref.shape, 0)
        words = jnp.where(row < valid, xs_ref[...], 0.0)
        h = _unpack_bf16_pairs(words).astype(BF16)
        gt = _dot(h, wgb_ref[...])
        hid = gt * _sigmoid(gt) * _dot(h, wub_ref[...])
        o_ref[...] = _pack_bf16_pairs(_dot(hid.astype(BF16), wdb_ref[...]))

    @pl.when(valid == 0)
    def _():
        o_ref[...] = jnp.zeros_like(o_ref)


def _expert_blocks(xs, wg, wu, wd, layer, block_e, n_used):
    p_rows, half = xs.shape
    d, hid = wg.shape[-2:]
    blk = MOE_BLOCK
    grid_spec = pltpu.PrefetchScalarGridSpec(
        num_scalar_prefetch=2, grid=(p_rows // blk,),
        in_specs=[pl.BlockSpec((blk, half), lambda b, be, nu: (b, 0)),
                  pl.BlockSpec((None, 1, d, hid), lambda b, be, nu: (layer, be[b], 0, 0)),
                  pl.BlockSpec((None, 1, d, hid), lambda b, be, nu: (layer, be[b], 0, 0)),
                  pl.BlockSpec((None, 1, hid, d), lambda b, be, nu: (layer, be[b], 0, 0))],
        out_specs=pl.BlockSpec((blk, half), lambda b, be, nu: (b, 0)),
        scratch_shapes=[pltpu.VMEM((d, hid), BF16), pltpu.VMEM((d, hid), BF16), pltpu.VMEM((hid, d), BF16)])
    return pl.pallas_call(
        _expert_block_kernel, out_shape=jax.ShapeDtypeStruct((p_rows, half), F32), grid_spec=grid_spec,
        compiler_params=_params(("arbitrary",)),
        name="moe_expert_blocks",
    )(block_e, n_used, xs, wg, wu, wd)


def _combine_kernel(x_ref, y1_ref, y2_ref, rt_ref, gfin_ref, o_ref, *, final_norm):
    out = (x_ref[...] + rt_ref[:, 2:3] * _unpack_bf16_pairs(y1_ref[...])
           + rt_ref[:, 3:4] * _unpack_bf16_pairs(y2_ref[...]))
    o_ref[...] = _rms(out, gfin_ref[...]) if final_norm else out


def _combine(x2, y_halves, rt, gfin, final_norm):
    n, d = x2.shape
    tm = COMBINE_TM
    nt = n // tm
    return pl.pallas_call(
        functools.partial(_combine_kernel, final_norm=final_norm),
        out_shape=jax.ShapeDtypeStruct((n, d), F32),
        grid=(nt,),
        in_specs=[pl.BlockSpec((tm, d), lambda i: (i, 0)),
                  pl.BlockSpec((tm, d // 2), lambda i: (i, 0)),
                  pl.BlockSpec((tm, d // 2), lambda i: (i + nt, 0)),
                  pl.BlockSpec((tm, LANES), lambda i: (i, 0)), _const_spec((1, d))],
        out_specs=pl.BlockSpec((tm, d), lambda i: (i, 0)),
        compiler_params=_params(("parallel",)), name="moe_combine",
    )(x2, y_halves, y_halves, rt, gfin)


def _moe(x2, hp, rt, cnt, wg, wu, wd, layer, gfin, final_norm):
    n = x2.shape[0]
    dest2, block_e, n_valid = _dispatch_plan(rt, cnt, n)
    xs = _sc_scatter_rows(hp, dest2, block_e.shape[0] * MOE_BLOCK)
    yb = _expert_blocks(xs, wg, wu, wd, layer, block_e, n_valid)
    y_halves = _sc_gather_rows(yb, dest2.reshape(2 * n))
    return _combine(x2, y_halves, rt, gfin, final_norm)


def kernel(x, mem, positions, norm_mix, w_in, diff_lambda, hgrn_lb_logits, spatial_w, spatial_b, w_branch, w_out,
           norm_mem_q, norm_mem_kv, w_mem_q, w_mem_kv, w_mem_o, norm_ffn, w_router_group, b_router_group,
           w_router_expert, b_router_expert, w_exp_gate, w_exp_up, w_exp_down, norm_final):
    batch, seq, d = x.shape
    depth = w_in.shape[0]
    n = batch * seq
    xf = x.reshape(n, d)
    tabs = _rope_tables(positions)
    w_all = _w_in_bf16(w_in)
    row = lambda v: v.reshape(1, -1).astype(F32)
    for l in range(depth):
        lam_init = 0.8 - 0.6 * math.exp(-0.3 * l)
        g0 = C_GATE - GATE_SHIFT
        w_gate = lax.slice(w_all, (l, 0, g0), (l + 1, d, g0 + N_BRANCH * d + LANES)).reshape(d, -1)
        sw = spatial_w[l].reshape(SGU_GROUPS * SGU_CHUNK, SGU_CHUNK)
        sb = jnp.repeat(spatial_b[l].T, SGU_GROUP_DIM, axis=1)
        qat, ka, vat, hb, y_c, qdt, iqt, dkv, dkvt, ikw, iwt = _projection(
            xf, row(norm_mix[l]), w_all, l, tabs, sw, sb, batch, seq)
        y_a = _diff_attention(diff_lambda[l], qat, ka, vat, lam_init, batch, seq)
        y_b = _hgrn(hgrn_lb_logits, hb, l, batch, seq)
        y_d = _dsa(qdt, iqt, iwt, dkv, dkvt, ikw, batch, seq)
        mkt, mv = _mem_kv(mem, row(norm_mem_kv[l]), w_mem_kv[l].astype(BF16))
        e0, e1 = ROUTER_EXPERT_LANE, ROUTER_EXPERT_LANE + MOE_N_EXPERTS
        wr = jnp.zeros((d, LANES), F32)
        wr = wr.at[:, :MOE_GROUPS].set(w_router_group[l]).at[:, e0:e1].set(w_router_expert[l]).astype(BF16)
        br = jnp.zeros((1, LANES), F32)
        br = br.at[0, :MOE_GROUPS].set(b_router_group[l]).at[0, e0:e1].set(b_router_expert[l])
        x2, hp, rt, cnt = _merge(xf, (y_a, y_b, y_c, y_d), row(norm_mix[l]), w_gate, w_branch[l].astype(BF16),
                                 w_out[l].astype(BF16), row(norm_mem_q[l]), w_mem_q[l].astype(BF16), mkt, mv,
                                 w_mem_o[l].astype(BF16), row(norm_ffn[l]), wr, br, batch, seq)
        xf = _moe(x2, hp, rt, cnt, w_exp_gate, w_exp_up, w_exp_down, l,
                  row(norm_final), final_norm=(l == depth - 1))
    return xf.reshape(batch, seq, d)
```

```python
import functools
import math

import numpy as np
import jax
import jax.numpy as jnp
from jax import lax
from jax.experimental import pallas as pl
from jax.experimental.pallas import tpu as pltpu
from jax.experimental.pallas import tpu_sc as plsc

F32 = jnp.float32
BF16 = jnp.bfloat16

NORM_EPS = 1e-6
ROPE_THETA = 10000.0
NEG_BIG = -1e30

N_BRANCH = 4
DIFF_HEADS = 4
DIFF_HEAD_DIM = 32
HGRN_DIM = 64
HGRN_CHUNK = 32
HGRN_UNROLL = 8
HGRN_MIN_FORGET = 1e-30
SGU_GROUPS = 4
SGU_GROUP_DIM = 64
SGU_CHUNK = 128
DSA_HEADS = 4
DSA_HEAD_DIM = 64
DSA_IDX_HEADS = 4
DSA_IDX_DIM = 32
DSA_TOPK = 256
MEM_HEADS = 4
MEM_HEAD_DIM = 64
MOE_GROUPS = 4
MOE_EXPERTS_PER_GROUP = 8
MOE_N_EXPERTS = 32
MOE_BLOCK = 512
ROUTER_EXPERT_LANE = 32
SC_CORES = 2
SC_SUBCORES = 16
SC_GATHER_CHUNK = 16
SC_GATHER_BUFS = 4

LANES = 128
VMEM_LIMIT = 56 * 1024 * 1024

PROJ_TM = 1024
DIFF_TQ = 512
DIFF_TK = 512
HGRN_TC = 1024
DSA_TQ = 512
DSA_TK = 512
MERGE_TM = 1024
COMBINE_TM = 1024
ROPE_TM = 1024

C_AQ, C_AK, C_AV = 0, 256, 512
C_HB = 768
C_UV = 1792
C_DQ = 2304
C_DKV = 2560
C_IQ = 2688
C_IKW = 2816
IW_LANE = 32
C_GATE = 2852
GATE_SHIFT = C_GATE % 128
C_TOTAL = 2944
LOG2E = math.log2(math.e)


def _params(sem):
    return pltpu.CompilerParams(dimension_semantics=sem, vmem_limit_bytes=VMEM_LIMIT)


def _const_spec(shape):
    nd = len(shape)
    return pl.BlockSpec(shape, lambda *_: (0,) * nd, pipeline_mode=pl.Buffered(1))


def _rms(xf, gain=None):
    y = xf * lax.rsqrt(jnp.mean(xf * xf, axis=-1, keepdims=True) + NORM_EPS)
    return y if gain is None else y * gain


def _sigmoid(x):
    return 0.5 * jnp.tanh(0.5 * x) + 0.5


def _pack_bf16_pairs(x):
    w = x.shape[-1] // 2
    xb = x.astype(BF16).astype(F32)
    lo = lax.shift_right_logical(pltpu.bitcast(xb[:, :w], jnp.int32), 16)
    hi = pltpu.bitcast(xb[:, w:], jnp.int32) & jnp.int32(-65536)
    return pltpu.bitcast(hi | lo, F32)


def _unpack_bf16_pairs(words):
    bits = pltpu.bitcast(words, jnp.int32)
    lo = pltpu.bitcast(bits << 16, F32)
    hi = pltpu.bitcast(bits & jnp.int32(-65536), F32)
    return jnp.concatenate([lo, hi], axis=1)


def _dot(a, b):
    return jnp.dot(a, b, preferred_element_type=F32)


def _dot_nt(a, b):
    return lax.dot_general(a, b, (((1,), (1,)), ((), ())), preferred_element_type=F32)


def _rope_table_kernel(pos_ref, frq_ref, sgn_ref, c32_ref, s32_ref, c64_ref, s64_ref):
    pos = pos_ref[...].astype(F32)
    twice = lambda t: jnp.concatenate([t, t], axis=1)
    a32 = pos * frq_ref[0:1, :LANES]
    a64 = pos * frq_ref[1:2, :LANES]
    c32_ref[...] = twice(jnp.cos(a32))
    s32_ref[...] = twice(jnp.sin(a32) * sgn_ref[0:1, :LANES])
    c64_ref[...] = twice(jnp.cos(a64))
    s64_ref[...] = twice(jnp.sin(a64) * sgn_ref[1:2, :LANES])


def _rope_tables(positions):
    n = positions.size
    pos = positions.reshape(n, 1).astype(jnp.int32)
    lane = np.arange(256)
    inv32 = ROPE_THETA ** (-jnp.arange(16, dtype=F32) * (2.0 / 32))
    inv64 = ROPE_THETA ** (-jnp.arange(32, dtype=F32) * (2.0 / 64))
    frq = jnp.stack([inv32[lane % 16], inv64[lane % 32]])
    sgn = jnp.asarray(np.stack([np.where(lane % 32 < 16, -1.0, 1.0),
                                np.where(lane % 64 < 32, -1.0, 1.0)]), F32)
    tm = ROPE_TM
    tab = jax.ShapeDtypeStruct((n, 256), F32)
    return pl.pallas_call(
        _rope_table_kernel,
        out_shape=(tab, tab, tab, tab),
        grid=(n // tm,),
        in_specs=[pl.BlockSpec((tm, 1), lambda i: (i, 0)), _const_spec((2, 256)), _const_spec((2, 256))],
        out_specs=tuple(pl.BlockSpec((tm, 256), lambda i: (i, 0)) for _ in range(4)),
        compiler_params=_params(("parallel",)),
        name="rope_tables",
    )(pos, frq, sgn)


def _w_in_kernel(wt_ref, o_ref, *, width):
    row = lax.broadcasted_iota(jnp.int32, (LANES, wt_ref.shape[-1]), 0)
    valid = row < width - pl.program_id(0) * LANES
    for layer in range(o_ref.shape[0]):
        o_ref[layer] = jnp.where(valid, wt_ref[:, layer, :], 0.0).T.astype(BF16)


def _w_in_bf16(w_in):
    depth, d, width = w_in.shape
    nblk = pl.cdiv(width, LANES)
    return pl.pallas_call(
        functools.partial(_w_in_kernel, width=width), out_shape=jax.ShapeDtypeStruct((depth, d, nblk * LANES), BF16),
        grid=(nblk,), in_specs=[pl.BlockSpec((LANES, depth, d), lambda i: (i, 0, 0))],
        out_specs=pl.BlockSpec((depth, d, LANES), lambda i: (0, 0, i)),
        compiler_params=_params(("parallel",)), name="w_in_bf16",
    )(jnp.transpose(w_in, (2, 0, 1)))


def _gelu_tanh(x):
    return 0.5 * x * (1.0 + jnp.tanh(math.sqrt(2.0 / math.pi) * (x + 0.044715 * (x * x * x))))


def _rope(x, cos, sin_signed, half):
    w = x.shape[-1]
    lane = lax.broadcasted_iota(jnp.int32, x.shape, 1)
    partner = jnp.where(lane % (2 * half) < half, pltpu.roll(x, w - half, 1), pltpu.roll(x, half, 1))
    return x * cos + partner * sin_signed


def _proj_kernel(x_ref, g_ref, w_ref, c32_ref, s32_ref, c64_ref, s64_ref, sw_ref, sb_ref,
                 qat_ref, ka_ref, vat_ref, hb_ref, yc_ref, qdt_ref, iqt_ref, dkv_ref, dkvt_ref, ikw_ref, iwt_ref,
                 *, tm):
    h = _rms(x_ref[...], g_ref[...]).astype(BF16)

    def proj(c0, width):
        return _dot(h, w_ref[:, c0:c0 + width])

    c32, s32, c64, s64 = c32_ref[...], s32_ref[...], c64_ref[...], s64_ref[...]
    qat_ref[0] = (_rope(proj(C_AQ, 256), c32, s32, 16) * (DIFF_HEAD_DIM ** -0.5 * LOG2E)).T.astype(BF16)
    ka_ref[...] = _rope(proj(C_AK, 256), c32, s32, 16).astype(BF16)
    vat_ref[0] = proj(C_AV, 256).astype(BF16).T
    hb_ref[...] = proj(C_HB, 1024)
    qdt_ref[0] = (_rope(proj(C_DQ, 256), c64, s64, 32) * (DSA_HEAD_DIM ** -0.5 * LOG2E)).T.astype(BF16)
    iqt_ref[0] = _rope(proj(C_IQ, 128), c32[:, :128], s32[:, :128], 16).T.astype(BF16)
    lane = lax.broadcasted_iota(jnp.int32, (tm, 128), 1)
    is_k = lane < DSA_HEAD_DIM
    dkv = _rope(proj(C_DKV, 128), jnp.where(is_k, c64[:, :128], 1.0), jnp.where(is_k, s64[:, :128], 0.0), 32)
    is_ik = lane < DSA_IDX_DIM
    ikw = _rope(proj(C_IKW, 128), jnp.where(is_ik, c32[:, :128], 1.0), jnp.where(is_ik, s32[:, :128], 0.0), 16)
    dkv_ref[...] = dkv.astype(BF16)
    dkvt_ref[0] = dkv.T.astype(BF16)
    ikw_ref[...] = ikw.astype(BF16)
    iw_scale = DSA_IDX_HEADS ** -0.5 * DSA_IDX_DIM ** -0.5
    iwt_ref[0] = (ikw * iw_scale).T[IW_LANE:IW_LANE + 8, :]
    uv = _gelu_tanh(proj(C_UV, 512))
    u, v = uv[:, :256], uv[:, 256:]
    mu = jnp.mean(v, axis=-1, keepdims=True)
    vc = v - mu
    vn = (vc * lax.rsqrt(jnp.mean(vc * vc, axis=-1, keepdims=True) + NORM_EPS)).astype(BF16)
    r = lax.broadcasted_iota(jnp.int32, (SGU_GROUPS * SGU_CHUNK, SGU_CHUNK), 0)
    c = lax.broadcasted_iota(jnp.int32, (SGU_GROUPS * SGU_CHUNK, SGU_CHUNK), 1)
    wt = jnp.where((r % SGU_CHUNK) >= c, sw_ref[...], 0.0).astype(BF16)
    lane_grp = lax.broadcasted_iota(jnp.int32, (SGU_CHUNK, 256), 1) // SGU_GROUP_DIM
    for ch in range(tm // SGU_CHUNK):
        r0 = ch * SGU_CHUNK
        full = _dot(wt, vn[r0:r0 + SGU_CHUNK, :])
        mixed = sb_ref[...]
        for g in range(SGU_GROUPS):
            mixed = mixed + jnp.where(lane_grp == g, full[g * SGU_CHUNK:(g + 1) * SGU_CHUNK, :], 0.0)
        yc_ref[r0:r0 + SGU_CHUNK, :] = (u[r0:r0 + SGU_CHUNK, :] * mixed).astype(BF16)


def _projection(x, gain, w_all, layer, tabs, sw, sb, batch, seq):
    n, d = x.shape
    tm = PROJ_TM
    spt = seq // tm
    tok = lambda w: pl.BlockSpec((tm, w), lambda i: (i, 0))
    tr = lambda rows: pl.BlockSpec((1, rows, tm), lambda i: (i // spt, 0, i % spt))
    out_shape = (
        jax.ShapeDtypeStruct((batch, 256, seq), BF16),
        jax.ShapeDtypeStruct((n, 256), BF16),
        jax.ShapeDtypeStruct((batch, 256, seq), BF16),
        jax.ShapeDtypeStruct((n, 1024), F32),
        jax.ShapeDtypeStruct((n, 256), BF16),
        jax.ShapeDtypeStruct((batch, 256, seq), BF16),
        jax.ShapeDtypeStruct((batch, 128, seq), BF16),
        jax.ShapeDtypeStruct((n, 128), BF16),
        jax.ShapeDtypeStruct((batch, 128, seq), BF16),
        jax.ShapeDtypeStruct((n, 128), BF16),
        jax.ShapeDtypeStruct((batch, 8, seq), F32),
    )
    return pl.pallas_call(
        functools.partial(_proj_kernel, tm=tm),
        out_shape=out_shape,
        grid=(n // tm,),
        in_specs=[tok(d), _const_spec((1, d)),
                  pl.BlockSpec((None, d, C_TOTAL), lambda i: (layer, 0, 0), pipeline_mode=pl.Buffered(1)),
                  tok(256), tok(256), tok(256), tok(256),
                  _const_spec((SGU_GROUPS * SGU_CHUNK, SGU_CHUNK)), _const_spec((SGU_CHUNK, 256))],
        out_specs=(tr(256), tok(256), tr(256), tok(1024), tok(256), tr(256), tr(128), tok(128), tr(128), tok(128), tr(8)),
        compiler_params=_params(("parallel",)),
        name="projection",
    )(x, gain, w_all, *tabs, sw, sb)


def _diff_attn_kernel(lam_ref, qt_ref, k_ref, vt_ref, o_ref, sa_ref, sb_ref, *, lam_init, tq, tk):
    q0 = pl.program_id(1) * tq
    kb_diag = q0 // tk
    lv = lam_ref[...]
    lam = (jnp.exp(jnp.sum(lv[0:1] * lv[1:2], axis=-1, keepdims=True))
           - jnp.exp(jnp.sum(lv[2:3] * lv[3:4], axis=-1, keepdims=True)) + lam_init)
    qt = qt_ref[0]
    feat = lax.broadcasted_iota(jnp.int32, (256, tq), 0) // DIFF_HEAD_DIM
    n_maps = 2 * DIFF_HEADS
    qz = jnp.concatenate([jnp.where(feat == i, qt, jnp.zeros_like(qt)) for i in range(n_maps)], axis=1)
    wide = n_maps * tq
    key_i = lax.broadcasted_iota(jnp.int32, (tk, wide), 0)
    qry_i = q0 + lax.broadcasted_iota(jnp.int32, (tk, wide), 1) % tq

    def scores(kb, dst_ref):
        k0 = pl.multiple_of(kb * tk, tk)
        dst_ref[...] = _dot(k_ref[pl.ds(k0, tk), :], qz)

    def absorb(src_ref, kb, carry, masked):
        m_i, l_i, acc = carry
        k0 = pl.multiple_of(kb * tk, tk)
        s = src_ref[...]
        if masked:
            s = jnp.where(k0 + key_i <= qry_i, s, NEG_BIG)
        m_new = jnp.maximum(m_i, jnp.max(s, axis=0, keepdims=True))
        p = jnp.exp2(s - m_new)
        alpha = jnp.exp2(m_i - m_new)
        l_new = alpha * l_i + jnp.sum(p, axis=0, keepdims=True)
        pb = p.astype(BF16)
        pv = jnp.concatenate(
            [_dot(vt_ref[0, hd * 64:(hd + 1) * 64, pl.ds(k0, tk)], pb[:, 2 * hd * tq:(2 * hd + 2) * tq])
             for hd in range(DIFF_HEADS)], axis=1)
        return m_new, l_new, alpha * acc + pv

    def pair(j, carry):
        kb = 2 * j
        scores(kb + 1, sb_ref)
        carry = absorb(sa_ref, kb, carry, False)
        scores(kb + 2, sa_ref)
        return absorb(sb_ref, kb + 1, carry, False)

    init = (jnp.full((1, wide), NEG_BIG, F32), jnp.zeros((1, wide), F32), jnp.zeros((64, wide), F32))
    scores(0, sa_ref)
    carry = lax.fori_loop(0, kb_diag // 2, pair, init)

    def tail_odd(carry):
        scores(kb_diag, sb_ref)
        carry = absorb(sa_ref, kb_diag - 1, carry, False)
        return absorb(sb_ref, kb_diag, carry, True)

    def tail_even(carry):
        return absorb(sa_ref, kb_diag, carry, True)

    _, l_f, acc = lax.cond(kb_diag % 2 == 1, tail_odd, tail_even, carry)
    o_all = acc / l_f
    heads = []
    for hd in range(DIFF_HEADS):
        o0 = o_all[:, 2 * hd * tq:(2 * hd + 1) * tq]
        o1 = o_all[:, (2 * hd + 1) * tq:(2 * hd + 2) * tq]
        o_h = o0 - lam * o1
        ms = jnp.mean(o_h * o_h, axis=0, keepdims=True)
        heads.append(o_h * lax.rsqrt(ms + NORM_EPS) * (1.0 - lam_init))
    o_ref[...] = jnp.concatenate(heads, axis=0).T.astype(BF16)


def _diff_attention(lam_vec, qat, ka, vat, lam_init, batch, seq):
    tq, tk = DIFF_TQ, DIFF_TK
    nq = seq // tq
    return pl.pallas_call(
        functools.partial(_diff_attn_kernel, lam_init=lam_init, tq=tq, tk=tk),
        out_shape=jax.ShapeDtypeStruct((batch * seq, 256), BF16),
        grid=(batch, nq),
        in_specs=[_const_spec((4, DIFF_HEAD_DIM)),
                  pl.BlockSpec((1, 256, tq), lambda b, i: (b, 0, i)),
                  pl.BlockSpec((seq, 256), lambda b, i: (b, 0)),
                  pl.BlockSpec((1, 256, seq), lambda b, i: (b, 0, 0))],
        out_specs=pl.BlockSpec((tq, 256), lambda b, i: (b * nq + i, 0)),
        scratch_shapes=[pltpu.VMEM((tk, 2 * DIFF_HEADS * tq), F32), pltpu.VMEM((tk, 2 * DIFF_HEADS * tq), F32)],
        compiler_params=_params(("parallel", "parallel")),
        name="diff_attention",
    )(lam_vec, qat, ka, vat)


def _hgrn_kernel(lbl_ref, hb_ref, o_ref, st_ref, pstk_ref, *, layer, tc):
    cz = HGRN_CHUNK
    w = 256

    @pl.when(pl.program_id(1) == 0)
    def _():
        st_ref[...] = jnp.zeros_like(st_ref)

    lg = lbl_ref[...]
    e = jnp.exp(lg - jnp.max(lg, axis=0, keepdims=True))
    lw = e / jnp.sum(e, axis=0, keepdims=True)
    lb = jnp.sum(lw[0:layer + 1], axis=0, keepdims=True) - lw[0:1]

    ri = lax.broadcasted_iota(jnp.int32, (cz, cz), 0)
    ci = lax.broadcasted_iota(jnp.int32, (cz, cz), 1)
    tri = (ri >= ci).astype(F32)
    rb = lax.broadcasted_iota(jnp.int32, (w, w), 0) // HGRN_DIM
    cb = lax.broadcasted_iota(jnp.int32, (w, w), 1) // HGRN_DIM
    same_head = rb == cb
    head_ones = same_head.astype(BF16)
    trows = {r: r + lax.broadcasted_iota(jnp.int32, (cz - r, w), 0) for r in range(0, cz, 16)}

    def chunk(c, carry):
        r0 = pl.multiple_of(c * cz, cz)
        q = hb_ref[pl.ds(r0, cz), 0:256]
        fp = hb_ref[pl.ds(r0, cz), 256:512]
        v = hb_ref[pl.ds(r0, cz), 512:768]
        g = hb_ref[pl.ds(r0, cz), 768:1024]
        qf = q * _sigmoid(q)
        f = lb + (1.0 - lb) * jax.nn.sigmoid(fp)
        log_f = jnp.log(jnp.maximum(f, HGRN_MIN_FORGET))
        kf = (1.0 - lb) * jax.nn.sigmoid(-fp)
        bc = jnp.dot(tri, log_f, preferred_element_type=F32, precision=lax.Precision.HIGHEST)
        st = st_ref[...]
        o = _dot_nt((qf * jnp.exp(bc)).astype(BF16), st.astype(BF16))
        for s in range(cz):
            r_lo = (s // 16) * 16
            arg = bc[r_lo:, :] - bc[s:s + 1, :]
            if s > r_lo:
                arg = jnp.where(trows[r_lo] >= s, arg, NEG_BIG)
            p = qf[r_lo:, :] * kf[s:s + 1, :] * jnp.exp(arg)
            if r_lo:
                pstk_ref[s * cz:s * cz + r_lo, :] = jnp.zeros((r_lo, w), BF16)
            pstk_ref[s * cz + r_lo:(s + 1) * cz, :] = p.astype(BF16)
        accs = [jnp.zeros((16, w), F32) for _ in range(cz // 16)]
        for sg in range(cz // 16):
            att = _dot(pstk_ref[sg * 16 * cz:(sg + 1) * 16 * cz, :], head_ones)
            for sl in range(16):
                s = sg * 16 + sl
                for j in range(sg, cz // 16):
                    accs[j] = accs[j] + att[sl * cz + 16 * j:sl * cz + 16 * j + 16, :] * v[s:s + 1, :]
        o = o + jnp.concatenate(accs, axis=0)
        b_end = bc[cz - 1:cz, :]
        kd = kf * jnp.exp(b_end - bc)
        upd = _dot(v.T.astype(BF16), kd.astype(BF16))
        st_ref[...] = st * jnp.exp(b_end) + jnp.where(same_head, upd, 0.0)
        ms = _dot(o * o, head_ones.astype(F32)) * (1.0 / HGRN_DIM)
        y = o * lax.rsqrt(ms + NORM_EPS)
        o_ref[pl.ds(r0, cz), :] = (y * (g * _sigmoid(g))).astype(BF16)
        return carry

    def group(gi, carry):
        for u in range(HGRN_UNROLL):
            chunk(gi * HGRN_UNROLL + u, carry)
        return carry

    lax.fori_loop(0, tc // cz // HGRN_UNROLL, group, 0)


def _hgrn(lb_logits, hb, layer, batch, seq):
    tc = HGRN_TC
    nt = seq // tc
    cz = HGRN_CHUNK
    return pl.pallas_call(
        functools.partial(_hgrn_kernel, layer=layer, tc=tc),
        out_shape=jax.ShapeDtypeStruct((batch * seq, 256), BF16),
        grid=(batch, nt),
        in_specs=[_const_spec(lb_logits.shape),
                  pl.BlockSpec((tc, 1024), lambda b, i: (b * nt + i, 0))],
        out_specs=pl.BlockSpec((tc, 256), lambda b, i: (b * nt + i, 0)),
        scratch_shapes=[pltpu.VMEM((256, 256), F32), pltpu.VMEM((cz * cz, 256), BF16)],
        compiler_params=_params(("parallel", "arbitrary")),
        name="hgrn2",
    )(lb_logits, hb)


def _dsa_kernel(qdt_ref, iqt_ref, iwt_ref, dkv_ref, dkvt_ref, ikw_ref, o_ref, key_ref, bias_ref, half_ref,
                sa_ref, sb_ref, *, tq, tk, n_sel):
    q0 = pl.program_id(1) * tq
    nkb = q0 // tk + 1
    key_i = lax.broadcasted_iota(jnp.int32, (tk, tq), 0)
    qry_i = q0 + lax.broadcasted_iota(jnp.int32, (tk, tq), 1)
    grp = tk // 8
    rows8 = lambda x: x.reshape(grp, 8, tq)
    iqt = iqt_ref[0]
    zpad = jnp.zeros((LANES - DSA_IDX_DIM, tq), BF16)
    iqz = jnp.concatenate([jnp.concatenate([iqt[hd * DSA_IDX_DIM:(hd + 1) * DSA_IDX_DIM, :], zpad], axis=0)
                           for hd in range(DSA_IDX_HEADS)], axis=1)
    iw = iwt_ref[0]

    last = nkb - 1

    def score_block(kb, carry, masked):
        k0 = pl.multiple_of(kb * tk, tk)
        sh = jnp.maximum(_dot(ikw_ref[pl.ds(k0, tk), :], iqz), 0.0)
        sc = jnp.zeros((tk, tq), F32)
        for hd in range(DSA_IDX_HEADS):
            sc = sc + sh[:, hd * tq:(hd + 1) * tq] * iw[hd:hd + 1, :]
        sc = sc + 0.0
        if masked:
            sc = jnp.where(k0 + key_i <= qry_i, sc, -jnp.inf)
        bits = pltpu.bitcast(sc, jnp.int32)
        key = jnp.where(bits < 0, bits ^ jnp.int32(0x7FFFFFFF), bits)
        key_ref[pl.ds(k0, tk), :] = key
        half_ref[pl.ds(k0, tk), :] = (key >> 16).astype(jnp.int16)
        return carry

    lax.fori_loop(0, last, functools.partial(score_block, masked=False), 0)
    score_block(last, 0, True)

    one16, zero16 = jnp.ones((), jnp.int16), jnp.zeros((), jnp.int16)
    low16 = np.int16(-2 ** 15)

    def count16(limit, strict):
        def body(kb, acc):
            k0 = pl.multiple_of(kb * tk, tk)
            for c in range(tk // 128):
                blk = half_ref[pl.ds(k0 + 128 * c, 128), :].reshape(8, 16, tq)
                hit = jnp.where(blk > limit if strict else blk >= limit, one16, zero16)
                parts = [hit[j] for j in range(8)]
                while len(parts) > 1:
                    parts = [a + b for a, b in zip(parts[0::2], parts[1::2])]
                acc = acc + parts[0]
            return acc
        acc = lax.fori_loop(0, nkb, body, jnp.zeros((16, tq), jnp.int16))
        return jnp.broadcast_to(jnp.sum(acc.astype(jnp.int32), axis=0, keepdims=True), (16, tq))

    def search16(need):
        t = jnp.full((16, tq), -2 ** 15, jnp.int32)
        for bit in range(15, -1, -1):
            trial = t + 2 ** bit
            t = jnp.where(count16(trial.astype(jnp.int16), False) >= need, trial, t)
        return t

    t_hi = search16(n_sel)
    t_hi16 = t_hi.astype(jnp.int16)
    need_lo = n_sel - count16(t_hi16, True)

    def low_block(kb, carry):
        k0 = pl.multiple_of(kb * tk, tk)
        lo = ((key_ref[pl.ds(k0, tk), :] & 0xFFFF) - 2 ** 15).astype(jnp.int16).reshape(tk // 16, 16, tq)
        hi = half_ref[pl.ds(k0, tk), :].reshape(tk // 16, 16, tq)
        half_ref[pl.ds(k0, tk), :] = jnp.where(hi == t_hi16, lo, low16).reshape(tk, tq)
        return carry

    lax.fori_loop(0, nkb, low_block, 0)
    t_lo = search16(need_lo)
    thr = ((t_hi << 16) | (t_lo + 2 ** 15))[0:8, :]

    need = (need_lo - count16(t_lo.astype(jnp.int16), True))[0:8, :].astype(F32)
    ur = lax.broadcasted_iota(jnp.int32, (LANES, LANES), 0)
    uc = lax.broadcasted_iota(jnp.int32, (LANES, LANES), 1)
    earlier = (uc < ur).astype(BF16)
    ones8 = jnp.ones((8, LANES), BF16)
    key_s = lax.broadcasted_iota(jnp.int32, (LANES, tq), 0)
    qry_s = q0 + lax.broadcasted_iota(jnp.int32, (LANES, tq), 1)

    def select_block(kb, seen, masked):
        k0 = pl.multiple_of(kb * tk, tk)
        for g in range(tk // LANES):
            r0 = g * LANES
            blk = key_ref[pl.ds(k0 + r0, LANES), :].reshape(LANES // 8, 8, tq)
            eq = blk == thr[None]
            eqb = jnp.where(eq, 1.0, 0.0).reshape(LANES, tq).astype(BF16)
            rank = _dot(earlier, eqb).reshape(LANES // 8, 8, tq) + seen[None]
            tie_bias = jnp.where(rank < need[None], 0.0, NEG_BIG)
            bias = jnp.where(blk > thr[None], 0.0, jnp.where(eq, tie_bias, NEG_BIG)).reshape(LANES, tq)
            if masked:
                bias = jnp.where(k0 + r0 + key_s <= qry_s, bias, NEG_BIG)
            bias_ref[pl.ds(k0 + r0, LANES), :] = bias
            seen = seen + _dot(ones8, eqb)
        return seen

    seen = lax.fori_loop(0, last, functools.partial(select_block, masked=False), jnp.zeros((8, tq), F32))
    select_block(last, seen, True)

    qdt = qdt_ref[0]
    zq = jnp.zeros((LANES - DSA_HEAD_DIM, tq), BF16)
    qz = jnp.concatenate([jnp.concatenate([qdt[hd * DSA_HEAD_DIM:(hd + 1) * DSA_HEAD_DIM, :], zq], axis=0)
                          for hd in range(DSA_HEADS)], axis=1)
    wide = DSA_HEADS * tq

    def scores(kb, dst_ref):
        k0 = pl.multiple_of(kb * tk, tk)
        dst_ref[...] = _dot(dkv_ref[pl.ds(k0, tk), :], qz)

    def absorb(src_ref, kb, carry):
        m_i, l_i, acc = carry
        k0 = pl.multiple_of(kb * tk, tk)
        bias = bias_ref[pl.ds(k0, tk), :]
        s = src_ref[...] + jnp.concatenate([bias] * DSA_HEADS, axis=1)
        m_new = jnp.maximum(m_i, jnp.max(s, axis=0, keepdims=True))
        p = jnp.exp2(s - m_new)
        alpha = jnp.exp2(m_i - m_new)
        l_new = alpha * l_i + jnp.sum(p, axis=0, keepdims=True)
        pv = _dot(dkvt_ref[0, DSA_HEAD_DIM:, pl.ds(k0, tk)], p.astype(BF16))
        return m_new, l_new, alpha * acc + pv

    def pair(j, carry):
        kb = 2 * j
        scores(kb + 1, sb_ref)
        carry = absorb(sa_ref, kb, carry)
        scores(kb + 2, sa_ref)
        return absorb(sb_ref, kb + 1, carry)

    init = (jnp.full((1, wide), NEG_BIG, F32), jnp.zeros((1, wide), F32), jnp.zeros((DSA_HEAD_DIM, wide), F32))
    last = nkb - 1
    scores(0, sa_ref)
    carry = lax.fori_loop(0, last // 2, pair, init)

    def tail_odd(carry):
        scores(last, sb_ref)
        return absorb(sb_ref, last, absorb(sa_ref, last - 1, carry))

    def tail_even(carry):
        return absorb(sa_ref, last, carry)

    _, l_f, acc = lax.cond(last % 2 == 1, tail_odd, tail_even, carry)
    o_all = acc / l_f
    o_ref[...] = jnp.concatenate([o_all[:, hd * tq:(hd + 1) * tq] for hd in range(DSA_HEADS)],
                                 axis=0).T.astype(BF16)


def _dsa(qdt, iqt, iwt, dkv, dkvt, ikw, batch, seq):
    tq, tk = DSA_TQ, DSA_TK
    nq = seq // tq
    n_sel = min(DSA_TOPK, seq // 4)
    return pl.pallas_call(
        functools.partial(_dsa_kernel, tq=tq, tk=tk, n_sel=n_sel),
        out_shape=jax.ShapeDtypeStruct((batch * seq, 256), BF16),
        grid=(batch, nq),
        in_specs=[pl.BlockSpec((1, 256, tq), lambda b, i: (b, 0, i)),
                  pl.BlockSpec((1, 128, tq), lambda b, i: (b, 0, i)),
                  pl.BlockSpec((1, 8, tq), lambda b, i: (b, 0, i)),
                  pl.BlockSpec((seq, 128), lambda b, i: (b, 0)),
                  pl.BlockSpec((1, 128, seq), lambda b, i: (b, 0, 0)),
                  pl.BlockSpec((seq, 128), lambda b, i: (b, 0))],
        out_specs=pl.BlockSpec((tq, 256), lambda b, i: (b * nq + i, 0)),
        scratch_shapes=[pltpu.VMEM((seq, tq), jnp.int32), pltpu.VMEM((seq, tq), F32),
                        pltpu.VMEM((seq, tq), jnp.int16),
                        pltpu.VMEM((tk, DSA_HEADS * tq), F32), pltpu.VMEM((tk, DSA_HEADS * tq), F32)],
        compiler_params=_params(("parallel", "parallel")),
        name="dsa",
    )(qdt, iqt, iwt, dkv, dkvt, ikw)


def _mem_kv_kernel(mem_ref, g_ref, w_ref, kt_ref, v_ref):
    mn = _rms(mem_ref[0], g_ref[...]).astype(BF16)
    kv = _dot(mn, w_ref[...])
    kt_ref[0] = kv[:, :256].T.astype(BF16)
    v_ref[0] = kv[:, 256:].astype(BF16)


def _mem_kv(mem, gain, w_kv):
    b, m, d = mem.shape
    return pl.pallas_call(
        _mem_kv_kernel,
        out_shape=(jax.ShapeDtypeStruct((b, 256, m), BF16), jax.ShapeDtypeStruct((b, m, 256), BF16)),
        grid=(b,),
        in_specs=[pl.BlockSpec((1, m, d), lambda i: (i, 0, 0)), _const_spec((1, d)), _const_spec((d, 512))],
        out_specs=(pl.BlockSpec((1, 256, m), lambda i: (i, 0, 0)), pl.BlockSpec((1, m, 256), lambda i: (i, 0, 0))),
        compiler_params=_params(("parallel",)),
        name="mem_kv",
    )(mem, gain, w_kv)


def _merge_kernel(x_ref, ya_ref, yb_ref, yc_ref, yd_ref, gmix_ref, wg_ref, wbr_ref, wout_ref,
                  gq_ref, wq_ref, mkt_ref, mv_ref, wo_ref, gffn_ref, wr_ref, br_ref,
                  x2_ref, hp_ref, rt_ref, cnt_ref, run_ref, *, tm):
    x = x_ref[...]
    d = x.shape[-1]
    h = _rms(x, gmix_ref[...]).astype(BF16)
    merged = jnp.zeros((tm, d), F32)
    for n, y_ref in enumerate((ya_ref, yb_ref, yc_ref, yd_ref)):
        wn = wg_ref[:, n * d:n * d + d + LANES][:, GATE_SHIFT:GATE_SHIFT + d]
        gate = _sigmoid(_dot(h, wn))
        merged = merged + gate * _dot(y_ref[...], wbr_ref[n])
    x1 = x + _dot(merged.astype(BF16), wout_ref[...])
    h2 = _rms(x1, gq_ref[...]).astype(BF16)
    q = (_dot(h2, wq_ref[...]) * (MEM_HEAD_DIM ** -0.5)).astype(BF16)
    lane_head = lax.broadcasted_iota(jnp.int32, (tm, 256), 1) // MEM_HEAD_DIM
    mv = mv_ref[0]
    o = jnp.zeros((tm, 256), F32)
    for hd in range(MEM_HEADS):
        s = _dot(q[:, hd * MEM_HEAD_DIM:(hd + 1) * MEM_HEAD_DIM], mkt_ref[0, hd * MEM_HEAD_DIM:(hd + 1) * MEM_HEAD_DIM, :])
        p = jnp.exp(s - jnp.max(s, axis=-1, keepdims=True))
        p = p / jnp.sum(p, axis=-1, keepdims=True)
        o = o + jnp.where(lane_head == hd, _dot(p.astype(BF16), mv), 0.0)
    x2 = x1 + _dot(o.astype(BF16), wo_ref[...])
    x2_ref[...] = x2
    h3f = _rms(x2, gffn_ref[...])
    hp_ref[...] = _pack_bf16_pairs(h3f)
    h3 = h3f.astype(BF16)
    logits = _dot(h3, wr_ref[...]) + br_ref[...]
    lane = lax.broadcasted_iota(jnp.int32, (tm, LANES), 1)
    gl = jnp.where(lane < MOE_GROUPS, logits, -jnp.inf)
    gmax = jnp.max(gl, axis=-1, keepdims=True)
    gsel = jnp.min(jnp.where(gl == gmax, lane, LANES), axis=-1, keepdims=True)
    pg_sel = 1.0 / jnp.sum(jnp.exp(gl - gmax), axis=-1, keepdims=True)
    in_group = (lane - ROUTER_EXPERT_LANE) // MOE_EXPERTS_PER_GROUP == gsel
    el = jnp.where(in_group, logits, -jnp.inf)
    m1 = jnp.max(el, axis=-1, keepdims=True)
    i1 = jnp.min(jnp.where(el == m1, lane, LANES), axis=-1, keepdims=True)
    el2 = jnp.where(lane == i1, -jnp.inf, el)
    m2 = jnp.max(el2, axis=-1, keepdims=True)
    i2 = jnp.min(jnp.where(el2 == m2, lane, LANES), axis=-1, keepdims=True)
    e21 = jnp.exp(m2 - m1)
    c1 = pg_sel / (1.0 + e21)
    @pl.when(pl.program_id(0) == 0)
    def _():
        run_ref[...] = jnp.zeros_like(run_ref)

    oh1 = jnp.where(lane == i1, 1.0, 0.0)
    oh2 = jnp.where(lane == i2, 1.0, 0.0)
    both = oh1 + oh2
    tr = lax.broadcasted_iota(jnp.int32, (tm, tm), 0)
    tc = lax.broadcasted_iota(jnp.int32, (tm, tm), 1)
    before = _dot(jnp.where(tc < tr, 1.0, 0.0).astype(BF16), both.astype(BF16)) + run_ref[0:1, :]
    r1 = jnp.sum(oh1 * before, axis=-1, keepdims=True)
    r2 = jnp.sum(oh2 * before, axis=-1, keepdims=True)
    total = run_ref[...] + jnp.sum(both, axis=0, keepdims=True)
    run_ref[...] = total
    cnt_ref[...] = total
    ids = (jnp.where(lane == 0, i1, i2) - ROUTER_EXPERT_LANE).astype(F32)
    rt_ref[...] = jnp.where(lane < 2, ids, jnp.where(lane == 2, c1, jnp.where(lane == 3, c1 * e21,
                            jnp.where(lane == 4, r1, jnp.where(lane == 5, r2, 0.0)))))


def _merge(x, ys, gmix, wg, wbr, wout, gq, wq, mkt, mv, wo, gffn, wr, br, batch, seq):
    n, d = x.shape
    tm = MERGE_TM
    spt = seq // tm
    m = mv.shape[1]
    tok = lambda w: pl.BlockSpec((tm, w), lambda i: (i, 0))
    return pl.pallas_call(
        functools.partial(_merge_kernel, tm=tm),
        out_shape=(jax.ShapeDtypeStruct((n, d), F32), jax.ShapeDtypeStruct((n, d // 2), F32),
                   jax.ShapeDtypeStruct((n, LANES), F32), jax.ShapeDtypeStruct((8, LANES), F32)),
        grid=(n // tm,),
        in_specs=[tok(d), tok(256), tok(256), tok(256), tok(256),
                  _const_spec((1, d)), _const_spec((d, N_BRANCH * d + LANES)), _const_spec((N_BRANCH, 256, d)),
                  _const_spec((d, d)), _const_spec((1, d)), _const_spec((d, 256)),
                  pl.BlockSpec((1, 256, m), lambda i: (i // spt, 0, 0)),
                  pl.BlockSpec((1, m, 256), lambda i: (i // spt, 0, 0)),
                  _const_spec((256, d)), _const_spec((1, d)), _const_spec((d, LANES)), _const_spec((1, LANES))],
        out_specs=(tok(d), tok(d // 2), tok(LANES), pl.BlockSpec((8, LANES), lambda i: (0, 0))),
        scratch_shapes=[pltpu.VMEM((8, LANES), F32)],
        compiler_params=_params(("arbitrary",)),
        name="merge_mem_router",
    )(x, *ys, gmix, wg, wbr, wout, gq, wq, mkt, mv, wo, gffn, wr, br)


def _sc_gather_rows(table, idx):
    _, width = table.shape
    total = idx.shape[0]
    chunk, nbuf = SC_GATHER_CHUNK, SC_GATHER_BUFS
    workers = SC_CORES * SC_SUBCORES
    per_w = total // workers
    nch = per_w // chunk
    assert total % (workers * chunk * nbuf) == 0
    mesh = plsc.VectorSubcoreMesh(core_axis_name="c", subcore_axis_name="s")

    @functools.partial(
        pl.kernel, mesh=mesh, out_type=jax.ShapeDtypeStruct((total, width), table.dtype),
        scratch_types=[pltpu.VMEM((nch, chunk), jnp.int32), pltpu.VMEM((nbuf, chunk, width), table.dtype),
                       pltpu.SemaphoreType.DMA((nbuf,)), pltpu.SemaphoreType.DMA((nbuf,))])
    def gather_kernel(table_hbm, idx_hbm, out_hbm, idx_v, rows_v, gsem, wsem):
        wid = lax.axis_index("s") * SC_CORES + lax.axis_index("c")
        pltpu.sync_copy(idx_hbm.at[wid], idx_v)

        def gather(j, slot):
            return pltpu.make_async_copy(table_hbm.at[idx_v.at[j]], rows_v.at[slot], gsem.at[slot])

        def write(j, slot):
            off = pl.multiple_of(wid * per_w + j * chunk, chunk)
            return pltpu.make_async_copy(rows_v.at[slot], out_hbm.at[pl.ds(off, chunk)], wsem.at[slot])

        for slot in range(nbuf):
            gather(slot, slot).start()

        @pl.loop(0, nch // nbuf)
        def _(g):
            for slot in range(nbuf):
                j = g * nbuf + slot
                gather(j, slot).wait()
                write(j, slot).start()
                write(j, slot).wait()

                @pl.when(j + nbuf < nch)
                def _():
                    gather(j + nbuf, slot).start()

    return gather_kernel(table, idx.reshape(workers, nch, chunk))


def _sc_scatter_rows(table, dest2, total):
    n, width = table.shape
    chunk, nbuf = SC_GATHER_CHUNK, SC_GATHER_BUFS
    workers = SC_CORES * SC_SUBCORES
    per_w = n // workers
    nch = per_w // chunk
    assert n % (workers * chunk * nbuf) == 0
    mesh = plsc.VectorSubcoreMesh(core_axis_name="c", subcore_axis_name="s")

    @functools.partial(
        pl.kernel, mesh=mesh, out_type=jax.ShapeDtypeStruct((total, width), table.dtype),
        scratch_types=[pltpu.VMEM((2, nch, chunk), jnp.int32), pltpu.VMEM((nbuf, chunk, width), table.dtype),
                       pltpu.SemaphoreType.DMA((nbuf,)), pltpu.SemaphoreType.DMA((nbuf,))])
    def scatter_kernel(table_hbm, idx_hbm, out_hbm, idx_v, rows_v, rsem, wsem):
        wid = lax.axis_index("s") * SC_CORES + lax.axis_index("c")
        pltpu.sync_copy(idx_hbm.at[wid], idx_v)

        def read(j, slot):
            off = pl.multiple_of(wid * per_w + j * chunk, chunk)
            return pltpu.make_async_copy(table_hbm.at[pl.ds(off, chunk)], rows_v.at[slot], rsem.at[slot])

        def write(j, slot, k):
            return pltpu.make_async_copy(rows_v.at[slot], out_hbm.at[idx_v.at[k, j]], wsem.at[slot])

        for slot in range(nbuf):
            read(slot, slot).start()

        @pl.loop(0, nch // nbuf)
        def _(g):
            for slot in range(nbuf):
                j = g * nbuf + slot
                read(j, slot).wait()
                write(j, slot, 0).start()
                write(j, slot, 1).start()
                write(j, slot, 0).wait()
                write(j, slot, 1).wait()

                @pl.when(j + nbuf < nch)
                def _():
                    read(j + nbuf, slot).start()

    idx = dest2.reshape(2, workers, nch, chunk).transpose(1, 0, 2, 3)
    return scatter_kernel(table, idx)


def _dispatch_plan(rt, cnt, n):
    ne, blk = MOE_N_EXPERTS, MOE_BLOCK
    n_blocks = (2 * n) // blk + ne
    experts = jnp.arange(ne, dtype=jnp.int32)
    counts = cnt[0, ROUTER_EXPERT_LANE:ROUTER_EXPERT_LANE + ne].astype(jnp.int32)
    padded = (counts + blk - 1) // blk * blk
    pend = jnp.cumsum(padded)
    pstart = pend - padded
    ids = rt[:, 0:2].astype(jnp.int32)
    pos = rt[:, 4:6].astype(jnp.int32)
    first_row = jnp.sum(jnp.where(ids[:, :, None] == experts[None, None, :], pstart[None, None, :], 0), axis=-1)
    dest2 = (first_row + pos).T
    b0 = jnp.arange(n_blocks, dtype=jnp.int32) * blk
    block_e = jnp.minimum(jnp.sum((pend[None, :] <= b0[:, None]).astype(jnp.int32), axis=1), ne - 1)
    n_valid = jnp.clip(counts[block_e] - (b0 - pstart[block_e]), 0, blk).astype(jnp.int32)
    return dest2, block_e, n_valid


def _expert_block_kernel(be_ref, nv_ref, xs_ref, wg_ref, wu_ref, wd_ref, o_ref, wgb_ref, wub_ref, wdb_ref):
    b = pl.program_id(0)
    valid = nv_ref[b]

    @pl.when((b == 0) | (be_ref[b] != be_ref[jnp.maximum(b - 1, 0)]))
    def _():
        wgb_ref[...] = wg_ref[0].astype(BF16)
        wub_ref[...] = wu_ref[0].astype(BF16)
        wdb_ref[...] = wd_ref[0].astype(BF16)

    @pl.when(valid > 0)
    def _():
        row = lax.broadcasted_iota(jnp.int32, xs_ref.shape, 0)
        words = jnp.where(row < valid, xs_ref[...], 0.0)
        h = _unpack_bf16_pairs(words).astype(BF16)
        gt = _dot(h, wgb_ref[...])
        hid = gt * _sigmoid(gt) * _dot(h, wub_ref[...])
        o_ref[...] = _pack_bf16_pairs(_dot(hid.astype(BF16), wdb_ref[...]))

    @pl.when(valid == 0)
    def _():
        o_ref[...] = jnp.zeros_like(o_ref)


def _expert_blocks(xs, wg, wu, wd, layer, block_e, n_used):
    p_rows, half = xs.shape
    d, hid = wg.shape[-2:]
    blk = MOE_BLOCK
    grid_spec = pltpu.PrefetchScalarGridSpec(
        num_scalar_prefetch=2, grid=(p_rows // blk,),
        in_specs=[pl.BlockSpec((blk, half), lambda b, be, nu: (b, 0)),
                  pl.BlockSpec((None, 1, d, hid), lambda b, be, nu: (layer, be[b], 0, 0)),
                  pl.BlockSpec((None, 1, d, hid), lambda b, be, nu: (layer, be[b], 0, 0)),
                  pl.BlockSpec((None, 1, hid, d), lambda b, be, nu: (layer, be[b], 0, 0))],
        out_specs=pl.BlockSpec((blk, half), lambda b, be, nu: (b, 0)),
        scratch_shapes=[pltpu.VMEM((d, hid), BF16), pltpu.VMEM((d, hid), BF16), pltpu.VMEM((hid, d), BF16)])
    return pl.pallas_call(
        _expert_block_kernel, out_shape=jax.ShapeDtypeStruct((p_rows, half), F32), grid_spec=grid_spec,
        compiler_params=_params(("arbitrary",)),
        name="moe_expert_blocks",
    )(block_e, n_used, xs, wg, wu, wd)


def _combine_kernel(x_ref, y1_ref, y2_ref, rt_ref, gfin_ref, o_ref, *, final_norm):
    out = (x_ref[...] + rt_ref[:, 2:3] * _unpack_bf16_pairs(y1_ref[...])
           + rt_ref[:, 3:4] * _unpack_bf16_pairs(y2_ref[...]))
    o_ref[...] = _rms(out, gfin_ref[...]) if final_norm else out


def _combine(x2, y_halves, rt, gfin, final_norm):
    n, d = x2.shape
    tm = COMBINE_TM
    nt = n // tm
    return pl.pallas_call(
        functools.partial(_combine_kernel, final_norm=final_norm),
        out_shape=jax.ShapeDtypeStruct((n, d), F32),
        grid=(nt,),
        in_specs=[pl.BlockSpec((tm, d), lambda i: (i, 0)),
                  pl.BlockSpec((tm, d // 2), lambda i: (i, 0)),
                  pl.BlockSpec((tm, d // 2), lambda i: (i + nt, 0)),
                  pl.BlockSpec((tm, LANES), lambda i: (i, 0)), _const_spec((1, d))],
        out_specs=pl.BlockSpec((tm, d), lambda i: (i, 0)),
        compiler_params=_params(("parallel",)), name="moe_combine",
    )(x2, y_halves, y_halves, rt, gfin)


def _moe(x2, hp, rt, cnt, wg, wu, wd, layer, gfin, final_norm):
    n = x2.shape[0]
    dest2, block_e, n_valid = _dispatch_plan(rt, cnt, n)
    xs = _sc_scatter_rows(hp, dest2, block_e.shape[0] * MOE_BLOCK)
    yb = _expert_blocks(xs, wg, wu, wd, layer, block_e, n_valid)
    y_halves = _sc_gather_rows(yb, dest2.reshape(2 * n))
    return _combine(x2, y_halves, rt, gfin, final_norm)


def kernel(x, mem, positions, norm_mix, w_in, diff_lambda, hgrn_lb_logits, spatial_w, spatial_b, w_branch, w_out,
           norm_mem_q, norm_mem_kv, w_mem_q, w_mem_kv, w_mem_o, norm_ffn, w_router_group, b_router_group,
           w_router_expert, b_router_expert, w_exp_gate, w_exp_up, w_exp_down, norm_final):
    batch, seq, d = x.shape
    depth = w_in.shape[0]
    n = batch * seq
    xf = x.reshape(n, d)
    tabs = _rope_tables(positions)
    w_all = _w_in_bf16(w_in)
    row = lambda v: v.reshape(1, -1).astype(F32)
    for l in range(depth):
        lam_init = 0.8 - 0.6 * math.exp(-0.3 * l)
        g0 = C_GATE - GATE_SHIFT
        w_gate = lax.slice(w_all, (l, 0, g0), (l + 1, d, g0 + N_BRANCH * d + LANES)).reshape(d, -1)
        sw = spatial_w[l].reshape(SGU_GROUPS * SGU_CHUNK, SGU_CHUNK)
        sb = jnp.repeat(spatial_b[l].T, SGU_GROUP_DIM, axis=1)
        qat, ka, vat, hb, y_c, qdt, iqt, dkv, dkvt, ikw, iwt = _projection(
            xf, row(norm_mix[l]), w_all, l, tabs, sw, sb, batch, seq)
        y_a = _diff_attention(diff_lambda[l], qat, ka, vat, lam_init, batch, seq)
        y_b = _hgrn(hgrn_lb_logits, hb, l, batch, seq)
        y_d = _dsa(qdt, iqt, iwt, dkv, dkvt, ikw, batch, seq)
        mkt, mv = _mem_kv(mem, row(norm_mem_kv[l]), w_mem_kv[l].astype(BF16))
        e0, e1 = ROUTER_EXPERT_LANE, ROUTER_EXPERT_LANE + MOE_N_EXPERTS
        wr = jnp.zeros((d, LANES), F32)
        wr = wr.at[:, :MOE_GROUPS].set(w_router_group[l]).at[:, e0:e1].set(w_router_expert[l]).astype(BF16)
        br = jnp.zeros((1, LANES), F32)
        br = br.at[0, :MOE_GROUPS].set(b_router_group[l]).at[0, e0:e1].set(b_router_expert[l])
        x2, hp, rt, cnt = _merge(xf, (y_a, y_b, y_c, y_d), row(norm_mix[l]), w_gate, w_branch[l].astype(BF16),
                                 w_out[l].astype(BF16), row(norm_mem_q[l]), w_mem_q[l].astype(BF16), mkt, mv,
                                 w_mem_o[l].astype(BF16), row(norm_ffn[l]), wr, br, batch, seq)
        xf = _moe(x2, hp, rt, cnt, w_exp_gate, w_exp_up, w_exp_down, l,
                  row(norm_final), final_norm=(l == depth - 1))
    return xf.reshape(batch, seq, d)
```

```python
import functools
import math

import numpy as np
import jax
import jax.numpy as jnp
from jax import lax
from jax.experimental import pallas as pl
from jax.experimental.pallas import tpu as pltpu
from jax.experimental.pallas import tpu_sc as plsc

F32 = jnp.float32
BF16 = jnp.bfloat16

NORM_EPS = 1e-6
ROPE_THETA = 10000.0
NEG_BIG = -1e30

N_BRANCH = 4
DIFF_HEADS = 4
DIFF_HEAD_DIM = 32
HGRN_DIM = 64
HGRN_CHUNK = 32
HGRN_UNROLL = 8
HGRN_MIN_FORGET = 1e-30
SGU_GROUPS = 4
SGU_GROUP_DIM = 64
SGU_CHUNK = 128
DSA_HEADS = 4
DSA_HEAD_DIM = 64
DSA_IDX_HEADS = 4
DSA_IDX_DIM = 32
DSA_TOPK = 256
MEM_HEADS = 4
MEM_HEAD_DIM = 64
MOE_GROUPS = 4
MOE_EXPERTS_PER_GROUP = 8
MOE_N_EXPERTS = 32
MOE_BLOCK = 512
ROUTER_EXPERT_LANE = 32
SC_CORES = 2
SC_SUBCORES = 16
SC_GATHER_CHUNK = 16
SC_GATHER_BUFS = 4

LANES = 128
VMEM_LIMIT = 56 * 1024 * 1024

PROJ_TM = 1024
DIFF_TQ = 512
DIFF_TK = 512
HGRN_TC = 1024
DSA_TQ = 512
DSA_TK = 512
MERGE_TM = 1024
COMBINE_TM = 1024
ROPE_TM = 1024

C_AQ, C_AK, C_AV = 0, 256, 512
C_HB = 768
C_UV = 1792
C_DQ = 2304
C_DKV = 2560
C_IQ = 2688
C_IKW = 2816
IW_LANE = 32
C_GATE = 2852
GATE_SHIFT = C_GATE % 128
C_TOTAL = 2944
LOG2E = math.log2(math.e)


def _params(sem):
    return pltpu.CompilerParams(dimension_semantics=sem, vmem_limit_bytes=VMEM_LIMIT)


def _const_spec(shape):
    nd = len(shape)
    return pl.BlockSpec(shape, lambda *_: (0,) * nd, pipeline_mode=pl.Buffered(1))


def _rms(xf, gain=None):
    y = xf * lax.rsqrt(jnp.mean(xf * xf, axis=-1, keepdims=True) + NORM_EPS)
    return y if gain is None else y * gain


def _sigmoid(x):
    return 0.5 * jnp.tanh(0.5 * x) + 0.5


def _pack_bf16_pairs(x):
    w = x.shape[-1] // 2
    xb = x.astype(BF16).astype(F32)
    lo = lax.shift_right_logical(pltpu.bitcast(xb[:, :w], jnp.int32), 16)
    hi = pltpu.bitcast(xb[:, w:], jnp.int32) & jnp.int32(-65536)
    return pltpu.bitcast(hi | lo, F32)


def _unpack_bf16_pairs(words):
    bits = pltpu.bitcast(words, jnp.int32)
    lo = pltpu.bitcast(bits << 16, F32)
    hi = pltpu.bitcast(bits & jnp.int32(-65536), F32)
    return jnp.concatenate([lo, hi], axis=1)


def _dot(a, b):
    return jnp.dot(a, b, preferred_element_type=F32)


def _dot_nt(a, b):
    return lax.dot_general(a, b, (((1,), (1,)), ((), ())), preferred_element_type=F32)


def _rope_table_kernel(pos_ref, frq_ref, sgn_ref, c32_ref, s32_ref, c64_ref, s64_ref):
    pos = pos_ref[...].astype(F32)
    twice = lambda t: jnp.concatenate([t, t], axis=1)
    a32 = pos * frq_ref[0:1, :LANES]
    a64 = pos * frq_ref[1:2, :LANES]
    c32_ref[...] = twice(jnp.cos(a32))
    s32_ref[...] = twice(jnp.sin(a32) * sgn_ref[0:1, :LANES])
    c64_ref[...] = twice(jnp.cos(a64))
    s64_ref[...] = twice(jnp.sin(a64) * sgn_ref[1:2, :LANES])


def _rope_tables(positions):
    n = positions.size
    pos = positions.reshape(n, 1).astype(jnp.int32)
    lane = np.arange(256)
    inv32 = ROPE_THETA ** (-jnp.arange(16, dtype=F32) * (2.0 / 32))
    inv64 = ROPE_THETA ** (-jnp.arange(32, dtype=F32) * (2.0 / 64))
    frq = jnp.stack([inv32[lane % 16], inv64[lane % 32]])
    sgn = jnp.asarray(np.stack([np.where(lane % 32 < 16, -1.0, 1.0),
                                np.where(lane % 64 < 32, -1.0, 1.0)]), F32)
    tm = ROPE_TM
    tab = jax.ShapeDtypeStruct((n, 256), F32)
    return pl.pallas_call(
        _rope_table_kernel,
        out_shape=(tab, tab, tab, tab),
        grid=(n // tm,),
        in_specs=[pl.BlockSpec((tm, 1), lambda i: (i, 0)), _const_spec((2, 256)), _const_spec((2, 256))],
        out_specs=tuple(pl.BlockSpec((tm, 256), lambda i: (i, 0)) for _ in range(4)),
        compiler_params=_params(("parallel",)),
        name="rope_tables",
    )(pos, frq, sgn)


def _w_in_kernel(wt_ref, o_ref, *, width):
    row = lax.broadcasted_iota(jnp.int32, (LANES, wt_ref.shape[-1]), 0)
    valid = row < width - pl.program_id(0) * LANES
    for layer in range(o_ref.shape[0]):
        o_ref[layer] = jnp.where(valid, wt_ref[:, layer, :], 0.0).T.astype(BF16)


def _w_in_bf16(w_in):
    depth, d, width = w_in.shape
    nblk = pl.cdiv(width, LANES)
    return pl.pallas_call(
        functools.partial(_w_in_kernel, width=width), out_shape=jax.ShapeDtypeStruct((depth, d, nblk * LANES), BF16),
        grid=(nblk,), in_specs=[pl.BlockSpec((LANES, depth, d), lambda i: (i, 0, 0))],
        out_specs=pl.BlockSpec((depth, d, LANES), lambda i: (0, 0, i)),
        compiler_params=_params(("parallel",)), name="w_in_bf16",
    )(jnp.transpose(w_in, (2, 0, 1)))


def _gelu_tanh(x):
    return 0.5 * x * (1.0 + jnp.tanh(math.sqrt(2.0 / math.pi) * (x + 0.044715 * (x * x * x))))


def _rope(x, cos, sin_signed, half):
    w = x.shape[-1]
    lane = lax.broadcasted_iota(jnp.int32, x.shape, 1)
    partner = jnp.where(lane % (2 * half) < half, pltpu.roll(x, w - half, 1), pltpu.roll(x, half, 1))
    return x * cos + partner * sin_signed


def _proj_kernel(x_ref, g_ref, w_ref, c32_ref, s32_ref, c64_ref, s64_ref, sw_ref, sb_ref,
                 qat_ref, ka_ref, vat_ref, hb_ref, yc_ref, qdt_ref, iqt_ref, dkv_ref, dkvt_ref, ikw_ref, iwt_ref,
                 *, tm):
    h = _rms(x_ref[...], g_ref[...]).astype(BF16)

    def proj(c0, width):
        return _dot(h, w_ref[:, c0:c0 + width])

    c32, s32, c64, s64 = c32_ref[...], s32_ref[...], c64_ref[...], s64_ref[...]
    qat_ref[0] = (_rope(proj(C_AQ, 256), c32, s32, 16) * (DIFF_HEAD_DIM ** -0.5 * LOG2E)).T.astype(BF16)
    ka_ref[...] = _rope(proj(C_AK, 256), c32, s32, 16).astype(BF16)
    vat_ref[0] = proj(C_AV, 256).astype(BF16).T
    hb_ref[...] = proj(C_HB, 1024)
    qdt_ref[0] = (_rope(proj(C_DQ, 256), c64, s64, 32) * (DSA_HEAD_DIM ** -0.5 * LOG2E)).T.astype(BF16)
    iqt_ref[0] = _rope(proj(C_IQ, 128), c32[:, :128], s32[:, :128], 16).T.astype(BF16)
    lane = lax.broadcasted_iota(jnp.int32, (tm, 128), 1)
    is_k = lane < DSA_HEAD_DIM
    dkv = _rope(proj(C_DKV, 128), jnp.where(is_k, c64[:, :128], 1.0), jnp.where(is_k, s64[:, :128], 0.0), 32)
    is_ik = lane < DSA_IDX_DIM
    ikw = _rope(proj(C_IKW, 128), jnp.where(is_ik, c32[:, :128], 1.0), jnp.where(is_ik, s32[:, :128], 0.0), 16)
    dkv_ref[...] = dkv.astype(BF16)
    dkvt_ref[0] = dkv.T.astype(BF16)
    ikw_ref[...] = ikw.astype(BF16)
    iw_scale = DSA_IDX_HEADS ** -0.5 * DSA_IDX_DIM ** -0.5
    iwt_ref[0] = (ikw * iw_scale).T[IW_LANE:IW_LANE + 8, :]
    uv = _gelu_tanh(proj(C_UV, 512))
    u, v = uv[:, :256], uv[:, 256:]
    mu = jnp.mean(v, axis=-1, keepdims=True)
    vc = v - mu
    vn = (vc * lax.rsqrt(jnp.mean(vc * vc, axis=-1, keepdims=True) + NORM_EPS)).astype(BF16)
    r = lax.broadcasted_iota(jnp.int32, (SGU_GROUPS * SGU_CHUNK, SGU_CHUNK), 0)
    c = lax.broadcasted_iota(jnp.int32, (SGU_GROUPS * SGU_CHUNK, SGU_CHUNK), 1)
    wt = jnp.where((r % SGU_CHUNK) >= c, sw_ref[...], 0.0).astype(BF16)
    lane_grp = lax.broadcasted_iota(jnp.int32, (SGU_CHUNK, 256), 1) // SGU_GROUP_DIM
    for ch in range(tm // SGU_CHUNK):
        r0 = ch * SGU_CHUNK
        full = _dot(wt, vn[r0:r0 + SGU_CHUNK, :])
        mixed = sb_ref[...]
        for g in range(SGU_GROUPS):
            mixed = mixed + jnp.where(lane_grp == g, full[g * SGU_CHUNK:(g + 1) * SGU_CHUNK, :], 0.0)
        yc_ref[r0:r0 + SGU_CHUNK, :] = (u[r0:r0 + SGU_CHUNK, :] * mixed).astype(BF16)


def _projection(x, gain, w_all, layer, tabs, sw, sb, batch, seq):
    n, d = x.shape
    tm = PROJ_TM
    spt = seq // tm
    tok = lambda w: pl.BlockSpec((tm, w), lambda i: (i, 0))
    tr = lambda rows: pl.BlockSpec((1, rows, tm), lambda i: (i // spt, 0, i % spt))
    out_shape = (
        jax.ShapeDtypeStruct((batch, 256, seq), BF16),
        jax.ShapeDtypeStruct((n, 256), BF16),
        jax.ShapeDtypeStruct((batch, 256, seq), BF16),
        jax.ShapeDtypeStruct((n, 1024), F32),
        jax.ShapeDtypeStruct((n, 256), BF16),
        jax.ShapeDtypeStruct((batch, 256, seq), BF16),
        jax.ShapeDtypeStruct((batch, 128, seq), BF16),
        jax.ShapeDtypeStruct((n, 128), BF16),
        jax.ShapeDtypeStruct((batch, 128, seq), BF16),
        jax.ShapeDtypeStruct((n, 128), BF16),
        jax.ShapeDtypeStruct((batch, 8, seq), F32),
    )
    return pl.pallas_call(
        functools.partial(_proj_kernel, tm=tm),
        out_shape=out_shape,
        grid=(n // tm,),
        in_specs=[tok(d), _const_spec((1, d)),
                  pl.BlockSpec((None, d, C_TOTAL), lambda i: (layer, 0, 0), pipeline_mode=pl.Buffered(1)),
                  tok(256), tok(256), tok(256), tok(256),
                  _const_spec((SGU_GROUPS * SGU_CHUNK, SGU_CHUNK)), _const_spec((SGU_CHUNK, 256))],
        out_specs=(tr(256), tok(256), tr(256), tok(1024), tok(256), tr(256), tr(128), tok(128), tr(128), tok(128), tr(8)),
        compiler_params=_params(("parallel",)),
        name="projection",
    )(x, gain, w_all, *tabs, sw, sb)


def _diff_attn_kernel(lam_ref, qt_ref, k_ref, vt_ref, o_ref, sa_ref, sb_ref, *, lam_init, tq, tk):
    q0 = pl.program_id(1) * tq
    kb_diag = q0 // tk
    lv = lam_ref[...]
    lam = (jnp.exp(jnp.sum(lv[0:1] * lv[1:2], axis=-1, keepdims=True))
           - jnp.exp(jnp.sum(lv[2:3] * lv[3:4], axis=-1, keepdims=True)) + lam_init)
    qt = qt_ref[0]
    feat = lax.broadcasted_iota(jnp.int32, (256, tq), 0) // DIFF_HEAD_DIM
    n_maps = 2 * DIFF_HEADS
    qz = jnp.concatenate([jnp.where(feat == i, qt, jnp.zeros_like(qt)) for i in range(n_maps)], axis=1)
    wide = n_maps * tq
    key_i = lax.broadcasted_iota(jnp.int32, (tk, wide), 0)
    qry_i = q0 + lax.broadcasted_iota(jnp.int32, (tk, wide), 1) % tq

    def scores(kb, dst_ref):
        k0 = pl.multiple_of(kb * tk, tk)
        dst_ref[...] = _dot(k_ref[pl.ds(k0, tk), :], qz)

    def absorb(src_ref, kb, carry, masked):
        m_i, l_i, acc = carry
        k0 = pl.multiple_of(kb * tk, tk)
        s = src_ref[...]
        if masked:
            s = jnp.where(k0 + key_i <= qry_i, s, NEG_BIG)
        m_new = jnp.maximum(m_i, jnp.max(s, axis=0, keepdims=True))
        p = jnp.exp2(s - m_new)
        alpha = jnp.exp2(m_i - m_new)
        l_new = alpha * l_i + jnp.sum(p, axis=0, keepdims=True)
        pb = p.astype(BF16)
        pv = jnp.concatenate(
            [_dot(vt_ref[0, hd * 64:(hd + 1) * 64, pl.ds(k0, tk)], pb[:, 2 * hd * tq:(2 * hd + 2) * tq])
             for hd in range(DIFF_HEADS)], axis=1)
        return m_new, l_new, alpha * acc + pv

    def pair(j, carry):
        kb = 2 * j
        scores(kb + 1, sb_ref)
        carry = absorb(sa_ref, kb, carry, False)
        scores(kb + 2, sa_ref)
        return absorb(sb_ref, kb + 1, carry, False)

    init = (jnp.full((1, wide), NEG_BIG, F32), jnp.zeros((1, wide), F32), jnp.zeros((64, wide), F32))
    scores(0, sa_ref)
    carry = lax.fori_loop(0, kb_diag // 2, pair, init)

    def tail_odd(carry):
        scores(kb_diag, sb_ref)
        carry = absorb(sa_ref, kb_diag - 1, carry, False)
        return absorb(sb_ref, kb_diag, carry, True)

    def tail_even(carry):
        return absorb(sa_ref, kb_diag, carry, True)

    _, l_f, acc = lax.cond(kb_diag % 2 == 1, tail_odd, tail_even, carry)
    o_all = acc / l_f
    heads = []
    for hd in range(DIFF_HEADS):
        o0 = o_all[:, 2 * hd * tq:(2 * hd + 1) * tq]
        o1 = o_all[:, (2 * hd + 1) * tq:(2 * hd + 2) * tq]
        o_h = o0 - lam * o1
        ms = jnp.mean(o_h * o_h, axis=0, keepdims=True)
        heads.append(o_h * lax.rsqrt(ms + NORM_EPS) * (1.0 - lam_init))
    o_ref[...] = jnp.concatenate(heads, axis=0).T.astype(BF16)


def _diff_attention(lam_vec, qat, ka, vat, lam_init, batch, seq):
    tq, tk = DIFF_TQ, DIFF_TK
    nq = seq // tq
    return pl.pallas_call(
        functools.partial(_diff_attn_kernel, lam_init=lam_init, tq=tq, tk=tk),
        out_shape=jax.ShapeDtypeStruct((batch * seq, 256), BF16),
        grid=(batch, nq),
        in_specs=[_const_spec((4, DIFF_HEAD_DIM)),
                  pl.BlockSpec((1, 256, tq), lambda b, i: (b, 0, i)),
                  pl.BlockSpec((seq, 256), lambda b, i: (b, 0)),
                  pl.BlockSpec((1, 256, seq), lambda b, i: (b, 0, 0))],
        out_specs=pl.BlockSpec((tq, 256), lambda b, i: (b * nq + i, 0)),
        scratch_shapes=[pltpu.VMEM((tk, 2 * DIFF_HEADS * tq), F32), pltpu.VMEM((tk, 2 * DIFF_HEADS * tq), F32)],
        compiler_params=_params(("parallel", "parallel")),
        name="diff_attention",
    )(lam_vec, qat, ka, vat)


def _hgrn_kernel(lbl_ref, hb_ref, o_ref, st_ref, pstk_ref, *, layer, tc):
    cz = HGRN_CHUNK
    w = 256

    @pl.when(pl.program_id(1) == 0)
    def _():
        st_ref[...] = jnp.zeros_like(st_ref)

    lg = lbl_ref[...]
    e = jnp.exp(lg - jnp.max(lg, axis=0, keepdims=True))
    lw = e / jnp.sum(e, axis=0, keepdims=True)
    lb = jnp.sum(lw[0:layer + 1], axis=0, keepdims=True) - lw[0:1]

    scan_row = lax.broadcasted_iota(jnp.int32, (cz, w), 0)
    rb = lax.broadcasted_iota(jnp.int32, (w, w), 0) // HGRN_DIM
    cb = lax.broadcasted_iota(jnp.int32, (w, w), 1) // HGRN_DIM
    same_head = rb == cb
    head_ones = same_head.astype(BF16)
    trows = {r: r + lax.broadcasted_iota(jnp.int32, (16, w), 0) for r in range(0, cz, 16)}

    def chunk(c, carry):
        r0 = pl.multiple_of(c * cz, cz)
        q = hb_ref[pl.ds(r0, cz), 0:256]
        fp = hb_ref[pl.ds(r0, cz), 256:512]
        v = hb_ref[pl.ds(r0, cz), 512:768]
        g = hb_ref[pl.ds(r0, cz), 768:1024]
        qf = q * _sigmoid(q)
        f = lb + (1.0 - lb) * jax.nn.sigmoid(fp)
        log_f = jnp.log(jnp.maximum(f, HGRN_MIN_FORGET))
        kf = (1.0 - lb) * jax.nn.sigmoid(-fp)
        bc = log_f
        step = 1
        while step < cz:
            bc = bc + jnp.where(scan_row >= step, pltpu.roll(bc, step, 0), 0.0)
            step *= 2
        st = st_ref[...]
        o = _dot_nt((qf * jnp.exp(bc)).astype(BF16), st.astype(BF16))
        later = {}
        for blk in range(1, cz // 16):
            r = bc[16 * blk - 1:16 * blk, :]
            later[blk] = (qf[16 * blk:16 * blk + 16, :] * jnp.exp(bc[16 * blk:16 * blk + 16, :] - r),
                          kf[:16 * blk, :] * jnp.exp(r - bc[:16 * blk, :]))
        for s in range(cz):
            r_lo = (s // 16) * 16
            arg = bc[r_lo:r_lo + 16, :] - bc[s:s + 1, :]
            if s > r_lo:
                arg = jnp.where(trows[r_lo] >= s, arg, NEG_BIG)
            p = qf[r_lo:r_lo + 16, :] * kf[s:s + 1, :] * jnp.exp(arg)
            if r_lo:
                pstk_ref[s * cz:s * cz + r_lo, :] = jnp.zeros((r_lo, w), BF16)
            pstk_ref[s * cz + r_lo:s * cz + r_lo + 16, :] = p.astype(BF16)
            for blk in range(s // 16 + 1, cz // 16):
                q_dec, k_dec = later[blk]
                pstk_ref[s * cz + 16 * blk:s * cz + 16 * blk + 16, :] = (q_dec * k_dec[s:s + 1, :]).astype(BF16)
        accs = [jnp.zeros((16, w), F32) for _ in range(cz // 16)]
        for sg in range(cz // 16):
            att = _dot(pstk_ref[sg * 16 * cz:(sg + 1) * 16 * cz, :], head_ones)
            for sl in range(16):
                s = sg * 16 + sl
                for j in range(sg, cz // 16):
                    accs[j] = accs[j] + att[sl * cz + 16 * j:sl * cz + 16 * j + 16, :] * v[s:s + 1, :]
        o = o + jnp.concatenate(accs, axis=0)
        b_end = bc[cz - 1:cz, :]
        kd = kf * jnp.exp(b_end - bc)
        upd = _dot(v.T.astype(BF16), kd.astype(BF16))
        st_ref[...] = st * jnp.exp(b_end) + jnp.where(same_head, upd, 0.0)
        ms = _dot(o * o, head_ones.astype(F32)) * (1.0 / HGRN_DIM)
        y = o * lax.rsqrt(ms + NORM_EPS)
        o_ref[pl.ds(r0, cz), :] = (y * (g * _sigmoid(g))).astype(BF16)
        return carry

    def group(gi, carry):
        for u in range(HGRN_UNROLL):
            chunk(gi * HGRN_UNROLL + u, carry)
        return carry

    lax.fori_loop(0, tc // cz // HGRN_UNROLL, group, 0)


def _hgrn(lb_logits, hb, layer, batch, seq):
    tc = HGRN_TC
    nt = seq // tc
    cz = HGRN_CHUNK
    return pl.pallas_call(
        functools.partial(_hgrn_kernel, layer=layer, tc=tc),
        out_shape=jax.ShapeDtypeStruct((batch * seq, 256), BF16),
        grid=(batch, nt),
        in_specs=[_const_spec(lb_logits.shape),
                  pl.BlockSpec((tc, 1024), lambda b, i: (b * nt + i, 0))],
        out_specs=pl.BlockSpec((tc, 256), lambda b, i: (b * nt + i, 0)),
        scratch_shapes=[pltpu.VMEM((256, 256), F32), pltpu.VMEM((cz * cz, 256), BF16)],
        compiler_params=_params(("parallel", "arbitrary")),
        name="hgrn2",
    )(lb_logits, hb)


def _dsa_kernel(qdt_ref, iqt_ref, iwt_ref, dkv_ref, dkvt_ref, ikw_ref, o_ref, key_ref, bias_ref, half_ref,
                sa_ref, sb_ref, *, tq, tk, n_sel):
    q0 = pl.program_id(1) * tq
    nkb = q0 // tk + 1
    key_i = lax.broadcasted_iota(jnp.int32, (tk, tq), 0)
    qry_i = q0 + lax.broadcasted_iota(jnp.int32, (tk, tq), 1)
    grp = tk // 8
    rows8 = lambda x: x.reshape(grp, 8, tq)
    iqt = iqt_ref[0]
    zpad = jnp.zeros((LANES - DSA_IDX_DIM, tq), BF16)
    iqz = jnp.concatenate([jnp.concatenate([iqt[hd * DSA_IDX_DIM:(hd + 1) * DSA_IDX_DIM, :], zpad], axis=0)
                           for hd in range(DSA_IDX_HEADS)], axis=1)
    iw = iwt_ref[0]

    last = nkb - 1

    def score_block(kb, carry, masked):
        k0 = pl.multiple_of(kb * tk, tk)
        sh = jnp.maximum(_dot(ikw_ref[pl.ds(k0, tk), :], iqz), 0.0)
        sc = jnp.zeros((tk, tq), F32)
        for hd in range(DSA_IDX_HEADS):
            sc = sc + sh[:, hd * tq:(hd + 1) * tq] * iw[hd:hd + 1, :]
        sc = sc + 0.0
        if masked:
            sc = jnp.where(k0 + key_i <= qry_i, sc, -jnp.inf)
        bits = pltpu.bitcast(sc, jnp.int32)
        key = jnp.where(bits < 0, bits ^ jnp.int32(0x7FFFFFFF), bits)
        key_ref[pl.ds(k0, tk), :] = key
        half_ref[pl.ds(k0, tk), :] = (key >> 16).astype(jnp.int16)
        return carry

    lax.fori_loop(0, last, functools.partial(score_block, masked=False), 0)
    score_block(last, 0, True)

    one16, zero16 = jnp.ones((), jnp.int16), jnp.zeros((), jnp.int16)
    low16 = np.int16(-2 ** 15)

    def count16(limit, strict):
        def body(kb, acc):
            k0 = pl.multiple_of(kb * tk, tk)
            for c in range(tk // 128):
                blk = half_ref[pl.ds(k0 + 128 * c, 128), :].reshape(8, 16, tq)
                hit = jnp.where(blk > limit if strict else blk >= limit, one16, zero16)
                parts = [hit[j] for j in range(8)]
                while len(parts) > 1:
                    parts = [a + b for a, b in zip(parts[0::2], parts[1::2])]
                acc = acc + parts[0]
            return acc
        acc = lax.fori_loop(0, nkb, body, jnp.zeros((16, tq), jnp.int16))
        return jnp.broadcast_to(jnp.sum(acc.astype(jnp.int32), axis=0, keepdims=True), (16, tq))

    def search16(need):
        t = jnp.full((16, tq), -2 ** 15, jnp.int32)
        for bit in range(15, -1, -1):
            trial = t + 2 ** bit
            t = jnp.where(count16(trial.astype(jnp.int16), False) >= need, trial, t)
        return t

    t_hi = search16(n_sel)
    t_hi16 = t_hi.astype(jnp.int16)
    need_lo = n_sel - count16(t_hi16, True)

    def low_block(kb, carry):
        k0 = pl.multiple_of(kb * tk, tk)
        lo = ((key_ref[pl.ds(k0, tk), :] & 0xFFFF) - 2 ** 15).astype(jnp.int16).reshape(tk // 16, 16, tq)
        hi = half_ref[pl.ds(k0, tk), :].reshape(tk // 16, 16, tq)
        half_ref[pl.ds(k0, tk), :] = jnp.where(hi == t_hi16, lo, low16).reshape(tk, tq)
        return carry

    lax.fori_loop(0, nkb, low_block, 0)
    t_lo = search16(need_lo)
    thr = ((t_hi << 16) | (t_lo + 2 ** 15))[0:8, :]

    need = (need_lo - count16(t_lo.astype(jnp.int16), True))[0:8, :].astype(F32)
    ur = lax.broadcasted_iota(jnp.int32, (LANES, LANES), 0)
    uc = lax.broadcasted_iota(jnp.int32, (LANES, LANES), 1)
    earlier = (uc < ur).astype(BF16)
    ones8 = jnp.ones((8, LANES), BF16)
    key_s = lax.broadcasted_iota(jnp.int32, (LANES, tq), 0)
    qry_s = q0 + lax.broadcasted_iota(jnp.int32, (LANES, tq), 1)

    def select_block(kb, seen, masked):
        k0 = pl.multiple_of(kb * tk, tk)
        for g in range(tk // LANES):
            r0 = g * LANES
            blk = key_ref[pl.ds(k0 + r0, LANES), :].reshape(LANES // 8, 8, tq)
            eq = blk == thr[None]
            eqb = jnp.where(eq, 1.0, 0.0).reshape(LANES, tq).astype(BF16)
            rank = _dot(earlier, eqb).reshape(LANES // 8, 8, tq) + seen[None]
            tie_bias = jnp.where(rank < need[None], 0.0, NEG_BIG)
            bias = jnp.where(blk > thr[None], 0.0, jnp.where(eq, tie_bias, NEG_BIG)).reshape(LANES, tq)
            if masked:
                bias = jnp.where(k0 + r0 + key_s <= qry_s, bias, NEG_BIG)
            bias_ref[pl.ds(k0 + r0, LANES), :] = bias
            seen = seen + _dot(ones8, eqb)
        return seen

    seen = lax.fori_loop(0, last, functools.partial(select_block, masked=False), jnp.zeros((8, tq), F32))
    select_block(last, seen, True)

    qdt = qdt_ref[0]
    zq = jnp.zeros((LANES - DSA_HEAD_DIM, tq), BF16)
    qz = jnp.concatenate([jnp.concatenate([qdt[hd * DSA_HEAD_DIM:(hd + 1) * DSA_HEAD_DIM, :], zq], axis=0)
                          for hd in range(DSA_HEADS)], axis=1)
    wide = DSA_HEADS * tq

    def scores(kb, dst_ref):
        k0 = pl.multiple_of(kb * tk, tk)
        dst_ref[...] = _dot(dkv_ref[pl.ds(k0, tk), :], qz)

    def absorb(src_ref, kb, carry):
        m_i, l_i, acc = carry
        k0 = pl.multiple_of(kb * tk, tk)
        bias = bias_ref[pl.ds(k0, tk), :]
        s = src_ref[...] + jnp.concatenate([bias] * DSA_HEADS, axis=1)
        m_new = jnp.maximum(m_i, jnp.max(s, axis=0, keepdims=True))
        p = jnp.exp2(s - m_new)
        alpha = jnp.exp2(m_i - m_new)
        l_new = alpha * l_i + jnp.sum(p, axis=0, keepdims=True)
        pv = _dot(dkvt_ref[0, DSA_HEAD_DIM:, pl.ds(k0, tk)], p.astype(BF16))
        return m_new, l_new, alpha * acc + pv

    def pair(j, carry):
        kb = 2 * j
        scores(kb + 1, sb_ref)
        carry = absorb(sa_ref, kb, carry)
        scores(kb + 2, sa_ref)
        return absorb(sb_ref, kb + 1, carry)

    init = (jnp.full((1, wide), NEG_BIG, F32), jnp.zeros((1, wide), F32), jnp.zeros((DSA_HEAD_DIM, wide), F32))
    last = nkb - 1
    scores(0, sa_ref)
    carry = lax.fori_loop(0, last // 2, pair, init)

    def tail_odd(carry):
        scores(last, sb_ref)
        return absorb(sb_ref, last, absorb(sa_ref, last - 1, carry))

    def tail_even(carry):
        return absorb(sa_ref, last, carry)

    _, l_f, acc = lax.cond(last % 2 == 1, tail_odd, tail_even, carry)
    o_all = acc / l_f
    o_ref[...] = jnp.concatenate([o_all[:, hd * tq:(hd + 1) * tq] for hd in range(DSA_HEADS)],
                                 axis=0).T.astype(BF16)


def _dsa(qdt, iqt, iwt, dkv, dkvt, ikw, batch, seq):
    tq, tk = DSA_TQ, DSA_TK
    nq = seq // tq
    n_sel = min(DSA_TOPK, seq // 4)
    return pl.pallas_call(
        functools.partial(_dsa_kernel, tq=tq, tk=tk, n_sel=n_sel),
        out_shape=jax.ShapeDtypeStruct((batch * seq, 256), BF16),
        grid=(batch, nq),
        in_specs=[pl.BlockSpec((1, 256, tq), lambda b, i: (b, 0, i)),
                  pl.BlockSpec((1, 128, tq), lambda b, i: (b, 0, i)),
                  pl.BlockSpec((1, 8, tq), lambda b, i: (b, 0, i)),
                  pl.BlockSpec((seq, 128), lambda b, i: (b, 0)),
                  pl.BlockSpec((1, 128, seq), lambda b, i: (b, 0, 0)),
                  pl.BlockSpec((seq, 128), lambda b, i: (b, 0))],
        out_specs=pl.BlockSpec((tq, 256), lambda b, i: (b * nq + i, 0)),
        scratch_shapes=[pltpu.VMEM((seq, tq), jnp.int32), pltpu.VMEM((seq, tq), F32),
                        pltpu.VMEM((seq, tq), jnp.int16),
                        pltpu.VMEM((tk, DSA_HEADS * tq), F32), pltpu.VMEM((tk, DSA_HEADS * tq), F32)],
        compiler_params=_params(("parallel", "parallel")),
        name="dsa",
    )(qdt, iqt, iwt, dkv, dkvt, ikw)


def _mem_kv_kernel(mem_ref, g_ref, w_ref, kt_ref, v_ref):
    mn = _rms(mem_ref[0], g_ref[...]).astype(BF16)
    kv = _dot(mn, w_ref[...])
    kt_ref[0] = kv[:, :256].T.astype(BF16)
    v_ref[0] = kv[:, 256:].astype(BF16)


def _mem_kv(mem, gain, w_kv):
    b, m, d = mem.shape
    return pl.pallas_call(
        _mem_kv_kernel,
        out_shape=(jax.ShapeDtypeStruct((b, 256, m), BF16), jax.ShapeDtypeStruct((b, m, 256), BF16)),
        grid=(b,),
        in_specs=[pl.BlockSpec((1, m, d), lambda i: (i, 0, 0)), _const_spec((1, d)), _const_spec((d, 512))],
        out_specs=(pl.BlockSpec((1, 256, m), lambda i: (i, 0, 0)), pl.BlockSpec((1, m, 256), lambda i: (i, 0, 0))),
        compiler_params=_params(("parallel",)),
        name="mem_kv",
    )(mem, gain, w_kv)


def _merge_kernel(x_ref, ya_ref, yb_ref, yc_ref, yd_ref, gmix_ref, wg_ref, wbr_ref, wout_ref,
                  gq_ref, wq_ref, mkt_ref, mv_ref, wo_ref, gffn_ref, wr_ref, br_ref,
                  x2_ref, hp_ref, rt_ref, cnt_ref, run_ref, *, tm):
    x = x_ref[...]
    d = x.shape[-1]
    h = _rms(x, gmix_ref[...]).astype(BF16)
    merged = jnp.zeros((tm, d), F32)
    for n, y_ref in enumerate((ya_ref, yb_ref, yc_ref, yd_ref)):
        wn = wg_ref[:, n * d:n * d + d + LANES][:, GATE_SHIFT:GATE_SHIFT + d]
        gate = _sigmoid(_dot(h, wn))
        merged = merged + gate * _dot(y_ref[...], wbr_ref[n])
    x1 = x + _dot(merged.astype(BF16), wout_ref[...])
    h2 = _rms(x1, gq_ref[...]).astype(BF16)
    q = (_dot(h2, wq_ref[...]) * (MEM_HEAD_DIM ** -0.5)).astype(BF16)
    lane_head = lax.broadcasted_iota(jnp.int32, (tm, 256), 1) // MEM_HEAD_DIM
    mv = mv_ref[0]
    o = jnp.zeros((tm, 256), F32)
    for hd in range(MEM_HEADS):
        s = _dot(q[:, hd * MEM_HEAD_DIM:(hd + 1) * MEM_HEAD_DIM], mkt_ref[0, hd * MEM_HEAD_DIM:(hd + 1) * MEM_HEAD_DIM, :])
        p = jnp.exp(s - jnp.max(s, axis=-1, keepdims=True))
        p = p / jnp.sum(p, axis=-1, keepdims=True)
        o = o + jnp.where(lane_head == hd, _dot(p.astype(BF16), mv), 0.0)
    x2 = x1 + _dot(o.astype(BF16), wo_ref[...])
    x2_ref[...] = x2
    h3f = _rms(x2, gffn_ref[...])
    hp_ref[...] = _pack_bf16_pairs(h3f)
    h3 = h3f.astype(BF16)
    logits = _dot(h3, wr_ref[...]) + br_ref[...]
    lane = lax.broadcasted_iota(jnp.int32, (tm, LANES), 1)
    gl = jnp.where(lane < MOE_GROUPS, logits, -jnp.inf)
    gmax = jnp.max(gl, axis=-1, keepdims=True)
    gsel = jnp.min(jnp.where(gl == gmax, lane, LANES), axis=-1, keepdims=True)
    pg_sel = 1.0 / jnp.sum(jnp.exp(gl - gmax), axis=-1, keepdims=True)
    in_group = (lane - ROUTER_EXPERT_LANE) // MOE_EXPERTS_PER_GROUP == gsel
    el = jnp.where(in_group, logits, -jnp.inf)
    m1 = jnp.max(el, axis=-1, keepdims=True)
    i1 = jnp.min(jnp.where(el == m1, lane, LANES), axis=-1, keepdims=True)
    el2 = jnp.where(lane == i1, -jnp.inf, el)
    m2 = jnp.max(el2, axis=-1, keepdims=True)
    i2 = jnp.min(jnp.where(el2 == m2, lane, LANES), axis=-1, keepdims=True)
    e21 = jnp.exp(m2 - m1)
    c1 = pg_sel / (1.0 + e21)
    @pl.when(pl.program_id(0) == 0)
    def _():
        run_ref[...] = jnp.zeros_like(run_ref)

    oh1 = jnp.where(lane == i1, 1.0, 0.0)
    oh2 = jnp.where(lane == i2, 1.0, 0.0)
    both = oh1 + oh2
    tr = lax.broadcasted_iota(jnp.int32, (tm, tm), 0)
    tc = lax.broadcasted_iota(jnp.int32, (tm, tm), 1)
    before = _dot(jnp.where(tc < tr, 1.0, 0.0).astype(BF16), both.astype(BF16)) + run_ref[0:1, :]
    r1 = jnp.sum(oh1 * before, axis=-1, keepdims=True)
    r2 = jnp.sum(oh2 * before, axis=-1, keepdims=True)
    total = run_ref[...] + jnp.sum(both, axis=0, keepdims=True)
    run_ref[...] = total
    cnt_ref[...] = total
    ids = (jnp.where(lane == 0, i1, i2) - ROUTER_EXPERT_LANE).astype(F32)
    rt_ref[...] = jnp.where(lane < 2, ids, jnp.where(lane == 2, c1, jnp.where(lane == 3, c1 * e21,
                            jnp.where(lane == 4, r1, jnp.where(lane == 5, r2, 0.0)))))


def _merge(x, ys, gmix, wg, wbr, wout, gq, wq, mkt, mv, wo, gffn, wr, br, batch, seq):
    n, d = x.shape
    tm = MERGE_TM
    spt = seq // tm
    m = mv.shape[1]
    tok = lambda w: pl.BlockSpec((tm, w), lambda i: (i, 0))
    return pl.pallas_call(
        functools.partial(_merge_kernel, tm=tm),
        out_shape=(jax.ShapeDtypeStruct((n, d), F32), jax.ShapeDtypeStruct((n, d // 2), F32),
                   jax.ShapeDtypeStruct((n, LANES), F32), jax.ShapeDtypeStruct((8, LANES), F32)),
        grid=(n // tm,),
        in_specs=[tok(d), tok(256), tok(256), tok(256), tok(256),
                  _const_spec((1, d)), _const_spec((d, N_BRANCH * d + LANES)), _const_spec((N_BRANCH, 256, d)),
                  _const_spec((d, d)), _const_spec((1, d)), _const_spec((d, 256)),
                  pl.BlockSpec((1, 256, m), lambda i: (i // spt, 0, 0)),
                  pl.BlockSpec((1, m, 256), lambda i: (i // spt, 0, 0)),
                  _const_spec((256, d)), _const_spec((1, d)), _const_spec((d, LANES)), _const_spec((1, LANES))],
        out_specs=(tok(d), tok(d // 2), tok(LANES), pl.BlockSpec((8, LANES), lambda i: (0, 0))),
        scratch_shapes=[pltpu.VMEM((8, LANES), F32)],
        compiler_params=_params(("arbitrary",)),
        name="merge_mem_router",
    )(x, *ys, gmix, wg, wbr, wout, gq, wq, mkt, mv, wo, gffn, wr, br)


def _sc_gather_rows(table, idx):
    _, width = table.shape
    total = idx.shape[0]
    chunk, nbuf = SC_GATHER_CHUNK, SC_GATHER_BUFS
    workers = SC_CORES * SC_SUBCORES
    per_w = total // workers
    nch = per_w // chunk
    assert total % (workers * chunk * nbuf) == 0
    mesh = plsc.VectorSubcoreMesh(core_axis_name="c", subcore_axis_name="s")

    @functools.partial(
        pl.kernel, mesh=mesh, out_type=jax.ShapeDtypeStruct((total, width), table.dtype),
        scratch_types=[pltpu.VMEM((nch, chunk), jnp.int32), pltpu.VMEM((nbuf, chunk, width), table.dtype),
                       pltpu.SemaphoreType.DMA((nbuf,)), pltpu.SemaphoreType.DMA((nbuf,))])
    def gather_kernel(table_hbm, idx_hbm, out_hbm, idx_v, rows_v, gsem, wsem):
        wid = lax.axis_index("s") * SC_CORES + lax.axis_index("c")
        pltpu.sync_copy(idx_hbm.at[wid], idx_v)

        def gather(j, slot):
            return pltpu.make_async_copy(table_hbm.at[idx_v.at[j]], rows_v.at[slot], gsem.at[slot])

        def write(j, slot):
            off = pl.multiple_of(wid * per_w + j * chunk, chunk)
            return pltpu.make_async_copy(rows_v.at[slot], out_hbm.at[pl.ds(off, chunk)], wsem.at[slot])

        for slot in range(nbuf):
            gather(slot, slot).start()

        @pl.loop(0, nch // nbuf)
        def _(g):
            for slot in range(nbuf):
                j = g * nbuf + slot
                gather(j, slot).wait()
                write(j, slot).start()
                write(j, slot).wait()

                @pl.when(j + nbuf < nch)
                def _():
                    gather(j + nbuf, slot).start()

    return gather_kernel(table, idx.reshape(workers, nch, chunk))


def _sc_scatter_rows(table, dest2, total):
    n, width = table.shape
    chunk, nbuf = SC_GATHER_CHUNK, SC_GATHER_BUFS
    workers = SC_CORES * SC_SUBCORES
    per_w = n // workers
    nch = per_w // chunk
    assert n % (workers * chunk * nbuf) == 0
    mesh = plsc.VectorSubcoreMesh(core_axis_name="c", subcore_axis_name="s")

    @functools.partial(
        pl.kernel, mesh=mesh, out_type=jax.ShapeDtypeStruct((total, width), table.dtype),
        scratch_types=[pltpu.VMEM((2, nch, chunk), jnp.int32), pltpu.VMEM((nbuf, chunk, width), table.dtype),
                       pltpu.SemaphoreType.DMA((nbuf,)), pltpu.SemaphoreType.DMA((nbuf,))])
    def scatter_kernel(table_hbm, idx_hbm, out_hbm, idx_v, rows_v, rsem, wsem):
        wid = lax.axis_index("s") * SC_CORES + lax.axis_index("c")
        pltpu.sync_copy(idx_hbm.at[wid], idx_v)

        def read(j, slot):
            off = pl.multiple_of(wid * per_w + j * chunk, chunk)
            return pltpu.make_async_copy(table_hbm.at[pl.ds(off, chunk)], rows_v.at[slot], rsem.at[slot])

        def write(j, slot, k):
            return pltpu.make_async_copy(rows_v.at[slot], out_hbm.at[idx_v.at[k, j]], wsem.at[slot])

        for slot in range(nbuf):
            read(slot, slot).start()

        @pl.loop(0, nch // nbuf)
        def _(g):
            for slot in range(nbuf):
                j = g * nbuf + slot
                read(j, slot).wait()
                write(j, slot, 0).start()
                write(j, slot, 1).start()
                write(j, slot, 0).wait()
                write(j, slot, 1).wait()

                @pl.when(j + nbuf < nch)
                def _():
                    read(j + nbuf, slot).start()

    idx = dest2.reshape(2, workers, nch, chunk).transpose(1, 0, 2, 3)
    return scatter_kernel(table, idx)


def _dispatch_plan(rt, cnt, n):
    ne, blk = MOE_N_EXPERTS, MOE_BLOCK
    n_blocks = (2 * n) // blk + ne
    experts = jnp.arange(ne, dtype=jnp.int32)
    counts = cnt[0, ROUTER_EXPERT_LANE:ROUTER_EXPERT_LANE + ne].astype(jnp.int32)
    padded = (counts + blk - 1) // blk * blk
    pend = jnp.cumsum(padded)
    pstart = pend - padded
    ids = rt[:, 0:2].astype(jnp.int32)
    pos = rt[:, 4:6].astype(jnp.int32)
    first_row = jnp.sum(jnp.where(ids[:, :, None] == experts[None, None, :], pstart[None, None, :], 0), axis=-1)
    dest2 = (first_row + pos).T
    b0 = jnp.arange(n_blocks, dtype=jnp.int32) * blk
    block_e = jnp.minimum(jnp.sum((pend[None, :] <= b0[:, None]).astype(jnp.int32), axis=1), ne - 1)
    n_valid = jnp.clip(counts[block_e] - (b0 - pstart[block_e]), 0, blk).astype(jnp.int32)
    return dest2, block_e, n_valid


def _expert_block_kernel(be_ref, nv_ref, xs_ref, wg_ref, wu_ref, wd_ref, o_ref, wgb_ref, wub_ref, wdb_ref):
    b = pl.program_id(0)
    valid = nv_ref[b]

    @pl.when((b == 0) | (be_ref[b] != be_ref[jnp.maximum(b - 1, 0)]))
    def _():
        wgb_ref[...] = wg_ref[0].astype(BF16)
        wub_ref[...] = wu_ref[0].astype(BF16)
        wdb_ref[...] = wd_ref[0].astype(BF16)

    @pl.when(valid > 0)
    def _():
        row = lax.broadcasted_iota(jnp.int32, xs_ref.shape, 0)
        words = jnp.where(row < valid, xs_ref[...], 0.0)
        h = _unpack_bf16_pairs(words).astype(BF16)
        gt = _dot(h, wgb_ref[...])
        hid = gt * _sigmoid(gt) * _dot(h, wub_ref[...])
        o_ref[...] = _pack_bf16_pairs(_dot(hid.astype(BF16), wdb_ref[...]))

    @pl.when(valid == 0)
    def _():
        o_ref[...] = jnp.zeros_like(o_ref)


def _expert_blocks(xs, wg, wu, wd, layer, block_e, n_used):
    p_rows, half = xs.shape
    d, hid = wg.shape[-2:]
    blk = MOE_BLOCK
    grid_spec = pltpu.PrefetchScalarGridSpec(
        num_scalar_prefetch=2, grid=(p_rows // blk,),
        in_specs=[pl.BlockSpec((blk, half), lambda b, be, nu: (b, 0)),
                  pl.BlockSpec((None, 1, d, hid), lambda b, be, nu: (layer, be[b], 0, 0)),
                  pl.BlockSpec((None, 1, d, hid), lambda b, be, nu: (layer, be[b], 0, 0)),
                  pl.BlockSpec((None, 1, hid, d), lambda b, be, nu: (layer, be[b], 0, 0))],
        out_specs=pl.BlockSpec((blk, half), lambda b, be, nu: (b, 0)),
        scratch_shapes=[pltpu.VMEM((d, hid), BF16), pltpu.VMEM((d, hid), BF16), pltpu.VMEM((hid, d), BF16)])
    return pl.pallas_call(
        _expert_block_kernel, out_shape=jax.ShapeDtypeStruct((p_rows, half), F32), grid_spec=grid_spec,
        compiler_params=_params(("arbitrary",)),
        name="moe_expert_blocks",
    )(block_e, n_used, xs, wg, wu, wd)


def _combine_kernel(x_ref, y1_ref, y2_ref, rt_ref, gfin_ref, o_ref, *, final_norm):
    out = (x_ref[...] + rt_ref[:, 2:3] * _unpack_bf16_pairs(y1_ref[...])
           + rt_ref[:, 3:4] * _unpack_bf16_pairs(y2_ref[...]))
    o_ref[...] = _rms(out, gfin_ref[...]) if final_norm else out


def _combine(x2, y_halves, rt, gfin, final_norm):
    n, d = x2.shape
    tm = COMBINE_TM
    nt = n // tm
    return pl.pallas_call(
        functools.partial(_combine_kernel, final_norm=final_norm),
        out_shape=jax.ShapeDtypeStruct((n, d), F32),
        grid=(nt,),
        in_specs=[pl.BlockSpec((tm, d), lambda i: (i, 0)),
                  pl.BlockSpec((tm, d // 2), lambda i: (i, 0)),
                  pl.BlockSpec((tm, d // 2), lambda i: (i + nt, 0)),
                  pl.BlockSpec((tm, LANES), lambda i: (i, 0)), _const_spec((1, d))],
        out_specs=pl.BlockSpec((tm, d), lambda i: (i, 0)),
        compiler_params=_params(("parallel",)), name="moe_combine",
    )(x2, y_halves, y_halves, rt, gfin)


def _moe(x2, hp, rt, cnt, wg, wu, wd, layer, gfin, final_norm):
    n = x2.shape[0]
    dest2, block_e, n_valid = _dispatch_plan(rt, cnt, n)
    xs = _sc_scatter_rows(hp, dest2, block_e.shape[0] * MOE_BLOCK)
    yb = _expert_blocks(xs, wg, wu, wd, layer, block_e, n_valid)
    y_halves = _sc_gather_rows(yb, dest2.reshape(2 * n))
    return _combine(x2, y_halves, rt, gfin, final_norm)


def kernel(x, mem, positions, norm_mix, w_in, diff_lambda, hgrn_lb_logits, spatial_w, spatial_b, w_branch, w_out,
           norm_mem_q, norm_mem_kv, w_mem_q, w_mem_kv, w_mem_o, norm_ffn, w_router_group, b_router_group,
           w_router_expert, b_router_expert, w_exp_gate, w_exp_up, w_exp_down, norm_final):
    batch, seq, d = x.shape
    depth = w_in.shape[0]
    n = batch * seq
    xf = x.reshape(n, d)
    tabs = _rope_tables(positions)
    w_all = _w_in_bf16(w_in)
    row = lambda v: v.reshape(1, -1).astype(F32)
    for l in range(depth):
        lam_init = 0.8 - 0.6 * math.exp(-0.3 * l)
        g0 = C_GATE - GATE_SHIFT
        w_gate = lax.slice(w_all, (l, 0, g0), (l + 1, d, g0 + N_BRANCH * d + LANES)).reshape(d, -1)
        sw = spatial_w[l].reshape(SGU_GROUPS * SGU_CHUNK, SGU_CHUNK)
        sb = jnp.repeat(spatial_b[l].T, SGU_GROUP_DIM, axis=1)
        qat, ka, vat, hb, y_c, qdt, iqt, dkv, dkvt, ikw, iwt = _projection(
            xf, row(norm_mix[l]), w_all, l, tabs, sw, sb, batch, seq)
        y_a = _diff_attention(diff_lambda[l], qat, ka, vat, lam_init, batch, seq)
        y_b = _hgrn(hgrn_lb_logits, hb, l, batch, seq)
        y_d = _dsa(qdt, iqt, iwt, dkv, dkvt, ikw, batch, seq)
        mkt, mv = _mem_kv(mem, row(norm_mem_kv[l]), w_mem_kv[l].astype(BF16))
        e0, e1 = ROUTER_EXPERT_LANE, ROUTER_EXPERT_LANE + MOE_N_EXPERTS
        wr = jnp.zeros((d, LANES), F32)
        wr = wr.at[:, :MOE_GROUPS].set(w_router_group[l]).at[:, e0:e1].set(w_router_expert[l]).astype(BF16)
        br = jnp.zeros((1, LANES), F32)
        br = br.at[0, :MOE_GROUPS].set(b_router_group[l]).at[0, e0:e1].set(b_router_expert[l])
        x2, hp, rt, cnt = _merge(xf, (y_a, y_b, y_c, y_d), row(norm_mix[l]), w_gate, w_branch[l].astype(BF16),
                                 w_out[l].astype(BF16), row(norm_mem_q[l]), w_mem_q[l].astype(BF16), mkt, mv,
                                 w_mem_o[l].astype(BF16), row(norm_ffn[l]), wr, br, batch, seq)
        xf = _moe(x2, hp, rt, cnt, w_exp_gate, w_exp_up, w_exp_down, l,
                  row(norm_final), final_norm=(l == depth - 1))
    return xf.reshape(batch, seq, d)
```

```python
import functools
import math

import numpy as np
import jax
import jax.numpy as jnp
from jax import lax
from jax.experimental import pallas as pl
from jax.experimental.pallas import tpu as pltpu
from jax.experimental.pallas import tpu_sc as plsc

F32 = jnp.float32
BF16 = jnp.bfloat16

NORM_EPS = 1e-6
ROPE_THETA = 10000.0
NEG_BIG = -1e30

N_BRANCH = 4
DIFF_HEADS = 4
DIFF_HEAD_DIM = 32
HGRN_DIM = 64
HGRN_CHUNK = 32
HGRN_UNROLL = 8
HGRN_MIN_FORGET = 1e-30
SGU_GROUPS = 4
SGU_GROUP_DIM = 64
SGU_CHUNK = 128
DSA_HEADS = 4
DSA_HEAD_DIM = 64
DSA_IDX_HEADS = 4
DSA_IDX_DIM = 32
DSA_TOPK = 256
MEM_HEADS = 4
MEM_HEAD_DIM = 64
MOE_GROUPS = 4
MOE_EXPERTS_PER_GROUP = 8
MOE_N_EXPERTS = 32
MOE_BLOCK = 512
ROUTER_EXPERT_LANE = 32
SC_CORES = 2
SC_SUBCORES = 16
SC_GATHER_CHUNK = 16
SC_GATHER_BUFS = 4

LANES = 128
VMEM_LIMIT = 56 * 1024 * 1024

PROJ_TM = 1024
DIFF_TQ = 512
DIFF_TK = 512
HGRN_TC = 1024
DSA_TQ = 512
DSA_TK = 512
MERGE_TM = 1024
COMBINE_TM = 1024
ROPE_TM = 1024

C_AQ, C_AK, C_AV = 0, 256, 512
C_HB = 768
C_UV = 1792
C_DQ = 2304
C_DKV = 2560
C_IQ = 2688
C_IKW = 2816
IW_LANE = 32
C_GATE = 2852
GATE_SHIFT = C_GATE % 128
C_TOTAL = 2944
LOG2E = math.log2(math.e)


def _params(sem):
    return pltpu.CompilerParams(dimension_semantics=sem, vmem_limit_bytes=VMEM_LIMIT)


def _const_spec(shape):
    nd = len(shape)
    return pl.BlockSpec(shape, lambda *_: (0,) * nd, pipeline_mode=pl.Buffered(1))


def _rms(xf, gain=None):
    y = xf * lax.rsqrt(jnp.mean(xf * xf, axis=-1, keepdims=True) + NORM_EPS)
    return y if gain is None else y * gain


def _sigmoid(x):
    return 0.5 * jnp.tanh(0.5 * x) + 0.5


def _pack_bf16_pairs(x):
    w = x.shape[-1] // 2
    xb = x.astype(BF16).astype(F32)
    lo = lax.shift_right_logical(pltpu.bitcast(xb[:, :w], jnp.int32), 16)
    hi = pltpu.bitcast(xb[:, w:], jnp.int32) & jnp.int32(-65536)
    return pltpu.bitcast(hi | lo, F32)


def _unpack_bf16_pairs(words):
    bits = pltpu.bitcast(words, jnp.int32)
    lo = pltpu.bitcast(bits << 16, F32)
    hi = pltpu.bitcast(bits & jnp.int32(-65536), F32)
    return jnp.concatenate([lo, hi], axis=1)


def _dot(a, b):
    return jnp.dot(a, b, preferred_element_type=F32)


def _dot_nt(a, b):
    return lax.dot_general(a, b, (((1,), (1,)), ((), ())), preferred_element_type=F32)


def _rope_table_kernel(pos_ref, frq_ref, sgn_ref, c32_ref, s32_ref, c64_ref, s64_ref):
    ang = pos_ref[...].astype(F32) * frq_ref[...]
    cos = jnp.cos(ang)
    sin = jnp.sin(ang) * sgn_ref[...]
    four = lambda t: jnp.concatenate([t, t, t, t], axis=1)
    c64_ref[...] = four(cos[:, :64])
    s64_ref[...] = four(sin[:, :64])
    c32_ref[...] = four(cos[:, 64:])
    s32_ref[...] = four(sin[:, 64:])


def _rope_tables(positions):
    n = positions.size
    pos = positions.reshape(n, 1).astype(jnp.int32)
    lane = np.arange(64)
    inv32 = ROPE_THETA ** (-jnp.arange(16, dtype=F32) * (2.0 / 32))
    inv64 = ROPE_THETA ** (-jnp.arange(32, dtype=F32) * (2.0 / 64))
    frq = jnp.concatenate([inv64[lane % 32], inv32[lane % 16]]).reshape(1, LANES)
    sgn = jnp.asarray(np.concatenate([np.where(lane % 64 < 32, -1.0, 1.0),
                                      np.where(lane % 32 < 16, -1.0, 1.0)]), F32).reshape(1, LANES)
    tm = ROPE_TM
    tab = jax.ShapeDtypeStruct((n, 256), F32)
    return pl.pallas_call(
        _rope_table_kernel,
        out_shape=(tab, tab, tab, tab),
        grid=(n // tm,),
        in_specs=[pl.BlockSpec((tm, 1), lambda i: (i, 0)), _const_spec((1, LANES)), _const_spec((1, LANES))],
        out_specs=tuple(pl.BlockSpec((tm, 256), lambda i: (i, 0)) for _ in range(4)),
        compiler_params=_params(("parallel",)),
        name="rope_tables",
    )(pos, frq, sgn)


def _w_in_kernel(wt_ref, o_ref, *, width):
    row = lax.broadcasted_iota(jnp.int32, (LANES, wt_ref.shape[-1]), 0)
    valid = row < width - pl.program_id(0) * LANES
    for layer in range(o_ref.shape[0]):
        o_ref[layer] = jnp.where(valid, wt_ref[:, layer, :], 0.0).T.astype(BF16)


def _w_in_bf16(w_in):
    depth, d, width = w_in.shape
    nblk = pl.cdiv(width, LANES)
    return pl.pallas_call(
        functools.partial(_w_in_kernel, width=width), out_shape=jax.ShapeDtypeStruct((depth, d, nblk * LANES), BF16),
        grid=(nblk,), in_specs=[pl.BlockSpec((LANES, depth, d), lambda i: (i, 0, 0))],
        out_specs=pl.BlockSpec((depth, d, LANES), lambda i: (0, 0, i)),
        compiler_params=_params(("parallel",)), name="w_in_bf16",
    )(jnp.transpose(w_in, (2, 0, 1)))


def _gelu_tanh(x):
    return 0.5 * x * (1.0 + jnp.tanh(math.sqrt(2.0 / math.pi) * (x + 0.044715 * (x * x * x))))


def _rope(x, cos, sin_signed, half):
    w = x.shape[-1]
    lane = lax.broadcasted_iota(jnp.int32, x.shape, 1)
    partner = jnp.where(lane % (2 * half) < half, pltpu.roll(x, w - half, 1), pltpu.roll(x, half, 1))
    return x * cos + partner * sin_signed


def _proj_kernel(x_ref, g_ref, w_ref, c32_ref, s32_ref, c64_ref, s64_ref, sw_ref, sb_ref,
                 qat_ref, ka_ref, vat_ref, hb_ref, yc_ref, qdt_ref, iqt_ref, dkv_ref, dkvt_ref, ikw_ref, iwt_ref,
                 *, tm):
    h = _rms(x_ref[...], g_ref[...]).astype(BF16)

    def proj(c0, width):
        return _dot(h, w_ref[:, c0:c0 + width])

    c32, s32, c64, s64 = c32_ref[...], s32_ref[...], c64_ref[...], s64_ref[...]
    qat_ref[0] = (_rope(proj(C_AQ, 256), c32, s32, 16) * (DIFF_HEAD_DIM ** -0.5 * LOG2E)).T.astype(BF16)
    ka_ref[...] = _rope(proj(C_AK, 256), c32, s32, 16).astype(BF16)
    vat_ref[0] = proj(C_AV, 256).astype(BF16).T
    hb_ref[...] = proj(C_HB, 1024)
    qdt_ref[0] = (_rope(proj(C_DQ, 256), c64, s64, 32) * (DSA_HEAD_DIM ** -0.5 * LOG2E)).T.astype(BF16)
    iqt_ref[0] = _rope(proj(C_IQ, 128), c32[:, :128], s32[:, :128], 16).T.astype(BF16)
    lane = lax.broadcasted_iota(jnp.int32, (tm, 128), 1)
    is_k = lane < DSA_HEAD_DIM
    dkv = _rope(proj(C_DKV, 128), jnp.where(is_k, c64[:, :128], 1.0), jnp.where(is_k, s64[:, :128], 0.0), 32)
    is_ik = lane < DSA_IDX_DIM
    ikw = _rope(proj(C_IKW, 128), jnp.where(is_ik, c32[:, :128], 1.0), jnp.where(is_ik, s32[:, :128], 0.0), 16)
    dkv_ref[...] = dkv.astype(BF16)
    dkvt_ref[0] = dkv.T.astype(BF16)
    ikw_ref[...] = ikw.astype(BF16)
    iw_scale = DSA_IDX_HEADS ** -0.5 * DSA_IDX_DIM ** -0.5
    iwt_ref[0] = (ikw * iw_scale).T[IW_LANE:IW_LANE + 8, :]
    uv = _gelu_tanh(proj(C_UV, 512))
    u, v = uv[:, :256], uv[:, 256:]
    mu = jnp.mean(v, axis=-1, keepdims=True)
    vc = v - mu
    vn = (vc * lax.rsqrt(jnp.mean(vc * vc, axis=-1, keepdims=True) + NORM_EPS)).astype(BF16)
    r = lax.broadcasted_iota(jnp.int32, (SGU_GROUPS * SGU_CHUNK, SGU_CHUNK), 0)
    c = lax.broadcasted_iota(jnp.int32, (SGU_GROUPS * SGU_CHUNK, SGU_CHUNK), 1)
    wt = jnp.where((r % SGU_CHUNK) >= c, sw_ref[...], 0.0).astype(BF16)
    lane_grp = lax.broadcasted_iota(jnp.int32, (SGU_CHUNK, 256), 1) // SGU_GROUP_DIM
    for ch in range(tm // SGU_CHUNK):
        r0 = ch * SGU_CHUNK
        full = _dot(wt, vn[r0:r0 + SGU_CHUNK, :])
        mixed = sb_ref[...]
        for g in range(SGU_GROUPS):
            mixed = mixed + jnp.where(lane_grp == g, full[g * SGU_CHUNK:(g + 1) * SGU_CHUNK, :], 0.0)
        yc_ref[r0:r0 + SGU_CHUNK, :] = (u[r0:r0 + SGU_CHUNK, :] * mixed).astype(BF16)


def _projection(x, gain, w_all, layer, tabs, sw, sb, batch, seq):
    n, d = x.shape
    tm = PROJ_TM
    spt = seq // tm
    tok = lambda w: pl.BlockSpec((tm, w), lambda i: (i, 0))
    tr = lambda rows: pl.BlockSpec((1, rows, tm), lambda i: (i // spt, 0, i % spt))
    out_shape = (
        jax.ShapeDtypeStruct((batch, 256, seq), BF16),
        jax.ShapeDtypeStruct((n, 256), BF16),
        jax.ShapeDtypeStruct((batch, 256, seq), BF16),
        jax.ShapeDtypeStruct((n, 1024), F32),
        jax.ShapeDtypeStruct((n, 256), BF16),
        jax.ShapeDtypeStruct((batch, 256, seq), BF16),
        jax.ShapeDtypeStruct((batch, 128, seq), BF16),
        jax.ShapeDtypeStruct((n, 128), BF16),
        jax.ShapeDtypeStruct((batch, 128, seq), BF16),
        jax.ShapeDtypeStruct((n, 128), BF16),
        jax.ShapeDtypeStruct((batch, 8, seq), F32),
    )
    return pl.pallas_call(
        functools.partial(_proj_kernel, tm=tm),
        out_shape=out_shape,
        grid=(n // tm,),
        in_specs=[tok(d), _const_spec((1, d)),
                  pl.BlockSpec((None, d, C_TOTAL), lambda i: (layer, 0, 0), pipeline_mode=pl.Buffered(1)),
                  tok(256), tok(256), tok(256), tok(256),
                  _const_spec((SGU_GROUPS * SGU_CHUNK, SGU_CHUNK)), _const_spec((SGU_CHUNK, 256))],
        out_specs=(tr(256), tok(256), tr(256), tok(1024), tok(256), tr(256), tr(128), tok(128), tr(128), tok(128), tr(8)),
        compiler_params=_params(("parallel",)),
        name="projection",
    )(x, gain, w_all, *tabs, sw, sb)


def _diff_attn_kernel(lam_ref, qt_ref, k_ref, vt_ref, o_ref, sa_ref, sb_ref, *, lam_init, tq, tk):
    q0 = pl.program_id(1) * tq
    kb_diag = q0 // tk
    lv = lam_ref[...]
    lam = (jnp.exp(jnp.sum(lv[0:1] * lv[1:2], axis=-1, keepdims=True))
           - jnp.exp(jnp.sum(lv[2:3] * lv[3:4], axis=-1, keepdims=True)) + lam_init)
    qt = qt_ref[0]
    feat = lax.broadcasted_iota(jnp.int32, (256, tq), 0) // DIFF_HEAD_DIM
    n_maps = 2 * DIFF_HEADS
    qz = jnp.concatenate([jnp.where(feat == i, qt, jnp.zeros_like(qt)) for i in range(n_maps)], axis=1)
    wide = n_maps * tq
    key_i = lax.broadcasted_iota(jnp.int32, (tk, wide), 0)
    qry_i = q0 + lax.broadcasted_iota(jnp.int32, (tk, wide), 1) % tq

    def scores(kb, dst_ref):
        k0 = pl.multiple_of(kb * tk, tk)
        dst_ref[...] = _dot(k_ref[pl.ds(k0, tk), :], qz)

    def absorb(src_ref, kb, carry, masked):
        m_i, l_i, acc = carry
        k0 = pl.multiple_of(kb * tk, tk)
        s = src_ref[...]
        if masked:
            s = jnp.where(k0 + key_i <= qry_i, s, NEG_BIG)
        m_new = jnp.maximum(m_i, jnp.max(s, axis=0, keepdims=True))
        p = jnp.exp2(s - m_new)
        alpha = jnp.exp2(m_i - m_new)
        l_new = alpha * l_i + jnp.sum(p, axis=0, keepdims=True)
        pb = p.astype(BF16)
        pv = jnp.concatenate(
            [_dot(vt_ref[0, hd * 64:(hd + 1) * 64, pl.ds(k0, tk)], pb[:, 2 * hd * tq:(2 * hd + 2) * tq])
             for hd in range(DIFF_HEADS)], axis=1)
        return m_new, l_new, alpha * acc + pv

    def pair(j, carry):
        kb = 2 * j
        scores(kb + 1, sb_ref)
        carry = absorb(sa_ref, kb, carry, False)
        scores(kb + 2, sa_ref)
        return absorb(sb_ref, kb + 1, carry, False)

    init = (jnp.full((1, wide), NEG_BIG, F32), jnp.zeros((1, wide), F32), jnp.zeros((64, wide), F32))
    scores(0, sa_ref)
    carry = lax.fori_loop(0, kb_diag // 2, pair, init)

    def tail_odd(carry):
        scores(kb_diag, sb_ref)
        carry = absorb(sa_ref, kb_diag - 1, carry, False)
        return absorb(sb_ref, kb_diag, carry, True)

    def tail_even(carry):
        return absorb(sa_ref, kb_diag, carry, True)

    _, l_f, acc = lax.cond(kb_diag % 2 == 1, tail_odd, tail_even, carry)
    o_all = acc / l_f
    heads = []
    for hd in range(DIFF_HEADS):
        o0 = o_all[:, 2 * hd * tq:(2 * hd + 1) * tq]
        o1 = o_all[:, (2 * hd + 1) * tq:(2 * hd + 2) * tq]
        o_h = o0 - lam * o1
        ms = jnp.mean(o_h * o_h, axis=0, keepdims=True)
        heads.append(o_h * lax.rsqrt(ms + NORM_EPS) * (1.0 - lam_init))
    o_ref[...] = jnp.concatenate(heads, axis=0).T.astype(BF16)


def _diff_attention(lam_vec, qat, ka, vat, lam_init, batch, seq):
    tq, tk = DIFF_TQ, DIFF_TK
    nq = seq // tq
    return pl.pallas_call(
        functools.partial(_diff_attn_kernel, lam_init=lam_init, tq=tq, tk=tk),
        out_shape=jax.ShapeDtypeStruct((batch * seq, 256), BF16),
        grid=(batch, nq),
        in_specs=[_const_spec((4, DIFF_HEAD_DIM)),
                  pl.BlockSpec((1, 256, tq), lambda b, i: (b, 0, i)),
                  pl.BlockSpec((seq, 256), lambda b, i: (b, 0)),
                  pl.BlockSpec((1, 256, seq), lambda b, i: (b, 0, 0))],
        out_specs=pl.BlockSpec((tq, 256), lambda b, i: (b * nq + i, 0)),
        scratch_shapes=[pltpu.VMEM((tk, 2 * DIFF_HEADS * tq), F32), pltpu.VMEM((tk, 2 * DIFF_HEADS * tq), F32)],
        compiler_params=_params(("parallel", "parallel")),
        name="diff_attention",
    )(lam_vec, qat, ka, vat)


def _hgrn_kernel(lbl_ref, hb_ref, o_ref, st_ref, pstk_ref, *, layer, tc):
    cz = HGRN_CHUNK
    w = 256

    @pl.when(pl.program_id(1) == 0)
    def _():
        st_ref[...] = jnp.zeros_like(st_ref)

    lg = lbl_ref[...]
    e = jnp.exp(lg - jnp.max(lg, axis=0, keepdims=True))
    lw = e / jnp.sum(e, axis=0, keepdims=True)
    lb = jnp.sum(lw[0:layer + 1], axis=0, keepdims=True) - lw[0:1]

    scan_row = lax.broadcasted_iota(jnp.int32, (cz, w), 0)
    rb = lax.broadcasted_iota(jnp.int32, (w, w), 0) // HGRN_DIM
    cb = lax.broadcasted_iota(jnp.int32, (w, w), 1) // HGRN_DIM
    same_head = rb == cb
    head_ones = same_head.astype(BF16)
    trows = {r: r + lax.broadcasted_iota(jnp.int32, (16, w), 0) for r in range(0, cz, 16)}

    def chunk(c, carry):
        r0 = pl.multiple_of(c * cz, cz)
        q = hb_ref[pl.ds(r0, cz), 0:256]
        fp = hb_ref[pl.ds(r0, cz), 256:512]
        v = hb_ref[pl.ds(r0, cz), 512:768]
        g = hb_ref[pl.ds(r0, cz), 768:1024]
        qf = q * _sigmoid(q)
        f = lb + (1.0 - lb) * jax.nn.sigmoid(fp)
        log_f = jnp.log(jnp.maximum(f, HGRN_MIN_FORGET))
        kf = (1.0 - lb) * jax.nn.sigmoid(-fp)
        bc = log_f
        step = 1
        while step < cz:
            bc = bc + jnp.where(scan_row >= step, pltpu.roll(bc, step, 0), 0.0)
            step *= 2
        st = st_ref[...]
        o = _dot_nt((qf * jnp.exp(bc)).astype(BF16), st.astype(BF16))
        later = {}
        for blk in range(1, cz // 16):
            r = bc[16 * blk - 1:16 * blk, :]
            later[blk] = (qf[16 * blk:16 * blk + 16, :] * jnp.exp(bc[16 * blk:16 * blk + 16, :] - r),
                          kf[:16 * blk, :] * jnp.exp(r - bc[:16 * blk, :]))
        for s in range(cz):
            r_lo = (s // 16) * 16
            arg = bc[r_lo:r_lo + 16, :] - bc[s:s + 1, :]
            if s > r_lo:
                arg = jnp.where(trows[r_lo] >= s, arg, NEG_BIG)
            p = qf[r_lo:r_lo + 16, :] * kf[s:s + 1, :] * jnp.exp(arg)
            if r_lo:
                pstk_ref[s * cz:s * cz + r_lo, :] = jnp.zeros((r_lo, w), BF16)
            pstk_ref[s * cz + r_lo:s * cz + r_lo + 16, :] = p.astype(BF16)
            for blk in range(s // 16 + 1, cz // 16):
                q_dec, k_dec = later[blk]
                pstk_ref[s * cz + 16 * blk:s * cz + 16 * blk + 16, :] = (q_dec * k_dec[s:s + 1, :]).astype(BF16)
        accs = [jnp.zeros((16, w), F32) for _ in range(cz // 16)]
        for sg in range(cz // 16):
            att = _dot(pstk_ref[sg * 16 * cz:(sg + 1) * 16 * cz, :], head_ones)
            for sl in range(16):
                s = sg * 16 + sl
                for j in range(sg, cz // 16):
                    accs[j] = accs[j] + att[sl * cz + 16 * j:sl * cz + 16 * j + 16, :] * v[s:s + 1, :]
        o = o + jnp.concatenate(accs, axis=0)
        b_end = bc[cz - 1:cz, :]
        kd = kf * jnp.exp(b_end - bc)
        upd = _dot(v.T.astype(BF16), kd.astype(BF16))
        st_ref[...] = st * jnp.exp(b_end) + jnp.where(same_head, upd, 0.0)
        ms = _dot(o * o, head_ones.astype(F32)) * (1.0 / HGRN_DIM)
        y = o * lax.rsqrt(ms + NORM_EPS)
        o_ref[pl.ds(r0, cz), :] = (y * (g * _sigmoid(g))).astype(BF16)
        return carry

    def group(gi, carry):
        for u in range(HGRN_UNROLL):
            chunk(gi * HGRN_UNROLL + u, carry)
        return carry

    lax.fori_loop(0, tc // cz // HGRN_UNROLL, group, 0)


def _hgrn(lb_logits, hb, layer, batch, seq):
    tc = HGRN_TC
    nt = seq // tc
    cz = HGRN_CHUNK
    return pl.pallas_call(
        functools.partial(_hgrn_kernel, layer=layer, tc=tc),
        out_shape=jax.ShapeDtypeStruct((batch * seq, 256), BF16),
        grid=(batch, nt),
        in_specs=[_const_spec(lb_logits.shape),
                  pl.BlockSpec((tc, 1024), lambda b, i: (b * nt + i, 0))],
        out_specs=pl.BlockSpec((tc, 256), lambda b, i: (b * nt + i, 0)),
        scratch_shapes=[pltpu.VMEM((256, 256), F32), pltpu.VMEM((cz * cz, 256), BF16)],
        compiler_params=_params(("parallel", "arbitrary")),
        name="hgrn2",
    )(lb_logits, hb)


def _dsa_kernel(qdt_ref, iqt_ref, iwt_ref, dkv_ref, dkvt_ref, ikw_ref, o_ref, key_ref, bias_ref, half_ref,
                sa_ref, sb_ref, *, tq, tk, n_sel):
    q0 = pl.program_id(1) * tq
    nkb = q0 // tk + 1
    key_i = lax.broadcasted_iota(jnp.int32, (tk, tq), 0)
    qry_i = q0 + lax.broadcasted_iota(jnp.int32, (tk, tq), 1)
    grp = tk // 8
    rows8 = lambda x: x.reshape(grp, 8, tq)
    iqt = iqt_ref[0]
    zpad = jnp.zeros((LANES - DSA_IDX_DIM, tq), BF16)
    iqz = jnp.concatenate([jnp.concatenate([iqt[hd * DSA_IDX_DIM:(hd + 1) * DSA_IDX_DIM, :], zpad], axis=0)
                           for hd in range(DSA_IDX_HEADS)], axis=1)
    iw = iwt_ref[0]

    last = nkb - 1

    def score_block(kb, carry, masked):
        k0 = pl.multiple_of(kb * tk, tk)
        sh = jnp.maximum(_dot(ikw_ref[pl.ds(k0, tk), :], iqz), 0.0)
        sc = jnp.zeros((tk, tq), F32)
        for hd in range(DSA_IDX_HEADS):
            sc = sc + sh[:, hd * tq:(hd + 1) * tq] * iw[hd:hd + 1, :]
        sc = sc + 0.0
        if masked:
            sc = jnp.where(k0 + key_i <= qry_i, sc, -jnp.inf)
        bits = pltpu.bitcast(sc, jnp.int32)
        key = jnp.where(bits < 0, bits ^ jnp.int32(0x7FFFFFFF), bits)
        key_ref[pl.ds(k0, tk), :] = key
        half_ref[pl.ds(k0, tk), :] = (key >> 16).astype(jnp.int16)
        return carry

    lax.fori_loop(0, last, functools.partial(score_block, masked=False), 0)
    score_block(last, 0, True)

    one16, zero16 = jnp.ones((), jnp.int16), jnp.zeros((), jnp.int16)
    low16 = np.int16(-2 ** 15)

    def count16(limit, strict):
        def body(kb, acc):
            k0 = pl.multiple_of(kb * tk, tk)
            for c in range(tk // 128):
                blk = half_ref[pl.ds(k0 + 128 * c, 128), :].reshape(8, 16, tq)
                hit = jnp.where(blk > limit if strict else blk >= limit, one16, zero16)
                parts = [hit[j] for j in range(8)]
                while len(parts) > 1:
                    parts = [a + b for a, b in zip(parts[0::2], parts[1::2])]
                acc = acc + parts[0]
            return acc
        acc = lax.fori_loop(0, nkb, body, jnp.zeros((16, tq), jnp.int16))
        return jnp.broadcast_to(jnp.sum(acc.astype(jnp.int32), axis=0, keepdims=True), (16, tq))

    def search16(need):
        t = jnp.full((16, tq), -2 ** 15, jnp.int32)
        for bit in range(15, -1, -1):
            trial = t + 2 ** bit
            t = jnp.where(count16(trial.astype(jnp.int16), False) >= need, trial, t)
        return t

    t_hi = search16(n_sel)
    t_hi16 = t_hi.astype(jnp.int16)
    need_lo = n_sel - count16(t_hi16, True)

    def low_block(kb, carry):
        k0 = pl.multiple_of(kb * tk, tk)
        lo = ((key_ref[pl.ds(k0, tk), :] & 0xFFFF) - 2 ** 15).astype(jnp.int16).reshape(tk // 16, 16, tq)
        hi = half_ref[pl.ds(k0, tk), :].reshape(tk // 16, 16, tq)
        half_ref[pl.ds(k0, tk), :] = jnp.where(hi == t_hi16, lo, low16).reshape(tk, tq)
        return carry

    lax.fori_loop(0, nkb, low_block, 0)
    t_lo = search16(need_lo)
    thr = ((t_hi << 16) | (t_lo + 2 ** 15))[0:8, :]

    need = (need_lo - count16(t_lo.astype(jnp.int16), True))[0:8, :].astype(F32)
    ur = lax.broadcasted_iota(jnp.int32, (LANES, LANES), 0)
    uc = lax.broadcasted_iota(jnp.int32, (LANES, LANES), 1)
    earlier = (uc < ur).astype(BF16)
    ones8 = jnp.ones((8, LANES), BF16)
    key_s = lax.broadcasted_iota(jnp.int32, (LANES, tq), 0)
    qry_s = q0 + lax.broadcasted_iota(jnp.int32, (LANES, tq), 1)

    def select_block(kb, seen, masked):
        k0 = pl.multiple_of(kb * tk, tk)
        for g in range(tk // LANES):
            r0 = g * LANES
            blk = key_ref[pl.ds(k0 + r0, LANES), :].reshape(LANES // 8, 8, tq)
            eq = blk == thr[None]
            eqb = jnp.where(eq, 1.0, 0.0).reshape(LANES, tq).astype(BF16)
            rank = _dot(earlier, eqb).reshape(LANES // 8, 8, tq) + seen[None]
            tie_bias = jnp.where(rank < need[None], 0.0, NEG_BIG)
            bias = jnp.where(blk > thr[None], 0.0, jnp.where(eq, tie_bias, NEG_BIG)).reshape(LANES, tq)
            if masked:
                bias = jnp.where(k0 + r0 + key_s <= qry_s, bias, NEG_BIG)
            bias_ref[pl.ds(k0 + r0, LANES), :] = bias
            seen = seen + _dot(ones8, eqb)
        return seen

    seen = lax.fori_loop(0, last, functools.partial(select_block, masked=False), jnp.zeros((8, tq), F32))
    select_block(last, seen, True)

    qdt = qdt_ref[0]
    zq = jnp.zeros((LANES - DSA_HEAD_DIM, tq), BF16)
    qz = jnp.concatenate([jnp.concatenate([qdt[hd * DSA_HEAD_DIM:(hd + 1) * DSA_HEAD_DIM, :], zq], axis=0)
                          for hd in range(DSA_HEADS)], axis=1)
    wide = DSA_HEADS * tq

    def scores(kb, dst_ref):
        k0 = pl.multiple_of(kb * tk, tk)
        dst_ref[...] = _dot(dkv_ref[pl.ds(k0, tk), :], qz)

    def absorb(src_ref, kb, carry):
        m_i, l_i, acc = carry
        k0 = pl.multiple_of(kb * tk, tk)
        bias = bias_ref[pl.ds(k0, tk), :]
        s = src_ref[...] + jnp.concatenate([bias] * DSA_HEADS, axis=1)
        m_new = jnp.maximum(m_i, jnp.max(s, axis=0, keepdims=True))
        p = jnp.exp2(s - m_new)
        alpha = jnp.exp2(m_i - m_new)
        l_new = alpha * l_i + jnp.sum(p, axis=0, keepdims=True)
        pv = _dot(dkvt_ref[0, DSA_HEAD_DIM:, pl.ds(k0, tk)], p.astype(BF16))
        return m_new, l_new, alpha * acc + pv

    def pair(j, carry):
        kb = 2 * j
        scores(kb + 1, sb_ref)
        carry = absorb(sa_ref, kb, carry)
        scores(kb + 2, sa_ref)
        return absorb(sb_ref, kb + 1, carry)

    init = (jnp.full((1, wide), NEG_BIG, F32), jnp.zeros((1, wide), F32), jnp.zeros((DSA_HEAD_DIM, wide), F32))
    last = nkb - 1
    scores(0, sa_ref)
    carry = lax.fori_loop(0, last // 2, pair, init)

    def tail_odd(carry):
        scores(last, sb_ref)
        return absorb(sb_ref, last, absorb(sa_ref, last - 1, carry))

    def tail_even(carry):
        return absorb(sa_ref, last, carry)

    _, l_f, acc = lax.cond(last % 2 == 1, tail_odd, tail_even, carry)
    o_all = acc / l_f
    o_ref[...] = jnp.concatenate([o_all[:, hd * tq:(hd + 1) * tq] for hd in range(DSA_HEADS)],
                                 axis=0).T.astype(BF16)


def _dsa(qdt, iqt, iwt, dkv, dkvt, ikw, batch, seq):
    tq, tk = DSA_TQ, DSA_TK
    nq = seq // tq
    n_sel = min(DSA_TOPK, seq // 4)
    return pl.pallas_call(
        functools.partial(_dsa_kernel, tq=tq, tk=tk, n_sel=n_sel),
        out_shape=jax.ShapeDtypeStruct((batch * seq, 256), BF16),
        grid=(batch, nq),
        in_specs=[pl.BlockSpec((1, 256, tq), lambda b, i: (b, 0, i)),
                  pl.BlockSpec((1, 128, tq), lambda b, i: (b, 0, i)),
                  pl.BlockSpec((1, 8, tq), lambda b, i: (b, 0, i)),
                  pl.BlockSpec((seq, 128), lambda b, i: (b, 0)),
                  pl.BlockSpec((1, 128, seq), lambda b, i: (b, 0, 0)),
                  pl.BlockSpec((seq, 128), lambda b, i: (b, 0))],
        out_specs=pl.BlockSpec((tq, 256), lambda b, i: (b * nq + i, 0)),
        scratch_shapes=[pltpu.VMEM((seq, tq), jnp.int32), pltpu.VMEM((seq, tq), F32),
                        pltpu.VMEM((seq, tq), jnp.int16),
                        pltpu.VMEM((tk, DSA_HEADS * tq), F32), pltpu.VMEM((tk, DSA_HEADS * tq), F32)],
        compiler_params=_params(("parallel", "parallel")),
        name="dsa",
    )(qdt, iqt, iwt, dkv, dkvt, ikw)


def _mem_kv_kernel(mem_ref, g_ref, w_ref, kt_ref, v_ref):
    mn = _rms(mem_ref[0], g_ref[...]).astype(BF16)
    kv = _dot(mn, w_ref[...])
    kt_ref[0] = kv[:, :256].T.astype(BF16)
    v_ref[0] = kv[:, 256:].astype(BF16)


def _mem_kv(mem, gain, w_kv):
    b, m, d = mem.shape
    return pl.pallas_call(
        _mem_kv_kernel,
        out_shape=(jax.ShapeDtypeStruct((b, 256, m), BF16), jax.ShapeDtypeStruct((b, m, 256), BF16)),
        grid=(b,),
        in_specs=[pl.BlockSpec((1, m, d), lambda i: (i, 0, 0)), _const_spec((1, d)), _const_spec((d, 512))],
        out_specs=(pl.BlockSpec((1, 256, m), lambda i: (i, 0, 0)), pl.BlockSpec((1, m, 256), lambda i: (i, 0, 0))),
        compiler_params=_params(("parallel",)),
        name="mem_kv",
    )(mem, gain, w_kv)


def _merge_kernel(x_ref, ya_ref, yb_ref, yc_ref, yd_ref, gmix_ref, wg_ref, wbr_ref, wout_ref,
                  gq_ref, wq_ref, mkt_ref, mv_ref, wo_ref, gffn_ref, wr_ref, br_ref,
                  x2_ref, hp_ref, rt_ref, cnt_ref, run_ref, *, tm):
    x = x_ref[...]
    d = x.shape[-1]
    h = _rms(x, gmix_ref[...]).astype(BF16)
    merged = jnp.zeros((tm, d), F32)
    for n, y_ref in enumerate((ya_ref, yb_ref, yc_ref, yd_ref)):
        wn = wg_ref[:, n * d:n * d + d + LANES][:, GATE_SHIFT:GATE_SHIFT + d]
        gate = _sigmoid(_dot(h, wn))
        merged = merged + gate * _dot(y_ref[...], wbr_ref[n])
    x1 = x + _dot(merged.astype(BF16), wout_ref[...])
    h2 = _rms(x1, gq_ref[...]).astype(BF16)
    q = (_dot(h2, wq_ref[...]) * (MEM_HEAD_DIM ** -0.5)).astype(BF16)
    lane_head = lax.broadcasted_iota(jnp.int32, (tm, 256), 1) // MEM_HEAD_DIM
    mv = mv_ref[0]
    o = jnp.zeros((tm, 256), F32)
    for hd in range(MEM_HEADS):
        s = _dot(q[:, hd * MEM_HEAD_DIM:(hd + 1) * MEM_HEAD_DIM], mkt_ref[0, hd * MEM_HEAD_DIM:(hd + 1) * MEM_HEAD_DIM, :])
        p = jnp.exp(s - jnp.max(s, axis=-1, keepdims=True))
        p = p / jnp.sum(p, axis=-1, keepdims=True)
        o = o + jnp.where(lane_head == hd, _dot(p.astype(BF16), mv), 0.0)
    x2 = x1 + _dot(o.astype(BF16), wo_ref[...])
    x2_ref[...] = x2
    h3f = _rms(x2, gffn_ref[...])
    hp_ref[...] = _pack_bf16_pairs(h3f)
    h3 = h3f.astype(BF16)
    logits = _dot(h3, wr_ref[...]) + br_ref[...]
    lane = lax.broadcasted_iota(jnp.int32, (tm, LANES), 1)
    gl = jnp.where(lane < MOE_GROUPS, logits, -jnp.inf)
    gmax = jnp.max(gl, axis=-1, keepdims=True)
    gsel = jnp.min(jnp.where(gl == gmax, lane, LANES), axis=-1, keepdims=True)
    pg_sel = 1.0 / jnp.sum(jnp.exp(gl - gmax), axis=-1, keepdims=True)
    in_group = (lane - ROUTER_EXPERT_LANE) // MOE_EXPERTS_PER_GROUP == gsel
    el = jnp.where(in_group, logits, -jnp.inf)
    m1 = jnp.max(el, axis=-1, keepdims=True)
    i1 = jnp.min(jnp.where(el == m1, lane, LANES), axis=-1, keepdims=True)
    el2 = jnp.where(lane == i1, -jnp.inf, el)
    m2 = jnp.max(el2, axis=-1, keepdims=True)
    i2 = jnp.min(jnp.where(el2 == m2, lane, LANES), axis=-1, keepdims=True)
    e21 = jnp.exp(m2 - m1)
    c1 = pg_sel / (1.0 + e21)
    @pl.when(pl.program_id(0) == 0)
    def _():
        run_ref[...] = jnp.zeros_like(run_ref)

    oh1 = jnp.where(lane == i1, 1.0, 0.0)
    oh2 = jnp.where(lane == i2, 1.0, 0.0)
    both = oh1 + oh2
    tr = lax.broadcasted_iota(jnp.int32, (tm, tm), 0)
    tc = lax.broadcasted_iota(jnp.int32, (tm, tm), 1)
    before = _dot(jnp.where(tc < tr, 1.0, 0.0).astype(BF16), both.astype(BF16)) + run_ref[0:1, :]
    r1 = jnp.sum(oh1 * before, axis=-1, keepdims=True)
    r2 = jnp.sum(oh2 * before, axis=-1, keepdims=True)
    total = run_ref[...] + jnp.sum(both, axis=0, keepdims=True)
    run_ref[...] = total
    cnt_ref[...] = total
    ids = (jnp.where(lane == 0, i1, i2) - ROUTER_EXPERT_LANE).astype(F32)
    rt_ref[...] = jnp.where(lane < 2, ids, jnp.where(lane == 2, c1, jnp.where(lane == 3, c1 * e21,
                            jnp.where(lane == 4, r1, jnp.where(lane == 5, r2, 0.0)))))


def _merge(x, ys, gmix, wg, wbr, wout, gq, wq, mkt, mv, wo, gffn, wr, br, batch, seq):
    n, d = x.shape
    tm = MERGE_TM
    spt = seq // tm
    m = mv.shape[1]
    tok = lambda w: pl.BlockSpec((tm, w), lambda i: (i, 0))
    return pl.pallas_call(
        functools.partial(_merge_kernel, tm=tm),
        out_shape=(jax.ShapeDtypeStruct((n, d), F32), jax.ShapeDtypeStruct((n, d // 2), F32),
                   jax.ShapeDtypeStruct((n, LANES), F32), jax.ShapeDtypeStruct((8, LANES), F32)),
        grid=(n // tm,),
        in_specs=[tok(d), tok(256), tok(256), tok(256), tok(256),
                  _const_spec((1, d)), _const_spec((d, N_BRANCH * d + LANES)), _const_spec((N_BRANCH, 256, d)),
                  _const_spec((d, d)), _const_spec((1, d)), _const_spec((d, 256)),
                  pl.BlockSpec((1, 256, m), lambda i: (i // spt, 0, 0)),
                  pl.BlockSpec((1, m, 256), lambda i: (i // spt, 0, 0)),
                  _const_spec((256, d)), _const_spec((1, d)), _const_spec((d, LANES)), _const_spec((1, LANES))],
        out_specs=(tok(d), tok(d // 2), tok(LANES), pl.BlockSpec((8, LANES), lambda i: (0, 0))),
        scratch_shapes=[pltpu.VMEM((8, LANES), F32)],
        compiler_params=_params(("arbitrary",)),
        name="merge_mem_router",
    )(x, *ys, gmix, wg, wbr, wout, gq, wq, mkt, mv, wo, gffn, wr, br)


def _sc_gather_rows(table, idx):
    _, width = table.shape
    total = idx.shape[0]
    chunk, nbuf = SC_GATHER_CHUNK, SC_GATHER_BUFS
    workers = SC_CORES * SC_SUBCORES
    per_w = total // workers
    nch = per_w // chunk
    assert total % (workers * chunk * nbuf) == 0
    mesh = plsc.VectorSubcoreMesh(core_axis_name="c", subcore_axis_name="s")

    @functools.partial(
        pl.kernel, mesh=mesh, out_type=jax.ShapeDtypeStruct((total, width), table.dtype),
        scratch_types=[pltpu.VMEM((nch, chunk), jnp.int32), pltpu.VMEM((nbuf, chunk, width), table.dtype),
                       pltpu.SemaphoreType.DMA((nbuf,)), pltpu.SemaphoreType.DMA((nbuf,))])
    def gather_kernel(table_hbm, idx_hbm, out_hbm, idx_v, rows_v, gsem, wsem):
        wid = lax.axis_index("s") * SC_CORES + lax.axis_index("c")
        pltpu.sync_copy(idx_hbm.at[wid], idx_v)

        def gather(j, slot):
            return pltpu.make_async_copy(table_hbm.at[idx_v.at[j]], rows_v.at[slot], gsem.at[slot])

        def write(j, slot):
            off = pl.multiple_of(wid * per_w + j * chunk, chunk)
            return pltpu.make_async_copy(rows_v.at[slot], out_hbm.at[pl.ds(off, chunk)], wsem.at[slot])

        for slot in range(nbuf):
            gather(slot, slot).start()

        @pl.loop(0, nch // nbuf)
        def _(g):
            for slot in range(nbuf):
                j = g * nbuf + slot
                gather(j, slot).wait()
                write(j, slot).start()
                write(j, slot).wait()

                @pl.when(j + nbuf < nch)
                def _():
                    gather(j + nbuf, slot).start()

    return gather_kernel(table, idx.reshape(workers, nch, chunk))


def _sc_scatter_rows(table, dest2, total):
    n, width = table.shape
    chunk, nbuf = SC_GATHER_CHUNK, SC_GATHER_BUFS
    workers = SC_CORES * SC_SUBCORES
    per_w = n // workers
    nch = per_w // chunk
    assert n % (workers * chunk * nbuf) == 0
    mesh = plsc.VectorSubcoreMesh(core_axis_name="c", subcore_axis_name="s")

    @functools.partial(
        pl.kernel, mesh=mesh, out_type=jax.ShapeDtypeStruct((total, width), table.dtype),
        scratch_types=[pltpu.VMEM((2, nch, chunk), jnp.int32), pltpu.VMEM((nbuf, chunk, width), table.dtype),
                       pltpu.SemaphoreType.DMA((nbuf,)), pltpu.SemaphoreType.DMA((nbuf,))])
    def scatter_kernel(table_hbm, idx_hbm, out_hbm, idx_v, rows_v, rsem, wsem):
        wid = lax.axis_index("s") * SC_CORES + lax.axis_index("c")
        pltpu.sync_copy(idx_hbm.at[wid], idx_v)

        def read(j, slot):
            off = pl.multiple_of(wid * per_w + j * chunk, chunk)
            return pltpu.make_async_copy(table_hbm.at[pl.ds(off, chunk)], rows_v.at[slot], rsem.at[slot])

        def write(j, slot, k):
            return pltpu.make_async_copy(rows_v.at[slot], out_hbm.at[idx_v.at[k, j]], wsem.at[slot])

        for slot in range(nbuf):
            read(slot, slot).start()

        @pl.loop(0, nch // nbuf)
        def _(g):
            for slot in range(nbuf):
                j = g * nbuf + slot
                read(j, slot).wait()
                write(j, slot, 0).start()
                write(j, slot, 1).start()
                write(j, slot, 0).wait()
                write(j, slot, 1).wait()

                @pl.when(j + nbuf < nch)
                def _():
                    read(j + nbuf, slot).start()

    idx = dest2.reshape(2, workers, nch, chunk).transpose(1, 0, 2, 3)
    return scatter_kernel(table, idx)


def _dispatch_plan(rt, cnt, n):
    ne, blk = MOE_N_EXPERTS, MOE_BLOCK
    n_blocks = (2 * n) // blk + ne
    experts = jnp.arange(ne, dtype=jnp.int32)
    counts = cnt[0, ROUTER_EXPERT_LANE:ROUTER_EXPERT_LANE + ne].astype(jnp.int32)
    padded = (counts + blk - 1) // blk * blk
    pend = jnp.cumsum(padded)
    pstart = pend - padded
    ids = rt[:, 0:2].astype(jnp.int32)
    pos = rt[:, 4:6].astype(jnp.int32)
    first_row = jnp.sum(jnp.where(ids[:, :, None] == experts[None, None, :], pstart[None, None, :], 0), axis=-1)
    dest2 = (first_row + pos).T
    b0 = jnp.arange(n_blocks, dtype=jnp.int32) * blk
    block_e = jnp.minimum(jnp.sum((pend[None, :] <= b0[:, None]).astype(jnp.int32), axis=1), ne - 1)
    n_valid = jnp.clip(counts[block_e] - (b0 - pstart[block_e]), 0, blk).astype(jnp.int32)
    return dest2, block_e, n_valid


def _expert_block_kernel(be_ref, nv_ref, xs_ref, wg_ref, wu_ref, wd_ref, o_ref, wgb_ref, wub_ref, wdb_ref):
    b = pl.program_id(0)
    valid = nv_ref[b]

    @pl.when((b == 0) | (be_ref[b] != be_ref[jnp.maximum(b - 1, 0)]))
    def _():
        wgb_ref[...] = wg_ref[0].astype(BF16)
        wub_ref[...] = wu_ref[0].astype(BF16)
        wdb_ref[...] = wd_ref[0].astype(BF16)

    @pl.when(valid > 0)
    def _():
        row = lax.broadcasted_iota(jnp.int32, xs_ref.shape, 0)
        words = jnp.where(row < valid, xs_ref[...], 0.0)
        h = _unpack_bf16_pairs(words).astype(BF16)
        gt = _dot(h, wgb_ref[...])
        hid = gt * _sigmoid(gt) * _dot(h, wub_ref[...])
        o_ref[...] = _pack_bf16_pairs(_dot(hid.astype(BF16), wdb_ref[...]))

    @pl.when(valid == 0)
    def _():
        o_ref[...] = jnp.zeros_like(o_ref)


def _expert_blocks(xs, wg, wu, wd, layer, block_e, n_used):
    p_rows, half = xs.shape
    d, hid = wg.shape[-2:]
    blk = MOE_BLOCK
    grid_spec = pltpu.PrefetchScalarGridSpec(
        num_scalar_prefetch=2, grid=(p_rows // blk,),
        in_specs=[pl.BlockSpec((blk, half), lambda b, be, nu: (b, 0)),
                  pl.BlockSpec((None, 1, d, hid), lambda b, be, nu: (layer, be[b], 0, 0)),
                  pl.BlockSpec((None, 1, d, hid), lambda b, be, nu: (layer, be[b], 0, 0)),
                  pl.BlockSpec((None, 1, hid, d), lambda b, be, nu: (layer, be[b], 0, 0))],
        out_specs=pl.BlockSpec((blk, half), lambda b, be, nu: (b, 0)),
        scratch_shapes=[pltpu.VMEM((d, hid), BF16), pltpu.VMEM((d, hid), BF16), pltpu.VMEM((hid, d), BF16)])
    return pl.pallas_call(
        _expert_block_kernel, out_shape=jax.ShapeDtypeStruct((p_rows, half), F32), grid_spec=grid_spec,
        compiler_params=_params(("arbitrary",)),
        name="moe_expert_blocks",
    )(block_e, n_used, xs, wg, wu, wd)


def _combine_kernel(x_ref, y1_ref, y2_ref, rt_ref, gfin_ref, o_ref, *, final_norm):
    out = (x_ref[...] + rt_ref[:, 2:3] * _unpack_bf16_pairs(y1_ref[...])
           + rt_ref[:, 3:4] * _unpack_bf16_pairs(y2_ref[...]))
    o_ref[...] = _rms(out, gfin_ref[...]) if final_norm else out


def _combine(x2, y_halves, rt, gfin, final_norm):
    n, d = x2.shape
    tm = COMBINE_TM
    nt = n // tm
    return pl.pallas_call(
        functools.partial(_combine_kernel, final_norm=final_norm),
        out_shape=jax.ShapeDtypeStruct((n, d), F32),
        grid=(nt,),
        in_specs=[pl.BlockSpec((tm, d), lambda i: (i, 0)),
                  pl.BlockSpec((tm, d // 2), lambda i: (i, 0)),
                  pl.BlockSpec((tm, d // 2), lambda i: (i + nt, 0)),
                  pl.BlockSpec((tm, LANES), lambda i: (i, 0)), _const_spec((1, d))],
        out_specs=pl.BlockSpec((tm, d), lambda i: (i, 0)),
        compiler_params=_params(("parallel",)), name="moe_combine",
    )(x2, y_halves, y_halves, rt, gfin)


def _moe(x2, hp, rt, cnt, wg, wu, wd, layer, gfin, final_norm):
    n = x2.shape[0]
    dest2, block_e, n_valid = _dispatch_plan(rt, cnt, n)
    xs = _sc_scatter_rows(hp, dest2, block_e.shape[0] * MOE_BLOCK)
    yb = _expert_blocks(xs, wg, wu, wd, layer, block_e, n_valid)
    y_halves = _sc_gather_rows(yb, dest2.reshape(2 * n))
    return _combine(x2, y_halves, rt, gfin, final_norm)


def kernel(x, mem, positions, norm_mix, w_in, diff_lambda, hgrn_lb_logits, spatial_w, spatial_b, w_branch, w_out,
           norm_mem_q, norm_mem_kv, w_mem_q, w_mem_kv, w_mem_o, norm_ffn, w_router_group, b_router_group,
           w_router_expert, b_router_expert, w_exp_gate, w_exp_up, w_exp_down, norm_final):
    batch, seq, d = x.shape
    depth = w_in.shape[0]
    n = batch * seq
    xf = x.reshape(n, d)
    tabs = _rope_tables(positions)
    w_all = _w_in_bf16(w_in)
    row = lambda v: v.reshape(1, -1).astype(F32)
    for l in range(depth):
        lam_init = 0.8 - 0.6 * math.exp(-0.3 * l)
        g0 = C_GATE - GATE_SHIFT
        w_gate = lax.slice(w_all, (l, 0, g0), (l + 1, d, g0 + N_BRANCH * d + LANES)).reshape(d, -1)
        sw = spatial_w[l].reshape(SGU_GROUPS * SGU_CHUNK, SGU_CHUNK)
        sb = jnp.repeat(spatial_b[l].T, SGU_GROUP_DIM, axis=1)
        qat, ka, vat, hb, y_c, qdt, iqt, dkv, dkvt, ikw, iwt = _projection(
            xf, row(norm_mix[l]), w_all, l, tabs, sw, sb, batch, seq)
        y_a = _diff_attention(diff_lambda[l], qat, ka, vat, lam_init, batch, seq)
        y_b = _hgrn(hgrn_lb_logits, hb, l, batch, seq)
        y_d = _dsa(qdt, iqt, iwt, dkv, dkvt, ikw, batch, seq)
        mkt, mv = _mem_kv(mem, row(norm_mem_kv[l]), w_mem_kv[l].astype(BF16))
        e0, e1 = ROUTER_EXPERT_LANE, ROUTER_EXPERT_LANE + MOE_N_EXPERTS
        wr = jnp.zeros((d, LANES), F32)
        wr = wr.at[:, :MOE_GROUPS].set(w_router_group[l]).at[:, e0:e1].set(w_router_expert[l]).astype(BF16)
        br = jnp.zeros((1, LANES), F32)
        br = br.at[0, :MOE_GROUPS].set(b_router_group[l]).at[0, e0:e1].set(b_router_expert[l])
        x2, hp, rt, cnt = _merge(xf, (y_a, y_b, y_c, y_d), row(norm_mix[l]), w_gate, w_branch[l].astype(BF16),
                                 w_out[l].astype(BF16), row(norm_mem_q[l]), w_mem_q[l].astype(BF16), mkt, mv,
                                 w_mem_o[l].astype(BF16), row(norm_ffn[l]), wr, br, batch, seq)
        xf = _moe(x2, hp, rt, cnt, w_exp_gate, w_exp_up, w_exp_down, l,
                  row(norm_final), final_norm=(l == depth - 1))
    return xf.reshape(batch, seq, d)
```

```python
import functools
import math

import numpy as np
import jax
import jax.numpy as jnp
from jax import lax
from jax.experimental import pallas as pl
from jax.experimental.pallas import tpu as pltpu
from jax.experimental.pallas import tpu_sc as plsc

F32 = jnp.float32
BF16 = jnp.bfloat16

NORM_EPS = 1e-6
ROPE_THETA = 10000.0
NEG_BIG = -1e30

N_BRANCH = 4
DIFF_HEADS = 4
DIFF_HEAD_DIM = 32
HGRN_DIM = 64
HGRN_CHUNK = 32
HGRN_UNROLL = 8
HGRN_MIN_FORGET = 1e-30
SGU_GROUPS = 4
SGU_GROUP_DIM = 64
SGU_CHUNK = 128
DSA_HEADS = 4
DSA_HEAD_DIM = 64
DSA_IDX_HEADS = 4
DSA_IDX_DIM = 32
DSA_TOPK = 256
MEM_HEADS = 4
MEM_HEAD_DIM = 64
MOE_GROUPS = 4
MOE_EXPERTS_PER_GROUP = 8
MOE_N_EXPERTS = 32
MOE_BLOCK = 512
ROUTER_EXPERT_LANE = 32
SC_CORES = 2
SC_SUBCORES = 16
SC_GATHER_CHUNK = 16
SC_GATHER_BUFS = 4

LANES = 128
VMEM_LIMIT = 56 * 1024 * 1024

PROJ_TM = 1024
DIFF_TQ = 512
DIFF_TK = 512
HGRN_TC = 1024
DSA_TQ = 512
DSA_TK = 512
MERGE_TM = 1024
COMBINE_TM = 1024
ROPE_TM = 1024
W_IN_COLS = 256

C_AQ, C_AK, C_AV = 0, 256, 512
C_HB = 768
C_UV = 1792
C_DQ = 2304
C_DKV = 2560
C_IQ = 2688
C_IKW = 2816
IW_LANE = 32
C_GATE = 2852
GATE_SHIFT = C_GATE % 128
C_TOTAL = 2944
LOG2E = math.log2(math.e)


def _params(sem):
    return pltpu.CompilerParams(dimension_semantics=sem, vmem_limit_bytes=VMEM_LIMIT)


def _const_spec(shape):
    nd = len(shape)
    return pl.BlockSpec(shape, lambda *_: (0,) * nd, pipeline_mode=pl.Buffered(1))


def _rms(xf, gain=None):
    y = xf * lax.rsqrt(jnp.mean(xf * xf, axis=-1, keepdims=True) + NORM_EPS)
    return y if gain is None else y * gain


def _sigmoid(x):
    return 0.5 * jnp.tanh(0.5 * x) + 0.5


def _pack_bf16_pairs(x):
    w = x.shape[-1] // 2
    xb = x.astype(BF16).astype(F32)
    lo = lax.shift_right_logical(pltpu.bitcast(xb[:, :w], jnp.int32), 16)
    hi = pltpu.bitcast(xb[:, w:], jnp.int32) & jnp.int32(-65536)
    return pltpu.bitcast(hi | lo, F32)


def _unpack_bf16_pairs(words):
    bits = pltpu.bitcast(words, jnp.int32)
    lo = pltpu.bitcast(bits << 16, F32)
    hi = pltpu.bitcast(bits & jnp.int32(-65536), F32)
    return jnp.concatenate([lo, hi], axis=1)


def _dot(a, b):
    return jnp.dot(a, b, preferred_element_type=F32)


def _dot_nt(a, b):
    return lax.dot_general(a, b, (((1,), (1,)), ((), ())), preferred_element_type=F32)


def _rope_table_kernel(pos_ref, frq_ref, sgn_ref, c32_ref, s32_ref, c64_ref, s64_ref):
    ang = pos_ref[...].astype(F32) * frq_ref[...]
    cos = jnp.cos(ang)
    sin = jnp.sin(ang) * sgn_ref[...]
    four = lambda t: jnp.concatenate([t, t, t, t], axis=1)
    c64_ref[...] = four(cos[:, :64])
    s64_ref[...] = four(sin[:, :64])
    c32_ref[...] = four(cos[:, 64:])
    s32_ref[...] = four(sin[:, 64:])


def _rope_tables(positions):
    n = positions.size
    pos = positions.reshape(n, 1).astype(jnp.int32)
    lane = np.arange(64)
    inv32 = ROPE_THETA ** (-jnp.arange(16, dtype=F32) * (2.0 / 32))
    inv64 = ROPE_THETA ** (-jnp.arange(32, dtype=F32) * (2.0 / 64))
    frq = jnp.concatenate([inv64[lane % 32], inv32[lane % 16]]).reshape(1, LANES)
    sgn = jnp.asarray(np.concatenate([np.where(lane % 64 < 32, -1.0, 1.0),
                                      np.where(lane % 32 < 16, -1.0, 1.0)]), F32).reshape(1, LANES)
    tm = ROPE_TM
    tab = jax.ShapeDtypeStruct((n, 256), F32)
    return pl.pallas_call(
        _rope_table_kernel,
        out_shape=(tab, tab, tab, tab),
        grid=(n // tm,),
        in_specs=[pl.BlockSpec((tm, 1), lambda i: (i, 0)), _const_spec((1, LANES)), _const_spec((1, LANES))],
        out_specs=tuple(pl.BlockSpec((tm, 256), lambda i: (i, 0)) for _ in range(4)),
        compiler_params=_params(("parallel",)),
        name="rope_tables",
    )(pos, frq, sgn)


def _w_in_kernel(wt_ref, o_ref, *, width):
    rows = wt_ref.shape[0]
    row = lax.broadcasted_iota(jnp.int32, (rows, wt_ref.shape[-1]), 0)
    valid = row < width - pl.program_id(0) * rows
    for layer in range(o_ref.shape[0]):
        o_ref[layer] = jnp.where(valid, wt_ref[:, layer, :], 0.0).T.astype(BF16)


def _w_in_bf16(w_in):
    depth, d, width = w_in.shape
    cols = W_IN_COLS
    nblk = pl.cdiv(width, cols)
    return pl.pallas_call(
        functools.partial(_w_in_kernel, width=width), out_shape=jax.ShapeDtypeStruct((depth, d, nblk * cols), BF16),
        grid=(nblk,), in_specs=[pl.BlockSpec((cols, depth, d), lambda i: (i, 0, 0))],
        out_specs=pl.BlockSpec((depth, d, cols), lambda i: (0, 0, i)),
        compiler_params=_params(("parallel",)), name="w_in_bf16",
    )(jnp.transpose(w_in, (2, 0, 1)))


def _gelu_tanh(x):
    return 0.5 * x * (1.0 + jnp.tanh(math.sqrt(2.0 / math.pi) * (x + 0.044715 * (x * x * x))))


def _rope(x, cos, sin_signed, half):
    w = x.shape[-1]
    lane = lax.broadcasted_iota(jnp.int32, x.shape, 1)
    partner = jnp.where(lane % (2 * half) < half, pltpu.roll(x, w - half, 1), pltpu.roll(x, half, 1))
    return x * cos + partner * sin_signed


def _proj_kernel(x_ref, g_ref, w_ref, c32_ref, s32_ref, c64_ref, s64_ref, sw_ref, sb_ref,
                 qat_ref, ka_ref, vat_ref, hb_ref, yc_ref, qdt_ref, iqt_ref, dkv_ref, dkvt_ref, ikw_ref, iwt_ref,
                 *, tm):
    h = _rms(x_ref[...], g_ref[...]).astype(BF16)

    def proj(c0, width):
        return _dot(h, w_ref[:, c0:c0 + width])

    c32, s32, c64, s64 = c32_ref[...], s32_ref[...], c64_ref[...], s64_ref[...]
    qat_ref[0] = (_rope(proj(C_AQ, 256), c32, s32, 16) * (DIFF_HEAD_DIM ** -0.5 * LOG2E)).T.astype(BF16)
    ka_ref[...] = _rope(proj(C_AK, 256), c32, s32, 16).astype(BF16)
    vat_ref[0] = proj(C_AV, 256).astype(BF16).T
    hb_ref[...] = proj(C_HB, 1024)
    qdt_ref[0] = (_rope(proj(C_DQ, 256), c64, s64, 32) * (DSA_HEAD_DIM ** -0.5 * LOG2E)).T.astype(BF16)
    iqt_ref[0] = _rope(proj(C_IQ, 128), c32[:, :128], s32[:, :128], 16).T.astype(BF16)
    lane = lax.broadcasted_iota(jnp.int32, (tm, 128), 1)
    is_k = lane < DSA_HEAD_DIM
    dkv = _rope(proj(C_DKV, 128), jnp.where(is_k, c64[:, :128], 1.0), jnp.where(is_k, s64[:, :128], 0.0), 32)
    is_ik = lane < DSA_IDX_DIM
    ikw = _rope(proj(C_IKW, 128), jnp.where(is_ik, c32[:, :128], 1.0), jnp.where(is_ik, s32[:, :128], 0.0), 16)
    dkv_ref[...] = dkv.astype(BF16)
    dkvt_ref[0] = dkv.T.astype(BF16)
    ikw_ref[...] = ikw.astype(BF16)
    iw_scale = DSA_IDX_HEADS ** -0.5 * DSA_IDX_DIM ** -0.5
    iwt_ref[0] = (ikw * iw_scale).T[IW_LANE:IW_LANE + 8, :]
    uv = _gelu_tanh(proj(C_UV, 512))
    u, v = uv[:, :256], uv[:, 256:]
    mu = jnp.mean(v, axis=-1, keepdims=True)
    vc = v - mu
    vn = (vc * lax.rsqrt(jnp.mean(vc * vc, axis=-1, keepdims=True) + NORM_EPS)).astype(BF16)
    r = lax.broadcasted_iota(jnp.int32, (SGU_GROUPS * SGU_CHUNK, SGU_CHUNK), 0)
    c = lax.broadcasted_iota(jnp.int32, (SGU_GROUPS * SGU_CHUNK, SGU_CHUNK), 1)
    wt = jnp.where((r % SGU_CHUNK) >= c, sw_ref[...], 0.0).astype(BF16)
    lane_grp = lax.broadcasted_iota(jnp.int32, (SGU_CHUNK, 256), 1) // SGU_GROUP_DIM
    for ch in range(tm // SGU_CHUNK):
        r0 = ch * SGU_CHUNK
        full = _dot(wt, vn[r0:r0 + SGU_CHUNK, :])
        mixed = sb_ref[...]
        for g in range(SGU_GROUPS):
            mixed = mixed + jnp.where(lane_grp == g, full[g * SGU_CHUNK:(g + 1) * SGU_CHUNK, :], 0.0)
        yc_ref[r0:r0 + SGU_CHUNK, :] = (u[r0:r0 + SGU_CHUNK, :] * mixed).astype(BF16)


def _projection(x, gain, w_all, layer, tabs, sw, sb, batch, seq):
    n, d = x.shape
    tm = PROJ_TM
    spt = seq // tm
    tok = lambda w: pl.BlockSpec((tm, w), lambda i: (i, 0))
    tr = lambda rows: pl.BlockSpec((1, rows, tm), lambda i: (i // spt, 0, i % spt))
    out_shape = (
        jax.ShapeDtypeStruct((batch, 256, seq), BF16),
        jax.ShapeDtypeStruct((n, 256), BF16),
        jax.ShapeDtypeStruct((batch, 256, seq), BF16),
        jax.ShapeDtypeStruct((n, 1024), F32),
        jax.ShapeDtypeStruct((n, 256), BF16),
        jax.ShapeDtypeStruct((batch, 256, seq), BF16),
        jax.ShapeDtypeStruct((batch, 128, seq), BF16),
        jax.ShapeDtypeStruct((n, 128), BF16),
        jax.ShapeDtypeStruct((batch, 128, seq), BF16),
        jax.ShapeDtypeStruct((n, 128), BF16),
        jax.ShapeDtypeStruct((batch, 8, seq), F32),
    )
    return pl.pallas_call(
        functools.partial(_proj_kernel, tm=tm),
        out_shape=out_shape,
        grid=(n // tm,),
        in_specs=[tok(d), _const_spec((1, d)),
                  pl.BlockSpec((None, d, C_TOTAL), lambda i: (layer, 0, 0), pipeline_mode=pl.Buffered(1)),
                  tok(256), tok(256), tok(256), tok(256),
                  _const_spec((SGU_GROUPS * SGU_CHUNK, SGU_CHUNK)), _const_spec((SGU_CHUNK, 256))],
        out_specs=(tr(256), tok(256), tr(256), tok(1024), tok(256), tr(256), tr(128), tok(128), tr(128), tok(128), tr(8)),
        compiler_params=_params(("parallel",)),
        name="projection",
    )(x, gain, w_all, *tabs, sw, sb)


def _diff_attn_kernel(lam_ref, qt_ref, k_ref, vt_ref, o_ref, sa_ref, sb_ref, *, lam_init, tq, tk):
    q0 = pl.program_id(1) * tq
    kb_diag = q0 // tk
    lv = lam_ref[...]
    lam = (jnp.exp(jnp.sum(lv[0:1] * lv[1:2], axis=-1, keepdims=True))
           - jnp.exp(jnp.sum(lv[2:3] * lv[3:4], axis=-1, keepdims=True)) + lam_init)
    qt = qt_ref[0]
    feat = lax.broadcasted_iota(jnp.int32, (256, tq), 0) // DIFF_HEAD_DIM
    n_maps = 2 * DIFF_HEADS
    qz = jnp.concatenate([jnp.where(feat == i, qt, jnp.zeros_like(qt)) for i in range(n_maps)], axis=1)
    wide = n_maps * tq
    key_i = lax.broadcasted_iota(jnp.int32, (tk, wide), 0)
    qry_i = q0 + lax.broadcasted_iota(jnp.int32, (tk, wide), 1) % tq

    def scores(kb, dst_ref):
        k0 = pl.multiple_of(kb * tk, tk)
        dst_ref[...] = _dot(k_ref[pl.ds(k0, tk), :], qz)

    def absorb(src_ref, kb, carry, masked):
        m_i, l_i, acc = carry
        k0 = pl.multiple_of(kb * tk, tk)
        s = src_ref[...]
        if masked:
            s = jnp.where(k0 + key_i <= qry_i, s, NEG_BIG)
        m_new = jnp.maximum(m_i, jnp.max(s, axis=0, keepdims=True))
        p = jnp.exp2(s - m_new)
        alpha = jnp.exp2(m_i - m_new)
        l_new = alpha * l_i + jnp.sum(p, axis=0, keepdims=True)
        pb = p.astype(BF16)
        pv = jnp.concatenate(
            [_dot(vt_ref[0, hd * 64:(hd + 1) * 64, pl.ds(k0, tk)], pb[:, 2 * hd * tq:(2 * hd + 2) * tq])
             for hd in range(DIFF_HEADS)], axis=1)
        return m_new, l_new, alpha * acc + pv

    def pair(j, carry):
        kb = 2 * j
        scores(kb + 1, sb_ref)
        carry = absorb(sa_ref, kb, carry, False)
        scores(kb + 2, sa_ref)
        return absorb(sb_ref, kb + 1, carry, False)

    init = (jnp.full((1, wide), NEG_BIG, F32), jnp.zeros((1, wide), F32), jnp.zeros((64, wide), F32))
    scores(0, sa_ref)
    carry = lax.fori_loop(0, kb_diag // 2, pair, init)

    def tail_odd(carry):
        scores(kb_diag, sb_ref)
        carry = absorb(sa_ref, kb_diag - 1, carry, False)
        return absorb(sb_ref, kb_diag, carry, True)

    def tail_even(carry):
        return absorb(sa_ref, kb_diag, carry, True)

    _, l_f, acc = lax.cond(kb_diag % 2 == 1, tail_odd, tail_even, carry)
    o_all = acc / l_f
    heads = []
    for hd in range(DIFF_HEADS):
        o0 = o_all[:, 2 * hd * tq:(2 * hd + 1) * tq]
        o1 = o_all[:, (2 * hd + 1) * tq:(2 * hd + 2) * tq]
        o_h = o0 - lam * o1
        ms = jnp.mean(o_h * o_h, axis=0, keepdims=True)
        heads.append(o_h * lax.rsqrt(ms + NORM_EPS) * (1.0 - lam_init))
    o_ref[...] = jnp.concatenate(heads, axis=0).T.astype(BF16)


def _diff_attention(lam_vec, qat, ka, vat, lam_init, batch, seq):
    tq, tk = DIFF_TQ, DIFF_TK
    nq = seq // tq
    return pl.pallas_call(
        functools.partial(_diff_attn_kernel, lam_init=lam_init, tq=tq, tk=tk),
        out_shape=jax.ShapeDtypeStruct((batch * seq, 256), BF16),
        grid=(batch, nq),
        in_specs=[_const_spec((4, DIFF_HEAD_DIM)),
                  pl.BlockSpec((1, 256, tq), lambda b, i: (b, 0, i)),
                  pl.BlockSpec((seq, 256), lambda b, i: (b, 0)),
                  pl.BlockSpec((1, 256, seq), lambda b, i: (b, 0, 0))],
        out_specs=pl.BlockSpec((tq, 256), lambda b, i: (b * nq + i, 0)),
        scratch_shapes=[pltpu.VMEM((tk, 2 * DIFF_HEADS * tq), F32), pltpu.VMEM((tk, 2 * DIFF_HEADS * tq), F32)],
        compiler_params=_params(("parallel", "parallel")),
        name="diff_attention",
    )(lam_vec, qat, ka, vat)


def _hgrn_kernel(lbl_ref, hb_ref, o_ref, st_ref, pstk_ref, *, layer, tc):
    cz = HGRN_CHUNK
    w = 256

    @pl.when(pl.program_id(1) == 0)
    def _():
        st_ref[...] = jnp.zeros_like(st_ref)

    lg = lbl_ref[...]
    e = jnp.exp(lg - jnp.max(lg, axis=0, keepdims=True))
    lw = e / jnp.sum(e, axis=0, keepdims=True)
    lb = jnp.sum(lw[0:layer + 1], axis=0, keepdims=True) - lw[0:1]

    scan_row = lax.broadcasted_iota(jnp.int32, (cz, w), 0)
    rb = lax.broadcasted_iota(jnp.int32, (w, w), 0) // HGRN_DIM
    cb = lax.broadcasted_iota(jnp.int32, (w, w), 1) // HGRN_DIM
    same_head = rb == cb
    head_ones = same_head.astype(BF16)
    trows = {r: r + lax.broadcasted_iota(jnp.int32, (16, w), 0) for r in range(0, cz, 16)}

    def chunk(c, carry):
        r0 = pl.multiple_of(c * cz, cz)
        q = hb_ref[pl.ds(r0, cz), 0:256]
        fp = hb_ref[pl.ds(r0, cz), 256:512]
        v = hb_ref[pl.ds(r0, cz), 512:768]
        g = hb_ref[pl.ds(r0, cz), 768:1024]
        qf = q * _sigmoid(q)
        f = lb + (1.0 - lb) * jax.nn.sigmoid(fp)
        log_f = jnp.log(jnp.maximum(f, HGRN_MIN_FORGET))
        kf = (1.0 - lb) * jax.nn.sigmoid(-fp)
        bc = log_f
        step = 1
        while step < cz:
            bc = bc + jnp.where(scan_row >= step, pltpu.roll(bc, step, 0), 0.0)
            step *= 2
        st = st_ref[...]
        o = _dot_nt((qf * jnp.exp(bc)).astype(BF16), st.astype(BF16))
        later = {}
        for blk in range(1, cz // 16):
            r = bc[16 * blk - 1:16 * blk, :]
            later[blk] = (qf[16 * blk:16 * blk + 16, :] * jnp.exp(bc[16 * blk:16 * blk + 16, :] - r),
                          kf[:16 * blk, :] * jnp.exp(r - bc[:16 * blk, :]))
        for s in range(cz):
            r_lo = (s // 16) * 16
            arg = bc[r_lo:r_lo + 16, :] - bc[s:s + 1, :]
            if s > r_lo:
                arg = jnp.where(trows[r_lo] >= s, arg, NEG_BIG)
            p = qf[r_lo:r_lo + 16, :] * kf[s:s + 1, :] * jnp.exp(arg)
            if r_lo:
                pstk_ref[s * cz:s * cz + r_lo, :] = jnp.zeros((r_lo, w), BF16)
            pstk_ref[s * cz + r_lo:s * cz + r_lo + 16, :] = p.astype(BF16)
            for blk in range(s // 16 + 1, cz // 16):
                q_dec, k_dec = later[blk]
                pstk_ref[s * cz + 16 * blk:s * cz + 16 * blk + 16, :] = (q_dec * k_dec[s:s + 1, :]).astype(BF16)
        accs = [jnp.zeros((16, w), F32) for _ in range(cz // 16)]
        for sg in range(cz // 16):
            att = _dot(pstk_ref[sg * 16 * cz:(sg + 1) * 16 * cz, :], head_ones)
            for sl in range(16):
                s = sg * 16 + sl
                for j in range(sg, cz // 16):
                    accs[j] = accs[j] + att[sl * cz + 16 * j:sl * cz + 16 * j + 16, :] * v[s:s + 1, :]
        o = o + jnp.concatenate(accs, axis=0)
        b_end = bc[cz - 1:cz, :]
        kd = kf * jnp.exp(b_end - bc)
        upd = _dot(v.T.astype(BF16), kd.astype(BF16))
        st_ref[...] = st * jnp.exp(b_end) + jnp.where(same_head, upd, 0.0)
        ms = _dot(o * o, head_ones.astype(F32)) * (1.0 / HGRN_DIM)
        y = o * lax.rsqrt(ms + NORM_EPS)
        o_ref[pl.ds(r0, cz), :] = (y * (g * _sigmoid(g))).astype(BF16)
        return carry

    def group(gi, carry):
        for u in range(HGRN_UNROLL):
            chunk(gi * HGRN_UNROLL + u, carry)
        return carry

    lax.fori_loop(0, tc // cz // HGRN_UNROLL, group, 0)


def _hgrn(lb_logits, hb, layer, batch, seq):
    tc = HGRN_TC
    nt = seq // tc
    cz = HGRN_CHUNK
    return pl.pallas_call(
        functools.partial(_hgrn_kernel, layer=layer, tc=tc),
        out_shape=jax.ShapeDtypeStruct((batch * seq, 256), BF16),
        grid=(batch, nt),
        in_specs=[_const_spec(lb_logits.shape),
                  pl.BlockSpec((tc, 1024), lambda b, i: (b * nt + i, 0))],
        out_specs=pl.BlockSpec((tc, 256), lambda b, i: (b * nt + i, 0)),
        scratch_shapes=[pltpu.VMEM((256, 256), F32), pltpu.VMEM((cz * cz, 256), BF16)],
        compiler_params=_params(("parallel", "arbitrary")),
        name="hgrn2",
    )(lb_logits, hb)


def _dsa_kernel(qdt_ref, iqt_ref, iwt_ref, dkv_ref, dkvt_ref, ikw_ref, o_ref, key_ref, bias_ref, half_ref,
                sa_ref, sb_ref, *, tq, tk, n_sel):
    q0 = pl.program_id(1) * tq
    nkb = q0 // tk + 1
    key_i = lax.broadcasted_iota(jnp.int32, (tk, tq), 0)
    qry_i = q0 + lax.broadcasted_iota(jnp.int32, (tk, tq), 1)
    grp = tk // 8
    rows8 = lambda x: x.reshape(grp, 8, tq)
    iqt = iqt_ref[0]
    zpad = jnp.zeros((LANES - DSA_IDX_DIM, tq), BF16)
    iqz = jnp.concatenate([jnp.concatenate([iqt[hd * DSA_IDX_DIM:(hd + 1) * DSA_IDX_DIM, :], zpad], axis=0)
                           for hd in range(DSA_IDX_HEADS)], axis=1)
    iw = iwt_ref[0]

    last = nkb - 1

    def score_block(kb, carry, masked):
        k0 = pl.multiple_of(kb * tk, tk)
        sh = jnp.maximum(_dot(ikw_ref[pl.ds(k0, tk), :], iqz), 0.0)
        sc = jnp.zeros((tk, tq), F32)
        for hd in range(DSA_IDX_HEADS):
            sc = sc + sh[:, hd * tq:(hd + 1) * tq] * iw[hd:hd + 1, :]
        sc = sc + 0.0
        if masked:
            sc = jnp.where(k0 + key_i <= qry_i, sc, -jnp.inf)
        bits = pltpu.bitcast(sc, jnp.int32)
        key = jnp.where(bits < 0, bits ^ jnp.int32(0x7FFFFFFF), bits)
        key_ref[pl.ds(k0, tk), :] = key
        half_ref[pl.ds(k0, tk), :] = (key >> 16).astype(jnp.int16)
        return carry

    lax.fori_loop(0, last, functools.partial(score_block, masked=False), 0)
    score_block(last, 0, True)

    one16, zero16 = jnp.ones((), jnp.int16), jnp.zeros((), jnp.int16)
    low16 = np.int16(-2 ** 15)

    def count16(limit, strict):
        def body(kb, acc):
            k0 = pl.multiple_of(kb * tk, tk)
            for c in range(tk // 128):
                blk = half_ref[pl.ds(k0 + 128 * c, 128), :].reshape(8, 16, tq)
                hit = jnp.where(blk > limit if strict else blk >= limit, one16, zero16)
                parts = [hit[j] for j in range(8)]
                while len(parts) > 1:
                    parts = [a + b for a, b in zip(parts[0::2], parts[1::2])]
                acc = acc + parts[0]
            return acc
        acc = lax.fori_loop(0, nkb, body, jnp.zeros((16, tq), jnp.int16))
        return jnp.broadcast_to(jnp.sum(acc.astype(jnp.int32), axis=0, keepdims=True), (16, tq))

    def search16(need):
        t = jnp.full((16, tq), -2 ** 15, jnp.int32)
        for bit in range(15, -1, -1):
            trial = t + 2 ** bit
            t = jnp.where(count16(trial.astype(jnp.int16), False) >= need, trial, t)
        return t

    t_hi = search16(n_sel)
    t_hi16 = t_hi.astype(jnp.int16)
    need_lo = n_sel - count16(t_hi16, True)

    def low_block(kb, carry):
        k0 = pl.multiple_of(kb * tk, tk)
        lo = ((key_ref[pl.ds(k0, tk), :] & 0xFFFF) - 2 ** 15).astype(jnp.int16).reshape(tk // 16, 16, tq)
        hi = half_ref[pl.ds(k0, tk), :].reshape(tk // 16, 16, tq)
        half_ref[pl.ds(k0, tk), :] = jnp.where(hi == t_hi16, lo, low16).reshape(tk, tq)
        return carry

    lax.fori_loop(0, nkb, low_block, 0)
    t_lo = search16(need_lo)
    thr = ((t_hi << 16) | (t_lo + 2 ** 15))[0:8, :]

    need = (need_lo - count16(t_lo.astype(jnp.int16), True))[0:8, :].astype(F32)
    ur = lax.broadcasted_iota(jnp.int32, (LANES, LANES), 0)
    uc = lax.broadcasted_iota(jnp.int32, (LANES, LANES), 1)
    earlier = (uc < ur).astype(BF16)
    ones8 = jnp.ones((8, LANES), BF16)
    key_s = lax.broadcasted_iota(jnp.int32, (LANES, tq), 0)
    qry_s = q0 + lax.broadcasted_iota(jnp.int32, (LANES, tq), 1)

    def select_block(kb, seen, masked):
        k0 = pl.multiple_of(kb * tk, tk)
        for g in range(tk // LANES):
            r0 = g * LANES
            blk = key_ref[pl.ds(k0 + r0, LANES), :].reshape(LANES // 8, 8, tq)
            eq = blk == thr[None]
            eqb = jnp.where(eq, 1.0, 0.0).reshape(LANES, tq).astype(BF16)
            rank = _dot(earlier, eqb).reshape(LANES // 8, 8, tq) + seen[None]
            tie_bias = jnp.where(rank < need[None], 0.0, NEG_BIG)
            bias = jnp.where(blk > thr[None], 0.0, jnp.where(eq, tie_bias, NEG_BIG)).reshape(LANES, tq)
            if masked:
                bias = jnp.where(k0 + r0 + key_s <= qry_s, bias, NEG_BIG)
            bias_ref[pl.ds(k0 + r0, LANES), :] = bias
            seen = seen + _dot(ones8, eqb)
        return seen

    seen = lax.fori_loop(0, last, functools.partial(select_block, masked=False), jnp.zeros((8, tq), F32))
    select_block(last, seen, True)

    qdt = qdt_ref[0]
    zq = jnp.zeros((LANES - DSA_HEAD_DIM, tq), BF16)
    qz = jnp.concatenate([jnp.concatenate([qdt[hd * DSA_HEAD_DIM:(hd + 1) * DSA_HEAD_DIM, :], zq], axis=0)
                          for hd in range(DSA_HEADS)], axis=1)
    wide = DSA_HEADS * tq

    def scores(kb, dst_ref):
        k0 = pl.multiple_of(kb * tk, tk)
        dst_ref[...] = _dot(dkv_ref[pl.ds(k0, tk), :], qz)

    def absorb(src_ref, kb, carry):
        m_i, l_i, acc = carry
        k0 = pl.multiple_of(kb * tk, tk)
        bias = bias_ref[pl.ds(k0, tk), :]
        s = src_ref[...] + jnp.concatenate([bias] * DSA_HEADS, axis=1)
        m_new = jnp.maximum(m_i, jnp.max(s, axis=0, keepdims=True))
        p = jnp.exp2(s - m_new)
        alpha = jnp.exp2(m_i - m_new)
        l_new = alpha * l_i + jnp.sum(p, axis=0, keepdims=True)
        pv = _dot(dkvt_ref[0, DSA_HEAD_DIM:, pl.ds(k0, tk)], p.astype(BF16))
        return m_new, l_new, alpha * acc + pv

    def pair(j, carry):
        kb = 2 * j
        scores(kb + 1, sb_ref)
        carry = absorb(sa_ref, kb, carry)
        scores(kb + 2, sa_ref)
        return absorb(sb_ref, kb + 1, carry)

    init = (jnp.full((1, wide), NEG_BIG, F32), jnp.zeros((1, wide), F32), jnp.zeros((DSA_HEAD_DIM, wide), F32))
    last = nkb - 1
    scores(0, sa_ref)
    carry = lax.fori_loop(0, last // 2, pair, init)

    def tail_odd(carry):
        scores(last, sb_ref)
        return absorb(sb_ref, last, absorb(sa_ref, last - 1, carry))

    def tail_even(carry):
        return absorb(sa_ref, last, carry)

    _, l_f, acc = lax.cond(last % 2 == 1, tail_odd, tail_even, carry)
    o_all = acc / l_f
    o_ref[...] = jnp.concatenate([o_all[:, hd * tq:(hd + 1) * tq] for hd in range(DSA_HEADS)],
                                 axis=0).T.astype(BF16)


def _dsa(qdt, iqt, iwt, dkv, dkvt, ikw, batch, seq):
    tq, tk = DSA_TQ, DSA_TK
    nq = seq // tq
    n_sel = min(DSA_TOPK, seq // 4)
    return pl.pallas_call(
        functools.partial(_dsa_kernel, tq=tq, tk=tk, n_sel=n_sel),
        out_shape=jax.ShapeDtypeStruct((batch * seq, 256), BF16),
        grid=(batch, nq),
        in_specs=[pl.BlockSpec((1, 256, tq), lambda b, i: (b, 0, i)),
                  pl.BlockSpec((1, 128, tq), lambda b, i: (b, 0, i)),
                  pl.BlockSpec((1, 8, tq), lambda b, i: (b, 0, i)),
                  pl.BlockSpec((seq, 128), lambda b, i: (b, 0)),
                  pl.BlockSpec((1, 128, seq), lambda b, i: (b, 0, 0)),
                  pl.BlockSpec((seq, 128), lambda b, i: (b, 0))],
        out_specs=pl.BlockSpec((tq, 256), lambda b, i: (b * nq + i, 0)),
        scratch_shapes=[pltpu.VMEM((seq, tq), jnp.int32), pltpu.VMEM((seq, tq), F32),
                        pltpu.VMEM((seq, tq), jnp.int16),
                        pltpu.VMEM((tk, DSA_HEADS * tq), F32), pltpu.VMEM((tk, DSA_HEADS * tq), F32)],
        compiler_params=_params(("parallel", "parallel")),
        name="dsa",
    )(qdt, iqt, iwt, dkv, dkvt, ikw)


def _mem_kv_kernel(mem_ref, g_ref, w_ref, kt_ref, v_ref):
    mn = _rms(mem_ref[0], g_ref[...]).astype(BF16)
    kv = _dot(mn, w_ref[...])
    kt_ref[0] = kv[:, :256].T.astype(BF16)
    v_ref[0] = kv[:, 256:].astype(BF16)


def _mem_kv(mem, gain, w_kv):
    b, m, d = mem.shape
    return pl.pallas_call(
        _mem_kv_kernel,
        out_shape=(jax.ShapeDtypeStruct((b, 256, m), BF16), jax.ShapeDtypeStruct((b, m, 256), BF16)),
        grid=(b,),
        in_specs=[pl.BlockSpec((1, m, d), lambda i: (i, 0, 0)), _const_spec((1, d)), _const_spec((d, 512))],
        out_specs=(pl.BlockSpec((1, 256, m), lambda i: (i, 0, 0)), pl.BlockSpec((1, m, 256), lambda i: (i, 0, 0))),
        compiler_params=_params(("parallel",)),
        name="mem_kv",
    )(mem, gain, w_kv)


def _merge_kernel(x_ref, ya_ref, yb_ref, yc_ref, yd_ref, gmix_ref, wg_ref, wbr_ref, wout_ref,
                  gq_ref, wq_ref, mkt_ref, mv_ref, wo_ref, gffn_ref, wr_ref, br_ref,
                  x2_ref, hp_ref, rt_ref, cnt_ref, run_ref, *, tm):
    x = x_ref[...]
    d = x.shape[-1]
    h = _rms(x, gmix_ref[...]).astype(BF16)
    merged = jnp.zeros((tm, d), F32)
    for n, y_ref in enumerate((ya_ref, yb_ref, yc_ref, yd_ref)):
        wn = wg_ref[:, n * d:n * d + d + LANES][:, GATE_SHIFT:GATE_SHIFT + d]
        gate = _sigmoid(_dot(h, wn))
        merged = merged + gate * _dot(y_ref[...], wbr_ref[n])
    x1 = x + _dot(merged.astype(BF16), wout_ref[...])
    h2 = _rms(x1, gq_ref[...]).astype(BF16)
    q = (_dot(h2, wq_ref[...]) * (MEM_HEAD_DIM ** -0.5)).astype(BF16)
    lane_head = lax.broadcasted_iota(jnp.int32, (tm, 256), 1) // MEM_HEAD_DIM
    mv = mv_ref[0]
    o = jnp.zeros((tm, 256), F32)
    for hd in range(MEM_HEADS):
        s = _dot(q[:, hd * MEM_HEAD_DIM:(hd + 1) * MEM_HEAD_DIM], mkt_ref[0, hd * MEM_HEAD_DIM:(hd + 1) * MEM_HEAD_DIM, :])
        p = jnp.exp(s - jnp.max(s, axis=-1, keepdims=True))
        p = p / jnp.sum(p, axis=-1, keepdims=True)
        o = o + jnp.where(lane_head == hd, _dot(p.astype(BF16), mv), 0.0)
    x2 = x1 + _dot(o.astype(BF16), wo_ref[...])
    x2_ref[...] = x2
    h3f = _rms(x2, gffn_ref[...])
    hp_ref[...] = _pack_bf16_pairs(h3f)
    h3 = h3f.astype(BF16)
    logits = _dot(h3, wr_ref[...]) + br_ref[...]
    lane = lax.broadcasted_iota(jnp.int32, (tm, LANES), 1)
    gl = jnp.where(lane < MOE_GROUPS, logits, -jnp.inf)
    gmax = jnp.max(gl, axis=-1, keepdims=True)
    gsel = jnp.min(jnp.where(gl == gmax, lane, LANES), axis=-1, keepdims=True)
    pg_sel = 1.0 / jnp.sum(jnp.exp(gl - gmax), axis=-1, keepdims=True)
    in_group = (lane - ROUTER_EXPERT_LANE) // MOE_EXPERTS_PER_GROUP == gsel
    el = jnp.where(in_group, logits, -jnp.inf)
    m1 = jnp.max(el, axis=-1, keepdims=True)
    i1 = jnp.min(jnp.where(el == m1, lane, LANES), axis=-1, keepdims=True)
    el2 = jnp.where(lane == i1, -jnp.inf, el)
    m2 = jnp.max(el2, axis=-1, keepdims=True)
    i2 = jnp.min(jnp.where(el2 == m2, lane, LANES), axis=-1, keepdims=True)
    e21 = jnp.exp(m2 - m1)
    c1 = pg_sel / (1.0 + e21)
    @pl.when(pl.program_id(0) == 0)
    def _():
        run_ref[...] = jnp.zeros_like(run_ref)

    oh1 = jnp.where(lane == i1, 1.0, 0.0)
    oh2 = jnp.where(lane == i2, 1.0, 0.0)
    both = oh1 + oh2
    tr = lax.broadcasted_iota(jnp.int32, (LANES, LANES), 0)
    tc = lax.broadcasted_iota(jnp.int32, (LANES, LANES), 1)
    earlier = jnp.where(tc < tr, 1.0, 0.0).astype(BF16)
    seen = run_ref[0:1, :]
    r1, r2 = [], []
    for g0 in range(0, tm, LANES):
        grp = both[g0:g0 + LANES, :]
        before = _dot(earlier, grp.astype(BF16)) + seen
        r1.append(jnp.sum(oh1[g0:g0 + LANES, :] * before, axis=-1, keepdims=True))
        r2.append(jnp.sum(oh2[g0:g0 + LANES, :] * before, axis=-1, keepdims=True))
        seen = seen + jnp.sum(grp, axis=0, keepdims=True)
    r1, r2 = jnp.concatenate(r1, axis=0), jnp.concatenate(r2, axis=0)
    total = jnp.broadcast_to(seen, run_ref.shape)
    run_ref[...] = total
    cnt_ref[...] = total
    ids = (jnp.where(lane == 0, i1, i2) - ROUTER_EXPERT_LANE).astype(F32)
    rt_ref[...] = jnp.where(lane < 2, ids, jnp.where(lane == 2, c1, jnp.where(lane == 3, c1 * e21,
                            jnp.where(lane == 4, r1, jnp.where(lane == 5, r2, 0.0)))))


def _merge(x, ys, gmix, wg, wbr, wout, gq, wq, mkt, mv, wo, gffn, wr, br, batch, seq):
    n, d = x.shape
    tm = MERGE_TM
    spt = seq // tm
    m = mv.shape[1]
    tok = lambda w: pl.BlockSpec((tm, w), lambda i: (i, 0))
    return pl.pallas_call(
        functools.partial(_merge_kernel, tm=tm),
        out_shape=(jax.ShapeDtypeStruct((n, d), F32), jax.ShapeDtypeStruct((n, d // 2), F32),
                   jax.ShapeDtypeStruct((n, LANES), F32), jax.ShapeDtypeStruct((8, LANES), F32)),
        grid=(n // tm,),
        in_specs=[tok(d), tok(256), tok(256), tok(256), tok(256),
                  _const_spec((1, d)), _const_spec((d, N_BRANCH * d + LANES)), _const_spec((N_BRANCH, 256, d)),
                  _const_spec((d, d)), _const_spec((1, d)), _const_spec((d, 256)),
                  pl.BlockSpec((1, 256, m), lambda i: (i // spt, 0, 0)),
                  pl.BlockSpec((1, m, 256), lambda i: (i // spt, 0, 0)),
                  _const_spec((256, d)), _const_spec((1, d)), _const_spec((d, LANES)), _const_spec((1, LANES))],
        out_specs=(tok(d), tok(d // 2), tok(LANES), pl.BlockSpec((8, LANES), lambda i: (0, 0))),
        scratch_shapes=[pltpu.VMEM((8, LANES), F32)],
        compiler_params=_params(("arbitrary",)),
        name="merge_mem_router",
    )(x, *ys, gmix, wg, wbr, wout, gq, wq, mkt, mv, wo, gffn, wr, br)


def _sc_gather_rows(table, idx):
    _, width = table.shape
    total = idx.shape[0]
    chunk, nbuf = SC_GATHER_CHUNK, SC_GATHER_BUFS
    workers = SC_CORES * SC_SUBCORES
    per_w = total // workers
    nch = per_w // chunk
    assert total % (workers * chunk * nbuf) == 0
    mesh = plsc.VectorSubcoreMesh(core_axis_name="c", subcore_axis_name="s")

    @functools.partial(
        pl.kernel, mesh=mesh, out_type=jax.ShapeDtypeStruct((total, width), table.dtype),
        scratch_types=[pltpu.VMEM((nch, chunk), jnp.int32), pltpu.VMEM((nbuf, chunk, width), table.dtype),
                       pltpu.SemaphoreType.DMA((nbuf,)), pltpu.SemaphoreType.DMA((nbuf,))])
    def gather_kernel(table_hbm, idx_hbm, out_hbm, idx_v, rows_v, gsem, wsem):
        wid = lax.axis_index("s") * SC_CORES + lax.axis_index("c")
        pltpu.sync_copy(idx_hbm.at[wid], idx_v)

        def gather(j, slot):
            return pltpu.make_async_copy(table_hbm.at[idx_v.at[j]], rows_v.at[slot], gsem.at[slot])

        def write(j, slot):
            off = pl.multiple_of(wid * per_w + j * chunk, chunk)
            return pltpu.make_async_copy(rows_v.at[slot], out_hbm.at[pl.ds(off, chunk)], wsem.at[slot])

        for slot in range(nbuf):
            gather(slot, slot).start()

        @pl.loop(0, nch // nbuf)
        def _(g):
            for slot in range(nbuf):
                j = g * nbuf + slot
                gather(j, slot).wait()
                write(j, slot).start()
                write(j, slot).wait()

                @pl.when(j + nbuf < nch)
                def _():
                    gather(j + nbuf, slot).start()

    return gather_kernel(table, idx.reshape(workers, nch, chunk))


def _sc_scatter_rows(table, dest2, total):
    n, width = table.shape
    chunk, nbuf = SC_GATHER_CHUNK, SC_GATHER_BUFS
    workers = SC_CORES * SC_SUBCORES
    per_w = n // workers
    nch = per_w // chunk
    assert n % (workers * chunk * nbuf) == 0
    mesh = plsc.VectorSubcoreMesh(core_axis_name="c", subcore_axis_name="s")

    @functools.partial(
        pl.kernel, mesh=mesh, out_type=jax.ShapeDtypeStruct((total, width), table.dtype),
        scratch_types=[pltpu.VMEM((2, nch, chunk), jnp.int32), pltpu.VMEM((nbuf, chunk, width), table.dtype),
                       pltpu.SemaphoreType.DMA((nbuf,)), pltpu.SemaphoreType.DMA((nbuf,))])
    def scatter_kernel(table_hbm, idx_hbm, out_hbm, idx_v, rows_v, rsem, wsem):
        wid = lax.axis_index("s") * SC_CORES + lax.axis_index("c")
        pltpu.sync_copy(idx_hbm.at[wid], idx_v)

        def read(j, slot):
            off = pl.multiple_of(wid * per_w + j * chunk, chunk)
            return pltpu.make_async_copy(table_hbm.at[pl.ds(off, chunk)], rows_v.at[slot], rsem.at[slot])

        def write(j, slot, k):
            return pltpu.make_async_copy(rows_v.at[slot], out_hbm.at[idx_v.at[k, j]], wsem.at[slot])

        for slot in range(nbuf):
            read(slot, slot).start()

        @pl.loop(0, nch // nbuf)
        def _(g):
            for slot in range(nbuf):
                j = g * nbuf + slot
                read(j, slot).wait()
                write(j, slot, 0).start()
                write(j, slot, 1).start()
                write(j, slot, 0).wait()
                write(j, slot, 1).wait()

                @pl.when(j + nbuf < nch)
                def _():
                    read(j + nbuf, slot).start()

    idx = dest2.reshape(2, workers, nch, chunk).transpose(1, 0, 2, 3)
    return scatter_kernel(table, idx)


def _dispatch_plan(rt, cnt, n):
    ne, blk = MOE_N_EXPERTS, MOE_BLOCK
    n_blocks = (2 * n) // blk + ne
    experts = jnp.arange(ne, dtype=jnp.int32)
    counts = cnt[0, ROUTER_EXPERT_LANE:ROUTER_EXPERT_LANE + ne].astype(jnp.int32)
    padded = (counts + blk - 1) // blk * blk
    pend = jnp.cumsum(padded)
    pstart = pend - padded
    ids = rt[:, 0:2].astype(jnp.int32)
    pos = rt[:, 4:6].astype(jnp.int32)
    first_row = jnp.sum(jnp.where(ids[:, :, None] == experts[None, None, :], pstart[None, None, :], 0), axis=-1)
    dest2 = (first_row + pos).T
    b0 = jnp.arange(n_blocks, dtype=jnp.int32) * blk
    block_e = jnp.minimum(jnp.sum((pend[None, :] <= b0[:, None]).astype(jnp.int32), axis=1), ne - 1)
    n_valid = jnp.clip(counts[block_e] - (b0 - pstart[block_e]), 0, blk).astype(jnp.int32)
    return dest2, block_e, n_valid


def _expert_block_kernel(be_ref, nv_ref, xs_ref, wg_ref, wu_ref, wd_ref, o_ref, wgb_ref, wub_ref, wdb_ref):
    b = pl.program_id(0)
    valid = nv_ref[b]

    @pl.when((b == 0) | (be_ref[b] != be_ref[jnp.maximum(b - 1, 0)]))
    def _():
        wgb_ref[...] = wg_ref[0].astype(BF16)
        wub_ref[...] = wu_ref[0].astype(BF16)
        wdb_ref[...] = wd_ref[0].astype(BF16)

    @pl.when(valid > 0)
    def _():
        row = lax.broadcasted_iota(jnp.int32, xs_ref.shape, 0)
        words = jnp.where(row < valid, xs_ref[...], 0.0)
        h = _unpack_bf16_pairs(words).astype(BF16)
        gt = _dot(h, wgb_ref[...])
        hid = gt * _sigmoid(gt) * _dot(h, wub_ref[...])
        o_ref[...] = _pack_bf16_pairs(_dot(hid.astype(BF16), wdb_ref[...]))

    @pl.when(valid == 0)
    def _():
        o_ref[...] = jnp.zeros_like(o_ref)


def _expert_blocks(xs, wg, wu, wd, layer, block_e, n_used):
    p_rows, half = xs.shape
    d, hid = wg.shape[-2:]
    blk = MOE_BLOCK
    grid_spec = pltpu.PrefetchScalarGridSpec(
        num_scalar_prefetch=2, grid=(p_rows // blk,),
        in_specs=[pl.BlockSpec((blk, half), lambda b, be, nu: (b, 0)),
                  pl.BlockSpec((None, 1, d, hid), lambda b, be, nu: (layer, be[b], 0, 0)),
                  pl.BlockSpec((None, 1, d, hid), lambda b, be, nu: (layer, be[b], 0, 0)),
                  pl.BlockSpec((None, 1, hid, d), lambda b, be, nu: (layer, be[b], 0, 0))],
        out_specs=pl.BlockSpec((blk, half), lambda b, be, nu: (b, 0)),
        scratch_shapes=[pltpu.VMEM((d, hid), BF16), pltpu.VMEM((d, hid), BF16), pltpu.VMEM((hid, d), BF16)])
    return pl.pallas_call(
        _expert_block_kernel, out_shape=jax.ShapeDtypeStruct((p_rows, half), F32), grid_spec=grid_spec,
        compiler_params=_params(("arbitrary",)),
        name="moe_expert_blocks",
    )(block_e, n_used, xs, wg, wu, wd)


def _combine_kernel(x_ref, y1_ref, y2_ref, rt_ref, gfin_ref, o_ref, *, final_norm):
    out = (x_ref[...] + rt_ref[:, 2:3] * _unpack_bf16_pairs(y1_ref[...])
           + rt_ref[:, 3:4] * _unpack_bf16_pairs(y2_ref[...]))
    o_ref[...] = _rms(out, gfin_ref[...]) if final_norm else out


def _combine(x2, y_halves, rt, gfin, final_norm):
    n, d = x2.shape
    tm = COMBINE_TM
    nt = n // tm
    return pl.pallas_call(
        functools.partial(_combine_kernel, final_norm=final_norm),
        out_shape=jax.ShapeDtypeStruct((n, d), F32),
        grid=(nt,),
        in_specs=[pl.BlockSpec((tm, d), lambda i: (i, 0)),
                  pl.BlockSpec((tm, d // 2), lambda i: (i, 0)),
                  pl.BlockSpec((tm, d // 2), lambda i: (i + nt, 0)),
                  pl.BlockSpec((tm, LANES), lambda i: (i, 0)), _const_spec((1, d))],
        out_specs=pl.BlockSpec((tm, d), lambda i: (i, 0)),
        compiler_params=_params(("parallel",)), name="moe_combine",
    )(x2, y_halves, y_halves, rt, gfin)


def _moe(x2, hp, rt, cnt, wg, wu, wd, layer, gfin, final_norm):
    n = x2.shape[0]
    dest2, block_e, n_valid = _dispatch_plan(rt, cnt, n)
    xs = _sc_scatter_rows(hp, dest2, block_e.shape[0] * MOE_BLOCK)
    yb = _expert_blocks(xs, wg, wu, wd, layer, block_e, n_valid)
    y_halves = _sc_gather_rows(yb, dest2.reshape(2 * n))
    return _combine(x2, y_halves, rt, gfin, final_norm)


def kernel(x, mem, positions, norm_mix, w_in, diff_lambda, hgrn_lb_logits, spatial_w, spatial_b, w_branch, w_out,
           norm_mem_q, norm_mem_kv, w_mem_q, w_mem_kv, w_mem_o, norm_ffn, w_router_group, b_router_group,
           w_router_expert, b_router_expert, w_exp_gate, w_exp_up, w_exp_down, norm_final):
    batch, seq, d = x.shape
    depth = w_in.shape[0]
    n = batch * seq
    xf = x.reshape(n, d)
    tabs = _rope_tables(positions)
    w_all = _w_in_bf16(w_in)
    row = lambda v: v.reshape(1, -1).astype(F32)
    for l in range(depth):
        lam_init = 0.8 - 0.6 * math.exp(-0.3 * l)
        g0 = C_GATE - GATE_SHIFT
        w_gate = lax.slice(w_all, (l, 0, g0), (l + 1, d, g0 + N_BRANCH * d + LANES)).reshape(d, -1)
        sw = spatial_w[l].reshape(SGU_GROUPS * SGU_CHUNK, SGU_CHUNK)
        sb = jnp.repeat(spatial_b[l].T, SGU_GROUP_DIM, axis=1)
        qat, ka, vat, hb, y_c, qdt, iqt, dkv, dkvt, ikw, iwt = _projection(
            xf, row(norm_mix[l]), w_all, l, tabs, sw, sb, batch, seq)
        y_a = _diff_attention(diff_lambda[l], qat, ka, vat, lam_init, batch, seq)
        y_b = _hgrn(hgrn_lb_logits, hb, l, batch, seq)
        y_d = _dsa(qdt, iqt, iwt, dkv, dkvt, ikw, batch, seq)
        mkt, mv = _mem_kv(mem, row(norm_mem_kv[l]), w_mem_kv[l].astype(BF16))
        e0, e1 = ROUTER_EXPERT_LANE, ROUTER_EXPERT_LANE + MOE_N_EXPERTS
        wr = jnp.zeros((d, LANES), F32)
        wr = wr.at[:, :MOE_GROUPS].set(w_router_group[l]).at[:, e0:e1].set(w_router_expert[l]).astype(BF16)
        br = jnp.zeros((1, LANES), F32)
        br = br.at[0, :MOE_GROUPS].set(b_router_group[l]).at[0, e0:e1].set(b_router_expert[l])
        x2, hp, rt, cnt = _merge(xf, (y_a, y_b, y_c, y_d), row(norm_mix[l]), w_gate, w_branch[l].astype(BF16),
                                 w_out[l].astype(BF16), row(norm_mem_q[l]), w_mem_q[l].astype(BF16), mkt, mv,
                                 w_mem_o[l].astype(BF16), row(norm_ffn[l]), wr, br, batch, seq)
        xf = _moe(x2, hp, rt, cnt, w_exp_gate, w_exp_up, w_exp_down, l,
                  row(norm_final), final_norm=(l == depth - 1))
    return xf.reshape(batch, seq, d)
```

```python
import functools
import math

import numpy as np
import jax
import jax.numpy as jnp
from jax import lax
from jax.experimental import pallas as pl
from jax.experimental.pallas import tpu as pltpu
from jax.experimental.pallas import tpu_sc as plsc

F32 = jnp.float32
BF16 = jnp.bfloat16

NORM_EPS = 1e-6
ROPE_THETA = 10000.0
NEG_BIG = -1e30

N_BRANCH = 4
DIFF_HEADS = 4
DIFF_HEAD_DIM = 32
HGRN_DIM = 64
HGRN_CHUNK = 32
HGRN_UNROLL = 8
HGRN_MIN_FORGET = 1e-30
SGU_GROUPS = 4
SGU_GROUP_DIM = 64
SGU_CHUNK = 128
DSA_HEADS = 4
DSA_HEAD_DIM = 64
DSA_IDX_HEADS = 4
DSA_IDX_DIM = 32
DSA_TOPK = 256
MEM_HEADS = 4
MEM_HEAD_DIM = 64
MOE_GROUPS = 4
MOE_EXPERTS_PER_GROUP = 8
MOE_N_EXPERTS = 32
MOE_BLOCK = 512
ROUTER_EXPERT_LANE = 32
SC_CORES = 2
SC_SUBCORES = 16
SC_GATHER_CHUNK = 16
SC_GATHER_BUFS = 4

LANES = 128
VMEM_LIMIT = 56 * 1024 * 1024

PROJ_TM = 1024
DIFF_TQ = 512
DIFF_TK = 512
HGRN_TC = 1024
DSA_TQ = 512
DSA_TK = 512
MERGE_TM = 1024
COMBINE_TM = 1024
ROPE_TM = 1024
W_IN_COLS = 256

C_AQ, C_AK, C_AV = 0, 256, 512
C_HB = 768
C_UV = 1792
C_DQ = 2304
C_DKV = 2560
C_IQ = 2688
C_IKW = 2816
IW_LANE = 32
C_GATE = 2852
GATE_SHIFT = C_GATE % 128
C_TOTAL = 2944
LOG2E = math.log2(math.e)


def _params(sem):
    return pltpu.CompilerParams(dimension_semantics=sem, vmem_limit_bytes=VMEM_LIMIT)


def _const_spec(shape):
    nd = len(shape)
    return pl.BlockSpec(shape, lambda *_: (0,) * nd, pipeline_mode=pl.Buffered(1))


def _rms(xf, gain=None):
    y = xf * lax.rsqrt(jnp.mean(xf * xf, axis=-1, keepdims=True) + NORM_EPS)
    return y if gain is None else y * gain


def _sigmoid(x):
    return 0.5 * jnp.tanh(0.5 * x) + 0.5


def _pack_bf16_pairs(x):
    w = x.shape[-1] // 2
    xb = x.astype(BF16).astype(F32)
    lo = lax.shift_right_logical(pltpu.bitcast(xb[:, :w], jnp.int32), 16)
    hi = pltpu.bitcast(xb[:, w:], jnp.int32) & jnp.int32(-65536)
    return pltpu.bitcast(hi | lo, F32)


def _unpack_bf16_pairs(words):
    bits = pltpu.bitcast(words, jnp.int32)
    lo = pltpu.bitcast(bits << 16, F32)
    hi = pltpu.bitcast(bits & jnp.int32(-65536), F32)
    return jnp.concatenate([lo, hi], axis=1)


def _dot(a, b):
    return jnp.dot(a, b, preferred_element_type=F32)


def _dot_nt(a, b):
    return lax.dot_general(a, b, (((1,), (1,)), ((), ())), preferred_element_type=F32)


def _rope_table_kernel(pos_ref, frq_ref, sgn_ref, c32_ref, s32_ref, c64_ref, s64_ref):
    ang = pos_ref[...].astype(F32) * frq_ref[...]
    cos = jnp.cos(ang)
    sin = jnp.sin(ang) * sgn_ref[...]
    four = lambda t: jnp.concatenate([t, t, t, t], axis=1)
    c64_ref[...] = four(cos[:, :64])
    s64_ref[...] = four(sin[:, :64])
    c32_ref[...] = four(cos[:, 64:])
    s32_ref[...] = four(sin[:, 64:])


def _rope_tables(positions):
    n = positions.size
    pos = positions.reshape(n, 1).astype(jnp.int32)
    lane = np.arange(64)
    inv32 = ROPE_THETA ** (-jnp.arange(16, dtype=F32) * (2.0 / 32))
    inv64 = ROPE_THETA ** (-jnp.arange(32, dtype=F32) * (2.0 / 64))
    frq = jnp.concatenate([inv64[lane % 32], inv32[lane % 16]]).reshape(1, LANES)
    sgn = jnp.asarray(np.concatenate([np.where(lane % 64 < 32, -1.0, 1.0),
                                      np.where(lane % 32 < 16, -1.0, 1.0)]), F32).reshape(1, LANES)
    tm = ROPE_TM
    tab = jax.ShapeDtypeStruct((n, 256), F32)
    return pl.pallas_call(
        _rope_table_kernel,
        out_shape=(tab, tab, tab, tab),
        grid=(n // tm,),
        in_specs=[pl.BlockSpec((tm, 1), lambda i: (i, 0)), _const_spec((1, LANES)), _const_spec((1, LANES))],
        out_specs=tuple(pl.BlockSpec((tm, 256), lambda i: (i, 0)) for _ in range(4)),
        compiler_params=_params(("parallel",)),
        name="rope_tables",
    )(pos, frq, sgn)


def _w_in_kernel(wt_ref, o_ref, *, width):
    rows = wt_ref.shape[0]
    row = lax.broadcasted_iota(jnp.int32, (rows, wt_ref.shape[-1]), 0)
    valid = row < width - pl.program_id(0) * rows
    for layer in range(o_ref.shape[0]):
        o_ref[layer] = jnp.where(valid, wt_ref[:, layer, :], 0.0).T.astype(BF16)


def _w_in_bf16(w_in):
    depth, d, width = w_in.shape
    cols = W_IN_COLS
    nblk = pl.cdiv(width, cols)
    return pl.pallas_call(
        functools.partial(_w_in_kernel, width=width), out_shape=jax.ShapeDtypeStruct((depth, d, nblk * cols), BF16),
        grid=(nblk,), in_specs=[pl.BlockSpec((cols, depth, d), lambda i: (i, 0, 0))],
        out_specs=pl.BlockSpec((depth, d, cols), lambda i: (0, 0, i)),
        compiler_params=_params(("parallel",)), name="w_in_bf16",
    )(jnp.transpose(w_in, (2, 0, 1)))


def _gelu_tanh(x):
    return 0.5 * x * (1.0 + jnp.tanh(math.sqrt(2.0 / math.pi) * (x + 0.044715 * (x * x * x))))


def _rope(x, cos, sin_signed, half):
    w = x.shape[-1]
    lane = lax.broadcasted_iota(jnp.int32, x.shape, 1)
    partner = jnp.where(lane % (2 * half) < half, pltpu.roll(x, w - half, 1), pltpu.roll(x, half, 1))
    return x * cos + partner * sin_signed


def _proj_kernel(x_ref, g_ref, w_ref, c32_ref, s32_ref, c64_ref, s64_ref, sw_ref, sb_ref,
                 qat_ref, ka_ref, vat_ref, hb_ref, yc_ref, qdt_ref, iqt_ref, dkv_ref, dkvt_ref, ikw_ref, iwt_ref,
                 *, tm):
    h = _rms(x_ref[...], g_ref[...]).astype(BF16)

    def proj(c0, width):
        return _dot(h, w_ref[:, c0:c0 + width])

    c32, s32, c64, s64 = c32_ref[...], s32_ref[...], c64_ref[...], s64_ref[...]
    qat_ref[0] = (_rope(proj(C_AQ, 256), c32, s32, 16) * (DIFF_HEAD_DIM ** -0.5 * LOG2E)).T.astype(BF16)
    ka_ref[...] = _rope(proj(C_AK, 256), c32, s32, 16).astype(BF16)
    vat_ref[0] = proj(C_AV, 256).astype(BF16).T
    hb_ref[...] = proj(C_HB, 1024)
    qdt_ref[0] = (_rope(proj(C_DQ, 256), c64, s64, 32) * (DSA_HEAD_DIM ** -0.5 * LOG2E)).T.astype(BF16)
    iqt_ref[0] = _rope(proj(C_IQ, 128), c32[:, :128], s32[:, :128], 16).T.astype(BF16)
    lane = lax.broadcasted_iota(jnp.int32, (tm, 128), 1)
    is_k = lane < DSA_HEAD_DIM
    dkv = _rope(proj(C_DKV, 128), jnp.where(is_k, c64[:, :128], 1.0), jnp.where(is_k, s64[:, :128], 0.0), 32)
    is_ik = lane < DSA_IDX_DIM
    ikw = _rope(proj(C_IKW, 128), jnp.where(is_ik, c32[:, :128], 1.0), jnp.where(is_ik, s32[:, :128], 0.0), 16)
    dkv_ref[...] = dkv.astype(BF16)
    dkvt_ref[0] = dkv.T.astype(BF16)
    ikw_ref[...] = ikw.astype(BF16)
    iw_scale = DSA_IDX_HEADS ** -0.5 * DSA_IDX_DIM ** -0.5
    iwt_ref[0] = (ikw * iw_scale).T[IW_LANE:IW_LANE + 8, :]
    uv = _gelu_tanh(proj(C_UV, 512))
    u, v = uv[:, :256], uv[:, 256:]
    mu = jnp.mean(v, axis=-1, keepdims=True)
    vc = v - mu
    vn = (vc * lax.rsqrt(jnp.mean(vc * vc, axis=-1, keepdims=True) + NORM_EPS)).astype(BF16)
    r = lax.broadcasted_iota(jnp.int32, (SGU_GROUPS * SGU_CHUNK, SGU_CHUNK), 0)
    c = lax.broadcasted_iota(jnp.int32, (SGU_GROUPS * SGU_CHUNK, SGU_CHUNK), 1)
    wt = jnp.where((r % SGU_CHUNK) >= c, sw_ref[...], 0.0).astype(BF16)
    lane_grp = lax.broadcasted_iota(jnp.int32, (SGU_CHUNK, 256), 1) // SGU_GROUP_DIM
    for ch in range(tm // SGU_CHUNK):
        r0 = ch * SGU_CHUNK
        full = _dot(wt, vn[r0:r0 + SGU_CHUNK, :])
        mixed = sb_ref[...]
        for g in range(SGU_GROUPS):
            mixed = mixed + jnp.where(lane_grp == g, full[g * SGU_CHUNK:(g + 1) * SGU_CHUNK, :], 0.0)
        yc_ref[r0:r0 + SGU_CHUNK, :] = (u[r0:r0 + SGU_CHUNK, :] * mixed).astype(BF16)


def _projection(x, gain, w_all, layer, tabs, sw, sb, batch, seq):
    n, d = x.shape
    tm = PROJ_TM
    spt = seq // tm
    tok = lambda w: pl.BlockSpec((tm, w), lambda i: (i, 0))
    tr = lambda rows: pl.BlockSpec((1, rows, tm), lambda i: (i // spt, 0, i % spt))
    out_shape = (
        jax.ShapeDtypeStruct((batch, 256, seq), BF16),
        jax.ShapeDtypeStruct((n, 256), BF16),
        jax.ShapeDtypeStruct((batch, 256, seq), BF16),
        jax.ShapeDtypeStruct((n, 1024), F32),
        jax.ShapeDtypeStruct((n, 256), BF16),
        jax.ShapeDtypeStruct((batch, 256, seq), BF16),
        jax.ShapeDtypeStruct((batch, 128, seq), BF16),
        jax.ShapeDtypeStruct((n, 128), BF16),
        jax.ShapeDtypeStruct((batch, 128, seq), BF16),
        jax.ShapeDtypeStruct((n, 128), BF16),
        jax.ShapeDtypeStruct((batch, 8, seq), F32),
    )
    return pl.pallas_call(
        functools.partial(_proj_kernel, tm=tm),
        out_shape=out_shape,
        grid=(n // tm,),
        in_specs=[tok(d), _const_spec((1, d)),
                  pl.BlockSpec((None, d, C_TOTAL), lambda i: (layer, 0, 0), pipeline_mode=pl.Buffered(1)),
                  tok(256), tok(256), tok(256), tok(256),
                  _const_spec((SGU_GROUPS * SGU_CHUNK, SGU_CHUNK)), _const_spec((SGU_CHUNK, 256))],
        out_specs=(tr(256), tok(256), tr(256), tok(1024), tok(256), tr(256), tr(128), tok(128), tr(128), tok(128), tr(8)),
        compiler_params=_params(("parallel",)),
        name="projection",
    )(x, gain, w_all, *tabs, sw, sb)


def _diff_attn_kernel(lam_ref, qt_ref, k_ref, vt_ref, o_ref, sa_ref, sb_ref, *, lam_init, tq, tk):
    q0 = pl.program_id(1) * tq
    kb_diag = q0 // tk
    lv = lam_ref[...]
    lam = (jnp.exp(jnp.sum(lv[0:1] * lv[1:2], axis=-1, keepdims=True))
           - jnp.exp(jnp.sum(lv[2:3] * lv[3:4], axis=-1, keepdims=True)) + lam_init)
    qt = qt_ref[0]
    feat = lax.broadcasted_iota(jnp.int32, (256, tq), 0) // DIFF_HEAD_DIM
    n_maps = 2 * DIFF_HEADS
    qz = jnp.concatenate([jnp.where(feat == i, qt, jnp.zeros_like(qt)) for i in range(n_maps)], axis=1)
    wide = n_maps * tq
    key_i = lax.broadcasted_iota(jnp.int32, (tk, wide), 0)
    qry_i = q0 + lax.broadcasted_iota(jnp.int32, (tk, wide), 1) % tq

    def scores(kb, dst_ref):
        k0 = pl.multiple_of(kb * tk, tk)
        dst_ref[...] = _dot(k_ref[pl.ds(k0, tk), :], qz)

    def absorb(src_ref, kb, carry, masked):
        m_i, l_i, acc = carry
        k0 = pl.multiple_of(kb * tk, tk)
        s = src_ref[...]
        if masked:
            s = jnp.where(k0 + key_i <= qry_i, s, NEG_BIG)
        m_new = jnp.maximum(m_i, jnp.max(s, axis=0, keepdims=True))
        p = jnp.exp2(s - m_new)
        alpha = jnp.exp2(m_i - m_new)
        l_new = alpha * l_i + jnp.sum(p, axis=0, keepdims=True)
        pb = p.astype(BF16)
        pv = jnp.concatenate(
            [_dot(vt_ref[0, hd * 64:(hd + 1) * 64, pl.ds(k0, tk)], pb[:, 2 * hd * tq:(2 * hd + 2) * tq])
             for hd in range(DIFF_HEADS)], axis=1)
        return m_new, l_new, alpha * acc + pv

    def pair(j, carry):
        kb = 2 * j
        scores(kb + 1, sb_ref)
        carry = absorb(sa_ref, kb, carry, False)
        scores(kb + 2, sa_ref)
        return absorb(sb_ref, kb + 1, carry, False)

    init = (jnp.full((1, wide), NEG_BIG, F32), jnp.zeros((1, wide), F32), jnp.zeros((64, wide), F32))
    scores(0, sa_ref)
    carry = lax.fori_loop(0, kb_diag // 2, pair, init)

    def tail_odd(carry):
        scores(kb_diag, sb_ref)
        carry = absorb(sa_ref, kb_diag - 1, carry, False)
        return absorb(sb_ref, kb_diag, carry, True)

    def tail_even(carry):
        return absorb(sa_ref, kb_diag, carry, True)

    _, l_f, acc = lax.cond(kb_diag % 2 == 1, tail_odd, tail_even, carry)
    o_all = acc / l_f
    heads = []
    for hd in range(DIFF_HEADS):
        o0 = o_all[:, 2 * hd * tq:(2 * hd + 1) * tq]
        o1 = o_all[:, (2 * hd + 1) * tq:(2 * hd + 2) * tq]
        o_h = o0 - lam * o1
        ms = jnp.mean(o_h * o_h, axis=0, keepdims=True)
        heads.append(o_h * lax.rsqrt(ms + NORM_EPS) * (1.0 - lam_init))
    o_ref[...] = jnp.concatenate(heads, axis=0).T.astype(BF16)


def _diff_attention(lam_vec, qat, ka, vat, lam_init, batch, seq):
    tq, tk = DIFF_TQ, DIFF_TK
    nq = seq // tq
    return pl.pallas_call(
        functools.partial(_diff_attn_kernel, lam_init=lam_init, tq=tq, tk=tk),
        out_shape=jax.ShapeDtypeStruct((batch * seq, 256), BF16),
        grid=(batch, nq),
        in_specs=[_const_spec((4, DIFF_HEAD_DIM)),
                  pl.BlockSpec((1, 256, tq), lambda b, i: (b, 0, i)),
                  pl.BlockSpec((seq, 256), lambda b, i: (b, 0)),
                  pl.BlockSpec((1, 256, seq), lambda b, i: (b, 0, 0))],
        out_specs=pl.BlockSpec((tq, 256), lambda b, i: (b * nq + i, 0)),
        scratch_shapes=[pltpu.VMEM((tk, 2 * DIFF_HEADS * tq), F32), pltpu.VMEM((tk, 2 * DIFF_HEADS * tq), F32)],
        compiler_params=_params(("parallel", "parallel")),
        name="diff_attention",
    )(lam_vec, qat, ka, vat)


def _hgrn_kernel(lbl_ref, hb_ref, o_ref, st_ref, pstk_ref, *, layer, tc):
    cz = HGRN_CHUNK
    w = 256

    @pl.when(pl.program_id(1) == 0)
    def _():
        st_ref[...] = jnp.zeros_like(st_ref)

    lg = lbl_ref[...]
    e = jnp.exp(lg - jnp.max(lg, axis=0, keepdims=True))
    lw = e / jnp.sum(e, axis=0, keepdims=True)
    lb = jnp.sum(lw[0:layer + 1], axis=0, keepdims=True) - lw[0:1]

    scan_row = lax.broadcasted_iota(jnp.int32, (cz, w), 0)
    rb = lax.broadcasted_iota(jnp.int32, (w, w), 0) // HGRN_DIM
    cb = lax.broadcasted_iota(jnp.int32, (w, w), 1) // HGRN_DIM
    same_head = rb == cb
    head_ones = same_head.astype(BF16)
    trows = {r: r + lax.broadcasted_iota(jnp.int32, (16, w), 0) for r in range(0, cz, 16)}

    def chunk(c, carry):
        r0 = pl.multiple_of(c * cz, cz)
        q = hb_ref[pl.ds(r0, cz), 0:256]
        fp = hb_ref[pl.ds(r0, cz), 256:512]
        v = hb_ref[pl.ds(r0, cz), 512:768]
        g = hb_ref[pl.ds(r0, cz), 768:1024]
        qf = q * _sigmoid(q)
        f = lb + (1.0 - lb) * jax.nn.sigmoid(fp)
        log_f = jnp.log(jnp.maximum(f, HGRN_MIN_FORGET))
        kf = (1.0 - lb) * jax.nn.sigmoid(-fp)
        bc = log_f
        step = 1
        while step < cz:
            bc = bc + jnp.where(scan_row >= step, pltpu.roll(bc, step, 0), 0.0)
            step *= 2
        st = st_ref[...]
        o = _dot_nt((qf * jnp.exp(bc)).astype(BF16), st.astype(BF16))
        later = {}
        for blk in range(1, cz // 16):
            r = bc[16 * blk - 1:16 * blk, :]
            later[blk] = (qf[16 * blk:16 * blk + 16, :] * jnp.exp(bc[16 * blk:16 * blk + 16, :] - r),
                          kf[:16 * blk, :] * jnp.exp(r - bc[:16 * blk, :]))
        for s in range(cz):
            r_lo = (s // 16) * 16
            arg = bc[r_lo:r_lo + 16, :] - bc[s:s + 1, :]
            if s > r_lo:
                arg = jnp.where(trows[r_lo] >= s, arg, NEG_BIG)
            p = qf[r_lo:r_lo + 16, :] * kf[s:s + 1, :] * jnp.exp(arg)
            if r_lo:
                pstk_ref[s * cz:s * cz + r_lo, :] = jnp.zeros((r_lo, w), BF16)
            pstk_ref[s * cz + r_lo:s * cz + r_lo + 16, :] = p.astype(BF16)
            for blk in range(s // 16 + 1, cz // 16):
                q_dec, k_dec = later[blk]
                pstk_ref[s * cz + 16 * blk:s * cz + 16 * blk + 16, :] = (q_dec * k_dec[s:s + 1, :]).astype(BF16)
        accs = [jnp.zeros((16, w), F32) for _ in range(cz // 16)]
        for sg in range(cz // 16):
            att = _dot(pstk_ref[sg * 16 * cz:(sg + 1) * 16 * cz, :], head_ones)
            for sl in range(16):
                s = sg * 16 + sl
                for j in range(sg, cz // 16):
                    accs[j] = accs[j] + att[sl * cz + 16 * j:sl * cz + 16 * j + 16, :] * v[s:s + 1, :]
        o = o + jnp.concatenate(accs, axis=0)
        b_end = bc[cz - 1:cz, :]
        kd = kf * jnp.exp(b_end - bc)
        upd = _dot(v.T.astype(BF16), kd.astype(BF16))
        st_ref[...] = st * jnp.exp(b_end) + jnp.where(same_head, upd, 0.0)
        ms = _dot(o * o, head_ones.astype(F32)) * (1.0 / HGRN_DIM)
        y = o * lax.rsqrt(ms + NORM_EPS)
        o_ref[pl.ds(r0, cz), :] = (y * (g * _sigmoid(g))).astype(BF16)
        return carry

    def group(gi, carry):
        for u in range(HGRN_UNROLL):
            chunk(gi * HGRN_UNROLL + u, carry)
        return carry

    lax.fori_loop(0, tc // cz // HGRN_UNROLL, group, 0)


def _hgrn(lb_logits, hb, layer, batch, seq):
    tc = HGRN_TC
    nt = seq // tc
    cz = HGRN_CHUNK
    return pl.pallas_call(
        functools.partial(_hgrn_kernel, layer=layer, tc=tc),
        out_shape=jax.ShapeDtypeStruct((batch * seq, 256), BF16),
        grid=(batch, nt),
        in_specs=[_const_spec(lb_logits.shape),
                  pl.BlockSpec((tc, 1024), lambda b, i: (b * nt + i, 0))],
        out_specs=pl.BlockSpec((tc, 256), lambda b, i: (b * nt + i, 0)),
        scratch_shapes=[pltpu.VMEM((256, 256), F32), pltpu.VMEM((cz * cz, 256), BF16)],
        compiler_params=_params(("parallel", "arbitrary")),
        name="hgrn2",
    )(lb_logits, hb)


def _dsa_kernel(qdt_ref, iqt_ref, iwt_ref, dkv_ref, dkvt_ref, ikw_ref, o_ref, key_ref, bias_ref, half_ref,
                sa_ref, sb_ref, *, tq, tk, n_sel):
    q0 = pl.program_id(1) * tq
    nkb = q0 // tk + 1
    key_i = lax.broadcasted_iota(jnp.int32, (tk, tq), 0)
    qry_i = q0 + lax.broadcasted_iota(jnp.int32, (tk, tq), 1)
    iqt = iqt_ref[0]
    zpad = jnp.zeros((LANES - DSA_IDX_DIM, tq), BF16)
    iqz = jnp.concatenate([jnp.concatenate([iqt[hd * DSA_IDX_DIM:(hd + 1) * DSA_IDX_DIM, :], zpad], axis=0)
                           for hd in range(DSA_IDX_HEADS)], axis=1)
    iw = iwt_ref[0]

    last = nkb - 1

    def score_block(kb, carry, masked):
        k0 = pl.multiple_of(kb * tk, tk)
        sh = jnp.maximum(_dot(ikw_ref[pl.ds(k0, tk), :], iqz), 0.0)
        sc = jnp.zeros((tk, tq), F32)
        for hd in range(DSA_IDX_HEADS):
            sc = sc + sh[:, hd * tq:(hd + 1) * tq] * iw[hd:hd + 1, :]
        sc = sc + 0.0
        if masked:
            sc = jnp.where(k0 + key_i <= qry_i, sc, -jnp.inf)
        bits = pltpu.bitcast(sc, jnp.int32)
        key = jnp.where(bits < 0, bits ^ jnp.int32(0x7FFFFFFF), bits)
        key_ref[pl.ds(k0, tk), :] = key
        half_ref[pl.ds(k0, tk), :] = (key >> 16).astype(jnp.int16)
        return carry

    lax.fori_loop(0, last, functools.partial(score_block, masked=False), 0)
    score_block(last, 0, True)

    one16, zero16 = jnp.ones((), jnp.int16), jnp.zeros((), jnp.int16)
    low16 = np.int16(-2 ** 15)

    def count16(limit, strict):
        def body(kb, acc):
            k0 = pl.multiple_of(kb * tk, tk)
            for c in range(tk // 128):
                blk = half_ref[pl.ds(k0 + 128 * c, 128), :].reshape(8, 16, tq)
                hit = jnp.where(blk > limit if strict else blk >= limit, one16, zero16)
                parts = [hit[j] for j in range(8)]
                while len(parts) > 1:
                    parts = [a + b for a, b in zip(parts[0::2], parts[1::2])]
                acc = acc + parts[0]
            return acc
        acc = lax.fori_loop(0, nkb, body, jnp.zeros((16, tq), jnp.int16))
        return jnp.broadcast_to(jnp.sum(acc.astype(jnp.int32), axis=0, keepdims=True), (16, tq))

    def search16(need):
        t = jnp.full((16, tq), -2 ** 15, jnp.int32)
        for bit in range(15, -1, -1):
            trial = t + 2 ** bit
            t = jnp.where(count16(trial.astype(jnp.int16), False) >= need, trial, t)
        return t

    t_hi = search16(n_sel)
    t_hi16 = t_hi.astype(jnp.int16)
    need_lo = n_sel - count16(t_hi16, True)

    def low_block(kb, carry):
        k0 = pl.multiple_of(kb * tk, tk)
        lo = ((key_ref[pl.ds(k0, tk), :] & 0xFFFF) - 2 ** 15).astype(jnp.int16).reshape(tk // 16, 16, tq)
        hi = half_ref[pl.ds(k0, tk), :].reshape(tk // 16, 16, tq)
        half_ref[pl.ds(k0, tk), :] = jnp.where(hi == t_hi16, lo, low16).reshape(tk, tq)
        return carry

    lax.fori_loop(0, nkb, low_block, 0)
    t_lo = search16(need_lo)
    thr = ((t_hi << 16) | (t_lo + 2 ** 15))[0:8, :]

    need = (need_lo - count16(t_lo.astype(jnp.int16), True))[0:8, :].astype(F32)
    ur = lax.broadcasted_iota(jnp.int32, (LANES, LANES), 0)
    uc = lax.broadcasted_iota(jnp.int32, (LANES, LANES), 1)
    earlier = (uc < ur).astype(BF16)
    ones8 = jnp.ones((8, LANES), BF16)
    key_s = lax.broadcasted_iota(jnp.int32, (LANES, tq), 0)
    qry_s = q0 + lax.broadcasted_iota(jnp.int32, (LANES, tq), 1)

    def select_block(kb, seen, masked):
        k0 = pl.multiple_of(kb * tk, tk)
        for g in range(tk // LANES):
            r0 = g * LANES
            blk = key_ref[pl.ds(k0 + r0, LANES), :].reshape(LANES // 8, 8, tq)
            eq = blk == thr[None]
            eqb = jnp.where(eq, 1.0, 0.0).reshape(LANES, tq).astype(BF16)
            rank = _dot(earlier, eqb).reshape(LANES // 8, 8, tq) + seen[None]
            tie_bias = jnp.where(rank < need[None], 0.0, NEG_BIG)
            bias = jnp.where(blk > thr[None], 0.0, jnp.where(eq, tie_bias, NEG_BIG)).reshape(LANES, tq)
            if masked:
                bias = jnp.where(k0 + r0 + key_s <= qry_s, bias, NEG_BIG)
            bias_ref[pl.ds(k0 + r0, LANES), :] = bias
            seen = seen + _dot(ones8, eqb)
        return seen

    seen = lax.fori_loop(0, last, functools.partial(select_block, masked=False), jnp.zeros((8, tq), F32))
    select_block(last, seen, True)

    qdt = qdt_ref[0]
    zq = jnp.zeros((LANES - DSA_HEAD_DIM, tq), BF16)
    qz = jnp.concatenate([jnp.concatenate([qdt[hd * DSA_HEAD_DIM:(hd + 1) * DSA_HEAD_DIM, :], zq], axis=0)
                          for hd in range(DSA_HEADS)], axis=1)
    wide = DSA_HEADS * tq

    def scores(kb, dst_ref):
        k0 = pl.multiple_of(kb * tk, tk)
        dst_ref[...] = _dot(dkv_ref[pl.ds(k0, tk), :], qz)

    def absorb(src_ref, kb, carry):
        m_i, l_i, acc = carry
        k0 = pl.multiple_of(kb * tk, tk)
        bias = bias_ref[pl.ds(k0, tk), :]
        s = src_ref[...] + jnp.concatenate([bias] * DSA_HEADS, axis=1)
        m_new = jnp.maximum(m_i, jnp.max(s, axis=0, keepdims=True))
        p = jnp.exp2(s - m_new)
        alpha = jnp.exp2(m_i - m_new)
        l_new = alpha * l_i + jnp.sum(p, axis=0, keepdims=True)
        pv = _dot(dkvt_ref[0, DSA_HEAD_DIM:, pl.ds(k0, tk)], p.astype(BF16))
        return m_new, l_new, alpha * acc + pv

    def pair(j, carry):
        kb = 2 * j
        scores(kb + 1, sb_ref)
        carry = absorb(sa_ref, kb, carry)
        scores(kb + 2, sa_ref)
        return absorb(sb_ref, kb + 1, carry)

    init = (jnp.full((1, wide), NEG_BIG, F32), jnp.zeros((1, wide), F32), jnp.zeros((DSA_HEAD_DIM, wide), F32))
    last = nkb - 1
    scores(0, sa_ref)
    carry = lax.fori_loop(0, last // 2, pair, init)

    def tail_odd(carry):
        scores(last, sb_ref)
        return absorb(sb_ref, last, absorb(sa_ref, last - 1, carry))

    def tail_even(carry):
        return absorb(sa_ref, last, carry)

    _, l_f, acc = lax.cond(last % 2 == 1, tail_odd, tail_even, carry)
    o_all = acc / l_f
    o_ref[...] = jnp.concatenate([o_all[:, hd * tq:(hd + 1) * tq] for hd in range(DSA_HEADS)],
                                 axis=0).T.astype(BF16)


def _dsa(qdt, iqt, iwt, dkv, dkvt, ikw, batch, seq):
    tq, tk = DSA_TQ, DSA_TK
    nq = seq // tq
    n_sel = min(DSA_TOPK, seq // 4)
    return pl.pallas_call(
        functools.partial(_dsa_kernel, tq=tq, tk=tk, n_sel=n_sel),
        out_shape=jax.ShapeDtypeStruct((batch * seq, 256), BF16),
        grid=(batch, nq),
        in_specs=[pl.BlockSpec((1, 256, tq), lambda b, i: (b, 0, i)),
                  pl.BlockSpec((1, 128, tq), lambda b, i: (b, 0, i)),
                  pl.BlockSpec((1, 8, tq), lambda b, i: (b, 0, i)),
                  pl.BlockSpec((seq, 128), lambda b, i: (b, 0)),
                  pl.BlockSpec((1, 128, seq), lambda b, i: (b, 0, 0)),
                  pl.BlockSpec((seq, 128), lambda b, i: (b, 0))],
        out_specs=pl.BlockSpec((tq, 256), lambda b, i: (b * nq + i, 0)),
        scratch_shapes=[pltpu.VMEM((seq, tq), jnp.int32), pltpu.VMEM((seq, tq), F32),
                        pltpu.VMEM((seq, tq), jnp.int16),
                        pltpu.VMEM((tk, DSA_HEADS * tq), F32), pltpu.VMEM((tk, DSA_HEADS * tq), F32)],
        compiler_params=_params(("parallel", "parallel")),
        name="dsa",
    )(qdt, iqt, iwt, dkv, dkvt, ikw)


def _mem_kv_kernel(mem_ref, g_ref, w_ref, kt_ref, v_ref):
    mn = _rms(mem_ref[0], g_ref[...]).astype(BF16)
    kv = _dot(mn, w_ref[...])
    kt_ref[0] = kv[:, :256].T.astype(BF16)
    v_ref[0] = kv[:, 256:].astype(BF16)


def _mem_kv(mem, gain, w_kv):
    b, m, d = mem.shape
    return pl.pallas_call(
        _mem_kv_kernel,
        out_shape=(jax.ShapeDtypeStruct((b, 256, m), BF16), jax.ShapeDtypeStruct((b, m, 256), BF16)),
        grid=(b,),
        in_specs=[pl.BlockSpec((1, m, d), lambda i: (i, 0, 0)), _const_spec((1, d)), _const_spec((d, 512))],
        out_specs=(pl.BlockSpec((1, 256, m), lambda i: (i, 0, 0)), pl.BlockSpec((1, m, 256), lambda i: (i, 0, 0))),
        compiler_params=_params(("parallel",)),
        name="mem_kv",
    )(mem, gain, w_kv)


def _merge_kernel(x_ref, ya_ref, yb_ref, yc_ref, yd_ref, gmix_ref, wg_ref, wbr_ref, wout_ref,
                  gq_ref, wq_ref, mkt_ref, mv_ref, wo_ref, gffn_ref, wr_ref, br_ref,
                  x2_ref, hp_ref, rt_ref, cnt_ref, run_ref, *, tm):
    x = x_ref[...]
    d = x.shape[-1]
    h = _rms(x, gmix_ref[...]).astype(BF16)
    merged = jnp.zeros((tm, d), F32)
    for n, y_ref in enumerate((ya_ref, yb_ref, yc_ref, yd_ref)):
        wn = wg_ref[:, n * d:n * d + d + LANES][:, GATE_SHIFT:GATE_SHIFT + d]
        gate = _sigmoid(_dot(h, wn))
        merged = merged + gate * _dot(y_ref[...], wbr_ref[n])
    x1 = x + _dot(merged.astype(BF16), wout_ref[...])
    h2 = _rms(x1, gq_ref[...]).astype(BF16)
    q = (_dot(h2, wq_ref[...]) * (MEM_HEAD_DIM ** -0.5)).astype(BF16)
    lane_head = lax.broadcasted_iota(jnp.int32, (tm, 256), 1) // MEM_HEAD_DIM
    mv = mv_ref[0]
    o = jnp.zeros((tm, 256), F32)
    for hd in range(MEM_HEADS):
        s = _dot(q[:, hd * MEM_HEAD_DIM:(hd + 1) * MEM_HEAD_DIM], mkt_ref[0, hd * MEM_HEAD_DIM:(hd + 1) * MEM_HEAD_DIM, :])
        p = jnp.exp(s - jnp.max(s, axis=-1, keepdims=True))
        p = p / jnp.sum(p, axis=-1, keepdims=True)
        o = o + jnp.where(lane_head == hd, _dot(p.astype(BF16), mv), 0.0)
    x2 = x1 + _dot(o.astype(BF16), wo_ref[...])
    x2_ref[...] = x2
    h3f = _rms(x2, gffn_ref[...])
    hp_ref[...] = _pack_bf16_pairs(h3f)
    h3 = h3f.astype(BF16)
    logits = _dot(h3, wr_ref[...]) + br_ref[...]
    lane = lax.broadcasted_iota(jnp.int32, (tm, LANES), 1)
    gl = jnp.where(lane < MOE_GROUPS, logits, -jnp.inf)
    gmax = jnp.max(gl, axis=-1, keepdims=True)
    gsel = jnp.min(jnp.where(gl == gmax, lane, LANES), axis=-1, keepdims=True)
    pg_sel = 1.0 / jnp.sum(jnp.exp(gl - gmax), axis=-1, keepdims=True)
    in_group = (lane - ROUTER_EXPERT_LANE) // MOE_EXPERTS_PER_GROUP == gsel
    el = jnp.where(in_group, logits, -jnp.inf)
    m1 = jnp.max(el, axis=-1, keepdims=True)
    i1 = jnp.min(jnp.where(el == m1, lane, LANES), axis=-1, keepdims=True)
    el2 = jnp.where(lane == i1, -jnp.inf, el)
    m2 = jnp.max(el2, axis=-1, keepdims=True)
    i2 = jnp.min(jnp.where(el2 == m2, lane, LANES), axis=-1, keepdims=True)
    e21 = jnp.exp(m2 - m1)
    c1 = pg_sel / (1.0 + e21)
    @pl.when(pl.program_id(0) == 0)
    def _():
        run_ref[...] = jnp.zeros_like(run_ref)

    oh1 = jnp.where(lane == i1, 1.0, 0.0)
    oh2 = jnp.where(lane == i2, 1.0, 0.0)
    both = oh1 + oh2
    tr = lax.broadcasted_iota(jnp.int32, (LANES, LANES), 0)
    tc = lax.broadcasted_iota(jnp.int32, (LANES, LANES), 1)
    earlier = jnp.where(tc < tr, 1.0, 0.0).astype(BF16)
    seen = run_ref[0:1, :]
    r1, r2 = [], []
    for g0 in range(0, tm, LANES):
        grp = both[g0:g0 + LANES, :]
        before = _dot(earlier, grp.astype(BF16)) + seen
        r1.append(jnp.sum(oh1[g0:g0 + LANES, :] * before, axis=-1, keepdims=True))
        r2.append(jnp.sum(oh2[g0:g0 + LANES, :] * before, axis=-1, keepdims=True))
        seen = seen + jnp.sum(grp, axis=0, keepdims=True)
    r1, r2 = jnp.concatenate(r1, axis=0), jnp.concatenate(r2, axis=0)
    total = jnp.broadcast_to(seen, run_ref.shape)
    run_ref[...] = total
    cnt_ref[...] = total
    ids = (jnp.where(lane == 0, i1, i2) - ROUTER_EXPERT_LANE).astype(F32)
    rt_ref[...] = jnp.where(lane < 2, ids, jnp.where(lane == 2, c1, jnp.where(lane == 3, c1 * e21,
                            jnp.where(lane == 4, r1, jnp.where(lane == 5, r2, 0.0)))))


def _merge(x, ys, gmix, wg, wbr, wout, gq, wq, mkt, mv, wo, gffn, wr, br, batch, seq):
    n, d = x.shape
    tm = MERGE_TM
    spt = seq // tm
    m = mv.shape[1]
    tok = lambda w: pl.BlockSpec((tm, w), lambda i: (i, 0))
    return pl.pallas_call(
        functools.partial(_merge_kernel, tm=tm),
        out_shape=(jax.ShapeDtypeStruct((n, d), F32), jax.ShapeDtypeStruct((n, d // 2), F32),
                   jax.ShapeDtypeStruct((n, LANES), F32), jax.ShapeDtypeStruct((8, LANES), F32)),
        grid=(n // tm,),
        in_specs=[tok(d), tok(256), tok(256), tok(256), tok(256),
                  _const_spec((1, d)), _const_spec((d, N_BRANCH * d + LANES)), _const_spec((N_BRANCH, 256, d)),
                  _const_spec((d, d)), _const_spec((1, d)), _const_spec((d, 256)),
                  pl.BlockSpec((1, 256, m), lambda i: (i // spt, 0, 0)),
                  pl.BlockSpec((1, m, 256), lambda i: (i // spt, 0, 0)),
                  _const_spec((256, d)), _const_spec((1, d)), _const_spec((d, LANES)), _const_spec((1, LANES))],
        out_specs=(tok(d), tok(d // 2), tok(LANES), pl.BlockSpec((8, LANES), lambda i: (0, 0))),
        scratch_shapes=[pltpu.VMEM((8, LANES), F32)],
        compiler_params=_params(("arbitrary",)),
        name="merge_mem_router",
    )(x, *ys, gmix, wg, wbr, wout, gq, wq, mkt, mv, wo, gffn, wr, br)


def _sc_gather_rows(table, idx):
    _, width = table.shape
    total = idx.shape[0]
    chunk, nbuf = SC_GATHER_CHUNK, SC_GATHER_BUFS
    workers = SC_CORES * SC_SUBCORES
    per_w = total // workers
    nch = per_w // chunk
    assert total % (workers * chunk * nbuf) == 0
    mesh = plsc.VectorSubcoreMesh(core_axis_name="c", subcore_axis_name="s")

    @functools.partial(
        pl.kernel, mesh=mesh, out_type=jax.ShapeDtypeStruct((total, width), table.dtype),
        scratch_types=[pltpu.VMEM((nch, chunk), jnp.int32), pltpu.VMEM((nbuf, chunk, width), table.dtype),
                       pltpu.SemaphoreType.DMA((nbuf,)), pltpu.SemaphoreType.DMA((nbuf,))])
    def gather_kernel(table_hbm, idx_hbm, out_hbm, idx_v, rows_v, gsem, wsem):
        wid = lax.axis_index("s") * SC_CORES + lax.axis_index("c")
        pltpu.sync_copy(idx_hbm.at[wid], idx_v)

        def gather(j, slot):
            return pltpu.make_async_copy(table_hbm.at[idx_v.at[j]], rows_v.at[slot], gsem.at[slot])

        def write(j, slot):
            off = pl.multiple_of(wid * per_w + j * chunk, chunk)
            return pltpu.make_async_copy(rows_v.at[slot], out_hbm.at[pl.ds(off, chunk)], wsem.at[slot])

        for slot in range(nbuf):
            gather(slot, slot).start()

        @pl.loop(0, nch // nbuf)
        def _(g):
            for slot in range(nbuf):
                j = g * nbuf + slot
                gather(j, slot).wait()
                write(j, slot).start()
                write(j, slot).wait()

                @pl.when(j + nbuf < nch)
                def _():
                    gather(j + nbuf, slot).start()

    return gather_kernel(table, idx.reshape(workers, nch, chunk))


def _sc_scatter_rows(table, dest2, total):
    n, width = table.shape
    chunk, nbuf = SC_GATHER_CHUNK, SC_GATHER_BUFS
    workers = SC_CORES * SC_SUBCORES
    per_w = n // workers
    nch = per_w // chunk
    assert n % (workers * chunk * nbuf) == 0
    mesh = plsc.VectorSubcoreMesh(core_axis_name="c", subcore_axis_name="s")

    @functools.partial(
        pl.kernel, mesh=mesh, out_type=jax.ShapeDtypeStruct((total, width), table.dtype),
        scratch_types=[pltpu.VMEM((2, nch, chunk), jnp.int32), pltpu.VMEM((nbuf, chunk, width), table.dtype),
                       pltpu.SemaphoreType.DMA((nbuf,)), pltpu.SemaphoreType.DMA((nbuf,))])
    def scatter_kernel(table_hbm, idx_hbm, out_hbm, idx_v, rows_v, rsem, wsem):
        wid = lax.axis_index("s") * SC_CORES + lax.axis_index("c")
        pltpu.sync_copy(idx_hbm.at[wid], idx_v)

        def read(j, slot):
            off = pl.multiple_of(wid * per_w + j * chunk, chunk)
            return pltpu.make_async_copy(table_hbm.at[pl.ds(off, chunk)], rows_v.at[slot], rsem.at[slot])

        def write(j, slot, k):
            return pltpu.make_async_copy(rows_v.at[slot], out_hbm.at[idx_v.at[k, j]], wsem.at[slot])

        for slot in range(nbuf):
            read(slot, slot).start()

        @pl.loop(0, nch // nbuf)
        def _(g):
            for slot in range(nbuf):
                j = g * nbuf + slot
                read(j, slot).wait()
                write(j, slot, 0).start()
                write(j, slot, 1).start()
                write(j, slot, 0).wait()
                write(j, slot, 1).wait()

                @pl.when(j + nbuf < nch)
                def _():
                    read(j + nbuf, slot).start()

    idx = dest2.reshape(2, workers, nch, chunk).transpose(1, 0, 2, 3)
    return scatter_kernel(table, idx)


def _dispatch_plan(rt, cnt, n):
    ne, blk = MOE_N_EXPERTS, MOE_BLOCK
    n_blocks = (2 * n) // blk + ne
    experts = jnp.arange(ne, dtype=jnp.int32)
    counts = cnt[0, ROUTER_EXPERT_LANE:ROUTER_EXPERT_LANE + ne].astype(jnp.int32)
    padded = (counts + blk - 1) // blk * blk
    pend = jnp.cumsum(padded)
    pstart = pend - padded
    ids = rt[:, 0:2].astype(jnp.int32)
    pos = rt[:, 4:6].astype(jnp.int32)
    first_row = jnp.sum(jnp.where(ids[:, :, None] == experts[None, None, :], pstart[None, None, :], 0), axis=-1)
    dest2 = (first_row + pos).T
    b0 = jnp.arange(n_blocks, dtype=jnp.int32) * blk
    block_e = jnp.minimum(jnp.sum((pend[None, :] <= b0[:, None]).astype(jnp.int32), axis=1), ne - 1)
    n_valid = jnp.clip(counts[block_e] - (b0 - pstart[block_e]), 0, blk).astype(jnp.int32)
    return dest2, block_e, n_valid


def _expert_block_kernel(be_ref, nv_ref, xs_ref, wg_ref, wu_ref, wd_ref, o_ref, wgb_ref, wub_ref, wdb_ref):
    b = pl.program_id(0)
    valid = nv_ref[b]

    @pl.when((b == 0) | (be_ref[b] != be_ref[jnp.maximum(b - 1, 0)]))
    def _():
        wgb_ref[...] = wg_ref[0].astype(BF16)
        wub_ref[...] = wu_ref[0].astype(BF16)
        wdb_ref[...] = wd_ref[0].astype(BF16)

    @pl.when(valid > 0)
    def _():
        row = lax.broadcasted_iota(jnp.int32, xs_ref.shape, 0)
        words = jnp.where(row < valid, xs_ref[...], 0.0)
        h = _unpack_bf16_pairs(words).astype(BF16)
        gt = _dot(h, wgb_ref[...])
        hid = gt * _sigmoid(gt) * _dot(h, wub_ref[...])
        o_ref[...] = _pack_bf16_pairs(_dot(hid.astype(BF16), wdb_ref[...]))

    @pl.when(valid == 0)
    def _():
        o_ref[...] = jnp.zeros_like(o_ref)


def _expert_blocks(xs, wg, wu, wd, layer, block_e, n_used):
    p_rows, half = xs.shape
    d, hid = wg.shape[-2:]
    blk = MOE_BLOCK
    grid_spec = pltpu.PrefetchScalarGridSpec(
        num_scalar_prefetch=2, grid=(p_rows // blk,),
        in_specs=[pl.BlockSpec((blk, half), lambda b, be, nu: (b, 0)),
                  pl.BlockSpec((None, 1, d, hid), lambda b, be, nu: (layer, be[b], 0, 0)),
                  pl.BlockSpec((None, 1, d, hid), lambda b, be, nu: (layer, be[b], 0, 0)),
                  pl.BlockSpec((None, 1, hid, d), lambda b, be, nu: (layer, be[b], 0, 0))],
        out_specs=pl.BlockSpec((blk, half), lambda b, be, nu: (b, 0)),
        scratch_shapes=[pltpu.VMEM((d, hid), BF16), pltpu.VMEM((d, hid), BF16), pltpu.VMEM((hid, d), BF16)])
    return pl.pallas_call(
        _expert_block_kernel, out_shape=jax.ShapeDtypeStruct((p_rows, half), F32), grid_spec=grid_spec,
        compiler_params=_params(("arbitrary",)),
        name="moe_expert_blocks",
    )(block_e, n_used, xs, wg, wu, wd)


def _combine_kernel(x_ref, y1_ref, y2_ref, rt_ref, gfin_ref, o_ref, *, final_norm):
    out = (x_ref[...] + rt_ref[:, 2:3] * _unpack_bf16_pairs(y1_ref[...])
           + rt_ref[:, 3:4] * _unpack_bf16_pairs(y2_ref[...]))
    o_ref[...] = _rms(out, gfin_ref[...]) if final_norm else out


def _combine(x2, y_halves, rt, gfin, final_norm):
    n, d = x2.shape
    tm = COMBINE_TM
    nt = n // tm
    return pl.pallas_call(
        functools.partial(_combine_kernel, final_norm=final_norm),
        out_shape=jax.ShapeDtypeStruct((n, d), F32),
        grid=(nt,),
        in_specs=[pl.BlockSpec((tm, d), lambda i: (i, 0)),
                  pl.BlockSpec((tm, d // 2), lambda i: (i, 0)),
                  pl.BlockSpec((tm, d // 2), lambda i: (i + nt, 0)),
                  pl.BlockSpec((tm, LANES), lambda i: (i, 0)), _const_spec((1, d))],
        out_specs=pl.BlockSpec((tm, d), lambda i: (i, 0)),
        compiler_params=_params(("parallel",)), name="moe_combine",
    )(x2, y_halves, y_halves, rt, gfin)


def _moe(x2, hp, rt, cnt, wg, wu, wd, layer, gfin, final_norm):
    n = x2.shape[0]
    dest2, block_e, n_valid = _dispatch_plan(rt, cnt, n)
    xs = _sc_scatter_rows(hp, dest2, block_e.shape[0] * MOE_BLOCK)
    yb = _expert_blocks(xs, wg, wu, wd, layer, block_e, n_valid)
    y_halves = _sc_gather_rows(yb, dest2.reshape(2 * n))
    return _combine(x2, y_halves, rt, gfin, final_norm)


def kernel(x, mem, positions, norm_mix, w_in, diff_lambda, hgrn_lb_logits, spatial_w, spatial_b, w_branch, w_out,
           norm_mem_q, norm_mem_kv, w_mem_q, w_mem_kv, w_mem_o, norm_ffn, w_router_group, b_router_group,
           w_router_expert, b_router_expert, w_exp_gate, w_exp_up, w_exp_down, norm_final):
    batch, seq, d = x.shape
    depth = w_in.shape[0]
    n = batch * seq
    xf = x.reshape(n, d)
    tabs = _rope_tables(positions)
    w_all = _w_in_bf16(w_in)
    row = lambda v: v.reshape(1, -1).astype(F32)
    for l in range(depth):
        lam_init = 0.8 - 0.6 * math.exp(-0.3 * l)
        g0 = C_GATE - GATE_SHIFT
        w_gate = lax.slice(w_all, (l, 0, g0), (l + 1, d, g0 + N_BRANCH * d + LANES)).reshape(d, -1)
        sw = spatial_w[l].reshape(SGU_GROUPS * SGU_CHUNK, SGU_CHUNK)
        sb = jnp.repeat(spatial_b[l].T, SGU_GROUP_DIM, axis=1)
        qat, ka, vat, hb, y_c, qdt, iqt, dkv, dkvt, ikw, iwt = _projection(
            xf, row(norm_mix[l]), w_all, l, tabs, sw, sb, batch, seq)
        y_a = _diff_attention(diff_lambda[l], qat, ka, vat, lam_init, batch, seq)
        y_b = _hgrn(hgrn_lb_logits, hb, l, batch, seq)
        y_d = _dsa(qdt, iqt, iwt, dkv, dkvt, ikw, batch, seq)
        mkt, mv = _mem_kv(mem, row(norm_mem_kv[l]), w_mem_kv[l].astype(BF16))
        e0, e1 = ROUTER_EXPERT_LANE, ROUTER_EXPERT_LANE + MOE_N_EXPERTS
        wr = jnp.zeros((d, LANES), F32)
        wr = wr.at[:, :MOE_GROUPS].set(w_router_group[l]).at[:, e0:e1].set(w_router_expert[l]).astype(BF16)
        br = jnp.zeros((1, LANES), F32)
        br = br.at[0, :MOE_GROUPS].set(b_router_group[l]).at[0, e0:e1].set(b_router_expert[l])
        x2, hp, rt, cnt = _merge(xf, (y_a, y_b, y_c, y_d), row(norm_mix[l]), w_gate, w_branch[l].astype(BF16),
                                 w_out[l].astype(BF16), row(norm_mem_q[l]), w_mem_q[l].astype(BF16), mkt, mv,
                                 w_mem_o[l].astype(BF16), row(norm_ffn[l]), wr, br, batch, seq)
        xf = _moe(x2, hp, rt, cnt, w_exp_gate, w_exp_up, w_exp_down, l,
                  row(norm_final), final_norm=(l == depth - 1))
    return xf.reshape(batch, seq, d)
```

```python
import functools
import math

import numpy as np
import jax
import jax.numpy as jnp
from jax import lax
from jax.experimental import pallas as pl
from jax.experimental.pallas import tpu as pltpu
from jax.experimental.pallas import tpu_sc as plsc

F32 = jnp.float32
BF16 = jnp.bfloat16

NORM_EPS = 1e-6
ROPE_THETA = 10000.0
NEG_BIG = -1e30

N_BRANCH = 4
DIFF_HEADS = 4
DIFF_HEAD_DIM = 32
HGRN_DIM = 64
HGRN_CHUNK = 32
HGRN_UNROLL = 8
HGRN_MIN_FORGET = 1e-30
SGU_GROUPS = 4
SGU_GROUP_DIM = 64
SGU_CHUNK = 128
DSA_HEADS = 4
DSA_HEAD_DIM = 64
DSA_IDX_HEADS = 4
DSA_IDX_DIM = 32
DSA_TOPK = 256
MEM_HEADS = 4
MEM_HEAD_DIM = 64
MOE_GROUPS = 4
MOE_EXPERTS_PER_GROUP = 8
MOE_N_EXPERTS = 32
MOE_BLOCK = 512
ROUTER_EXPERT_LANE = 32
SC_CORES = 2
SC_SUBCORES = 16
SC_GATHER_CHUNK = 32
SC_GATHER_BUFS = 4

LANES = 128
VMEM_LIMIT = 56 * 1024 * 1024

PROJ_TM = 1024
DIFF_TQ = 512
DIFF_TK = 512
HGRN_TC = 1024
DSA_TQ = 512
DSA_TK = 512
MERGE_TM = 1024
COMBINE_TM = 1024
ROPE_TM = 1024
W_IN_COLS = 256

C_AQ, C_AK, C_AV = 0, 256, 512
C_HB = 768
C_UV = 1792
C_DQ = 2304
C_DKV = 2560
C_IQ = 2688
C_IKW = 2816
IW_LANE = 32
C_GATE = 2852
GATE_SHIFT = C_GATE % 128
C_TOTAL = 2944
LOG2E = math.log2(math.e)


def _params(sem):
    return pltpu.CompilerParams(dimension_semantics=sem, vmem_limit_bytes=VMEM_LIMIT)


def _const_spec(shape):
    nd = len(shape)
    return pl.BlockSpec(shape, lambda *_: (0,) * nd, pipeline_mode=pl.Buffered(1))


def _rms(xf, gain=None):
    y = xf * lax.rsqrt(jnp.mean(xf * xf, axis=-1, keepdims=True) + NORM_EPS)
    return y if gain is None else y * gain


def _sigmoid(x):
    return 0.5 * jnp.tanh(0.5 * x) + 0.5


def _pack_bf16_pairs(x):
    w = x.shape[-1] // 2
    xb = x.astype(BF16).astype(F32)
    lo = lax.shift_right_logical(pltpu.bitcast(xb[:, :w], jnp.int32), 16)
    hi = pltpu.bitcast(xb[:, w:], jnp.int32) & jnp.int32(-65536)
    return pltpu.bitcast(hi | lo, F32)


def _unpack_bf16_pairs(words):
    bits = pltpu.bitcast(words, jnp.int32)
    lo = pltpu.bitcast(bits << 16, F32)
    hi = pltpu.bitcast(bits & jnp.int32(-65536), F32)
    return jnp.concatenate([lo, hi], axis=1)


def _dot(a, b):
    return jnp.dot(a, b, preferred_element_type=F32)


def _dot_nt(a, b):
    return lax.dot_general(a, b, (((1,), (1,)), ((), ())), preferred_element_type=F32)


def _rope_table_kernel(pos_ref, frq_ref, sgn_ref, c32_ref, s32_ref, c64_ref, s64_ref):
    ang = pos_ref[...].astype(F32) * frq_ref[...]
    cos = jnp.cos(ang)
    sin = jnp.sin(ang) * sgn_ref[...]
    four = lambda t: jnp.concatenate([t, t, t, t], axis=1)
    c64_ref[...] = four(cos[:, :64])
    s64_ref[...] = four(sin[:, :64])
    c32_ref[...] = four(cos[:, 64:])
    s32_ref[...] = four(sin[:, 64:])


def _rope_tables(positions):
    n = positions.size
    pos = positions.reshape(n, 1).astype(jnp.int32)
    lane = np.arange(64)
    inv32 = ROPE_THETA ** (-jnp.arange(16, dtype=F32) * (2.0 / 32))
    inv64 = ROPE_THETA ** (-jnp.arange(32, dtype=F32) * (2.0 / 64))
    frq = jnp.concatenate([inv64[lane % 32], inv32[lane % 16]]).reshape(1, LANES)
    sgn = jnp.asarray(np.concatenate([np.where(lane % 64 < 32, -1.0, 1.0),
                                      np.where(lane % 32 < 16, -1.0, 1.0)]), F32).reshape(1, LANES)
    tm = ROPE_TM
    tab = jax.ShapeDtypeStruct((n, 256), F32)
    return pl.pallas_call(
        _rope_table_kernel,
        out_shape=(tab, tab, tab, tab),
        grid=(n // tm,),
        in_specs=[pl.BlockSpec((tm, 1), lambda i: (i, 0)), _const_spec((1, LANES)), _const_spec((1, LANES))],
        out_specs=tuple(pl.BlockSpec((tm, 256), lambda i: (i, 0)) for _ in range(4)),
        compiler_params=_params(("parallel",)),
        name="rope_tables",
    )(pos, frq, sgn)


def _w_in_kernel(wt_ref, o_ref, *, width):
    rows = wt_ref.shape[0]
    row = lax.broadcasted_iota(jnp.int32, (rows, wt_ref.shape[-1]), 0)
    valid = row < width - pl.program_id(0) * rows
    for layer in range(o_ref.shape[0]):
        o_ref[layer] = jnp.where(valid, wt_ref[:, layer, :], 0.0).T.astype(BF16)


def _w_in_bf16(w_in):
    depth, d, width = w_in.shape
    cols = W_IN_COLS
    nblk = pl.cdiv(width, cols)
    return pl.pallas_call(
        functools.partial(_w_in_kernel, width=width), out_shape=jax.ShapeDtypeStruct((depth, d, nblk * cols), BF16),
        grid=(nblk,), in_specs=[pl.BlockSpec((cols, depth, d), lambda i: (i, 0, 0))],
        out_specs=pl.BlockSpec((depth, d, cols), lambda i: (0, 0, i)),
        compiler_params=_params(("parallel",)), name="w_in_bf16",
    )(jnp.transpose(w_in, (2, 0, 1)))


def _gelu_tanh(x):
    return 0.5 * x * (1.0 + jnp.tanh(math.sqrt(2.0 / math.pi) * (x + 0.044715 * (x * x * x))))


def _rope(x, cos, sin_signed, half):
    w = x.shape[-1]
    lane = lax.broadcasted_iota(jnp.int32, x.shape, 1)
    partner = jnp.where(lane % (2 * half) < half, pltpu.roll(x, w - half, 1), pltpu.roll(x, half, 1))
    return x * cos + partner * sin_signed


def _proj_kernel(x_ref, g_ref, w_ref, c32_ref, s32_ref, c64_ref, s64_ref, sw_ref, sb_ref,
                 qat_ref, ka_ref, vat_ref, hb_ref, yc_ref, qdt_ref, iqt_ref, dkv_ref, dkvt_ref, ikw_ref, iwt_ref,
                 *, tm):
    h = _rms(x_ref[...], g_ref[...]).astype(BF16)

    def proj(c0, width):
        return _dot(h, w_ref[:, c0:c0 + width])

    c32, s32, c64, s64 = c32_ref[...], s32_ref[...], c64_ref[...], s64_ref[...]
    qat_ref[0] = (_rope(proj(C_AQ, 256), c32, s32, 16) * (DIFF_HEAD_DIM ** -0.5 * LOG2E)).T.astype(BF16)
    ka_ref[...] = _rope(proj(C_AK, 256), c32, s32, 16).astype(BF16)
    vat_ref[0] = proj(C_AV, 256).astype(BF16).T
    hb_ref[...] = proj(C_HB, 1024)
    qdt_ref[0] = (_rope(proj(C_DQ, 256), c64, s64, 32) * (DSA_HEAD_DIM ** -0.5 * LOG2E)).T.astype(BF16)
    iqt_ref[0] = _rope(proj(C_IQ, 128), c32[:, :128], s32[:, :128], 16).T.astype(BF16)
    lane = lax.broadcasted_iota(jnp.int32, (tm, 128), 1)
    is_k = lane < DSA_HEAD_DIM
    dkv = _rope(proj(C_DKV, 128), jnp.where(is_k, c64[:, :128], 1.0), jnp.where(is_k, s64[:, :128], 0.0), 32)
    is_ik = lane < DSA_IDX_DIM
    ikw = _rope(proj(C_IKW, 128), jnp.where(is_ik, c32[:, :128], 1.0), jnp.where(is_ik, s32[:, :128], 0.0), 16)
    dkv_ref[...] = dkv.astype(BF16)
    dkvt_ref[0] = dkv.T.astype(BF16)
    ikw_ref[...] = ikw.astype(BF16)
    iw_scale = DSA_IDX_HEADS ** -0.5 * DSA_IDX_DIM ** -0.5
    iwt_ref[0] = (ikw * iw_scale).T[IW_LANE:IW_LANE + 8, :]
    uv = _gelu_tanh(proj(C_UV, 512))
    u, v = uv[:, :256], uv[:, 256:]
    mu = jnp.mean(v, axis=-1, keepdims=True)
    vc = v - mu
    vn = (vc * lax.rsqrt(jnp.mean(vc * vc, axis=-1, keepdims=True) + NORM_EPS)).astype(BF16)
    r = lax.broadcasted_iota(jnp.int32, (SGU_GROUPS * SGU_CHUNK, SGU_CHUNK), 0)
    c = lax.broadcasted_iota(jnp.int32, (SGU_GROUPS * SGU_CHUNK, SGU_CHUNK), 1)
    wt = jnp.where((r % SGU_CHUNK) >= c, sw_ref[...], 0.0).astype(BF16)
    lane_grp = lax.broadcasted_iota(jnp.int32, (SGU_CHUNK, 256), 1) // SGU_GROUP_DIM
    for ch in range(tm // SGU_CHUNK):
        r0 = ch * SGU_CHUNK
        full = _dot(wt, vn[r0:r0 + SGU_CHUNK, :])
        mixed = sb_ref[...]
        for g in range(SGU_GROUPS):
            mixed = mixed + jnp.where(lane_grp == g, full[g * SGU_CHUNK:(g + 1) * SGU_CHUNK, :], 0.0)
        yc_ref[r0:r0 + SGU_CHUNK, :] = (u[r0:r0 + SGU_CHUNK, :] * mixed).astype(BF16)


def _projection(x, gain, w_all, layer, tabs, sw, sb, batch, seq):
    n, d = x.shape
    tm = PROJ_TM
    spt = seq // tm
    tok = lambda w: pl.BlockSpec((tm, w), lambda i: (i, 0))
    tr = lambda rows: pl.BlockSpec((1, rows, tm), lambda i: (i // spt, 0, i % spt))
    out_shape = (
        jax.ShapeDtypeStruct((batch, 256, seq), BF16),
        jax.ShapeDtypeStruct((n, 256), BF16),
        jax.ShapeDtypeStruct((batch, 256, seq), BF16),
        jax.ShapeDtypeStruct((n, 1024), F32),
        jax.ShapeDtypeStruct((n, 256), BF16),
        jax.ShapeDtypeStruct((batch, 256, seq), BF16),
        jax.ShapeDtypeStruct((batch, 128, seq), BF16),
        jax.ShapeDtypeStruct((n, 128), BF16),
        jax.ShapeDtypeStruct((batch, 128, seq), BF16),
        jax.ShapeDtypeStruct((n, 128), BF16),
        jax.ShapeDtypeStruct((batch, 8, seq), F32),
    )
    return pl.pallas_call(
        functools.partial(_proj_kernel, tm=tm),
        out_shape=out_shape,
        grid=(n // tm,),
        in_specs=[tok(d), _const_spec((1, d)),
                  pl.BlockSpec((None, d, C_TOTAL), lambda i: (layer, 0, 0), pipeline_mode=pl.Buffered(1)),
                  tok(256), tok(256), tok(256), tok(256),
                  _const_spec((SGU_GROUPS * SGU_CHUNK, SGU_CHUNK)), _const_spec((SGU_CHUNK, 256))],
        out_specs=(tr(256), tok(256), tr(256), tok(1024), tok(256), tr(256), tr(128), tok(128), tr(128), tok(128), tr(8)),
        compiler_params=_params(("parallel",)),
        name="projection",
    )(x, gain, w_all, *tabs, sw, sb)


def _diff_attn_kernel(lam_ref, qt_ref, k_ref, vt_ref, o_ref, sa_ref, sb_ref, *, lam_init, tq, tk):
    q0 = pl.program_id(1) * tq
    kb_diag = q0 // tk
    lv = lam_ref[...]
    lam = (jnp.exp(jnp.sum(lv[0:1] * lv[1:2], axis=-1, keepdims=True))
           - jnp.exp(jnp.sum(lv[2:3] * lv[3:4], axis=-1, keepdims=True)) + lam_init)
    qt = qt_ref[0]
    feat = lax.broadcasted_iota(jnp.int32, (256, tq), 0) // DIFF_HEAD_DIM
    n_maps = 2 * DIFF_HEADS
    qz = jnp.concatenate([jnp.where(feat == i, qt, jnp.zeros_like(qt)) for i in range(n_maps)], axis=1)
    wide = n_maps * tq
    key_i = lax.broadcasted_iota(jnp.int32, (tk, wide), 0)
    qry_i = q0 + lax.broadcasted_iota(jnp.int32, (tk, wide), 1) % tq

    def scores(kb, dst_ref):
        k0 = pl.multiple_of(kb * tk, tk)
        dst_ref[...] = _dot(k_ref[pl.ds(k0, tk), :], qz)

    def absorb(src_ref, kb, carry, masked):
        m_i, l_i, acc = carry
        k0 = pl.multiple_of(kb * tk, tk)
        s = src_ref[...]
        if masked:
            s = jnp.where(k0 + key_i <= qry_i, s, NEG_BIG)
        m_new = jnp.maximum(m_i, jnp.max(s, axis=0, keepdims=True))
        p = jnp.exp2(s - m_new)
        alpha = jnp.exp2(m_i - m_new)
        l_new = alpha * l_i + jnp.sum(p, axis=0, keepdims=True)
        pb = p.astype(BF16)
        pv = jnp.concatenate(
            [_dot(vt_ref[0, hd * 64:(hd + 1) * 64, pl.ds(k0, tk)], pb[:, 2 * hd * tq:(2 * hd + 2) * tq])
             for hd in range(DIFF_HEADS)], axis=1)
        return m_new, l_new, alpha * acc + pv

    def pair(j, carry):
        kb = 2 * j
        scores(kb + 1, sb_ref)
        carry = absorb(sa_ref, kb, carry, False)
        scores(kb + 2, sa_ref)
        return absorb(sb_ref, kb + 1, carry, False)

    init = (jnp.full((1, wide), NEG_BIG, F32), jnp.zeros((1, wide), F32), jnp.zeros((64, wide), F32))
    scores(0, sa_ref)
    carry = lax.fori_loop(0, kb_diag // 2, pair, init)

    def tail_odd(carry):
        scores(kb_diag, sb_ref)
        carry = absorb(sa_ref, kb_diag - 1, carry, False)
        return absorb(sb_ref, kb_diag, carry, True)

    def tail_even(carry):
        return absorb(sa_ref, kb_diag, carry, True)

    _, l_f, acc = lax.cond(kb_diag % 2 == 1, tail_odd, tail_even, carry)
    o_all = acc / l_f
    heads = []
    for hd in range(DIFF_HEADS):
        o0 = o_all[:, 2 * hd * tq:(2 * hd + 1) * tq]
        o1 = o_all[:, (2 * hd + 1) * tq:(2 * hd + 2) * tq]
        o_h = o0 - lam * o1
        ms = jnp.mean(o_h * o_h, axis=0, keepdims=True)
        heads.append(o_h * lax.rsqrt(ms + NORM_EPS) * (1.0 - lam_init))
    o_ref[...] = jnp.concatenate(heads, axis=0).T.astype(BF16)


def _diff_attention(lam_vec, qat, ka, vat, lam_init, batch, seq):
    tq, tk = DIFF_TQ, DIFF_TK
    nq = seq // tq
    return pl.pallas_call(
        functools.partial(_diff_attn_kernel, lam_init=lam_init, tq=tq, tk=tk),
        out_shape=jax.ShapeDtypeStruct((batch * seq, 256), BF16),
        grid=(batch, nq),
        in_specs=[_const_spec((4, DIFF_HEAD_DIM)),
                  pl.BlockSpec((1, 256, tq), lambda b, i: (b, 0, i)),
                  pl.BlockSpec((seq, 256), lambda b, i: (b, 0)),
                  pl.BlockSpec((1, 256, seq), lambda b, i: (b, 0, 0))],
        out_specs=pl.BlockSpec((tq, 256), lambda b, i: (b * nq + i, 0)),
        scratch_shapes=[pltpu.VMEM((tk, 2 * DIFF_HEADS * tq), F32), pltpu.VMEM((tk, 2 * DIFF_HEADS * tq), F32)],
        compiler_params=_params(("parallel", "parallel")),
        name="diff_attention",
    )(lam_vec, qat, ka, vat)


def _hgrn_kernel(lbl_ref, hb_ref, o_ref, st_ref, pstk_ref, *, layer, tc):
    cz = HGRN_CHUNK
    w = 256

    @pl.when(pl.program_id(1) == 0)
    def _():
        st_ref[...] = jnp.zeros_like(st_ref)

    lg = lbl_ref[...]
    e = jnp.exp(lg - jnp.max(lg, axis=0, keepdims=True))
    lw = e / jnp.sum(e, axis=0, keepdims=True)
    lb = jnp.sum(lw[0:layer + 1], axis=0, keepdims=True) - lw[0:1]

    scan_row = lax.broadcasted_iota(jnp.int32, (cz, w), 0)
    rb = lax.broadcasted_iota(jnp.int32, (w, w), 0) // HGRN_DIM
    cb = lax.broadcasted_iota(jnp.int32, (w, w), 1) // HGRN_DIM
    same_head = rb == cb
    head_ones = same_head.astype(BF16)
    trows = {r: r + lax.broadcasted_iota(jnp.int32, (16, w), 0) for r in range(0, cz, 16)}

    def chunk(c, carry):
        r0 = pl.multiple_of(c * cz, cz)
        q = hb_ref[pl.ds(r0, cz), 0:256]
        fp = hb_ref[pl.ds(r0, cz), 256:512]
        v = hb_ref[pl.ds(r0, cz), 512:768]
        g = hb_ref[pl.ds(r0, cz), 768:1024]
        qf = q * _sigmoid(q)
        f = lb + (1.0 - lb) * jax.nn.sigmoid(fp)
        log_f = jnp.log(jnp.maximum(f, HGRN_MIN_FORGET))
        kf = (1.0 - lb) * jax.nn.sigmoid(-fp)
        bc = log_f
        step = 1
        while step < cz:
            bc = bc + jnp.where(scan_row >= step, pltpu.roll(bc, step, 0), 0.0)
            step *= 2
        st = st_ref[...]
        o = _dot_nt((qf * jnp.exp(bc)).astype(BF16), st.astype(BF16))
        later = {}
        for blk in range(1, cz // 16):
            r = bc[16 * blk - 1:16 * blk, :]
            later[blk] = (qf[16 * blk:16 * blk + 16, :] * jnp.exp(bc[16 * blk:16 * blk + 16, :] - r),
                          kf[:16 * blk, :] * jnp.exp(r - bc[:16 * blk, :]))
        for s in range(cz):
            r_lo = (s // 16) * 16
            arg = bc[r_lo:r_lo + 16, :] - bc[s:s + 1, :]
            if s > r_lo:
                arg = jnp.where(trows[r_lo] >= s, arg, NEG_BIG)
            p = qf[r_lo:r_lo + 16, :] * kf[s:s + 1, :] * jnp.exp(arg)
            if r_lo:
                pstk_ref[s * cz:s * cz + r_lo, :] = jnp.zeros((r_lo, w), BF16)
            pstk_ref[s * cz + r_lo:s * cz + r_lo + 16, :] = p.astype(BF16)
            for blk in range(s // 16 + 1, cz // 16):
                q_dec, k_dec = later[blk]
                pstk_ref[s * cz + 16 * blk:s * cz + 16 * blk + 16, :] = (q_dec * k_dec[s:s + 1, :]).astype(BF16)
        accs = [jnp.zeros((16, w), F32) for _ in range(cz // 16)]
        for sg in range(cz // 16):
            att = _dot(pstk_ref[sg * 16 * cz:(sg + 1) * 16 * cz, :], head_ones)
            for sl in range(16):
                s = sg * 16 + sl
                for j in range(sg, cz // 16):
                    accs[j] = accs[j] + att[sl * cz + 16 * j:sl * cz + 16 * j + 16, :] * v[s:s + 1, :]
        o = o + jnp.concatenate(accs, axis=0)
        b_end = bc[cz - 1:cz, :]
        kd = kf * jnp.exp(b_end - bc)
        upd = _dot(v.T.astype(BF16), kd.astype(BF16))
        st_ref[...] = st * jnp.exp(b_end) + jnp.where(same_head, upd, 0.0)
        ms = _dot(o * o, head_ones.astype(F32)) * (1.0 / HGRN_DIM)
        y = o * lax.rsqrt(ms + NORM_EPS)
        o_ref[pl.ds(r0, cz), :] = (y * (g * _sigmoid(g))).astype(BF16)
        return carry

    def group(gi, carry):
        for u in range(HGRN_UNROLL):
            chunk(gi * HGRN_UNROLL + u, carry)
        return carry

    lax.fori_loop(0, tc // cz // HGRN_UNROLL, group, 0)


def _hgrn(lb_logits, hb, layer, batch, seq):
    tc = HGRN_TC
    nt = seq // tc
    cz = HGRN_CHUNK
    return pl.pallas_call(
        functools.partial(_hgrn_kernel, layer=layer, tc=tc),
        out_shape=jax.ShapeDtypeStruct((batch * seq, 256), BF16),
        grid=(batch, nt),
        in_specs=[_const_spec(lb_logits.shape),
                  pl.BlockSpec((tc, 1024), lambda b, i: (b * nt + i, 0))],
        out_specs=pl.BlockSpec((tc, 256), lambda b, i: (b * nt + i, 0)),
        scratch_shapes=[pltpu.VMEM((256, 256), F32), pltpu.VMEM((cz * cz, 256), BF16)],
        compiler_params=_params(("parallel", "arbitrary")),
        name="hgrn2",
    )(lb_logits, hb)


def _dsa_kernel(qdt_ref, iqt_ref, iwt_ref, dkv_ref, dkvt_ref, ikw_ref, o_ref, key_ref, bias_ref, half_ref,
                sa_ref, sb_ref, *, tq, tk, n_sel):
    q0 = pl.program_id(1) * tq
    nkb = q0 // tk + 1
    key_i = lax.broadcasted_iota(jnp.int32, (tk, tq), 0)
    qry_i = q0 + lax.broadcasted_iota(jnp.int32, (tk, tq), 1)
    iqt = iqt_ref[0]
    zpad = jnp.zeros((LANES - DSA_IDX_DIM, tq), BF16)
    iqz = jnp.concatenate([jnp.concatenate([iqt[hd * DSA_IDX_DIM:(hd + 1) * DSA_IDX_DIM, :], zpad], axis=0)
                           for hd in range(DSA_IDX_HEADS)], axis=1)
    iw = iwt_ref[0]

    last = nkb - 1

    def score_block(kb, carry, masked):
        k0 = pl.multiple_of(kb * tk, tk)
        sh = jnp.maximum(_dot(ikw_ref[pl.ds(k0, tk), :], iqz), 0.0)
        sc = jnp.zeros((tk, tq), F32)
        for hd in range(DSA_IDX_HEADS):
            sc = sc + sh[:, hd * tq:(hd + 1) * tq] * iw[hd:hd + 1, :]
        sc = sc + 0.0
        if masked:
            sc = jnp.where(k0 + key_i <= qry_i, sc, -jnp.inf)
        bits = pltpu.bitcast(sc, jnp.int32)
        key = jnp.where(bits < 0, bits ^ jnp.int32(0x7FFFFFFF), bits)
        key_ref[pl.ds(k0, tk), :] = key
        half_ref[pl.ds(k0, tk), :] = (key >> 16).astype(jnp.int16)
        return carry

    lax.fori_loop(0, last, functools.partial(score_block, masked=False), 0)
    score_block(last, 0, True)

    one16, zero16 = jnp.ones((), jnp.int16), jnp.zeros((), jnp.int16)
    low16 = np.int16(-2 ** 15)

    def count16(limit, strict):
        def body(kb, acc):
            k0 = pl.multiple_of(kb * tk, tk)
            for c in range(tk // 128):
                blk = half_ref[pl.ds(k0 + 128 * c, 128), :].reshape(8, 16, tq)
                hit = jnp.where(blk > limit if strict else blk >= limit, one16, zero16)
                parts = [hit[j] for j in range(8)]
                while len(parts) > 1:
                    parts = [a + b for a, b in zip(parts[0::2], parts[1::2])]
                acc = acc + parts[0]
            return acc
        acc = lax.fori_loop(0, nkb, body, jnp.zeros((16, tq), jnp.int16))
        return jnp.broadcast_to(jnp.sum(acc.astype(jnp.int32), axis=0, keepdims=True), (16, tq))

    def search16(need):
        t = jnp.full((16, tq), -2 ** 15, jnp.int32)
        for bit in range(15, -1, -1):
            trial = t + 2 ** bit
            t = jnp.where(count16(trial.astype(jnp.int16), False) >= need, trial, t)
        return t

    t_hi = search16(n_sel)
    t_hi16 = t_hi.astype(jnp.int16)
    need_lo = n_sel - count16(t_hi16, True)

    def low_block(kb, carry):
        k0 = pl.multiple_of(kb * tk, tk)
        lo = ((key_ref[pl.ds(k0, tk), :] & 0xFFFF) - 2 ** 15).astype(jnp.int16).reshape(tk // 16, 16, tq)
        hi = half_ref[pl.ds(k0, tk), :].reshape(tk // 16, 16, tq)
        half_ref[pl.ds(k0, tk), :] = jnp.where(hi == t_hi16, lo, low16).reshape(tk, tq)
        return carry

    lax.fori_loop(0, nkb, low_block, 0)
    t_lo = search16(need_lo)
    thr = ((t_hi << 16) | (t_lo + 2 ** 15))[0:8, :]

    need = (need_lo - count16(t_lo.astype(jnp.int16), True))[0:8, :].astype(F32)
    ur = lax.broadcasted_iota(jnp.int32, (LANES, LANES), 0)
    uc = lax.broadcasted_iota(jnp.int32, (LANES, LANES), 1)
    earlier = (uc < ur).astype(BF16)
    ones8 = jnp.ones((8, LANES), BF16)
    key_s = lax.broadcasted_iota(jnp.int32, (LANES, tq), 0)
    qry_s = q0 + lax.broadcasted_iota(jnp.int32, (LANES, tq), 1)

    def select_block(kb, seen, masked):
        k0 = pl.multiple_of(kb * tk, tk)
        for g in range(tk // LANES):
            r0 = g * LANES
            blk = key_ref[pl.ds(k0 + r0, LANES), :].reshape(LANES // 8, 8, tq)
            eq = blk == thr[None]
            eqb = jnp.where(eq, 1.0, 0.0).reshape(LANES, tq).astype(BF16)
            rank = _dot(earlier, eqb).reshape(LANES // 8, 8, tq) + seen[None]
            tie_bias = jnp.where(rank < need[None], 0.0, NEG_BIG)
            bias = jnp.where(blk > thr[None], 0.0, jnp.where(eq, tie_bias, NEG_BIG)).reshape(LANES, tq)
            if masked:
                bias = jnp.where(k0 + r0 + key_s <= qry_s, bias, NEG_BIG)
            bias_ref[pl.ds(k0 + r0, LANES), :] = bias
            seen = seen + _dot(ones8, eqb)
        return seen

    seen = lax.fori_loop(0, last, functools.partial(select_block, masked=False), jnp.zeros((8, tq), F32))
    select_block(last, seen, True)

    qdt = qdt_ref[0]
    zq = jnp.zeros((LANES - DSA_HEAD_DIM, tq), BF16)
    qz = jnp.concatenate([jnp.concatenate([qdt[hd * DSA_HEAD_DIM:(hd + 1) * DSA_HEAD_DIM, :], zq], axis=0)
                          for hd in range(DSA_HEADS)], axis=1)
    wide = DSA_HEADS * tq

    def scores(kb, dst_ref):
        k0 = pl.multiple_of(kb * tk, tk)
        dst_ref[...] = _dot(dkv_ref[pl.ds(k0, tk), :], qz)

    def absorb(src_ref, kb, carry):
        m_i, l_i, acc = carry
        k0 = pl.multiple_of(kb * tk, tk)
        bias = bias_ref[pl.ds(k0, tk), :]
        s = src_ref[...] + jnp.concatenate([bias] * DSA_HEADS, axis=1)
        m_new = jnp.maximum(m_i, jnp.max(s, axis=0, keepdims=True))
        p = jnp.exp2(s - m_new)
        alpha = jnp.exp2(m_i - m_new)
        l_new = alpha * l_i + jnp.sum(p, axis=0, keepdims=True)
        pv = _dot(dkvt_ref[0, DSA_HEAD_DIM:, pl.ds(k0, tk)], p.astype(BF16))
        return m_new, l_new, alpha * acc + pv

    def pair(j, carry):
        kb = 2 * j
        scores(kb + 1, sb_ref)
        carry = absorb(sa_ref, kb, carry)
        scores(kb + 2, sa_ref)
        return absorb(sb_ref, kb + 1, carry)

    init = (jnp.full((1, wide), NEG_BIG, F32), jnp.zeros((1, wide), F32), jnp.zeros((DSA_HEAD_DIM, wide), F32))
    last = nkb - 1
    scores(0, sa_ref)
    carry = lax.fori_loop(0, last // 2, pair, init)

    def tail_odd(carry):
        scores(last, sb_ref)
        return absorb(sb_ref, last, absorb(sa_ref, last - 1, carry))

    def tail_even(carry):
        return absorb(sa_ref, last, carry)

    _, l_f, acc = lax.cond(last % 2 == 1, tail_odd, tail_even, carry)
    o_all = acc / l_f
    o_ref[...] = jnp.concatenate([o_all[:, hd * tq:(hd + 1) * tq] for hd in range(DSA_HEADS)],
                                 axis=0).T.astype(BF16)


def _dsa(qdt, iqt, iwt, dkv, dkvt, ikw, batch, seq):
    tq, tk = DSA_TQ, DSA_TK
    nq = seq // tq
    n_sel = min(DSA_TOPK, seq // 4)
    return pl.pallas_call(
        functools.partial(_dsa_kernel, tq=tq, tk=tk, n_sel=n_sel),
        out_shape=jax.ShapeDtypeStruct((batch * seq, 256), BF16),
        grid=(batch, nq),
        in_specs=[pl.BlockSpec((1, 256, tq), lambda b, i: (b, 0, i)),
                  pl.BlockSpec((1, 128, tq), lambda b, i: (b, 0, i)),
                  pl.BlockSpec((1, 8, tq), lambda b, i: (b, 0, i)),
                  pl.BlockSpec((seq, 128), lambda b, i: (b, 0)),
                  pl.BlockSpec((1, 128, seq), lambda b, i: (b, 0, 0)),
                  pl.BlockSpec((seq, 128), lambda b, i: (b, 0))],
        out_specs=pl.BlockSpec((tq, 256), lambda b, i: (b * nq + i, 0)),
        scratch_shapes=[pltpu.VMEM((seq, tq), jnp.int32), pltpu.VMEM((seq, tq), F32),
                        pltpu.VMEM((seq, tq), jnp.int16),
                        pltpu.VMEM((tk, DSA_HEADS * tq), F32), pltpu.VMEM((tk, DSA_HEADS * tq), F32)],
        compiler_params=_params(("parallel", "parallel")),
        name="dsa",
    )(qdt, iqt, iwt, dkv, dkvt, ikw)


def _mem_kv_kernel(mem_ref, g_ref, w_ref, kt_ref, v_ref):
    mn = _rms(mem_ref[0], g_ref[...]).astype(BF16)
    kv = _dot(mn, w_ref[...])
    kt_ref[0] = kv[:, :256].T.astype(BF16)
    v_ref[0] = kv[:, 256:].astype(BF16)


def _mem_kv(mem, gain, w_kv):
    b, m, d = mem.shape
    return pl.pallas_call(
        _mem_kv_kernel,
        out_shape=(jax.ShapeDtypeStruct((b, 256, m), BF16), jax.ShapeDtypeStruct((b, m, 256), BF16)),
        grid=(b,),
        in_specs=[pl.BlockSpec((1, m, d), lambda i: (i, 0, 0)), _const_spec((1, d)), _const_spec((d, 512))],
        out_specs=(pl.BlockSpec((1, 256, m), lambda i: (i, 0, 0)), pl.BlockSpec((1, m, 256), lambda i: (i, 0, 0))),
        compiler_params=_params(("parallel",)),
        name="mem_kv",
    )(mem, gain, w_kv)


def _merge_kernel(x_ref, ya_ref, yb_ref, yc_ref, yd_ref, gmix_ref, wg_ref, wbr_ref, wout_ref,
                  gq_ref, wq_ref, mkt_ref, mv_ref, wo_ref, gffn_ref, wr_ref, br_ref,
                  x2_ref, hp_ref, rt_ref, cnt_ref, run_ref, *, tm):
    x = x_ref[...]
    d = x.shape[-1]
    h = _rms(x, gmix_ref[...]).astype(BF16)
    merged = jnp.zeros((tm, d), F32)
    for n, y_ref in enumerate((ya_ref, yb_ref, yc_ref, yd_ref)):
        wn = wg_ref[:, n * d:n * d + d + LANES][:, GATE_SHIFT:GATE_SHIFT + d]
        gate = _sigmoid(_dot(h, wn))
        merged = merged + gate * _dot(y_ref[...], wbr_ref[n])
    x1 = x + _dot(merged.astype(BF16), wout_ref[...])
    h2 = _rms(x1, gq_ref[...]).astype(BF16)
    q = (_dot(h2, wq_ref[...]) * (MEM_HEAD_DIM ** -0.5)).astype(BF16)
    lane_head = lax.broadcasted_iota(jnp.int32, (tm, 256), 1) // MEM_HEAD_DIM
    mv = mv_ref[0]
    o = jnp.zeros((tm, 256), F32)
    for hd in range(MEM_HEADS):
        s = _dot(q[:, hd * MEM_HEAD_DIM:(hd + 1) * MEM_HEAD_DIM], mkt_ref[0, hd * MEM_HEAD_DIM:(hd + 1) * MEM_HEAD_DIM, :])
        p = jnp.exp(s - jnp.max(s, axis=-1, keepdims=True))
        p = p / jnp.sum(p, axis=-1, keepdims=True)
        o = o + jnp.where(lane_head == hd, _dot(p.astype(BF16), mv), 0.0)
    x2 = x1 + _dot(o.astype(BF16), wo_ref[...])
    x2_ref[...] = x2
    h3f = _rms(x2, gffn_ref[...])
    hp_ref[...] = _pack_bf16_pairs(h3f)
    h3 = h3f.astype(BF16)
    logits = _dot(h3, wr_ref[...]) + br_ref[...]
    lane = lax.broadcasted_iota(jnp.int32, (tm, LANES), 1)
    gl = jnp.where(lane < MOE_GROUPS, logits, -jnp.inf)
    gmax = jnp.max(gl, axis=-1, keepdims=True)
    gsel = jnp.min(jnp.where(gl == gmax, lane, LANES), axis=-1, keepdims=True)
    pg_sel = 1.0 / jnp.sum(jnp.exp(gl - gmax), axis=-1, keepdims=True)
    in_group = (lane - ROUTER_EXPERT_LANE) // MOE_EXPERTS_PER_GROUP == gsel
    el = jnp.where(in_group, logits, -jnp.inf)
    m1 = jnp.max(el, axis=-1, keepdims=True)
    i1 = jnp.min(jnp.where(el == m1, lane, LANES), axis=-1, keepdims=True)
    el2 = jnp.where(lane == i1, -jnp.inf, el)
    m2 = jnp.max(el2, axis=-1, keepdims=True)
    i2 = jnp.min(jnp.where(el2 == m2, lane, LANES), axis=-1, keepdims=True)
    e21 = jnp.exp(m2 - m1)
    c1 = pg_sel / (1.0 + e21)
    @pl.when(pl.program_id(0) == 0)
    def _():
        run_ref[...] = jnp.zeros_like(run_ref)

    oh1 = jnp.where(lane == i1, 1.0, 0.0)
    oh2 = jnp.where(lane == i2, 1.0, 0.0)
    both = oh1 + oh2
    tr = lax.broadcasted_iota(jnp.int32, (LANES, LANES), 0)
    tc = lax.broadcasted_iota(jnp.int32, (LANES, LANES), 1)
    earlier = jnp.where(tc < tr, 1.0, 0.0).astype(BF16)
    seen = run_ref[0:1, :]
    r1, r2 = [], []
    for g0 in range(0, tm, LANES):
        grp = both[g0:g0 + LANES, :]
        before = _dot(earlier, grp.astype(BF16)) + seen
        r1.append(jnp.sum(oh1[g0:g0 + LANES, :] * before, axis=-1, keepdims=True))
        r2.append(jnp.sum(oh2[g0:g0 + LANES, :] * before, axis=-1, keepdims=True))
        seen = seen + jnp.sum(grp, axis=0, keepdims=True)
    r1, r2 = jnp.concatenate(r1, axis=0), jnp.concatenate(r2, axis=0)
    total = jnp.broadcast_to(seen, run_ref.shape)
    run_ref[...] = total
    cnt_ref[...] = total
    ids = (jnp.where(lane == 0, i1, i2) - ROUTER_EXPERT_LANE).astype(F32)
    rt_ref[...] = jnp.where(lane < 2, ids, jnp.where(lane == 2, c1, jnp.where(lane == 3, c1 * e21,
                            jnp.where(lane == 4, r1, jnp.where(lane == 5, r2, 0.0)))))


def _merge(x, ys, gmix, wg, wbr, wout, gq, wq, mkt, mv, wo, gffn, wr, br, batch, seq):
    n, d = x.shape
    tm = MERGE_TM
    spt = seq // tm
    m = mv.shape[1]
    tok = lambda w: pl.BlockSpec((tm, w), lambda i: (i, 0))
    return pl.pallas_call(
        functools.partial(_merge_kernel, tm=tm),
        out_shape=(jax.ShapeDtypeStruct((n, d), F32), jax.ShapeDtypeStruct((n, d // 2), F32),
                   jax.ShapeDtypeStruct((n, LANES), F32), jax.ShapeDtypeStruct((8, LANES), F32)),
        grid=(n // tm,),
        in_specs=[tok(d), tok(256), tok(256), tok(256), tok(256),
                  _const_spec((1, d)), _const_spec((d, N_BRANCH * d + LANES)), _const_spec((N_BRANCH, 256, d)),
                  _const_spec((d, d)), _const_spec((1, d)), _const_spec((d, 256)),
                  pl.BlockSpec((1, 256, m), lambda i: (i // spt, 0, 0)),
                  pl.BlockSpec((1, m, 256), lambda i: (i // spt, 0, 0)),
                  _const_spec((256, d)), _const_spec((1, d)), _const_spec((d, LANES)), _const_spec((1, LANES))],
        out_specs=(tok(d), tok(d // 2), tok(LANES), pl.BlockSpec((8, LANES), lambda i: (0, 0))),
        scratch_shapes=[pltpu.VMEM((8, LANES), F32)],
        compiler_params=_params(("arbitrary",)),
        name="merge_mem_router",
    )(x, *ys, gmix, wg, wbr, wout, gq, wq, mkt, mv, wo, gffn, wr, br)


def _sc_gather_rows(table, idx):
    _, width = table.shape
    total = idx.shape[0]
    chunk, nbuf = SC_GATHER_CHUNK, SC_GATHER_BUFS
    workers = SC_CORES * SC_SUBCORES
    per_w = total // workers
    nch = per_w // chunk
    assert total % (workers * chunk * nbuf) == 0
    mesh = plsc.VectorSubcoreMesh(core_axis_name="c", subcore_axis_name="s")

    @functools.partial(
        pl.kernel, mesh=mesh, out_type=jax.ShapeDtypeStruct((total, width), table.dtype),
        scratch_types=[pltpu.VMEM((nch, chunk), jnp.int32), pltpu.VMEM((nbuf, chunk, width), table.dtype),
                       pltpu.SemaphoreType.DMA((nbuf,)), pltpu.SemaphoreType.DMA((nbuf,))])
    def gather_kernel(table_hbm, idx_hbm, out_hbm, idx_v, rows_v, gsem, wsem):
        wid = lax.axis_index("s") * SC_CORES + lax.axis_index("c")
        pltpu.sync_copy(idx_hbm.at[wid], idx_v)

        def gather(j, slot):
            return pltpu.make_async_copy(table_hbm.at[idx_v.at[j]], rows_v.at[slot], gsem.at[slot])

        def write(j, slot):
            off = pl.multiple_of(wid * per_w + j * chunk, chunk)
            return pltpu.make_async_copy(rows_v.at[slot], out_hbm.at[pl.ds(off, chunk)], wsem.at[slot])

        for slot in range(nbuf):
            gather(slot, slot).start()

        @pl.loop(0, nch // nbuf)
        def _(g):
            for slot in range(nbuf):
                j = g * nbuf + slot
                gather(j, slot).wait()
                write(j, slot).start()
                write(j, slot).wait()

                @pl.when(j + nbuf < nch)
                def _():
                    gather(j + nbuf, slot).start()

    return gather_kernel(table, idx.reshape(workers, nch, chunk))


def _sc_scatter_rows(table, dest2, total):
    n, width = table.shape
    chunk, nbuf = SC_GATHER_CHUNK, SC_GATHER_BUFS
    workers = SC_CORES * SC_SUBCORES
    per_w = n // workers
    nch = per_w // chunk
    assert n % (workers * chunk * nbuf) == 0
    mesh = plsc.VectorSubcoreMesh(core_axis_name="c", subcore_axis_name="s")

    @functools.partial(
        pl.kernel, mesh=mesh, out_type=jax.ShapeDtypeStruct((total, width), table.dtype),
        scratch_types=[pltpu.VMEM((2, nch, chunk), jnp.int32), pltpu.VMEM((nbuf, chunk, width), table.dtype),
                       pltpu.SemaphoreType.DMA((nbuf,)), pltpu.SemaphoreType.DMA((nbuf,))])
    def scatter_kernel(table_hbm, idx_hbm, out_hbm, idx_v, rows_v, rsem, wsem):
        wid = lax.axis_index("s") * SC_CORES + lax.axis_index("c")
        pltpu.sync_copy(idx_hbm.at[wid], idx_v)

        def read(j, slot):
            off = pl.multiple_of(wid * per_w + j * chunk, chunk)
            return pltpu.make_async_copy(table_hbm.at[pl.ds(off, chunk)], rows_v.at[slot], rsem.at[slot])

        def write(j, slot, k):
            return pltpu.make_async_copy(rows_v.at[slot], out_hbm.at[idx_v.at[k, j]], wsem.at[slot])

        for slot in range(nbuf):
            read(slot, slot).start()

        @pl.loop(0, nch // nbuf)
        def _(g):
            for slot in range(nbuf):
                j = g * nbuf + slot
                read(j, slot).wait()
                write(j, slot, 0).start()
                write(j, slot, 1).start()
                write(j, slot, 0).wait()
                write(j, slot, 1).wait()

                @pl.when(j + nbuf < nch)
                def _():
                    read(j + nbuf, slot).start()

    idx = dest2.reshape(2, workers, nch, chunk).transpose(1, 0, 2, 3)
    return scatter_kernel(table, idx)


def _dispatch_plan(rt, cnt, n):
    ne, blk = MOE_N_EXPERTS, MOE_BLOCK
    n_blocks = (2 * n) // blk + ne
    experts = jnp.arange(ne, dtype=jnp.int32)
    counts = cnt[0, ROUTER_EXPERT_LANE:ROUTER_EXPERT_LANE + ne].astype(jnp.int32)
    padded = (counts + blk - 1) // blk * blk
    pend = jnp.cumsum(padded)
    pstart = pend - padded
    ids = rt[:, 0:2].astype(jnp.int32)
    pos = rt[:, 4:6].astype(jnp.int32)
    first_row = jnp.sum(jnp.where(ids[:, :, None] == experts[None, None, :], pstart[None, None, :], 0), axis=-1)
    dest2 = (first_row + pos).T
    b0 = jnp.arange(n_blocks, dtype=jnp.int32) * blk
    block_e = jnp.minimum(jnp.sum((pend[None, :] <= b0[:, None]).astype(jnp.int32), axis=1), ne - 1)
    n_valid = jnp.clip(counts[block_e] - (b0 - pstart[block_e]), 0, blk).astype(jnp.int32)
    return dest2, block_e, n_valid


def _expert_block_kernel(be_ref, nv_ref, xs_ref, wg_ref, wu_ref, wd_ref, o_ref, wgb_ref, wub_ref, wdb_ref):
    b = pl.program_id(0)
    valid = nv_ref[b]

    @pl.when((b == 0) | (be_ref[b] != be_ref[jnp.maximum(b - 1, 0)]))
    def _():
        wgb_ref[...] = wg_ref[0].astype(BF16)
        wub_ref[...] = wu_ref[0].astype(BF16)
        wdb_ref[...] = wd_ref[0].astype(BF16)

    @pl.when(valid > 0)
    def _():
        row = lax.broadcasted_iota(jnp.int32, xs_ref.shape, 0)
        words = jnp.where(row < valid, xs_ref[...], 0.0)
        h = _unpack_bf16_pairs(words).astype(BF16)
        gt = _dot(h, wgb_ref[...])
        hid = gt * _sigmoid(gt) * _dot(h, wub_ref[...])
        o_ref[...] = _pack_bf16_pairs(_dot(hid.astype(BF16), wdb_ref[...]))

    @pl.when(valid == 0)
    def _():
        o_ref[...] = jnp.zeros_like(o_ref)


def _expert_blocks(xs, wg, wu, wd, layer, block_e, n_used):
    p_rows, half = xs.shape
    d, hid = wg.shape[-2:]
    blk = MOE_BLOCK
    grid_spec = pltpu.PrefetchScalarGridSpec(
        num_scalar_prefetch=2, grid=(p_rows // blk,),
        in_specs=[pl.BlockSpec((blk, half), lambda b, be, nu: (b, 0)),
                  pl.BlockSpec((None, 1, d, hid), lambda b, be, nu: (layer, be[b], 0, 0)),
                  pl.BlockSpec((None, 1, d, hid), lambda b, be, nu: (layer, be[b], 0, 0)),
                  pl.BlockSpec((None, 1, hid, d), lambda b, be, nu: (layer, be[b], 0, 0))],
        out_specs=pl.BlockSpec((blk, half), lambda b, be, nu: (b, 0)),
        scratch_shapes=[pltpu.VMEM((d, hid), BF16), pltpu.VMEM((d, hid), BF16), pltpu.VMEM((hid, d), BF16)])
    return pl.pallas_call(
        _expert_block_kernel, out_shape=jax.ShapeDtypeStruct((p_rows, half), F32), grid_spec=grid_spec,
        compiler_params=_params(("arbitrary",)),
        name="moe_expert_blocks",
    )(block_e, n_used, xs, wg, wu, wd)


def _combine_kernel(x_ref, y1_ref, y2_ref, rt_ref, gfin_ref, o_ref, *, final_norm):
    out = (x_ref[...] + rt_ref[:, 2:3] * _unpack_bf16_pairs(y1_ref[...])
           + rt_ref[:, 3:4] * _unpack_bf16_pairs(y2_ref[...]))
    o_ref[...] = _rms(out, gfin_ref[...]) if final_norm else out


def _combine(x2, y_halves, rt, gfin, final_norm):
    n, d = x2.shape
    tm = COMBINE_TM
    nt = n // tm
    return pl.pallas_call(
        functools.partial(_combine_kernel, final_norm=final_norm),
        out_shape=jax.ShapeDtypeStruct((n, d), F32),
        grid=(nt,),
        in_specs=[pl.BlockSpec((tm, d), lambda i: (i, 0)),
                  pl.BlockSpec((tm, d // 2), lambda i: (i, 0)),
                  pl.BlockSpec((tm, d // 2), lambda i: (i + nt, 0)),
                  pl.BlockSpec((tm, LANES), lambda i: (i, 0)), _const_spec((1, d))],
        out_specs=pl.BlockSpec((tm, d), lambda i: (i, 0)),
        compiler_params=_params(("parallel",)), name="moe_combine",
    )(x2, y_halves, y_halves, rt, gfin)


def _moe(x2, hp, rt, cnt, wg, wu, wd, layer, gfin, final_norm):
    n = x2.shape[0]
    dest2, block_e, n_valid = _dispatch_plan(rt, cnt, n)
    xs = _sc_scatter_rows(hp, dest2, block_e.shape[0] * MOE_BLOCK)
    yb = _expert_blocks(xs, wg, wu, wd, layer, block_e, n_valid)
    y_halves = _sc_gather_rows(yb, dest2.reshape(2 * n))
    return _combine(x2, y_halves, rt, gfin, final_norm)


def kernel(x, mem, positions, norm_mix, w_in, diff_lambda, hgrn_lb_logits, spatial_w, spatial_b, w_branch, w_out,
           norm_mem_q, norm_mem_kv, w_mem_q, w_mem_kv, w_mem_o, norm_ffn, w_router_group, b_router_group,
           w_router_expert, b_router_expert, w_exp_gate, w_exp_up, w_exp_down, norm_final):
    batch, seq, d = x.shape
    depth = w_in.shape[0]
    n = batch * seq
    xf = x.reshape(n, d)
    tabs = _rope_tables(positions)
    w_all = _w_in_bf16(w_in)
    row = lambda v: v.reshape(1, -1).astype(F32)
    for l in range(depth):
        lam_init = 0.8 - 0.6 * math.exp(-0.3 * l)
        g0 = C_GATE - GATE_SHIFT
        w_gate = lax.slice(w_all, (l, 0, g0), (l + 1, d, g0 + N_BRANCH * d + LANES)).reshape(d, -1)
        sw = spatial_w[l].reshape(SGU_GROUPS * SGU_CHUNK, SGU_CHUNK)
        sb = jnp.repeat(spatial_b[l].T, SGU_GROUP_DIM, axis=1)
        qat, ka, vat, hb, y_c, qdt, iqt, dkv, dkvt, ikw, iwt = _projection(
            xf, row(norm_mix[l]), w_all, l, tabs, sw, sb, batch, seq)
        y_a = _diff_attention(diff_lambda[l], qat, ka, vat, lam_init, batch, seq)
        y_b = _hgrn(hgrn_lb_logits, hb, l, batch, seq)
        y_d = _dsa(qdt, iqt, iwt, dkv, dkvt, ikw, batch, seq)
        mkt, mv = _mem_kv(mem, row(norm_mem_kv[l]), w_mem_kv[l].astype(BF16))
        e0, e1 = ROUTER_EXPERT_LANE, ROUTER_EXPERT_LANE + MOE_N_EXPERTS
        wr = jnp.zeros((d, LANES), F32)
        wr = wr.at[:, :MOE_GROUPS].set(w_router_group[l]).at[:, e0:e1].set(w_router_expert[l]).astype(BF16)
        br = jnp.zeros((1, LANES), F32)
        br = br.at[0, :MOE_GROUPS].set(b_router_group[l]).at[0, e0:e1].set(b_router_expert[l])
        x2, hp, rt, cnt = _merge(xf, (y_a, y_b, y_c, y_d), row(norm_mix[l]), w_gate, w_branch[l].astype(BF16),
                                 w_out[l].astype(BF16), row(norm_mem_q[l]), w_mem_q[l].astype(BF16), mkt, mv,
                                 w_mem_o[l].astype(BF16), row(norm_ffn[l]), wr, br, batch, seq)
        xf = _moe(x2, hp, rt, cnt, w_exp_gate, w_exp_up, w_exp_down, l,
                  row(norm_final), final_norm=(l == depth - 1))
    return xf.reshape(batch, seq, d)
```
